```python
import jax, jax.numpy as jnp
from jax import lax
import numpy as np

D_MODEL = 1024
BATCH = 8
SEQ = 8192
DEPTH = 2

D_MIX = D_MODEL
HEAD_DIM = 64
N_ATTN_HEADS = 8
ATTN_DIM = N_ATTN_HEADS * HEAD_DIM
N_CONV_GROUPS = 8
CONV_DIM = D_MIX - ATTN_DIM
CONV_WIDTH = 3
IN_COLS = 3 * ATTN_DIM + 3 * CONV_DIM
D_FF = ((8 * D_MODEL // 3 + 255) // 256) * 256
Q_BLOCK = 128
EPS = 1e-6

kernel_name = "hymba_stickbreak_shortconv_swiglu"


def _rms_norm(x, gain):
    x32 = x.astype(jnp.float32)
    y = x32 * lax.rsqrt(jnp.mean(x32 * x32, axis=-1, keepdims=True) + EPS)
    return (y * gain.astype(jnp.float32)).astype(x.dtype)


def _stick_breaking_attention(q, k, v):
    b, h, s, dh = q.shape
    nb = s // Q_BLOCK
    scale = dh ** -0.5
    k32 = k.astype(jnp.float32)
    v32 = v.astype(jnp.float32)
    q_blocks = q.reshape(b, h, nb, Q_BLOCK, dh).transpose(2, 0, 1, 3, 4)
    starts = jnp.arange(nb, dtype=jnp.int32) * Q_BLOCK
    key_pos = jnp.arange(s, dtype=jnp.int32)

    def one_block(args):
        q_blk, start = args
        z = jnp.einsum('bhqd,bhkd->bhqk', q_blk.astype(jnp.float32), k32) * scale
        q_pos = start + jnp.arange(Q_BLOCK, dtype=jnp.int32)
        causal = key_pos[None, :] < q_pos[:, None]
        sp = jax.nn.softplus(z)
        neg_log_rem = jnp.where(causal, sp, 0.0)
        after = lax.cumsum(neg_log_rem, axis=3, reverse=True) - neg_log_rem
        log_a = (z - sp) - after
        a = jnp.where(causal, jnp.exp(log_a), 0.0)
        return jnp.einsum('bhqk,bhkd->bhqd', a, v32)

    out = lax.map(one_block, (q_blocks, starts))
    return out.transpose(1, 2, 0, 3, 4).reshape(b, h, s, dh).astype(q.dtype)


def _short_conv_mixer(b_gate, c_gate, u, conv_w):
    h = c_gate * u
    rhs = conv_w.astype(h.dtype)[:, None, :]
    y = lax.conv_general_dilated(h, rhs, window_strides=(1,), padding=[(CONV_WIDTH - 1, 0)],
                                 dimension_numbers=('NWC', 'WIO', 'NWC'),
                                 feature_group_count=CONV_DIM)
    return b_gate * y


def _fwd_setup_inputs(seed: int = 0) -> dict:
    key = jax.random.key(seed)
    ks = jax.random.split(key, 12)
    f32 = jnp.float32

    def gain(k, shape):
        return jnp.ones(shape, f32) + 0.01 * jax.random.normal(k, shape, f32)

    x = jax.random.normal(ks[0], (BATCH, SEQ, D_MODEL), f32)
    norm_mix = gain(ks[1], (DEPTH, D_MODEL))
    w_in = jax.random.normal(ks[2], (DEPTH, D_MODEL, IN_COLS), f32) * D_MODEL ** -0.5
    q_norm = gain(ks[3], (DEPTH, HEAD_DIM))
    k_norm = gain(ks[4], (DEPTH, HEAD_DIM))
    conv_w = jax.random.normal(ks[5], (DEPTH, CONV_WIDTH, CONV_DIM), f32) * CONV_WIDTH ** -0.5
    w_out = jax.random.normal(ks[6], (DEPTH, D_MIX, D_MODEL), f32) * D_MIX ** -0.5
    norm_ffn = gain(ks[7], (DEPTH, D_MODEL))
    w_gate = jax.random.normal(ks[8], (DEPTH, D_MODEL, D_FF), f32) * D_MODEL ** -0.5
    w_up = jax.random.normal(ks[9], (DEPTH, D_MODEL, D_FF), f32) * D_MODEL ** -0.5
    w_down = jax.random.normal(ks[10], (DEPTH, D_FF, D_MODEL), f32) * D_FF ** -0.5
    return {"x": x, "norm_mix": norm_mix, "w_in": w_in, "q_norm": q_norm, "k_norm": k_norm,
            "conv_w": conv_w, "w_out": w_out, "norm_ffn": norm_ffn, "w_gate": w_gate,
            "w_up": w_up, "w_down": w_down}


def _fwd_reference(x, norm_mix, w_in, q_norm, k_norm, conv_w, w_out, norm_ffn, w_gate, w_up, w_down):
    b, s, _ = x.shape
    splits = [ATTN_DIM, 2 * ATTN_DIM, 3 * ATTN_DIM,
              3 * ATTN_DIM + CONV_DIM, 3 * ATTN_DIM + 2 * CONV_DIM]
    for l in range(DEPTH):
        h = _rms_norm(x, norm_mix[l])
        proj = h @ w_in[l].astype(h.dtype)
        q, k, v, cb, cc, cu = jnp.split(proj, splits, axis=-1)
        q = _rms_norm(q.reshape(b, s, N_ATTN_HEADS, HEAD_DIM), q_norm[l])
        k = _rms_norm(k.reshape(b, s, N_ATTN_HEADS, HEAD_DIM), k_norm[l])
        v = v.reshape(b, s, N_ATTN_HEADS, HEAD_DIM)
        attn = _stick_breaking_attention(q.transpose(0, 2, 1, 3), k.transpose(0, 2, 1, 3),
                                         v.transpose(0, 2, 1, 3))
        attn = attn.transpose(0, 2, 1, 3).reshape(b, s, ATTN_DIM)
        conv = _short_conv_mixer(cb, cc, cu, conv_w[l])
        mix = jnp.concatenate([attn, conv], axis=-1)
        x = x + mix @ w_out[l].astype(mix.dtype)
        h = _rms_norm(x, norm_ffn[l])
        g = h @ w_gate[l].astype(h.dtype)
        u = h @ w_up[l].astype(h.dtype)
        x = x + (jax.nn.silu(g) * u) @ w_down[l].astype(h.dtype)
    return x


import jax as _jax
import jax.numpy as _jnp

TWIN_FORMAT = 'train_step'
FWD_PARAMS = ['x', 'norm_mix', 'w_in', 'q_norm', 'k_norm', 'conv_w', 'w_out', 'norm_ffn', 'w_gate', 'w_up', 'w_down']
TWIN_WEIGHTS = ['norm_mix', 'w_in', 'q_norm', 'k_norm', 'conv_w', 'w_out', 'norm_ffn', 'w_gate', 'w_up', 'w_down']
TWIN_DIFF_INPUT = 'x'
TWIN_INPUTS = ['x', 'norm_mix', 'w_in', 'q_norm', 'k_norm', 'conv_w', 'w_out', 'norm_ffn', 'w_gate', 'w_up', 'w_down', 'loss_target', 'm_norm_mix', 'm_w_in', 'm_q_norm', 'm_k_norm', 'm_conv_w', 'm_w_out', 'm_norm_ffn', 'm_w_gate', 'm_w_up', 'm_w_down', 'v_norm_mix', 'v_w_in', 'v_q_norm', 'v_k_norm', 'v_conv_w', 'v_w_out', 'v_norm_ffn', 'v_w_gate', 'v_w_up', 'v_w_down']
TWIN_OUTPUTS = ['loss', 'grad_x', 'grad_norm_mix', 'grad_w_in', 'grad_q_norm', 'grad_k_norm', 'grad_conv_w', 'grad_w_out', 'grad_norm_ffn', 'grad_w_gate', 'grad_w_up', 'grad_w_down', 'delta_norm_mix', 'delta_w_in', 'delta_q_norm', 'delta_k_norm', 'delta_conv_w', 'delta_w_out', 'delta_norm_ffn', 'delta_w_gate', 'delta_w_up', 'delta_w_down', 'new_m_norm_mix', 'new_m_w_in', 'new_m_q_norm', 'new_m_k_norm', 'new_m_conv_w', 'new_m_w_out', 'new_m_norm_ffn', 'new_m_w_gate', 'new_m_w_up', 'new_m_w_down', 'new_v_norm_mix', 'new_v_w_in', 'new_v_q_norm', 'new_v_k_norm', 'new_v_conv_w', 'new_v_w_out', 'new_v_norm_ffn', 'new_v_w_gate', 'new_v_w_up', 'new_v_w_down']
TWIN_LEAF_KINDS = {'loss': 'loss', 'grad_x': 'grad_x', 'grad_norm_mix': 'grad_w', 'grad_w_in': 'grad_w', 'grad_q_norm': 'grad_w', 'grad_k_norm': 'grad_w', 'grad_conv_w': 'grad_w', 'grad_w_out': 'grad_w', 'grad_norm_ffn': 'grad_w', 'grad_w_gate': 'grad_w', 'grad_w_up': 'grad_w', 'grad_w_down': 'grad_w', 'delta_norm_mix': 'delta_w', 'delta_w_in': 'delta_w', 'delta_q_norm': 'delta_w', 'delta_k_norm': 'delta_w', 'delta_conv_w': 'delta_w', 'delta_w_out': 'delta_w', 'delta_norm_ffn': 'delta_w', 'delta_w_gate': 'delta_w', 'delta_w_up': 'delta_w', 'delta_w_down': 'delta_w', 'new_m_norm_mix': 'new_m', 'new_m_w_in': 'new_m', 'new_m_q_norm': 'new_m', 'new_m_k_norm': 'new_m', 'new_m_conv_w': 'new_m', 'new_m_w_out': 'new_m', 'new_m_norm_ffn': 'new_m', 'new_m_w_gate': 'new_m', 'new_m_w_up': 'new_m', 'new_m_w_down': 'new_m', 'new_v_norm_mix': 'new_v', 'new_v_w_in': 'new_v', 'new_v_q_norm': 'new_v', 'new_v_k_norm': 'new_v', 'new_v_conv_w': 'new_v', 'new_v_w_out': 'new_v', 'new_v_norm_ffn': 'new_v', 'new_v_w_gate': 'new_v', 'new_v_w_up': 'new_v', 'new_v_w_down': 'new_v'}


def _forward(args):
    return _fwd_reference(*[args[k] for k in FWD_PARAMS])


def _output_shape():
    def fwd():
        inp = _fwd_setup_inputs(0)
        return _fwd_reference(*[inp[k] for k in FWD_PARAMS])
    out = _jax.eval_shape(fwd)
    return out.shape, out.dtype

N_MICROBATCH = 1
ADAM_LR = 0.001
ADAM_B1 = 0.9
ADAM_B2 = 0.999
ADAM_EPS = 1e-08
ADAM_WD = 0.01
ADAM_STEP = 10
PER_EXAMPLE_BATCH_AXIS = {'x': 0, 'loss_target': 0}
SHARED_INPUTS = []
_WEIGHT_DTYPES = {'norm_mix': _jnp.float32, 'w_in': _jnp.float32, 'q_norm': _jnp.float32, 'k_norm': _jnp.float32, 'conv_w': _jnp.float32, 'w_out': _jnp.float32, 'norm_ffn': _jnp.float32, 'w_gate': _jnp.float32, 'w_up': _jnp.float32, 'w_down': _jnp.float32}
MOMENT_SCALE = {'norm_mix': 1.083079e+02, 'w_in': 1.528743e+00, 'q_norm': 3.035600e+01, 'k_norm': 3.035876e+01, 'conv_w': 3.627062e+01, 'w_out': 1.661131e+00, 'norm_ffn': 4.938567e+01, 'w_gate': 3.802314e-01, 'w_up': 4.505359e-01, 'w_down': 7.330935e-01}


def _to_microbatches(a, axis):
    t = _jnp.moveaxis(a, axis, 0)
    t = t.reshape((N_MICROBATCH, t.shape[0] // N_MICROBATCH) + t.shape[1:])
    return _jnp.moveaxis(t, 1, axis + 1)


def setup_inputs(seed: int = 0) -> dict:
    inp = _fwd_setup_inputs(seed)
    key = _jax.random.fold_in(_jax.random.key(seed), 7919)
    shape, _ = _output_shape()
    out = dict(inp)
    out["loss_target"] = _jax.random.normal(_jax.random.fold_in(key, 0), shape, _jnp.float32)
    for i, name in enumerate(TWIN_WEIGHTS):
        w = inp[name].astype(_jnp.float32)
        if MOMENT_SCALE is None:
            s = _jnp.sqrt(_jnp.mean(_jnp.square(w)) + 1e-30)
        else:
            s = MOMENT_SCALE[name]
        km, kv = _jax.random.split(_jax.random.fold_in(key, i + 1))
        out[name] = w
        out["m_" + name] = s * _jax.random.normal(km, w.shape, _jnp.float32)
        out["v_" + name] = (s * s) * _jax.random.uniform(kv, w.shape, _jnp.float32, 0.5, 1.5)
    if N_MICROBATCH > 1:
        for name, axis in PER_EXAMPLE_BATCH_AXIS.items():
            out[name] = _to_microbatches(out[name], axis)
    return {'x': out['x'], 'norm_mix': out['norm_mix'], 'w_in': out['w_in'], 'q_norm': out['q_norm'], 'k_norm': out['k_norm'], 'conv_w': out['conv_w'], 'w_out': out['w_out'], 'norm_ffn': out['norm_ffn'], 'w_gate': out['w_gate'], 'w_up': out['w_up'], 'w_down': out['w_down'], 'loss_target': out['loss_target'], 'm_norm_mix': out['m_norm_mix'], 'm_w_in': out['m_w_in'], 'm_q_norm': out['m_q_norm'], 'm_k_norm': out['m_k_norm'], 'm_conv_w': out['m_conv_w'], 'm_w_out': out['m_w_out'], 'm_norm_ffn': out['m_norm_ffn'], 'm_w_gate': out['m_w_gate'], 'm_w_up': out['m_w_up'], 'm_w_down': out['m_w_down'], 'v_norm_mix': out['v_norm_mix'], 'v_w_in': out['v_w_in'], 'v_q_norm': out['v_q_norm'], 'v_k_norm': out['v_k_norm'], 'v_conv_w': out['v_conv_w'], 'v_w_out': out['v_w_out'], 'v_norm_ffn': out['v_norm_ffn'], 'v_w_gate': out['v_w_gate'], 'v_w_up': out['v_w_up'], 'v_w_down': out['v_w_down']}


def _loss(weights, diff, rest, loss_target):
    with _jax.named_scope("forward"):
        args = {**rest, TWIN_DIFF_INPUT: diff, **{k: w.astype(_WEIGHT_DTYPES[k]) for k, w in weights.items()}}
        y = _forward(args)
    with _jax.named_scope("loss_head"):
        err = _jnp.square(y.astype(_jnp.float32) - loss_target)
        return 0.5 * _jnp.sum(_jnp.mean(err, axis=-1)) if err.ndim else 0.5 * err


def _adamw(w, g, m, v):
    m = ADAM_B1 * m + (1.0 - ADAM_B1) * g
    v = ADAM_B2 * v + (1.0 - ADAM_B2) * _jnp.square(g)
    m_hat = m / (1.0 - ADAM_B1 ** ADAM_STEP)
    v_hat = v / (1.0 - ADAM_B2 ** ADAM_STEP)
    delta = -ADAM_LR * (m_hat / (_jnp.sqrt(v_hat) + ADAM_EPS) + ADAM_WD * w)
    return delta, m, v


def reference(x, norm_mix, w_in, q_norm, k_norm, conv_w, w_out, norm_ffn, w_gate, w_up, w_down, loss_target, m_norm_mix, m_w_in, m_q_norm, m_k_norm, m_conv_w, m_w_out, m_norm_ffn, m_w_gate, m_w_up, m_w_down, v_norm_mix, v_w_in, v_q_norm, v_k_norm, v_conv_w, v_w_out, v_norm_ffn, v_w_gate, v_w_up, v_w_down):
    given = dict(x=x, norm_mix=norm_mix, w_in=w_in, q_norm=q_norm, k_norm=k_norm, conv_w=conv_w, w_out=w_out, norm_ffn=norm_ffn, w_gate=w_gate, w_up=w_up, w_down=w_down, loss_target=loss_target, m_norm_mix=m_norm_mix, m_w_in=m_w_in, m_q_norm=m_q_norm, m_k_norm=m_k_norm, m_conv_w=m_conv_w, m_w_out=m_w_out, m_norm_ffn=m_norm_ffn, m_w_gate=m_w_gate, m_w_up=m_w_up, m_w_down=m_w_down, v_norm_mix=v_norm_mix, v_w_in=v_w_in, v_q_norm=v_q_norm, v_k_norm=v_k_norm, v_conv_w=v_conv_w, v_w_out=v_w_out, v_norm_ffn=v_norm_ffn, v_w_gate=v_w_gate, v_w_up=v_w_up, v_w_down=v_w_down)
    weights = {n: given[n] for n in TWIN_WEIGHTS}
    shared = {n: given[n] for n in SHARED_INPUTS}
    per_example = {n: given[n] for n in ['x']}
    grad_fn = _jax.value_and_grad(_loss, argnums=(0, 1))

    def one_microbatch(ex, loss_target):
        ex = dict(ex)
        diff = ex.pop(TWIN_DIFF_INPUT)
        return grad_fn(weights, diff, {**shared, **ex}, loss_target)

    if N_MICROBATCH == 1:
        loss, (grad_w, grad_x) = one_microbatch(per_example, given["loss_target"])
    else:
        def body(carry, xs):
            loss_sum, grad_sum = carry
            l_k, (gw_k, gx_k) = one_microbatch(xs[0], xs[1])
            with _jax.named_scope("update"):
                return (loss_sum + l_k, _jax.tree.map(_jnp.add, grad_sum, gw_k)), gx_k

        init = (_jnp.zeros((), _jnp.float32), _jax.tree.map(_jnp.zeros_like, weights))
        (loss, grad_w), grad_x = _jax.lax.scan(body, init, (per_example, given["loss_target"]))
    with _jax.named_scope("update"):
        delta_w, new_m, new_v = {}, {}, {}
        for n in TWIN_WEIGHTS:
            delta_w[n], new_m[n], new_v[n] = _adamw(weights[n], grad_w[n], given["m_" + n], given["v_" + n])
    return (loss, grad_x, *[grad_w[n] for n in TWIN_WEIGHTS], *[delta_w[n] for n in TWIN_WEIGHTS],
            *[new_m[n] for n in TWIN_WEIGHTS], *[new_v[n] for n in TWIN_WEIGHTS])
```

```python
import functools

import jax
import jax.numpy as jnp
from jax import lax
from jax.experimental import pallas as pl
from jax.experimental.pallas import tpu as pltpu

F32 = jnp.float32
BF16 = jnp.bfloat16

EPS = 1e-6
HEAD_DIM = 64
LANES = 128
ATTN_DIM = 512
CONV_DIM = 512
N_CHIPS = 4
N_DEV = 8
Q_SCALE = HEAD_DIM ** -0.5
ATTN_TILE = 256
TOKEN_TILE = 512
VMEM_LIMIT = 56 * 1024 * 1024

ADAM_LR = 0.001
ADAM_B1 = 0.9
ADAM_B2 = 0.999
ADAM_EPS = 1e-08
ADAM_WD = 0.01
ADAM_STEP = 10

MESH_ID = pl.DeviceIdType.MESH
ANY = pl.BlockSpec(memory_space=pl.ANY)
VMEM_SPEC = pl.BlockSpec(memory_space=pltpu.VMEM)


def _params(n_axes):
    return pltpu.CompilerParams(dimension_semantics=("arbitrary",) * n_axes, vmem_limit_bytes=VMEM_LIMIT)


def _dot(a, b):
    return jnp.dot(a, b, preferred_element_type=F32)


def _dot_nt(a, b):
    return lax.dot_general(a, b, (((1,), (1,)), ((), ())), preferred_element_type=F32)


def _dot_tn(a, b):
    return lax.dot_general(a, b, (((0,), (0,)), ((), ())), preferred_element_type=F32)


def _split_bf16(v):
    hi = v.astype(BF16)
    lo = (v - hi.astype(F32)).astype(BF16)
    return hi, lo


def _softplus(z):
    return jnp.maximum(z, 0.0) + jnp.log(1.0 + jnp.exp(-jnp.abs(z)))


def _norm_matmul(x, gain, w_s, layer, name):
    s, d = x.shape
    n_blocks, _, _, n = w_s.shape
    tm = TOKEN_TILE

    def body(x_ref, g_ref, w_ref, h_ref, o_ref):
        @pl.when(pl.program_id(1) == 0)
        def _():
            xv = x_ref[...]
            r = lax.rsqrt(jnp.mean(xv * xv, axis=-1, keepdims=True) + EPS)
            h_ref[...] = (xv * r * g_ref[...]).astype(BF16)

        o_ref[...] = _dot(h_ref[...], w_ref[0, 0])

    return pl.pallas_call(
        body,
        name=name,
        grid=(s // tm, n_blocks),
        in_specs=[
            pl.BlockSpec((tm, d), lambda i, j: (i, 0)),
            pl.BlockSpec((1, d), lambda i, j: (0, 0)),
            pl.BlockSpec((1, 1, d, n), lambda i, j: (j, layer, 0, 0)),
        ],
        out_specs=[pl.BlockSpec((tm, d), lambda i, j: (i, 0)), pl.BlockSpec((tm, n), lambda i, j: (i, j))],
        out_shape=[jax.ShapeDtypeStruct((s, d), BF16), jax.ShapeDtypeStruct((s, n_blocks * n), F32)],
        compiler_params=_params(2),
    )(x, gain, w_s)


def _head_norm(xv, gain, low):
    sq = xv * xv
    s_low = jnp.sum(jnp.where(low, sq, 0.0), axis=-1, keepdims=True)
    s_high = jnp.sum(jnp.where(low, 0.0, sq), axis=-1, keepdims=True)
    r = jnp.where(low, lax.rsqrt(s_low / HEAD_DIM + EPS), lax.rsqrt(s_high / HEAD_DIM + EPS))
    return xv * r * gain, r


def _qkv_prep(proj, q_gain, k_gain, name):
    s = proj.shape[0]
    tm = TOKEN_TILE

    def body(p_ref, qg_ref, kg_ref, q_ref, k_ref, v_ref):
        low = lax.broadcasted_iota(jnp.int32, (tm, LANES), 1) < HEAD_DIM
        for g in range(ATTN_DIM // LANES):
            cq = slice(LANES * g, LANES * (g + 1))
            ck = slice(ATTN_DIM + LANES * g, ATTN_DIM + LANES * (g + 1))
            cv = slice(2 * ATTN_DIM + LANES * g, 2 * ATTN_DIM + LANES * (g + 1))
            qn, _ = _head_norm(p_ref[:, cq], qg_ref[...], low)
            kn, _ = _head_norm(p_ref[:, ck], kg_ref[...], low)
            q_ref[:, cq] = (qn * Q_SCALE).astype(BF16)
            k_ref[:, cq] = kn.astype(BF16)
            v_ref[:, cq] = p_ref[:, cv].astype(BF16)

    out = jax.ShapeDtypeStruct((s, ATTN_DIM), BF16)
    return pl.pallas_call(
        body,
        name=name,
        grid=(s // tm,),
        in_specs=[
            pl.BlockSpec((tm, 3 * ATTN_DIM), lambda i: (i, 0)),
            pl.BlockSpec((1, LANES), lambda i: (0, 0)),
            pl.BlockSpec((1, LANES), lambda i: (0, 0)),
        ],
        out_specs=[pl.BlockSpec((tm, ATTN_DIM), lambda i: (i, 0))] * 3,
        out_shape=[out, out, out],
        compiler_params=_params(1),
    )(proj, q_gain, k_gain)


def _qkv_prep_bwd(proj, q_gain, k_gain, dq, dk, dv, name):
    s = proj.shape[0]
    tm = TOKEN_TILE

    def norm_bwd(xv, gain, dy, low):
        _, r = _head_norm(xv, gain, low)
        xhat = xv * r
        dxhat = dy * gain
        prod = dxhat * xhat
        m_low = jnp.sum(jnp.where(low, prod, 0.0), axis=-1, keepdims=True)
        m_high = jnp.sum(jnp.where(low, 0.0, prod), axis=-1, keepdims=True)
        mean = jnp.where(low, m_low, m_high) / HEAD_DIM
        return r * (dxhat - xhat * mean), jnp.sum(dy * xhat, axis=0, keepdims=True)

    def body(p_ref, qg_ref, kg_ref, dq_ref, dk_ref, dv_ref, dp_ref, dqg_ref, dkg_ref):
        @pl.when(pl.program_id(0) == 0)
        def _():
            dqg_ref[...] = jnp.zeros_like(dqg_ref)
            dkg_ref[...] = jnp.zeros_like(dkg_ref)

        low = lax.broadcasted_iota(jnp.int32, (tm, LANES), 1) < HEAD_DIM
        for g in range(ATTN_DIM // LANES):
            cq = slice(LANES * g, LANES * (g + 1))
            ck = slice(ATTN_DIM + LANES * g, ATTN_DIM + LANES * (g + 1))
            cv = slice(2 * ATTN_DIM + LANES * g, 2 * ATTN_DIM + LANES * (g + 1))
            dxq, dgq = norm_bwd(p_ref[:, cq], qg_ref[...], dq_ref[:, cq] * Q_SCALE, low)
            dxk, dgk = norm_bwd(p_ref[:, ck], kg_ref[...], dk_ref[:, cq], low)
            dp_ref[:, cq] = dxq.astype(BF16)
            dp_ref[:, ck] = dxk.astype(BF16)
            dp_ref[:, cv] = dv_ref[:, cq].astype(BF16)
            dqg_ref[:, cq] += dgq
            dkg_ref[:, cq] += dgk

    grad_spec = pl.BlockSpec((tm, ATTN_DIM), lambda i: (i, 0))
    gain_spec = pl.BlockSpec((1, LANES), lambda i: (0, 0))
    sum_spec = pl.BlockSpec((1, ATTN_DIM), lambda i: (0, 0))
    return pl.pallas_call(
        body,
        name=name,
        grid=(s // tm,),
        in_specs=[pl.BlockSpec((tm, 3 * ATTN_DIM), lambda i: (i, 0)), gain_spec, gain_spec, grad_spec, grad_spec, grad_spec],
        out_specs=[pl.BlockSpec((tm, 3 * ATTN_DIM), lambda i: (i, 0)), sum_spec, sum_spec],
        out_shape=[
            jax.ShapeDtypeStruct((s, 3 * ATTN_DIM), BF16),
            jax.ShapeDtypeStruct((1, ATTN_DIM), F32),
            jax.ShapeDtypeStruct((1, ATTN_DIM), F32),
        ],
        compiler_params=_params(1),
    )(proj, q_gain, k_gain, dq, dk, dv)


def _attn_tile_consts(t):
    row = lax.broadcasted_iota(jnp.int32, (t, t), 0)
    col = lax.broadcasted_iota(jnp.int32, (t, t), 1)
    return row, col


def _attn_fwd(qn, kn, vb, name):
    s = qn.shape[0]
    t = min(ATTN_TILE, s)

    def body(q_ref, k_ref, v_ref, o_ref):
        i = pl.program_id(1)
        low = lax.broadcasted_iota(jnp.int32, (t, LANES), 1) < HEAD_DIM
        row, col = _attn_tile_consts(t)
        suffix = (row > col).astype(BF16)
        causal = col < row
        q = q_ref[...]
        zero_q = jnp.zeros_like(q)
        heads = []
        for head in range(2):
            qh = jnp.where(low, q, zero_q) if head == 0 else jnp.where(low, zero_q, q)

            def tile(kb, carry, diagonal):
                acc, after = carry
                off = pl.multiple_of(kb * t, t)
                ks = k_ref[pl.ds(off, t), :]
                vs = v_ref[pl.ds(off, t), :]
                z = _dot_nt(qh, ks)
                sp = _softplus(z)
                if diagonal:
                    sp = jnp.where(causal, sp, 0.0)
                hi, lo = _split_bf16(sp)
                inside = _dot(hi, suffix) + _dot(lo, suffix)
                a = jnp.exp(z - sp - inside - after)
                if diagonal:
                    a = jnp.where(causal, a, 0.0)
                acc = acc + _dot(a.astype(BF16), vs)
                return acc, after + jnp.sum(sp, axis=-1, keepdims=True)

            carry = tile(i, (jnp.zeros((t, LANES), F32), jnp.zeros((t, 1), F32)), True)
            carry = lax.fori_loop(0, i, lambda n, c: tile(i - 1 - n, c, False), carry)
            heads.append(carry[0])
        o_ref[...] = jnp.where(low, heads[0], heads[1]).astype(BF16)

    return pl.pallas_call(
        body,
        name=name,
        grid=(ATTN_DIM // LANES, s // t),
        in_specs=[
            pl.BlockSpec((t, LANES), lambda p, i: (i, p)),
            pl.BlockSpec((s, LANES), lambda p, i: (0, p)),
            pl.BlockSpec((s, LANES), lambda p, i: (0, p)),
        ],
        out_specs=pl.BlockSpec((t, LANES), lambda p, i: (i, p)),
        out_shape=jax.ShapeDtypeStruct((s, ATTN_DIM), BF16),
        compiler_params=_params(2),
    )(qn, kn, vb)


def _attn_bwd(qn, kn, vb, do, name):
    s = qn.shape[0]
    t = min(ATTN_TILE, s)
    nq = s // t

    def body(q_ref, k_ref, v_ref, do_ref, dq_ref, dk_ref, dv_ref, a_s, sg_s):
        i = pl.program_id(1)

        @pl.when(i == 0)
        def _():
            dk_ref[...] = jnp.zeros_like(dk_ref)
            dv_ref[...] = jnp.zeros_like(dv_ref)

        low = lax.broadcasted_iota(jnp.int32, (t, LANES), 1) < HEAD_DIM
        row, col = _attn_tile_consts(t)
        suffix = (row > col).astype(BF16)
        prefix = (row < col).astype(BF16)
        causal = col < row
        q = q_ref[...]
        dob = do_ref[...]
        zero_q = jnp.zeros_like(q)
        heads = []
        for head in range(2):
            if head == 0:
                qh, doh = jnp.where(low, q, zero_q), jnp.where(low, dob, zero_q)
            else:
                qh, doh = jnp.where(low, zero_q, q), jnp.where(low, zero_q, dob)

            def pass1(kb, after, diagonal):
                off = pl.multiple_of(kb * t, t)
                ks = k_ref[pl.ds(off, t), :]
                z = _dot_nt(qh, ks)
                sp = _softplus(z)
                if diagonal:
                    sp = jnp.where(causal, sp, 0.0)
                hi, lo = _split_bf16(sp)
                inside = _dot(hi, suffix) + _dot(lo, suffix)
                a = jnp.exp(z - sp - inside - after)
                sg = jnp.exp(z - sp)
                if diagonal:
                    a = jnp.where(causal, a, 0.0)
                    sg = jnp.where(causal, sg, 0.0)
                a_s[kb] = a
                sg_s[kb] = sg
                return after + jnp.sum(sp, axis=-1, keepdims=True)

            after = pass1(i, jnp.zeros((t, 1), F32), True)
            lax.fori_loop(0, i, lambda n, c: pass1(i - 1 - n, c, False), after)

            def pass2(kb, carry):
                dq, before = carry
                off = pl.multiple_of(kb * t, t)
                ks = k_ref[pl.ds(off, t), :]
                vs = v_ref[pl.ds(off, t), :]
                a = a_s[kb]
                sg = sg_s[kb]
                g = a * _dot_nt(doh, vs)
                hi, lo = _split_bf16(g)
                pre = _dot(hi, prefix) + _dot(lo, prefix) + before
                dz = (g * (1.0 - sg) - sg * pre).astype(BF16)
                dk_ref[pl.ds(off, t), :] += _dot_tn(dz, qh)
                dv_ref[pl.ds(off, t), :] += _dot_tn(a.astype(BF16), doh)
                return dq + _dot(dz, ks), before + jnp.sum(g, axis=-1, keepdims=True)

            dq, _ = lax.fori_loop(0, i + 1, pass2, (jnp.zeros((t, LANES), F32), jnp.zeros((t, 1), F32)))
            heads.append(dq)
        dq_ref[...] = jnp.where(low, heads[0], heads[1])

    q_spec = pl.BlockSpec((t, LANES), lambda p, i: (i, p))
    kv_spec = pl.BlockSpec((s, LANES), lambda p, i: (0, p))
    return pl.pallas_call(
        body,
        name=name,
        grid=(ATTN_DIM // LANES, nq),
        in_specs=[q_spec, kv_spec, kv_spec, q_spec],
        out_specs=[q_spec, kv_spec, kv_spec],
        out_shape=[jax.ShapeDtypeStruct((s, ATTN_DIM), F32)] * 3,
        scratch_shapes=[pltpu.VMEM((nq, t, t), F32), pltpu.VMEM((nq, t, t), F32)],
        compiler_params=_params(2),
    )(qn, kn, vb, do)


CB_BLOCK, CC_BLOCK, CU_BLOCK = 3, 4, 5


def _shift_down(h, prev_rows, n):
    row = lax.broadcasted_iota(jnp.int32, h.shape, 0)
    out = pltpu.roll(h, n, 0)
    for r in range(n):
        out = jnp.where(row == r, prev_rows[len(prev_rows) - n + r], out)
    return out


def _shift_up(h, next_rows, n):
    tm = h.shape[0]
    row = lax.broadcasted_iota(jnp.int32, h.shape, 0)
    out = pltpu.roll(h, tm - n, 0)
    for r in range(n):
        out = jnp.where(row == tm - n + r, next_rows[r], out)
    return out


def _conv_fwd(proj, conv_w, name):
    s = proj.shape[0]
    tm = TOKEN_TILE
    nb = tm // 8

    def body(cb_ref, cc_ref, cu_ref, pc_ref, pu_ref, w_ref, o_ref):
        first = pl.program_id(0) == 0
        h = cc_ref[...] * cu_ref[...]
        prev = [jnp.where(first, 0.0, pc_ref[r : r + 1, :] * pu_ref[r : r + 1, :]) for r in (6, 7)]
        y = w_ref[0:1, :] * _shift_down(h, prev, 2) + w_ref[1:2, :] * _shift_down(h, prev, 1) + w_ref[2:3, :] * h
        o_ref[...] = (cb_ref[...] * y).astype(BF16)

    def col(block):
        return pl.BlockSpec((tm, CONV_DIM), lambda i: (i, block))

    def halo(block):
        return pl.BlockSpec((8, CONV_DIM), lambda i: (jnp.maximum(i * nb - 1, 0), block))

    return pl.pallas_call(
        body,
        name=name,
        grid=(s // tm,),
        in_specs=[col(CB_BLOCK), col(CC_BLOCK), col(CU_BLOCK), halo(CC_BLOCK), halo(CU_BLOCK), pl.BlockSpec((8, CONV_DIM), lambda i: (0, 0))],
        out_specs=pl.BlockSpec((tm, CONV_DIM), lambda i: (i, 0)),
        out_shape=jax.ShapeDtypeStruct((s, CONV_DIM), BF16),
        compiler_params=_params(1),
    )(proj, proj, proj, proj, proj, conv_w)


def _conv_bwd(proj, conv_w, dconv, name):
    s = proj.shape[0]
    tm = TOKEN_TILE
    nb = tm // 8
    n_tiles = s // tm

    def body(cb_ref, cc_ref, cu_ref, dy_ref, pc_ref, pu_ref, nb_ref, ndy_ref, w_ref, dp_ref, dw_ref):
        i = pl.program_id(0)

        @pl.when(i == 0)
        def _():
            dw_ref[...] = jnp.zeros_like(dw_ref)

        first = i == 0
        last = i == n_tiles - 1
        cc, cu, cb, dy = cc_ref[...], cu_ref[...], cb_ref[...], dy_ref[...]
        h = cc * cu
        prev = [jnp.where(first, 0.0, pc_ref[r : r + 1, :] * pu_ref[r : r + 1, :]) for r in (6, 7)]
        h1 = _shift_down(h, prev, 1)
        h2 = _shift_down(h, prev, 2)
        y = w_ref[0:1, :] * h2 + w_ref[1:2, :] * h1 + w_ref[2:3, :] * h
        dyb = dy * cb
        nxt = [jnp.where(last, 0.0, ndy_ref[r : r + 1, :] * nb_ref[r : r + 1, :]) for r in (0, 1)]
        dh = w_ref[2:3, :] * dyb + w_ref[1:2, :] * _shift_up(dyb, nxt, 1) + w_ref[0:1, :] * _shift_up(dyb, nxt, 2)
        dp_ref[:, 0:CONV_DIM] = (dy * y).astype(BF16)
        dp_ref[:, CONV_DIM : 2 * CONV_DIM] = (dh * cu).astype(BF16)
        dp_ref[:, 2 * CONV_DIM : 3 * CONV_DIM] = (dh * cc).astype(BF16)
        dw_ref[0:1, :] += jnp.sum(dyb * h2, axis=0, keepdims=True)
        dw_ref[1:2, :] += jnp.sum(dyb * h1, axis=0, keepdims=True)
        dw_ref[2:3, :] += jnp.sum(dyb * h, axis=0, keepdims=True)

    def col(block):
        return pl.BlockSpec((tm, CONV_DIM), lambda i: (i, block))

    def halo_prev(block):
        return pl.BlockSpec((8, CONV_DIM), lambda i: (jnp.maximum(i * nb - 1, 0), block))

    def halo_next(block):
        return pl.BlockSpec((8, CONV_DIM), lambda i: (jnp.minimum((i + 1) * nb, s // 8 - 1), block))

    return pl.pallas_call(
        body,
        name=name,
        grid=(n_tiles,),
        in_specs=[
            col(CB_BLOCK), col(CC_BLOCK), col(CU_BLOCK), col(0),
            halo_prev(CC_BLOCK), halo_prev(CU_BLOCK), halo_next(CB_BLOCK), halo_next(0),
            pl.BlockSpec((8, CONV_DIM), lambda i: (0, 0)),
        ],
        out_specs=[pl.BlockSpec((tm, 3 * CONV_DIM), lambda i: (i, 0)), pl.BlockSpec((8, CONV_DIM), lambda i: (0, 0))],
        out_shape=[jax.ShapeDtypeStruct((s, 3 * CONV_DIM), BF16), jax.ShapeDtypeStruct((8, CONV_DIM), F32)],
        compiler_params=_params(1),
    )(proj, proj, proj, dconv, proj, proj, proj, dconv, conv_w)


def _out_proj(x, attn, conv, w_s, layer, name):
    s, d = x.shape
    tm = TOKEN_TILE
    rows = w_s.shape[2]

    def body(x_ref, a_ref, c_ref, w_ref, o_ref):
        acc = x_ref[...]
        for j in range(N_CHIPS):
            src = a_ref if j < 2 else c_ref
            cols = slice((j % 2) * rows, (j % 2 + 1) * rows)
            acc = acc + _dot(src[:, cols], w_ref[j, 0])
        o_ref[...] = acc

    return pl.pallas_call(
        body,
        name=name,
        grid=(s // tm,),
        in_specs=[
            pl.BlockSpec((tm, d), lambda i: (i, 0)),
            pl.BlockSpec((tm, ATTN_DIM), lambda i: (i, 0)),
            pl.BlockSpec((tm, CONV_DIM), lambda i: (i, 0)),
            pl.BlockSpec((N_CHIPS, 1, rows, d), lambda i: (0, layer, 0, 0)),
        ],
        out_specs=pl.BlockSpec((tm, d), lambda i: (i, 0)),
        out_shape=jax.ShapeDtypeStruct((s, d), F32),
        compiler_params=_params(1),
    )(x, attn, conv, w_s)


def _out_proj_bwd(dx, w_s, layer, name):
    s, d = dx.shape
    tm = TOKEN_TILE
    rows = w_s.shape[2]

    def body(dx_ref, w_ref, da_ref, dc_ref, dxb_ref):
        dxb = dx_ref[...].astype(BF16)
        dxb_ref[...] = dxb
        for j in range(N_CHIPS):
            cols = slice((j % 2) * rows, (j % 2 + 1) * rows)
            part = _dot_nt(dxb, w_ref[j, 0])
            if j < 2:
                da_ref[:, cols] = part.astype(BF16)
            else:
                dc_ref[:, cols] = part

    return pl.pallas_call(
        body,
        name=name,
        grid=(s // tm,),
        in_specs=[pl.BlockSpec((tm, d), lambda i: (i, 0)), pl.BlockSpec((N_CHIPS, 1, rows, d), lambda i: (0, layer, 0, 0))],
        out_specs=[
            pl.BlockSpec((tm, ATTN_DIM), lambda i: (i, 0)),
            pl.BlockSpec((tm, CONV_DIM), lambda i: (i, 0)),
            pl.BlockSpec((tm, d), lambda i: (i, 0)),
        ],
        out_shape=[
            jax.ShapeDtypeStruct((s, ATTN_DIM), BF16),
            jax.ShapeDtypeStruct((s, CONV_DIM), F32),
            jax.ShapeDtypeStruct((s, d), BF16),
        ],
        compiler_params=_params(1),
    )(dx, w_s)


def _ffn_fwd(x, gain, wg_s, wu_s, wd_s, layer, name):
    s, d = x.shape
    tm = TOKEN_TILE
    f = wg_s.shape[3]

    def body(x_ref, g_ref, wg_ref, wu_ref, wd_ref, o_ref, h_s):
        j = pl.program_id(1)

        @pl.when(j == 0)
        def _():
            xv = x_ref[...]
            r = lax.rsqrt(jnp.mean(xv * xv, axis=-1, keepdims=True) + EPS)
            h_s[...] = (xv * r * g_ref[...]).astype(BF16)
            o_ref[...] = xv

        h = h_s[...]
        gate = _dot(h, wg_ref[0, 0])
        up = _dot(h, wu_ref[0, 0])
        act = (gate / (1.0 + jnp.exp(-gate))) * up
        o_ref[...] += _dot(act.astype(BF16), wd_ref[0, 0])

    return pl.pallas_call(
        body,
        name=name,
        grid=(s // tm, N_CHIPS),
        in_specs=[
            pl.BlockSpec((tm, d), lambda i, j: (i, 0)),
            pl.BlockSpec((1, d), lambda i, j: (0, 0)),
            pl.BlockSpec((1, 1, d, f), lambda i, j: (j, layer, 0, 0)),
            pl.BlockSpec((1, 1, d, f), lambda i, j: (j, layer, 0, 0)),
            pl.BlockSpec((1, 1, f, d), lambda i, j: (j, layer, 0, 0)),
        ],
        out_specs=pl.BlockSpec((tm, d), lambda i, j: (i, 0)),
        out_shape=jax.ShapeDtypeStruct((s, d), F32),
        scratch_shapes=[pltpu.VMEM((tm, d), BF16)],
        compiler_params=_params(2),
    )(x, gain, wg_s, wu_s, wd_s)


def _rms_bwd(xv, gain, dh):
    r = lax.rsqrt(jnp.mean(xv * xv, axis=-1, keepdims=True) + EPS)
    xhat = xv * r
    dxhat = dh * gain
    dx = r * (dxhat - xhat * jnp.mean(dxhat * xhat, axis=-1, keepdims=True))
    return dx, jnp.sum(dh * xhat, axis=0, keepdims=True)


def _ffn_bwd(x, dy, gain, wg_s, wu_s, wd_s, layer, name):
    s, d = x.shape
    tm = TOKEN_TILE
    f = wg_s.shape[3]

    def body(x_ref, dy_ref, g_ref, wg_ref, wu_ref, wd_ref, dx_ref, dgain_ref, h_ref, dyb_ref, dg_ref, du_ref, act_ref, acc_s):
        i, j = pl.program_id(0), pl.program_id(1)

        @pl.when((i == 0) & (j == 0))
        def _():
            dgain_ref[...] = jnp.zeros_like(dgain_ref)

        @pl.when(j == 0)
        def _():
            xv = x_ref[...]
            r = lax.rsqrt(jnp.mean(xv * xv, axis=-1, keepdims=True) + EPS)
            h_ref[...] = (xv * r * g_ref[...]).astype(BF16)
            dyb_ref[...] = dy_ref[...].astype(BF16)
            acc_s[...] = jnp.zeros_like(acc_s)

        h = h_ref[...]
        gate = _dot(h, wg_ref[0, 0])
        up = _dot(h, wu_ref[0, 0])
        sig = 1.0 / (1.0 + jnp.exp(-gate))
        silu = gate * sig
        dact = _dot_nt(dyb_ref[...], wd_ref[0, 0])
        dgate = (dact * up * (sig * (1.0 + gate * (1.0 - sig)))).astype(BF16)
        dup = (dact * silu).astype(BF16)
        act_ref[0] = (silu * up).astype(BF16)
        dg_ref[0] = dgate
        du_ref[0] = dup
        acc_s[...] += _dot_nt(dgate, wg_ref[0, 0]) + _dot_nt(dup, wu_ref[0, 0])

        @pl.when(j == N_CHIPS - 1)
        def _():
            dxn, dgain = _rms_bwd(x_ref[...], g_ref[...], acc_s[...])
            dx_ref[...] = dy_ref[...] + dxn
            dgain_ref[...] += dgain

    tok = pl.BlockSpec((tm, d), lambda i, j: (i, 0))
    vec = pl.BlockSpec((1, d), lambda i, j: (0, 0))
    hid = pl.BlockSpec((1, tm, f), lambda i, j: (j, i, 0))
    hid_shape = jax.ShapeDtypeStruct((N_CHIPS, s, f), BF16)
    return pl.pallas_call(
        body,
        name=name,
        grid=(s // tm, N_CHIPS),
        in_specs=[
            tok, tok, vec,
            pl.BlockSpec((1, 1, d, f), lambda i, j: (j, layer, 0, 0)),
            pl.BlockSpec((1, 1, d, f), lambda i, j: (j, layer, 0, 0)),
            pl.BlockSpec((1, 1, f, d), lambda i, j: (j, layer, 0, 0)),
        ],
        out_specs=[tok, vec, tok, tok, hid, hid, hid],
        out_shape=[
            jax.ShapeDtypeStruct((s, d), F32),
            jax.ShapeDtypeStruct((1, d), F32),
            jax.ShapeDtypeStruct((s, d), BF16),
            jax.ShapeDtypeStruct((s, d), BF16),
            hid_shape, hid_shape, hid_shape,
        ],
        scratch_shapes=[pltpu.VMEM((tm, d), F32)],
        compiler_params=_params(2),
    )(x, dy, gain, wg_s, wu_s, wd_s)


def _in_proj_bwd(x, dx_res, gain, dproj, w_s, layer, name):
    s, d = x.shape
    tm = TOKEN_TILE
    n = w_s.shape[3]

    def body(x_ref, r_ref, g_ref, dp_ref, w_ref, dx_ref, dgain_ref, acc_s):
        i, j = pl.program_id(0), pl.program_id(1)

        @pl.when((i == 0) & (j == 0))
        def _():
            dgain_ref[...] = jnp.zeros_like(dgain_ref)

        @pl.when(j == 0)
        def _():
            acc_s[...] = jnp.zeros_like(acc_s)

        acc_s[...] += _dot_nt(dp_ref[...], w_ref[0, 0])

        @pl.when(j == N_CHIPS - 1)
        def _():
            dxn, dgain = _rms_bwd(x_ref[...], g_ref[...], acc_s[...])
            dx_ref[...] = r_ref[...] + dxn
            dgain_ref[...] += dgain

    tok = pl.BlockSpec((tm, d), lambda i, j: (i, 0))
    vec = pl.BlockSpec((1, d), lambda i, j: (0, 0))
    return pl.pallas_call(
        body,
        name=name,
        grid=(s // tm, N_CHIPS),
        in_specs=[tok, tok, vec, pl.BlockSpec((tm, n), lambda i, j: (i, j)), pl.BlockSpec((1, 1, d, n), lambda i, j: (j, layer, 0, 0))],
        out_specs=[tok, vec],
        out_shape=[jax.ShapeDtypeStruct((s, d), F32), jax.ShapeDtypeStruct((1, d), F32)],
        scratch_shapes=[pltpu.VMEM((tm, d), F32)],
        compiler_params=_params(2),
    )(x, dx_res, gain, dproj, w_s)


def _loss_grad(y, target, name):
    s, d = y.shape
    tm = TOKEN_TILE

    def body(y_ref, t_ref, dy_ref, l_ref):
        @pl.when(pl.program_id(0) == 0)
        def _():
            l_ref[...] = jnp.zeros_like(l_ref)

        err = y_ref[...] - t_ref[...]
        dy_ref[...] = err / d
        l_ref[...] += jnp.sum(err * err, axis=0, keepdims=True) * (0.5 / d)

    tok = pl.BlockSpec((tm, d), lambda i: (i, 0))
    return pl.pallas_call(
        body,
        name=name,
        grid=(s // tm,),
        in_specs=[tok, tok],
        out_specs=[tok, pl.BlockSpec((1, d), lambda i: (0, 0))],
        out_shape=[jax.ShapeDtypeStruct((s, d), F32), jax.ShapeDtypeStruct((1, d), F32)],
        compiler_params=_params(1),
    )(y, target)


def _wgrad(a, b, a_spec, b_spec, n_blocks, k, n, name):
    n_tiles = (a.shape[-2]) // TOKEN_TILE

    def body(a_ref, b_ref, o_ref):
        @pl.when(pl.program_id(1) == 0)
        def _():
            o_ref[...] = jnp.zeros_like(o_ref)

        av = a_ref[0] if len(a_ref.shape) == 3 else a_ref[...]
        bv = b_ref[0] if len(b_ref.shape) == 3 else b_ref[...]
        o_ref[0] += _dot_tn(av, bv)

    return pl.pallas_call(
        body,
        name=name,
        grid=(n_blocks, n_tiles),
        in_specs=[a_spec, b_spec],
        out_specs=pl.BlockSpec((1, k, n), lambda j, i: (j, 0, 0)),
        out_shape=jax.ShapeDtypeStruct((n_blocks, k, n), F32),
        compiler_params=_params(2),
    )(a, b)


def _mesh_position():
    return lax.axis_index("x"), lax.axis_index("y"), lax.axis_index("c")


def _other_chips(x, y):
    return [(1 - x, y), (x, 1 - y), (1 - x, 1 - y)]


def _half_rows(ref_rows, c):
    half = ref_rows // 2
    return pl.ds(c * half, half)


def _gather_weights(shards):
    n = len(shards)

    def body(*refs):
        ins, outs = refs[:n], refs[n : 2 * n]
        send_sems, recv_sems, pass_send_sems, pass_recv_sems, local_sems = refs[2 * n :]
        x, y, c = _mesh_position()
        me = 2 * x + y
        sibling = (x, y, 1 - c)
        chips = _other_chips(x, y)

        def block(t, chip_index, core):
            return outs[t].at[chip_index, :, _half_rows(ins[t].shape[1], core), :]

        def copy(t, k, chip_index, core, to, sems, src=None):
            dst = block(t, chip_index, core)
            return pltpu.make_async_remote_copy(
                src_ref=dst if src is None else src, dst_ref=dst, send_sem=sems[0].at[t, k], recv_sem=sems[1].at[t, k],
                device_id=to, device_id_type=MESH_ID,
            )

        ici, d2d = (send_sems, recv_sems), (pass_send_sems, pass_recv_sems)
        own = [pltpu.make_async_copy(ins[t], outs[t].at[me], local_sems.at[t]) for t in range(n)]
        for cp in own:
            cp.start()
        started = []
        for t in range(n):
            mine = ins[t].at[:, _half_rows(ins[t].shape[1], c), :]
            for k, (px, py) in enumerate(chips):
                started.append(copy(t, k, me, c, (px, py, c), ici, src=mine))
                started[-1].start()
        for t in range(n):
            for k, (px, py) in enumerate(chips):
                copy(t, k, 2 * px + py, c, sibling, ici).wait_recv()
                started.append(copy(t, k, 2 * px + py, c, sibling, d2d))
                started[-1].start()
        for t in range(n):
            for k, (px, py) in enumerate(chips):
                copy(t, k, 2 * px + py, 1 - c, sibling, d2d).wait_recv()
        for cp in started:
            cp.wait_send()
        for cp in own:
            cp.wait()

    sems = pltpu.SemaphoreType.DMA((n, N_CHIPS - 1))
    return pl.pallas_call(
        body,
        name="gather_weights",
        in_specs=[ANY] * n,
        out_specs=[ANY] * n,
        out_shape=[jax.ShapeDtypeStruct((N_CHIPS,) + w.shape, w.dtype) for w in shards],
        scratch_shapes=[sems, sems, sems, sems, pltpu.SemaphoreType.DMA((n,))],
    )(*shards)


def _swap_halves(grads):
    n = len(grads)

    def body(*refs):
        ins, outs = refs[:n], refs[n : 2 * n]
        send_sems, recv_sems = refs[2 * n :]
        x, y, c = _mesh_position()
        copies = []
        for t in range(n):
            copies.append(pltpu.make_async_remote_copy(
                src_ref=ins[t].at[:, _half_rows(ins[t].shape[1], 1 - c), :], dst_ref=outs[t],
                send_sem=send_sems.at[t], recv_sem=recv_sems.at[t], device_id=(x, y, 1 - c), device_id_type=MESH_ID,
            ))
            copies[-1].start()
        for cp in copies:
            cp.wait()

    sems = pltpu.SemaphoreType.DMA((n,))
    return pl.pallas_call(
        body,
        name="swap_halves",
        in_specs=[ANY] * n,
        out_specs=[ANY] * n,
        out_shape=[jax.ShapeDtypeStruct((g.shape[0], g.shape[1] // 2, g.shape[2]), g.dtype) for g in grads],
        scratch_shapes=[sems, sems],
    )(*grads)


def _scatter_to_chips(parts):
    n = len(parts)

    def body(*refs):
        ins, outs = refs[:n], refs[n : 2 * n]
        send_sems, recv_sems = refs[2 * n :]
        x, y, c = _mesh_position()
        copies = []
        for t in range(n):
            for k, (px, py) in enumerate(_other_chips(x, y)):
                copies.append(pltpu.make_async_remote_copy(
                    src_ref=ins[t].at[2 * px + py], dst_ref=outs[t].at[k],
                    send_sem=send_sems.at[t, k], recv_sem=recv_sems.at[t, k], device_id=(px, py, c), device_id_type=MESH_ID,
                ))
                copies[-1].start()
        for cp in copies:
            cp.wait()

    sems = pltpu.SemaphoreType.DMA((n, N_CHIPS - 1))
    return pl.pallas_call(
        body,
        name="scatter_to_chips",
        in_specs=[ANY] * n,
        out_specs=[ANY] * n,
        out_shape=[jax.ShapeDtypeStruct((N_CHIPS - 1,) + p.shape[1:], p.dtype) for p in parts],
        scratch_shapes=[sems, sems],
    )(*parts)


def _join_halves(halves, n_layers):
    n = len(halves)
    per_layer = n // n_layers

    def body(*refs):
        ins, outs = refs[:n], refs[n : n + per_layer]
        send_sems, recv_sems, local_sems = refs[n + per_layer :]
        x, y, c = _mesh_position()
        local, remote = [], []
        for t in range(n):
            layer, tensor = divmod(t, per_layer)
            dst = outs[tensor].at[layer, _half_rows(outs[tensor].shape[1], c), :]
            local.append(pltpu.make_async_copy(ins[t], dst, local_sems.at[t]))
            local[-1].start()
            remote.append(pltpu.make_async_remote_copy(
                src_ref=ins[t], dst_ref=dst, send_sem=send_sems.at[t], recv_sem=recv_sems.at[t],
                device_id=(x, y, 1 - c), device_id_type=MESH_ID,
            ))
            remote[-1].start()
        for cp in remote:
            cp.wait()
        for cp in local:
            cp.wait()

    sems = pltpu.SemaphoreType.DMA((n,))
    return pl.pallas_call(
        body,
        name="join_halves",
        in_specs=[ANY] * n,
        out_specs=[ANY] * per_layer,
        out_shape=[jax.ShapeDtypeStruct((n_layers, 2 * h.shape[0], h.shape[1]), h.dtype) for h in halves[:per_layer]],
        scratch_shapes=[sems, sems, sems],
    )(*halves)


def _gather_small(pack):
    def body(p_ref, o_ref, send_sems, recv_sems, local_sem):
        x, y, c = _mesh_position()
        own = pltpu.make_async_copy(p_ref, o_ref.at[4 * x + 2 * y + c], local_sem)
        own.start()
        copies = []
        for k in range(1, N_DEV):
            px, py, pc = x ^ (k >> 2), y ^ ((k >> 1) & 1), c ^ (k & 1)
            send = pltpu.make_async_remote_copy(
                src_ref=p_ref, dst_ref=o_ref.at[4 * x + 2 * y + c], send_sem=send_sems.at[k - 1], recv_sem=recv_sems.at[k - 1],
                device_id=(px, py, pc), device_id_type=MESH_ID,
            )
            send.start()
            copies.append((send, 4 * px + 2 * py + pc))
        for send, peer_slot in copies:
            send.wait_send()
        for k in range(1, N_DEV):
            px, py, pc = x ^ (k >> 2), y ^ ((k >> 1) & 1), c ^ (k & 1)
            pltpu.make_async_remote_copy(
                src_ref=p_ref, dst_ref=o_ref.at[4 * px + 2 * py + pc], send_sem=send_sems.at[k - 1], recv_sem=recv_sems.at[k - 1],
                device_id=(px, py, pc), device_id_type=MESH_ID,
            ).wait_recv()
        own.wait()

    sems = pltpu.SemaphoreType.DMA((N_DEV - 1,))
    return pl.pallas_call(
        body,
        name="gather_small",
        in_specs=[VMEM_SPEC],
        out_specs=VMEM_SPEC,
        out_shape=jax.ShapeDtypeStruct((N_DEV,) + pack.shape, pack.dtype),
        scratch_shapes=[sems, sems, pltpu.SemaphoreType.DMA],
    )(pack)


def _row_tile(rows):
    for tile in (256, 128, 64, 32, 16, 8):
        if rows % tile == 0:
            return tile
    return rows


def _add_half(grad, received, half_index, name):
    slots, h, cdim = received.shape
    tile = _row_tile(h)
    per_half = h // tile

    def body(c_ref, g_ref, r_ref, o_ref):
        o_ref[...] = g_ref[...] + r_ref[...]

    grid_spec = pltpu.PrefetchScalarGridSpec(
        num_scalar_prefetch=1,
        grid=(slots, per_half),
        in_specs=[
            pl.BlockSpec((1, tile, cdim), lambda j, i, c: (j, c[0] * per_half + i, 0)),
            pl.BlockSpec((1, tile, cdim), lambda j, i, c: (j, i, 0)),
        ],
        out_specs=pl.BlockSpec((1, tile, cdim), lambda j, i, c: (j, i, 0)),
    )
    return pl.pallas_call(
        body, name=name, grid_spec=grid_spec, out_shape=jax.ShapeDtypeStruct(received.shape, F32), compiler_params=_params(2)
    )(half_index, grad, received)


def _add_chips(part, received, chip_index, name):
    _, h, cdim = part.shape
    tile = _row_tile(h)

    def body(j_ref, p_ref, r_ref, o_ref):
        o_ref[...] = ((p_ref[0] + r_ref[0]) + r_ref[1]) + r_ref[2]

    grid_spec = pltpu.PrefetchScalarGridSpec(
        num_scalar_prefetch=1,
        grid=(h // tile,),
        in_specs=[
            pl.BlockSpec((1, tile, cdim), lambda i, j: (j[0], i, 0)),
            pl.BlockSpec((N_CHIPS - 1, tile, cdim), lambda i, j: (0, i, 0)),
        ],
        out_specs=pl.BlockSpec((tile, cdim), lambda i, j: (i, 0)),
    )
    return pl.pallas_call(
        body, name=name, grid_spec=grid_spec, out_shape=jax.ShapeDtypeStruct((h, cdim), F32), compiler_params=_params(1)
    )(chip_index, part, received)


def _adamw(w, g, m, v, name):
    rows, cdim = w.shape
    tile = _row_tile(rows)

    def body(w_ref, g_ref, m_ref, v_ref, d_ref, nm_ref, nv_ref):
        gv = g_ref[...]
        nm = ADAM_B1 * m_ref[...] + (1.0 - ADAM_B1) * gv
        nv = ADAM_B2 * v_ref[...] + (1.0 - ADAM_B2) * (gv * gv)
        m_hat = nm / (1.0 - ADAM_B1 ** ADAM_STEP)
        v_hat = nv / (1.0 - ADAM_B2 ** ADAM_STEP)
        d_ref[...] = -ADAM_LR * (m_hat / (jnp.sqrt(v_hat) + ADAM_EPS) + ADAM_WD * w_ref[...])
        nm_ref[...] = nm
        nv_ref[...] = nv

    spec = pl.BlockSpec((tile, cdim), lambda i: (i, 0))
    shape = jax.ShapeDtypeStruct((rows, cdim), F32)
    return pl.pallas_call(
        body, name=name, grid=(rows // tile,), in_specs=[spec] * 4, out_specs=[spec] * 3, out_shape=[shape] * 3,
        compiler_params=_params(1),
    )(w, g, m, v)


SMALL_ROWS, SMALL_COLS = 24, 1024
ROW_NORM_MIX, ROW_NORM_FFN, ROW_LOSS, ROW_Q_NORM, ROW_K_NORM, ROW_CONV = 0, 2, 4, 8, 10, 16


def _sum_small(gathered):
    def body(g_ref, o_ref, heads_ref, lanes_ref):
        total = g_ref[0]
        for dev in range(1, N_DEV):
            total = total + g_ref[dev]
        o_ref[...] = total
        heads = o_ref[8:16, 0:LANES]
        for grp in range(1, ATTN_DIM // LANES):
            heads = heads + o_ref[8:16, grp * LANES : (grp + 1) * LANES]
        heads_ref[...] = heads + pltpu.roll(heads, HEAD_DIM, 1)
        lanes_ref[...] = jnp.broadcast_to(jnp.sum(o_ref[0:8, :], axis=-1, keepdims=True), (8, LANES))

    return pl.pallas_call(
        body,
        name="sum_small",
        in_specs=[VMEM_SPEC],
        out_specs=[VMEM_SPEC] * 3,
        out_shape=[jax.ShapeDtypeStruct((SMALL_ROWS, SMALL_COLS), F32), jax.ShapeDtypeStruct((8, LANES), F32), jax.ShapeDtypeStruct((8, LANES), F32)],
    )(gathered)


def _pad_rows(a, rows):
    return jnp.pad(a, ((0, rows - a.shape[0]), (0, 0)))


def _pad_to(a, rows, cols):
    return jnp.pad(a, ((0, rows - a.shape[0]), (0, cols - a.shape[1])))


def _local_step(x, target, norm_mix, q_norm, k_norm, norm_ffn, conv_full, win_s, wout_s, wg_s, wu_s, wd_s):
    n_layers = norm_mix.shape[0]
    s, d = x.shape
    tm = TOKEN_TILE
    n_in = win_s.shape[3]
    f = wg_s.shape[3]
    saved = []
    for l in range(n_layers):
        q_gain = jnp.tile(q_norm[l][None, :], (1, 2))
        k_gain = jnp.tile(k_norm[l][None, :], (1, 2))
        h1, proj = _norm_matmul(x, norm_mix[l][None, :], win_s, l, f"in_proj_{l}")
        qn, kn, vb = _qkv_prep(proj, q_gain, k_gain, f"qkv_prep_{l}")
        attn = _attn_fwd(qn, kn, vb, f"attn_fwd_{l}")
        conv = _conv_fwd(proj, conv_full[l], f"conv_fwd_{l}")
        x_mid = _out_proj(x, attn, conv, wout_s, l, f"out_proj_{l}")
        x_out = _ffn_fwd(x_mid, norm_ffn[l][None, :], wg_s, wu_s, wd_s, l, f"ffn_fwd_{l}")
        saved.append(dict(x=x, h1=h1, proj=proj, qn=qn, kn=kn, vb=vb, attn=attn, conv=conv, x_mid=x_mid, q_gain=q_gain, k_gain=k_gain))
        x = x_out

    dy, loss_lanes = _loss_grad(x, target, "loss_grad")
    grads = [None] * n_layers
    for l in reversed(range(n_layers)):
        sv = saved[l]
        dx_mid, d_norm_ffn, h2, dyb, dgate, dup, act = _ffn_bwd(sv["x_mid"], dy, norm_ffn[l][None, :], wg_s, wu_s, wd_s, l, f"ffn_bwd_{l}")
        tok2 = pl.BlockSpec((tm, d), lambda j, i: (i, 0))
        hid = pl.BlockSpec((1, tm, f), lambda j, i: (j, i, 0))
        d_wg = _wgrad(h2, dgate, tok2, hid, N_CHIPS, d, f, f"wgrad_gate_{l}")
        d_wu = _wgrad(h2, dup, tok2, hid, N_CHIPS, d, f, f"wgrad_up_{l}")
        d_wd = _wgrad(act, dyb, hid, tok2, N_CHIPS, f, d, f"wgrad_down_{l}")
        d_attn, d_conv, dxb = _out_proj_bwd(dx_mid, wout_s, l, f"out_proj_bwd_{l}")
        rows_out = wout_s.shape[2]
        mix_spec_a = pl.BlockSpec((tm, rows_out), lambda j, i: (i, j))
        d_wout_a = _wgrad(sv["attn"], dxb, mix_spec_a, tok2, ATTN_DIM // rows_out, rows_out, d, f"wgrad_out_attn_{l}")
        d_wout_c = _wgrad(sv["conv"], dxb, mix_spec_a, tok2, CONV_DIM // rows_out, rows_out, d, f"wgrad_out_conv_{l}")
        d_wout = jnp.concatenate([d_wout_a, d_wout_c], axis=0)
        dq, dk, dv = _attn_bwd(sv["qn"], sv["kn"], sv["vb"], d_attn, f"attn_bwd_{l}")
        dproj_a, d_qg, d_kg = _qkv_prep_bwd(sv["proj"], sv["q_gain"], sv["k_gain"], dq, dk, dv, f"qkv_prep_bwd_{l}")
        dproj_b, d_conv_w = _conv_bwd(sv["proj"], conv_full[l], d_conv, f"conv_bwd_{l}")
        dproj = jnp.concatenate([dproj_a, dproj_b], axis=1)
        d_win = _wgrad(sv["h1"], dproj, tok2, pl.BlockSpec((tm, n_in), lambda j, i: (i, j)), N_CHIPS, d, n_in, f"wgrad_in_{l}")
        dy, d_norm_mix = _in_proj_bwd(sv["x"], dx_mid, norm_mix[l][None, :], dproj, win_s, l, f"in_proj_bwd_{l}")
        grads[l] = dict(norm_mix=d_norm_mix, norm_ffn=d_norm_ffn, q_norm=d_qg, k_norm=d_kg, conv_w=d_conv_w,
                        w_in=d_win, w_out=d_wout, w_gate=d_wg, w_up=d_wu, w_down=d_wd)
    return loss_lanes, dy, grads


BIG = ("w_in", "w_out", "w_gate", "w_up", "w_down")


def kernel(x, norm_mix, w_in, q_norm, k_norm, conv_w, w_out, norm_ffn, w_gate, w_up, w_down, loss_target, m_norm_mix, m_w_in, m_q_norm, m_k_norm, m_conv_w, m_w_out, m_norm_ffn, m_w_gate, m_w_up, m_w_down, v_norm_mix, v_w_in, v_q_norm, v_k_norm, v_conv_w, v_w_out, v_norm_ffn, v_w_gate, v_w_up, v_w_down):
    n_layers = norm_mix.shape[0]
    weights = dict(w_in=w_in, w_out=w_out, w_gate=w_gate, w_up=w_up, w_down=w_down)
    moments_m = dict(w_in=m_w_in, w_out=m_w_out, w_gate=m_w_gate, w_up=m_w_up, w_down=m_w_down)
    moments_v = dict(w_in=v_w_in, w_out=v_w_out, w_gate=v_w_gate, w_up=v_w_up, w_down=v_w_down)
    cx, cy, cc = _mesh_position()
    chip_index = (2 * cx + cy).astype(jnp.int32).reshape(1)
    core_index = cc.astype(jnp.int32).reshape(1)

    conv_pad = jnp.pad(conv_w, ((0, 0), (0, 16 - conv_w.shape[1]), (0, 0)))
    gathered = _gather_weights([weights[k].astype(BF16) for k in BIG] + [conv_pad])
    win_s, wout_s, wg_s, wu_s, wd_s, conv_s = gathered
    conv_full = jnp.transpose(conv_s[:, :, 0:8], (1, 2, 0, 3)).reshape(n_layers, 8, N_CHIPS * conv_w.shape[2])

    loss_lanes, grad_x, grads = _local_step(
        x[0], loss_target[0], norm_mix, q_norm, k_norm, norm_ffn, conv_full, win_s, wout_s, wg_s, wu_s, wd_s)

    flat = [grads[l][k] for l in range(n_layers) for k in BIG]
    names = [f"{k}_{l}" for l in range(n_layers) for k in BIG]
    received = _swap_halves(flat)
    parts = [_add_half(g, r, core_index, f"add_half_{nm}") for g, r, nm in zip(flat, received, names)]
    from_chips = _scatter_to_chips(parts)
    halves = [_add_chips(p, r, chip_index, f"add_chips_{nm}") for p, r, nm in zip(parts, from_chips, names)]
    big_grads = dict(zip(BIG, _join_halves(halves, n_layers)))

    def lanes(a):
        return _pad_to(a, a.shape[0], SMALL_COLS)

    def tile_of(*groups):
        return _pad_rows(jnp.concatenate([lanes(jnp.concatenate(g, axis=0)) for g in groups], axis=0), 8)

    layers = range(n_layers)
    pack = jnp.concatenate([
        tile_of([grads[l]["norm_mix"] for l in layers], [grads[l]["norm_ffn"] for l in layers], [loss_lanes]),
        tile_of([grads[l]["q_norm"] for l in layers], [grads[l]["k_norm"] for l in layers]),
        tile_of([grads[l]["conv_w"][0:3] for l in layers]),
    ], axis=0)
    small, small_heads, small_lanes = _sum_small(_gather_small(pack))
    loss = small_lanes[ROW_LOSS, 0]
    d_model = norm_mix.shape[1]
    conv_cols = conv_w.shape[2]
    conv_all = small[ROW_CONV : ROW_CONV + 3 * n_layers, 0:CONV_DIM].reshape(n_layers, 3, CONV_DIM)
    small_grads = dict(
        norm_mix=small[ROW_NORM_MIX : ROW_NORM_MIX + n_layers, 0:d_model],
        norm_ffn=small[ROW_NORM_FFN : ROW_NORM_FFN + n_layers, 0:d_model],
        q_norm=small_heads[ROW_Q_NORM - 8 : ROW_Q_NORM - 8 + n_layers, 0:HEAD_DIM],
        k_norm=small_heads[ROW_K_NORM - 8 : ROW_K_NORM - 8 + n_layers, 0:HEAD_DIM],
        conv_w=lax.dynamic_slice_in_dim(conv_all, (2 * cx + cy) * conv_cols, conv_cols, axis=2),
    )

    out_grad, out_delta, out_m, out_v = {}, {}, {}, {}
    for k in BIG:
        shape = weights[k].shape
        view = (shape[0] * shape[1], shape[2])
        g = big_grads[k]
        delta, new_m, new_v = _adamw(weights[k].reshape(view), g.reshape(view), moments_m[k].reshape(view), moments_v[k].reshape(view), f"adamw_{k}")
        out_grad[k], out_delta[k], out_m[k], out_v[k] = g, delta.reshape(shape), new_m.reshape(shape), new_v.reshape(shape)

    small_w = dict(norm_mix=norm_mix, norm_ffn=norm_ffn, q_norm=q_norm, k_norm=k_norm, conv_w=conv_w)
    small_m = dict(norm_mix=m_norm_mix, norm_ffn=m_norm_ffn, q_norm=m_q_norm, k_norm=m_k_norm, conv_w=m_conv_w)
    small_v = dict(norm_mix=v_norm_mix, norm_ffn=v_norm_ffn, q_norm=v_q_norm, k_norm=v_k_norm, conv_w=v_conv_w)
    order = ("norm_mix", "norm_ffn", "q_norm", "k_norm", "conv_w")

    def packed(tree):
        parts2 = [_pad_to(tree[k].reshape(-1, tree[k].shape[-1]), tree[k].reshape(-1, tree[k].shape[-1]).shape[0], SMALL_COLS) for k in order]
        return _pad_rows(jnp.concatenate(parts2, axis=0), SMALL_ROWS)

    delta_p, m_p, v_p = _adamw(packed(small_w), packed(small_grads), packed(small_m), packed(small_v), "adamw_small")
    row = 0
    for k in order:
        shape = small_w[k].shape
        n_rows = 1
        for dim in shape[:-1]:
            n_rows *= dim
        cut = (slice(row, row + n_rows), slice(0, shape[-1]))
        out_grad[k] = small_grads[k]
        out_delta[k], out_m[k], out_v[k] = delta_p[cut].reshape(shape), m_p[cut].reshape(shape), v_p[cut].reshape(shape)
        row += n_rows

    names_out = ("norm_mix", "w_in", "q_norm", "k_norm", "conv_w", "w_out", "norm_ffn", "w_gate", "w_up", "w_down")
    return (loss, grad_x[None], *[out_grad[k] for k in names_out], *[out_delta[k] for k in names_out],
            *[out_m[k] for k in names_out], *[out_v[k] for k in names_out])
```

```python
import functools

import jax
import jax.numpy as jnp
from jax import lax
from jax.experimental import pallas as pl
from jax.experimental.pallas import tpu as pltpu

F32 = jnp.float32
BF16 = jnp.bfloat16

EPS = 1e-6
HEAD_DIM = 64
LANES = 128
ATTN_DIM = 512
CONV_DIM = 512
N_CHIPS = 4
N_DEV = 8
Q_SCALE = HEAD_DIM ** -0.5
ATTN_TILE = 256
TOKEN_TILE = 512
VMEM_LIMIT = 56 * 1024 * 1024

ADAM_LR = 0.001
ADAM_B1 = 0.9
ADAM_B2 = 0.999
ADAM_EPS = 1e-08
ADAM_WD = 0.01
ADAM_STEP = 10

MESH_ID = pl.DeviceIdType.MESH
ANY = pl.BlockSpec(memory_space=pl.ANY)
VMEM_SPEC = pl.BlockSpec(memory_space=pltpu.VMEM)


def _params(n_axes):
    return pltpu.CompilerParams(dimension_semantics=("arbitrary",) * n_axes, vmem_limit_bytes=VMEM_LIMIT)


def _dot(a, b):
    return jnp.dot(a, b, preferred_element_type=F32)


def _dot_nt(a, b):
    return lax.dot_general(a, b, (((1,), (1,)), ((), ())), preferred_element_type=F32)


def _dot_tn(a, b):
    return lax.dot_general(a, b, (((0,), (0,)), ((), ())), preferred_element_type=F32)


def _split_bf16(v):
    hi = v.astype(BF16)
    lo = (v - hi.astype(F32)).astype(BF16)
    return hi, lo


def _softplus(z):
    return jnp.maximum(z, 0.0) + jnp.log(1.0 + jnp.exp(-jnp.abs(z)))


def _norm_matmul(x, gain, w_s, layer, name):
    s, d = x.shape
    n_blocks, _, _, n = w_s.shape
    tm = TOKEN_TILE

    def body(x_ref, g_ref, w_ref, h_ref, o_ref):
        @pl.when(pl.program_id(1) == 0)
        def _():
            xv = x_ref[...]
            r = lax.rsqrt(jnp.mean(xv * xv, axis=-1, keepdims=True) + EPS)
            h_ref[...] = (xv * r * g_ref[...]).astype(BF16)

        o_ref[...] = _dot(h_ref[...], w_ref[0, 0])

    return pl.pallas_call(
        body,
        name=name,
        grid=(s // tm, n_blocks),
        in_specs=[
            pl.BlockSpec((tm, d), lambda i, j: (i, 0)),
            pl.BlockSpec((1, d), lambda i, j: (0, 0)),
            pl.BlockSpec((1, 1, d, n), lambda i, j: (j, layer, 0, 0)),
        ],
        out_specs=[pl.BlockSpec((tm, d), lambda i, j: (i, 0)), pl.BlockSpec((tm, n), lambda i, j: (i, j))],
        out_shape=[jax.ShapeDtypeStruct((s, d), BF16), jax.ShapeDtypeStruct((s, n_blocks * n), F32)],
        compiler_params=_params(2),
    )(x, gain, w_s)


def _head_norm(xv, gain, low):
    sq = xv * xv
    s_low = jnp.sum(jnp.where(low, sq, 0.0), axis=-1, keepdims=True)
    s_high = jnp.sum(jnp.where(low, 0.0, sq), axis=-1, keepdims=True)
    r = jnp.where(low, lax.rsqrt(s_low / HEAD_DIM + EPS), lax.rsqrt(s_high / HEAD_DIM + EPS))
    return xv * r * gain, r


def _qkv_prep(proj, q_gain, k_gain, name):
    s = proj.shape[0]
    tm = TOKEN_TILE

    def body(p_ref, qg_ref, kg_ref, q_ref, k_ref, v_ref):
        low = lax.broadcasted_iota(jnp.int32, (tm, LANES), 1) < HEAD_DIM
        for g in range(ATTN_DIM // LANES):
            cq = slice(LANES * g, LANES * (g + 1))
            ck = slice(ATTN_DIM + LANES * g, ATTN_DIM + LANES * (g + 1))
            cv = slice(2 * ATTN_DIM + LANES * g, 2 * ATTN_DIM + LANES * (g + 1))
            qn, _ = _head_norm(p_ref[:, cq], qg_ref[...], low)
            kn, _ = _head_norm(p_ref[:, ck], kg_ref[...], low)
            q_ref[:, cq] = (qn * Q_SCALE).astype(BF16)
            k_ref[:, cq] = kn.astype(BF16)
            v_ref[:, cq] = p_ref[:, cv].astype(BF16)

    out = jax.ShapeDtypeStruct((s, ATTN_DIM), BF16)
    return pl.pallas_call(
        body,
        name=name,
        grid=(s // tm,),
        in_specs=[
            pl.BlockSpec((tm, 3 * ATTN_DIM), lambda i: (i, 0)),
            pl.BlockSpec((1, LANES), lambda i: (0, 0)),
            pl.BlockSpec((1, LANES), lambda i: (0, 0)),
        ],
        out_specs=[pl.BlockSpec((tm, ATTN_DIM), lambda i: (i, 0))] * 3,
        out_shape=[out, out, out],
        compiler_params=_params(1),
    )(proj, q_gain, k_gain)


def _qkv_prep_bwd(proj, q_gain, k_gain, dq, dk, dv, name):
    s = proj.shape[0]
    tm = TOKEN_TILE

    def norm_bwd(xv, gain, dy, low):
        _, r = _head_norm(xv, gain, low)
        xhat = xv * r
        dxhat = dy * gain
        prod = dxhat * xhat
        m_low = jnp.sum(jnp.where(low, prod, 0.0), axis=-1, keepdims=True)
        m_high = jnp.sum(jnp.where(low, 0.0, prod), axis=-1, keepdims=True)
        mean = jnp.where(low, m_low, m_high) / HEAD_DIM
        return r * (dxhat - xhat * mean), jnp.sum(dy * xhat, axis=0, keepdims=True)

    def body(p_ref, qg_ref, kg_ref, dq_ref, dk_ref, dv_ref, dp_ref, dqg_ref, dkg_ref):
        @pl.when(pl.program_id(0) == 0)
        def _():
            dqg_ref[...] = jnp.zeros_like(dqg_ref)
            dkg_ref[...] = jnp.zeros_like(dkg_ref)

        low = lax.broadcasted_iota(jnp.int32, (tm, LANES), 1) < HEAD_DIM
        for g in range(ATTN_DIM // LANES):
            cq = slice(LANES * g, LANES * (g + 1))
            ck = slice(ATTN_DIM + LANES * g, ATTN_DIM + LANES * (g + 1))
            cv = slice(2 * ATTN_DIM + LANES * g, 2 * ATTN_DIM + LANES * (g + 1))
            dxq, dgq = norm_bwd(p_ref[:, cq], qg_ref[...], dq_ref[:, cq] * Q_SCALE, low)
            dxk, dgk = norm_bwd(p_ref[:, ck], kg_ref[...], dk_ref[:, cq], low)
            dp_ref[:, cq] = dxq.astype(BF16)
            dp_ref[:, ck] = dxk.astype(BF16)
            dp_ref[:, cv] = dv_ref[:, cq].astype(BF16)
            dqg_ref[:, cq] += dgq
            dkg_ref[:, cq] += dgk

    grad_spec = pl.BlockSpec((tm, ATTN_DIM), lambda i: (i, 0))
    gain_spec = pl.BlockSpec((1, LANES), lambda i: (0, 0))
    sum_spec = pl.BlockSpec((1, ATTN_DIM), lambda i: (0, 0))
    return pl.pallas_call(
        body,
        name=name,
        grid=(s // tm,),
        in_specs=[pl.BlockSpec((tm, 3 * ATTN_DIM), lambda i: (i, 0)), gain_spec, gain_spec, grad_spec, grad_spec, grad_spec],
        out_specs=[pl.BlockSpec((tm, 3 * ATTN_DIM), lambda i: (i, 0)), sum_spec, sum_spec],
        out_shape=[
            jax.ShapeDtypeStruct((s, 3 * ATTN_DIM), BF16),
            jax.ShapeDtypeStruct((1, ATTN_DIM), F32),
            jax.ShapeDtypeStruct((1, ATTN_DIM), F32),
        ],
        compiler_params=_params(1),
    )(proj, q_gain, k_gain, dq, dk, dv)


def _attn_tile_consts(t):
    row = lax.broadcasted_iota(jnp.int32, (t, t), 0)
    col = lax.broadcasted_iota(jnp.int32, (t, t), 1)
    return row, col


def _triangle_sum(v, triangle_twice):
    hi, lo = _split_bf16(v)
    return _dot(jnp.concatenate([hi, lo], axis=1), triangle_twice)


def _attn_fwd(qn, kn, vb, name):
    s = qn.shape[0]
    t = min(ATTN_TILE, s)

    def body(q_ref, k_ref, v_ref, o_ref):
        i = pl.program_id(1)
        low = lax.broadcasted_iota(jnp.int32, (t, LANES), 1) < HEAD_DIM
        row, col = _attn_tile_consts(t)
        suffix = (row > col).astype(BF16)
        suffix2 = jnp.concatenate([suffix, suffix], axis=0)
        causal = col < row
        q = q_ref[...]
        zero_q = jnp.zeros_like(q)
        qh = (jnp.where(low, q, zero_q), jnp.where(low, zero_q, q))

        def step(kbs, carry, diagonal=False):
            chains = [(head, m) for head in range(2) for m in range(len(kbs))]
            ks = [k_ref[pl.ds(pl.multiple_of(kb * t, t), t), :] for kb in kbs]
            vs = [v_ref[pl.ds(pl.multiple_of(kb * t, t), t), :] for kb in kbs]
            z = [_dot_nt(qh[head], ks[kb]) for head, kb in chains]
            sp = [_softplus(zc) for zc in z]
            if diagonal:
                sp = [jnp.where(causal, s_, 0.0) for s_ in sp]
            inside = [_triangle_sum(s_, suffix2) for s_ in sp]
            after = [carry[head][1] for head in range(2)]
            log_a = []
            for n, (head, kb) in enumerate(chains):
                log_a.append(z[n] - sp[n] - inside[n] - after[head])
                after[head] = after[head] + jnp.sum(sp[n], axis=-1, keepdims=True)
            a = [jnp.exp(l_) for l_ in log_a]
            if diagonal:
                a = [jnp.where(causal, a_, 0.0) for a_ in a]
            acc = [carry[head][0] for head in range(2)]
            for n, (head, kb) in enumerate(chains):
                acc[head] = acc[head] + _dot(a[n].astype(BF16), vs[kb])
            return tuple((acc[head], after[head]) for head in range(2))

        zero = (jnp.zeros((t, LANES), F32), jnp.zeros((t, 1), F32))
        carry = step((i,), (zero, zero), True)
        carry = lax.cond(i % 2 == 1, lambda c: step((i - 1,), c), lambda c: c, carry)
        pairs = i // 2
        carry = lax.fori_loop(0, pairs, lambda n, c: step((2 * (pairs - n) - 1, 2 * (pairs - n) - 2), c), carry)
        o_ref[...] = jnp.where(low, carry[0][0], carry[1][0]).astype(BF16)

    return pl.pallas_call(
        body,
        name=name,
        grid=(ATTN_DIM // LANES, s // t),
        in_specs=[
            pl.BlockSpec((t, LANES), lambda p, i: (i, p)),
            pl.BlockSpec((s, LANES), lambda p, i: (0, p)),
            pl.BlockSpec((s, LANES), lambda p, i: (0, p)),
        ],
        out_specs=pl.BlockSpec((t, LANES), lambda p, i: (i, p)),
        out_shape=jax.ShapeDtypeStruct((s, ATTN_DIM), BF16),
        compiler_params=_params(2),
    )(qn, kn, vb)


def _attn_bwd(qn, kn, vb, do, name):
    s = qn.shape[0]
    t = min(ATTN_TILE, s)
    nq = s // t

    def body(q_ref, k_ref, v_ref, do_ref, dq_ref, dk_ref, dv_ref, a_s, sg_s):
        i = pl.program_id(1)

        @pl.when(i == 0)
        def _():
            dk_ref[...] = jnp.zeros_like(dk_ref)
            dv_ref[...] = jnp.zeros_like(dv_ref)

        low = lax.broadcasted_iota(jnp.int32, (t, LANES), 1) < HEAD_DIM
        row, col = _attn_tile_consts(t)
        suffix = (row > col).astype(BF16)
        suffix2 = jnp.concatenate([suffix, suffix], axis=0)
        prefix = (row < col).astype(BF16)
        prefix2 = jnp.concatenate([prefix, prefix], axis=0)
        causal = col < row
        q = q_ref[...]
        dob = do_ref[...]
        zero_q = jnp.zeros_like(q)
        pairs = i // 2
        heads = []
        for head in range(2):
            if head == 0:
                qh, doh = jnp.where(low, q, zero_q), jnp.where(low, dob, zero_q)
            else:
                qh, doh = jnp.where(low, zero_q, q), jnp.where(low, zero_q, dob)

            def rows_of(kb):
                return pl.ds(pl.multiple_of(kb * t, t), t)

            def pass1(kbs, after, diagonal=False):
                z = [_dot_nt(qh, k_ref[rows_of(kb), :]) for kb in kbs]
                sp = [_softplus(z_) for z_ in z]
                if diagonal:
                    sp = [jnp.where(causal, s_, 0.0) for s_ in sp]
                inside = [_triangle_sum(s_, suffix2) for s_ in sp]
                for n, kb in enumerate(kbs):
                    log_sg = z[n] - sp[n]
                    a = jnp.exp(log_sg - inside[n] - after)
                    sg = jnp.exp(log_sg)
                    if diagonal:
                        a = jnp.where(causal, a, 0.0)
                        sg = jnp.where(causal, sg, 0.0)
                    a_s[kb] = a
                    sg_s[kb] = sg
                    after = after + jnp.sum(sp[n], axis=-1, keepdims=True)
                return after

            after = pass1((i,), jnp.zeros((t, 1), F32), True)
            after = lax.cond(i % 2 == 1, lambda c: pass1((i - 1,), c), lambda c: c, after)
            lax.fori_loop(0, pairs, lambda n, c: pass1((2 * (pairs - n) - 1, 2 * (pairs - n) - 2), c), after)

            def pass2(kbs, carry):
                dq, before = carry
                ks = [k_ref[rows_of(kb), :] for kb in kbs]
                a = [a_s[kb] for kb in kbs]
                g = [a_ * _dot_nt(doh, v_ref[rows_of(kb), :]) for a_, kb in zip(a, kbs)]
                inside = [_triangle_sum(g_, prefix2) for g_ in g]
                dz = []
                for n, kb in enumerate(kbs):
                    sg = sg_s[kb]
                    dz.append((g[n] * (1.0 - sg) - sg * (inside[n] + before)).astype(BF16))
                    before = before + jnp.sum(g[n], axis=-1, keepdims=True)
                for n, kb in enumerate(kbs):
                    dk_ref[rows_of(kb), :] += _dot_tn(dz[n], qh)
                for n, kb in enumerate(kbs):
                    dv_ref[rows_of(kb), :] += _dot_tn(a[n].astype(BF16), doh)
                for n in range(len(kbs)):
                    dq = dq + _dot(dz[n], ks[n])
                return dq, before

            carry = (jnp.zeros((t, LANES), F32), jnp.zeros((t, 1), F32))
            carry = lax.fori_loop(0, (i + 1) // 2, lambda n, c: pass2((2 * n, 2 * n + 1), c), carry)
            carry = lax.cond(i % 2 == 0, lambda c: pass2((i,), c), lambda c: c, carry)
            heads.append(carry[0])
        dq_ref[...] = jnp.where(low, heads[0], heads[1])

    q_spec = pl.BlockSpec((t, LANES), lambda p, i: (i, p))
    kv_spec = pl.BlockSpec((s, LANES), lambda p, i: (0, p))
    return pl.pallas_call(
        body,
        name=name,
        grid=(ATTN_DIM // LANES, nq),
        in_specs=[q_spec, kv_spec, kv_spec, q_spec],
        out_specs=[q_spec, kv_spec, kv_spec],
        out_shape=[jax.ShapeDtypeStruct((s, ATTN_DIM), F32)] * 3,
        scratch_shapes=[pltpu.VMEM((nq, t, t), F32), pltpu.VMEM((nq, t, t), F32)],
        compiler_params=_params(2),
    )(qn, kn, vb, do)


CB_BLOCK, CC_BLOCK, CU_BLOCK = 3, 4, 5


def _shift_down(h, prev_rows, n):
    row = lax.broadcasted_iota(jnp.int32, h.shape, 0)
    out = pltpu.roll(h, n, 0)
    for r in range(n):
        out = jnp.where(row == r, prev_rows[len(prev_rows) - n + r], out)
    return out


def _shift_up(h, next_rows, n):
    tm = h.shape[0]
    row = lax.broadcasted_iota(jnp.int32, h.shape, 0)
    out = pltpu.roll(h, tm - n, 0)
    for r in range(n):
        out = jnp.where(row == tm - n + r, next_rows[r], out)
    return out


def _conv_fwd(proj, conv_w, name):
    s = proj.shape[0]
    tm = TOKEN_TILE
    nb = tm // 8

    def body(cb_ref, cc_ref, cu_ref, pc_ref, pu_ref, w_ref, o_ref):
        first = pl.program_id(0) == 0
        h = cc_ref[...] * cu_ref[...]
        prev = [jnp.where(first, 0.0, pc_ref[r : r + 1, :] * pu_ref[r : r + 1, :]) for r in (6, 7)]
        y = w_ref[0:1, :] * _shift_down(h, prev, 2) + w_ref[1:2, :] * _shift_down(h, prev, 1) + w_ref[2:3, :] * h
        o_ref[...] = (cb_ref[...] * y).astype(BF16)

    def col(block):
        return pl.BlockSpec((tm, CONV_DIM), lambda i: (i, block))

    def halo(block):
        return pl.BlockSpec((8, CONV_DIM), lambda i: (jnp.maximum(i * nb - 1, 0), block))

    return pl.pallas_call(
        body,
        name=name,
        grid=(s // tm,),
        in_specs=[col(CB_BLOCK), col(CC_BLOCK), col(CU_BLOCK), halo(CC_BLOCK), halo(CU_BLOCK), pl.BlockSpec((8, CONV_DIM), lambda i: (0, 0))],
        out_specs=pl.BlockSpec((tm, CONV_DIM), lambda i: (i, 0)),
        out_shape=jax.ShapeDtypeStruct((s, CONV_DIM), BF16),
        compiler_params=_params(1),
    )(proj, proj, proj, proj, proj, conv_w)


def _conv_bwd(proj, conv_w, dconv, name):
    s = proj.shape[0]
    tm = TOKEN_TILE
    nb = tm // 8
    n_tiles = s // tm

    def body(cb_ref, cc_ref, cu_ref, dy_ref, pc_ref, pu_ref, nb_ref, ndy_ref, w_ref, dp_ref, dw_ref):
        i = pl.program_id(0)

        @pl.when(i == 0)
        def _():
            dw_ref[...] = jnp.zeros_like(dw_ref)

        first = i == 0
        last = i == n_tiles - 1
        cc, cu, cb, dy = cc_ref[...], cu_ref[...], cb_ref[...], dy_ref[...]
        h = cc * cu
        prev = [jnp.where(first, 0.0, pc_ref[r : r + 1, :] * pu_ref[r : r + 1, :]) for r in (6, 7)]
        h1 = _shift_down(h, prev, 1)
        h2 = _shift_down(h, prev, 2)
        y = w_ref[0:1, :] * h2 + w_ref[1:2, :] * h1 + w_ref[2:3, :] * h
        dyb = dy * cb
        nxt = [jnp.where(last, 0.0, ndy_ref[r : r + 1, :] * nb_ref[r : r + 1, :]) for r in (0, 1)]
        dh = w_ref[2:3, :] * dyb + w_ref[1:2, :] * _shift_up(dyb, nxt, 1) + w_ref[0:1, :] * _shift_up(dyb, nxt, 2)
        dp_ref[:, 0:CONV_DIM] = (dy * y).astype(BF16)
        dp_ref[:, CONV_DIM : 2 * CONV_DIM] = (dh * cu).astype(BF16)
        dp_ref[:, 2 * CONV_DIM : 3 * CONV_DIM] = (dh * cc).astype(BF16)
        dw_ref[0:1, :] += jnp.sum(dyb * h2, axis=0, keepdims=True)
        dw_ref[1:2, :] += jnp.sum(dyb * h1, axis=0, keepdims=True)
        dw_ref[2:3, :] += jnp.sum(dyb * h, axis=0, keepdims=True)

    def col(block):
        return pl.BlockSpec((tm, CONV_DIM), lambda i: (i, block))

    def halo_prev(block):
        return pl.BlockSpec((8, CONV_DIM), lambda i: (jnp.maximum(i * nb - 1, 0), block))

    def halo_next(block):
        return pl.BlockSpec((8, CONV_DIM), lambda i: (jnp.minimum((i + 1) * nb, s // 8 - 1), block))

    return pl.pallas_call(
        body,
        name=name,
        grid=(n_tiles,),
        in_specs=[
            col(CB_BLOCK), col(CC_BLOCK), col(CU_BLOCK), col(0),
            halo_prev(CC_BLOCK), halo_prev(CU_BLOCK), halo_next(CB_BLOCK), halo_next(0),
            pl.BlockSpec((8, CONV_DIM), lambda i: (0, 0)),
        ],
        out_specs=[pl.BlockSpec((tm, 3 * CONV_DIM), lambda i: (i, 0)), pl.BlockSpec((8, CONV_DIM), lambda i: (0, 0))],
        out_shape=[jax.ShapeDtypeStruct((s, 3 * CONV_DIM), BF16), jax.ShapeDtypeStruct((8, CONV_DIM), F32)],
        compiler_params=_params(1),
    )(proj, proj, proj, dconv, proj, proj, proj, dconv, conv_w)


def _out_proj(x, attn, conv, w_s, layer, name):
    s, d = x.shape
    tm = TOKEN_TILE
    rows = w_s.shape[2]

    def body(x_ref, a_ref, c_ref, w_ref, o_ref):
        acc = x_ref[...]
        for j in range(N_CHIPS):
            src = a_ref if j < 2 else c_ref
            cols = slice((j % 2) * rows, (j % 2 + 1) * rows)
            acc = acc + _dot(src[:, cols], w_ref[j, 0])
        o_ref[...] = acc

    return pl.pallas_call(
        body,
        name=name,
        grid=(s // tm,),
        in_specs=[
            pl.BlockSpec((tm, d), lambda i: (i, 0)),
            pl.BlockSpec((tm, ATTN_DIM), lambda i: (i, 0)),
            pl.BlockSpec((tm, CONV_DIM), lambda i: (i, 0)),
            pl.BlockSpec((N_CHIPS, 1, rows, d), lambda i: (0, layer, 0, 0)),
        ],
        out_specs=pl.BlockSpec((tm, d), lambda i: (i, 0)),
        out_shape=jax.ShapeDtypeStruct((s, d), F32),
        compiler_params=_params(1),
    )(x, attn, conv, w_s)


def _out_proj_bwd(dx, w_s, layer, name):
    s, d = dx.shape
    tm = TOKEN_TILE
    rows = w_s.shape[2]

    def body(dx_ref, w_ref, da_ref, dc_ref, dxb_ref):
        dxb = dx_ref[...].astype(BF16)
        dxb_ref[...] = dxb
        for j in range(N_CHIPS):
            cols = slice((j % 2) * rows, (j % 2 + 1) * rows)
            part = _dot_nt(dxb, w_ref[j, 0])
            if j < 2:
                da_ref[:, cols] = part.astype(BF16)
            else:
                dc_ref[:, cols] = part

    return pl.pallas_call(
        body,
        name=name,
        grid=(s // tm,),
        in_specs=[pl.BlockSpec((tm, d), lambda i: (i, 0)), pl.BlockSpec((N_CHIPS, 1, rows, d), lambda i: (0, layer, 0, 0))],
        out_specs=[
            pl.BlockSpec((tm, ATTN_DIM), lambda i: (i, 0)),
            pl.BlockSpec((tm, CONV_DIM), lambda i: (i, 0)),
            pl.BlockSpec((tm, d), lambda i: (i, 0)),
        ],
        out_shape=[
            jax.ShapeDtypeStruct((s, ATTN_DIM), BF16),
            jax.ShapeDtypeStruct((s, CONV_DIM), F32),
            jax.ShapeDtypeStruct((s, d), BF16),
        ],
        compiler_params=_params(1),
    )(dx, w_s)


def _ffn_fwd(x, gain, wg_s, wu_s, wd_s, layer, name):
    s, d = x.shape
    tm = TOKEN_TILE
    f = wg_s.shape[3]

    def body(x_ref, g_ref, wg_ref, wu_ref, wd_ref, o_ref, h_s):
        j = pl.program_id(1)

        @pl.when(j == 0)
        def _():
            xv = x_ref[...]
            r = lax.rsqrt(jnp.mean(xv * xv, axis=-1, keepdims=True) + EPS)
            h_s[...] = (xv * r * g_ref[...]).astype(BF16)
            o_ref[...] = xv

        h = h_s[...]
        gate = _dot(h, wg_ref[0, 0])
        up = _dot(h, wu_ref[0, 0])
        act = (gate / (1.0 + jnp.exp(-gate))) * up
        o_ref[...] += _dot(act.astype(BF16), wd_ref[0, 0])

    return pl.pallas_call(
        body,
        name=name,
        grid=(s // tm, N_CHIPS),
        in_specs=[
            pl.BlockSpec((tm, d), lambda i, j: (i, 0)),
            pl.BlockSpec((1, d), lambda i, j: (0, 0)),
            pl.BlockSpec((1, 1, d, f), lambda i, j: (j, layer, 0, 0)),
            pl.BlockSpec((1, 1, d, f), lambda i, j: (j, layer, 0, 0)),
            pl.BlockSpec((1, 1, f, d), lambda i, j: (j, layer, 0, 0)),
        ],
        out_specs=pl.BlockSpec((tm, d), lambda i, j: (i, 0)),
        out_shape=jax.ShapeDtypeStruct((s, d), F32),
        scratch_shapes=[pltpu.VMEM((tm, d), BF16)],
        compiler_params=_params(2),
    )(x, gain, wg_s, wu_s, wd_s)


def _rms_bwd(xv, gain, dh):
    r = lax.rsqrt(jnp.mean(xv * xv, axis=-1, keepdims=True) + EPS)
    xhat = xv * r
    dxhat = dh * gain
    dx = r * (dxhat - xhat * jnp.mean(dxhat * xhat, axis=-1, keepdims=True))
    return dx, jnp.sum(dh * xhat, axis=0, keepdims=True)


def _ffn_bwd(x, dy, gain, wg_s, wu_s, wd_s, layer, name):
    s, d = x.shape
    tm = TOKEN_TILE
    f = wg_s.shape[3]

    def body(x_ref, dy_ref, g_ref, wg_ref, wu_ref, wd_ref, dx_ref, dgain_ref, h_ref, dyb_ref, dg_ref, du_ref, act_ref, acc_s):
        i, j = pl.program_id(0), pl.program_id(1)

        @pl.when((i == 0) & (j == 0))
        def _():
            dgain_ref[...] = jnp.zeros_like(dgain_ref)

        @pl.when(j == 0)
        def _():
            xv = x_ref[...]
            r = lax.rsqrt(jnp.mean(xv * xv, axis=-1, keepdims=True) + EPS)
            h_ref[...] = (xv * r * g_ref[...]).astype(BF16)
            dyb_ref[...] = dy_ref[...].astype(BF16)
            acc_s[...] = jnp.zeros_like(acc_s)

        h = h_ref[...]
        gate = _dot(h, wg_ref[0, 0])
        up = _dot(h, wu_ref[0, 0])
        sig = 1.0 / (1.0 + jnp.exp(-gate))
        silu = gate * sig
        dact = _dot_nt(dyb_ref[...], wd_ref[0, 0])
        dgate = (dact * up * (sig * (1.0 + gate * (1.0 - sig)))).astype(BF16)
        dup = (dact * silu).astype(BF16)
        act_ref[0] = (silu * up).astype(BF16)
        dg_ref[0] = dgate
        du_ref[0] = dup
        acc_s[...] += _dot_nt(dgate, wg_ref[0, 0]) + _dot_nt(dup, wu_ref[0, 0])

        @pl.when(j == N_CHIPS - 1)
        def _():
            dxn, dgain = _rms_bwd(x_ref[...], g_ref[...], acc_s[...])
            dx_ref[...] = dy_ref[...] + dxn
            dgain_ref[...] += dgain

    tok = pl.BlockSpec((tm, d), lambda i, j: (i, 0))
    vec = pl.BlockSpec((1, d), lambda i, j: (0, 0))
    hid = pl.BlockSpec((1, tm, f), lambda i, j: (j, i, 0))
    hid_shape = jax.ShapeDtypeStruct((N_CHIPS, s, f), BF16)
    return pl.pallas_call(
        body,
        name=name,
        grid=(s // tm, N_CHIPS),
        in_specs=[
            tok, tok, vec,
            pl.BlockSpec((1, 1, d, f), lambda i, j: (j, layer, 0, 0)),
            pl.BlockSpec((1, 1, d, f), lambda i, j: (j, layer, 0, 0)),
            pl.BlockSpec((1, 1, f, d), lambda i, j: (j, layer, 0, 0)),
        ],
        out_specs=[tok, vec, tok, tok, hid, hid, hid],
        out_shape=[
            jax.ShapeDtypeStruct((s, d), F32),
            jax.ShapeDtypeStruct((1, d), F32),
            jax.ShapeDtypeStruct((s, d), BF16),
            jax.ShapeDtypeStruct((s, d), BF16),
            hid_shape, hid_shape, hid_shape,
        ],
        scratch_shapes=[pltpu.VMEM((tm, d), F32)],
        compiler_params=_params(2),
    )(x, dy, gain, wg_s, wu_s, wd_s)


def _in_proj_bwd(x, dx_res, gain, dproj, w_s, layer, name):
    s, d = x.shape
    tm = TOKEN_TILE
    n = w_s.shape[3]

    def body(x_ref, r_ref, g_ref, dp_ref, w_ref, dx_ref, dgain_ref, acc_s):
        i, j = pl.program_id(0), pl.program_id(1)

        @pl.when((i == 0) & (j == 0))
        def _():
            dgain_ref[...] = jnp.zeros_like(dgain_ref)

        @pl.when(j == 0)
        def _():
            acc_s[...] = jnp.zeros_like(acc_s)

        acc_s[...] += _dot_nt(dp_ref[...], w_ref[0, 0])

        @pl.when(j == N_CHIPS - 1)
        def _():
            dxn, dgain = _rms_bwd(x_ref[...], g_ref[...], acc_s[...])
            dx_ref[...] = r_ref[...] + dxn
            dgain_ref[...] += dgain

    tok = pl.BlockSpec((tm, d), lambda i, j: (i, 0))
    vec = pl.BlockSpec((1, d), lambda i, j: (0, 0))
    return pl.pallas_call(
        body,
        name=name,
        grid=(s // tm, N_CHIPS),
        in_specs=[tok, tok, vec, pl.BlockSpec((tm, n), lambda i, j: (i, j)), pl.BlockSpec((1, 1, d, n), lambda i, j: (j, layer, 0, 0))],
        out_specs=[tok, vec],
        out_shape=[jax.ShapeDtypeStruct((s, d), F32), jax.ShapeDtypeStruct((1, d), F32)],
        scratch_shapes=[pltpu.VMEM((tm, d), F32)],
        compiler_params=_params(2),
    )(x, dx_res, gain, dproj, w_s)


def _loss_grad(y, target, name):
    s, d = y.shape
    tm = TOKEN_TILE

    def body(y_ref, t_ref, dy_ref, l_ref):
        @pl.when(pl.program_id(0) == 0)
        def _():
            l_ref[...] = jnp.zeros_like(l_ref)

        err = y_ref[...] - t_ref[...]
        dy_ref[...] = err / d
        l_ref[...] += jnp.sum(err * err, axis=0, keepdims=True) * (0.5 / d)

    tok = pl.BlockSpec((tm, d), lambda i: (i, 0))
    return pl.pallas_call(
        body,
        name=name,
        grid=(s // tm,),
        in_specs=[tok, tok],
        out_specs=[tok, pl.BlockSpec((1, d), lambda i: (0, 0))],
        out_shape=[jax.ShapeDtypeStruct((s, d), F32), jax.ShapeDtypeStruct((1, d), F32)],
        compiler_params=_params(1),
    )(y, target)


def _wgrad(a, b, a_spec, b_spec, n_blocks, k, n, name):
    n_tiles = (a.shape[-2]) // TOKEN_TILE

    def body(a_ref, b_ref, o_ref):
        @pl.when(pl.program_id(1) == 0)
        def _():
            o_ref[...] = jnp.zeros_like(o_ref)

        av = a_ref[0] if len(a_ref.shape) == 3 else a_ref[...]
        bv = b_ref[0] if len(b_ref.shape) == 3 else b_ref[...]
        o_ref[0] += _dot_tn(av, bv)

    return pl.pallas_call(
        body,
        name=name,
        grid=(n_blocks, n_tiles),
        in_specs=[a_spec, b_spec],
        out_specs=pl.BlockSpec((1, k, n), lambda j, i: (j, 0, 0)),
        out_shape=jax.ShapeDtypeStruct((n_blocks, k, n), F32),
        compiler_params=_params(2),
    )(a, b)


def _mesh_position():
    return lax.axis_index("x"), lax.axis_index("y"), lax.axis_index("c")


def _other_chips(x, y):
    return [(1 - x, y), (x, 1 - y), (1 - x, 1 - y)]


def _half_rows(ref_rows, c):
    half = ref_rows // 2
    return pl.ds(c * half, half)


def _gather_weights(shards):
    n = len(shards)

    def body(*refs):
        ins, outs = refs[:n], refs[n : 2 * n]
        send_sems, recv_sems, pass_send_sems, pass_recv_sems, local_sems = refs[2 * n :]
        x, y, c = _mesh_position()
        me = 2 * x + y
        sibling = (x, y, 1 - c)
        chips = _other_chips(x, y)

        def block(t, chip_index, core):
            return outs[t].at[chip_index, :, _half_rows(ins[t].shape[1], core), :]

        def copy(t, k, chip_index, core, to, sems, src=None):
            dst = block(t, chip_index, core)
            return pltpu.make_async_remote_copy(
                src_ref=dst if src is None else src, dst_ref=dst, send_sem=sems[0].at[t, k], recv_sem=sems[1].at[t, k],
                device_id=to, device_id_type=MESH_ID,
            )

        ici, d2d = (send_sems, recv_sems), (pass_send_sems, pass_recv_sems)
        own = [pltpu.make_async_copy(ins[t], outs[t].at[me], local_sems.at[t]) for t in range(n)]
        for cp in own:
            cp.start()
        started = []
        for t in range(n):
            mine = ins[t].at[:, _half_rows(ins[t].shape[1], c), :]
            for k, (px, py) in enumerate(chips):
                started.append(copy(t, k, me, c, (px, py, c), ici, src=mine))
                started[-1].start()
        for t in range(n):
            for k, (px, py) in enumerate(chips):
                copy(t, k, 2 * px + py, c, sibling, ici).wait_recv()
                started.append(copy(t, k, 2 * px + py, c, sibling, d2d))
                started[-1].start()
        for t in range(n):
            for k, (px, py) in enumerate(chips):
                copy(t, k, 2 * px + py, 1 - c, sibling, d2d).wait_recv()
        for cp in started:
            cp.wait_send()
        for cp in own:
            cp.wait()

    sems = pltpu.SemaphoreType.DMA((n, N_CHIPS - 1))
    return pl.pallas_call(
        body,
        name="gather_weights",
        in_specs=[ANY] * n,
        out_specs=[ANY] * n,
        out_shape=[jax.ShapeDtypeStruct((N_CHIPS,) + w.shape, w.dtype) for w in shards],
        scratch_shapes=[sems, sems, sems, sems, pltpu.SemaphoreType.DMA((n,))],
    )(*shards)


def _swap_halves(grads):
    n = len(grads)

    def body(*refs):
        ins, outs = refs[:n], refs[n : 2 * n]
        send_sems, recv_sems = refs[2 * n :]
        x, y, c = _mesh_position()
        copies = []
        for t in range(n):
            copies.append(pltpu.make_async_remote_copy(
                src_ref=ins[t].at[:, _half_rows(ins[t].shape[1], 1 - c), :], dst_ref=outs[t],
                send_sem=send_sems.at[t], recv_sem=recv_sems.at[t], device_id=(x, y, 1 - c), device_id_type=MESH_ID,
            ))
            copies[-1].start()
        for cp in copies:
            cp.wait()

    sems = pltpu.SemaphoreType.DMA((n,))
    return pl.pallas_call(
        body,
        name="swap_halves",
        in_specs=[ANY] * n,
        out_specs=[ANY] * n,
        out_shape=[jax.ShapeDtypeStruct((g.shape[0], g.shape[1] // 2, g.shape[2]), g.dtype) for g in grads],
        scratch_shapes=[sems, sems],
    )(*grads)


def _scatter_to_chips(parts):
    n = len(parts)

    def body(*refs):
        ins, outs = refs[:n], refs[n : 2 * n]
        send_sems, recv_sems = refs[2 * n :]
        x, y, c = _mesh_position()
        copies = []
        for t in range(n):
            for k, (px, py) in enumerate(_other_chips(x, y)):
                copies.append(pltpu.make_async_remote_copy(
                    src_ref=ins[t].at[2 * px + py], dst_ref=outs[t].at[k],
                    send_sem=send_sems.at[t, k], recv_sem=recv_sems.at[t, k], device_id=(px, py, c), device_id_type=MESH_ID,
                ))
                copies[-1].start()
        for cp in copies:
            cp.wait()

    sems = pltpu.SemaphoreType.DMA((n, N_CHIPS - 1))
    return pl.pallas_call(
        body,
        name="scatter_to_chips",
        in_specs=[ANY] * n,
        out_specs=[ANY] * n,
        out_shape=[jax.ShapeDtypeStruct((N_CHIPS - 1,) + p.shape[1:], p.dtype) for p in parts],
        scratch_shapes=[sems, sems],
    )(*parts)


def _join_halves(halves, n_layers):
    n = len(halves)
    per_layer = n // n_layers

    def body(*refs):
        ins, outs = refs[:n], refs[n : n + per_layer]
        send_sems, recv_sems, local_sems = refs[n + per_layer :]
        x, y, c = _mesh_position()
        local, remote = [], []
        for t in range(n):
            layer, tensor = divmod(t, per_layer)
            dst = outs[tensor].at[layer, _half_rows(outs[tensor].shape[1], c), :]
            local.append(pltpu.make_async_copy(ins[t], dst, local_sems.at[t]))
            local[-1].start()
            remote.append(pltpu.make_async_remote_copy(
                src_ref=ins[t], dst_ref=dst, send_sem=send_sems.at[t], recv_sem=recv_sems.at[t],
                device_id=(x, y, 1 - c), device_id_type=MESH_ID,
            ))
            remote[-1].start()
        for cp in remote:
            cp.wait()
        for cp in local:
            cp.wait()

    sems = pltpu.SemaphoreType.DMA((n,))
    return pl.pallas_call(
        body,
        name="join_halves",
        in_specs=[ANY] * n,
        out_specs=[ANY] * per_layer,
        out_shape=[jax.ShapeDtypeStruct((n_layers, 2 * h.shape[0], h.shape[1]), h.dtype) for h in halves[:per_layer]],
        scratch_shapes=[sems, sems, sems],
    )(*halves)


def _gather_small(pack):
    def body(p_ref, o_ref, send_sems, recv_sems, local_sem):
        x, y, c = _mesh_position()
        own = pltpu.make_async_copy(p_ref, o_ref.at[4 * x + 2 * y + c], local_sem)
        own.start()
        copies = []
        for k in range(1, N_DEV):
            px, py, pc = x ^ (k >> 2), y ^ ((k >> 1) & 1), c ^ (k & 1)
            send = pltpu.make_async_remote_copy(
                src_ref=p_ref, dst_ref=o_ref.at[4 * x + 2 * y + c], send_sem=send_sems.at[k - 1], recv_sem=recv_sems.at[k - 1],
                device_id=(px, py, pc), device_id_type=MESH_ID,
            )
            send.start()
            copies.append((send, 4 * px + 2 * py + pc))
        for send, peer_slot in copies:
            send.wait_send()
        for k in range(1, N_DEV):
            px, py, pc = x ^ (k >> 2), y ^ ((k >> 1) & 1), c ^ (k & 1)
            pltpu.make_async_remote_copy(
                src_ref=p_ref, dst_ref=o_ref.at[4 * px + 2 * py + pc], send_sem=send_sems.at[k - 1], recv_sem=recv_sems.at[k - 1],
                device_id=(px, py, pc), device_id_type=MESH_ID,
            ).wait_recv()
        own.wait()

    sems = pltpu.SemaphoreType.DMA((N_DEV - 1,))
    return pl.pallas_call(
        body,
        name="gather_small",
        in_specs=[VMEM_SPEC],
        out_specs=VMEM_SPEC,
        out_shape=jax.ShapeDtypeStruct((N_DEV,) + pack.shape, pack.dtype),
        scratch_shapes=[sems, sems, pltpu.SemaphoreType.DMA],
    )(pack)


def _row_tile(rows):
    for tile in (256, 128, 64, 32, 16, 8):
        if rows % tile == 0:
            return tile
    return rows


def _add_half(grad, received, half_index, name):
    slots, h, cdim = received.shape
    tile = _row_tile(h)
    per_half = h // tile

    def body(c_ref, g_ref, r_ref, o_ref):
        o_ref[...] = g_ref[...] + r_ref[...]

    grid_spec = pltpu.PrefetchScalarGridSpec(
        num_scalar_prefetch=1,
        grid=(slots, per_half),
        in_specs=[
            pl.BlockSpec((1, tile, cdim), lambda j, i, c: (j, c[0] * per_half + i, 0)),
            pl.BlockSpec((1, tile, cdim), lambda j, i, c: (j, i, 0)),
        ],
        out_specs=pl.BlockSpec((1, tile, cdim), lambda j, i, c: (j, i, 0)),
    )
    return pl.pallas_call(
        body, name=name, grid_spec=grid_spec, out_shape=jax.ShapeDtypeStruct(received.shape, F32), compiler_params=_params(2)
    )(half_index, grad, received)


def _add_chips(part, received, chip_index, name):
    _, h, cdim = part.shape
    tile = _row_tile(h)

    def body(j_ref, p_ref, r_ref, o_ref):
        o_ref[...] = ((p_ref[0] + r_ref[0]) + r_ref[1]) + r_ref[2]

    grid_spec = pltpu.PrefetchScalarGridSpec(
        num_scalar_prefetch=1,
        grid=(h // tile,),
        in_specs=[
            pl.BlockSpec((1, tile, cdim), lambda i, j: (j[0], i, 0)),
            pl.BlockSpec((N_CHIPS - 1, tile, cdim), lambda i, j: (0, i, 0)),
        ],
        out_specs=pl.BlockSpec((tile, cdim), lambda i, j: (i, 0)),
    )
    return pl.pallas_call(
        body, name=name, grid_spec=grid_spec, out_shape=jax.ShapeDtypeStruct((h, cdim), F32), compiler_params=_params(1)
    )(chip_index, part, received)


def _adamw(w, g, m, v, name):
    rows, cdim = w.shape
    tile = _row_tile(rows)

    def body(w_ref, g_ref, m_ref, v_ref, d_ref, nm_ref, nv_ref):
        gv = g_ref[...]
        nm = ADAM_B1 * m_ref[...] + (1.0 - ADAM_B1) * gv
        nv = ADAM_B2 * v_ref[...] + (1.0 - ADAM_B2) * (gv * gv)
        m_hat = nm / (1.0 - ADAM_B1 ** ADAM_STEP)
        v_hat = nv / (1.0 - ADAM_B2 ** ADAM_STEP)
        d_ref[...] = -ADAM_LR * (m_hat / (jnp.sqrt(v_hat) + ADAM_EPS) + ADAM_WD * w_ref[...])
        nm_ref[...] = nm
        nv_ref[...] = nv

    spec = pl.BlockSpec((tile, cdim), lambda i: (i, 0))
    shape = jax.ShapeDtypeStruct((rows, cdim), F32)
    return pl.pallas_call(
        body, name=name, grid=(rows // tile,), in_specs=[spec] * 4, out_specs=[spec] * 3, out_shape=[shape] * 3,
        compiler_params=_params(1),
    )(w, g, m, v)


SMALL_ROWS, SMALL_COLS = 24, 1024
ROW_NORM_MIX, ROW_NORM_FFN, ROW_LOSS, ROW_Q_NORM, ROW_K_NORM, ROW_CONV = 0, 2, 4, 8, 10, 16


def _sum_small(gathered):
    def body(g_ref, o_ref, heads_ref, lanes_ref):
        total = g_ref[0]
        for dev in range(1, N_DEV):
            total = total + g_ref[dev]
        o_ref[...] = total
        heads = o_ref[8:16, 0:LANES]
        for grp in range(1, ATTN_DIM // LANES):
            heads = heads + o_ref[8:16, grp * LANES : (grp + 1) * LANES]
        heads_ref[...] = heads + pltpu.roll(heads, HEAD_DIM, 1)
        lanes_ref[...] = jnp.broadcast_to(jnp.sum(o_ref[0:8, :], axis=-1, keepdims=True), (8, LANES))

    return pl.pallas_call(
        body,
        name="sum_small",
        in_specs=[VMEM_SPEC],
        out_specs=[VMEM_SPEC] * 3,
        out_shape=[jax.ShapeDtypeStruct((SMALL_ROWS, SMALL_COLS), F32), jax.ShapeDtypeStruct((8, LANES), F32), jax.ShapeDtypeStruct((8, LANES), F32)],
    )(gathered)


def _pad_rows(a, rows):
    return jnp.pad(a, ((0, rows - a.shape[0]), (0, 0)))


def _pad_to(a, rows, cols):
    return jnp.pad(a, ((0, rows - a.shape[0]), (0, cols - a.shape[1])))


def _local_step(x, target, norm_mix, q_norm, k_norm, norm_ffn, conv_full, win_s, wout_s, wg_s, wu_s, wd_s):
    n_layers = norm_mix.shape[0]
    s, d = x.shape
    tm = TOKEN_TILE
    n_in = win_s.shape[3]
    f = wg_s.shape[3]
    saved = []
    for l in range(n_layers):
        q_gain = jnp.tile(q_norm[l][None, :], (1, 2))
        k_gain = jnp.tile(k_norm[l][None, :], (1, 2))
        h1, proj = _norm_matmul(x, norm_mix[l][None, :], win_s, l, f"in_proj_{l}")
        qn, kn, vb = _qkv_prep(proj, q_gain, k_gain, f"qkv_prep_{l}")
        attn = _attn_fwd(qn, kn, vb, f"attn_fwd_{l}")
        conv = _conv_fwd(proj, conv_full[l], f"conv_fwd_{l}")
        x_mid = _out_proj(x, attn, conv, wout_s, l, f"out_proj_{l}")
        x_out = _ffn_fwd(x_mid, norm_ffn[l][None, :], wg_s, wu_s, wd_s, l, f"ffn_fwd_{l}")
        saved.append(dict(x=x, h1=h1, proj=proj, qn=qn, kn=kn, vb=vb, attn=attn, conv=conv, x_mid=x_mid, q_gain=q_gain, k_gain=k_gain))
        x = x_out

    dy, loss_lanes = _loss_grad(x, target, "loss_grad")
    grads = [None] * n_layers
    for l in reversed(range(n_layers)):
        sv = saved[l]
        dx_mid, d_norm_ffn, h2, dyb, dgate, dup, act = _ffn_bwd(sv["x_mid"], dy, norm_ffn[l][None, :], wg_s, wu_s, wd_s, l, f"ffn_bwd_{l}")
        tok2 = pl.BlockSpec((tm, d), lambda j, i: (i, 0))
        hid = pl.BlockSpec((1, tm, f), lambda j, i: (j, i, 0))
        d_wg = _wgrad(h2, dgate, tok2, hid, N_CHIPS, d, f, f"wgrad_gate_{l}")
        d_wu = _wgrad(h2, dup, tok2, hid, N_CHIPS, d, f, f"wgrad_up_{l}")
        d_wd = _wgrad(act, dyb, hid, tok2, N_CHIPS, f, d, f"wgrad_down_{l}")
        d_attn, d_conv, dxb = _out_proj_bwd(dx_mid, wout_s, l, f"out_proj_bwd_{l}")
        rows_out = wout_s.shape[2]
        mix_spec_a = pl.BlockSpec((tm, rows_out), lambda j, i: (i, j))
        d_wout_a = _wgrad(sv["attn"], dxb, mix_spec_a, tok2, ATTN_DIM // rows_out, rows_out, d, f"wgrad_out_attn_{l}")
        d_wout_c = _wgrad(sv["conv"], dxb, mix_spec_a, tok2, CONV_DIM // rows_out, rows_out, d, f"wgrad_out_conv_{l}")
        d_wout = jnp.concatenate([d_wout_a, d_wout_c], axis=0)
        dq, dk, dv = _attn_bwd(sv["qn"], sv["kn"], sv["vb"], d_attn, f"attn_bwd_{l}")
        dproj_a, d_qg, d_kg = _qkv_prep_bwd(sv["proj"], sv["q_gain"], sv["k_gain"], dq, dk, dv, f"qkv_prep_bwd_{l}")
        dproj_b, d_conv_w = _conv_bwd(sv["proj"], conv_full[l], d_conv, f"conv_bwd_{l}")
        dproj = jnp.concatenate([dproj_a, dproj_b], axis=1)
        d_win = _wgrad(sv["h1"], dproj, tok2, pl.BlockSpec((tm, n_in), lambda j, i: (i, j)), N_CHIPS, d, n_in, f"wgrad_in_{l}")
        dy, d_norm_mix = _in_proj_bwd(sv["x"], dx_mid, norm_mix[l][None, :], dproj, win_s, l, f"in_proj_bwd_{l}")
        grads[l] = dict(norm_mix=d_norm_mix, norm_ffn=d_norm_ffn, q_norm=d_qg, k_norm=d_kg, conv_w=d_conv_w,
                        w_in=d_win, w_out=d_wout, w_gate=d_wg, w_up=d_wu, w_down=d_wd)
    return loss_lanes, dy, grads


BIG = ("w_in", "w_out", "w_gate", "w_up", "w_down")


def kernel(x, norm_mix, w_in, q_norm, k_norm, conv_w, w_out, norm_ffn, w_gate, w_up, w_down, loss_target, m_norm_mix, m_w_in, m_q_norm, m_k_norm, m_conv_w, m_w_out, m_norm_ffn, m_w_gate, m_w_up, m_w_down, v_norm_mix, v_w_in, v_q_norm, v_k_norm, v_conv_w, v_w_out, v_norm_ffn, v_w_gate, v_w_up, v_w_down):
    n_layers = norm_mix.shape[0]
    weights = dict(w_in=w_in, w_out=w_out, w_gate=w_gate, w_up=w_up, w_down=w_down)
    moments_m = dict(w_in=m_w_in, w_out=m_w_out, w_gate=m_w_gate, w_up=m_w_up, w_down=m_w_down)
    moments_v = dict(w_in=v_w_in, w_out=v_w_out, w_gate=v_w_gate, w_up=v_w_up, w_down=v_w_down)
    cx, cy, cc = _mesh_position()
    chip_index = (2 * cx + cy).astype(jnp.int32).reshape(1)
    core_index = cc.astype(jnp.int32).reshape(1)

    conv_pad = jnp.pad(conv_w, ((0, 0), (0, 16 - conv_w.shape[1]), (0, 0)))
    gathered = _gather_weights([weights[k].astype(BF16) for k in BIG] + [conv_pad])
    win_s, wout_s, wg_s, wu_s, wd_s, conv_s = gathered
    conv_full = jnp.transpose(conv_s[:, :, 0:8], (1, 2, 0, 3)).reshape(n_layers, 8, N_CHIPS * conv_w.shape[2])

    loss_lanes, grad_x, grads = _local_step(
        x[0], loss_target[0], norm_mix, q_norm, k_norm, norm_ffn, conv_full, win_s, wout_s, wg_s, wu_s, wd_s)

    flat = [grads[l][k] for l in range(n_layers) for k in BIG]
    names = [f"{k}_{l}" for l in range(n_layers) for k in BIG]
    received = _swap_halves(flat)
    parts = [_add_half(g, r, core_index, f"add_half_{nm}") for g, r, nm in zip(flat, received, names)]
    from_chips = _scatter_to_chips(parts)
    halves = [_add_chips(p, r, chip_index, f"add_chips_{nm}") for p, r, nm in zip(parts, from_chips, names)]
    big_grads = dict(zip(BIG, _join_halves(halves, n_layers)))

    def lanes(a):
        return _pad_to(a, a.shape[0], SMALL_COLS)

    def tile_of(*groups):
        return _pad_rows(jnp.concatenate([lanes(jnp.concatenate(g, axis=0)) for g in groups], axis=0), 8)

    layers = range(n_layers)
    pack = jnp.concatenate([
        tile_of([grads[l]["norm_mix"] for l in layers], [grads[l]["norm_ffn"] for l in layers], [loss_lanes]),
        tile_of([grads[l]["q_norm"] for l in layers], [grads[l]["k_norm"] for l in layers]),
        tile_of([grads[l]["conv_w"][0:3] for l in layers]),
    ], axis=0)
    small, small_heads, small_lanes = _sum_small(_gather_small(pack))
    loss = small_lanes[ROW_LOSS, 0]
    d_model = norm_mix.shape[1]
    conv_cols = conv_w.shape[2]
    conv_all = small[ROW_CONV : ROW_CONV + 3 * n_layers, 0:CONV_DIM].reshape(n_layers, 3, CONV_DIM)
    small_grads = dict(
        norm_mix=small[ROW_NORM_MIX : ROW_NORM_MIX + n_layers, 0:d_model],
        norm_ffn=small[ROW_NORM_FFN : ROW_NORM_FFN + n_layers, 0:d_model],
        q_norm=small_heads[ROW_Q_NORM - 8 : ROW_Q_NORM - 8 + n_layers, 0:HEAD_DIM],
        k_norm=small_heads[ROW_K_NORM - 8 : ROW_K_NORM - 8 + n_layers, 0:HEAD_DIM],
        conv_w=lax.dynamic_slice_in_dim(conv_all, (2 * cx + cy) * conv_cols, conv_cols, axis=2),
    )

    out_grad, out_delta, out_m, out_v = {}, {}, {}, {}
    for k in BIG:
        shape = weights[k].shape
        view = (shape[0] * shape[1], shape[2])
        g = big_grads[k]
        delta, new_m, new_v = _adamw(weights[k].reshape(view), g.reshape(view), moments_m[k].reshape(view), moments_v[k].reshape(view), f"adamw_{k}")
        out_grad[k], out_delta[k], out_m[k], out_v[k] = g, delta.reshape(shape), new_m.reshape(shape), new_v.reshape(shape)

    small_w = dict(norm_mix=norm_mix, norm_ffn=norm_ffn, q_norm=q_norm, k_norm=k_norm, conv_w=conv_w)
    small_m = dict(norm_mix=m_norm_mix, norm_ffn=m_norm_ffn, q_norm=m_q_norm, k_norm=m_k_norm, conv_w=m_conv_w)
    small_v = dict(norm_mix=v_norm_mix, norm_ffn=v_norm_ffn, q_norm=v_q_norm, k_norm=v_k_norm, conv_w=v_conv_w)
    order = ("norm_mix", "norm_ffn", "q_norm", "k_norm", "conv_w")

    def packed(tree):
        parts2 = [_pad_to(tree[k].reshape(-1, tree[k].shape[-1]), tree[k].reshape(-1, tree[k].shape[-1]).shape[0], SMALL_COLS) for k in order]
        return _pad_rows(jnp.concatenate(parts2, axis=0), SMALL_ROWS)

    delta_p, m_p, v_p = _adamw(packed(small_w), packed(small_grads), packed(small_m), packed(small_v), "adamw_small")
    row = 0
    for k in order:
        shape = small_w[k].shape
        n_rows = 1
        for dim in shape[:-1]:
            n_rows *= dim
        cut = (slice(row, row + n_rows), slice(0, shape[-1]))
        out_grad[k] = small_grads[k]
        out_delta[k], out_m[k], out_v[k] = delta_p[cut].reshape(shape), m_p[cut].reshape(shape), v_p[cut].reshape(shape)
        row += n_rows

    names_out = ("norm_mix", "w_in", "q_norm", "k_norm", "conv_w", "w_out", "norm_ffn", "w_gate", "w_up", "w_down")
    return (loss, grad_x[None], *[out_grad[k] for k in names_out], *[out_delta[k] for k in names_out],
            *[out_m[k] for k in names_out], *[out_v[k] for k in names_out])
```

```python
import functools

import jax
import jax.numpy as jnp
from jax import lax
from jax.experimental import pallas as pl
from jax.experimental.pallas import tpu as pltpu

F32 = jnp.float32
BF16 = jnp.bfloat16

EPS = 1e-6
HEAD_DIM = 64
LANES = 128
ATTN_DIM = 512
CONV_DIM = 512
N_CHIPS = 4
N_DEV = 8
Q_SCALE = HEAD_DIM ** -0.5
ATTN_Q_TILE = 256
ATTN_TILE = 256
TOKEN_TILE = 512
VMEM_LIMIT = 56 * 1024 * 1024

ADAM_LR = 0.001
ADAM_B1 = 0.9
ADAM_B2 = 0.999
ADAM_EPS = 1e-08
ADAM_WD = 0.01
ADAM_STEP = 10

MESH_ID = pl.DeviceIdType.MESH
ANY = pl.BlockSpec(memory_space=pl.ANY)
VMEM_SPEC = pl.BlockSpec(memory_space=pltpu.VMEM)


def _params(n_axes):
    return pltpu.CompilerParams(dimension_semantics=("arbitrary",) * n_axes, vmem_limit_bytes=VMEM_LIMIT)


def _dot(a, b):
    return jnp.dot(a, b, preferred_element_type=F32)


def _dot_nt(a, b):
    return lax.dot_general(a, b, (((1,), (1,)), ((), ())), preferred_element_type=F32)


def _dot_tn(a, b):
    return lax.dot_general(a, b, (((0,), (0,)), ((), ())), preferred_element_type=F32)


def _softplus(z):
    return jnp.maximum(z, 0.0) + jnp.log(1.0 + jnp.exp(-jnp.abs(z)))


def _norm_matmul(x, gain, w_s, layer, name):
    s, d = x.shape
    n_blocks, _, _, n = w_s.shape
    tm = TOKEN_TILE

    def body(x_ref, g_ref, w_ref, h_ref, o_ref):
        @pl.when(pl.program_id(1) == 0)
        def _():
            xv = x_ref[...]
            r = lax.rsqrt(jnp.mean(xv * xv, axis=-1, keepdims=True) + EPS)
            h_ref[...] = (xv * r * g_ref[...]).astype(BF16)

        o_ref[...] = _dot(h_ref[...], w_ref[0, 0])

    return pl.pallas_call(
        body,
        name=name,
        grid=(s // tm, n_blocks),
        in_specs=[
            pl.BlockSpec((tm, d), lambda i, j: (i, 0)),
            pl.BlockSpec((1, d), lambda i, j: (0, 0)),
            pl.BlockSpec((1, 1, d, n), lambda i, j: (j, layer, 0, 0)),
        ],
        out_specs=[pl.BlockSpec((tm, d), lambda i, j: (i, 0)), pl.BlockSpec((tm, n), lambda i, j: (i, j))],
        out_shape=[jax.ShapeDtypeStruct((s, d), BF16), jax.ShapeDtypeStruct((s, n_blocks * n), F32)],
        compiler_params=_params(2),
    )(x, gain, w_s)


def _head_norm(xv, gain, low):
    sq = xv * xv
    s_low = jnp.sum(jnp.where(low, sq, 0.0), axis=-1, keepdims=True)
    s_high = jnp.sum(jnp.where(low, 0.0, sq), axis=-1, keepdims=True)
    r = jnp.where(low, lax.rsqrt(s_low / HEAD_DIM + EPS), lax.rsqrt(s_high / HEAD_DIM + EPS))
    return xv * r * gain, r


def _qkv_prep(proj, q_gain, k_gain, name):
    s = proj.shape[0]
    tm = TOKEN_TILE

    def body(p_ref, qg_ref, kg_ref, q_ref, k_ref, v_ref):
        low = lax.broadcasted_iota(jnp.int32, (tm, LANES), 1) < HEAD_DIM
        for g in range(ATTN_DIM // LANES):
            cq = slice(LANES * g, LANES * (g + 1))
            ck = slice(ATTN_DIM + LANES * g, ATTN_DIM + LANES * (g + 1))
            cv = slice(2 * ATTN_DIM + LANES * g, 2 * ATTN_DIM + LANES * (g + 1))
            qn, _ = _head_norm(p_ref[:, cq], qg_ref[...], low)
            kn, _ = _head_norm(p_ref[:, ck], kg_ref[...], low)
            q_ref[:, cq] = (qn * Q_SCALE).astype(BF16)
            k_ref[:, cq] = kn.astype(BF16)
            v_ref[:, cq] = p_ref[:, cv].astype(BF16)

    out = jax.ShapeDtypeStruct((s, ATTN_DIM), BF16)
    return pl.pallas_call(
        body,
        name=name,
        grid=(s // tm,),
        in_specs=[
            pl.BlockSpec((tm, 3 * ATTN_DIM), lambda i: (i, 0)),
            pl.BlockSpec((1, LANES), lambda i: (0, 0)),
            pl.BlockSpec((1, LANES), lambda i: (0, 0)),
        ],
        out_specs=[pl.BlockSpec((tm, ATTN_DIM), lambda i: (i, 0))] * 3,
        out_shape=[out, out, out],
        compiler_params=_params(1),
    )(proj, q_gain, k_gain)


def _qkv_prep_bwd(proj, q_gain, k_gain, dq, dk, dv, name):
    s = proj.shape[0]
    tm = TOKEN_TILE

    def norm_bwd(xv, gain, dy, low):
        _, r = _head_norm(xv, gain, low)
        xhat = xv * r
        dxhat = dy * gain
        prod = dxhat * xhat
        m_low = jnp.sum(jnp.where(low, prod, 0.0), axis=-1, keepdims=True)
        m_high = jnp.sum(jnp.where(low, 0.0, prod), axis=-1, keepdims=True)
        mean = jnp.where(low, m_low, m_high) / HEAD_DIM
        return r * (dxhat - xhat * mean), jnp.sum(dy * xhat, axis=0, keepdims=True)

    def body(p_ref, qg_ref, kg_ref, dq_ref, dk_ref, dv_ref, dp_ref, dqg_ref, dkg_ref):
        @pl.when(pl.program_id(0) == 0)
        def _():
            dqg_ref[...] = jnp.zeros_like(dqg_ref)
            dkg_ref[...] = jnp.zeros_like(dkg_ref)

        low = lax.broadcasted_iota(jnp.int32, (tm, LANES), 1) < HEAD_DIM
        for g in range(ATTN_DIM // LANES):
            cq = slice(LANES * g, LANES * (g + 1))
            ck = slice(ATTN_DIM + LANES * g, ATTN_DIM + LANES * (g + 1))
            cv = slice(2 * ATTN_DIM + LANES * g, 2 * ATTN_DIM + LANES * (g + 1))
            dxq, dgq = norm_bwd(p_ref[:, cq], qg_ref[...], dq_ref[:, cq] * Q_SCALE, low)
            dxk, dgk = norm_bwd(p_ref[:, ck], kg_ref[...], dk_ref[:, cq], low)
            dp_ref[:, cq] = dxq.astype(BF16)
            dp_ref[:, ck] = dxk.astype(BF16)
            dp_ref[:, cv] = dv_ref[:, cq].astype(BF16)
            dqg_ref[:, cq] += dgq
            dkg_ref[:, cq] += dgk

    grad_spec = pl.BlockSpec((tm, ATTN_DIM), lambda i: (i, 0))
    gain_spec = pl.BlockSpec((1, LANES), lambda i: (0, 0))
    sum_spec = pl.BlockSpec((1, ATTN_DIM), lambda i: (0, 0))
    return pl.pallas_call(
        body,
        name=name,
        grid=(s // tm,),
        in_specs=[pl.BlockSpec((tm, 3 * ATTN_DIM), lambda i: (i, 0)), gain_spec, gain_spec, grad_spec, grad_spec, grad_spec],
        out_specs=[pl.BlockSpec((tm, 3 * ATTN_DIM), lambda i: (i, 0)), sum_spec, sum_spec],
        out_shape=[
            jax.ShapeDtypeStruct((s, 3 * ATTN_DIM), BF16),
            jax.ShapeDtypeStruct((1, ATTN_DIM), F32),
            jax.ShapeDtypeStruct((1, ATTN_DIM), F32),
        ],
        compiler_params=_params(1),
    )(proj, q_gain, k_gain, dq, dk, dv)


def _attn_tile_consts(t):
    row = lax.broadcasted_iota(jnp.int32, (t, t), 0)
    col = lax.broadcasted_iota(jnp.int32, (t, t), 1)
    return row, col


def _triangle_sum(v, triangle):
    return _dot(v.astype(BF16), triangle)


def _attn_fwd(qn, kn, vb, name):
    s = qn.shape[0]
    t = min(ATTN_TILE, s)
    tq = min(ATTN_Q_TILE, t)
    per_key_tile = t // tq

    def body(q_ref, k_ref, v_ref, o_ref):
        i = pl.program_id(1) // per_key_tile
        low = lax.broadcasted_iota(jnp.int32, (tq, LANES), 1) < HEAD_DIM
        row, col = _attn_tile_consts(t)
        suffix = (row > col).astype(BF16)
        first_row = (pl.program_id(1) % per_key_tile) * tq
        causal = lax.broadcasted_iota(jnp.int32, (tq, t), 1) < lax.broadcasted_iota(jnp.int32, (tq, t), 0) + first_row
        q = q_ref[...]
        zero_q = jnp.zeros_like(q)
        qh = (jnp.where(low, q, zero_q), jnp.where(low, zero_q, q))

        def step(kbs, carry, diagonal=False):
            chains = [(head, m) for head in range(2) for m in range(len(kbs))]
            ks = [k_ref[pl.ds(pl.multiple_of(kb * t, t), t), :] for kb in kbs]
            vs = [v_ref[pl.ds(pl.multiple_of(kb * t, t), t), :] for kb in kbs]
            z = [_dot_nt(qh[head], ks[kb]) for head, kb in chains]
            sp = [_softplus(zc) for zc in z]
            if diagonal:
                sp = [jnp.where(causal, s_, 0.0) for s_ in sp]
            inside = [_triangle_sum(s_, suffix) for s_ in sp]
            after = [carry[head][1] for head in range(2)]
            log_a = []
            for n, (head, kb) in enumerate(chains):
                log_a.append(z[n] - sp[n] - inside[n] - after[head])
                after[head] = after[head] + jnp.sum(sp[n], axis=-1, keepdims=True)
            a = [jnp.exp(l_) for l_ in log_a]
            if diagonal:
                a = [jnp.where(causal, a_, 0.0) for a_ in a]
            acc = [carry[head][0] for head in range(2)]
            for n, (head, kb) in enumerate(chains):
                acc[head] = acc[head] + _dot(a[n].astype(BF16), vs[kb])
            return tuple((acc[head], after[head]) for head in range(2))

        zero = (jnp.zeros((tq, LANES), F32), jnp.zeros((tq, 1), F32))
        carry = step((i,), (zero, zero), True)
        carry = lax.cond(i % 2 == 1, lambda c: step((i - 1,), c), lambda c: c, carry)
        pairs = i // 2
        carry = lax.fori_loop(0, pairs, lambda n, c: step((2 * (pairs - n) - 1, 2 * (pairs - n) - 2), c), carry)
        o_ref[...] = jnp.where(low, carry[0][0], carry[1][0]).astype(BF16)

    return pl.pallas_call(
        body,
        name=name,
        grid=(ATTN_DIM // LANES, s // tq),
        in_specs=[
            pl.BlockSpec((tq, LANES), lambda p, i: (i, p)),
            pl.BlockSpec((s, LANES), lambda p, i: (0, p)),
            pl.BlockSpec((s, LANES), lambda p, i: (0, p)),
        ],
        out_specs=pl.BlockSpec((tq, LANES), lambda p, i: (i, p)),
        out_shape=jax.ShapeDtypeStruct((s, ATTN_DIM), BF16),
        compiler_params=_params(2),
    )(qn, kn, vb)


def _attn_bwd(qn, kn, vb, do, name):
    s = qn.shape[0]
    t = min(ATTN_TILE, s)
    nq = s // t

    def body(q_ref, k_ref, v_ref, do_ref, dq_ref, dk_ref, dv_ref, a_s, sg_s):
        i = pl.program_id(1)

        @pl.when(i == 0)
        def _():
            dk_ref[...] = jnp.zeros_like(dk_ref)
            dv_ref[...] = jnp.zeros_like(dv_ref)

        low = lax.broadcasted_iota(jnp.int32, (t, LANES), 1) < HEAD_DIM
        row, col = _attn_tile_consts(t)
        suffix = (row > col).astype(BF16)
        prefix = (row < col).astype(BF16)
        causal = col < row
        q = q_ref[...]
        dob = do_ref[...]
        zero_q = jnp.zeros_like(q)
        pairs = i // 2
        heads = []
        for head in range(2):
            if head == 0:
                qh, doh = jnp.where(low, q, zero_q), jnp.where(low, dob, zero_q)
            else:
                qh, doh = jnp.where(low, zero_q, q), jnp.where(low, zero_q, dob)

            def rows_of(kb):
                return pl.ds(pl.multiple_of(kb * t, t), t)

            def pass1(kbs, after, diagonal=False):
                z = [_dot_nt(qh, k_ref[rows_of(kb), :]) for kb in kbs]
                sp = [_softplus(z_) for z_ in z]
                if diagonal:
                    sp = [jnp.where(causal, s_, 0.0) for s_ in sp]
                inside = [_triangle_sum(s_, suffix) for s_ in sp]
                for n, kb in enumerate(kbs):
                    log_sg = z[n] - sp[n]
                    a = jnp.exp(log_sg - inside[n] - after)
                    sg = jnp.exp(log_sg)
                    if diagonal:
                        a = jnp.where(causal, a, 0.0)
                        sg = jnp.where(causal, sg, 0.0)
                    a_s[kb] = a
                    sg_s[kb] = sg
                    after = after + jnp.sum(sp[n], axis=-1, keepdims=True)
                return after

            after = pass1((i,), jnp.zeros((t, 1), F32), True)
            after = lax.cond(i % 2 == 1, lambda c: pass1((i - 1,), c), lambda c: c, after)
            lax.fori_loop(0, pairs, lambda n, c: pass1((2 * (pairs - n) - 1, 2 * (pairs - n) - 2), c), after)

            def pass2(kbs, carry):
                dq, before = carry
                ks = [k_ref[rows_of(kb), :] for kb in kbs]
                a = [a_s[kb] for kb in kbs]
                g = [a_ * _dot_nt(doh, v_ref[rows_of(kb), :]) for a_, kb in zip(a, kbs)]
                for n, kb in enumerate(kbs):
                    dv_ref[rows_of(kb), :] += _dot_tn(a[n].astype(BF16), doh)
                inside = [_triangle_sum(g_, prefix) for g_ in g]
                dz = []
                for n, kb in enumerate(kbs):
                    sg = sg_s[kb]
                    dz.append((g[n] * (1.0 - sg) - sg * (inside[n] + before)).astype(BF16))
                    before = before + jnp.sum(g[n], axis=-1, keepdims=True)
                for n, kb in enumerate(kbs):
                    dk_ref[rows_of(kb), :] += _dot_tn(dz[n], qh)
                for n in range(len(kbs)):
                    dq = dq + _dot(dz[n], ks[n])
                return dq, before

            carry = (jnp.zeros((t, LANES), F32), jnp.zeros((t, 1), F32))
            carry = lax.fori_loop(0, (i + 1) // 2, lambda n, c: pass2((2 * n, 2 * n + 1), c), carry)
            carry = lax.cond(i % 2 == 0, lambda c: pass2((i,), c), lambda c: c, carry)
            heads.append(carry[0])
        dq_ref[...] = jnp.where(low, heads[0], heads[1])

    q_spec = pl.BlockSpec((t, LANES), lambda p, i: (i, p))
    kv_spec = pl.BlockSpec((s, LANES), lambda p, i: (0, p))
    return pl.pallas_call(
        body,
        name=name,
        grid=(ATTN_DIM // LANES, nq),
        in_specs=[q_spec, kv_spec, kv_spec, q_spec],
        out_specs=[q_spec, kv_spec, kv_spec],
        out_shape=[jax.ShapeDtypeStruct((s, ATTN_DIM), F32)] * 3,
        scratch_shapes=[pltpu.VMEM((nq, t, t), F32), pltpu.VMEM((nq, t, t), F32)],
        compiler_params=_params(2),
    )(qn, kn, vb, do)


CB_BLOCK, CC_BLOCK, CU_BLOCK = 3, 4, 5


def _shift_down(h, prev_rows, n):
    row = lax.broadcasted_iota(jnp.int32, h.shape, 0)
    out = pltpu.roll(h, n, 0)
    for r in range(n):
        out = jnp.where(row == r, prev_rows[len(prev_rows) - n + r], out)
    return out


def _shift_up(h, next_rows, n):
    tm = h.shape[0]
    row = lax.broadcasted_iota(jnp.int32, h.shape, 0)
    out = pltpu.roll(h, tm - n, 0)
    for r in range(n):
        out = jnp.where(row == tm - n + r, next_rows[r], out)
    return out


def _conv_fwd(proj, conv_w, name):
    s = proj.shape[0]
    tm = TOKEN_TILE
    nb = tm // 8

    def body(cb_ref, cc_ref, cu_ref, pc_ref, pu_ref, w_ref, o_ref):
        first = pl.program_id(0) == 0
        h = cc_ref[...] * cu_ref[...]
        prev = [jnp.where(first, 0.0, pc_ref[r : r + 1, :] * pu_ref[r : r + 1, :]) for r in (6, 7)]
        y = w_ref[0:1, :] * _shift_down(h, prev, 2) + w_ref[1:2, :] * _shift_down(h, prev, 1) + w_ref[2:3, :] * h
        o_ref[...] = (cb_ref[...] * y).astype(BF16)

    def col(block):
        return pl.BlockSpec((tm, CONV_DIM), lambda i: (i, block))

    def halo(block):
        return pl.BlockSpec((8, CONV_DIM), lambda i: (jnp.maximum(i * nb - 1, 0), block))

    return pl.pallas_call(
        body,
        name=name,
        grid=(s // tm,),
        in_specs=[col(CB_BLOCK), col(CC_BLOCK), col(CU_BLOCK), halo(CC_BLOCK), halo(CU_BLOCK), pl.BlockSpec((8, CONV_DIM), lambda i: (0, 0))],
        out_specs=pl.BlockSpec((tm, CONV_DIM), lambda i: (i, 0)),
        out_shape=jax.ShapeDtypeStruct((s, CONV_DIM), BF16),
        compiler_params=_params(1),
    )(proj, proj, proj, proj, proj, conv_w)


def _conv_bwd(proj, conv_w, dconv, name):
    s = proj.shape[0]
    tm = TOKEN_TILE
    nb = tm // 8
    n_tiles = s // tm

    def body(cb_ref, cc_ref, cu_ref, dy_ref, pc_ref, pu_ref, nb_ref, ndy_ref, w_ref, dp_ref, dw_ref):
        i = pl.program_id(0)

        @pl.when(i == 0)
        def _():
            dw_ref[...] = jnp.zeros_like(dw_ref)

        first = i == 0
        last = i == n_tiles - 1
        cc, cu, cb, dy = cc_ref[...], cu_ref[...], cb_ref[...], dy_ref[...]
        h = cc * cu
        prev = [jnp.where(first, 0.0, pc_ref[r : r + 1, :] * pu_ref[r : r + 1, :]) for r in (6, 7)]
        h1 = _shift_down(h, prev, 1)
        h2 = _shift_down(h, prev, 2)
        y = w_ref[0:1, :] * h2 + w_ref[1:2, :] * h1 + w_ref[2:3, :] * h
        dyb = dy * cb
        nxt = [jnp.where(last, 0.0, ndy_ref[r : r + 1, :] * nb_ref[r : r + 1, :]) for r in (0, 1)]
        dh = w_ref[2:3, :] * dyb + w_ref[1:2, :] * _shift_up(dyb, nxt, 1) + w_ref[0:1, :] * _shift_up(dyb, nxt, 2)
        dp_ref[:, 0:CONV_DIM] = (dy * y).astype(BF16)
        dp_ref[:, CONV_DIM : 2 * CONV_DIM] = (dh * cu).astype(BF16)
        dp_ref[:, 2 * CONV_DIM : 3 * CONV_DIM] = (dh * cc).astype(BF16)
        dw_ref[0:1, :] += jnp.sum(dyb * h2, axis=0, keepdims=True)
        dw_ref[1:2, :] += jnp.sum(dyb * h1, axis=0, keepdims=True)
        dw_ref[2:3, :] += jnp.sum(dyb * h, axis=0, keepdims=True)

    def col(block):
        return pl.BlockSpec((tm, CONV_DIM), lambda i: (i, block))

    def halo_prev(block):
        return pl.BlockSpec((8, CONV_DIM), lambda i: (jnp.maximum(i * nb - 1, 0), block))

    def halo_next(block):
        return pl.BlockSpec((8, CONV_DIM), lambda i: (jnp.minimum((i + 1) * nb, s // 8 - 1), block))

    return pl.pallas_call(
        body,
        name=name,
        grid=(n_tiles,),
        in_specs=[
            col(CB_BLOCK), col(CC_BLOCK), col(CU_BLOCK), col(0),
            halo_prev(CC_BLOCK), halo_prev(CU_BLOCK), halo_next(CB_BLOCK), halo_next(0),
            pl.BlockSpec((8, CONV_DIM), lambda i: (0, 0)),
        ],
        out_specs=[pl.BlockSpec((tm, 3 * CONV_DIM), lambda i: (i, 0)), pl.BlockSpec((8, CONV_DIM), lambda i: (0, 0))],
        out_shape=[jax.ShapeDtypeStruct((s, 3 * CONV_DIM), BF16), jax.ShapeDtypeStruct((8, CONV_DIM), F32)],
        compiler_params=_params(1),
    )(proj, proj, proj, dconv, proj, proj, proj, dconv, conv_w)


def _out_proj(x, attn, conv, w_s, layer, name):
    s, d = x.shape
    tm = TOKEN_TILE
    rows = w_s.shape[2]

    def body(x_ref, a_ref, c_ref, w_ref, o_ref):
        acc = x_ref[...]
        for j in range(N_CHIPS):
            src = a_ref if j < 2 else c_ref
            cols = slice((j % 2) * rows, (j % 2 + 1) * rows)
            acc = acc + _dot(src[:, cols], w_ref[j, 0])
        o_ref[...] = acc

    return pl.pallas_call(
        body,
        name=name,
        grid=(s // tm,),
        in_specs=[
            pl.BlockSpec((tm, d), lambda i: (i, 0)),
            pl.BlockSpec((tm, ATTN_DIM), lambda i: (i, 0)),
            pl.BlockSpec((tm, CONV_DIM), lambda i: (i, 0)),
            pl.BlockSpec((N_CHIPS, 1, rows, d), lambda i: (0, layer, 0, 0)),
        ],
        out_specs=pl.BlockSpec((tm, d), lambda i: (i, 0)),
        out_shape=jax.ShapeDtypeStruct((s, d), F32),
        compiler_params=_params(1),
    )(x, attn, conv, w_s)


def _out_proj_bwd(dx, w_s, layer, name):
    s, d = dx.shape
    tm = TOKEN_TILE
    rows = w_s.shape[2]

    def body(dx_ref, w_ref, da_ref, dc_ref, dxb_ref):
        dxb = dx_ref[...].astype(BF16)
        dxb_ref[...] = dxb
        for j in range(N_CHIPS):
            cols = slice((j % 2) * rows, (j % 2 + 1) * rows)
            part = _dot_nt(dxb, w_ref[j, 0])
            if j < 2:
                da_ref[:, cols] = part.astype(BF16)
            else:
                dc_ref[:, cols] = part

    return pl.pallas_call(
        body,
        name=name,
        grid=(s // tm,),
        in_specs=[pl.BlockSpec((tm, d), lambda i: (i, 0)), pl.BlockSpec((N_CHIPS, 1, rows, d), lambda i: (0, layer, 0, 0))],
        out_specs=[
            pl.BlockSpec((tm, ATTN_DIM), lambda i: (i, 0)),
            pl.BlockSpec((tm, CONV_DIM), lambda i: (i, 0)),
            pl.BlockSpec((tm, d), lambda i: (i, 0)),
        ],
        out_shape=[
            jax.ShapeDtypeStruct((s, ATTN_DIM), BF16),
            jax.ShapeDtypeStruct((s, CONV_DIM), F32),
            jax.ShapeDtypeStruct((s, d), BF16),
        ],
        compiler_params=_params(1),
    )(dx, w_s)


def _ffn_fwd(x, gain, wg_s, wu_s, wd_s, layer, name):
    s, d = x.shape
    tm = TOKEN_TILE
    f = wg_s.shape[3]

    def body(x_ref, g_ref, wg_ref, wu_ref, wd_ref, o_ref, h_s):
        j = pl.program_id(1)

        @pl.when(j == 0)
        def _():
            xv = x_ref[...]
            r = lax.rsqrt(jnp.mean(xv * xv, axis=-1, keepdims=True) + EPS)
            h_s[...] = (xv * r * g_ref[...]).astype(BF16)
            o_ref[...] = xv

        h = h_s[...]
        gate = _dot(h, wg_ref[0, 0])
        up = _dot(h, wu_ref[0, 0])
        act = (gate / (1.0 + jnp.exp(-gate))) * up
        o_ref[...] += _dot(act.astype(BF16), wd_ref[0, 0])

    return pl.pallas_call(
        body,
        name=name,
        grid=(s // tm, N_CHIPS),
        in_specs=[
            pl.BlockSpec((tm, d), lambda i, j: (i, 0)),
            pl.BlockSpec((1, d), lambda i, j: (0, 0)),
            pl.BlockSpec((1, 1, d, f), lambda i, j: (j, layer, 0, 0)),
            pl.BlockSpec((1, 1, d, f), lambda i, j: (j, layer, 0, 0)),
            pl.BlockSpec((1, 1, f, d), lambda i, j: (j, layer, 0, 0)),
        ],
        out_specs=pl.BlockSpec((tm, d), lambda i, j: (i, 0)),
        out_shape=jax.ShapeDtypeStruct((s, d), F32),
        scratch_shapes=[pltpu.VMEM((tm, d), BF16)],
        compiler_params=_params(2),
    )(x, gain, wg_s, wu_s, wd_s)


def _rms_bwd(xv, gain, dh):
    r = lax.rsqrt(jnp.mean(xv * xv, axis=-1, keepdims=True) + EPS)
    xhat = xv * r
    dxhat = dh * gain
    dx = r * (dxhat - xhat * jnp.mean(dxhat * xhat, axis=-1, keepdims=True))
    return dx, jnp.sum(dh * xhat, axis=0, keepdims=True)


def _ffn_bwd(x, dy, gain, wg_s, wu_s, wd_s, layer, name):
    s, d = x.shape
    tm = TOKEN_TILE
    f = wg_s.shape[3]

    def body(x_ref, dy_ref, g_ref, wg_ref, wu_ref, wd_ref, dx_ref, dgain_ref, h_ref, dyb_ref, dg_ref, du_ref, act_ref, acc_s):
        i, j = pl.program_id(0), pl.program_id(1)

        @pl.when((i == 0) & (j == 0))
        def _():
            dgain_ref[...] = jnp.zeros_like(dgain_ref)

        @pl.when(j == 0)
        def _():
            xv = x_ref[...]
            r = lax.rsqrt(jnp.mean(xv * xv, axis=-1, keepdims=True) + EPS)
            h_ref[...] = (xv * r * g_ref[...]).astype(BF16)
            dyb_ref[...] = dy_ref[...].astype(BF16)
            acc_s[...] = jnp.zeros_like(acc_s)

        h = h_ref[...]
        gate = _dot(h, wg_ref[0, 0])
        up = _dot(h, wu_ref[0, 0])
        sig = 1.0 / (1.0 + jnp.exp(-gate))
        silu = gate * sig
        dact = _dot_nt(dyb_ref[...], wd_ref[0, 0])
        dgate = (dact * up * (sig * (1.0 + gate * (1.0 - sig)))).astype(BF16)
        dup = (dact * silu).astype(BF16)
        act_ref[0] = (silu * up).astype(BF16)
        dg_ref[0] = dgate
        du_ref[0] = dup
        acc_s[...] += _dot_nt(dgate, wg_ref[0, 0]) + _dot_nt(dup, wu_ref[0, 0])

        @pl.when(j == N_CHIPS - 1)
        def _():
            dxn, dgain = _rms_bwd(x_ref[...], g_ref[...], acc_s[...])
            dx_ref[...] = dy_ref[...] + dxn
            dgain_ref[...] += dgain

    tok = pl.BlockSpec((tm, d), lambda i, j: (i, 0))
    vec = pl.BlockSpec((1, d), lambda i, j: (0, 0))
    hid = pl.BlockSpec((1, tm, f), lambda i, j: (j, i, 0))
    hid_shape = jax.ShapeDtypeStruct((N_CHIPS, s, f), BF16)
    return pl.pallas_call(
        body,
        name=name,
        grid=(s // tm, N_CHIPS),
        in_specs=[
            tok, tok, vec,
            pl.BlockSpec((1, 1, d, f), lambda i, j: (j, layer, 0, 0)),
            pl.BlockSpec((1, 1, d, f), lambda i, j: (j, layer, 0, 0)),
            pl.BlockSpec((1, 1, f, d), lambda i, j: (j, layer, 0, 0)),
        ],
        out_specs=[tok, vec, tok, tok, hid, hid, hid],
        out_shape=[
            jax.ShapeDtypeStruct((s, d), F32),
            jax.ShapeDtypeStruct((1, d), F32),
            jax.ShapeDtypeStruct((s, d), BF16),
            jax.ShapeDtypeStruct((s, d), BF16),
            hid_shape, hid_shape, hid_shape,
        ],
        scratch_shapes=[pltpu.VMEM((tm, d), F32)],
        compiler_params=_params(2),
    )(x, dy, gain, wg_s, wu_s, wd_s)


def _in_proj_bwd(x, dx_res, gain, dproj, w_s, layer, name):
    s, d = x.shape
    tm = TOKEN_TILE
    n = w_s.shape[3]

    def body(x_ref, r_ref, g_ref, dp_ref, w_ref, dx_ref, dgain_ref, acc_s):
        i, j = pl.program_id(0), pl.program_id(1)

        @pl.when((i == 0) & (j == 0))
        def _():
            dgain_ref[...] = jnp.zeros_like(dgain_ref)

        @pl.when(j == 0)
        def _():
            acc_s[...] = jnp.zeros_like(acc_s)

        acc_s[...] += _dot_nt(dp_ref[...], w_ref[0, 0])

        @pl.when(j == N_CHIPS - 1)
        def _():
            dxn, dgain = _rms_bwd(x_ref[...], g_ref[...], acc_s[...])
            dx_ref[...] = r_ref[...] + dxn
            dgain_ref[...] += dgain

    tok = pl.BlockSpec((tm, d), lambda i, j: (i, 0))
    vec = pl.BlockSpec((1, d), lambda i, j: (0, 0))
    return pl.pallas_call(
        body,
        name=name,
        grid=(s // tm, N_CHIPS),
        in_specs=[tok, tok, vec, pl.BlockSpec((tm, n), lambda i, j: (i, j)), pl.BlockSpec((1, 1, d, n), lambda i, j: (j, layer, 0, 0))],
        out_specs=[tok, vec],
        out_shape=[jax.ShapeDtypeStruct((s, d), F32), jax.ShapeDtypeStruct((1, d), F32)],
        scratch_shapes=[pltpu.VMEM((tm, d), F32)],
        compiler_params=_params(2),
    )(x, dx_res, gain, dproj, w_s)


def _loss_grad(y, target, name):
    s, d = y.shape
    tm = TOKEN_TILE

    def body(y_ref, t_ref, dy_ref, l_ref):
        @pl.when(pl.program_id(0) == 0)
        def _():
            l_ref[...] = jnp.zeros_like(l_ref)

        err = y_ref[...] - t_ref[...]
        dy_ref[...] = err / d
        l_ref[...] += jnp.sum(err * err, axis=0, keepdims=True) * (0.5 / d)

    tok = pl.BlockSpec((tm, d), lambda i: (i, 0))
    return pl.pallas_call(
        body,
        name=name,
        grid=(s // tm,),
        in_specs=[tok, tok],
        out_specs=[tok, pl.BlockSpec((1, d), lambda i: (0, 0))],
        out_shape=[jax.ShapeDtypeStruct((s, d), F32), jax.ShapeDtypeStruct((1, d), F32)],
        compiler_params=_params(1),
    )(y, target)


def _wgrad(a, b, a_spec, b_spec, n_blocks, k, n, name):
    n_tiles = (a.shape[-2]) // TOKEN_TILE

    def body(a_ref, b_ref, o_ref):
        @pl.when(pl.program_id(1) == 0)
        def _():
            o_ref[...] = jnp.zeros_like(o_ref)

        av = a_ref[0] if len(a_ref.shape) == 3 else a_ref[...]
        bv = b_ref[0] if len(b_ref.shape) == 3 else b_ref[...]
        o_ref[0] += _dot_tn(av, bv)

    return pl.pallas_call(
        body,
        name=name,
        grid=(n_blocks, n_tiles),
        in_specs=[a_spec, b_spec],
        out_specs=pl.BlockSpec((1, k, n), lambda j, i: (j, 0, 0)),
        out_shape=jax.ShapeDtypeStruct((n_blocks, k, n), F32),
        compiler_params=_params(2),
    )(a, b)


def _mesh_position():
    return lax.axis_index("x"), lax.axis_index("y"), lax.axis_index("c")


def _other_chips(x, y):
    return [(1 - x, y), (x, 1 - y), (1 - x, 1 - y)]


def _half_rows(ref_rows, c):
    half = ref_rows // 2
    return pl.ds(c * half, half)


def _gather_weights(shards):
    n = len(shards)

    def body(*refs):
        ins, outs = refs[:n], refs[n : 2 * n]
        send_sems, recv_sems, pass_send_sems, pass_recv_sems, local_sems = refs[2 * n :]
        x, y, c = _mesh_position()
        me = 2 * x + y
        sibling = (x, y, 1 - c)
        chips = _other_chips(x, y)

        def block(t, chip_index, core):
            return outs[t].at[chip_index, :, _half_rows(ins[t].shape[1], core), :]

        def copy(t, k, chip_index, core, to, sems, src=None):
            dst = block(t, chip_index, core)
            return pltpu.make_async_remote_copy(
                src_ref=dst if src is None else src, dst_ref=dst, send_sem=sems[0].at[t, k], recv_sem=sems[1].at[t, k],
                device_id=to, device_id_type=MESH_ID,
            )

        ici, d2d = (send_sems, recv_sems), (pass_send_sems, pass_recv_sems)
        own = [pltpu.make_async_copy(ins[t], outs[t].at[me], local_sems.at[t]) for t in range(n)]
        for cp in own:
            cp.start()
        started = []
        for t in range(n):
            mine = ins[t].at[:, _half_rows(ins[t].shape[1], c), :]
            for k, (px, py) in enumerate(chips):
                started.append(copy(t, k, me, c, (px, py, c), ici, src=mine))
                started[-1].start()
        for t in range(n):
            for k, (px, py) in enumerate(chips):
                copy(t, k, 2 * px + py, c, sibling, ici).wait_recv()
                started.append(copy(t, k, 2 * px + py, c, sibling, d2d))
                started[-1].start()
        for t in range(n):
            for k, (px, py) in enumerate(chips):
                copy(t, k, 2 * px + py, 1 - c, sibling, d2d).wait_recv()
        for cp in started:
            cp.wait_send()
        for cp in own:
            cp.wait()

    sems = pltpu.SemaphoreType.DMA((n, N_CHIPS - 1))
    return pl.pallas_call(
        body,
        name="gather_weights",
        in_specs=[ANY] * n,
        out_specs=[ANY] * n,
        out_shape=[jax.ShapeDtypeStruct((N_CHIPS,) + w.shape, w.dtype) for w in shards],
        scratch_shapes=[sems, sems, sems, sems, pltpu.SemaphoreType.DMA((n,))],
    )(*shards)


def _swap_halves(grads):
    n = len(grads)

    def body(*refs):
        ins, outs = refs[:n], refs[n : 2 * n]
        send_sems, recv_sems = refs[2 * n :]
        x, y, c = _mesh_position()
        copies = []
        for t in range(n):
            copies.append(pltpu.make_async_remote_copy(
                src_ref=ins[t].at[:, _half_rows(ins[t].shape[1], 1 - c), :], dst_ref=outs[t],
                send_sem=send_sems.at[t], recv_sem=recv_sems.at[t], device_id=(x, y, 1 - c), device_id_type=MESH_ID,
            ))
            copies[-1].start()
        for cp in copies:
            cp.wait()

    sems = pltpu.SemaphoreType.DMA((n,))
    return pl.pallas_call(
        body,
        name="swap_halves",
        in_specs=[ANY] * n,
        out_specs=[ANY] * n,
        out_shape=[jax.ShapeDtypeStruct((g.shape[0], g.shape[1] // 2, g.shape[2]), g.dtype) for g in grads],
        scratch_shapes=[sems, sems],
    )(*grads)


def _scatter_to_chips(parts):
    n = len(parts)

    def body(*refs):
        ins, outs = refs[:n], refs[n : 2 * n]
        send_sems, recv_sems = refs[2 * n :]
        x, y, c = _mesh_position()
        copies = []
        for t in range(n):
            for k, (px, py) in enumerate(_other_chips(x, y)):
                copies.append(pltpu.make_async_remote_copy(
                    src_ref=ins[t].at[2 * px + py], dst_ref=outs[t].at[k],
                    send_sem=send_sems.at[t, k], recv_sem=recv_sems.at[t, k], device_id=(px, py, c), device_id_type=MESH_ID,
                ))
                copies[-1].start()
        for cp in copies:
            cp.wait()

    sems = pltpu.SemaphoreType.DMA((n, N_CHIPS - 1))
    return pl.pallas_call(
        body,
        name="scatter_to_chips",
        in_specs=[ANY] * n,
        out_specs=[ANY] * n,
        out_shape=[jax.ShapeDtypeStruct((N_CHIPS - 1,) + p.shape[1:], p.dtype) for p in parts],
        scratch_shapes=[sems, sems],
    )(*parts)


def _join_halves(halves, n_layers):
    n = len(halves)
    per_layer = n // n_layers

    def body(*refs):
        ins, outs = refs[:n], refs[n : n + per_layer]
        send_sems, recv_sems, local_sems = refs[n + per_layer :]
        x, y, c = _mesh_position()
        local, remote = [], []
        for t in range(n):
            layer, tensor = divmod(t, per_layer)
            dst = outs[tensor].at[layer, _half_rows(outs[tensor].shape[1], c), :]
            local.append(pltpu.make_async_copy(ins[t], dst, local_sems.at[t]))
            local[-1].start()
            remote.append(pltpu.make_async_remote_copy(
                src_ref=ins[t], dst_ref=dst, send_sem=send_sems.at[t], recv_sem=recv_sems.at[t],
                device_id=(x, y, 1 - c), device_id_type=MESH_ID,
            ))
            remote[-1].start()
        for cp in remote:
            cp.wait()
        for cp in local:
            cp.wait()

    sems = pltpu.SemaphoreType.DMA((n,))
    return pl.pallas_call(
        body,
        name="join_halves",
        in_specs=[ANY] * n,
        out_specs=[ANY] * per_layer,
        out_shape=[jax.ShapeDtypeStruct((n_layers, 2 * h.shape[0], h.shape[1]), h.dtype) for h in halves[:per_layer]],
        scratch_shapes=[sems, sems, sems],
    )(*halves)


def _gather_small(pack):
    def body(p_ref, o_ref, send_sems, recv_sems, local_sem):
        x, y, c = _mesh_position()
        own = pltpu.make_async_copy(p_ref, o_ref.at[4 * x + 2 * y + c], local_sem)
        own.start()
        copies = []
        for k in range(1, N_DEV):
            px, py, pc = x ^ (k >> 2), y ^ ((k >> 1) & 1), c ^ (k & 1)
            send = pltpu.make_async_remote_copy(
                src_ref=p_ref, dst_ref=o_ref.at[4 * x + 2 * y + c], send_sem=send_sems.at[k - 1], recv_sem=recv_sems.at[k - 1],
                device_id=(px, py, pc), device_id_type=MESH_ID,
            )
            send.start()
            copies.append((send, 4 * px + 2 * py + pc))
        for send, peer_slot in copies:
            send.wait_send()
        for k in range(1, N_DEV):
            px, py, pc = x ^ (k >> 2), y ^ ((k >> 1) & 1), c ^ (k & 1)
            pltpu.make_async_remote_copy(
                src_ref=p_ref, dst_ref=o_ref.at[4 * px + 2 * py + pc], send_sem=send_sems.at[k - 1], recv_sem=recv_sems.at[k - 1],
                device_id=(px, py, pc), device_id_type=MESH_ID,
            ).wait_recv()
        own.wait()

    sems = pltpu.SemaphoreType.DMA((N_DEV - 1,))
    return pl.pallas_call(
        body,
        name="gather_small",
        in_specs=[VMEM_SPEC],
        out_specs=VMEM_SPEC,
        out_shape=jax.ShapeDtypeStruct((N_DEV,) + pack.shape, pack.dtype),
        scratch_shapes=[sems, sems, pltpu.SemaphoreType.DMA],
    )(pack)


def _row_tile(rows):
    for tile in (256, 128, 64, 32, 16, 8):
        if rows % tile == 0:
            return tile
    return rows


def _add_half(grad, received, half_index, name):
    slots, h, cdim = received.shape
    tile = _row_tile(h)
    per_half = h // tile

    def body(c_ref, g_ref, r_ref, o_ref):
        o_ref[...] = g_ref[...] + r_ref[...]

    grid_spec = pltpu.PrefetchScalarGridSpec(
        num_scalar_prefetch=1,
        grid=(slots, per_half),
        in_specs=[
            pl.BlockSpec((1, tile, cdim), lambda j, i, c: (j, c[0] * per_half + i, 0)),
            pl.BlockSpec((1, tile, cdim), lambda j, i, c: (j, i, 0)),
        ],
        out_specs=pl.BlockSpec((1, tile, cdim), lambda j, i, c: (j, i, 0)),
    )
    return pl.pallas_call(
        body, name=name, grid_spec=grid_spec, out_shape=jax.ShapeDtypeStruct(received.shape, F32), compiler_params=_params(2)
    )(half_index, grad, received)


def _add_chips(part, received, chip_index, name):
    _, h, cdim = part.shape
    tile = _row_tile(h)

    def body(j_ref, p_ref, r_ref, o_ref):
        o_ref[...] = ((p_ref[0] + r_ref[0]) + r_ref[1]) + r_ref[2]

    grid_spec = pltpu.PrefetchScalarGridSpec(
        num_scalar_prefetch=1,
        grid=(h // tile,),
        in_specs=[
            pl.BlockSpec((1, tile, cdim), lambda i, j: (j[0], i, 0)),
            pl.BlockSpec((N_CHIPS - 1, tile, cdim), lambda i, j: (0, i, 0)),
        ],
        out_specs=pl.BlockSpec((tile, cdim), lambda i, j: (i, 0)),
    )
    return pl.pallas_call(
        body, name=name, grid_spec=grid_spec, out_shape=jax.ShapeDtypeStruct((h, cdim), F32), compiler_params=_params(1)
    )(chip_index, part, received)


def _adamw(w, g, m, v, name):
    rows, cdim = w.shape
    tile = _row_tile(rows)

    def body(w_ref, g_ref, m_ref, v_ref, d_ref, nm_ref, nv_ref):
        gv = g_ref[...]
        nm = ADAM_B1 * m_ref[...] + (1.0 - ADAM_B1) * gv
        nv = ADAM_B2 * v_ref[...] + (1.0 - ADAM_B2) * (gv * gv)
        m_hat = nm / (1.0 - ADAM_B1 ** ADAM_STEP)
        v_hat = nv / (1.0 - ADAM_B2 ** ADAM_STEP)
        d_ref[...] = -ADAM_LR * (m_hat / (jnp.sqrt(v_hat) + ADAM_EPS) + ADAM_WD * w_ref[...])
        nm_ref[...] = nm
        nv_ref[...] = nv

    spec = pl.BlockSpec((tile, cdim), lambda i: (i, 0))
    shape = jax.ShapeDtypeStruct((rows, cdim), F32)
    return pl.pallas_call(
        body, name=name, grid=(rows // tile,), in_specs=[spec] * 4, out_specs=[spec] * 3, out_shape=[shape] * 3,
        compiler_params=_params(1),
    )(w, g, m, v)


SMALL_ROWS, SMALL_COLS = 24, 1024
ROW_NORM_MIX, ROW_NORM_FFN, ROW_LOSS, ROW_Q_NORM, ROW_K_NORM, ROW_CONV = 0, 2, 4, 8, 10, 16


def _sum_small(gathered):
    def body(g_ref, o_ref, heads_ref, lanes_ref):
        total = g_ref[0]
        for dev in range(1, N_DEV):
            total = total + g_ref[dev]
        o_ref[...] = total
        heads = o_ref[8:16, 0:LANES]
        for grp in range(1, ATTN_DIM // LANES):
            heads = heads + o_ref[8:16, grp * LANES : (grp + 1) * LANES]
        heads_ref[...] = heads + pltpu.roll(heads, HEAD_DIM, 1)
        lanes_ref[...] = jnp.broadcast_to(jnp.sum(o_ref[0:8, :], axis=-1, keepdims=True), (8, LANES))

    return pl.pallas_call(
        body,
        name="sum_small",
        in_specs=[VMEM_SPEC],
        out_specs=[VMEM_SPEC] * 3,
        out_shape=[jax.ShapeDtypeStruct((SMALL_ROWS, SMALL_COLS), F32), jax.ShapeDtypeStruct((8, LANES), F32), jax.ShapeDtypeStruct((8, LANES), F32)],
    )(gathered)


def _pad_rows(a, rows):
    return jnp.pad(a, ((0, rows - a.shape[0]), (0, 0)))


def _pad_to(a, rows, cols):
    return jnp.pad(a, ((0, rows - a.shape[0]), (0, cols - a.shape[1])))


def _local_step(x, target, norm_mix, q_norm, k_norm, norm_ffn, conv_full, win_s, wout_s, wg_s, wu_s, wd_s):
    n_layers = norm_mix.shape[0]
    s, d = x.shape
    tm = TOKEN_TILE
    n_in = win_s.shape[3]
    f = wg_s.shape[3]
    saved = []
    for l in range(n_layers):
        q_gain = jnp.tile(q_norm[l][None, :], (1, 2))
        k_gain = jnp.tile(k_norm[l][None, :], (1, 2))
        h1, proj = _norm_matmul(x, norm_mix[l][None, :], win_s, l, f"in_proj_{l}")
        qn, kn, vb = _qkv_prep(proj, q_gain, k_gain, f"qkv_prep_{l}")
        attn = _attn_fwd(qn, kn, vb, f"attn_fwd_{l}")
        conv = _conv_fwd(proj, conv_full[l], f"conv_fwd_{l}")
        x_mid = _out_proj(x, attn, conv, wout_s, l, f"out_proj_{l}")
        x_out = _ffn_fwd(x_mid, norm_ffn[l][None, :], wg_s, wu_s, wd_s, l, f"ffn_fwd_{l}")
        saved.append(dict(x=x, h1=h1, proj=proj, qn=qn, kn=kn, vb=vb, attn=attn, conv=conv, x_mid=x_mid, q_gain=q_gain, k_gain=k_gain))
        x = x_out

    dy, loss_lanes = _loss_grad(x, target, "loss_grad")
    grads = [None] * n_layers
    for l in reversed(range(n_layers)):
        sv = saved[l]
        dx_mid, d_norm_ffn, h2, dyb, dgate, dup, act = _ffn_bwd(sv["x_mid"], dy, norm_ffn[l][None, :], wg_s, wu_s, wd_s, l, f"ffn_bwd_{l}")
        tok2 = pl.BlockSpec((tm, d), lambda j, i: (i, 0))
        hid = pl.BlockSpec((1, tm, f), lambda j, i: (j, i, 0))
        d_wg = _wgrad(h2, dgate, tok2, hid, N_CHIPS, d, f, f"wgrad_gate_{l}")
        d_wu = _wgrad(h2, dup, tok2, hid, N_CHIPS, d, f, f"wgrad_up_{l}")
        d_wd = _wgrad(act, dyb, hid, tok2, N_CHIPS, f, d, f"wgrad_down_{l}")
        d_attn, d_conv, dxb = _out_proj_bwd(dx_mid, wout_s, l, f"out_proj_bwd_{l}")
        rows_out = wout_s.shape[2]
        mix_spec_a = pl.BlockSpec((tm, rows_out), lambda j, i: (i, j))
        d_wout_a = _wgrad(sv["attn"], dxb, mix_spec_a, tok2, ATTN_DIM // rows_out, rows_out, d, f"wgrad_out_attn_{l}")
        d_wout_c = _wgrad(sv["conv"], dxb, mix_spec_a, tok2, CONV_DIM // rows_out, rows_out, d, f"wgrad_out_conv_{l}")
        d_wout = jnp.concatenate([d_wout_a, d_wout_c], axis=0)
        dq, dk, dv = _attn_bwd(sv["qn"], sv["kn"], sv["vb"], d_attn, f"attn_bwd_{l}")
        dproj_a, d_qg, d_kg = _qkv_prep_bwd(sv["proj"], sv["q_gain"], sv["k_gain"], dq, dk, dv, f"qkv_prep_bwd_{l}")
        dproj_b, d_conv_w = _conv_bwd(sv["proj"], conv_full[l], d_conv, f"conv_bwd_{l}")
        dproj = jnp.concatenate([dproj_a, dproj_b], axis=1)
        d_win = _wgrad(sv["h1"], dproj, tok2, pl.BlockSpec((tm, n_in), lambda j, i: (i, j)), N_CHIPS, d, n_in, f"wgrad_in_{l}")
        dy, d_norm_mix = _in_proj_bwd(sv["x"], dx_mid, norm_mix[l][None, :], dproj, win_s, l, f"in_proj_bwd_{l}")
        grads[l] = dict(norm_mix=d_norm_mix, norm_ffn=d_norm_ffn, q_norm=d_qg, k_norm=d_kg, conv_w=d_conv_w,
                        w_in=d_win, w_out=d_wout, w_gate=d_wg, w_up=d_wu, w_down=d_wd)
    return loss_lanes, dy, grads


BIG = ("w_in", "w_out", "w_gate", "w_up", "w_down")


def kernel(x, norm_mix, w_in, q_norm, k_norm, conv_w, w_out, norm_ffn, w_gate, w_up, w_down, loss_target, m_norm_mix, m_w_in, m_q_norm, m_k_norm, m_conv_w, m_w_out, m_norm_ffn, m_w_gate, m_w_up, m_w_down, v_norm_mix, v_w_in, v_q_norm, v_k_norm, v_conv_w, v_w_out, v_norm_ffn, v_w_gate, v_w_up, v_w_down):
    n_layers = norm_mix.shape[0]
    weights = dict(w_in=w_in, w_out=w_out, w_gate=w_gate, w_up=w_up, w_down=w_down)
    moments_m = dict(w_in=m_w_in, w_out=m_w_out, w_gate=m_w_gate, w_up=m_w_up, w_down=m_w_down)
    moments_v = dict(w_in=v_w_in, w_out=v_w_out, w_gate=v_w_gate, w_up=v_w_up, w_down=v_w_down)
    cx, cy, cc = _mesh_position()
    chip_index = (2 * cx + cy).astype(jnp.int32).reshape(1)
    core_index = cc.astype(jnp.int32).reshape(1)

    conv_pad = jnp.pad(conv_w, ((0, 0), (0, 16 - conv_w.shape[1]), (0, 0)))
    gathered = _gather_weights([weights[k].astype(BF16) for k in BIG] + [conv_pad])
    win_s, wout_s, wg_s, wu_s, wd_s, conv_s = gathered
    conv_full = jnp.transpose(conv_s[:, :, 0:8], (1, 2, 0, 3)).reshape(n_layers, 8, N_CHIPS * conv_w.shape[2])

    loss_lanes, grad_x, grads = _local_step(
        x[0], loss_target[0], norm_mix, q_norm, k_norm, norm_ffn, conv_full, win_s, wout_s, wg_s, wu_s, wd_s)

    flat = [grads[l][k] for l in range(n_layers) for k in BIG]
    names = [f"{k}_{l}" for l in range(n_layers) for k in BIG]
    received = _swap_halves(flat)
    parts = [_add_half(g, r, core_index, f"add_half_{nm}") for g, r, nm in zip(flat, received, names)]
    from_chips = _scatter_to_chips(parts)
    halves = [_add_chips(p, r, chip_index, f"add_chips_{nm}") for p, r, nm in zip(parts, from_chips, names)]
    big_grads = dict(zip(BIG, _join_halves(halves, n_layers)))

    def lanes(a):
        return _pad_to(a, a.shape[0], SMALL_COLS)

    def tile_of(*groups):
        return _pad_rows(jnp.concatenate([lanes(jnp.concatenate(g, axis=0)) for g in groups], axis=0), 8)

    layers = range(n_layers)
    pack = jnp.concatenate([
        tile_of([grads[l]["norm_mix"] for l in layers], [grads[l]["norm_ffn"] for l in layers], [loss_lanes]),
        tile_of([grads[l]["q_norm"] for l in layers], [grads[l]["k_norm"] for l in layers]),
        tile_of([grads[l]["conv_w"][0:3] for l in layers]),
    ], axis=0)
    small, small_heads, small_lanes = _sum_small(_gather_small(pack))
    loss = small_lanes[ROW_LOSS, 0]
    d_model = norm_mix.shape[1]
    conv_cols = conv_w.shape[2]
    conv_all = small[ROW_CONV : ROW_CONV + 3 * n_layers, 0:CONV_DIM].reshape(n_layers, 3, CONV_DIM)
    small_grads = dict(
        norm_mix=small[ROW_NORM_MIX : ROW_NORM_MIX + n_layers, 0:d_model],
        norm_ffn=small[ROW_NORM_FFN : ROW_NORM_FFN + n_layers, 0:d_model],
        q_norm=small_heads[ROW_Q_NORM - 8 : ROW_Q_NORM - 8 + n_layers, 0:HEAD_DIM],
        k_norm=small_heads[ROW_K_NORM - 8 : ROW_K_NORM - 8 + n_layers, 0:HEAD_DIM],
        conv_w=lax.dynamic_slice_in_dim(conv_all, (2 * cx + cy) * conv_cols, conv_cols, axis=2),
    )

    out_grad, out_delta, out_m, out_v = {}, {}, {}, {}
    for k in BIG:
        shape = weights[k].shape
        view = (shape[0] * shape[1], shape[2])
        g = big_grads[k]
        delta, new_m, new_v = _adamw(weights[k].reshape(view), g.reshape(view), moments_m[k].reshape(view), moments_v[k].reshape(view), f"adamw_{k}")
        out_grad[k], out_delta[k], out_m[k], out_v[k] = g, delta.reshape(shape), new_m.reshape(shape), new_v.reshape(shape)

    small_w = dict(norm_mix=norm_mix, norm_ffn=norm_ffn, q_norm=q_norm, k_norm=k_norm, conv_w=conv_w)
    small_m = dict(norm_mix=m_norm_mix, norm_ffn=m_norm_ffn, q_norm=m_q_norm, k_norm=m_k_norm, conv_w=m_conv_w)
    small_v = dict(norm_mix=v_norm_mix, norm_ffn=v_norm_ffn, q_norm=v_q_norm, k_norm=v_k_norm, conv_w=v_conv_w)
    order = ("norm_mix", "norm_ffn", "q_norm", "k_norm", "conv_w")

    def packed(tree):
        parts2 = [_pad_to(tree[k].reshape(-1, tree[k].shape[-1]), tree[k].reshape(-1, tree[k].shape[-1]).shape[0], SMALL_COLS) for k in order]
        return _pad_rows(jnp.concatenate(parts2, axis=0), SMALL_ROWS)

    delta_p, m_p, v_p = _adamw(packed(small_w), packed(small_grads), packed(small_m), packed(small_v), "adamw_small")
    row = 0
    for k in order:
        shape = small_w[k].shape
        n_rows = 1
        for dim in shape[:-1]:
            n_rows *= dim
        cut = (slice(row, row + n_rows), slice(0, shape[-1]))
        out_grad[k] = small_grads[k]
        out_delta[k], out_m[k], out_v[k] = delta_p[cut].reshape(shape), m_p[cut].reshape(shape), v_p[cut].reshape(shape)
        row += n_rows

    names_out = ("norm_mix", "w_in", "q_norm", "k_norm", "conv_w", "w_out", "norm_ffn", "w_gate", "w_up", "w_down")
    return (loss, grad_x[None], *[out_grad[k] for k in names_out], *[out_delta[k] for k in names_out],
            *[out_m[k] for k in names_out], *[out_v[k] for k in names_out])
```

```python
import functools

import jax
import jax.numpy as jnp
from jax import lax
from jax.experimental import pallas as pl
from jax.experimental.pallas import tpu as pltpu

F32 = jnp.float32
BF16 = jnp.bfloat16

EPS = 1e-6
HEAD_DIM = 64
LANES = 128
ATTN_DIM = 512
CONV_DIM = 512
N_CHIPS = 4
N_DEV = 8
Q_SCALE = HEAD_DIM ** -0.5
ATTN_Q_TILE = 256
ATTN_TILE = 256
TOKEN_TILE = 512
VMEM_LIMIT = 56 * 1024 * 1024

ADAM_LR = 0.001
ADAM_B1 = 0.9
ADAM_B2 = 0.999
ADAM_EPS = 1e-08
ADAM_WD = 0.01
ADAM_STEP = 10

MESH_ID = pl.DeviceIdType.MESH
ANY = pl.BlockSpec(memory_space=pl.ANY)
VMEM_SPEC = pl.BlockSpec(memory_space=pltpu.VMEM)


def _params(n_axes):
    return pltpu.CompilerParams(dimension_semantics=("arbitrary",) * n_axes, vmem_limit_bytes=VMEM_LIMIT)


def _dot(a, b):
    return jnp.dot(a, b, preferred_element_type=F32)


def _dot_nt(a, b):
    return lax.dot_general(a, b, (((1,), (1,)), ((), ())), preferred_element_type=F32)


def _dot_tn(a, b):
    return lax.dot_general(a, b, (((0,), (0,)), ((), ())), preferred_element_type=F32)


def _softplus(z):
    return jnp.maximum(z, 0.0) + jnp.log(1.0 + jnp.exp(-jnp.abs(z)))


def _norm_matmul(x, gain, w_s, layer, name):
    s, d = x.shape
    n_blocks, _, _, n = w_s.shape
    tm = TOKEN_TILE

    def body(x_ref, g_ref, w_ref, h_ref, o_ref):
        @pl.when(pl.program_id(1) == 0)
        def _():
            xv = x_ref[...]
            r = lax.rsqrt(jnp.mean(xv * xv, axis=-1, keepdims=True) + EPS)
            h_ref[...] = (xv * r * g_ref[...]).astype(BF16)

        o_ref[...] = _dot(h_ref[...], w_ref[0, 0])

    return pl.pallas_call(
        body,
        name=name,
        grid=(s // tm, n_blocks),
        in_specs=[
            pl.BlockSpec((tm, d), lambda i, j: (i, 0)),
            pl.BlockSpec((1, d), lambda i, j: (0, 0)),
            pl.BlockSpec((1, 1, d, n), lambda i, j: (j, layer, 0, 0)),
        ],
        out_specs=[pl.BlockSpec((tm, d), lambda i, j: (i, 0)), pl.BlockSpec((tm, n), lambda i, j: (i, j))],
        out_shape=[jax.ShapeDtypeStruct((s, d), BF16), jax.ShapeDtypeStruct((s, n_blocks * n), F32)],
        compiler_params=_params(2),
    )(x, gain, w_s)


def _head_norm(xv, gain, low):
    sq = xv * xv
    s_low = jnp.sum(jnp.where(low, sq, 0.0), axis=-1, keepdims=True)
    s_high = jnp.sum(jnp.where(low, 0.0, sq), axis=-1, keepdims=True)
    r = jnp.where(low, lax.rsqrt(s_low / HEAD_DIM + EPS), lax.rsqrt(s_high / HEAD_DIM + EPS))
    return xv * r * gain, r


def _qkv_prep(proj, q_gain, k_gain, name):
    s = proj.shape[0]
    tm = TOKEN_TILE

    def body(p_ref, qg_ref, kg_ref, q_ref, k_ref, v_ref):
        low = lax.broadcasted_iota(jnp.int32, (tm, LANES), 1) < HEAD_DIM
        for g in range(ATTN_DIM // LANES):
            cq = slice(LANES * g, LANES * (g + 1))
            ck = slice(ATTN_DIM + LANES * g, ATTN_DIM + LANES * (g + 1))
            cv = slice(2 * ATTN_DIM + LANES * g, 2 * ATTN_DIM + LANES * (g + 1))
            qn, _ = _head_norm(p_ref[:, cq], qg_ref[...], low)
            kn, _ = _head_norm(p_ref[:, ck], kg_ref[...], low)
            q_ref[:, cq] = (qn * Q_SCALE).astype(BF16)
            k_ref[:, cq] = kn.astype(BF16)
            v_ref[:, cq] = p_ref[:, cv].astype(BF16)

    out = jax.ShapeDtypeStruct((s, ATTN_DIM), BF16)
    return pl.pallas_call(
        body,
        name=name,
        grid=(s // tm,),
        in_specs=[
            pl.BlockSpec((tm, 3 * ATTN_DIM), lambda i: (i, 0)),
            pl.BlockSpec((1, LANES), lambda i: (0, 0)),
            pl.BlockSpec((1, LANES), lambda i: (0, 0)),
        ],
        out_specs=[pl.BlockSpec((tm, ATTN_DIM), lambda i: (i, 0))] * 3,
        out_shape=[out, out, out],
        compiler_params=_params(1),
    )(proj, q_gain, k_gain)


def _qkv_prep_bwd(proj, q_gain, k_gain, dq, dk, dv, name):
    s = proj.shape[0]
    tm = TOKEN_TILE

    def norm_bwd(xv, gain, dy, low):
        _, r = _head_norm(xv, gain, low)
        xhat = xv * r
        dxhat = dy * gain
        prod = dxhat * xhat
        m_low = jnp.sum(jnp.where(low, prod, 0.0), axis=-1, keepdims=True)
        m_high = jnp.sum(jnp.where(low, 0.0, prod), axis=-1, keepdims=True)
        mean = jnp.where(low, m_low, m_high) / HEAD_DIM
        return r * (dxhat - xhat * mean), jnp.sum(dy * xhat, axis=0, keepdims=True)

    def body(p_ref, qg_ref, kg_ref, dq_ref, dk_ref, dv_ref, dp_ref, dqg_ref, dkg_ref):
        @pl.when(pl.program_id(0) == 0)
        def _():
            dqg_ref[...] = jnp.zeros_like(dqg_ref)
            dkg_ref[...] = jnp.zeros_like(dkg_ref)

        low = lax.broadcasted_iota(jnp.int32, (tm, LANES), 1) < HEAD_DIM
        for g in range(ATTN_DIM // LANES):
            cq = slice(LANES * g, LANES * (g + 1))
            ck = slice(ATTN_DIM + LANES * g, ATTN_DIM + LANES * (g + 1))
            cv = slice(2 * ATTN_DIM + LANES * g, 2 * ATTN_DIM + LANES * (g + 1))
            dxq, dgq = norm_bwd(p_ref[:, cq], qg_ref[...], dq_ref[:, cq] * Q_SCALE, low)
            dxk, dgk = norm_bwd(p_ref[:, ck], kg_ref[...], dk_ref[:, cq], low)
            dp_ref[:, cq] = dxq.astype(BF16)
            dp_ref[:, ck] = dxk.astype(BF16)
            dp_ref[:, cv] = dv_ref[:, cq].astype(BF16)
            dqg_ref[:, cq] += dgq
            dkg_ref[:, cq] += dgk

    grad_spec = pl.BlockSpec((tm, ATTN_DIM), lambda i: (i, 0))
    gain_spec = pl.BlockSpec((1, LANES), lambda i: (0, 0))
    sum_spec = pl.BlockSpec((1, ATTN_DIM), lambda i: (0, 0))
    return pl.pallas_call(
        body,
        name=name,
        grid=(s // tm,),
        in_specs=[pl.BlockSpec((tm, 3 * ATTN_DIM), lambda i: (i, 0)), gain_spec, gain_spec, grad_spec, grad_spec, grad_spec],
        out_specs=[pl.BlockSpec((tm, 3 * ATTN_DIM), lambda i: (i, 0)), sum_spec, sum_spec],
        out_shape=[
            jax.ShapeDtypeStruct((s, 3 * ATTN_DIM), BF16),
            jax.ShapeDtypeStruct((1, ATTN_DIM), F32),
            jax.ShapeDtypeStruct((1, ATTN_DIM), F32),
        ],
        compiler_params=_params(1),
    )(proj, q_gain, k_gain, dq, dk, dv)


def _attn_tile_consts(t):
    row = lax.broadcasted_iota(jnp.int32, (t, t), 0)
    col = lax.broadcasted_iota(jnp.int32, (t, t), 1)
    return row, col


def _triangle_sum(v, triangle):
    return _dot(v.astype(BF16), triangle)


def _attn_fwd(qn, kn, vb, name):
    s = qn.shape[0]
    t = min(ATTN_TILE, s)
    tq = min(ATTN_Q_TILE, t)
    per_key_tile = t // tq

    def body(q_ref, k_ref, v_ref, o_ref):
        i = pl.program_id(1) // per_key_tile
        low = lax.broadcasted_iota(jnp.int32, (tq, LANES), 1) < HEAD_DIM
        row, col = _attn_tile_consts(t)
        suffix = (row > col).astype(BF16)
        first_row = (pl.program_id(1) % per_key_tile) * tq
        causal = lax.broadcasted_iota(jnp.int32, (tq, t), 1) < lax.broadcasted_iota(jnp.int32, (tq, t), 0) + first_row
        q = q_ref[...]
        zero_q = jnp.zeros_like(q)
        qh = (jnp.where(low, q, zero_q), jnp.where(low, zero_q, q))

        def step(kbs, carry, diagonal=False):
            chains = [(head, m) for head in range(2) for m in range(len(kbs))]
            ks = [k_ref[pl.ds(pl.multiple_of(kb * t, t), t), :] for kb in kbs]
            vs = [v_ref[pl.ds(pl.multiple_of(kb * t, t), t), :] for kb in kbs]
            z = [_dot_nt(qh[head], ks[kb]) for head, kb in chains]
            sp = [_softplus(zc) for zc in z]
            if diagonal:
                sp = [jnp.where(causal, s_, 0.0) for s_ in sp]
            inside = [_triangle_sum(s_, suffix) for s_ in sp]
            after = [carry[head][1] for head in range(2)]
            log_a = []
            for n, (head, kb) in enumerate(chains):
                log_a.append(z[n] - sp[n] - inside[n] - after[head])
                after[head] = after[head] + jnp.sum(sp[n], axis=-1, keepdims=True)
            a = [jnp.exp(l_) for l_ in log_a]
            if diagonal:
                a = [jnp.where(causal, a_, 0.0) for a_ in a]
            acc = [carry[head][0] for head in range(2)]
            for n, (head, kb) in enumerate(chains):
                acc[head] = acc[head] + _dot(a[n].astype(BF16), vs[kb])
            return tuple((acc[head], after[head]) for head in range(2))

        zero = (jnp.zeros((tq, LANES), F32), jnp.zeros((tq, 1), F32))
        carry = step((i,), (zero, zero), True)
        carry = lax.cond(i % 2 == 1, lambda c: step((i - 1,), c), lambda c: c, carry)
        pairs = i // 2
        carry = lax.fori_loop(0, pairs, lambda n, c: step((2 * (pairs - n) - 1, 2 * (pairs - n) - 2), c), carry)
        o_ref[...] = jnp.where(low, carry[0][0], carry[1][0]).astype(BF16)

    return pl.pallas_call(
        body,
        name=name,
        grid=(ATTN_DIM // LANES, s // tq),
        in_specs=[
            pl.BlockSpec((tq, LANES), lambda p, i: (i, p)),
            pl.BlockSpec((s, LANES), lambda p, i: (0, p)),
            pl.BlockSpec((s, LANES), lambda p, i: (0, p)),
        ],
        out_specs=pl.BlockSpec((tq, LANES), lambda p, i: (i, p)),
        out_shape=jax.ShapeDtypeStruct((s, ATTN_DIM), BF16),
        compiler_params=_params(2),
    )(qn, kn, vb)


def _attn_bwd(qn, kn, vb, do, name):
    s = qn.shape[0]
    t = min(ATTN_TILE, s)
    nq = s // t

    def body(q_ref, k_ref, v_ref, do_ref, dq_ref, dk_ref, dv_ref, a_s, sg_s):
        i = pl.program_id(1)

        @pl.when(i == 0)
        def _():
            dk_ref[...] = jnp.zeros_like(dk_ref)
            dv_ref[...] = jnp.zeros_like(dv_ref)

        low = lax.broadcasted_iota(jnp.int32, (t, LANES), 1) < HEAD_DIM
        row, col = _attn_tile_consts(t)
        suffix = (row > col).astype(BF16)
        prefix = (row < col).astype(BF16)
        causal = col < row
        q = q_ref[...]
        dob = do_ref[...]
        zero_q = jnp.zeros_like(q)
        pairs = i // 2
        heads = []
        for head in range(2):
            if head == 0:
                qh, doh = jnp.where(low, q, zero_q), jnp.where(low, dob, zero_q)
            else:
                qh, doh = jnp.where(low, zero_q, q), jnp.where(low, zero_q, dob)

            def rows_of(kb):
                return pl.ds(pl.multiple_of(kb * t, t), t)

            def pass1(kbs, after, diagonal=False):
                z = [_dot_nt(qh, k_ref[rows_of(kb), :]) for kb in kbs]
                sp = [_softplus(z_) for z_ in z]
                if diagonal:
                    sp = [jnp.where(causal, s_, 0.0) for s_ in sp]
                inside = [_triangle_sum(s_, suffix) for s_ in sp]
                for n, kb in enumerate(kbs):
                    log_sg = z[n] - sp[n]
                    a = jnp.exp(log_sg - inside[n] - after)
                    sg = jnp.exp(log_sg)
                    if diagonal:
                        a = jnp.where(causal, a, 0.0)
                        sg = jnp.where(causal, sg, 0.0)
                    a_s[kb] = a
                    sg_s[kb] = sg
                    after = after + jnp.sum(sp[n], axis=-1, keepdims=True)
                return after

            after = pass1((i,), jnp.zeros((t, 1), F32), True)
            after = lax.cond(i % 2 == 1, lambda c: pass1((i - 1,), c), lambda c: c, after)
            lax.fori_loop(0, pairs, lambda n, c: pass1((2 * (pairs - n) - 1, 2 * (pairs - n) - 2), c), after)

            def pass2(kbs, carry):
                dq, before = carry
                ks = [k_ref[rows_of(kb), :] for kb in kbs]
                a = [a_s[kb] for kb in kbs]
                g = [a_ * _dot_nt(doh, v_ref[rows_of(kb), :]) for a_, kb in zip(a, kbs)]
                for n, kb in enumerate(kbs):
                    dv_ref[rows_of(kb), :] += _dot_tn(a[n].astype(BF16), doh)
                inside = [_triangle_sum(g_, prefix) for g_ in g]
                dz = []
                for n, kb in enumerate(kbs):
                    sg = sg_s[kb]
                    dz.append((g[n] * (1.0 - sg) - sg * (inside[n] + before)).astype(BF16))
                    before = before + jnp.sum(g[n], axis=-1, keepdims=True)
                for n, kb in enumerate(kbs):
                    dk_ref[rows_of(kb), :] += _dot_tn(dz[n], qh)
                for n in range(len(kbs)):
                    dq = dq + _dot(dz[n], ks[n])
                return dq, before

            carry = (jnp.zeros((t, LANES), F32), jnp.zeros((t, 1), F32))
            carry = lax.fori_loop(0, (i + 1) // 2, lambda n, c: pass2((2 * n, 2 * n + 1), c), carry)
            carry = lax.cond(i % 2 == 0, lambda c: pass2((i,), c), lambda c: c, carry)
            heads.append(carry[0])
        dq_ref[...] = jnp.where(low, heads[0], heads[1])

    q_spec = pl.BlockSpec((t, LANES), lambda p, i: (i, p))
    kv_spec = pl.BlockSpec((s, LANES), lambda p, i: (0, p))
    return pl.pallas_call(
        body,
        name=name,
        grid=(ATTN_DIM // LANES, nq),
        in_specs=[q_spec, kv_spec, kv_spec, q_spec],
        out_specs=[q_spec, kv_spec, kv_spec],
        out_shape=[jax.ShapeDtypeStruct((s, ATTN_DIM), F32)] * 3,
        scratch_shapes=[pltpu.VMEM((nq, t, t), F32), pltpu.VMEM((nq, t, t), F32)],
        compiler_params=_params(2),
    )(qn, kn, vb, do)


CB_BLOCK, CC_BLOCK, CU_BLOCK = 3, 4, 5


def _shift_down(h, prev_rows, n):
    row = lax.broadcasted_iota(jnp.int32, h.shape, 0)
    out = pltpu.roll(h, n, 0)
    for r in range(n):
        out = jnp.where(row == r, prev_rows[len(prev_rows) - n + r], out)
    return out


def _shift_up(h, next_rows, n):
    tm = h.shape[0]
    row = lax.broadcasted_iota(jnp.int32, h.shape, 0)
    out = pltpu.roll(h, tm - n, 0)
    for r in range(n):
        out = jnp.where(row == tm - n + r, next_rows[r], out)
    return out


def _conv_fwd(proj, conv_w, name):
    s = proj.shape[0]
    tm = TOKEN_TILE
    nb = tm // 8

    def body(cb_ref, cc_ref, cu_ref, pc_ref, pu_ref, w_ref, o_ref):
        first = pl.program_id(0) == 0
        h = cc_ref[...] * cu_ref[...]
        prev = [jnp.where(first, 0.0, pc_ref[r : r + 1, :] * pu_ref[r : r + 1, :]) for r in (6, 7)]
        y = w_ref[0:1, :] * _shift_down(h, prev, 2) + w_ref[1:2, :] * _shift_down(h, prev, 1) + w_ref[2:3, :] * h
        o_ref[...] = (cb_ref[...] * y).astype(BF16)

    def col(block):
        return pl.BlockSpec((tm, CONV_DIM), lambda i: (i, block))

    def halo(block):
        return pl.BlockSpec((8, CONV_DIM), lambda i: (jnp.maximum(i * nb - 1, 0), block))

    return pl.pallas_call(
        body,
        name=name,
        grid=(s // tm,),
        in_specs=[col(CB_BLOCK), col(CC_BLOCK), col(CU_BLOCK), halo(CC_BLOCK), halo(CU_BLOCK), pl.BlockSpec((8, CONV_DIM), lambda i: (0, 0))],
        out_specs=pl.BlockSpec((tm, CONV_DIM), lambda i: (i, 0)),
        out_shape=jax.ShapeDtypeStruct((s, CONV_DIM), BF16),
        compiler_params=_params(1),
    )(proj, proj, proj, proj, proj, conv_w)


def _conv_bwd(proj, conv_w, dconv, name):
    s = proj.shape[0]
    tm = TOKEN_TILE
    nb = tm // 8
    n_tiles = s // tm

    def body(cb_ref, cc_ref, cu_ref, dy_ref, pc_ref, pu_ref, nb_ref, ndy_ref, w_ref, dp_ref, dw_ref):
        i = pl.program_id(0)

        @pl.when(i == 0)
        def _():
            dw_ref[...] = jnp.zeros_like(dw_ref)

        first = i == 0
        last = i == n_tiles - 1
        cc, cu, cb, dy = cc_ref[...], cu_ref[...], cb_ref[...], dy_ref[...]
        h = cc * cu
        prev = [jnp.where(first, 0.0, pc_ref[r : r + 1, :] * pu_ref[r : r + 1, :]) for r in (6, 7)]
        h1 = _shift_down(h, prev, 1)
        h2 = _shift_down(h, prev, 2)
        y = w_ref[0:1, :] * h2 + w_ref[1:2, :] * h1 + w_ref[2:3, :] * h
        dyb = dy * cb
        nxt = [jnp.where(last, 0.0, ndy_ref[r : r + 1, :] * nb_ref[r : r + 1, :]) for r in (0, 1)]
        dh = w_ref[2:3, :] * dyb + w_ref[1:2, :] * _shift_up(dyb, nxt, 1) + w_ref[0:1, :] * _shift_up(dyb, nxt, 2)
        dp_ref[:, 0:CONV_DIM] = (dy * y).astype(BF16)
        dp_ref[:, CONV_DIM : 2 * CONV_DIM] = (dh * cu).astype(BF16)
        dp_ref[:, 2 * CONV_DIM : 3 * CONV_DIM] = (dh * cc).astype(BF16)
        dw_ref[0:1, :] += jnp.sum(dyb * h2, axis=0, keepdims=True)
        dw_ref[1:2, :] += jnp.sum(dyb * h1, axis=0, keepdims=True)
        dw_ref[2:3, :] += jnp.sum(dyb * h, axis=0, keepdims=True)

    def col(block):
        return pl.BlockSpec((tm, CONV_DIM), lambda i: (i, block))

    def halo_prev(block):
        return pl.BlockSpec((8, CONV_DIM), lambda i: (jnp.maximum(i * nb - 1, 0), block))

    def halo_next(block):
        return pl.BlockSpec((8, CONV_DIM), lambda i: (jnp.minimum((i + 1) * nb, s // 8 - 1), block))

    return pl.pallas_call(
        body,
        name=name,
        grid=(n_tiles,),
        in_specs=[
            col(CB_BLOCK), col(CC_BLOCK), col(CU_BLOCK), col(0),
            halo_prev(CC_BLOCK), halo_prev(CU_BLOCK), halo_next(CB_BLOCK), halo_next(0),
            pl.BlockSpec((8, CONV_DIM), lambda i: (0, 0)),
        ],
        out_specs=[pl.BlockSpec((tm, 3 * CONV_DIM), lambda i: (i, 0)), pl.BlockSpec((8, CONV_DIM), lambda i: (0, 0))],
        out_shape=[jax.ShapeDtypeStruct((s, 3 * CONV_DIM), BF16), jax.ShapeDtypeStruct((8, CONV_DIM), F32)],
        compiler_params=_params(1),
    )(proj, proj, proj, dconv, proj, proj, proj, dconv, conv_w)


def _out_proj(x, attn, conv, w_s, layer, name):
    s, d = x.shape
    tm = TOKEN_TILE
    rows = w_s.shape[2]

    def body(x_ref, a_ref, c_ref, w_ref, o_ref):
        acc = x_ref[...]
        for j in range(N_CHIPS):
            src = a_ref if j < 2 else c_ref
            cols = slice((j % 2) * rows, (j % 2 + 1) * rows)
            acc = acc + _dot(src[:, cols], w_ref[j, 0])
        o_ref[...] = acc

    return pl.pallas_call(
        body,
        name=name,
        grid=(s // tm,),
        in_specs=[
            pl.BlockSpec((tm, d), lambda i: (i, 0)),
            pl.BlockSpec((tm, ATTN_DIM), lambda i: (i, 0)),
            pl.BlockSpec((tm, CONV_DIM), lambda i: (i, 0)),
            pl.BlockSpec((N_CHIPS, 1, rows, d), lambda i: (0, layer, 0, 0)),
        ],
        out_specs=pl.BlockSpec((tm, d), lambda i: (i, 0)),
        out_shape=jax.ShapeDtypeStruct((s, d), F32),
        compiler_params=_params(1),
    )(x, attn, conv, w_s)


def _out_proj_bwd(dx, w_s, layer, name):
    s, d = dx.shape
    tm = TOKEN_TILE
    rows = w_s.shape[2]

    def body(dx_ref, w_ref, da_ref, dc_ref, dxb_ref):
        dxb = dx_ref[...].astype(BF16)
        dxb_ref[...] = dxb
        for j in range(N_CHIPS):
            cols = slice((j % 2) * rows, (j % 2 + 1) * rows)
            part = _dot_nt(dxb, w_ref[j, 0])
            if j < 2:
                da_ref[:, cols] = part.astype(BF16)
            else:
                dc_ref[:, cols] = part

    return pl.pallas_call(
        body,
        name=name,
        grid=(s // tm,),
        in_specs=[pl.BlockSpec((tm, d), lambda i: (i, 0)), pl.BlockSpec((N_CHIPS, 1, rows, d), lambda i: (0, layer, 0, 0))],
        out_specs=[
            pl.BlockSpec((tm, ATTN_DIM), lambda i: (i, 0)),
            pl.BlockSpec((tm, CONV_DIM), lambda i: (i, 0)),
            pl.BlockSpec((tm, d), lambda i: (i, 0)),
        ],
        out_shape=[
            jax.ShapeDtypeStruct((s, ATTN_DIM), BF16),
            jax.ShapeDtypeStruct((s, CONV_DIM), F32),
            jax.ShapeDtypeStruct((s, d), BF16),
        ],
        compiler_params=_params(1),
    )(dx, w_s)


def _ffn_fwd(x, gain, wg_s, wu_s, wd_s, layer, name):
    s, d = x.shape
    tm = TOKEN_TILE
    f = wg_s.shape[3]

    def body(x_ref, g_ref, wg_ref, wu_ref, wd_ref, o_ref, h_s):
        j = pl.program_id(1)

        @pl.when(j == 0)
        def _():
            xv = x_ref[...]
            r = lax.rsqrt(jnp.mean(xv * xv, axis=-1, keepdims=True) + EPS)
            h_s[...] = (xv * r * g_ref[...]).astype(BF16)
            o_ref[...] = xv

        h = h_s[...]
        gate = _dot(h, wg_ref[0, 0])
        up = _dot(h, wu_ref[0, 0])
        act = (gate / (1.0 + jnp.exp(-gate))) * up
        o_ref[...] += _dot(act.astype(BF16), wd_ref[0, 0])

    return pl.pallas_call(
        body,
        name=name,
        grid=(s // tm, N_CHIPS),
        in_specs=[
            pl.BlockSpec((tm, d), lambda i, j: (i, 0)),
            pl.BlockSpec((1, d), lambda i, j: (0, 0)),
            pl.BlockSpec((1, 1, d, f), lambda i, j: (j, layer, 0, 0)),
            pl.BlockSpec((1, 1, d, f), lambda i, j: (j, layer, 0, 0)),
            pl.BlockSpec((1, 1, f, d), lambda i, j: (j, layer, 0, 0)),
        ],
        out_specs=pl.BlockSpec((tm, d), lambda i, j: (i, 0)),
        out_shape=jax.ShapeDtypeStruct((s, d), F32),
        scratch_shapes=[pltpu.VMEM((tm, d), BF16)],
        compiler_params=_params(2),
    )(x, gain, wg_s, wu_s, wd_s)


def _rms_bwd(xv, gain, dh):
    r = lax.rsqrt(jnp.mean(xv * xv, axis=-1, keepdims=True) + EPS)
    xhat = xv * r
    dxhat = dh * gain
    dx = r * (dxhat - xhat * jnp.mean(dxhat * xhat, axis=-1, keepdims=True))
    return dx, jnp.sum(dh * xhat, axis=0, keepdims=True)


def _ffn_bwd(x, dy, gain, wg_s, wu_s, wd_s, layer, name):
    s, d = x.shape
    tm = TOKEN_TILE
    f = wg_s.shape[3]

    def body(x_ref, dy_ref, g_ref, wg_ref, wu_ref, wd_ref, dx_ref, dgain_ref, h_ref, dyb_ref, dg_ref, du_ref, act_ref, acc_s):
        i, j = pl.program_id(0), pl.program_id(1)

        @pl.when((i == 0) & (j == 0))
        def _():
            dgain_ref[...] = jnp.zeros_like(dgain_ref)

        @pl.when(j == 0)
        def _():
            xv = x_ref[...]
            r = lax.rsqrt(jnp.mean(xv * xv, axis=-1, keepdims=True) + EPS)
            h_ref[...] = (xv * r * g_ref[...]).astype(BF16)
            dyb_ref[...] = dy_ref[...].astype(BF16)
            acc_s[...] = jnp.zeros_like(acc_s)

        h = h_ref[...]
        gate = _dot(h, wg_ref[0, 0])
        up = _dot(h, wu_ref[0, 0])
        sig = 1.0 / (1.0 + jnp.exp(-gate))
        silu = gate * sig
        dact = _dot_nt(dyb_ref[...], wd_ref[0, 0])
        dgate = (dact * up * (sig * (1.0 + gate * (1.0 - sig)))).astype(BF16)
        dup = (dact * silu).astype(BF16)
        act_ref[0] = (silu * up).astype(BF16)
        dg_ref[0] = dgate
        du_ref[0] = dup
        acc_s[...] += _dot_nt(dgate, wg_ref[0, 0]) + _dot_nt(dup, wu_ref[0, 0])

        @pl.when(j == N_CHIPS - 1)
        def _():
            dxn, dgain = _rms_bwd(x_ref[...], g_ref[...], acc_s[...])
            dx_ref[...] = dy_ref[...] + dxn
            dgain_ref[...] += dgain

    tok = pl.BlockSpec((tm, d), lambda i, j: (i, 0))
    vec = pl.BlockSpec((1, d), lambda i, j: (0, 0))
    hid = pl.BlockSpec((1, tm, f), lambda i, j: (j, i, 0))
    hid_shape = jax.ShapeDtypeStruct((N_CHIPS, s, f), BF16)
    return pl.pallas_call(
        body,
        name=name,
        grid=(s // tm, N_CHIPS),
        in_specs=[
            tok, tok, vec,
            pl.BlockSpec((1, 1, d, f), lambda i, j: (j, layer, 0, 0)),
            pl.BlockSpec((1, 1, d, f), lambda i, j: (j, layer, 0, 0)),
            pl.BlockSpec((1, 1, f, d), lambda i, j: (j, layer, 0, 0)),
        ],
        out_specs=[tok, vec, tok, tok, hid, hid, hid],
        out_shape=[
            jax.ShapeDtypeStruct((s, d), F32),
            jax.ShapeDtypeStruct((1, d), F32),
            jax.ShapeDtypeStruct((s, d), BF16),
            jax.ShapeDtypeStruct((s, d), BF16),
            hid_shape, hid_shape, hid_shape,
        ],
        scratch_shapes=[pltpu.VMEM((tm, d), F32)],
        compiler_params=_params(2),
    )(x, dy, gain, wg_s, wu_s, wd_s)


def _in_proj_bwd(x, dx_res, gain, dproj, w_s, layer, name):
    s, d = x.shape
    tm = TOKEN_TILE
    n = w_s.shape[3]

    def body(x_ref, r_ref, g_ref, dp_ref, w_ref, dx_ref, dgain_ref, acc_s):
        i, j = pl.program_id(0), pl.program_id(1)

        @pl.when((i == 0) & (j == 0))
        def _():
            dgain_ref[...] = jnp.zeros_like(dgain_ref)

        @pl.when(j == 0)
        def _():
            acc_s[...] = jnp.zeros_like(acc_s)

        acc_s[...] += _dot_nt(dp_ref[...], w_ref[0, 0])

        @pl.when(j == N_CHIPS - 1)
        def _():
            dxn, dgain = _rms_bwd(x_ref[...], g_ref[...], acc_s[...])
            dx_ref[...] = r_ref[...] + dxn
            dgain_ref[...] += dgain

    tok = pl.BlockSpec((tm, d), lambda i, j: (i, 0))
    vec = pl.BlockSpec((1, d), lambda i, j: (0, 0))
    return pl.pallas_call(
        body,
        name=name,
        grid=(s // tm, N_CHIPS),
        in_specs=[tok, tok, vec, pl.BlockSpec((tm, n), lambda i, j: (i, j)), pl.BlockSpec((1, 1, d, n), lambda i, j: (j, layer, 0, 0))],
        out_specs=[tok, vec],
        out_shape=[jax.ShapeDtypeStruct((s, d), F32), jax.ShapeDtypeStruct((1, d), F32)],
        scratch_shapes=[pltpu.VMEM((tm, d), F32)],
        compiler_params=_params(2),
    )(x, dx_res, gain, dproj, w_s)


def _loss_grad(y, target, name):
    s, d = y.shape
    tm = TOKEN_TILE

    def body(y_ref, t_ref, dy_ref, l_ref):
        @pl.when(pl.program_id(0) == 0)
        def _():
            l_ref[...] = jnp.zeros_like(l_ref)

        err = y_ref[...] - t_ref[...]
        dy_ref[...] = err / d
        l_ref[...] += jnp.sum(err * err, axis=0, keepdims=True) * (0.5 / d)

    tok = pl.BlockSpec((tm, d), lambda i: (i, 0))
    return pl.pallas_call(
        body,
        name=name,
        grid=(s // tm,),
        in_specs=[tok, tok],
        out_specs=[tok, pl.BlockSpec((1, d), lambda i: (0, 0))],
        out_shape=[jax.ShapeDtypeStruct((s, d), F32), jax.ShapeDtypeStruct((1, d), F32)],
        compiler_params=_params(1),
    )(y, target)


def _wgrad(a, b, a_spec, b_spec, n_blocks, k, n, name):
    n_tiles = (a.shape[-2]) // TOKEN_TILE

    def body(a_ref, b_ref, o_ref):
        @pl.when(pl.program_id(1) == 0)
        def _():
            o_ref[...] = jnp.zeros_like(o_ref)

        av = a_ref[0] if len(a_ref.shape) == 3 else a_ref[...]
        bv = b_ref[0] if len(b_ref.shape) == 3 else b_ref[...]
        o_ref[0] += _dot_tn(av, bv)

    return pl.pallas_call(
        body,
        name=name,
        grid=(n_blocks, n_tiles),
        in_specs=[a_spec, b_spec],
        out_specs=pl.BlockSpec((1, k, n), lambda j, i: (j, 0, 0)),
        out_shape=jax.ShapeDtypeStruct((n_blocks, k, n), F32),
        compiler_params=_params(2),
    )(a, b)


def _mesh_position():
    return lax.axis_index("x"), lax.axis_index("y"), lax.axis_index("c")


def _other_chips(x, y):
    return [(1 - x, y), (x, 1 - y), (1 - x, 1 - y)]


def _half_rows(ref_rows, c):
    half = ref_rows // 2
    return pl.ds(c * half, half)


def _gather_weights(shards):
    n = len(shards)

    def body(*refs):
        ins, outs = refs[:n], refs[n : 2 * n]
        send_sems, recv_sems, pass_send_sems, pass_recv_sems, local_sems = refs[2 * n :]
        x, y, c = _mesh_position()
        me = 2 * x + y
        sibling = (x, y, 1 - c)
        chips = _other_chips(x, y)

        def block(t, chip_index, core):
            return outs[t].at[chip_index, :, _half_rows(ins[t].shape[1], core), :]

        def copy(t, k, chip_index, core, to, sems, src=None):
            dst = block(t, chip_index, core)
            return pltpu.make_async_remote_copy(
                src_ref=dst if src is None else src, dst_ref=dst, send_sem=sems[0].at[t, k], recv_sem=sems[1].at[t, k],
                device_id=to, device_id_type=MESH_ID,
            )

        ici, d2d = (send_sems, recv_sems), (pass_send_sems, pass_recv_sems)
        own = [pltpu.make_async_copy(ins[t], outs[t].at[me], local_sems.at[t]) for t in range(n)]
        for cp in own:
            cp.start()
        started = []
        for t in range(n):
            mine = ins[t].at[:, _half_rows(ins[t].shape[1], c), :]
            for k, (px, py) in enumerate(chips):
                started.append(copy(t, k, me, c, (px, py, c), ici, src=mine))
                started[-1].start()
        for t in range(n):
            for k, (px, py) in enumerate(chips):
                copy(t, k, 2 * px + py, c, sibling, ici).wait_recv()
                started.append(copy(t, k, 2 * px + py, c, sibling, d2d))
                started[-1].start()
        for t in range(n):
            for k, (px, py) in enumerate(chips):
                copy(t, k, 2 * px + py, 1 - c, sibling, d2d).wait_recv()
        for cp in started:
            cp.wait_send()
        for cp in own:
            cp.wait()

    sems = pltpu.SemaphoreType.DMA((n, N_CHIPS - 1))
    return pl.pallas_call(
        body,
        name="gather_weights",
        in_specs=[ANY] * n,
        out_specs=[ANY] * n,
        out_shape=[jax.ShapeDtypeStruct((N_CHIPS,) + w.shape, w.dtype) for w in shards],
        scratch_shapes=[sems, sems, sems, sems, pltpu.SemaphoreType.DMA((n,))],
    )(*shards)


def _swap_halves(grads):
    n = len(grads)

    def body(*refs):
        ins, outs = refs[:n], refs[n : 2 * n]
        send_sems, recv_sems = refs[2 * n :]
        x, y, c = _mesh_position()
        copies = []
        for t in range(n):
            copies.append(pltpu.make_async_remote_copy(
                src_ref=ins[t].at[:, _half_rows(ins[t].shape[1], 1 - c), :], dst_ref=outs[t],
                send_sem=send_sems.at[t], recv_sem=recv_sems.at[t], device_id=(x, y, 1 - c), device_id_type=MESH_ID,
            ))
            copies[-1].start()
        for cp in copies:
            cp.wait()

    sems = pltpu.SemaphoreType.DMA((n,))
    return pl.pallas_call(
        body,
        name="swap_halves",
        in_specs=[ANY] * n,
        out_specs=[ANY] * n,
        out_shape=[jax.ShapeDtypeStruct((g.shape[0], g.shape[1] // 2, g.shape[2]), g.dtype) for g in grads],
        scratch_shapes=[sems, sems],
    )(*grads)


def _scatter_to_chips(parts):
    n = len(parts)

    def body(*refs):
        ins, outs = refs[:n], refs[n : 2 * n]
        send_sems, recv_sems = refs[2 * n :]
        x, y, c = _mesh_position()
        copies = []
        for t in range(n):
            for k, (px, py) in enumerate(_other_chips(x, y)):
                copies.append(pltpu.make_async_remote_copy(
                    src_ref=ins[t].at[2 * px + py], dst_ref=outs[t].at[k],
                    send_sem=send_sems.at[t, k], recv_sem=recv_sems.at[t, k], device_id=(px, py, c), device_id_type=MESH_ID,
                ))
                copies[-1].start()
        for cp in copies:
            cp.wait()

    sems = pltpu.SemaphoreType.DMA((n, N_CHIPS - 1))
    return pl.pallas_call(
        body,
        name="scatter_to_chips",
        in_specs=[ANY] * n,
        out_specs=[ANY] * n,
        out_shape=[jax.ShapeDtypeStruct((N_CHIPS - 1,) + p.shape[1:], p.dtype) for p in parts],
        scratch_shapes=[sems, sems],
    )(*parts)


def _join_halves(shards):
    n = len(shards)

    def body(*refs):
        outs = refs[n : 2 * n]
        send_sems, recv_sems = refs[2 * n :]
        x, y, c = _mesh_position()
        copies = []
        for t in range(n):
            mine = outs[t].at[:, _half_rows(outs[t].shape[1], c), :]
            copies.append(pltpu.make_async_remote_copy(
                src_ref=mine, dst_ref=mine, send_sem=send_sems.at[t], recv_sem=recv_sems.at[t],
                device_id=(x, y, 1 - c), device_id_type=MESH_ID,
            ))
            copies[-1].start()
        for cp in copies:
            cp.wait()

    sems = pltpu.SemaphoreType.DMA((n,))
    return pl.pallas_call(
        body,
        name="join_halves",
        in_specs=[ANY] * n,
        out_specs=[ANY] * n,
        out_shape=[jax.ShapeDtypeStruct(g.shape, g.dtype) for g in shards],
        input_output_aliases={t: t for t in range(n)},
        scratch_shapes=[sems, sems],
    )(*shards)


def _gather_small(pack):
    def body(p_ref, o_ref, send_sems, recv_sems, local_sem):
        x, y, c = _mesh_position()
        own = pltpu.make_async_copy(p_ref, o_ref.at[4 * x + 2 * y + c], local_sem)
        own.start()
        copies = []
        for k in range(1, N_DEV):
            px, py, pc = x ^ (k >> 2), y ^ ((k >> 1) & 1), c ^ (k & 1)
            send = pltpu.make_async_remote_copy(
                src_ref=p_ref, dst_ref=o_ref.at[4 * x + 2 * y + c], send_sem=send_sems.at[k - 1], recv_sem=recv_sems.at[k - 1],
                device_id=(px, py, pc), device_id_type=MESH_ID,
            )
            send.start()
            copies.append((send, 4 * px + 2 * py + pc))
        for send, peer_slot in copies:
            send.wait_send()
        for k in range(1, N_DEV):
            px, py, pc = x ^ (k >> 2), y ^ ((k >> 1) & 1), c ^ (k & 1)
            pltpu.make_async_remote_copy(
                src_ref=p_ref, dst_ref=o_ref.at[4 * px + 2 * py + pc], send_sem=send_sems.at[k - 1], recv_sem=recv_sems.at[k - 1],
                device_id=(px, py, pc), device_id_type=MESH_ID,
            ).wait_recv()
        own.wait()

    sems = pltpu.SemaphoreType.DMA((N_DEV - 1,))
    return pl.pallas_call(
        body,
        name="gather_small",
        in_specs=[VMEM_SPEC],
        out_specs=VMEM_SPEC,
        out_shape=jax.ShapeDtypeStruct((N_DEV,) + pack.shape, pack.dtype),
        scratch_shapes=[sems, sems, pltpu.SemaphoreType.DMA],
    )(pack)


def _row_tile(rows):
    for tile in (256, 128, 64, 32, 16, 8):
        if rows % tile == 0:
            return tile
    return rows


def _add_half(grad, received, half_index, name):
    slots, h, cdim = received.shape
    tile = _row_tile(h)
    per_half = h // tile

    def body(c_ref, g_ref, r_ref, o_ref, ob_ref):
        total = g_ref[...] + r_ref[...]
        o_ref[...] = total
        ob_ref[...] = total.astype(BF16)

    block = pl.BlockSpec((1, tile, cdim), lambda j, i, c: (j, i, 0))
    grid_spec = pltpu.PrefetchScalarGridSpec(
        num_scalar_prefetch=1,
        grid=(slots, per_half),
        in_specs=[pl.BlockSpec((1, tile, cdim), lambda j, i, c: (j, c[0] * per_half + i, 0)), block],
        out_specs=[block, block],
    )
    return pl.pallas_call(
        body, name=name, grid_spec=grid_spec,
        out_shape=[jax.ShapeDtypeStruct(received.shape, F32), jax.ShapeDtypeStruct(received.shape, BF16)],
        compiler_params=_params(2),
    )(half_index, grad, received)


def _add_chips(part, received, chip_index, core_index, layer, n_layers, shard, name):
    _, h, cdim = part.shape
    tile = _row_tile(h)
    per_half = h // tile

    def body(chip_ref, core_ref, p_ref, r_ref, *rest):
        o_ref = rest[-1]
        o_ref[0] = ((p_ref[0] + r_ref[0].astype(F32)) + r_ref[1].astype(F32)) + r_ref[2].astype(F32)

    in_specs = [
        pl.BlockSpec((1, tile, cdim), lambda i, chip, core: (chip[0], i, 0)),
        pl.BlockSpec((N_CHIPS - 1, tile, cdim), lambda i, chip, core: (0, i, 0)),
    ]
    operands = [chip_index, core_index, part, received]
    aliases = {}
    if shard is not None:
        in_specs.append(ANY)
        operands.append(shard)
        aliases = {4: 0}
    grid_spec = pltpu.PrefetchScalarGridSpec(
        num_scalar_prefetch=2,
        grid=(per_half,),
        in_specs=in_specs,
        out_specs=pl.BlockSpec((1, tile, cdim), lambda i, chip, core: (layer, core[0] * per_half + i, 0)),
    )
    return pl.pallas_call(
        body, name=name, grid_spec=grid_spec, out_shape=jax.ShapeDtypeStruct((n_layers, 2 * h, cdim), F32),
        input_output_aliases=aliases, compiler_params=_params(1),
    )(*operands)


def _adamw(w, g, m, v, name):
    rows, cdim = w.shape
    tile = _row_tile(rows)

    def body(w_ref, g_ref, m_ref, v_ref, d_ref, nm_ref, nv_ref):
        gv = g_ref[...]
        nm = ADAM_B1 * m_ref[...] + (1.0 - ADAM_B1) * gv
        nv = ADAM_B2 * v_ref[...] + (1.0 - ADAM_B2) * (gv * gv)
        m_hat = nm / (1.0 - ADAM_B1 ** ADAM_STEP)
        v_hat = nv / (1.0 - ADAM_B2 ** ADAM_STEP)
        d_ref[...] = -ADAM_LR * (m_hat / (jnp.sqrt(v_hat) + ADAM_EPS) + ADAM_WD * w_ref[...])
        nm_ref[...] = nm
        nv_ref[...] = nv

    spec = pl.BlockSpec((tile, cdim), lambda i: (i, 0))
    shape = jax.ShapeDtypeStruct((rows, cdim), F32)
    return pl.pallas_call(
        body, name=name, grid=(rows // tile,), in_specs=[spec] * 4, out_specs=[spec] * 3, out_shape=[shape] * 3,
        compiler_params=_params(1),
    )(w, g, m, v)


SMALL_ROWS, SMALL_COLS = 24, 1024
ROW_NORM_MIX, ROW_NORM_FFN, ROW_LOSS, ROW_Q_NORM, ROW_K_NORM, ROW_CONV = 0, 2, 4, 8, 10, 16


def _sum_small(gathered):
    def body(g_ref, o_ref, heads_ref, lanes_ref):
        total = g_ref[0]
        for dev in range(1, N_DEV):
            total = total + g_ref[dev]
        o_ref[...] = total
        heads = o_ref[8:16, 0:LANES]
        for grp in range(1, ATTN_DIM // LANES):
            heads = heads + o_ref[8:16, grp * LANES : (grp + 1) * LANES]
        heads_ref[...] = heads + pltpu.roll(heads, HEAD_DIM, 1)
        lanes_ref[...] = jnp.broadcast_to(jnp.sum(o_ref[0:8, :], axis=-1, keepdims=True), (8, LANES))

    return pl.pallas_call(
        body,
        name="sum_small",
        in_specs=[VMEM_SPEC],
        out_specs=[VMEM_SPEC] * 3,
        out_shape=[jax.ShapeDtypeStruct((SMALL_ROWS, SMALL_COLS), F32), jax.ShapeDtypeStruct((8, LANES), F32), jax.ShapeDtypeStruct((8, LANES), F32)],
    )(gathered)


def _pad_rows(a, rows):
    return jnp.pad(a, ((0, rows - a.shape[0]), (0, 0)))


def _pad_to(a, rows, cols):
    return jnp.pad(a, ((0, rows - a.shape[0]), (0, cols - a.shape[1])))


def _local_step(x, target, norm_mix, q_norm, k_norm, norm_ffn, conv_full, win_s, wout_s, wg_s, wu_s, wd_s):
    n_layers = norm_mix.shape[0]
    s, d = x.shape
    tm = TOKEN_TILE
    n_in = win_s.shape[3]
    f = wg_s.shape[3]
    saved = []
    for l in range(n_layers):
        q_gain = jnp.tile(q_norm[l][None, :], (1, 2))
        k_gain = jnp.tile(k_norm[l][None, :], (1, 2))
        h1, proj = _norm_matmul(x, norm_mix[l][None, :], win_s, l, f"in_proj_{l}")
        qn, kn, vb = _qkv_prep(proj, q_gain, k_gain, f"qkv_prep_{l}")
        attn = _attn_fwd(qn, kn, vb, f"attn_fwd_{l}")
        conv = _conv_fwd(proj, conv_full[l], f"conv_fwd_{l}")
        x_mid = _out_proj(x, attn, conv, wout_s, l, f"out_proj_{l}")
        x_out = _ffn_fwd(x_mid, norm_ffn[l][None, :], wg_s, wu_s, wd_s, l, f"ffn_fwd_{l}")
        saved.append(dict(x=x, h1=h1, proj=proj, qn=qn, kn=kn, vb=vb, attn=attn, conv=conv, x_mid=x_mid, q_gain=q_gain, k_gain=k_gain))
        x = x_out

    dy, loss_lanes = _loss_grad(x, target, "loss_grad")
    grads = [None] * n_layers
    for l in reversed(range(n_layers)):
        sv = saved[l]
        dx_mid, d_norm_ffn, h2, dyb, dgate, dup, act = _ffn_bwd(sv["x_mid"], dy, norm_ffn[l][None, :], wg_s, wu_s, wd_s, l, f"ffn_bwd_{l}")
        tok2 = pl.BlockSpec((tm, d), lambda j, i: (i, 0))
        hid = pl.BlockSpec((1, tm, f), lambda j, i: (j, i, 0))
        d_wg = _wgrad(h2, dgate, tok2, hid, N_CHIPS, d, f, f"wgrad_gate_{l}")
        d_wu = _wgrad(h2, dup, tok2, hid, N_CHIPS, d, f, f"wgrad_up_{l}")
        d_wd = _wgrad(act, dyb, hid, tok2, N_CHIPS, f, d, f"wgrad_down_{l}")
        d_attn, d_conv, dxb = _out_proj_bwd(dx_mid, wout_s, l, f"out_proj_bwd_{l}")
        rows_out = wout_s.shape[2]
        mix_spec_a = pl.BlockSpec((tm, rows_out), lambda j, i: (i, j))
        d_wout_a = _wgrad(sv["attn"], dxb, mix_spec_a, tok2, ATTN_DIM // rows_out, rows_out, d, f"wgrad_out_attn_{l}")
        d_wout_c = _wgrad(sv["conv"], dxb, mix_spec_a, tok2, CONV_DIM // rows_out, rows_out, d, f"wgrad_out_conv_{l}")
        d_wout = jnp.concatenate([d_wout_a, d_wout_c], axis=0)
        dq, dk, dv = _attn_bwd(sv["qn"], sv["kn"], sv["vb"], d_attn, f"attn_bwd_{l}")
        dproj_a, d_qg, d_kg = _qkv_prep_bwd(sv["proj"], sv["q_gain"], sv["k_gain"], dq, dk, dv, f"qkv_prep_bwd_{l}")
        dproj_b, d_conv_w = _conv_bwd(sv["proj"], conv_full[l], d_conv, f"conv_bwd_{l}")
        dproj = jnp.concatenate([dproj_a, dproj_b], axis=1)
        d_win = _wgrad(sv["h1"], dproj, tok2, pl.BlockSpec((tm, n_in), lambda j, i: (i, j)), N_CHIPS, d, n_in, f"wgrad_in_{l}")
        dy, d_norm_mix = _in_proj_bwd(sv["x"], dx_mid, norm_mix[l][None, :], dproj, win_s, l, f"in_proj_bwd_{l}")
        grads[l] = dict(norm_mix=d_norm_mix, norm_ffn=d_norm_ffn, q_norm=d_qg, k_norm=d_kg, conv_w=d_conv_w,
                        w_in=d_win, w_out=d_wout, w_gate=d_wg, w_up=d_wu, w_down=d_wd)
    return loss_lanes, dy, grads


BIG = ("w_in", "w_out", "w_gate", "w_up", "w_down")


def kernel(x, norm_mix, w_in, q_norm, k_norm, conv_w, w_out, norm_ffn, w_gate, w_up, w_down, loss_target, m_norm_mix, m_w_in, m_q_norm, m_k_norm, m_conv_w, m_w_out, m_norm_ffn, m_w_gate, m_w_up, m_w_down, v_norm_mix, v_w_in, v_q_norm, v_k_norm, v_conv_w, v_w_out, v_norm_ffn, v_w_gate, v_w_up, v_w_down):
    n_layers = norm_mix.shape[0]
    weights = dict(w_in=w_in, w_out=w_out, w_gate=w_gate, w_up=w_up, w_down=w_down)
    moments_m = dict(w_in=m_w_in, w_out=m_w_out, w_gate=m_w_gate, w_up=m_w_up, w_down=m_w_down)
    moments_v = dict(w_in=v_w_in, w_out=v_w_out, w_gate=v_w_gate, w_up=v_w_up, w_down=v_w_down)
    cx, cy, cc = _mesh_position()
    chip_index = (2 * cx + cy).astype(jnp.int32).reshape(1)
    core_index = cc.astype(jnp.int32).reshape(1)

    conv_pad = jnp.pad(conv_w, ((0, 0), (0, 16 - conv_w.shape[1]), (0, 0)))
    gathered = _gather_weights([weights[k].astype(BF16) for k in BIG] + [conv_pad])
    win_s, wout_s, wg_s, wu_s, wd_s, conv_s = gathered
    conv_full = jnp.transpose(conv_s[:, :, 0:8], (1, 2, 0, 3)).reshape(n_layers, 8, N_CHIPS * conv_w.shape[2])

    loss_lanes, grad_x, grads = _local_step(
        x[0], loss_target[0], norm_mix, q_norm, k_norm, norm_ffn, conv_full, win_s, wout_s, wg_s, wu_s, wd_s)

    flat = [grads[l][k] for l in range(n_layers) for k in BIG]
    names = [f"{k}_{l}" for l in range(n_layers) for k in BIG]
    received = _swap_halves(flat)
    parts = [_add_half(g, r, core_index, f"add_half_{nm}") for g, r, nm in zip(flat, received, names)]
    from_chips = _scatter_to_chips([p_bf16 for _, p_bf16 in parts])
    shards = [None] * len(BIG)
    for n, ((p_f32, _), r, nm) in enumerate(zip(parts, from_chips, names)):
        layer, tensor = divmod(n, len(BIG))
        shards[tensor] = _add_chips(p_f32, r, chip_index, core_index, layer, n_layers, shards[tensor], f"add_chips_{nm}")
    big_grads = dict(zip(BIG, _join_halves(shards)))

    def lanes(a):
        return _pad_to(a, a.shape[0], SMALL_COLS)

    def tile_of(*groups):
        return _pad_rows(jnp.concatenate([lanes(jnp.concatenate(g, axis=0)) for g in groups], axis=0), 8)

    layers = range(n_layers)
    pack = jnp.concatenate([
        tile_of([grads[l]["norm_mix"] for l in layers], [grads[l]["norm_ffn"] for l in layers], [loss_lanes]),
        tile_of([grads[l]["q_norm"] for l in layers], [grads[l]["k_norm"] for l in layers]),
        tile_of([grads[l]["conv_w"][0:3] for l in layers]),
    ], axis=0)
    small, small_heads, small_lanes = _sum_small(_gather_small(pack))
    loss = small_lanes[ROW_LOSS, 0]
    d_model = norm_mix.shape[1]
    conv_cols = conv_w.shape[2]
    conv_all = small[ROW_CONV : ROW_CONV + 3 * n_layers, 0:CONV_DIM].reshape(n_layers, 3, CONV_DIM)
    small_grads = dict(
        norm_mix=small[ROW_NORM_MIX : ROW_NORM_MIX + n_layers, 0:d_model],
        norm_ffn=small[ROW_NORM_FFN : ROW_NORM_FFN + n_layers, 0:d_model],
        q_norm=small_heads[ROW_Q_NORM - 8 : ROW_Q_NORM - 8 + n_layers, 0:HEAD_DIM],
        k_norm=small_heads[ROW_K_NORM - 8 : ROW_K_NORM - 8 + n_layers, 0:HEAD_DIM],
        conv_w=lax.dynamic_slice_in_dim(conv_all, (2 * cx + cy) * conv_cols, conv_cols, axis=2),
    )

    out_grad, out_delta, out_m, out_v = {}, {}, {}, {}
    for k in BIG:
        shape = weights[k].shape
        view = (shape[0] * shape[1], shape[2])
        g = big_grads[k]
        delta, new_m, new_v = _adamw(weights[k].reshape(view), g.reshape(view), moments_m[k].reshape(view), moments_v[k].reshape(view), f"adamw_{k}")
        out_grad[k], out_delta[k], out_m[k], out_v[k] = g, delta.reshape(shape), new_m.reshape(shape), new_v.reshape(shape)

    small_w = dict(norm_mix=norm_mix, norm_ffn=norm_ffn, q_norm=q_norm, k_norm=k_norm, conv_w=conv_w)
    small_m = dict(norm_mix=m_norm_mix, norm_ffn=m_norm_ffn, q_norm=m_q_norm, k_norm=m_k_norm, conv_w=m_conv_w)
    small_v = dict(norm_mix=v_norm_mix, norm_ffn=v_norm_ffn, q_norm=v_q_norm, k_norm=v_k_norm, conv_w=v_conv_w)
    order = ("norm_mix", "norm_ffn", "q_norm", "k_norm", "conv_w")

    def packed(tree):
        parts2 = [_pad_to(tree[k].reshape(-1, tree[k].shape[-1]), tree[k].reshape(-1, tree[k].shape[-1]).shape[0], SMALL_COLS) for k in order]
        return _pad_rows(jnp.concatenate(parts2, axis=0), SMALL_ROWS)

    delta_p, m_p, v_p = _adamw(packed(small_w), packed(small_grads), packed(small_m), packed(small_v), "adamw_small")
    row = 0
    for k in order:
        shape = small_w[k].shape
        n_rows = 1
        for dim in shape[:-1]:
            n_rows *= dim
        cut = (slice(row, row + n_rows), slice(0, shape[-1]))
        out_grad[k] = small_grads[k]
        out_delta[k], out_m[k], out_v[k] = delta_p[cut].reshape(shape), m_p[cut].reshape(shape), v_p[cut].reshape(shape)
        row += n_rows

    names_out = ("norm_mix", "w_in", "q_norm", "k_norm", "conv_w", "w_out", "norm_ffn", "w_gate", "w_up", "w_down")
    return (loss, grad_x[None], *[out_grad[k] for k in names_out], *[out_delta[k] for k in names_out],
            *[out_m[k] for k in names_out], *[out_v[k] for k in names_out])
```

```python
import functools

import jax
import jax.numpy as jnp
from jax import lax
from jax.experimental import pallas as pl
from jax.experimental.pallas import tpu as pltpu

F32 = jnp.float32
BF16 = jnp.bfloat16

EPS = 1e-6
HEAD_DIM = 64
LANES = 128
ATTN_DIM = 512
CONV_DIM = 512
N_CHIPS = 4
N_DEV = 8
Q_SCALE = HEAD_DIM ** -0.5
ATTN_Q_TILE = 256
ATTN_TILE = 256
TOKEN_TILE = 512
VMEM_LIMIT = 56 * 1024 * 1024

ADAM_LR = 0.001
ADAM_B1 = 0.9
ADAM_B2 = 0.999
ADAM_EPS = 1e-08
ADAM_WD = 0.01
ADAM_STEP = 10

MESH_ID = pl.DeviceIdType.MESH
ANY = pl.BlockSpec(memory_space=pl.ANY)
VMEM_SPEC = pl.BlockSpec(memory_space=pltpu.VMEM)


def _params(n_axes):
    return pltpu.CompilerParams(dimension_semantics=("arbitrary",) * n_axes, vmem_limit_bytes=VMEM_LIMIT)


def _dot(a, b):
    return jnp.dot(a, b, preferred_element_type=F32)


def _dot_nt(a, b):
    return lax.dot_general(a, b, (((1,), (1,)), ((), ())), preferred_element_type=F32)


def _dot_tn(a, b):
    return lax.dot_general(a, b, (((0,), (0,)), ((), ())), preferred_element_type=F32)


SCORE_MAX = 80.0
UNDERFLOW_EXIT = 90.0


def _scores(q, k):
    return jnp.minimum(_dot_nt(q, k), SCORE_MAX)


def _softplus(z):
    return jnp.log(1.0 + jnp.exp(z))


def _norm_matmul(x, gain, w_s, layer, name):
    s, d = x.shape
    n_blocks, _, _, n = w_s.shape
    tm = TOKEN_TILE

    def body(x_ref, g_ref, w_ref, h_ref, o_ref):
        @pl.when(pl.program_id(1) == 0)
        def _():
            xv = x_ref[...]
            r = lax.rsqrt(jnp.mean(xv * xv, axis=-1, keepdims=True) + EPS)
            h_ref[...] = (xv * r * g_ref[...]).astype(BF16)

        o_ref[...] = _dot(h_ref[...], w_ref[0, 0])

    return pl.pallas_call(
        body,
        name=name,
        grid=(s // tm, n_blocks),
        in_specs=[
            pl.BlockSpec((tm, d), lambda i, j: (i, 0)),
            pl.BlockSpec((1, d), lambda i, j: (0, 0)),
            pl.BlockSpec((1, 1, d, n), lambda i, j: (j, layer, 0, 0)),
        ],
        out_specs=[pl.BlockSpec((tm, d), lambda i, j: (i, 0)), pl.BlockSpec((tm, n), lambda i, j: (i, j))],
        out_shape=[jax.ShapeDtypeStruct((s, d), BF16), jax.ShapeDtypeStruct((s, n_blocks * n), F32)],
        compiler_params=_params(2),
    )(x, gain, w_s)


def _head_norm(xv, gain, low):
    sq = xv * xv
    s_low = jnp.sum(jnp.where(low, sq, 0.0), axis=-1, keepdims=True)
    s_high = jnp.sum(jnp.where(low, 0.0, sq), axis=-1, keepdims=True)
    r = jnp.where(low, lax.rsqrt(s_low / HEAD_DIM + EPS), lax.rsqrt(s_high / HEAD_DIM + EPS))
    return xv * r * gain, r


def _qkv_prep(proj, q_gain, k_gain, name):
    s = proj.shape[0]
    tm = TOKEN_TILE

    def body(p_ref, qg_ref, kg_ref, q_ref, k_ref, v_ref):
        low = lax.broadcasted_iota(jnp.int32, (tm, LANES), 1) < HEAD_DIM
        for g in range(ATTN_DIM // LANES):
            cq = slice(LANES * g, LANES * (g + 1))
            ck = slice(ATTN_DIM + LANES * g, ATTN_DIM + LANES * (g + 1))
            cv = slice(2 * ATTN_DIM + LANES * g, 2 * ATTN_DIM + LANES * (g + 1))
            qn, _ = _head_norm(p_ref[:, cq], qg_ref[...], low)
            kn, _ = _head_norm(p_ref[:, ck], kg_ref[...], low)
            q_ref[:, cq] = (qn * Q_SCALE).astype(BF16)
            k_ref[:, cq] = kn.astype(BF16)
            v_ref[:, cq] = p_ref[:, cv].astype(BF16)

    out = jax.ShapeDtypeStruct((s, ATTN_DIM), BF16)
    return pl.pallas_call(
        body,
        name=name,
        grid=(s // tm,),
        in_specs=[
            pl.BlockSpec((tm, 3 * ATTN_DIM), lambda i: (i, 0)),
            pl.BlockSpec((1, LANES), lambda i: (0, 0)),
            pl.BlockSpec((1, LANES), lambda i: (0, 0)),
        ],
        out_specs=[pl.BlockSpec((tm, ATTN_DIM), lambda i: (i, 0))] * 3,
        out_shape=[out, out, out],
        compiler_params=_params(1),
    )(proj, q_gain, k_gain)


def _qkv_prep_bwd(proj, q_gain, k_gain, dq, dk, dv, name):
    s = proj.shape[0]
    tm = TOKEN_TILE

    def norm_bwd(xv, gain, dy, low):
        _, r = _head_norm(xv, gain, low)
        xhat = xv * r
        dxhat = dy * gain
        prod = dxhat * xhat
        m_low = jnp.sum(jnp.where(low, prod, 0.0), axis=-1, keepdims=True)
        m_high = jnp.sum(jnp.where(low, 0.0, prod), axis=-1, keepdims=True)
        mean = jnp.where(low, m_low, m_high) / HEAD_DIM
        return r * (dxhat - xhat * mean), jnp.sum(dy * xhat, axis=0, keepdims=True)

    def body(p_ref, qg_ref, kg_ref, dq_ref, dk_ref, dv_ref, dp_ref, dqg_ref, dkg_ref):
        @pl.when(pl.program_id(0) == 0)
        def _():
            dqg_ref[...] = jnp.zeros_like(dqg_ref)
            dkg_ref[...] = jnp.zeros_like(dkg_ref)

        low = lax.broadcasted_iota(jnp.int32, (tm, LANES), 1) < HEAD_DIM
        for g in range(ATTN_DIM // LANES):
            cq = slice(LANES * g, LANES * (g + 1))
            ck = slice(ATTN_DIM + LANES * g, ATTN_DIM + LANES * (g + 1))
            cv = slice(2 * ATTN_DIM + LANES * g, 2 * ATTN_DIM + LANES * (g + 1))
            dxq, dgq = norm_bwd(p_ref[:, cq], qg_ref[...], dq_ref[:, cq] * Q_SCALE, low)
            dxk, dgk = norm_bwd(p_ref[:, ck], kg_ref[...], dk_ref[:, cq], low)
            dp_ref[:, cq] = dxq.astype(BF16)
            dp_ref[:, ck] = dxk.astype(BF16)
            dp_ref[:, cv] = dv_ref[:, cq].astype(BF16)
            dqg_ref[:, cq] += dgq
            dkg_ref[:, cq] += dgk

    grad_spec = pl.BlockSpec((tm, ATTN_DIM), lambda i: (i, 0))
    gain_spec = pl.BlockSpec((1, LANES), lambda i: (0, 0))
    sum_spec = pl.BlockSpec((1, ATTN_DIM), lambda i: (0, 0))
    return pl.pallas_call(
        body,
        name=name,
        grid=(s // tm,),
        in_specs=[pl.BlockSpec((tm, 3 * ATTN_DIM), lambda i: (i, 0)), gain_spec, gain_spec, grad_spec, grad_spec, grad_spec],
        out_specs=[pl.BlockSpec((tm, 3 * ATTN_DIM), lambda i: (i, 0)), sum_spec, sum_spec],
        out_shape=[
            jax.ShapeDtypeStruct((s, 3 * ATTN_DIM), BF16),
            jax.ShapeDtypeStruct((1, ATTN_DIM), F32),
            jax.ShapeDtypeStruct((1, ATTN_DIM), F32),
        ],
        compiler_params=_params(1),
    )(proj, q_gain, k_gain, dq, dk, dv)


def _attn_tile_consts(t):
    row = lax.broadcasted_iota(jnp.int32, (t, t), 0)
    col = lax.broadcasted_iota(jnp.int32, (t, t), 1)
    return row, col


def _triangle_sum(v, triangle):
    return _dot(v.astype(BF16), triangle)


def _attn_fwd(qn, kn, vb, name):
    s = qn.shape[0]
    t = min(ATTN_TILE, s)
    tq = min(ATTN_Q_TILE, t)
    per_key_tile = t // tq

    def body(q_ref, k_ref, v_ref, o_ref):
        i = pl.program_id(1) // per_key_tile
        low = lax.broadcasted_iota(jnp.int32, (tq, LANES), 1) < HEAD_DIM
        row, col = _attn_tile_consts(t)
        suffix = (row > col).astype(BF16)
        first_row = (pl.program_id(1) % per_key_tile) * tq
        causal = lax.broadcasted_iota(jnp.int32, (tq, t), 1) < lax.broadcasted_iota(jnp.int32, (tq, t), 0) + first_row
        q = q_ref[...]
        zero_q = jnp.zeros_like(q)
        qh = (jnp.where(low, q, zero_q), jnp.where(low, zero_q, q))

        def step(kbs, carry, diagonal=False):
            chains = [(head, m) for head in range(2) for m in range(len(kbs))]
            ks = [k_ref[pl.ds(pl.multiple_of(kb * t, t), t), :] for kb in kbs]
            vs = [v_ref[pl.ds(pl.multiple_of(kb * t, t), t), :] for kb in kbs]
            z = [_scores(qh[head], ks[kb]) for head, kb in chains]
            sp = [_softplus(zc) for zc in z]
            if diagonal:
                sp = [jnp.where(causal, s_, 0.0) for s_ in sp]
            inside = [_triangle_sum(s_, suffix) for s_ in sp]
            after = [carry[head][1] for head in range(2)]
            log_a = []
            for n, (head, kb) in enumerate(chains):
                log_a.append(z[n] - sp[n] - inside[n] - after[head])
                after[head] = after[head] + jnp.sum(sp[n], axis=-1, keepdims=True)
            a = [jnp.exp(l_) for l_ in log_a]
            if diagonal:
                a = [jnp.where(causal, a_, 0.0) for a_ in a]
            acc = [carry[head][0] for head in range(2)]
            for n, (head, kb) in enumerate(chains):
                acc[head] = acc[head] + _dot(a[n].astype(BF16), vs[kb])
            return tuple((acc[head], after[head]) for head in range(2))

        zero = (jnp.zeros((tq, LANES), F32), jnp.zeros((tq, 1), F32))
        def live(c):
            return jnp.minimum(jnp.min(c[0][1]), jnp.min(c[1][1])) < UNDERFLOW_EXIT

        carry = step((i,), (zero, zero), True)
        carry = lax.cond((i % 2 == 1) & live(carry), lambda c: step((i - 1,), c), lambda c: c, carry)
        pairs = i // 2
        _, carry = lax.while_loop(
            lambda st: (st[0] < pairs) & live(st[1]),
            lambda st: (st[0] + 1, step((2 * (pairs - st[0]) - 1, 2 * (pairs - st[0]) - 2), st[1])),
            (jnp.int32(0), carry))
        o_ref[...] = jnp.where(low, carry[0][0], carry[1][0]).astype(BF16)

    return pl.pallas_call(
        body,
        name=name,
        grid=(ATTN_DIM // LANES, s // tq),
        in_specs=[
            pl.BlockSpec((tq, LANES), lambda p, i: (i, p)),
            pl.BlockSpec((s, LANES), lambda p, i: (0, p)),
            pl.BlockSpec((s, LANES), lambda p, i: (0, p)),
        ],
        out_specs=pl.BlockSpec((tq, LANES), lambda p, i: (i, p)),
        out_shape=jax.ShapeDtypeStruct((s, ATTN_DIM), BF16),
        compiler_params=_params(2),
    )(qn, kn, vb)


def _attn_bwd(qn, kn, vb, do, name):
    s = qn.shape[0]
    t = min(ATTN_TILE, s)
    nq = s // t

    def body(q_ref, k_ref, v_ref, do_ref, dq_ref, dk_ref, dv_ref, a_s, sg_s):
        i = pl.program_id(1)

        @pl.when(i == 0)
        def _():
            dk_ref[...] = jnp.zeros_like(dk_ref)
            dv_ref[...] = jnp.zeros_like(dv_ref)

        low = lax.broadcasted_iota(jnp.int32, (t, LANES), 1) < HEAD_DIM
        row, col = _attn_tile_consts(t)
        suffix = (row > col).astype(BF16)
        prefix = (row < col).astype(BF16)
        causal = col < row
        q = q_ref[...]
        dob = do_ref[...]
        zero_q = jnp.zeros_like(q)
        pairs = i // 2
        heads = []
        for head in range(2):
            if head == 0:
                qh, doh = jnp.where(low, q, zero_q), jnp.where(low, dob, zero_q)
            else:
                qh, doh = jnp.where(low, zero_q, q), jnp.where(low, zero_q, dob)

            def rows_of(kb):
                return pl.ds(pl.multiple_of(kb * t, t), t)

            def pass1(kbs, after, diagonal=False):
                z = [_scores(qh, k_ref[rows_of(kb), :]) for kb in kbs]
                sp = [_softplus(z_) for z_ in z]
                if diagonal:
                    sp = [jnp.where(causal, s_, 0.0) for s_ in sp]
                inside = [_triangle_sum(s_, suffix) for s_ in sp]
                for n, kb in enumerate(kbs):
                    log_sg = z[n] - sp[n]
                    a = jnp.exp(log_sg - inside[n] - after)
                    sg = jnp.exp(log_sg)
                    if diagonal:
                        a = jnp.where(causal, a, 0.0)
                        sg = jnp.where(causal, sg, 0.0)
                    a_s[kb] = a
                    sg_s[kb] = sg
                    after = after + jnp.sum(sp[n], axis=-1, keepdims=True)
                return after

            def live(after):
                return jnp.min(after) < UNDERFLOW_EXIT

            after = pass1((i,), jnp.zeros((t, 1), F32), True)
            take_single = (i % 2 == 1) & live(after)
            after = lax.cond(take_single, lambda c: pass1((i - 1,), c), lambda c: c, after)
            pairs_done, _ = lax.while_loop(
                lambda st: (st[0] < pairs) & live(st[1]),
                lambda st: (st[0] + 1, pass1((2 * (pairs - st[0]) - 1, 2 * (pairs - st[0]) - 2), st[1])),
                (jnp.int32(0), after))
            first = i - take_single.astype(jnp.int32) - 2 * pairs_done
            walked = i - first + 1

            def pass2(kbs, carry):
                dq, before = carry
                ks = [k_ref[rows_of(kb), :] for kb in kbs]
                a = [a_s[kb] for kb in kbs]
                g = [a_ * _dot_nt(doh, v_ref[rows_of(kb), :]) for a_, kb in zip(a, kbs)]
                for n, kb in enumerate(kbs):
                    dv_ref[rows_of(kb), :] += _dot_tn(a[n].astype(BF16), doh)
                inside = [_triangle_sum(g_, prefix) for g_ in g]
                dz = []
                for n, kb in enumerate(kbs):
                    sg = sg_s[kb]
                    dz.append((g[n] - sg * (g[n] + inside[n] + before)).astype(BF16))
                    before = before + jnp.sum(g[n], axis=-1, keepdims=True)
                for n, kb in enumerate(kbs):
                    dk_ref[rows_of(kb), :] += _dot_tn(dz[n], qh)
                for n in range(len(kbs)):
                    dq = dq + _dot(dz[n], ks[n])
                return dq, before

            carry = (jnp.zeros((t, LANES), F32), jnp.zeros((t, 1), F32))
            carry = lax.fori_loop(0, walked // 2, lambda n, c: pass2((first + 2 * n, first + 2 * n + 1), c), carry)
            carry = lax.cond(walked % 2 == 1, lambda c: pass2((i,), c), lambda c: c, carry)
            heads.append(carry[0])
        dq_ref[...] = jnp.where(low, heads[0], heads[1])

    q_spec = pl.BlockSpec((t, LANES), lambda p, i: (i, p))
    kv_spec = pl.BlockSpec((s, LANES), lambda p, i: (0, p))
    return pl.pallas_call(
        body,
        name=name,
        grid=(ATTN_DIM // LANES, nq),
        in_specs=[q_spec, kv_spec, kv_spec, q_spec],
        out_specs=[q_spec, kv_spec, kv_spec],
        out_shape=[jax.ShapeDtypeStruct((s, ATTN_DIM), F32)] * 3,
        scratch_shapes=[pltpu.VMEM((nq, t, t), F32), pltpu.VMEM((nq, t, t), F32)],
        compiler_params=_params(2),
    )(qn, kn, vb, do)


CB_BLOCK, CC_BLOCK, CU_BLOCK = 3, 4, 5


def _shift_down(h, prev_rows, n):
    row = lax.broadcasted_iota(jnp.int32, h.shape, 0)
    out = pltpu.roll(h, n, 0)
    for r in range(n):
        out = jnp.where(row == r, prev_rows[len(prev_rows) - n + r], out)
    return out


def _shift_up(h, next_rows, n):
    tm = h.shape[0]
    row = lax.broadcasted_iota(jnp.int32, h.shape, 0)
    out = pltpu.roll(h, tm - n, 0)
    for r in range(n):
        out = jnp.where(row == tm - n + r, next_rows[r], out)
    return out


def _conv_fwd(proj, conv_w, name):
    s = proj.shape[0]
    tm = TOKEN_TILE
    nb = tm // 8

    def body(cb_ref, cc_ref, cu_ref, pc_ref, pu_ref, w_ref, o_ref):
        first = pl.program_id(0) == 0
        h = cc_ref[...] * cu_ref[...]
        prev = [jnp.where(first, 0.0, pc_ref[r : r + 1, :] * pu_ref[r : r + 1, :]) for r in (6, 7)]
        y = w_ref[0:1, :] * _shift_down(h, prev, 2) + w_ref[1:2, :] * _shift_down(h, prev, 1) + w_ref[2:3, :] * h
        o_ref[...] = (cb_ref[...] * y).astype(BF16)

    def col(block):
        return pl.BlockSpec((tm, CONV_DIM), lambda i: (i, block))

    def halo(block):
        return pl.BlockSpec((8, CONV_DIM), lambda i: (jnp.maximum(i * nb - 1, 0), block))

    return pl.pallas_call(
        body,
        name=name,
        grid=(s // tm,),
        in_specs=[col(CB_BLOCK), col(CC_BLOCK), col(CU_BLOCK), halo(CC_BLOCK), halo(CU_BLOCK), pl.BlockSpec((8, CONV_DIM), lambda i: (0, 0))],
        out_specs=pl.BlockSpec((tm, CONV_DIM), lambda i: (i, 0)),
        out_shape=jax.ShapeDtypeStruct((s, CONV_DIM), BF16),
        compiler_params=_params(1),
    )(proj, proj, proj, proj, proj, conv_w)


def _conv_bwd(proj, conv_w, dconv, name):
    s = proj.shape[0]
    tm = TOKEN_TILE
    nb = tm // 8
    n_tiles = s // tm

    def body(cb_ref, cc_ref, cu_ref, dy_ref, pc_ref, pu_ref, nb_ref, ndy_ref, w_ref, dp_ref, dw_ref):
        i = pl.program_id(0)

        @pl.when(i == 0)
        def _():
            dw_ref[...] = jnp.zeros_like(dw_ref)

        first = i == 0
        last = i == n_tiles - 1
        cc, cu, cb, dy = cc_ref[...], cu_ref[...], cb_ref[...], dy_ref[...]
        h = cc * cu
        prev = [jnp.where(first, 0.0, pc_ref[r : r + 1, :] * pu_ref[r : r + 1, :]) for r in (6, 7)]
        h1 = _shift_down(h, prev, 1)
        h2 = _shift_down(h, prev, 2)
        y = w_ref[0:1, :] * h2 + w_ref[1:2, :] * h1 + w_ref[2:3, :] * h
        dyb = dy * cb
        nxt = [jnp.where(last, 0.0, ndy_ref[r : r + 1, :] * nb_ref[r : r + 1, :]) for r in (0, 1)]
        dh = w_ref[2:3, :] * dyb + w_ref[1:2, :] * _shift_up(dyb, nxt, 1) + w_ref[0:1, :] * _shift_up(dyb, nxt, 2)
        dp_ref[:, 0:CONV_DIM] = (dy * y).astype(BF16)
        dp_ref[:, CONV_DIM : 2 * CONV_DIM] = (dh * cu).astype(BF16)
        dp_ref[:, 2 * CONV_DIM : 3 * CONV_DIM] = (dh * cc).astype(BF16)
        dw_ref[0:1, :] += jnp.sum(dyb * h2, axis=0, keepdims=True)
        dw_ref[1:2, :] += jnp.sum(dyb * h1, axis=0, keepdims=True)
        dw_ref[2:3, :] += jnp.sum(dyb * h, axis=0, keepdims=True)

    def col(block):
        return pl.BlockSpec((tm, CONV_DIM), lambda i: (i, block))

    def halo_prev(block):
        return pl.BlockSpec((8, CONV_DIM), lambda i: (jnp.maximum(i * nb - 1, 0), block))

    def halo_next(block):
        return pl.BlockSpec((8, CONV_DIM), lambda i: (jnp.minimum((i + 1) * nb, s // 8 - 1), block))

    return pl.pallas_call(
        body,
        name=name,
        grid=(n_tiles,),
        in_specs=[
            col(CB_BLOCK), col(CC_BLOCK), col(CU_BLOCK), col(0),
            halo_prev(CC_BLOCK), halo_prev(CU_BLOCK), halo_next(CB_BLOCK), halo_next(0),
            pl.BlockSpec((8, CONV_DIM), lambda i: (0, 0)),
        ],
        out_specs=[pl.BlockSpec((tm, 3 * CONV_DIM), lambda i: (i, 0)), pl.BlockSpec((8, CONV_DIM), lambda i: (0, 0))],
        out_shape=[jax.ShapeDtypeStruct((s, 3 * CONV_DIM), BF16), jax.ShapeDtypeStruct((8, CONV_DIM), F32)],
        compiler_params=_params(1),
    )(proj, proj, proj, dconv, proj, proj, proj, dconv, conv_w)


def _out_proj(x, attn, conv, w_s, layer, name):
    s, d = x.shape
    tm = TOKEN_TILE
    rows = w_s.shape[2]

    def body(x_ref, a_ref, c_ref, w_ref, o_ref):
        acc = x_ref[...]
        for j in range(N_CHIPS):
            src = a_ref if j < 2 else c_ref
            cols = slice((j % 2) * rows, (j % 2 + 1) * rows)
            acc = acc + _dot(src[:, cols], w_ref[j, 0])
        o_ref[...] = acc

    return pl.pallas_call(
        body,
        name=name,
        grid=(s // tm,),
        in_specs=[
            pl.BlockSpec((tm, d), lambda i: (i, 0)),
            pl.BlockSpec((tm, ATTN_DIM), lambda i: (i, 0)),
            pl.BlockSpec((tm, CONV_DIM), lambda i: (i, 0)),
            pl.BlockSpec((N_CHIPS, 1, rows, d), lambda i: (0, layer, 0, 0)),
        ],
        out_specs=pl.BlockSpec((tm, d), lambda i: (i, 0)),
        out_shape=jax.ShapeDtypeStruct((s, d), F32),
        compiler_params=_params(1),
    )(x, attn, conv, w_s)


def _out_proj_bwd(dx, w_s, layer, name):
    s, d = dx.shape
    tm = TOKEN_TILE
    rows = w_s.shape[2]

    def body(dx_ref, w_ref, da_ref, dc_ref, dxb_ref):
        dxb = dx_ref[...].astype(BF16)
        dxb_ref[...] = dxb
        for j in range(N_CHIPS):
            cols = slice((j % 2) * rows, (j % 2 + 1) * rows)
            part = _dot_nt(dxb, w_ref[j, 0])
            if j < 2:
                da_ref[:, cols] = part.astype(BF16)
            else:
                dc_ref[:, cols] = part

    return pl.pallas_call(
        body,
        name=name,
        grid=(s // tm,),
        in_specs=[pl.BlockSpec((tm, d), lambda i: (i, 0)), pl.BlockSpec((N_CHIPS, 1, rows, d), lambda i: (0, layer, 0, 0))],
        out_specs=[
            pl.BlockSpec((tm, ATTN_DIM), lambda i: (i, 0)),
            pl.BlockSpec((tm, CONV_DIM), lambda i: (i, 0)),
            pl.BlockSpec((tm, d), lambda i: (i, 0)),
        ],
        out_shape=[
            jax.ShapeDtypeStruct((s, ATTN_DIM), BF16),
            jax.ShapeDtypeStruct((s, CONV_DIM), F32),
            jax.ShapeDtypeStruct((s, d), BF16),
        ],
        compiler_params=_params(1),
    )(dx, w_s)


def _ffn_fwd(x, gain, wg_s, wu_s, wd_s, layer, name):
    s, d = x.shape
    tm = TOKEN_TILE
    f = wg_s.shape[3]

    def body(x_ref, g_ref, wg_ref, wu_ref, wd_ref, o_ref, h_s):
        j = pl.program_id(1)

        @pl.when(j == 0)
        def _():
            xv = x_ref[...]
            r = lax.rsqrt(jnp.mean(xv * xv, axis=-1, keepdims=True) + EPS)
            h_s[...] = (xv * r * g_ref[...]).astype(BF16)
            o_ref[...] = xv

        h = h_s[...]
        gate = _dot(h, wg_ref[0, 0])
        up = _dot(h, wu_ref[0, 0])
        act = (gate / (1.0 + jnp.exp(-gate))) * up
        o_ref[...] += _dot(act.astype(BF16), wd_ref[0, 0])

    return pl.pallas_call(
        body,
        name=name,
        grid=(s // tm, N_CHIPS),
        in_specs=[
            pl.BlockSpec((tm, d), lambda i, j: (i, 0)),
            pl.BlockSpec((1, d), lambda i, j: (0, 0)),
            pl.BlockSpec((1, 1, d, f), lambda i, j: (j, layer, 0, 0)),
            pl.BlockSpec((1, 1, d, f), lambda i, j: (j, layer, 0, 0)),
            pl.BlockSpec((1, 1, f, d), lambda i, j: (j, layer, 0, 0)),
        ],
        out_specs=pl.BlockSpec((tm, d), lambda i, j: (i, 0)),
        out_shape=jax.ShapeDtypeStruct((s, d), F32),
        scratch_shapes=[pltpu.VMEM((tm, d), BF16)],
        compiler_params=_params(2),
    )(x, gain, wg_s, wu_s, wd_s)


def _rms_bwd(xv, gain, dh):
    r = lax.rsqrt(jnp.mean(xv * xv, axis=-1, keepdims=True) + EPS)
    xhat = xv * r
    dxhat = dh * gain
    dx = r * (dxhat - xhat * jnp.mean(dxhat * xhat, axis=-1, keepdims=True))
    return dx, jnp.sum(dh * xhat, axis=0, keepdims=True)


def _ffn_bwd(x, dy, gain, wg_s, wu_s, wd_s, layer, name):
    s, d = x.shape
    tm = TOKEN_TILE
    f = wg_s.shape[3]

    def body(x_ref, dy_ref, g_ref, wg_ref, wu_ref, wd_ref, dx_ref, dgain_ref, h_ref, dyb_ref, dg_ref, du_ref, act_ref, acc_s):
        i, j = pl.program_id(0), pl.program_id(1)

        @pl.when((i == 0) & (j == 0))
        def _():
            dgain_ref[...] = jnp.zeros_like(dgain_ref)

        @pl.when(j == 0)
        def _():
            xv = x_ref[...]
            r = lax.rsqrt(jnp.mean(xv * xv, axis=-1, keepdims=True) + EPS)
            h_ref[...] = (xv * r * g_ref[...]).astype(BF16)
            dyb_ref[...] = dy_ref[...].astype(BF16)
            acc_s[...] = jnp.zeros_like(acc_s)

        h = h_ref[...]
        gate = _dot(h, wg_ref[0, 0])
        up = _dot(h, wu_ref[0, 0])
        sig = 1.0 / (1.0 + jnp.exp(-gate))
        silu = gate * sig
        dact = _dot_nt(dyb_ref[...], wd_ref[0, 0])
        dgate = (dact * up * (sig * (1.0 + gate * (1.0 - sig)))).astype(BF16)
        dup = (dact * silu).astype(BF16)
        act_ref[0] = (silu * up).astype(BF16)
        dg_ref[0] = dgate
        du_ref[0] = dup
        acc_s[...] += _dot_nt(dgate, wg_ref[0, 0]) + _dot_nt(dup, wu_ref[0, 0])

        @pl.when(j == N_CHIPS - 1)
        def _():
            dxn, dgain = _rms_bwd(x_ref[...], g_ref[...], acc_s[...])
            dx_ref[...] = dy_ref[...] + dxn
            dgain_ref[...] += dgain

    tok = pl.BlockSpec((tm, d), lambda i, j: (i, 0))
    vec = pl.BlockSpec((1, d), lambda i, j: (0, 0))
    hid = pl.BlockSpec((1, tm, f), lambda i, j: (j, i, 0))
    hid_shape = jax.ShapeDtypeStruct((N_CHIPS, s, f), BF16)
    return pl.pallas_call(
        body,
        name=name,
        grid=(s // tm, N_CHIPS),
        in_specs=[
            tok, tok, vec,
            pl.BlockSpec((1, 1, d, f), lambda i, j: (j, layer, 0, 0)),
            pl.BlockSpec((1, 1, d, f), lambda i, j: (j, layer, 0, 0)),
            pl.BlockSpec((1, 1, f, d), lambda i, j: (j, layer, 0, 0)),
        ],
        out_specs=[tok, vec, tok, tok, hid, hid, hid],
        out_shape=[
            jax.ShapeDtypeStruct((s, d), F32),
            jax.ShapeDtypeStruct((1, d), F32),
            jax.ShapeDtypeStruct((s, d), BF16),
            jax.ShapeDtypeStruct((s, d), BF16),
            hid_shape, hid_shape, hid_shape,
        ],
        scratch_shapes=[pltpu.VMEM((tm, d), F32)],
        compiler_params=_params(2),
    )(x, dy, gain, wg_s, wu_s, wd_s)


def _in_proj_bwd(x, dx_res, gain, dproj, w_s, layer, name):
    s, d = x.shape
    tm = TOKEN_TILE
    n = w_s.shape[3]

    def body(x_ref, r_ref, g_ref, dp_ref, w_ref, dx_ref, dgain_ref, acc_s):
        i, j = pl.program_id(0), pl.program_id(1)

        @pl.when((i == 0) & (j == 0))
        def _():
            dgain_ref[...] = jnp.zeros_like(dgain_ref)

        @pl.when(j == 0)
        def _():
            acc_s[...] = jnp.zeros_like(acc_s)

        acc_s[...] += _dot_nt(dp_ref[...], w_ref[0, 0])

        @pl.when(j == N_CHIPS - 1)
        def _():
            dxn, dgain = _rms_bwd(x_ref[...], g_ref[...], acc_s[...])
            dx_ref[...] = r_ref[...] + dxn
            dgain_ref[...] += dgain

    tok = pl.BlockSpec((tm, d), lambda i, j: (i, 0))
    vec = pl.BlockSpec((1, d), lambda i, j: (0, 0))
    return pl.pallas_call(
        body,
        name=name,
        grid=(s // tm, N_CHIPS),
        in_specs=[tok, tok, vec, pl.BlockSpec((tm, n), lambda i, j: (i, j)), pl.BlockSpec((1, 1, d, n), lambda i, j: (j, layer, 0, 0))],
        out_specs=[tok, vec],
        out_shape=[jax.ShapeDtypeStruct((s, d), F32), jax.ShapeDtypeStruct((1, d), F32)],
        scratch_shapes=[pltpu.VMEM((tm, d), F32)],
        compiler_params=_params(2),
    )(x, dx_res, gain, dproj, w_s)


def _loss_grad(y, target, name):
    s, d = y.shape
    tm = TOKEN_TILE

    def body(y_ref, t_ref, dy_ref, l_ref):
        @pl.when(pl.program_id(0) == 0)
        def _():
            l_ref[...] = jnp.zeros_like(l_ref)

        err = y_ref[...] - t_ref[...]
        dy_ref[...] = err / d
        l_ref[...] += jnp.sum(err * err, axis=0, keepdims=True) * (0.5 / d)

    tok = pl.BlockSpec((tm, d), lambda i: (i, 0))
    return pl.pallas_call(
        body,
        name=name,
        grid=(s // tm,),
        in_specs=[tok, tok],
        out_specs=[tok, pl.BlockSpec((1, d), lambda i: (0, 0))],
        out_shape=[jax.ShapeDtypeStruct((s, d), F32), jax.ShapeDtypeStruct((1, d), F32)],
        compiler_params=_params(1),
    )(y, target)


def _wgrad(a, b, a_spec, b_spec, n_blocks, k, n, name):
    n_tiles = (a.shape[-2]) // TOKEN_TILE

    def body(a_ref, b_ref, o_ref):
        @pl.when(pl.program_id(1) == 0)
        def _():
            o_ref[...] = jnp.zeros_like(o_ref)

        av = a_ref[0] if len(a_ref.shape) == 3 else a_ref[...]
        bv = b_ref[0] if len(b_ref.shape) == 3 else b_ref[...]
        o_ref[0] += _dot_tn(av, bv)

    return pl.pallas_call(
        body,
        name=name,
        grid=(n_blocks, n_tiles),
        in_specs=[a_spec, b_spec],
        out_specs=pl.BlockSpec((1, k, n), lambda j, i: (j, 0, 0)),
        out_shape=jax.ShapeDtypeStruct((n_blocks, k, n), F32),
        compiler_params=_params(2),
    )(a, b)


def _mesh_position():
    return lax.axis_index("x"), lax.axis_index("y"), lax.axis_index("c")


def _other_chips(x, y):
    return [(1 - x, y), (x, 1 - y), (1 - x, 1 - y)]


def _half_rows(ref_rows, c):
    half = ref_rows // 2
    return pl.ds(c * half, half)


def _gather_weights(shards):
    n = len(shards)

    def body(*refs):
        ins, outs = refs[:n], refs[n : 2 * n]
        send_sems, recv_sems, pass_send_sems, pass_recv_sems, local_sems = refs[2 * n :]
        x, y, c = _mesh_position()
        me = 2 * x + y
        sibling = (x, y, 1 - c)
        chips = _other_chips(x, y)

        def block(t, chip_index, core):
            return outs[t].at[chip_index, :, _half_rows(ins[t].shape[1], core), :]

        def copy(t, k, chip_index, core, to, sems, src=None):
            dst = block(t, chip_index, core)
            return pltpu.make_async_remote_copy(
                src_ref=dst if src is None else src, dst_ref=dst, send_sem=sems[0].at[t, k], recv_sem=sems[1].at[t, k],
                device_id=to, device_id_type=MESH_ID,
            )

        ici, d2d = (send_sems, recv_sems), (pass_send_sems, pass_recv_sems)
        own = [pltpu.make_async_copy(ins[t], outs[t].at[me], local_sems.at[t]) for t in range(n)]
        for cp in own:
            cp.start()
        started = []
        for t in range(n):
            mine = ins[t].at[:, _half_rows(ins[t].shape[1], c), :]
            for k, (px, py) in enumerate(chips):
                started.append(copy(t, k, me, c, (px, py, c), ici, src=mine))
                started[-1].start()
        for t in range(n):
            for k, (px, py) in enumerate(chips):
                copy(t, k, 2 * px + py, c, sibling, ici).wait_recv()
                started.append(copy(t, k, 2 * px + py, c, sibling, d2d))
                started[-1].start()
        for t in range(n):
            for k, (px, py) in enumerate(chips):
                copy(t, k, 2 * px + py, 1 - c, sibling, d2d).wait_recv()
        for cp in started:
            cp.wait_send()
        for cp in own:
            cp.wait()

    sems = pltpu.SemaphoreType.DMA((n, N_CHIPS - 1))
    return pl.pallas_call(
        body,
        name="gather_weights",
        in_specs=[ANY] * n,
        out_specs=[ANY] * n,
        out_shape=[jax.ShapeDtypeStruct((N_CHIPS,) + w.shape, w.dtype) for w in shards],
        scratch_shapes=[sems, sems, sems, sems, pltpu.SemaphoreType.DMA((n,))],
    )(*shards)


def _swap_halves(grads):
    n = len(grads)

    def body(*refs):
        ins, outs = refs[:n], refs[n : 2 * n]
        send_sems, recv_sems = refs[2 * n :]
        x, y, c = _mesh_position()
        copies = []
        for t in range(n):
            copies.append(pltpu.make_async_remote_copy(
                src_ref=ins[t].at[:, _half_rows(ins[t].shape[1], 1 - c), :], dst_ref=outs[t],
                send_sem=send_sems.at[t], recv_sem=recv_sems.at[t], device_id=(x, y, 1 - c), device_id_type=MESH_ID,
            ))
            copies[-1].start()
        for cp in copies:
            cp.wait()

    sems = pltpu.SemaphoreType.DMA((n,))
    return pl.pallas_call(
        body,
        name="swap_halves",
        in_specs=[ANY] * n,
        out_specs=[ANY] * n,
        out_shape=[jax.ShapeDtypeStruct((g.shape[0], g.shape[1] // 2, g.shape[2]), g.dtype) for g in grads],
        scratch_shapes=[sems, sems],
    )(*grads)


def _scatter_to_chips(parts):
    n = len(parts)

    def body(*refs):
        ins, outs = refs[:n], refs[n : 2 * n]
        send_sems, recv_sems = refs[2 * n :]
        x, y, c = _mesh_position()
        copies = []
        for t in range(n):
            for k, (px, py) in enumerate(_other_chips(x, y)):
                copies.append(pltpu.make_async_remote_copy(
                    src_ref=ins[t].at[2 * px + py], dst_ref=outs[t].at[k],
                    send_sem=send_sems.at[t, k], recv_sem=recv_sems.at[t, k], device_id=(px, py, c), device_id_type=MESH_ID,
                ))
                copies[-1].start()
        for cp in copies:
            cp.wait()

    sems = pltpu.SemaphoreType.DMA((n, N_CHIPS - 1))
    return pl.pallas_call(
        body,
        name="scatter_to_chips",
        in_specs=[ANY] * n,
        out_specs=[ANY] * n,
        out_shape=[jax.ShapeDtypeStruct((N_CHIPS - 1,) + p.shape[1:], p.dtype) for p in parts],
        scratch_shapes=[sems, sems],
    )(*parts)


def _join_halves(shards):
    n = len(shards)

    def body(*refs):
        outs = refs[n : 2 * n]
        send_sems, recv_sems = refs[2 * n :]
        x, y, c = _mesh_position()
        copies = []
        for t in range(n):
            mine = outs[t].at[:, _half_rows(outs[t].shape[1], c), :]
            copies.append(pltpu.make_async_remote_copy(
                src_ref=mine, dst_ref=mine, send_sem=send_sems.at[t], recv_sem=recv_sems.at[t],
                device_id=(x, y, 1 - c), device_id_type=MESH_ID,
            ))
            copies[-1].start()
        for cp in copies:
            cp.wait()

    sems = pltpu.SemaphoreType.DMA((n,))
    return pl.pallas_call(
        body,
        name="join_halves",
        in_specs=[ANY] * n,
        out_specs=[ANY] * n,
        out_shape=[jax.ShapeDtypeStruct(g.shape, g.dtype) for g in shards],
        input_output_aliases={t: t for t in range(n)},
        scratch_shapes=[sems, sems],
    )(*shards)


def _gather_small(pack):
    def body(p_ref, o_ref, send_sems, recv_sems, local_sem):
        x, y, c = _mesh_position()
        own = pltpu.make_async_copy(p_ref, o_ref.at[4 * x + 2 * y + c], local_sem)
        own.start()
        copies = []
        for k in range(1, N_DEV):
            px, py, pc = x ^ (k >> 2), y ^ ((k >> 1) & 1), c ^ (k & 1)
            send = pltpu.make_async_remote_copy(
                src_ref=p_ref, dst_ref=o_ref.at[4 * x + 2 * y + c], send_sem=send_sems.at[k - 1], recv_sem=recv_sems.at[k - 1],
                device_id=(px, py, pc), device_id_type=MESH_ID,
            )
            send.start()
            copies.append((send, 4 * px + 2 * py + pc))
        for send, peer_slot in copies:
            send.wait_send()
        for k in range(1, N_DEV):
            px, py, pc = x ^ (k >> 2), y ^ ((k >> 1) & 1), c ^ (k & 1)
            pltpu.make_async_remote_copy(
                src_ref=p_ref, dst_ref=o_ref.at[4 * px + 2 * py + pc], send_sem=send_sems.at[k - 1], recv_sem=recv_sems.at[k - 1],
                device_id=(px, py, pc), device_id_type=MESH_ID,
            ).wait_recv()
        own.wait()

    sems = pltpu.SemaphoreType.DMA((N_DEV - 1,))
    return pl.pallas_call(
        body,
        name="gather_small",
        in_specs=[VMEM_SPEC],
        out_specs=VMEM_SPEC,
        out_shape=jax.ShapeDtypeStruct((N_DEV,) + pack.shape, pack.dtype),
        scratch_shapes=[sems, sems, pltpu.SemaphoreType.DMA],
    )(pack)


def _row_tile(rows):
    for tile in (256, 128, 64, 32, 16, 8):
        if rows % tile == 0:
            return tile
    return rows


def _add_half(grad, received, half_index, name):
    slots, h, cdim = received.shape
    tile = _row_tile(h)
    per_half = h // tile

    def body(c_ref, g_ref, r_ref, o_ref, ob_ref):
        total = g_ref[...] + r_ref[...]
        o_ref[...] = total
        ob_ref[...] = total.astype(BF16)

    block = pl.BlockSpec((1, tile, cdim), lambda j, i, c: (j, i, 0))
    grid_spec = pltpu.PrefetchScalarGridSpec(
        num_scalar_prefetch=1,
        grid=(slots, per_half),
        in_specs=[pl.BlockSpec((1, tile, cdim), lambda j, i, c: (j, c[0] * per_half + i, 0)), block],
        out_specs=[block, block],
    )
    return pl.pallas_call(
        body, name=name, grid_spec=grid_spec,
        out_shape=[jax.ShapeDtypeStruct(received.shape, F32), jax.ShapeDtypeStruct(received.shape, BF16)],
        compiler_params=_params(2),
    )(half_index, grad, received)


def _add_chips(part, received, chip_index, core_index, layer, n_layers, shard, name):
    _, h, cdim = part.shape
    tile = _row_tile(h)
    per_half = h // tile

    def body(chip_ref, core_ref, p_ref, r_ref, *rest):
        o_ref = rest[-1]
        o_ref[0] = ((p_ref[0] + r_ref[0].astype(F32)) + r_ref[1].astype(F32)) + r_ref[2].astype(F32)

    in_specs = [
        pl.BlockSpec((1, tile, cdim), lambda i, chip, core: (chip[0], i, 0)),
        pl.BlockSpec((N_CHIPS - 1, tile, cdim), lambda i, chip, core: (0, i, 0)),
    ]
    operands = [chip_index, core_index, part, received]
    aliases = {}
    if shard is not None:
        in_specs.append(ANY)
        operands.append(shard)
        aliases = {4: 0}
    grid_spec = pltpu.PrefetchScalarGridSpec(
        num_scalar_prefetch=2,
        grid=(per_half,),
        in_specs=in_specs,
        out_specs=pl.BlockSpec((1, tile, cdim), lambda i, chip, core: (layer, core[0] * per_half + i, 0)),
    )
    return pl.pallas_call(
        body, name=name, grid_spec=grid_spec, out_shape=jax.ShapeDtypeStruct((n_layers, 2 * h, cdim), F32),
        input_output_aliases=aliases, compiler_params=_params(1),
    )(*operands)


def _adamw(w, g, m, v, name):
    rows, cdim = w.shape
    tile = _row_tile(rows)

    def body(w_ref, g_ref, m_ref, v_ref, d_ref, nm_ref, nv_ref):
        gv = g_ref[...]
        nm = ADAM_B1 * m_ref[...] + (1.0 - ADAM_B1) * gv
        nv = ADAM_B2 * v_ref[...] + (1.0 - ADAM_B2) * (gv * gv)
        m_hat = nm / (1.0 - ADAM_B1 ** ADAM_STEP)
        v_hat = nv / (1.0 - ADAM_B2 ** ADAM_STEP)
        d_ref[...] = -ADAM_LR * (m_hat / (jnp.sqrt(v_hat) + ADAM_EPS) + ADAM_WD * w_ref[...])
        nm_ref[...] = nm
        nv_ref[...] = nv

    spec = pl.BlockSpec((tile, cdim), lambda i: (i, 0))
    shape = jax.ShapeDtypeStruct((rows, cdim), F32)
    return pl.pallas_call(
        body, name=name, grid=(rows // tile,), in_specs=[spec] * 4, out_specs=[spec] * 3, out_shape=[shape] * 3,
        compiler_params=_params(1),
    )(w, g, m, v)


SMALL_ROWS, SMALL_COLS = 24, 1024
ROW_NORM_MIX, ROW_NORM_FFN, ROW_LOSS, ROW_Q_NORM, ROW_K_NORM, ROW_CONV = 0, 2, 4, 8, 10, 16


def _sum_small(gathered):
    def body(g_ref, o_ref, heads_ref, lanes_ref):
        total = g_ref[0]
        for dev in range(1, N_DEV):
            total = total + g_ref[dev]
        o_ref[...] = total
        heads = o_ref[8:16, 0:LANES]
        for grp in range(1, ATTN_DIM // LANES):
            heads = heads + o_ref[8:16, grp * LANES : (grp + 1) * LANES]
        heads_ref[...] = heads + pltpu.roll(heads, HEAD_DIM, 1)
        lanes_ref[...] = jnp.broadcast_to(jnp.sum(o_ref[0:8, :], axis=-1, keepdims=True), (8, LANES))

    return pl.pallas_call(
        body,
        name="sum_small",
        in_specs=[VMEM_SPEC],
        out_specs=[VMEM_SPEC] * 3,
        out_shape=[jax.ShapeDtypeStruct((SMALL_ROWS, SMALL_COLS), F32), jax.ShapeDtypeStruct((8, LANES), F32), jax.ShapeDtypeStruct((8, LANES), F32)],
    )(gathered)


def _pad_rows(a, rows):
    return jnp.pad(a, ((0, rows - a.shape[0]), (0, 0)))


def _pad_to(a, rows, cols):
    return jnp.pad(a, ((0, rows - a.shape[0]), (0, cols - a.shape[1])))


def _local_step(x, target, norm_mix, q_norm, k_norm, norm_ffn, conv_full, win_s, wout_s, wg_s, wu_s, wd_s):
    n_layers = norm_mix.shape[0]
    s, d = x.shape
    tm = TOKEN_TILE
    n_in = win_s.shape[3]
    f = wg_s.shape[3]
    saved = []
    for l in range(n_layers):
        q_gain = jnp.tile(q_norm[l][None, :], (1, 2))
        k_gain = jnp.tile(k_norm[l][None, :], (1, 2))
        h1, proj = _norm_matmul(x, norm_mix[l][None, :], win_s, l, f"in_proj_{l}")
        qn, kn, vb = _qkv_prep(proj, q_gain, k_gain, f"qkv_prep_{l}")
        attn = _attn_fwd(qn, kn, vb, f"attn_fwd_{l}")
        conv = _conv_fwd(proj, conv_full[l], f"conv_fwd_{l}")
        x_mid = _out_proj(x, attn, conv, wout_s, l, f"out_proj_{l}")
        x_out = _ffn_fwd(x_mid, norm_ffn[l][None, :], wg_s, wu_s, wd_s, l, f"ffn_fwd_{l}")
        saved.append(dict(x=x, h1=h1, proj=proj, qn=qn, kn=kn, vb=vb, attn=attn, conv=conv, x_mid=x_mid, q_gain=q_gain, k_gain=k_gain))
        x = x_out

    dy, loss_lanes = _loss_grad(x, target, "loss_grad")
    grads = [None] * n_layers
    for l in reversed(range(n_layers)):
        sv = saved[l]
        dx_mid, d_norm_ffn, h2, dyb, dgate, dup, act = _ffn_bwd(sv["x_mid"], dy, norm_ffn[l][None, :], wg_s, wu_s, wd_s, l, f"ffn_bwd_{l}")
        tok2 = pl.BlockSpec((tm, d), lambda j, i: (i, 0))
        hid = pl.BlockSpec((1, tm, f), lambda j, i: (j, i, 0))
        d_wg = _wgrad(h2, dgate, tok2, hid, N_CHIPS, d, f, f"wgrad_gate_{l}")
        d_wu = _wgrad(h2, dup, tok2, hid, N_CHIPS, d, f, f"wgrad_up_{l}")
        d_wd = _wgrad(act, dyb, hid, tok2, N_CHIPS, f, d, f"wgrad_down_{l}")
        d_attn, d_conv, dxb = _out_proj_bwd(dx_mid, wout_s, l, f"out_proj_bwd_{l}")
        rows_out = wout_s.shape[2]
        mix_spec_a = pl.BlockSpec((tm, rows_out), lambda j, i: (i, j))
        d_wout_a = _wgrad(sv["attn"], dxb, mix_spec_a, tok2, ATTN_DIM // rows_out, rows_out, d, f"wgrad_out_attn_{l}")
        d_wout_c = _wgrad(sv["conv"], dxb, mix_spec_a, tok2, CONV_DIM // rows_out, rows_out, d, f"wgrad_out_conv_{l}")
        d_wout = jnp.concatenate([d_wout_a, d_wout_c], axis=0)
        dq, dk, dv = _attn_bwd(sv["qn"], sv["kn"], sv["vb"], d_attn, f"attn_bwd_{l}")
        dproj_a, d_qg, d_kg = _qkv_prep_bwd(sv["proj"], sv["q_gain"], sv["k_gain"], dq, dk, dv, f"qkv_prep_bwd_{l}")
        dproj_b, d_conv_w = _conv_bwd(sv["proj"], conv_full[l], d_conv, f"conv_bwd_{l}")
        dproj = jnp.concatenate([dproj_a, dproj_b], axis=1)
        d_win = _wgrad(sv["h1"], dproj, tok2, pl.BlockSpec((tm, n_in), lambda j, i: (i, j)), N_CHIPS, d, n_in, f"wgrad_in_{l}")
        dy, d_norm_mix = _in_proj_bwd(sv["x"], dx_mid, norm_mix[l][None, :], dproj, win_s, l, f"in_proj_bwd_{l}")
        grads[l] = dict(norm_mix=d_norm_mix, norm_ffn=d_norm_ffn, q_norm=d_qg, k_norm=d_kg, conv_w=d_conv_w,
                        w_in=d_win, w_out=d_wout, w_gate=d_wg, w_up=d_wu, w_down=d_wd)
    return loss_lanes, dy, grads


BIG = ("w_in", "w_out", "w_gate", "w_up", "w_down")


def kernel(x, norm_mix, w_in, q_norm, k_norm, conv_w, w_out, norm_ffn, w_gate, w_up, w_down, loss_target, m_norm_mix, m_w_in, m_q_norm, m_k_norm, m_conv_w, m_w_out, m_norm_ffn, m_w_gate, m_w_up, m_w_down, v_norm_mix, v_w_in, v_q_norm, v_k_norm, v_conv_w, v_w_out, v_norm_ffn, v_w_gate, v_w_up, v_w_down):
    n_layers = norm_mix.shape[0]
    weights = dict(w_in=w_in, w_out=w_out, w_gate=w_gate, w_up=w_up, w_down=w_down)
    moments_m = dict(w_in=m_w_in, w_out=m_w_out, w_gate=m_w_gate, w_up=m_w_up, w_down=m_w_down)
    moments_v = dict(w_in=v_w_in, w_out=v_w_out, w_gate=v_w_gate, w_up=v_w_up, w_down=v_w_down)
    cx, cy, cc = _mesh_position()
    chip_index = (2 * cx + cy).astype(jnp.int32).reshape(1)
    core_index = cc.astype(jnp.int32).reshape(1)

    conv_pad = jnp.pad(conv_w, ((0, 0), (0, 16 - conv_w.shape[1]), (0, 0)))
    gathered = _gather_weights([weights[k].astype(BF16) for k in BIG] + [conv_pad])
    win_s, wout_s, wg_s, wu_s, wd_s, conv_s = gathered
    conv_full = jnp.transpose(conv_s[:, :, 0:8], (1, 2, 0, 3)).reshape(n_layers, 8, N_CHIPS * conv_w.shape[2])

    loss_lanes, grad_x, grads = _local_step(
        x[0], loss_target[0], norm_mix, q_norm, k_norm, norm_ffn, conv_full, win_s, wout_s, wg_s, wu_s, wd_s)

    flat = [grads[l][k] for l in range(n_layers) for k in BIG]
    names = [f"{k}_{l}" for l in range(n_layers) for k in BIG]
    received = _swap_halves(flat)
    parts = [_add_half(g, r, core_index, f"add_half_{nm}") for g, r, nm in zip(flat, received, names)]
    from_chips = _scatter_to_chips([p_bf16 for _, p_bf16 in parts])
    shards = [None] * len(BIG)
    for n, ((p_f32, _), r, nm) in enumerate(zip(parts, from_chips, names)):
        layer, tensor = divmod(n, len(BIG))
        shards[tensor] = _add_chips(p_f32, r, chip_index, core_index, layer, n_layers, shards[tensor], f"add_chips_{nm}")
    big_grads = dict(zip(BIG, _join_halves(shards)))

    def lanes(a):
        return _pad_to(a, a.shape[0], SMALL_COLS)

    def tile_of(*groups):
        return _pad_rows(jnp.concatenate([lanes(jnp.concatenate(g, axis=0)) for g in groups], axis=0), 8)

    layers = range(n_layers)
    pack = jnp.concatenate([
        tile_of([grads[l]["norm_mix"] for l in layers], [grads[l]["norm_ffn"] for l in layers], [loss_lanes]),
        tile_of([grads[l]["q_norm"] for l in layers], [grads[l]["k_norm"] for l in layers]),
        tile_of([grads[l]["conv_w"][0:3] for l in layers]),
    ], axis=0)
    small, small_heads, small_lanes = _sum_small(_gather_small(pack))
    loss = small_lanes[ROW_LOSS, 0]
    d_model = norm_mix.shape[1]
    conv_cols = conv_w.shape[2]
    conv_all = small[ROW_CONV : ROW_CONV + 3 * n_layers, 0:CONV_DIM].reshape(n_layers, 3, CONV_DIM)
    small_grads = dict(
        norm_mix=small[ROW_NORM_MIX : ROW_NORM_MIX + n_layers, 0:d_model],
        norm_ffn=small[ROW_NORM_FFN : ROW_NORM_FFN + n_layers, 0:d_model],
        q_norm=small_heads[ROW_Q_NORM - 8 : ROW_Q_NORM - 8 + n_layers, 0:HEAD_DIM],
        k_norm=small_heads[ROW_K_NORM - 8 : ROW_K_NORM - 8 + n_layers, 0:HEAD_DIM],
        conv_w=lax.dynamic_slice_in_dim(conv_all, (2 * cx + cy) * conv_cols, conv_cols, axis=2),
    )

    out_grad, out_delta, out_m, out_v = {}, {}, {}, {}
    for k in BIG:
        shape = weights[k].shape
        view = (shape[0] * shape[1], shape[2])
        g = big_grads[k]
        delta, new_m, new_v = _adamw(weights[k].reshape(view), g.reshape(view), moments_m[k].reshape(view), moments_v[k].reshape(view), f"adamw_{k}")
        out_grad[k], out_delta[k], out_m[k], out_v[k] = g, delta.reshape(shape), new_m.reshape(shape), new_v.reshape(shape)

    small_w = dict(norm_mix=norm_mix, norm_ffn=norm_ffn, q_norm=q_norm, k_norm=k_norm, conv_w=conv_w)
    small_m = dict(norm_mix=m_norm_mix, norm_ffn=m_norm_ffn, q_norm=m_q_norm, k_norm=m_k_norm, conv_w=m_conv_w)
    small_v = dict(norm_mix=v_norm_mix, norm_ffn=v_norm_ffn, q_norm=v_q_norm, k_norm=v_k_norm, conv_w=v_conv_w)
    order = ("norm_mix", "norm_ffn", "q_norm", "k_norm", "conv_w")

    def packed(tree):
        parts2 = [_pad_to(tree[k].reshape(-1, tree[k].shape[-1]), tree[k].reshape(-1, tree[k].shape[-1]).shape[0], SMALL_COLS) for k in order]
        return _pad_rows(jnp.concatenate(parts2, axis=0), SMALL_ROWS)

    delta_p, m_p, v_p = _adamw(packed(small_w), packed(small_grads), packed(small_m), packed(small_v), "adamw_small")
    row = 0
    for k in order:
        shape = small_w[k].shape
        n_rows = 1
        for dim in shape[:-1]:
            n_rows *= dim
        cut = (slice(row, row + n_rows), slice(0, shape[-1]))
        out_grad[k] = small_grads[k]
        out_delta[k], out_m[k], out_v[k] = delta_p[cut].reshape(shape), m_p[cut].reshape(shape), v_p[cut].reshape(shape)
        row += n_rows

    names_out = ("norm_mix", "w_in", "q_norm", "k_norm", "conv_w", "w_out", "norm_ffn", "w_gate", "w_up", "w_down")
    return (loss, grad_x[None], *[out_grad[k] for k in names_out], *[out_delta[k] for k in names_out],
            *[out_m[k] for k in names_out], *[out_v[k] for k in names_out])
```

```python
import functools

import jax
import jax.numpy as jnp
from jax import lax
from jax.experimental import pallas as pl
from jax.experimental.pallas import tpu as pltpu

F32 = jnp.float32
BF16 = jnp.bfloat16

EPS = 1e-6
HEAD_DIM = 64
LANES = 128
ATTN_DIM = 512
CONV_DIM = 512
N_CHIPS = 4
N_DEV = 8
Q_SCALE = HEAD_DIM ** -0.5
ATTN_Q_TILE = 256
ATTN_TILE = 256
TOKEN_TILE = 512
WGRAD_TILE = 2048
VMEM_LIMIT = 56 * 1024 * 1024

ADAM_LR = 0.001
ADAM_B1 = 0.9
ADAM_B2 = 0.999
ADAM_EPS = 1e-08
ADAM_WD = 0.01
ADAM_STEP = 10

MESH_ID = pl.DeviceIdType.MESH
ANY = pl.BlockSpec(memory_space=pl.ANY)
VMEM_SPEC = pl.BlockSpec(memory_space=pltpu.VMEM)


def _params(n_axes):
    return pltpu.CompilerParams(dimension_semantics=("arbitrary",) * n_axes, vmem_limit_bytes=VMEM_LIMIT)


def _dot(a, b):
    return jnp.dot(a, b, preferred_element_type=F32)


def _dot_nt(a, b):
    return lax.dot_general(a, b, (((1,), (1,)), ((), ())), preferred_element_type=F32)


def _dot_tn(a, b):
    return lax.dot_general(a, b, (((0,), (0,)), ((), ())), preferred_element_type=F32)


SCORE_MAX = 80.0
UNDERFLOW_EXIT = 90.0


def _scores(q, k):
    return jnp.minimum(_dot_nt(q, k), SCORE_MAX)


def _softplus(z):
    return jnp.log(1.0 + jnp.exp(z))


def _norm_matmul(x, gain, w_s, layer, name):
    s, d = x.shape
    n_blocks, _, _, n = w_s.shape
    tm = TOKEN_TILE

    def body(x_ref, g_ref, w_ref, h_ref, o_ref):
        xv = x_ref[...]
        r = lax.rsqrt(jnp.mean(xv * xv, axis=-1, keepdims=True) + EPS)
        h = (xv * r * g_ref[...]).astype(BF16)
        h_ref[...] = h
        for j in range(n_blocks):
            o_ref[:, j * n : (j + 1) * n] = _dot(h, w_ref[j, 0])

    return pl.pallas_call(
        body,
        name=name,
        grid=(s // tm,),
        in_specs=[
            pl.BlockSpec((tm, d), lambda i: (i, 0)),
            pl.BlockSpec((1, d), lambda i: (0, 0)),
            pl.BlockSpec((n_blocks, 1, d, n), lambda i: (0, layer, 0, 0)),
        ],
        out_specs=[pl.BlockSpec((tm, d), lambda i: (i, 0)), pl.BlockSpec((tm, n_blocks * n), lambda i: (i, 0))],
        out_shape=[jax.ShapeDtypeStruct((s, d), BF16), jax.ShapeDtypeStruct((s, n_blocks * n), F32)],
        compiler_params=_params(1),
    )(x, gain, w_s)


def _head_norm(xv, gain, low):
    sq = xv * xv
    s_low = jnp.sum(jnp.where(low, sq, 0.0), axis=-1, keepdims=True)
    s_high = jnp.sum(jnp.where(low, 0.0, sq), axis=-1, keepdims=True)
    r = jnp.where(low, lax.rsqrt(s_low / HEAD_DIM + EPS), lax.rsqrt(s_high / HEAD_DIM + EPS))
    return xv * r * gain, r


def _qkv_prep(proj, q_gain, k_gain, name):
    s = proj.shape[0]
    tm = TOKEN_TILE

    def body(p_ref, qg_ref, kg_ref, q_ref, k_ref, v_ref):
        low = lax.broadcasted_iota(jnp.int32, (tm, LANES), 1) < HEAD_DIM
        for g in range(ATTN_DIM // LANES):
            cq = slice(LANES * g, LANES * (g + 1))
            ck = slice(ATTN_DIM + LANES * g, ATTN_DIM + LANES * (g + 1))
            cv = slice(2 * ATTN_DIM + LANES * g, 2 * ATTN_DIM + LANES * (g + 1))
            qn, _ = _head_norm(p_ref[:, cq], qg_ref[...], low)
            kn, _ = _head_norm(p_ref[:, ck], kg_ref[...], low)
            q_ref[:, cq] = (qn * Q_SCALE).astype(BF16)
            k_ref[:, cq] = kn.astype(BF16)
            v_ref[:, cq] = p_ref[:, cv].astype(BF16)

    out = jax.ShapeDtypeStruct((s, ATTN_DIM), BF16)
    return pl.pallas_call(
        body,
        name=name,
        grid=(s // tm,),
        in_specs=[
            pl.BlockSpec((tm, 3 * ATTN_DIM), lambda i: (i, 0)),
            pl.BlockSpec((1, LANES), lambda i: (0, 0)),
            pl.BlockSpec((1, LANES), lambda i: (0, 0)),
        ],
        out_specs=[pl.BlockSpec((tm, ATTN_DIM), lambda i: (i, 0))] * 3,
        out_shape=[out, out, out],
        compiler_params=_params(1),
    )(proj, q_gain, k_gain)


def _qkv_prep_bwd(proj, q_gain, k_gain, dq, dk, dv, dproj, name):
    s = proj.shape[0]
    tm = TOKEN_TILE

    def norm_bwd(xv, gain, dy, low):
        _, r = _head_norm(xv, gain, low)
        xhat = xv * r
        dxhat = dy * gain
        prod = dxhat * xhat
        m_low = jnp.sum(jnp.where(low, prod, 0.0), axis=-1, keepdims=True)
        m_high = jnp.sum(jnp.where(low, 0.0, prod), axis=-1, keepdims=True)
        mean = jnp.where(low, m_low, m_high) / HEAD_DIM
        return r * (dxhat - xhat * mean), jnp.sum(dy * xhat, axis=0, keepdims=True)

    def body(p_ref, qg_ref, kg_ref, dq_ref, dk_ref, dv_ref, dproj_ref, dp_ref, dqg_ref, dkg_ref):
        @pl.when(pl.program_id(0) == 0)
        def _():
            dqg_ref[...] = jnp.zeros_like(dqg_ref)
            dkg_ref[...] = jnp.zeros_like(dkg_ref)

        low = lax.broadcasted_iota(jnp.int32, (tm, LANES), 1) < HEAD_DIM
        for g in range(ATTN_DIM // LANES):
            cq = slice(LANES * g, LANES * (g + 1))
            ck = slice(ATTN_DIM + LANES * g, ATTN_DIM + LANES * (g + 1))
            cv = slice(2 * ATTN_DIM + LANES * g, 2 * ATTN_DIM + LANES * (g + 1))
            dxq, dgq = norm_bwd(p_ref[:, cq], qg_ref[...], dq_ref[:, cq] * Q_SCALE, low)
            dxk, dgk = norm_bwd(p_ref[:, ck], kg_ref[...], dk_ref[:, cq], low)
            dp_ref[:, cq] = dxq.astype(BF16)
            dp_ref[:, ck] = dxk.astype(BF16)
            dp_ref[:, cv] = dv_ref[:, cq].astype(BF16)
            dqg_ref[:, cq] += dgq
            dkg_ref[:, cq] += dgk

    grad_spec = pl.BlockSpec((tm, ATTN_DIM), lambda i: (i, 0))
    gain_spec = pl.BlockSpec((1, LANES), lambda i: (0, 0))
    sum_spec = pl.BlockSpec((1, ATTN_DIM), lambda i: (0, 0))
    return pl.pallas_call(
        body,
        name=name,
        grid=(s // tm,),
        in_specs=[pl.BlockSpec((tm, 3 * ATTN_DIM), lambda i: (i, 0)), gain_spec, gain_spec, grad_spec, grad_spec, grad_spec, ANY],
        out_specs=[pl.BlockSpec((tm, 3 * ATTN_DIM), lambda i: (i, 0)), sum_spec, sum_spec],
        out_shape=[
            jax.ShapeDtypeStruct(dproj.shape, BF16),
            jax.ShapeDtypeStruct((1, ATTN_DIM), F32),
            jax.ShapeDtypeStruct((1, ATTN_DIM), F32),
        ],
        input_output_aliases={6: 0},
        compiler_params=_params(1),
    )(proj, q_gain, k_gain, dq, dk, dv, dproj)


def _attn_tile_consts(t):
    row = lax.broadcasted_iota(jnp.int32, (t, t), 0)
    col = lax.broadcasted_iota(jnp.int32, (t, t), 1)
    return row, col


def _triangle_sum(v, triangle):
    return _dot(v.astype(BF16), triangle)


def _attn_fwd(qn, kn, vb, name):
    s = qn.shape[0]
    t = min(ATTN_TILE, s)
    tq = min(ATTN_Q_TILE, t)
    per_key_tile = t // tq

    def body(q_ref, k_ref, v_ref, o_ref):
        i = pl.program_id(1) // per_key_tile
        low = lax.broadcasted_iota(jnp.int32, (tq, LANES), 1) < HEAD_DIM
        row, col = _attn_tile_consts(t)
        suffix = (row > col).astype(BF16)
        first_row = (pl.program_id(1) % per_key_tile) * tq
        causal = lax.broadcasted_iota(jnp.int32, (tq, t), 1) < lax.broadcasted_iota(jnp.int32, (tq, t), 0) + first_row
        q = q_ref[...]
        zero_q = jnp.zeros_like(q)
        qh = (jnp.where(low, q, zero_q), jnp.where(low, zero_q, q))

        def step(kbs, carry, diagonal_first=False):
            chains = [(head, m) for head in range(2) for m in range(len(kbs))]
            masked = [diagonal_first and m == 0 for _, m in chains]
            ks = [k_ref[pl.ds(pl.multiple_of(kb * t, t), t), :] for kb in kbs]
            vs = [v_ref[pl.ds(pl.multiple_of(kb * t, t), t), :] for kb in kbs]
            z = [_scores(qh[head], ks[kb]) for head, kb in chains]
            sp = [_softplus(zc) for zc in z]
            sp = [jnp.where(causal, s_, 0.0) if mk else s_ for s_, mk in zip(sp, masked)]
            inside = [_triangle_sum(s_, suffix) for s_ in sp]
            after = [carry[head][1] for head in range(2)]
            log_a = []
            for n, (head, kb) in enumerate(chains):
                log_a.append(z[n] - sp[n] - inside[n] - after[head])
                after[head] = after[head] + jnp.sum(sp[n], axis=-1, keepdims=True)
            a = [jnp.exp(l_) for l_ in log_a]
            a = [jnp.where(causal, a_, 0.0) if mk else a_ for a_, mk in zip(a, masked)]
            acc = [carry[head][0] for head in range(2)]
            for n, (head, kb) in enumerate(chains):
                acc[head] = acc[head] + _dot(a[n].astype(BF16), vs[kb])
            return tuple((acc[head], after[head]) for head in range(2))

        def live(c):
            return jnp.minimum(jnp.min(c[0][1]), jnp.min(c[1][1])) < UNDERFLOW_EXIT

        zero = (jnp.zeros((tq, LANES), F32), jnp.zeros((tq, 1), F32))
        carry = lax.cond(i >= 1, lambda c: step((i, i - 1), c, True), lambda c: step((i,), c, True), (zero, zero))
        rest = jnp.maximum(i - 1, 0)
        carry = lax.cond((rest % 2 == 1) & live(carry), lambda c: step((i - 2,), c), lambda c: c, carry)
        pairs = rest // 2
        _, carry = lax.while_loop(
            lambda st: (st[0] < pairs) & live(st[1]),
            lambda st: (st[0] + 1, step((2 * (pairs - st[0]) - 1, 2 * (pairs - st[0]) - 2), st[1])),
            (jnp.int32(0), carry))
        o_ref[...] = jnp.where(low, carry[0][0], carry[1][0]).astype(BF16)

    return pl.pallas_call(
        body,
        name=name,
        grid=(ATTN_DIM // LANES, s // tq),
        in_specs=[
            pl.BlockSpec((tq, LANES), lambda p, i: (i, p)),
            pl.BlockSpec((s, LANES), lambda p, i: (0, p)),
            pl.BlockSpec((s, LANES), lambda p, i: (0, p)),
        ],
        out_specs=pl.BlockSpec((tq, LANES), lambda p, i: (i, p)),
        out_shape=jax.ShapeDtypeStruct((s, ATTN_DIM), BF16),
        compiler_params=_params(2),
    )(qn, kn, vb)


def _attn_bwd(qn, kn, vb, do, name):
    s = qn.shape[0]
    t = min(ATTN_TILE, s)
    nq = s // t

    def body(q_ref, k_ref, v_ref, do_ref, dq_ref, dk_ref, dv_ref, a_s, sg_s):
        i = pl.program_id(1)

        @pl.when(i == 0)
        def _():
            dk_ref[...] = jnp.zeros_like(dk_ref)
            dv_ref[...] = jnp.zeros_like(dv_ref)

        low = lax.broadcasted_iota(jnp.int32, (t, LANES), 1) < HEAD_DIM
        row, col = _attn_tile_consts(t)
        suffix = (row > col).astype(BF16)
        prefix = (row < col).astype(BF16)
        causal = col < row
        q = q_ref[...]
        dob = do_ref[...]
        zero_q = jnp.zeros_like(q)
        heads = []
        for head in range(2):
            if head == 0:
                qh, doh = jnp.where(low, q, zero_q), jnp.where(low, dob, zero_q)
            else:
                qh, doh = jnp.where(low, zero_q, q), jnp.where(low, zero_q, dob)

            def rows_of(kb):
                return pl.ds(pl.multiple_of(kb * t, t), t)

            def pass1(kbs, after, diagonal_first=False):
                z = [_scores(qh, k_ref[rows_of(kb), :]) for kb in kbs]
                sp = [_softplus(z_) for z_ in z]
                if diagonal_first:
                    sp[0] = jnp.where(causal, sp[0], 0.0)
                inside = [_triangle_sum(s_, suffix) for s_ in sp]
                for n, kb in enumerate(kbs):
                    log_sg = z[n] - sp[n]
                    a = jnp.exp(log_sg - inside[n] - after)
                    sg = jnp.exp(log_sg)
                    if diagonal_first and n == 0:
                        a = jnp.where(causal, a, 0.0)
                        sg = jnp.where(causal, sg, 0.0)
                    a_s[kb] = a
                    sg_s[kb] = sg
                    after = after + jnp.sum(sp[n], axis=-1, keepdims=True)
                return after

            def live(after):
                return jnp.min(after) < UNDERFLOW_EXIT

            after = jnp.zeros((t, 1), F32)
            after = lax.cond(i >= 1, lambda c: pass1((i, i - 1), c, True), lambda c: pass1((i,), c, True), after)
            rest = jnp.maximum(i - 1, 0)
            take_single = (rest % 2 == 1) & live(after)
            after = lax.cond(take_single, lambda c: pass1((i - 2,), c), lambda c: c, after)
            pairs = rest // 2
            pairs_done, _ = lax.while_loop(
                lambda st: (st[0] < pairs) & live(st[1]),
                lambda st: (st[0] + 1, pass1((2 * (pairs - st[0]) - 1, 2 * (pairs - st[0]) - 2), st[1])),
                (jnp.int32(0), after))
            walked = jnp.minimum(i, 1) + 1 + take_single.astype(jnp.int32) + 2 * pairs_done
            first = i - walked + 1

            def pass2(kbs, carry):
                dq, before = carry
                ks = [k_ref[rows_of(kb), :] for kb in kbs]
                a = [a_s[kb] for kb in kbs]
                g = [a_ * _dot_nt(doh, v_ref[rows_of(kb), :]) for a_, kb in zip(a, kbs)]
                for n, kb in enumerate(kbs):
                    dv_ref[rows_of(kb), :] += _dot_tn(a[n].astype(BF16), doh)
                inside = [_triangle_sum(g_, prefix) for g_ in g]
                dz = []
                for n, kb in enumerate(kbs):
                    sg = sg_s[kb]
                    dz.append((g[n] - sg * (g[n] + inside[n] + before)).astype(BF16))
                    before = before + jnp.sum(g[n], axis=-1, keepdims=True)
                for n, kb in enumerate(kbs):
                    dk_ref[rows_of(kb), :] += _dot_tn(dz[n], qh)
                for n in range(len(kbs)):
                    dq = dq + _dot(dz[n], ks[n])
                return dq, before

            carry = (jnp.zeros((t, LANES), F32), jnp.zeros((t, 1), F32))
            carry = lax.fori_loop(0, walked // 2, lambda n, c: pass2((first + 2 * n, first + 2 * n + 1), c), carry)
            carry = lax.cond(walked % 2 == 1, lambda c: pass2((i,), c), lambda c: c, carry)
            heads.append(carry[0])
        dq_ref[...] = jnp.where(low, heads[0], heads[1])

    q_spec = pl.BlockSpec((t, LANES), lambda p, i: (i, p))
    kv_spec = pl.BlockSpec((s, LANES), lambda p, i: (0, p))
    return pl.pallas_call(
        body,
        name=name,
        grid=(ATTN_DIM // LANES, nq),
        in_specs=[q_spec, kv_spec, kv_spec, q_spec],
        out_specs=[q_spec, kv_spec, kv_spec],
        out_shape=[jax.ShapeDtypeStruct((s, ATTN_DIM), F32)] * 3,
        scratch_shapes=[pltpu.VMEM((nq, t, t), F32), pltpu.VMEM((nq, t, t), F32)],
        compiler_params=_params(2),
    )(qn, kn, vb, do)


CB_BLOCK, CC_BLOCK, CU_BLOCK = 3, 4, 5


def _shift_down(h, prev_rows, n):
    row = lax.broadcasted_iota(jnp.int32, h.shape, 0)
    out = pltpu.roll(h, n, 0)
    for r in range(n):
        out = jnp.where(row == r, prev_rows[len(prev_rows) - n + r], out)
    return out


def _shift_up(h, next_rows, n):
    tm = h.shape[0]
    row = lax.broadcasted_iota(jnp.int32, h.shape, 0)
    out = pltpu.roll(h, tm - n, 0)
    for r in range(n):
        out = jnp.where(row == tm - n + r, next_rows[r], out)
    return out


def _conv_fwd(proj, conv_w, name):
    s = proj.shape[0]
    tm = TOKEN_TILE
    nb = tm // 8

    def body(cb_ref, cc_ref, cu_ref, pc_ref, pu_ref, w_ref, o_ref):
        first = pl.program_id(0) == 0
        h = cc_ref[...] * cu_ref[...]
        prev = [jnp.where(first, 0.0, pc_ref[r : r + 1, :] * pu_ref[r : r + 1, :]) for r in (6, 7)]
        y = w_ref[0:1, :] * _shift_down(h, prev, 2) + w_ref[1:2, :] * _shift_down(h, prev, 1) + w_ref[2:3, :] * h
        o_ref[...] = (cb_ref[...] * y).astype(BF16)

    def col(block):
        return pl.BlockSpec((tm, CONV_DIM), lambda i: (i, block))

    def halo(block):
        return pl.BlockSpec((8, CONV_DIM), lambda i: (jnp.maximum(i * nb - 1, 0), block))

    return pl.pallas_call(
        body,
        name=name,
        grid=(s // tm,),
        in_specs=[col(CB_BLOCK), col(CC_BLOCK), col(CU_BLOCK), halo(CC_BLOCK), halo(CU_BLOCK), pl.BlockSpec((8, CONV_DIM), lambda i: (0, 0))],
        out_specs=pl.BlockSpec((tm, CONV_DIM), lambda i: (i, 0)),
        out_shape=jax.ShapeDtypeStruct((s, CONV_DIM), BF16),
        compiler_params=_params(1),
    )(proj, proj, proj, proj, proj, conv_w)


def _conv_bwd(proj, conv_w, dconv, name):
    s = proj.shape[0]
    tm = TOKEN_TILE
    nb = tm // 8
    n_tiles = s // tm

    def body(cb_ref, cc_ref, cu_ref, dy_ref, pc_ref, pu_ref, nb_ref, ndy_ref, w_ref, dp_ref, dw_ref):
        i = pl.program_id(0)

        @pl.when(i == 0)
        def _():
            dw_ref[...] = jnp.zeros_like(dw_ref)

        first = i == 0
        last = i == n_tiles - 1
        cc, cu, cb, dy = cc_ref[...], cu_ref[...], cb_ref[...], dy_ref[...]
        h = cc * cu
        prev = [jnp.where(first, 0.0, pc_ref[r : r + 1, :] * pu_ref[r : r + 1, :]) for r in (6, 7)]
        h1 = _shift_down(h, prev, 1)
        h2 = _shift_down(h, prev, 2)
        y = w_ref[0:1, :] * h2 + w_ref[1:2, :] * h1 + w_ref[2:3, :] * h
        dyb = dy * cb
        nxt = [jnp.where(last, 0.0, ndy_ref[r : r + 1, :] * nb_ref[r : r + 1, :]) for r in (0, 1)]
        dh = w_ref[2:3, :] * dyb + w_ref[1:2, :] * _shift_up(dyb, nxt, 1) + w_ref[0:1, :] * _shift_up(dyb, nxt, 2)
        dp_ref[:, 0:CONV_DIM] = (dy * y).astype(BF16)
        dp_ref[:, CONV_DIM : 2 * CONV_DIM] = (dh * cu).astype(BF16)
        dp_ref[:, 2 * CONV_DIM : 3 * CONV_DIM] = (dh * cc).astype(BF16)
        dw_ref[0:1, :] += jnp.sum(dyb * h2, axis=0, keepdims=True)
        dw_ref[1:2, :] += jnp.sum(dyb * h1, axis=0, keepdims=True)
        dw_ref[2:3, :] += jnp.sum(dyb * h, axis=0, keepdims=True)

    def col(block):
        return pl.BlockSpec((tm, CONV_DIM), lambda i: (i, block))

    def halo_prev(block):
        return pl.BlockSpec((8, CONV_DIM), lambda i: (jnp.maximum(i * nb - 1, 0), block))

    def halo_next(block):
        return pl.BlockSpec((8, CONV_DIM), lambda i: (jnp.minimum((i + 1) * nb, s // 8 - 1), block))

    return pl.pallas_call(
        body,
        name=name,
        grid=(n_tiles,),
        in_specs=[
            col(CB_BLOCK), col(CC_BLOCK), col(CU_BLOCK), col(0),
            halo_prev(CC_BLOCK), halo_prev(CU_BLOCK), halo_next(CB_BLOCK), halo_next(0),
            pl.BlockSpec((8, CONV_DIM), lambda i: (0, 0)),
        ],
        out_specs=[pl.BlockSpec((tm, 3 * CONV_DIM), lambda i: (i, 1)), pl.BlockSpec((8, CONV_DIM), lambda i: (0, 0))],
        out_shape=[jax.ShapeDtypeStruct((s, 3 * ATTN_DIM + 3 * CONV_DIM), BF16), jax.ShapeDtypeStruct((8, CONV_DIM), F32)],
        compiler_params=_params(1),
    )(proj, proj, proj, dconv, proj, proj, proj, dconv, conv_w)


def _out_proj(x, attn, conv, w_s, layer, name):
    s, d = x.shape
    tm = TOKEN_TILE
    rows = w_s.shape[2]

    def body(x_ref, a_ref, c_ref, w_ref, o_ref):
        acc = x_ref[...]
        for j in range(N_CHIPS):
            src = a_ref if j < 2 else c_ref
            cols = slice((j % 2) * rows, (j % 2 + 1) * rows)
            acc = acc + _dot(src[:, cols], w_ref[j, 0])
        o_ref[...] = acc

    return pl.pallas_call(
        body,
        name=name,
        grid=(s // tm,),
        in_specs=[
            pl.BlockSpec((tm, d), lambda i: (i, 0)),
            pl.BlockSpec((tm, ATTN_DIM), lambda i: (i, 0)),
            pl.BlockSpec((tm, CONV_DIM), lambda i: (i, 0)),
            pl.BlockSpec((N_CHIPS, 1, rows, d), lambda i: (0, layer, 0, 0)),
        ],
        out_specs=pl.BlockSpec((tm, d), lambda i: (i, 0)),
        out_shape=jax.ShapeDtypeStruct((s, d), F32),
        compiler_params=_params(1),
    )(x, attn, conv, w_s)


def _out_proj_bwd(dx, w_s, layer, name):
    s, d = dx.shape
    tm = TOKEN_TILE
    rows = w_s.shape[2]

    def body(dx_ref, w_ref, da_ref, dc_ref, dxb_ref):
        dxb = dx_ref[...].astype(BF16)
        dxb_ref[...] = dxb
        for j in range(N_CHIPS):
            cols = slice((j % 2) * rows, (j % 2 + 1) * rows)
            part = _dot_nt(dxb, w_ref[j, 0])
            if j < 2:
                da_ref[:, cols] = part.astype(BF16)
            else:
                dc_ref[:, cols] = part

    return pl.pallas_call(
        body,
        name=name,
        grid=(s // tm,),
        in_specs=[pl.BlockSpec((tm, d), lambda i: (i, 0)), pl.BlockSpec((N_CHIPS, 1, rows, d), lambda i: (0, layer, 0, 0))],
        out_specs=[
            pl.BlockSpec((tm, ATTN_DIM), lambda i: (i, 0)),
            pl.BlockSpec((tm, CONV_DIM), lambda i: (i, 0)),
            pl.BlockSpec((tm, d), lambda i: (i, 0)),
        ],
        out_shape=[
            jax.ShapeDtypeStruct((s, ATTN_DIM), BF16),
            jax.ShapeDtypeStruct((s, CONV_DIM), F32),
            jax.ShapeDtypeStruct((s, d), BF16),
        ],
        compiler_params=_params(1),
    )(dx, w_s)


def _ffn_fwd(x, gain, wg_s, wu_s, wd_s, layer, name):
    s, d = x.shape
    tm = TOKEN_TILE
    f = wg_s.shape[3]

    def body(x_ref, g_ref, wg_ref, wu_ref, wd_ref, o_ref, h_s):
        j = pl.program_id(1)

        @pl.when(j == 0)
        def _():
            xv = x_ref[...]
            r = lax.rsqrt(jnp.mean(xv * xv, axis=-1, keepdims=True) + EPS)
            h_s[...] = (xv * r * g_ref[...]).astype(BF16)
            o_ref[...] = xv

        h = h_s[...]
        gate = _dot(h, wg_ref[0, 0])
        up = _dot(h, wu_ref[0, 0])
        act = (gate / (1.0 + jnp.exp(-gate))) * up
        o_ref[...] += _dot(act.astype(BF16), wd_ref[0, 0])

    return pl.pallas_call(
        body,
        name=name,
        grid=(s // tm, N_CHIPS),
        in_specs=[
            pl.BlockSpec((tm, d), lambda i, j: (i, 0)),
            pl.BlockSpec((1, d), lambda i, j: (0, 0)),
            pl.BlockSpec((1, 1, d, f), lambda i, j: (j, layer, 0, 0)),
            pl.BlockSpec((1, 1, d, f), lambda i, j: (j, layer, 0, 0)),
            pl.BlockSpec((1, 1, f, d), lambda i, j: (j, layer, 0, 0)),
        ],
        out_specs=pl.BlockSpec((tm, d), lambda i, j: (i, 0)),
        out_shape=jax.ShapeDtypeStruct((s, d), F32),
        scratch_shapes=[pltpu.VMEM((tm, d), BF16)],
        compiler_params=_params(2),
    )(x, gain, wg_s, wu_s, wd_s)


def _rms_bwd(xv, gain, dh):
    r = lax.rsqrt(jnp.mean(xv * xv, axis=-1, keepdims=True) + EPS)
    xhat = xv * r
    dxhat = dh * gain
    dx = r * (dxhat - xhat * jnp.mean(dxhat * xhat, axis=-1, keepdims=True))
    return dx, jnp.sum(dh * xhat, axis=0, keepdims=True)


def _ffn_bwd(x, dy, gain, wg_s, wu_s, wd_s, layer, name):
    s, d = x.shape
    tm = TOKEN_TILE
    f = wg_s.shape[3]

    def body(x_ref, dy_ref, g_ref, wg_ref, wu_ref, wd_ref, dx_ref, dgain_ref, h_ref, dyb_ref, dg_ref, du_ref, act_ref, acc_s):
        i, j = pl.program_id(0), pl.program_id(1)

        @pl.when((i == 0) & (j == 0))
        def _():
            dgain_ref[...] = jnp.zeros_like(dgain_ref)

        @pl.when(j == 0)
        def _():
            xv = x_ref[...]
            r = lax.rsqrt(jnp.mean(xv * xv, axis=-1, keepdims=True) + EPS)
            h_ref[...] = (xv * r * g_ref[...]).astype(BF16)
            dyb_ref[...] = dy_ref[...].astype(BF16)
            acc_s[...] = jnp.zeros_like(acc_s)

        h = h_ref[...]
        gate = _dot(h, wg_ref[0, 0])
        up = _dot(h, wu_ref[0, 0])
        sig = 1.0 / (1.0 + jnp.exp(-gate))
        silu = gate * sig
        dact = _dot_nt(dyb_ref[...], wd_ref[0, 0])
        dgate = (dact * up * (sig * (1.0 + gate * (1.0 - sig)))).astype(BF16)
        dup = (dact * silu).astype(BF16)
        act_ref[0] = (silu * up).astype(BF16)
        dg_ref[0] = dgate
        du_ref[0] = dup
        acc_s[...] += _dot_nt(dgate, wg_ref[0, 0]) + _dot_nt(dup, wu_ref[0, 0])

        @pl.when(j == N_CHIPS - 1)
        def _():
            dxn, dgain = _rms_bwd(x_ref[...], g_ref[...], acc_s[...])
            dx_ref[...] = dy_ref[...] + dxn
            dgain_ref[...] += dgain

    tok = pl.BlockSpec((tm, d), lambda i, j: (i, 0))
    vec = pl.BlockSpec((1, d), lambda i, j: (0, 0))
    hid = pl.BlockSpec((1, tm, f), lambda i, j: (j, i, 0))
    hid_shape = jax.ShapeDtypeStruct((N_CHIPS, s, f), BF16)
    return pl.pallas_call(
        body,
        name=name,
        grid=(s // tm, N_CHIPS),
        in_specs=[
            tok, tok, vec,
            pl.BlockSpec((1, 1, d, f), lambda i, j: (j, layer, 0, 0)),
            pl.BlockSpec((1, 1, d, f), lambda i, j: (j, layer, 0, 0)),
            pl.BlockSpec((1, 1, f, d), lambda i, j: (j, layer, 0, 0)),
        ],
        out_specs=[tok, vec, tok, tok, hid, hid, hid],
        out_shape=[
            jax.ShapeDtypeStruct((s, d), F32),
            jax.ShapeDtypeStruct((1, d), F32),
            jax.ShapeDtypeStruct((s, d), BF16),
            jax.ShapeDtypeStruct((s, d), BF16),
            hid_shape, hid_shape, hid_shape,
        ],
        scratch_shapes=[pltpu.VMEM((tm, d), F32)],
        compiler_params=_params(2),
    )(x, dy, gain, wg_s, wu_s, wd_s)


def _in_proj_bwd(x, dx_res, gain, dproj, w_s, layer, name):
    s, d = x.shape
    tm = TOKEN_TILE
    n = w_s.shape[3]

    def body(x_ref, r_ref, g_ref, dp_ref, w_ref, dx_ref, dgain_ref):
        @pl.when(pl.program_id(0) == 0)
        def _():
            dgain_ref[...] = jnp.zeros_like(dgain_ref)

        dh = _dot_nt(dp_ref[:, 0:n], w_ref[0, 0])
        for j in range(1, N_CHIPS):
            dh = dh + _dot_nt(dp_ref[:, j * n : (j + 1) * n], w_ref[j, 0])
        dxn, dgain = _rms_bwd(x_ref[...], g_ref[...], dh)
        dx_ref[...] = r_ref[...] + dxn
        dgain_ref[...] += dgain

    tok = pl.BlockSpec((tm, d), lambda i: (i, 0))
    vec = pl.BlockSpec((1, d), lambda i: (0, 0))
    return pl.pallas_call(
        body,
        name=name,
        grid=(s // tm,),
        in_specs=[tok, tok, vec, pl.BlockSpec((tm, N_CHIPS * n), lambda i: (i, 0)), pl.BlockSpec((N_CHIPS, 1, d, n), lambda i: (0, layer, 0, 0))],
        out_specs=[tok, vec],
        out_shape=[jax.ShapeDtypeStruct((s, d), F32), jax.ShapeDtypeStruct((1, d), F32)],
        compiler_params=_params(1),
    )(x, dx_res, gain, dproj, w_s)


def _loss_grad(y, target, name):
    s, d = y.shape
    tm = TOKEN_TILE

    def body(y_ref, t_ref, dy_ref, l_ref):
        @pl.when(pl.program_id(0) == 0)
        def _():
            l_ref[...] = jnp.zeros_like(l_ref)

        err = y_ref[...] - t_ref[...]
        dy_ref[...] = err / d
        l_ref[...] += jnp.sum(err * err, axis=0, keepdims=True) * (0.5 / d)

    tok = pl.BlockSpec((tm, d), lambda i: (i, 0))
    return pl.pallas_call(
        body,
        name=name,
        grid=(s // tm,),
        in_specs=[tok, tok],
        out_specs=[tok, pl.BlockSpec((1, d), lambda i: (0, 0))],
        out_shape=[jax.ShapeDtypeStruct((s, d), F32), jax.ShapeDtypeStruct((1, d), F32)],
        compiler_params=_params(1),
    )(y, target)


def _wgrad(a, b, a_spec, b_spec, n_blocks, k, n, name):
    n_tiles = a.shape[-2] // min(WGRAD_TILE, a.shape[-2])

    def body(a_ref, b_ref, o_ref):
        @pl.when(pl.program_id(1) == 0)
        def _():
            o_ref[...] = jnp.zeros_like(o_ref)

        av = a_ref[0] if len(a_ref.shape) == 3 else a_ref[...]
        bv = b_ref[0] if len(b_ref.shape) == 3 else b_ref[...]
        o_ref[0] += _dot_tn(av, bv)

    return pl.pallas_call(
        body,
        name=name,
        grid=(n_blocks, n_tiles),
        in_specs=[a_spec, b_spec],
        out_specs=pl.BlockSpec((1, k, n), lambda j, i: (j, 0, 0)),
        out_shape=jax.ShapeDtypeStruct((n_blocks, k, n), F32),
        compiler_params=_params(2),
    )(a, b)


def _mesh_position():
    return lax.axis_index("x"), lax.axis_index("y"), lax.axis_index("c")


def _other_chips(x, y):
    return [(1 - x, y), (x, 1 - y), (1 - x, 1 - y)]


def _half_rows(ref_rows, c):
    half = ref_rows // 2
    return pl.ds(c * half, half)


def _gather_weights(shards):
    n = len(shards)

    def body(*refs):
        ins, outs = refs[:n], refs[n : 2 * n]
        send_sems, recv_sems, pass_send_sems, pass_recv_sems, local_sems = refs[2 * n :]
        x, y, c = _mesh_position()
        me = 2 * x + y
        sibling = (x, y, 1 - c)
        chips = _other_chips(x, y)

        def block(t, chip_index, core):
            return outs[t].at[chip_index, :, _half_rows(ins[t].shape[1], core), :]

        def copy(t, k, chip_index, core, to, sems, src=None):
            dst = block(t, chip_index, core)
            return pltpu.make_async_remote_copy(
                src_ref=dst if src is None else src, dst_ref=dst, send_sem=sems[0].at[t, k], recv_sem=sems[1].at[t, k],
                device_id=to, device_id_type=MESH_ID,
            )

        ici, d2d = (send_sems, recv_sems), (pass_send_sems, pass_recv_sems)
        own = [pltpu.make_async_copy(ins[t], outs[t].at[me], local_sems.at[t]) for t in range(n)]
        for cp in own:
            cp.start()
        started = []
        for t in range(n):
            mine = ins[t].at[:, _half_rows(ins[t].shape[1], c), :]
            for k, (px, py) in enumerate(chips):
                started.append(copy(t, k, me, c, (px, py, c), ici, src=mine))
                started[-1].start()
        for t in range(n):
            for k, (px, py) in enumerate(chips):
                copy(t, k, 2 * px + py, c, sibling, ici).wait_recv()
                started.append(copy(t, k, 2 * px + py, c, sibling, d2d))
                started[-1].start()
        for t in range(n):
            for k, (px, py) in enumerate(chips):
                copy(t, k, 2 * px + py, 1 - c, sibling, d2d).wait_recv()
        for cp in started:
            cp.wait_send()
        for cp in own:
            cp.wait()

    sems = pltpu.SemaphoreType.DMA((n, N_CHIPS - 1))
    return pl.pallas_call(
        body,
        name="gather_weights",
        in_specs=[ANY] * n,
        out_specs=[ANY] * n,
        out_shape=[jax.ShapeDtypeStruct((N_CHIPS,) + w.shape, w.dtype) for w in shards],
        scratch_shapes=[sems, sems, sems, sems, pltpu.SemaphoreType.DMA((n,))],
    )(*shards)


def _swap_halves(grads):
    n = len(grads)

    def body(*refs):
        ins, outs = refs[:n], refs[n : 2 * n]
        send_sems, recv_sems = refs[2 * n :]
        x, y, c = _mesh_position()
        copies = []
        for t in range(n):
            copies.append(pltpu.make_async_remote_copy(
                src_ref=ins[t].at[:, _half_rows(ins[t].shape[1], 1 - c), :], dst_ref=outs[t],
                send_sem=send_sems.at[t], recv_sem=recv_sems.at[t], device_id=(x, y, 1 - c), device_id_type=MESH_ID,
            ))
            copies[-1].start()
        for cp in copies:
            cp.wait()

    sems = pltpu.SemaphoreType.DMA((n,))
    return pl.pallas_call(
        body,
        name="swap_halves",
        in_specs=[ANY] * n,
        out_specs=[ANY] * n,
        out_shape=[jax.ShapeDtypeStruct((g.shape[0], g.shape[1] // 2, g.shape[2]), g.dtype) for g in grads],
        scratch_shapes=[sems, sems],
    )(*grads)


def _scatter_to_chips(parts):
    n = len(parts)

    def body(*refs):
        ins, outs = refs[:n], refs[n : 2 * n]
        send_sems, recv_sems = refs[2 * n :]
        x, y, c = _mesh_position()
        copies = []
        for t in range(n):
            for k, (px, py) in enumerate(_other_chips(x, y)):
                copies.append(pltpu.make_async_remote_copy(
                    src_ref=ins[t].at[2 * px + py], dst_ref=outs[t].at[k],
                    send_sem=send_sems.at[t, k], recv_sem=recv_sems.at[t, k], device_id=(px, py, c), device_id_type=MESH_ID,
                ))
                copies[-1].start()
        for cp in copies:
            cp.wait()

    sems = pltpu.SemaphoreType.DMA((n, N_CHIPS - 1))
    return pl.pallas_call(
        body,
        name="scatter_to_chips",
        in_specs=[ANY] * n,
        out_specs=[ANY] * n,
        out_shape=[jax.ShapeDtypeStruct((N_CHIPS - 1,) + p.shape[1:], p.dtype) for p in parts],
        scratch_shapes=[sems, sems],
    )(*parts)


def _join_halves(shards):
    n = len(shards)

    def body(*refs):
        outs = refs[n : 2 * n]
        send_sems, recv_sems = refs[2 * n :]
        x, y, c = _mesh_position()
        copies = []
        for t in range(n):
            mine = outs[t].at[:, _half_rows(outs[t].shape[1], c), :]
            copies.append(pltpu.make_async_remote_copy(
                src_ref=mine, dst_ref=mine, send_sem=send_sems.at[t], recv_sem=recv_sems.at[t],
                device_id=(x, y, 1 - c), device_id_type=MESH_ID,
            ))
            copies[-1].start()
        for cp in copies:
            cp.wait()

    sems = pltpu.SemaphoreType.DMA((n,))
    return pl.pallas_call(
        body,
        name="join_halves",
        in_specs=[ANY] * n,
        out_specs=[ANY] * n,
        out_shape=[jax.ShapeDtypeStruct(g.shape, g.dtype) for g in shards],
        input_output_aliases={t: t for t in range(n)},
        scratch_shapes=[sems, sems],
    )(*shards)


def _gather_small(pack):
    def body(p_ref, o_ref, send_sems, recv_sems, local_sem):
        x, y, c = _mesh_position()
        own = pltpu.make_async_copy(p_ref, o_ref.at[4 * x + 2 * y + c], local_sem)
        own.start()
        copies = []
        for k in range(1, N_DEV):
            px, py, pc = x ^ (k >> 2), y ^ ((k >> 1) & 1), c ^ (k & 1)
            send = pltpu.make_async_remote_copy(
                src_ref=p_ref, dst_ref=o_ref.at[4 * x + 2 * y + c], send_sem=send_sems.at[k - 1], recv_sem=recv_sems.at[k - 1],
                device_id=(px, py, pc), device_id_type=MESH_ID,
            )
            send.start()
            copies.append((send, 4 * px + 2 * py + pc))
        for send, peer_slot in copies:
            send.wait_send()
        for k in range(1, N_DEV):
            px, py, pc = x ^ (k >> 2), y ^ ((k >> 1) & 1), c ^ (k & 1)
            pltpu.make_async_remote_copy(
                src_ref=p_ref, dst_ref=o_ref.at[4 * px + 2 * py + pc], send_sem=send_sems.at[k - 1], recv_sem=recv_sems.at[k - 1],
                device_id=(px, py, pc), device_id_type=MESH_ID,
            ).wait_recv()
        own.wait()

    sems = pltpu.SemaphoreType.DMA((N_DEV - 1,))
    return pl.pallas_call(
        body,
        name="gather_small",
        in_specs=[VMEM_SPEC],
        out_specs=VMEM_SPEC,
        out_shape=jax.ShapeDtypeStruct((N_DEV,) + pack.shape, pack.dtype),
        scratch_shapes=[sems, sems, pltpu.SemaphoreType.DMA],
    )(pack)


def _row_tile(rows):
    for tile in (256, 128, 64, 32, 16, 8):
        if rows % tile == 0:
            return tile
    return rows


def _add_half(grad, received, half_index, name):
    slots, h, cdim = received.shape
    tile = _row_tile(h)
    per_half = h // tile

    def body(c_ref, g_ref, r_ref, o_ref, ob_ref):
        total = g_ref[...] + r_ref[...]
        o_ref[...] = total
        ob_ref[...] = total.astype(BF16)

    block = pl.BlockSpec((1, tile, cdim), lambda j, i, c: (j, i, 0))
    grid_spec = pltpu.PrefetchScalarGridSpec(
        num_scalar_prefetch=1,
        grid=(slots, per_half),
        in_specs=[pl.BlockSpec((1, tile, cdim), lambda j, i, c: (j, c[0] * per_half + i, 0)), block],
        out_specs=[block, block],
    )
    return pl.pallas_call(
        body, name=name, grid_spec=grid_spec,
        out_shape=[jax.ShapeDtypeStruct(received.shape, F32), jax.ShapeDtypeStruct(received.shape, BF16)],
        compiler_params=_params(2),
    )(half_index, grad, received)


def _add_chips(part, received, chip_index, core_index, layer, n_layers, shard, name):
    _, h, cdim = part.shape
    tile = _row_tile(h)
    per_half = h // tile

    def body(chip_ref, core_ref, p_ref, r_ref, *rest):
        o_ref = rest[-1]
        o_ref[0] = ((p_ref[0] + r_ref[0].astype(F32)) + r_ref[1].astype(F32)) + r_ref[2].astype(F32)

    in_specs = [
        pl.BlockSpec((1, tile, cdim), lambda i, chip, core: (chip[0], i, 0)),
        pl.BlockSpec((N_CHIPS - 1, tile, cdim), lambda i, chip, core: (0, i, 0)),
    ]
    operands = [chip_index, core_index, part, received]
    aliases = {}
    if shard is not None:
        in_specs.append(ANY)
        operands.append(shard)
        aliases = {4: 0}
    grid_spec = pltpu.PrefetchScalarGridSpec(
        num_scalar_prefetch=2,
        grid=(per_half,),
        in_specs=in_specs,
        out_specs=pl.BlockSpec((1, tile, cdim), lambda i, chip, core: (layer, core[0] * per_half + i, 0)),
    )
    return pl.pallas_call(
        body, name=name, grid_spec=grid_spec, out_shape=jax.ShapeDtypeStruct((n_layers, 2 * h, cdim), F32),
        input_output_aliases=aliases, compiler_params=_params(1),
    )(*operands)


def _adamw(w, g, m, v, name):
    rows, cdim = w.shape
    tile = _row_tile(rows)

    def body(w_ref, g_ref, m_ref, v_ref, d_ref, nm_ref, nv_ref):
        gv = g_ref[...]
        nm = ADAM_B1 * m_ref[...] + (1.0 - ADAM_B1) * gv
        nv = ADAM_B2 * v_ref[...] + (1.0 - ADAM_B2) * (gv * gv)
        m_hat = nm / (1.0 - ADAM_B1 ** ADAM_STEP)
        v_hat = nv / (1.0 - ADAM_B2 ** ADAM_STEP)
        d_ref[...] = -ADAM_LR * (m_hat / (jnp.sqrt(v_hat) + ADAM_EPS) + ADAM_WD * w_ref[...])
        nm_ref[...] = nm
        nv_ref[...] = nv

    spec = pl.BlockSpec((tile, cdim), lambda i: (i, 0))
    shape = jax.ShapeDtypeStruct((rows, cdim), F32)
    return pl.pallas_call(
        body, name=name, grid=(rows // tile,), in_specs=[spec] * 4, out_specs=[spec] * 3, out_shape=[shape] * 3,
        compiler_params=_params(1),
    )(w, g, m, v)


SMALL_ROWS, SMALL_COLS = 24, 1024
ROW_NORM_MIX, ROW_NORM_FFN, ROW_LOSS, ROW_Q_NORM, ROW_K_NORM, ROW_CONV = 0, 2, 4, 8, 10, 16


def _sum_small(gathered):
    def body(g_ref, o_ref, heads_ref, lanes_ref):
        total = g_ref[0]
        for dev in range(1, N_DEV):
            total = total + g_ref[dev]
        o_ref[...] = total
        heads = o_ref[8:16, 0:LANES]
        for grp in range(1, ATTN_DIM // LANES):
            heads = heads + o_ref[8:16, grp * LANES : (grp + 1) * LANES]
        heads_ref[...] = heads + pltpu.roll(heads, HEAD_DIM, 1)
        lanes_ref[...] = jnp.broadcast_to(jnp.sum(o_ref[0:8, :], axis=-1, keepdims=True), (8, LANES))

    return pl.pallas_call(
        body,
        name="sum_small",
        in_specs=[VMEM_SPEC],
        out_specs=[VMEM_SPEC] * 3,
        out_shape=[jax.ShapeDtypeStruct((SMALL_ROWS, SMALL_COLS), F32), jax.ShapeDtypeStruct((8, LANES), F32), jax.ShapeDtypeStruct((8, LANES), F32)],
    )(gathered)


def _pad_rows(a, rows):
    return jnp.pad(a, ((0, rows - a.shape[0]), (0, 0)))


def _pad_to(a, rows, cols):
    return jnp.pad(a, ((0, rows - a.shape[0]), (0, cols - a.shape[1])))


def _local_step(x, target, norm_mix, q_norm, k_norm, norm_ffn, conv_full, win_s, wout_s, wg_s, wu_s, wd_s):
    n_layers = norm_mix.shape[0]
    s, d = x.shape
    tw = min(WGRAD_TILE, s)
    n_in = win_s.shape[3]
    f = wg_s.shape[3]
    saved = []
    for l in range(n_layers):
        q_gain = jnp.tile(q_norm[l][None, :], (1, 2))
        k_gain = jnp.tile(k_norm[l][None, :], (1, 2))
        h1, proj = _norm_matmul(x, norm_mix[l][None, :], win_s, l, f"in_proj_{l}")
        qn, kn, vb = _qkv_prep(proj, q_gain, k_gain, f"qkv_prep_{l}")
        attn = _attn_fwd(qn, kn, vb, f"attn_fwd_{l}")
        conv = _conv_fwd(proj, conv_full[l], f"conv_fwd_{l}")
        x_mid = _out_proj(x, attn, conv, wout_s, l, f"out_proj_{l}")
        x_out = _ffn_fwd(x_mid, norm_ffn[l][None, :], wg_s, wu_s, wd_s, l, f"ffn_fwd_{l}")
        saved.append(dict(x=x, h1=h1, proj=proj, qn=qn, kn=kn, vb=vb, attn=attn, conv=conv, x_mid=x_mid, q_gain=q_gain, k_gain=k_gain))
        x = x_out

    dy, loss_lanes = _loss_grad(x, target, "loss_grad")
    grads = [None] * n_layers
    for l in reversed(range(n_layers)):
        sv = saved[l]
        dx_mid, d_norm_ffn, h2, dyb, dgate, dup, act = _ffn_bwd(sv["x_mid"], dy, norm_ffn[l][None, :], wg_s, wu_s, wd_s, l, f"ffn_bwd_{l}")
        tok2 = pl.BlockSpec((tw, d), lambda j, i: (i, 0))
        hid = pl.BlockSpec((1, tw, f), lambda j, i: (j, i, 0))
        d_wg = _wgrad(h2, dgate, tok2, hid, N_CHIPS, d, f, f"wgrad_gate_{l}")
        d_wu = _wgrad(h2, dup, tok2, hid, N_CHIPS, d, f, f"wgrad_up_{l}")
        d_wd = _wgrad(act, dyb, hid, tok2, N_CHIPS, f, d, f"wgrad_down_{l}")
        d_attn, d_conv, dxb = _out_proj_bwd(dx_mid, wout_s, l, f"out_proj_bwd_{l}")
        rows_out = wout_s.shape[2]
        mix_spec_a = pl.BlockSpec((tw, rows_out), lambda j, i: (i, j))
        d_wout_a = _wgrad(sv["attn"], dxb, mix_spec_a, tok2, ATTN_DIM // rows_out, rows_out, d, f"wgrad_out_attn_{l}")
        d_wout_c = _wgrad(sv["conv"], dxb, mix_spec_a, tok2, CONV_DIM // rows_out, rows_out, d, f"wgrad_out_conv_{l}")
        d_wout = jnp.concatenate([d_wout_a, d_wout_c], axis=0)
        dq, dk, dv = _attn_bwd(sv["qn"], sv["kn"], sv["vb"], d_attn, f"attn_bwd_{l}")
        dproj, d_conv_w = _conv_bwd(sv["proj"], conv_full[l], d_conv, f"conv_bwd_{l}")
        dproj, d_qg, d_kg = _qkv_prep_bwd(sv["proj"], sv["q_gain"], sv["k_gain"], dq, dk, dv, dproj, f"qkv_prep_bwd_{l}")
        d_win = _wgrad(sv["h1"], dproj, tok2, pl.BlockSpec((tw, n_in), lambda j, i: (i, j)), N_CHIPS, d, n_in, f"wgrad_in_{l}")
        dy, d_norm_mix = _in_proj_bwd(sv["x"], dx_mid, norm_mix[l][None, :], dproj, win_s, l, f"in_proj_bwd_{l}")
        grads[l] = dict(norm_mix=d_norm_mix, norm_ffn=d_norm_ffn, q_norm=d_qg, k_norm=d_kg, conv_w=d_conv_w,
                        w_in=d_win, w_out=d_wout, w_gate=d_wg, w_up=d_wu, w_down=d_wd)
    return loss_lanes, dy, grads


BIG = ("w_in", "w_out", "w_gate", "w_up", "w_down")


def kernel(x, norm_mix, w_in, q_norm, k_norm, conv_w, w_out, norm_ffn, w_gate, w_up, w_down, loss_target, m_norm_mix, m_w_in, m_q_norm, m_k_norm, m_conv_w, m_w_out, m_norm_ffn, m_w_gate, m_w_up, m_w_down, v_norm_mix, v_w_in, v_q_norm, v_k_norm, v_conv_w, v_w_out, v_norm_ffn, v_w_gate, v_w_up, v_w_down):
    n_layers = norm_mix.shape[0]
    weights = dict(w_in=w_in, w_out=w_out, w_gate=w_gate, w_up=w_up, w_down=w_down)
    moments_m = dict(w_in=m_w_in, w_out=m_w_out, w_gate=m_w_gate, w_up=m_w_up, w_down=m_w_down)
    moments_v = dict(w_in=v_w_in, w_out=v_w_out, w_gate=v_w_gate, w_up=v_w_up, w_down=v_w_down)
    cx, cy, cc = _mesh_position()
    chip_index = (2 * cx + cy).astype(jnp.int32).reshape(1)
    core_index = cc.astype(jnp.int32).reshape(1)

    conv_pad = jnp.pad(conv_w, ((0, 0), (0, 16 - conv_w.shape[1]), (0, 0)))
    gathered = _gather_weights([weights[k].astype(BF16) for k in BIG] + [conv_pad])
    win_s, wout_s, wg_s, wu_s, wd_s, conv_s = gathered
    conv_full = jnp.transpose(conv_s[:, :, 0:8], (1, 2, 0, 3)).reshape(n_layers, 8, N_CHIPS * conv_w.shape[2])

    loss_lanes, grad_x, grads = _local_step(
        x[0], loss_target[0], norm_mix, q_norm, k_norm, norm_ffn, conv_full, win_s, wout_s, wg_s, wu_s, wd_s)

    flat = [grads[l][k] for l in range(n_layers) for k in BIG]
    names = [f"{k}_{l}" for l in range(n_layers) for k in BIG]
    received = _swap_halves(flat)
    parts = [_add_half(g, r, core_index, f"add_half_{nm}") for g, r, nm in zip(flat, received, names)]
    from_chips = _scatter_to_chips([p_bf16 for _, p_bf16 in parts])
    shards = [None] * len(BIG)
    for n, ((p_f32, _), r, nm) in enumerate(zip(parts, from_chips, names)):
        layer, tensor = divmod(n, len(BIG))
        shards[tensor] = _add_chips(p_f32, r, chip_index, core_index, layer, n_layers, shards[tensor], f"add_chips_{nm}")
    big_grads = dict(zip(BIG, _join_halves(shards)))

    def lanes(a):
        return _pad_to(a, a.shape[0], SMALL_COLS)

    def tile_of(*groups):
        return _pad_rows(jnp.concatenate([lanes(jnp.concatenate(g, axis=0)) for g in groups], axis=0), 8)

    layers = range(n_layers)
    pack = jnp.concatenate([
        tile_of([grads[l]["norm_mix"] for l in layers], [grads[l]["norm_ffn"] for l in layers], [loss_lanes]),
        tile_of([grads[l]["q_norm"] for l in layers], [grads[l]["k_norm"] for l in layers]),
        tile_of([grads[l]["conv_w"][0:3] for l in layers]),
    ], axis=0)
    small, small_heads, small_lanes = _sum_small(_gather_small(pack))
    loss = small_lanes[ROW_LOSS, 0]
    d_model = norm_mix.shape[1]
    conv_cols = conv_w.shape[2]
    conv_all = small[ROW_CONV : ROW_CONV + 3 * n_layers, 0:CONV_DIM].reshape(n_layers, 3, CONV_DIM)
    small_grads = dict(
        norm_mix=small[ROW_NORM_MIX : ROW_NORM_MIX + n_layers, 0:d_model],
        norm_ffn=small[ROW_NORM_FFN : ROW_NORM_FFN + n_layers, 0:d_model],
        q_norm=small_heads[ROW_Q_NORM - 8 : ROW_Q_NORM - 8 + n_layers, 0:HEAD_DIM],
        k_norm=small_heads[ROW_K_NORM - 8 : ROW_K_NORM - 8 + n_layers, 0:HEAD_DIM],
        conv_w=lax.dynamic_slice_in_dim(conv_all, (2 * cx + cy) * conv_cols, conv_cols, axis=2),
    )

    out_grad, out_delta, out_m, out_v = {}, {}, {}, {}
    for k in BIG:
        shape = weights[k].shape
        view = (shape[0] * shape[1], shape[2])
        g = big_grads[k]
        delta, new_m, new_v = _adamw(weights[k].reshape(view), g.reshape(view), moments_m[k].reshape(view), moments_v[k].reshape(view), f"adamw_{k}")
        out_grad[k], out_delta[k], out_m[k], out_v[k] = g, delta.reshape(shape), new_m.reshape(shape), new_v.reshape(shape)

    small_w = dict(norm_mix=norm_mix, norm_ffn=norm_ffn, q_norm=q_norm, k_norm=k_norm, conv_w=conv_w)
    small_m = dict(norm_mix=m_norm_mix, norm_ffn=m_norm_ffn, q_norm=m_q_norm, k_norm=m_k_norm, conv_w=m_conv_w)
    small_v = dict(norm_mix=v_norm_mix, norm_ffn=v_norm_ffn, q_norm=v_q_norm, k_norm=v_k_norm, conv_w=v_conv_w)
    order = ("norm_mix", "norm_ffn", "q_norm", "k_norm", "conv_w")

    def packed(tree):
        parts2 = [_pad_to(tree[k].reshape(-1, tree[k].shape[-1]), tree[k].reshape(-1, tree[k].shape[-1]).shape[0], SMALL_COLS) for k in order]
        return _pad_rows(jnp.concatenate(parts2, axis=0), SMALL_ROWS)

    delta_p, m_p, v_p = _adamw(packed(small_w), packed(small_grads), packed(small_m), packed(small_v), "adamw_small")
    row = 0
    for k in order:
        shape = small_w[k].shape
        n_rows = 1
        for dim in shape[:-1]:
            n_rows *= dim
        cut = (slice(row, row + n_rows), slice(0, shape[-1]))
        out_grad[k] = small_grads[k]
        out_delta[k], out_m[k], out_v[k] = delta_p[cut].reshape(shape), m_p[cut].reshape(shape), v_p[cut].reshape(shape)
        row += n_rows

    names_out = ("norm_mix", "w_in", "q_norm", "k_norm", "conv_w", "w_out", "norm_ffn", "w_gate", "w_up", "w_down")
    return (loss, grad_x[None], *[out_grad[k] for k in names_out], *[out_delta[k] for k in names_out],
            *[out_m[k] for k in names_out], *[out_v[k] for k in names_out])
```

```python
import functools

import jax
import jax.numpy as jnp
from jax import lax
from jax.experimental import pallas as pl
from jax.experimental.pallas import tpu as pltpu

F32 = jnp.float32
BF16 = jnp.bfloat16

EPS = 1e-6
HEAD_DIM = 64
LANES = 128
ATTN_DIM = 512
CONV_DIM = 512
N_CHIPS = 4
N_DEV = 8
Q_SCALE = HEAD_DIM ** -0.5
ATTN_Q_TILE = 256
ATTN_TILE = 256
TOKEN_TILE = 512
WGRAD_TILE = 2048
VMEM_LIMIT = 56 * 1024 * 1024

ADAM_LR = 0.001
ADAM_B1 = 0.9
ADAM_B2 = 0.999
ADAM_EPS = 1e-08
ADAM_WD = 0.01
ADAM_STEP = 10

MESH_ID = pl.DeviceIdType.MESH
ANY = pl.BlockSpec(memory_space=pl.ANY)
VMEM_SPEC = pl.BlockSpec(memory_space=pltpu.VMEM)


def _params(n_axes):
    return pltpu.CompilerParams(dimension_semantics=("arbitrary",) * n_axes, vmem_limit_bytes=VMEM_LIMIT)


def _dot(a, b):
    return jnp.dot(a, b, preferred_element_type=F32)


def _dot_nt(a, b):
    return lax.dot_general(a, b, (((1,), (1,)), ((), ())), preferred_element_type=F32)


def _dot_tn(a, b):
    return lax.dot_general(a, b, (((0,), (0,)), ((), ())), preferred_element_type=F32)


SCORE_MAX = 80.0
UNDERFLOW_EXIT = 90.0


def _scores(q, k):
    return jnp.minimum(_dot_nt(q, k), SCORE_MAX)


def _softplus(z):
    return jnp.log(1.0 + jnp.exp(z))


def _norm_matmul(x, gain, w_s, layer, name):
    s, d = x.shape
    n_blocks, _, _, n = w_s.shape
    tm = TOKEN_TILE

    def body(x_ref, g_ref, w_ref, h_ref, o_ref):
        xv = x_ref[...]
        r = lax.rsqrt(jnp.mean(xv * xv, axis=-1, keepdims=True) + EPS)
        h = (xv * r * g_ref[...]).astype(BF16)
        h_ref[...] = h
        for j in range(n_blocks):
            o_ref[:, j * n : (j + 1) * n] = _dot(h, w_ref[j, 0])

    return pl.pallas_call(
        body,
        name=name,
        grid=(s // tm,),
        in_specs=[
            pl.BlockSpec((tm, d), lambda i: (i, 0)),
            pl.BlockSpec((1, d), lambda i: (0, 0)),
            pl.BlockSpec((n_blocks, 1, d, n), lambda i: (0, layer, 0, 0)),
        ],
        out_specs=[pl.BlockSpec((tm, d), lambda i: (i, 0)), pl.BlockSpec((tm, n_blocks * n), lambda i: (i, 0))],
        out_shape=[jax.ShapeDtypeStruct((s, d), BF16), jax.ShapeDtypeStruct((s, n_blocks * n), F32)],
        compiler_params=_params(1),
    )(x, gain, w_s)


def _head_norm(xv, gain, low):
    sq = xv * xv
    s_low = jnp.sum(jnp.where(low, sq, 0.0), axis=-1, keepdims=True)
    s_high = jnp.sum(jnp.where(low, 0.0, sq), axis=-1, keepdims=True)
    r = jnp.where(low, lax.rsqrt(s_low / HEAD_DIM + EPS), lax.rsqrt(s_high / HEAD_DIM + EPS))
    return xv * r * gain, r


def _qkv_prep(proj, q_gain, k_gain, name):
    s = proj.shape[0]
    tm = TOKEN_TILE

    def body(p_ref, qg_ref, kg_ref, q_ref, k_ref, v_ref):
        low = lax.broadcasted_iota(jnp.int32, (tm, LANES), 1) < HEAD_DIM
        for g in range(ATTN_DIM // LANES):
            cq = slice(LANES * g, LANES * (g + 1))
            ck = slice(ATTN_DIM + LANES * g, ATTN_DIM + LANES * (g + 1))
            cv = slice(2 * ATTN_DIM + LANES * g, 2 * ATTN_DIM + LANES * (g + 1))
            qn, _ = _head_norm(p_ref[:, cq], qg_ref[...], low)
            kn, _ = _head_norm(p_ref[:, ck], kg_ref[...], low)
            q_ref[:, cq] = (qn * Q_SCALE).astype(BF16)
            k_ref[:, cq] = kn.astype(BF16)
            v_ref[:, cq] = p_ref[:, cv].astype(BF16)

    out = jax.ShapeDtypeStruct((s, ATTN_DIM), BF16)
    return pl.pallas_call(
        body,
        name=name,
        grid=(s // tm,),
        in_specs=[
            pl.BlockSpec((tm, 3 * ATTN_DIM), lambda i: (i, 0)),
            pl.BlockSpec((1, LANES), lambda i: (0, 0)),
            pl.BlockSpec((1, LANES), lambda i: (0, 0)),
        ],
        out_specs=[pl.BlockSpec((tm, ATTN_DIM), lambda i: (i, 0))] * 3,
        out_shape=[out, out, out],
        compiler_params=_params(1),
    )(proj, q_gain, k_gain)


def _qkv_prep_bwd(proj, q_gain, k_gain, dq, dk, dv, dproj, name):
    s = proj.shape[0]
    tm = TOKEN_TILE

    def norm_bwd(xv, gain, dy, low):
        _, r = _head_norm(xv, gain, low)
        xhat = xv * r
        dxhat = dy * gain
        prod = dxhat * xhat
        m_low = jnp.sum(jnp.where(low, prod, 0.0), axis=-1, keepdims=True)
        m_high = jnp.sum(jnp.where(low, 0.0, prod), axis=-1, keepdims=True)
        mean = jnp.where(low, m_low, m_high) / HEAD_DIM
        return r * (dxhat - xhat * mean), jnp.sum(dy * xhat, axis=0, keepdims=True)

    def body(p_ref, qg_ref, kg_ref, dq_ref, dk_ref, dv_ref, dproj_ref, dp_ref, dqg_ref, dkg_ref):
        @pl.when(pl.program_id(0) == 0)
        def _():
            dqg_ref[...] = jnp.zeros_like(dqg_ref)
            dkg_ref[...] = jnp.zeros_like(dkg_ref)

        low = lax.broadcasted_iota(jnp.int32, (tm, LANES), 1) < HEAD_DIM
        for g in range(ATTN_DIM // LANES):
            cq = slice(LANES * g, LANES * (g + 1))
            ck = slice(ATTN_DIM + LANES * g, ATTN_DIM + LANES * (g + 1))
            cv = slice(2 * ATTN_DIM + LANES * g, 2 * ATTN_DIM + LANES * (g + 1))
            dxq, dgq = norm_bwd(p_ref[:, cq], qg_ref[...], dq_ref[:, cq] * Q_SCALE, low)
            dxk, dgk = norm_bwd(p_ref[:, ck], kg_ref[...], dk_ref[:, cq], low)
            dp_ref[:, cq] = dxq.astype(BF16)
            dp_ref[:, ck] = dxk.astype(BF16)
            dp_ref[:, cv] = dv_ref[:, cq].astype(BF16)
            dqg_ref[:, cq] += dgq
            dkg_ref[:, cq] += dgk

    grad_spec = pl.BlockSpec((tm, ATTN_DIM), lambda i: (i, 0))
    gain_spec = pl.BlockSpec((1, LANES), lambda i: (0, 0))
    sum_spec = pl.BlockSpec((1, ATTN_DIM), lambda i: (0, 0))
    return pl.pallas_call(
        body,
        name=name,
        grid=(s // tm,),
        in_specs=[pl.BlockSpec((tm, 3 * ATTN_DIM), lambda i: (i, 0)), gain_spec, gain_spec, grad_spec, grad_spec, grad_spec, ANY],
        out_specs=[pl.BlockSpec((tm, 3 * ATTN_DIM), lambda i: (i, 0)), sum_spec, sum_spec],
        out_shape=[
            jax.ShapeDtypeStruct(dproj.shape, BF16),
            jax.ShapeDtypeStruct((1, ATTN_DIM), F32),
            jax.ShapeDtypeStruct((1, ATTN_DIM), F32),
        ],
        input_output_aliases={6: 0},
        compiler_params=_params(1),
    )(proj, q_gain, k_gain, dq, dk, dv, dproj)


def _attn_tile_consts(t):
    row = lax.broadcasted_iota(jnp.int32, (t, t), 0)
    col = lax.broadcasted_iota(jnp.int32, (t, t), 1)
    return row, col


def _triangle_sum(v, triangle):
    return _dot(v.astype(BF16), triangle)


def _attn_fwd(qn, kn, vb, name):
    s = qn.shape[0]
    t = min(ATTN_TILE, s)
    tq = min(ATTN_Q_TILE, t)
    per_key_tile = t // tq

    def body(q_ref, k_ref, v_ref, o_ref):
        i = pl.program_id(1) // per_key_tile
        low = lax.broadcasted_iota(jnp.int32, (tq, LANES), 1) < HEAD_DIM
        row, col = _attn_tile_consts(t)
        suffix = (row > col).astype(BF16)
        first_row = (pl.program_id(1) % per_key_tile) * tq
        causal = lax.broadcasted_iota(jnp.int32, (tq, t), 1) < lax.broadcasted_iota(jnp.int32, (tq, t), 0) + first_row
        q = q_ref[...]
        zero_q = jnp.zeros_like(q)
        qh = (jnp.where(low, q, zero_q), jnp.where(low, zero_q, q))

        def step(kbs, carry, diagonal_first=False):
            chains = [(head, m) for head in range(2) for m in range(len(kbs))]
            masked = [diagonal_first and m == 0 for _, m in chains]
            ks = [k_ref[pl.ds(pl.multiple_of(kb * t, t), t), :] for kb in kbs]
            vs = [v_ref[pl.ds(pl.multiple_of(kb * t, t), t), :] for kb in kbs]
            z = [_scores(qh[head], ks[kb]) for head, kb in chains]
            sp = [_softplus(zc) for zc in z]
            sp = [jnp.where(causal, s_, 0.0) if mk else s_ for s_, mk in zip(sp, masked)]
            inside = [_triangle_sum(s_, suffix) for s_ in sp]
            after = [carry[head][1] for head in range(2)]
            log_a = []
            for n, (head, kb) in enumerate(chains):
                log_a.append(z[n] - sp[n] - inside[n] - after[head])
                after[head] = after[head] + jnp.sum(sp[n], axis=-1, keepdims=True)
            a = [jnp.exp(l_) for l_ in log_a]
            a = [jnp.where(causal, a_, 0.0) if mk else a_ for a_, mk in zip(a, masked)]
            acc = [carry[head][0] for head in range(2)]
            for n, (head, kb) in enumerate(chains):
                acc[head] = acc[head] + _dot(a[n].astype(BF16), vs[kb])
            return tuple((acc[head], after[head]) for head in range(2))

        def live(c):
            return jnp.minimum(jnp.min(c[0][1]), jnp.min(c[1][1])) < UNDERFLOW_EXIT

        zero = (jnp.zeros((tq, LANES), F32), jnp.zeros((tq, 1), F32))
        carry = lax.cond(i >= 1, lambda c: step((i, i - 1), c, True), lambda c: step((i,), c, True), (zero, zero))
        rest = jnp.maximum(i - 1, 0)
        carry = lax.cond((rest % 2 == 1) & live(carry), lambda c: step((i - 2,), c), lambda c: c, carry)
        pairs = rest // 2
        _, carry = lax.while_loop(
            lambda st: (st[0] < pairs) & live(st[1]),
            lambda st: (st[0] + 1, step((2 * (pairs - st[0]) - 1, 2 * (pairs - st[0]) - 2), st[1])),
            (jnp.int32(0), carry))
        o_ref[...] = jnp.where(low, carry[0][0], carry[1][0]).astype(BF16)

    return pl.pallas_call(
        body,
        name=name,
        grid=(ATTN_DIM // LANES, s // tq),
        in_specs=[
            pl.BlockSpec((tq, LANES), lambda p, i: (i, p)),
            pl.BlockSpec((s, LANES), lambda p, i: (0, p)),
            pl.BlockSpec((s, LANES), lambda p, i: (0, p)),
        ],
        out_specs=pl.BlockSpec((tq, LANES), lambda p, i: (i, p)),
        out_shape=jax.ShapeDtypeStruct((s, ATTN_DIM), BF16),
        compiler_params=_params(2),
    )(qn, kn, vb)


def _attn_bwd(qn, kn, vb, do, name):
    s = qn.shape[0]
    t = min(ATTN_TILE, s)
    nq = s // t

    def body(q_ref, k_ref, v_ref, do_ref, dq_ref, dk_ref, dv_ref, a_s, sg_s):
        i = pl.program_id(1)

        @pl.when(i == 0)
        def _():
            dk_ref[...] = jnp.zeros_like(dk_ref)
            dv_ref[...] = jnp.zeros_like(dv_ref)

        low = lax.broadcasted_iota(jnp.int32, (t, LANES), 1) < HEAD_DIM
        row, col = _attn_tile_consts(t)
        suffix = (row > col).astype(BF16)
        prefix = (row < col).astype(BF16)
        causal = col < row
        q = q_ref[...]
        dob = do_ref[...]
        zero_q = jnp.zeros_like(q)
        heads = []
        for head in range(2):
            if head == 0:
                qh, doh = jnp.where(low, q, zero_q), jnp.where(low, dob, zero_q)
            else:
                qh, doh = jnp.where(low, zero_q, q), jnp.where(low, zero_q, dob)

            def rows_of(kb):
                return pl.ds(pl.multiple_of(kb * t, t), t)

            def pass1(kbs, after, diagonal_first=False):
                z = [_scores(qh, k_ref[rows_of(kb), :]) for kb in kbs]
                sp = [_softplus(z_) for z_ in z]
                if diagonal_first:
                    sp[0] = jnp.where(causal, sp[0], 0.0)
                inside = [_triangle_sum(s_, suffix) for s_ in sp]
                for n, kb in enumerate(kbs):
                    log_sg = z[n] - sp[n]
                    a = jnp.exp(log_sg - inside[n] - after)
                    sg = jnp.exp(log_sg)
                    if diagonal_first and n == 0:
                        a = jnp.where(causal, a, 0.0)
                        sg = jnp.where(causal, sg, 0.0)
                    a_s[kb] = a
                    sg_s[kb] = sg
                    after = after + jnp.sum(sp[n], axis=-1, keepdims=True)
                return after

            def live(after):
                return jnp.min(after) < UNDERFLOW_EXIT

            after = jnp.zeros((t, 1), F32)
            after = lax.cond(i >= 1, lambda c: pass1((i, i - 1), c, True), lambda c: pass1((i,), c, True), after)
            rest = jnp.maximum(i - 1, 0)
            take_single = (rest % 2 == 1) & live(after)
            after = lax.cond(take_single, lambda c: pass1((i - 2,), c), lambda c: c, after)
            pairs = rest // 2
            pairs_done, _ = lax.while_loop(
                lambda st: (st[0] < pairs) & live(st[1]),
                lambda st: (st[0] + 1, pass1((2 * (pairs - st[0]) - 1, 2 * (pairs - st[0]) - 2), st[1])),
                (jnp.int32(0), after))
            walked = jnp.minimum(i, 1) + 1 + take_single.astype(jnp.int32) + 2 * pairs_done
            first = i - walked + 1

            def pass2(kbs, carry):
                dq, before = carry
                ks = [k_ref[rows_of(kb), :] for kb in kbs]
                a = [a_s[kb] for kb in kbs]
                g = [a_ * _dot_nt(doh, v_ref[rows_of(kb), :]) for a_, kb in zip(a, kbs)]
                for n, kb in enumerate(kbs):
                    dv_ref[rows_of(kb), :] += _dot_tn(a[n].astype(BF16), doh)
                inside = [_triangle_sum(g_, prefix) for g_ in g]
                dz = []
                for n, kb in enumerate(kbs):
                    sg = sg_s[kb]
                    dz.append((g[n] - sg * (g[n] + inside[n] + before)).astype(BF16))
                    before = before + jnp.sum(g[n], axis=-1, keepdims=True)
                for n, kb in enumerate(kbs):
                    dk_ref[rows_of(kb), :] += _dot_tn(dz[n], qh)
                for n in range(len(kbs)):
                    dq = dq + _dot(dz[n], ks[n])
                return dq, before

            carry = (jnp.zeros((t, LANES), F32), jnp.zeros((t, 1), F32))
            carry = lax.fori_loop(0, walked // 2, lambda n, c: pass2((first + 2 * n, first + 2 * n + 1), c), carry)
            carry = lax.cond(walked % 2 == 1, lambda c: pass2((i,), c), lambda c: c, carry)
            heads.append(carry[0])
        dq_ref[...] = jnp.where(low, heads[0], heads[1])

    q_spec = pl.BlockSpec((t, LANES), lambda p, i: (i, p))
    kv_spec = pl.BlockSpec((s, LANES), lambda p, i: (0, p))
    return pl.pallas_call(
        body,
        name=name,
        grid=(ATTN_DIM // LANES, nq),
        in_specs=[q_spec, kv_spec, kv_spec, q_spec],
        out_specs=[q_spec, kv_spec, kv_spec],
        out_shape=[jax.ShapeDtypeStruct((s, ATTN_DIM), F32)] * 3,
        scratch_shapes=[pltpu.VMEM((nq, t, t), F32), pltpu.VMEM((nq, t, t), F32)],
        compiler_params=_params(2),
    )(qn, kn, vb, do)


CB_BLOCK, CC_BLOCK, CU_BLOCK = 3, 4, 5


def _shift_down(h, prev_rows, n):
    row = lax.broadcasted_iota(jnp.int32, h.shape, 0)
    out = pltpu.roll(h, n, 0)
    for r in range(n):
        out = jnp.where(row == r, prev_rows[len(prev_rows) - n + r], out)
    return out


def _shift_up(h, next_rows, n):
    tm = h.shape[0]
    row = lax.broadcasted_iota(jnp.int32, h.shape, 0)
    out = pltpu.roll(h, tm - n, 0)
    for r in range(n):
        out = jnp.where(row == tm - n + r, next_rows[r], out)
    return out


def _conv_fwd(proj, conv_w, name):
    s = proj.shape[0]
    tm = TOKEN_TILE
    nb = tm // 8

    def body(cb_ref, cc_ref, cu_ref, pc_ref, pu_ref, w_ref, o_ref):
        first = pl.program_id(0) == 0
        h = cc_ref[...] * cu_ref[...]
        prev = [jnp.where(first, 0.0, pc_ref[r : r + 1, :] * pu_ref[r : r + 1, :]) for r in (6, 7)]
        y = w_ref[0:1, :] * _shift_down(h, prev, 2) + w_ref[1:2, :] * _shift_down(h, prev, 1) + w_ref[2:3, :] * h
        o_ref[...] = (cb_ref[...] * y).astype(BF16)

    def col(block):
        return pl.BlockSpec((tm, CONV_DIM), lambda i: (i, block))

    def halo(block):
        return pl.BlockSpec((8, CONV_DIM), lambda i: (jnp.maximum(i * nb - 1, 0), block))

    return pl.pallas_call(
        body,
        name=name,
        grid=(s // tm,),
        in_specs=[col(CB_BLOCK), col(CC_BLOCK), col(CU_BLOCK), halo(CC_BLOCK), halo(CU_BLOCK), pl.BlockSpec((8, CONV_DIM), lambda i: (0, 0))],
        out_specs=pl.BlockSpec((tm, CONV_DIM), lambda i: (i, 0)),
        out_shape=jax.ShapeDtypeStruct((s, CONV_DIM), BF16),
        compiler_params=_params(1),
    )(proj, proj, proj, proj, proj, conv_w)


def _conv_bwd(proj, conv_w, dconv, name):
    s = proj.shape[0]
    tm = TOKEN_TILE
    nb = tm // 8
    n_tiles = s // tm

    def body(cb_ref, cc_ref, cu_ref, dy_ref, pc_ref, pu_ref, nb_ref, ndy_ref, w_ref, dp_ref, dw_ref):
        i = pl.program_id(0)

        @pl.when(i == 0)
        def _():
            dw_ref[...] = jnp.zeros_like(dw_ref)

        first = i == 0
        last = i == n_tiles - 1
        cc, cu, cb, dy = cc_ref[...], cu_ref[...], cb_ref[...], dy_ref[...]
        h = cc * cu
        prev = [jnp.where(first, 0.0, pc_ref[r : r + 1, :] * pu_ref[r : r + 1, :]) for r in (6, 7)]
        h1 = _shift_down(h, prev, 1)
        h2 = _shift_down(h, prev, 2)
        y = w_ref[0:1, :] * h2 + w_ref[1:2, :] * h1 + w_ref[2:3, :] * h
        dyb = dy * cb
        nxt = [jnp.where(last, 0.0, ndy_ref[r : r + 1, :] * nb_ref[r : r + 1, :]) for r in (0, 1)]
        dh = w_ref[2:3, :] * dyb + w_ref[1:2, :] * _shift_up(dyb, nxt, 1) + w_ref[0:1, :] * _shift_up(dyb, nxt, 2)
        dp_ref[:, 0:CONV_DIM] = (dy * y).astype(BF16)
        dp_ref[:, CONV_DIM : 2 * CONV_DIM] = (dh * cu).astype(BF16)
        dp_ref[:, 2 * CONV_DIM : 3 * CONV_DIM] = (dh * cc).astype(BF16)
        dw_ref[0:1, :] += jnp.sum(dyb * h2, axis=0, keepdims=True)
        dw_ref[1:2, :] += jnp.sum(dyb * h1, axis=0, keepdims=True)
        dw_ref[2:3, :] += jnp.sum(dyb * h, axis=0, keepdims=True)

    def col(block):
        return pl.BlockSpec((tm, CONV_DIM), lambda i: (i, block))

    def halo_prev(block):
        return pl.BlockSpec((8, CONV_DIM), lambda i: (jnp.maximum(i * nb - 1, 0), block))

    def halo_next(block):
        return pl.BlockSpec((8, CONV_DIM), lambda i: (jnp.minimum((i + 1) * nb, s // 8 - 1), block))

    return pl.pallas_call(
        body,
        name=name,
        grid=(n_tiles,),
        in_specs=[
            col(CB_BLOCK), col(CC_BLOCK), col(CU_BLOCK), col(0),
            halo_prev(CC_BLOCK), halo_prev(CU_BLOCK), halo_next(CB_BLOCK), halo_next(0),
            pl.BlockSpec((8, CONV_DIM), lambda i: (0, 0)),
        ],
        out_specs=[pl.BlockSpec((tm, 3 * CONV_DIM), lambda i: (i, 1)), pl.BlockSpec((8, CONV_DIM), lambda i: (0, 0))],
        out_shape=[jax.ShapeDtypeStruct((s, 3 * ATTN_DIM + 3 * CONV_DIM), BF16), jax.ShapeDtypeStruct((8, CONV_DIM), F32)],
        compiler_params=_params(1),
    )(proj, proj, proj, dconv, proj, proj, proj, dconv, conv_w)


def _out_proj(x, attn, conv, w_s, layer, name):
    s, d = x.shape
    tm = TOKEN_TILE
    rows = w_s.shape[2]

    def body(x_ref, a_ref, c_ref, w_ref, o_ref):
        acc = x_ref[...]
        for j in range(N_CHIPS):
            src = a_ref if j < 2 else c_ref
            cols = slice((j % 2) * rows, (j % 2 + 1) * rows)
            acc = acc + _dot(src[:, cols], w_ref[j, 0])
        o_ref[...] = acc

    return pl.pallas_call(
        body,
        name=name,
        grid=(s // tm,),
        in_specs=[
            pl.BlockSpec((tm, d), lambda i: (i, 0)),
            pl.BlockSpec((tm, ATTN_DIM), lambda i: (i, 0)),
            pl.BlockSpec((tm, CONV_DIM), lambda i: (i, 0)),
            pl.BlockSpec((N_CHIPS, 1, rows, d), lambda i: (0, layer, 0, 0)),
        ],
        out_specs=pl.BlockSpec((tm, d), lambda i: (i, 0)),
        out_shape=jax.ShapeDtypeStruct((s, d), F32),
        compiler_params=_params(1),
    )(x, attn, conv, w_s)


def _out_proj_bwd(dx, w_s, layer, name):
    s, d = dx.shape
    tm = TOKEN_TILE
    rows = w_s.shape[2]

    def body(dx_ref, w_ref, da_ref, dc_ref, dxb_ref):
        dxb = dx_ref[...].astype(BF16)
        dxb_ref[...] = dxb
        for j in range(N_CHIPS):
            cols = slice((j % 2) * rows, (j % 2 + 1) * rows)
            part = _dot_nt(dxb, w_ref[j, 0])
            if j < 2:
                da_ref[:, cols] = part.astype(BF16)
            else:
                dc_ref[:, cols] = part

    return pl.pallas_call(
        body,
        name=name,
        grid=(s // tm,),
        in_specs=[pl.BlockSpec((tm, d), lambda i: (i, 0)), pl.BlockSpec((N_CHIPS, 1, rows, d), lambda i: (0, layer, 0, 0))],
        out_specs=[
            pl.BlockSpec((tm, ATTN_DIM), lambda i: (i, 0)),
            pl.BlockSpec((tm, CONV_DIM), lambda i: (i, 0)),
            pl.BlockSpec((tm, d), lambda i: (i, 0)),
        ],
        out_shape=[
            jax.ShapeDtypeStruct((s, ATTN_DIM), BF16),
            jax.ShapeDtypeStruct((s, CONV_DIM), F32),
            jax.ShapeDtypeStruct((s, d), BF16),
        ],
        compiler_params=_params(1),
    )(dx, w_s)


def _ffn_fwd(x, gain, wg_s, wu_s, wd_s, layer, name):
    s, d = x.shape
    tm = TOKEN_TILE
    f = wg_s.shape[3]

    def body(x_ref, g_ref, wg_ref, wu_ref, wd_ref, o_ref, gate_ref, up_ref, h_s):
        j = pl.program_id(1)

        @pl.when(j == 0)
        def _():
            xv = x_ref[...]
            r = lax.rsqrt(jnp.mean(xv * xv, axis=-1, keepdims=True) + EPS)
            h_s[...] = (xv * r * g_ref[...]).astype(BF16)
            o_ref[...] = xv

        halves = [slice(0, tm // 2), slice(tm // 2, tm)]
        pre = [(_dot(h_s[r, :], wg_ref[0, 0]), _dot(h_s[r, :], wu_ref[0, 0])) for r in halves]
        act = [((gate / (1.0 + jnp.exp(-gate))) * up).astype(BF16) for gate, up in pre]
        for r, (gate, up) in zip(halves, pre):
            gate_ref[0, r, :] = gate
            up_ref[0, r, :] = up
        for r, a in zip(halves, act):
            o_ref[r, :] += _dot(a, wd_ref[0, 0])

    hid = pl.BlockSpec((1, tm, f), lambda i, j: (j, i, 0))
    hid_shape = jax.ShapeDtypeStruct((N_CHIPS, s, f), F32)
    return pl.pallas_call(
        body,
        name=name,
        grid=(s // tm, N_CHIPS),
        in_specs=[
            pl.BlockSpec((tm, d), lambda i, j: (i, 0)),
            pl.BlockSpec((1, d), lambda i, j: (0, 0)),
            pl.BlockSpec((1, 1, d, f), lambda i, j: (j, layer, 0, 0)),
            pl.BlockSpec((1, 1, d, f), lambda i, j: (j, layer, 0, 0)),
            pl.BlockSpec((1, 1, f, d), lambda i, j: (j, layer, 0, 0)),
        ],
        out_specs=[pl.BlockSpec((tm, d), lambda i, j: (i, 0)), hid, hid],
        out_shape=[jax.ShapeDtypeStruct((s, d), F32), hid_shape, hid_shape],
        scratch_shapes=[pltpu.VMEM((tm, d), BF16)],
        compiler_params=_params(2),
    )(x, gain, wg_s, wu_s, wd_s)


def _rms_bwd(xv, gain, dh):
    r = lax.rsqrt(jnp.mean(xv * xv, axis=-1, keepdims=True) + EPS)
    xhat = xv * r
    dxhat = dh * gain
    dx = r * (dxhat - xhat * jnp.mean(dxhat * xhat, axis=-1, keepdims=True))
    return dx, jnp.sum(dh * xhat, axis=0, keepdims=True)


def _ffn_bwd(x, dy, gain, gate_s, up_s, wg_s, wu_s, wd_s, layer, name):
    s, d = x.shape
    tm = TOKEN_TILE
    f = wg_s.shape[3]

    def body(x_ref, dy_ref, g_ref, gate_ref, up_ref, wg_ref, wu_ref, wd_ref, dx_ref, dgain_ref, h_ref, dyb_ref, dg_ref, du_ref, act_ref, acc_s):
        i, j = pl.program_id(0), pl.program_id(1)

        @pl.when((i == 0) & (j == 0))
        def _():
            dgain_ref[...] = jnp.zeros_like(dgain_ref)

        @pl.when(j == 0)
        def _():
            xv = x_ref[...]
            r = lax.rsqrt(jnp.mean(xv * xv, axis=-1, keepdims=True) + EPS)
            h_ref[...] = (xv * r * g_ref[...]).astype(BF16)
            dyb_ref[...] = dy_ref[...].astype(BF16)
            acc_s[...] = jnp.zeros_like(acc_s)

        halves = [slice(0, tm // 2), slice(tm // 2, tm)]
        pre = [(gate_ref[0, r, :], up_ref[0, r, :], _dot_nt(dyb_ref[r, :], wd_ref[0, 0])) for r in halves]
        grads = []
        for r, (gate, up, dact) in zip(halves, pre):
            sig = 1.0 / (1.0 + jnp.exp(-gate))
            silu = gate * sig
            dgate = (dact * up * (sig * (1.0 + gate * (1.0 - sig)))).astype(BF16)
            dup = (dact * silu).astype(BF16)
            act_ref[0, r, :] = (silu * up).astype(BF16)
            dg_ref[0, r, :] = dgate
            du_ref[0, r, :] = dup
            grads.append((dgate, dup))
        for r, (dgate, dup) in zip(halves, grads):
            acc_s[r, :] += _dot_nt(dgate, wg_ref[0, 0]) + _dot_nt(dup, wu_ref[0, 0])

        @pl.when(j == N_CHIPS - 1)
        def _():
            dxn, dgain = _rms_bwd(x_ref[...], g_ref[...], acc_s[...])
            dx_ref[...] = dy_ref[...] + dxn
            dgain_ref[...] += dgain

    tok = pl.BlockSpec((tm, d), lambda i, j: (i, 0))
    vec = pl.BlockSpec((1, d), lambda i, j: (0, 0))
    hid = pl.BlockSpec((1, tm, f), lambda i, j: (j, i, 0))
    hid_shape = jax.ShapeDtypeStruct((N_CHIPS, s, f), BF16)
    return pl.pallas_call(
        body,
        name=name,
        grid=(s // tm, N_CHIPS),
        in_specs=[
            tok, tok, vec, hid, hid,
            pl.BlockSpec((1, 1, d, f), lambda i, j: (j, layer, 0, 0)),
            pl.BlockSpec((1, 1, d, f), lambda i, j: (j, layer, 0, 0)),
            pl.BlockSpec((1, 1, f, d), lambda i, j: (j, layer, 0, 0)),
        ],
        out_specs=[tok, vec, tok, tok, hid, hid, hid],
        out_shape=[
            jax.ShapeDtypeStruct((s, d), F32),
            jax.ShapeDtypeStruct((1, d), F32),
            jax.ShapeDtypeStruct((s, d), BF16),
            jax.ShapeDtypeStruct((s, d), BF16),
            hid_shape, hid_shape, hid_shape,
        ],
        scratch_shapes=[pltpu.VMEM((tm, d), F32)],
        compiler_params=_params(2),
    )(x, dy, gain, gate_s, up_s, wg_s, wu_s, wd_s)


def _in_proj_bwd(x, dx_res, gain, dproj, w_s, layer, name):
    s, d = x.shape
    tm = TOKEN_TILE
    n = w_s.shape[3]

    def body(x_ref, r_ref, g_ref, dp_ref, w_ref, dx_ref, dgain_ref):
        @pl.when(pl.program_id(0) == 0)
        def _():
            dgain_ref[...] = jnp.zeros_like(dgain_ref)

        dh = _dot_nt(dp_ref[:, 0:n], w_ref[0, 0])
        for j in range(1, N_CHIPS):
            dh = dh + _dot_nt(dp_ref[:, j * n : (j + 1) * n], w_ref[j, 0])
        dxn, dgain = _rms_bwd(x_ref[...], g_ref[...], dh)
        dx_ref[...] = r_ref[...] + dxn
        dgain_ref[...] += dgain

    tok = pl.BlockSpec((tm, d), lambda i: (i, 0))
    vec = pl.BlockSpec((1, d), lambda i: (0, 0))
    return pl.pallas_call(
        body,
        name=name,
        grid=(s // tm,),
        in_specs=[tok, tok, vec, pl.BlockSpec((tm, N_CHIPS * n), lambda i: (i, 0)), pl.BlockSpec((N_CHIPS, 1, d, n), lambda i: (0, layer, 0, 0))],
        out_specs=[tok, vec],
        out_shape=[jax.ShapeDtypeStruct((s, d), F32), jax.ShapeDtypeStruct((1, d), F32)],
        compiler_params=_params(1),
    )(x, dx_res, gain, dproj, w_s)


def _loss_grad(y, target, name):
    s, d = y.shape
    tm = TOKEN_TILE

    def body(y_ref, t_ref, dy_ref, l_ref):
        @pl.when(pl.program_id(0) == 0)
        def _():
            l_ref[...] = jnp.zeros_like(l_ref)

        err = y_ref[...] - t_ref[...]
        dy_ref[...] = err / d
        l_ref[...] += jnp.sum(err * err, axis=0, keepdims=True) * (0.5 / d)

    tok = pl.BlockSpec((tm, d), lambda i: (i, 0))
    return pl.pallas_call(
        body,
        name=name,
        grid=(s // tm,),
        in_specs=[tok, tok],
        out_specs=[tok, pl.BlockSpec((1, d), lambda i: (0, 0))],
        out_shape=[jax.ShapeDtypeStruct((s, d), F32), jax.ShapeDtypeStruct((1, d), F32)],
        compiler_params=_params(1),
    )(y, target)


def _wgrad(a, b, a_spec, b_spec, n_blocks, k, n, name):
    n_tiles = a.shape[-2] // min(WGRAD_TILE, a.shape[-2])

    def body(a_ref, b_ref, o_ref):
        @pl.when(pl.program_id(1) == 0)
        def _():
            o_ref[...] = jnp.zeros_like(o_ref)

        av = a_ref[0] if len(a_ref.shape) == 3 else a_ref[...]
        bv = b_ref[0] if len(b_ref.shape) == 3 else b_ref[...]
        o_ref[0] += _dot_tn(av, bv)

    return pl.pallas_call(
        body,
        name=name,
        grid=(n_blocks, n_tiles),
        in_specs=[a_spec, b_spec],
        out_specs=pl.BlockSpec((1, k, n), lambda j, i: (j, 0, 0)),
        out_shape=jax.ShapeDtypeStruct((n_blocks, k, n), F32),
        compiler_params=_params(2),
    )(a, b)


def _mesh_position():
    return lax.axis_index("x"), lax.axis_index("y"), lax.axis_index("c")


def _other_chips(x, y):
    return [(1 - x, y), (x, 1 - y), (1 - x, 1 - y)]


def _half_rows(ref_rows, c):
    half = ref_rows // 2
    return pl.ds(c * half, half)


def _gather_weights(shards):
    n = len(shards)

    def body(*refs):
        ins, outs = refs[:n], refs[n : 2 * n]
        send_sems, recv_sems, pass_send_sems, pass_recv_sems, local_sems = refs[2 * n :]
        x, y, c = _mesh_position()
        me = 2 * x + y
        sibling = (x, y, 1 - c)
        chips = _other_chips(x, y)

        def block(t, chip_index, core):
            return outs[t].at[chip_index, :, _half_rows(ins[t].shape[1], core), :]

        def copy(t, k, chip_index, core, to, sems, src=None):
            dst = block(t, chip_index, core)
            return pltpu.make_async_remote_copy(
                src_ref=dst if src is None else src, dst_ref=dst, send_sem=sems[0].at[t, k], recv_sem=sems[1].at[t, k],
                device_id=to, device_id_type=MESH_ID,
            )

        ici, d2d = (send_sems, recv_sems), (pass_send_sems, pass_recv_sems)
        own = [pltpu.make_async_copy(ins[t], outs[t].at[me], local_sems.at[t]) for t in range(n)]
        for cp in own:
            cp.start()
        started = []
        for t in range(n):
            mine = ins[t].at[:, _half_rows(ins[t].shape[1], c), :]
            for k, (px, py) in enumerate(chips):
                started.append(copy(t, k, me, c, (px, py, c), ici, src=mine))
                started[-1].start()
        for t in range(n):
            for k, (px, py) in enumerate(chips):
                copy(t, k, 2 * px + py, c, sibling, ici).wait_recv()
                started.append(copy(t, k, 2 * px + py, c, sibling, d2d))
                started[-1].start()
        for t in range(n):
            for k, (px, py) in enumerate(chips):
                copy(t, k, 2 * px + py, 1 - c, sibling, d2d).wait_recv()
        for cp in started:
            cp.wait_send()
        for cp in own:
            cp.wait()

    sems = pltpu.SemaphoreType.DMA((n, N_CHIPS - 1))
    return pl.pallas_call(
        body,
        name="gather_weights",
        in_specs=[ANY] * n,
        out_specs=[ANY] * n,
        out_shape=[jax.ShapeDtypeStruct((N_CHIPS,) + w.shape, w.dtype) for w in shards],
        scratch_shapes=[sems, sems, sems, sems, pltpu.SemaphoreType.DMA((n,))],
    )(*shards)


def _swap_halves(grads):
    n = len(grads)

    def body(*refs):
        ins, outs = refs[:n], refs[n : 2 * n]
        send_sems, recv_sems = refs[2 * n :]
        x, y, c = _mesh_position()
        copies = []
        for t in range(n):
            copies.append(pltpu.make_async_remote_copy(
                src_ref=ins[t].at[:, _half_rows(ins[t].shape[1], 1 - c), :], dst_ref=outs[t],
                send_sem=send_sems.at[t], recv_sem=recv_sems.at[t], device_id=(x, y, 1 - c), device_id_type=MESH_ID,
            ))
            copies[-1].start()
        for cp in copies:
            cp.wait()

    sems = pltpu.SemaphoreType.DMA((n,))
    return pl.pallas_call(
        body,
        name="swap_halves",
        in_specs=[ANY] * n,
        out_specs=[ANY] * n,
        out_shape=[jax.ShapeDtypeStruct((g.shape[0], g.shape[1] // 2, g.shape[2]), g.dtype) for g in grads],
        scratch_shapes=[sems, sems],
    )(*grads)


def _scatter_to_chips(parts):
    n = len(parts)

    def body(*refs):
        ins, outs = refs[:n], refs[n : 2 * n]
        send_sems, recv_sems = refs[2 * n :]
        x, y, c = _mesh_position()
        copies = []
        for t in range(n):
            for k, (px, py) in enumerate(_other_chips(x, y)):
                copies.append(pltpu.make_async_remote_copy(
                    src_ref=ins[t].at[2 * px + py], dst_ref=outs[t].at[k],
                    send_sem=send_sems.at[t, k], recv_sem=recv_sems.at[t, k], device_id=(px, py, c), device_id_type=MESH_ID,
                ))
                copies[-1].start()
        for cp in copies:
            cp.wait()

    sems = pltpu.SemaphoreType.DMA((n, N_CHIPS - 1))
    return pl.pallas_call(
        body,
        name="scatter_to_chips",
        in_specs=[ANY] * n,
        out_specs=[ANY] * n,
        out_shape=[jax.ShapeDtypeStruct((N_CHIPS - 1,) + p.shape[1:], p.dtype) for p in parts],
        scratch_shapes=[sems, sems],
    )(*parts)


def _join_halves(shards):
    n = len(shards)

    def body(*refs):
        outs = refs[n : 2 * n]
        send_sems, recv_sems = refs[2 * n :]
        x, y, c = _mesh_position()
        copies = []
        for t in range(n):
            mine = outs[t].at[:, _half_rows(outs[t].shape[1], c), :]
            copies.append(pltpu.make_async_remote_copy(
                src_ref=mine, dst_ref=mine, send_sem=send_sems.at[t], recv_sem=recv_sems.at[t],
                device_id=(x, y, 1 - c), device_id_type=MESH_ID,
            ))
            copies[-1].start()
        for cp in copies:
            cp.wait()

    sems = pltpu.SemaphoreType.DMA((n,))
    return pl.pallas_call(
        body,
        name="join_halves",
        in_specs=[ANY] * n,
        out_specs=[ANY] * n,
        out_shape=[jax.ShapeDtypeStruct(g.shape, g.dtype) for g in shards],
        input_output_aliases={t: t for t in range(n)},
        scratch_shapes=[sems, sems],
    )(*shards)


def _gather_small(pack):
    def body(p_ref, o_ref, send_sems, recv_sems, local_sem):
        x, y, c = _mesh_position()
        own = pltpu.make_async_copy(p_ref, o_ref.at[4 * x + 2 * y + c], local_sem)
        own.start()
        copies = []
        for k in range(1, N_DEV):
            px, py, pc = x ^ (k >> 2), y ^ ((k >> 1) & 1), c ^ (k & 1)
            send = pltpu.make_async_remote_copy(
                src_ref=p_ref, dst_ref=o_ref.at[4 * x + 2 * y + c], send_sem=send_sems.at[k - 1], recv_sem=recv_sems.at[k - 1],
                device_id=(px, py, pc), device_id_type=MESH_ID,
            )
            send.start()
            copies.append((send, 4 * px + 2 * py + pc))
        for send, peer_slot in copies:
            send.wait_send()
        for k in range(1, N_DEV):
            px, py, pc = x ^ (k >> 2), y ^ ((k >> 1) & 1), c ^ (k & 1)
            pltpu.make_async_remote_copy(
                src_ref=p_ref, dst_ref=o_ref.at[4 * px + 2 * py + pc], send_sem=send_sems.at[k - 1], recv_sem=recv_sems.at[k - 1],
                device_id=(px, py, pc), device_id_type=MESH_ID,
            ).wait_recv()
        own.wait()

    sems = pltpu.SemaphoreType.DMA((N_DEV - 1,))
    return pl.pallas_call(
        body,
        name="gather_small",
        in_specs=[VMEM_SPEC],
        out_specs=VMEM_SPEC,
        out_shape=jax.ShapeDtypeStruct((N_DEV,) + pack.shape, pack.dtype),
        scratch_shapes=[sems, sems, pltpu.SemaphoreType.DMA],
    )(pack)


def _row_tile(rows):
    for tile in (256, 128, 64, 32, 16, 8):
        if rows % tile == 0:
            return tile
    return rows


def _add_half(grad, received, half_index, name):
    slots, h, cdim = received.shape
    tile = _row_tile(h)
    per_half = h // tile

    def body(c_ref, g_ref, r_ref, o_ref, ob_ref):
        total = g_ref[...] + r_ref[...]
        o_ref[...] = total
        ob_ref[...] = total.astype(BF16)

    block = pl.BlockSpec((1, tile, cdim), lambda j, i, c: (j, i, 0))
    grid_spec = pltpu.PrefetchScalarGridSpec(
        num_scalar_prefetch=1,
        grid=(slots, per_half),
        in_specs=[pl.BlockSpec((1, tile, cdim), lambda j, i, c: (j, c[0] * per_half + i, 0)), block],
        out_specs=[block, block],
    )
    return pl.pallas_call(
        body, name=name, grid_spec=grid_spec,
        out_shape=[jax.ShapeDtypeStruct(received.shape, F32), jax.ShapeDtypeStruct(received.shape, BF16)],
        compiler_params=_params(2),
    )(half_index, grad, received)


def _add_chips(part, received, chip_index, core_index, layer, n_layers, shard, name):
    _, h, cdim = part.shape
    tile = _row_tile(h)
    per_half = h // tile

    def body(chip_ref, core_ref, p_ref, r_ref, *rest):
        o_ref = rest[-1]
        o_ref[0] = ((p_ref[0] + r_ref[0].astype(F32)) + r_ref[1].astype(F32)) + r_ref[2].astype(F32)

    in_specs = [
        pl.BlockSpec((1, tile, cdim), lambda i, chip, core: (chip[0], i, 0)),
        pl.BlockSpec((N_CHIPS - 1, tile, cdim), lambda i, chip, core: (0, i, 0)),
    ]
    operands = [chip_index, core_index, part, received]
    aliases = {}
    if shard is not None:
        in_specs.append(ANY)
        operands.append(shard)
        aliases = {4: 0}
    grid_spec = pltpu.PrefetchScalarGridSpec(
        num_scalar_prefetch=2,
        grid=(per_half,),
        in_specs=in_specs,
        out_specs=pl.BlockSpec((1, tile, cdim), lambda i, chip, core: (layer, core[0] * per_half + i, 0)),
    )
    return pl.pallas_call(
        body, name=name, grid_spec=grid_spec, out_shape=jax.ShapeDtypeStruct((n_layers, 2 * h, cdim), F32),
        input_output_aliases=aliases, compiler_params=_params(1),
    )(*operands)


def _adamw(w, g, m, v, name):
    rows, cdim = w.shape
    tile = _row_tile(rows)

    def body(w_ref, g_ref, m_ref, v_ref, d_ref, nm_ref, nv_ref):
        gv = g_ref[...]
        nm = ADAM_B1 * m_ref[...] + (1.0 - ADAM_B1) * gv
        nv = ADAM_B2 * v_ref[...] + (1.0 - ADAM_B2) * (gv * gv)
        m_hat = nm / (1.0 - ADAM_B1 ** ADAM_STEP)
        v_hat = nv / (1.0 - ADAM_B2 ** ADAM_STEP)
        d_ref[...] = -ADAM_LR * (m_hat / (jnp.sqrt(v_hat) + ADAM_EPS) + ADAM_WD * w_ref[...])
        nm_ref[...] = nm
        nv_ref[...] = nv

    spec = pl.BlockSpec((tile, cdim), lambda i: (i, 0))
    shape = jax.ShapeDtypeStruct((rows, cdim), F32)
    return pl.pallas_call(
        body, name=name, grid=(rows // tile,), in_specs=[spec] * 4, out_specs=[spec] * 3, out_shape=[shape] * 3,
        compiler_params=_params(1),
    )(w, g, m, v)


SMALL_ROWS, SMALL_COLS = 24, 1024
ROW_NORM_MIX, ROW_NORM_FFN, ROW_LOSS, ROW_Q_NORM, ROW_K_NORM, ROW_CONV = 0, 2, 4, 8, 10, 16


def _sum_small(gathered):
    def body(g_ref, o_ref, heads_ref, lanes_ref):
        total = g_ref[0]
        for dev in range(1, N_DEV):
            total = total + g_ref[dev]
        o_ref[...] = total
        heads = o_ref[8:16, 0:LANES]
        for grp in range(1, ATTN_DIM // LANES):
            heads = heads + o_ref[8:16, grp * LANES : (grp + 1) * LANES]
        heads_ref[...] = heads + pltpu.roll(heads, HEAD_DIM, 1)
        lanes_ref[...] = jnp.broadcast_to(jnp.sum(o_ref[0:8, :], axis=-1, keepdims=True), (8, LANES))

    return pl.pallas_call(
        body,
        name="sum_small",
        in_specs=[VMEM_SPEC],
        out_specs=[VMEM_SPEC] * 3,
        out_shape=[jax.ShapeDtypeStruct((SMALL_ROWS, SMALL_COLS), F32), jax.ShapeDtypeStruct((8, LANES), F32), jax.ShapeDtypeStruct((8, LANES), F32)],
    )(gathered)


def _pad_rows(a, rows):
    return jnp.pad(a, ((0, rows - a.shape[0]), (0, 0)))


def _pad_to(a, rows, cols):
    return jnp.pad(a, ((0, rows - a.shape[0]), (0, cols - a.shape[1])))


def _local_step(x, target, norm_mix, q_norm, k_norm, norm_ffn, conv_full, win_s, wout_s, wg_s, wu_s, wd_s):
    n_layers = norm_mix.shape[0]
    s, d = x.shape
    tw = min(WGRAD_TILE, s)
    n_in = win_s.shape[3]
    f = wg_s.shape[3]
    saved = []
    for l in range(n_layers):
        q_gain = jnp.tile(q_norm[l][None, :], (1, 2))
        k_gain = jnp.tile(k_norm[l][None, :], (1, 2))
        h1, proj = _norm_matmul(x, norm_mix[l][None, :], win_s, l, f"in_proj_{l}")
        qn, kn, vb = _qkv_prep(proj, q_gain, k_gain, f"qkv_prep_{l}")
        attn = _attn_fwd(qn, kn, vb, f"attn_fwd_{l}")
        conv = _conv_fwd(proj, conv_full[l], f"conv_fwd_{l}")
        x_mid = _out_proj(x, attn, conv, wout_s, l, f"out_proj_{l}")
        x_out, gate, up = _ffn_fwd(x_mid, norm_ffn[l][None, :], wg_s, wu_s, wd_s, l, f"ffn_fwd_{l}")
        saved.append(dict(x=x, h1=h1, proj=proj, qn=qn, kn=kn, vb=vb, attn=attn, conv=conv, x_mid=x_mid, q_gain=q_gain, k_gain=k_gain, gate=gate, up=up))
        x = x_out

    dy, loss_lanes = _loss_grad(x, target, "loss_grad")
    grads = [None] * n_layers
    for l in reversed(range(n_layers)):
        sv = saved[l]
        dx_mid, d_norm_ffn, h2, dyb, dgate, dup, act = _ffn_bwd(sv["x_mid"], dy, norm_ffn[l][None, :], sv["gate"], sv["up"], wg_s, wu_s, wd_s, l, f"ffn_bwd_{l}")
        tok2 = pl.BlockSpec((tw, d), lambda j, i: (i, 0))
        hid = pl.BlockSpec((1, tw, f), lambda j, i: (j, i, 0))
        d_wg = _wgrad(h2, dgate, tok2, hid, N_CHIPS, d, f, f"wgrad_gate_{l}")
        d_wu = _wgrad(h2, dup, tok2, hid, N_CHIPS, d, f, f"wgrad_up_{l}")
        d_wd = _wgrad(act, dyb, hid, tok2, N_CHIPS, f, d, f"wgrad_down_{l}")
        d_attn, d_conv, dxb = _out_proj_bwd(dx_mid, wout_s, l, f"out_proj_bwd_{l}")
        rows_out = wout_s.shape[2]
        mix_spec_a = pl.BlockSpec((tw, rows_out), lambda j, i: (i, j))
        d_wout_a = _wgrad(sv["attn"], dxb, mix_spec_a, tok2, ATTN_DIM // rows_out, rows_out, d, f"wgrad_out_attn_{l}")
        d_wout_c = _wgrad(sv["conv"], dxb, mix_spec_a, tok2, CONV_DIM // rows_out, rows_out, d, f"wgrad_out_conv_{l}")
        d_wout = jnp.concatenate([d_wout_a, d_wout_c], axis=0)
        dq, dk, dv = _attn_bwd(sv["qn"], sv["kn"], sv["vb"], d_attn, f"attn_bwd_{l}")
        dproj, d_conv_w = _conv_bwd(sv["proj"], conv_full[l], d_conv, f"conv_bwd_{l}")
        dproj, d_qg, d_kg = _qkv_prep_bwd(sv["proj"], sv["q_gain"], sv["k_gain"], dq, dk, dv, dproj, f"qkv_prep_bwd_{l}")
        d_win = _wgrad(sv["h1"], dproj, tok2, pl.BlockSpec((tw, n_in), lambda j, i: (i, j)), N_CHIPS, d, n_in, f"wgrad_in_{l}")
        dy, d_norm_mix = _in_proj_bwd(sv["x"], dx_mid, norm_mix[l][None, :], dproj, win_s, l, f"in_proj_bwd_{l}")
        grads[l] = dict(norm_mix=d_norm_mix, norm_ffn=d_norm_ffn, q_norm=d_qg, k_norm=d_kg, conv_w=d_conv_w,
                        w_in=d_win, w_out=d_wout, w_gate=d_wg, w_up=d_wu, w_down=d_wd)
    return loss_lanes, dy, grads


BIG = ("w_in", "w_out", "w_gate", "w_up", "w_down")


def kernel(x, norm_mix, w_in, q_norm, k_norm, conv_w, w_out, norm_ffn, w_gate, w_up, w_down, loss_target, m_norm_mix, m_w_in, m_q_norm, m_k_norm, m_conv_w, m_w_out, m_norm_ffn, m_w_gate, m_w_up, m_w_down, v_norm_mix, v_w_in, v_q_norm, v_k_norm, v_conv_w, v_w_out, v_norm_ffn, v_w_gate, v_w_up, v_w_down):
    n_layers = norm_mix.shape[0]
    weights = dict(w_in=w_in, w_out=w_out, w_gate=w_gate, w_up=w_up, w_down=w_down)
    moments_m = dict(w_in=m_w_in, w_out=m_w_out, w_gate=m_w_gate, w_up=m_w_up, w_down=m_w_down)
    moments_v = dict(w_in=v_w_in, w_out=v_w_out, w_gate=v_w_gate, w_up=v_w_up, w_down=v_w_down)
    cx, cy, cc = _mesh_position()
    chip_index = (2 * cx + cy).astype(jnp.int32).reshape(1)
    core_index = cc.astype(jnp.int32).reshape(1)

    conv_pad = jnp.pad(conv_w, ((0, 0), (0, 16 - conv_w.shape[1]), (0, 0)))
    gathered = _gather_weights([weights[k].astype(BF16) for k in BIG] + [conv_pad])
    win_s, wout_s, wg_s, wu_s, wd_s, conv_s = gathered
    conv_full = jnp.transpose(conv_s[:, :, 0:8], (1, 2, 0, 3)).reshape(n_layers, 8, N_CHIPS * conv_w.shape[2])

    loss_lanes, grad_x, grads = _local_step(
        x[0], loss_target[0], norm_mix, q_norm, k_norm, norm_ffn, conv_full, win_s, wout_s, wg_s, wu_s, wd_s)

    flat = [grads[l][k] for l in range(n_layers) for k in BIG]
    names = [f"{k}_{l}" for l in range(n_layers) for k in BIG]
    received = _swap_halves(flat)
    parts = [_add_half(g, r, core_index, f"add_half_{nm}") for g, r, nm in zip(flat, received, names)]
    from_chips = _scatter_to_chips([p_bf16 for _, p_bf16 in parts])
    shards = [None] * len(BIG)
    for n, ((p_f32, _), r, nm) in enumerate(zip(parts, from_chips, names)):
        layer, tensor = divmod(n, len(BIG))
        shards[tensor] = _add_chips(p_f32, r, chip_index, core_index, layer, n_layers, shards[tensor], f"add_chips_{nm}")
    big_grads = dict(zip(BIG, _join_halves(shards)))

    def lanes(a):
        return _pad_to(a, a.shape[0], SMALL_COLS)

    def tile_of(*groups):
        return _pad_rows(jnp.concatenate([lanes(jnp.concatenate(g, axis=0)) for g in groups], axis=0), 8)

    layers = range(n_layers)
    pack = jnp.concatenate([
        tile_of([grads[l]["norm_mix"] for l in layers], [grads[l]["norm_ffn"] for l in layers], [loss_lanes]),
        tile_of([grads[l]["q_norm"] for l in layers], [grads[l]["k_norm"] for l in layers]),
        tile_of([grads[l]["conv_w"][0:3] for l in layers]),
    ], axis=0)
    small, small_heads, small_lanes = _sum_small(_gather_small(pack))
    loss = small_lanes[ROW_LOSS, 0]
    d_model = norm_mix.shape[1]
    conv_cols = conv_w.shape[2]
    conv_all = small[ROW_CONV : ROW_CONV + 3 * n_layers, 0:CONV_DIM].reshape(n_layers, 3, CONV_DIM)
    small_grads = dict(
        norm_mix=small[ROW_NORM_MIX : ROW_NORM_MIX + n_layers, 0:d_model],
        norm_ffn=small[ROW_NORM_FFN : ROW_NORM_FFN + n_layers, 0:d_model],
        q_norm=small_heads[ROW_Q_NORM - 8 : ROW_Q_NORM - 8 + n_layers, 0:HEAD_DIM],
        k_norm=small_heads[ROW_K_NORM - 8 : ROW_K_NORM - 8 + n_layers, 0:HEAD_DIM],
        conv_w=lax.dynamic_slice_in_dim(conv_all, (2 * cx + cy) * conv_cols, conv_cols, axis=2),
    )

    out_grad, out_delta, out_m, out_v = {}, {}, {}, {}
    for k in BIG:
        shape = weights[k].shape
        view = (shape[0] * shape[1], shape[2])
        g = big_grads[k]
        delta, new_m, new_v = _adamw(weights[k].reshape(view), g.reshape(view), moments_m[k].reshape(view), moments_v[k].reshape(view), f"adamw_{k}")
        out_grad[k], out_delta[k], out_m[k], out_v[k] = g, delta.reshape(shape), new_m.reshape(shape), new_v.reshape(shape)

    small_w = dict(norm_mix=norm_mix, norm_ffn=norm_ffn, q_norm=q_norm, k_norm=k_norm, conv_w=conv_w)
    small_m = dict(norm_mix=m_norm_mix, norm_ffn=m_norm_ffn, q_norm=m_q_norm, k_norm=m_k_norm, conv_w=m_conv_w)
    small_v = dict(norm_mix=v_norm_mix, norm_ffn=v_norm_ffn, q_norm=v_q_norm, k_norm=v_k_norm, conv_w=v_conv_w)
    order = ("norm_mix", "norm_ffn", "q_norm", "k_norm", "conv_w")

    def packed(tree):
        parts2 = [_pad_to(tree[k].reshape(-1, tree[k].shape[-1]), tree[k].reshape(-1, tree[k].shape[-1]).shape[0], SMALL_COLS) for k in order]
        return _pad_rows(jnp.concatenate(parts2, axis=0), SMALL_ROWS)

    delta_p, m_p, v_p = _adamw(packed(small_w), packed(small_grads), packed(small_m), packed(small_v), "adamw_small")
    row = 0
    for k in order:
        shape = small_w[k].shape
        n_rows = 1
        for dim in shape[:-1]:
            n_rows *= dim
        cut = (slice(row, row + n_rows), slice(0, shape[-1]))
        out_grad[k] = small_grads[k]
        out_delta[k], out_m[k], out_v[k] = delta_p[cut].reshape(shape), m_p[cut].reshape(shape), v_p[cut].reshape(shape)
        row += n_rows

    names_out = ("norm_mix", "w_in", "q_norm", "k_norm", "conv_w", "w_out", "norm_ffn", "w_gate", "w_up", "w_down")
    return (loss, grad_x[None], *[out_grad[k] for k in names_out], *[out_delta[k] for k in names_out],
            *[out_m[k] for k in names_out], *[out_v[k] for k in names_out])
```

```python
import functools

import jax
import jax.numpy as jnp
from jax import lax
from jax.experimental import pallas as pl
from jax.experimental.pallas import tpu as pltpu

F32 = jnp.float32
BF16 = jnp.bfloat16

EPS = 1e-6
HEAD_DIM = 64
LANES = 128
ATTN_DIM = 512
CONV_DIM = 512
N_CHIPS = 4
N_DEV = 8
Q_SCALE = HEAD_DIM ** -0.5
ATTN_Q_TILE = 256
ATTN_TILE = 256
TOKEN_TILE = 512
WGRAD_TILE = 2048
VMEM_LIMIT = 56 * 1024 * 1024

ADAM_LR = 0.001
ADAM_B1 = 0.9
ADAM_B2 = 0.999
ADAM_EPS = 1e-08
ADAM_WD = 0.01
ADAM_STEP = 10

MESH_ID = pl.DeviceIdType.MESH
ANY = pl.BlockSpec(memory_space=pl.ANY)
VMEM_SPEC = pl.BlockSpec(memory_space=pltpu.VMEM)


def _params(n_axes):
    return pltpu.CompilerParams(dimension_semantics=("arbitrary",) * n_axes, vmem_limit_bytes=VMEM_LIMIT)


def _dot(a, b):
    return jnp.dot(a, b, preferred_element_type=F32)


def _dot_nt(a, b):
    return lax.dot_general(a, b, (((1,), (1,)), ((), ())), preferred_element_type=F32)


def _dot_tn(a, b):
    return lax.dot_general(a, b, (((0,), (0,)), ((), ())), preferred_element_type=F32)


SCORE_MAX = 80.0
UNDERFLOW_EXIT = 90.0


def _scores(q, k):
    return jnp.minimum(_dot_nt(q, k), SCORE_MAX)


def _softplus(z):
    return jnp.log(1.0 + jnp.exp(z))


def _norm_matmul(x, gain, w_s, layer, name):
    s, d = x.shape
    n_blocks, _, _, n = w_s.shape
    tm = TOKEN_TILE

    def body(x_ref, g_ref, w_ref, h_ref, o_ref):
        xv = x_ref[...]
        r = lax.rsqrt(jnp.mean(xv * xv, axis=-1, keepdims=True) + EPS)
        h = (xv * r * g_ref[...]).astype(BF16)
        h_ref[...] = h
        for j in range(n_blocks):
            o_ref[:, j * n : (j + 1) * n] = _dot(h, w_ref[j, 0])

    return pl.pallas_call(
        body,
        name=name,
        grid=(s // tm,),
        in_specs=[
            pl.BlockSpec((tm, d), lambda i: (i, 0)),
            pl.BlockSpec((1, d), lambda i: (0, 0)),
            pl.BlockSpec((n_blocks, 1, d, n), lambda i: (0, layer, 0, 0)),
        ],
        out_specs=[pl.BlockSpec((tm, d), lambda i: (i, 0)), pl.BlockSpec((tm, n_blocks * n), lambda i: (i, 0))],
        out_shape=[jax.ShapeDtypeStruct((s, d), BF16), jax.ShapeDtypeStruct((s, n_blocks * n), F32)],
        compiler_params=_params(1),
    )(x, gain, w_s)


def _head_norm(xv, gain, low):
    sq = xv * xv
    s_low = jnp.sum(jnp.where(low, sq, 0.0), axis=-1, keepdims=True)
    s_high = jnp.sum(jnp.where(low, 0.0, sq), axis=-1, keepdims=True)
    r = jnp.where(low, lax.rsqrt(s_low / HEAD_DIM + EPS), lax.rsqrt(s_high / HEAD_DIM + EPS))
    return xv * r * gain, r


def _qkv_prep(proj, q_gain, k_gain, name):
    s = proj.shape[0]
    tm = TOKEN_TILE

    def body(p_ref, qg_ref, kg_ref, q_ref, k_ref, v_ref):
        low = lax.broadcasted_iota(jnp.int32, (tm, LANES), 1) < HEAD_DIM
        for g in range(ATTN_DIM // LANES):
            cq = slice(LANES * g, LANES * (g + 1))
            ck = slice(ATTN_DIM + LANES * g, ATTN_DIM + LANES * (g + 1))
            cv = slice(2 * ATTN_DIM + LANES * g, 2 * ATTN_DIM + LANES * (g + 1))
            qn, _ = _head_norm(p_ref[:, cq], qg_ref[...], low)
            kn, _ = _head_norm(p_ref[:, ck], kg_ref[...], low)
            q_ref[:, cq] = (qn * Q_SCALE).astype(BF16)
            k_ref[:, cq] = kn.astype(BF16)
            v_ref[:, cq] = p_ref[:, cv].astype(BF16)

    out = jax.ShapeDtypeStruct((s, ATTN_DIM), BF16)
    return pl.pallas_call(
        body,
        name=name,
        grid=(s // tm,),
        in_specs=[
            pl.BlockSpec((tm, 3 * ATTN_DIM), lambda i: (i, 0)),
            pl.BlockSpec((1, LANES), lambda i: (0, 0)),
            pl.BlockSpec((1, LANES), lambda i: (0, 0)),
        ],
        out_specs=[pl.BlockSpec((tm, ATTN_DIM), lambda i: (i, 0))] * 3,
        out_shape=[out, out, out],
        compiler_params=_params(1),
    )(proj, q_gain, k_gain)


def _qkv_prep_bwd(proj, q_gain, k_gain, dq, dk, dv, dproj, name):
    s = proj.shape[0]
    tm = TOKEN_TILE

    def norm_bwd(xv, gain, dy, low):
        _, r = _head_norm(xv, gain, low)
        xhat = xv * r
        dxhat = dy * gain
        prod = dxhat * xhat
        m_low = jnp.sum(jnp.where(low, prod, 0.0), axis=-1, keepdims=True)
        m_high = jnp.sum(jnp.where(low, 0.0, prod), axis=-1, keepdims=True)
        mean = jnp.where(low, m_low, m_high) / HEAD_DIM
        return r * (dxhat - xhat * mean), jnp.sum(dy * xhat, axis=0, keepdims=True)

    def body(p_ref, qg_ref, kg_ref, dq_ref, dk_ref, dv_ref, dproj_ref, dp_ref, dqg_ref, dkg_ref):
        @pl.when(pl.program_id(0) == 0)
        def _():
            dqg_ref[...] = jnp.zeros_like(dqg_ref)
            dkg_ref[...] = jnp.zeros_like(dkg_ref)

        low = lax.broadcasted_iota(jnp.int32, (tm, LANES), 1) < HEAD_DIM
        for g in range(ATTN_DIM // LANES):
            cq = slice(LANES * g, LANES * (g + 1))
            ck = slice(ATTN_DIM + LANES * g, ATTN_DIM + LANES * (g + 1))
            cv = slice(2 * ATTN_DIM + LANES * g, 2 * ATTN_DIM + LANES * (g + 1))
            dxq, dgq = norm_bwd(p_ref[:, cq], qg_ref[...], dq_ref[:, cq] * Q_SCALE, low)
            dxk, dgk = norm_bwd(p_ref[:, ck], kg_ref[...], dk_ref[:, cq], low)
            dp_ref[:, cq] = dxq.astype(BF16)
            dp_ref[:, ck] = dxk.astype(BF16)
            dp_ref[:, cv] = dv_ref[:, cq].astype(BF16)
            dqg_ref[:, cq] += dgq
            dkg_ref[:, cq] += dgk

    grad_spec = pl.BlockSpec((tm, ATTN_DIM), lambda i: (i, 0))
    gain_spec = pl.BlockSpec((1, LANES), lambda i: (0, 0))
    sum_spec = pl.BlockSpec((1, ATTN_DIM), lambda i: (0, 0))
    return pl.pallas_call(
        body,
        name=name,
        grid=(s // tm,),
        in_specs=[pl.BlockSpec((tm, 3 * ATTN_DIM), lambda i: (i, 0)), gain_spec, gain_spec, grad_spec, grad_spec, grad_spec, ANY],
        out_specs=[pl.BlockSpec((tm, 3 * ATTN_DIM), lambda i: (i, 0)), sum_spec, sum_spec],
        out_shape=[
            jax.ShapeDtypeStruct(dproj.shape, BF16),
            jax.ShapeDtypeStruct((1, ATTN_DIM), F32),
            jax.ShapeDtypeStruct((1, ATTN_DIM), F32),
        ],
        input_output_aliases={6: 0},
        compiler_params=_params(1),
    )(proj, q_gain, k_gain, dq, dk, dv, dproj)


def _attn_tile_consts(t):
    row = lax.broadcasted_iota(jnp.int32, (t, t), 0)
    col = lax.broadcasted_iota(jnp.int32, (t, t), 1)
    return row, col


def _triangle_sum(v, triangle):
    return _dot(v.astype(BF16), triangle)


def _attn_fwd(qn, kn, vb, name):
    s = qn.shape[0]
    t = min(ATTN_TILE, s)
    tq = min(ATTN_Q_TILE, t)
    per_key_tile = t // tq

    def body(q_ref, k_ref, v_ref, o_ref):
        i = pl.program_id(1) // per_key_tile
        low = lax.broadcasted_iota(jnp.int32, (tq, LANES), 1) < HEAD_DIM
        row, col = _attn_tile_consts(t)
        suffix = (row > col).astype(BF16)
        first_row = (pl.program_id(1) % per_key_tile) * tq
        causal = lax.broadcasted_iota(jnp.int32, (tq, t), 1) < lax.broadcasted_iota(jnp.int32, (tq, t), 0) + first_row
        q = q_ref[...]
        zero_q = jnp.zeros_like(q)
        qh = (jnp.where(low, q, zero_q), jnp.where(low, zero_q, q))

        def step(kbs, carry, diagonal_first=False):
            chains = [(head, m) for head in range(2) for m in range(len(kbs))]
            masked = [diagonal_first and m == 0 for _, m in chains]
            ks = [k_ref[pl.ds(pl.multiple_of(kb * t, t), t), :] for kb in kbs]
            vs = [v_ref[pl.ds(pl.multiple_of(kb * t, t), t), :] for kb in kbs]
            z = [_scores(qh[head], ks[kb]) for head, kb in chains]
            sp = [_softplus(zc) for zc in z]
            sp = [jnp.where(causal, s_, 0.0) if mk else s_ for s_, mk in zip(sp, masked)]
            inside = [_triangle_sum(s_, suffix) for s_ in sp]
            after = [carry[head][1] for head in range(2)]
            log_a = []
            for n, (head, kb) in enumerate(chains):
                log_a.append(z[n] - sp[n] - inside[n] - after[head])
                after[head] = after[head] + jnp.sum(sp[n], axis=-1, keepdims=True)
            a = [jnp.exp(l_) for l_ in log_a]
            a = [jnp.where(causal, a_, 0.0) if mk else a_ for a_, mk in zip(a, masked)]
            acc = [carry[head][0] for head in range(2)]
            for n, (head, kb) in enumerate(chains):
                acc[head] = acc[head] + _dot(a[n].astype(BF16), vs[kb])
            return tuple((acc[head], after[head]) for head in range(2))

        def live(c):
            return jnp.minimum(jnp.min(c[0][1]), jnp.min(c[1][1])) < UNDERFLOW_EXIT

        zero = (jnp.zeros((tq, LANES), F32), jnp.zeros((tq, 1), F32))
        carry = lax.cond(i >= 1, lambda c: step((i, i - 1), c, True), lambda c: step((i,), c, True), (zero, zero))
        rest = jnp.maximum(i - 1, 0)
        carry = lax.cond((rest % 2 == 1) & live(carry), lambda c: step((i - 2,), c), lambda c: c, carry)
        pairs = rest // 2
        _, carry = lax.while_loop(
            lambda st: (st[0] < pairs) & live(st[1]),
            lambda st: (st[0] + 1, step((2 * (pairs - st[0]) - 1, 2 * (pairs - st[0]) - 2), st[1])),
            (jnp.int32(0), carry))
        o_ref[...] = jnp.where(low, carry[0][0], carry[1][0]).astype(BF16)

    return pl.pallas_call(
        body,
        name=name,
        grid=(ATTN_DIM // LANES, s // tq),
        in_specs=[
            pl.BlockSpec((tq, LANES), lambda p, i: (i, p)),
            pl.BlockSpec((s, LANES), lambda p, i: (0, p)),
            pl.BlockSpec((s, LANES), lambda p, i: (0, p)),
        ],
        out_specs=pl.BlockSpec((tq, LANES), lambda p, i: (i, p)),
        out_shape=jax.ShapeDtypeStruct((s, ATTN_DIM), BF16),
        compiler_params=_params(2),
    )(qn, kn, vb)


def _attn_bwd(qn, kn, vb, do, name):
    s = qn.shape[0]
    t = min(ATTN_TILE, s)
    nq = s // t

    def body(q_ref, k_ref, v_ref, do_ref, dq_ref, dk_ref, dv_ref, a_s, sg_s):
        i = pl.program_id(1)

        @pl.when(i == 0)
        def _():
            dk_ref[...] = jnp.zeros_like(dk_ref)
            dv_ref[...] = jnp.zeros_like(dv_ref)

        low = lax.broadcasted_iota(jnp.int32, (t, LANES), 1) < HEAD_DIM
        row, col = _attn_tile_consts(t)
        suffix = (row > col).astype(BF16)
        prefix = (row < col).astype(BF16)
        causal = col < row
        q = q_ref[...]
        dob = do_ref[...]
        zero_q = jnp.zeros_like(q)
        heads = []
        for head in range(2):
            if head == 0:
                qh, doh = jnp.where(low, q, zero_q), jnp.where(low, dob, zero_q)
            else:
                qh, doh = jnp.where(low, zero_q, q), jnp.where(low, zero_q, dob)

            def rows_of(kb):
                return pl.ds(pl.multiple_of(kb * t, t), t)

            def pass1(kbs, after, diagonal_first=False):
                z = [_scores(qh, k_ref[rows_of(kb), :]) for kb in kbs]
                sp = [_softplus(z_) for z_ in z]
                if diagonal_first:
                    sp[0] = jnp.where(causal, sp[0], 0.0)
                inside = [_triangle_sum(s_, suffix) for s_ in sp]
                for n, kb in enumerate(kbs):
                    log_sg = z[n] - sp[n]
                    a = jnp.exp(log_sg - inside[n] - after)
                    sg = jnp.exp(log_sg)
                    if diagonal_first and n == 0:
                        a = jnp.where(causal, a, 0.0)
                        sg = jnp.where(causal, sg, 0.0)
                    a_s[kb] = a
                    sg_s[kb] = sg
                    after = after + jnp.sum(sp[n], axis=-1, keepdims=True)
                return after

            def live(after):
                return jnp.min(after) < UNDERFLOW_EXIT

            after = jnp.zeros((t, 1), F32)
            after = lax.cond(i >= 1, lambda c: pass1((i, i - 1), c, True), lambda c: pass1((i,), c, True), after)
            rest = jnp.maximum(i - 1, 0)
            take_single = (rest % 2 == 1) & live(after)
            after = lax.cond(take_single, lambda c: pass1((i - 2,), c), lambda c: c, after)
            pairs = rest // 2
            pairs_done, _ = lax.while_loop(
                lambda st: (st[0] < pairs) & live(st[1]),
                lambda st: (st[0] + 1, pass1((2 * (pairs - st[0]) - 1, 2 * (pairs - st[0]) - 2), st[1])),
                (jnp.int32(0), after))
            walked = jnp.minimum(i, 1) + 1 + take_single.astype(jnp.int32) + 2 * pairs_done
            first = i - walked + 1

            def pass2(kbs, carry):
                dq, before = carry
                ks = [k_ref[rows_of(kb), :] for kb in kbs]
                a = [a_s[kb] for kb in kbs]
                g = [a_ * _dot_nt(doh, v_ref[rows_of(kb), :]) for a_, kb in zip(a, kbs)]
                for n, kb in enumerate(kbs):
                    dv_ref[rows_of(kb), :] += _dot_tn(a[n].astype(BF16), doh)
                inside = [_triangle_sum(g_, prefix) for g_ in g]
                dz = []
                for n, kb in enumerate(kbs):
                    sg = sg_s[kb]
                    dz.append((g[n] - sg * (g[n] + inside[n] + before)).astype(BF16))
                    before = before + jnp.sum(g[n], axis=-1, keepdims=True)
                for n, kb in enumerate(kbs):
                    dk_ref[rows_of(kb), :] += _dot_tn(dz[n], qh)
                for n in range(len(kbs)):
                    dq = dq + _dot(dz[n], ks[n])
                return dq, before

            carry = (jnp.zeros((t, LANES), F32), jnp.zeros((t, 1), F32))
            carry = lax.fori_loop(0, walked // 2, lambda n, c: pass2((first + 2 * n, first + 2 * n + 1), c), carry)
            carry = lax.cond(walked % 2 == 1, lambda c: pass2((i,), c), lambda c: c, carry)
            heads.append(carry[0])
        dq_ref[...] = jnp.where(low, heads[0], heads[1])

    q_spec = pl.BlockSpec((t, LANES), lambda p, i: (i, p))
    kv_spec = pl.BlockSpec((s, LANES), lambda p, i: (0, p))
    return pl.pallas_call(
        body,
        name=name,
        grid=(ATTN_DIM // LANES, nq),
        in_specs=[q_spec, kv_spec, kv_spec, q_spec],
        out_specs=[q_spec, kv_spec, kv_spec],
        out_shape=[jax.ShapeDtypeStruct((s, ATTN_DIM), F32)] * 3,
        scratch_shapes=[pltpu.VMEM((nq, t, t), F32), pltpu.VMEM((nq, t, t), F32)],
        compiler_params=_params(2),
    )(qn, kn, vb, do)


CB_BLOCK, CC_BLOCK, CU_BLOCK = 3, 4, 5


def _shift_down(h, prev_rows, n):
    row = lax.broadcasted_iota(jnp.int32, h.shape, 0)
    out = pltpu.roll(h, n, 0)
    for r in range(n):
        out = jnp.where(row == r, prev_rows[len(prev_rows) - n + r], out)
    return out


def _shift_up(h, next_rows, n):
    tm = h.shape[0]
    row = lax.broadcasted_iota(jnp.int32, h.shape, 0)
    out = pltpu.roll(h, tm - n, 0)
    for r in range(n):
        out = jnp.where(row == tm - n + r, next_rows[r], out)
    return out


def _conv_fwd(proj, conv_w, name):
    s = proj.shape[0]
    tm = TOKEN_TILE
    nb = tm // 8

    def body(cb_ref, cc_ref, cu_ref, pc_ref, pu_ref, w_ref, o_ref):
        first = pl.program_id(0) == 0
        h = cc_ref[...] * cu_ref[...]
        prev = [jnp.where(first, 0.0, pc_ref[r : r + 1, :] * pu_ref[r : r + 1, :]) for r in (6, 7)]
        y = w_ref[0:1, :] * _shift_down(h, prev, 2) + w_ref[1:2, :] * _shift_down(h, prev, 1) + w_ref[2:3, :] * h
        o_ref[...] = (cb_ref[...] * y).astype(BF16)

    def col(block):
        return pl.BlockSpec((tm, CONV_DIM), lambda i: (i, block))

    def halo(block):
        return pl.BlockSpec((8, CONV_DIM), lambda i: (jnp.maximum(i * nb - 1, 0), block))

    return pl.pallas_call(
        body,
        name=name,
        grid=(s // tm,),
        in_specs=[col(CB_BLOCK), col(CC_BLOCK), col(CU_BLOCK), halo(CC_BLOCK), halo(CU_BLOCK), pl.BlockSpec((8, CONV_DIM), lambda i: (0, 0))],
        out_specs=pl.BlockSpec((tm, CONV_DIM), lambda i: (i, 0)),
        out_shape=jax.ShapeDtypeStruct((s, CONV_DIM), BF16),
        compiler_params=_params(1),
    )(proj, proj, proj, proj, proj, conv_w)


def _conv_bwd(proj, conv_w, dconv, name):
    s = proj.shape[0]
    tm = TOKEN_TILE
    nb = tm // 8
    n_tiles = s // tm

    def body(cb_ref, cc_ref, cu_ref, dy_ref, pc_ref, pu_ref, nb_ref, ndy_ref, w_ref, dp_ref, dw_ref):
        i = pl.program_id(0)

        @pl.when(i == 0)
        def _():
            dw_ref[...] = jnp.zeros_like(dw_ref)

        first = i == 0
        last = i == n_tiles - 1
        cc, cu, cb, dy = cc_ref[...], cu_ref[...], cb_ref[...], dy_ref[...]
        h = cc * cu
        prev = [jnp.where(first, 0.0, pc_ref[r : r + 1, :] * pu_ref[r : r + 1, :]) for r in (6, 7)]
        h1 = _shift_down(h, prev, 1)
        h2 = _shift_down(h, prev, 2)
        y = w_ref[0:1, :] * h2 + w_ref[1:2, :] * h1 + w_ref[2:3, :] * h
        dyb = dy * cb
        nxt = [jnp.where(last, 0.0, ndy_ref[r : r + 1, :] * nb_ref[r : r + 1, :]) for r in (0, 1)]
        dh = w_ref[2:3, :] * dyb + w_ref[1:2, :] * _shift_up(dyb, nxt, 1) + w_ref[0:1, :] * _shift_up(dyb, nxt, 2)
        dp_ref[:, 0:CONV_DIM] = (dy * y).astype(BF16)
        dp_ref[:, CONV_DIM : 2 * CONV_DIM] = (dh * cu).astype(BF16)
        dp_ref[:, 2 * CONV_DIM : 3 * CONV_DIM] = (dh * cc).astype(BF16)
        dw_ref[0:1, :] += jnp.sum(dyb * h2, axis=0, keepdims=True)
        dw_ref[1:2, :] += jnp.sum(dyb * h1, axis=0, keepdims=True)
        dw_ref[2:3, :] += jnp.sum(dyb * h, axis=0, keepdims=True)

    def col(block):
        return pl.BlockSpec((tm, CONV_DIM), lambda i: (i, block))

    def halo_prev(block):
        return pl.BlockSpec((8, CONV_DIM), lambda i: (jnp.maximum(i * nb - 1, 0), block))

    def halo_next(block):
        return pl.BlockSpec((8, CONV_DIM), lambda i: (jnp.minimum((i + 1) * nb, s // 8 - 1), block))

    return pl.pallas_call(
        body,
        name=name,
        grid=(n_tiles,),
        in_specs=[
            col(CB_BLOCK), col(CC_BLOCK), col(CU_BLOCK), col(0),
            halo_prev(CC_BLOCK), halo_prev(CU_BLOCK), halo_next(CB_BLOCK), halo_next(0),
            pl.BlockSpec((8, CONV_DIM), lambda i: (0, 0)),
        ],
        out_specs=[pl.BlockSpec((tm, 3 * CONV_DIM), lambda i: (i, 1)), pl.BlockSpec((8, CONV_DIM), lambda i: (0, 0))],
        out_shape=[jax.ShapeDtypeStruct((s, 3 * ATTN_DIM + 3 * CONV_DIM), BF16), jax.ShapeDtypeStruct((8, CONV_DIM), F32)],
        compiler_params=_params(1),
    )(proj, proj, proj, dconv, proj, proj, proj, dconv, conv_w)


def _out_proj(x, attn, conv, w_s, layer, name):
    s, d = x.shape
    tm = TOKEN_TILE
    rows = w_s.shape[2]

    def body(x_ref, a_ref, c_ref, w_ref, o_ref):
        acc = x_ref[...]
        for j in range(N_CHIPS):
            src = a_ref if j < 2 else c_ref
            cols = slice((j % 2) * rows, (j % 2 + 1) * rows)
            acc = acc + _dot(src[:, cols], w_ref[j, 0])
        o_ref[...] = acc

    return pl.pallas_call(
        body,
        name=name,
        grid=(s // tm,),
        in_specs=[
            pl.BlockSpec((tm, d), lambda i: (i, 0)),
            pl.BlockSpec((tm, ATTN_DIM), lambda i: (i, 0)),
            pl.BlockSpec((tm, CONV_DIM), lambda i: (i, 0)),
            pl.BlockSpec((N_CHIPS, 1, rows, d), lambda i: (0, layer, 0, 0)),
        ],
        out_specs=pl.BlockSpec((tm, d), lambda i: (i, 0)),
        out_shape=jax.ShapeDtypeStruct((s, d), F32),
        compiler_params=_params(1),
    )(x, attn, conv, w_s)


def _out_proj_bwd(dx, w_s, layer, name):
    s, d = dx.shape
    tm = TOKEN_TILE
    rows = w_s.shape[2]

    def body(dx_ref, w_ref, da_ref, dc_ref, dxb_ref):
        dxb = dx_ref[...].astype(BF16)
        dxb_ref[...] = dxb
        for j in range(N_CHIPS):
            cols = slice((j % 2) * rows, (j % 2 + 1) * rows)
            part = _dot_nt(dxb, w_ref[j, 0])
            if j < 2:
                da_ref[:, cols] = part.astype(BF16)
            else:
                dc_ref[:, cols] = part

    return pl.pallas_call(
        body,
        name=name,
        grid=(s // tm,),
        in_specs=[pl.BlockSpec((tm, d), lambda i: (i, 0)), pl.BlockSpec((N_CHIPS, 1, rows, d), lambda i: (0, layer, 0, 0))],
        out_specs=[
            pl.BlockSpec((tm, ATTN_DIM), lambda i: (i, 0)),
            pl.BlockSpec((tm, CONV_DIM), lambda i: (i, 0)),
            pl.BlockSpec((tm, d), lambda i: (i, 0)),
        ],
        out_shape=[
            jax.ShapeDtypeStruct((s, ATTN_DIM), BF16),
            jax.ShapeDtypeStruct((s, CONV_DIM), F32),
            jax.ShapeDtypeStruct((s, d), BF16),
        ],
        compiler_params=_params(1),
    )(dx, w_s)


def _ffn_fwd(x, gain, wg_s, wu_s, wd_s, layer, name, gather=()):
    s, d = x.shape
    tm = TOKEN_TILE
    f = wg_s.shape[3]
    n_gather = len(gather)
    n_tiles = s // tm

    def body(*refs):
        x_ref, g_ref, wg_ref, wu_ref, wd_ref = refs[:5]
        o_ref, gate_ref, up_ref = refs[5 + n_gather : 8 + n_gather]
        h_s = refs[8 + 2 * n_gather]
        i, j = pl.program_id(0), pl.program_id(1)
        if n_gather:
            copies = _WeightGather(refs[5 : 5 + n_gather], refs[8 + n_gather : 8 + 2 * n_gather], refs[9 + 2 * n_gather :])
            pl.when((i == 0) & (j == 0))(copies.begin)
            pl.when((i == (3 * n_tiles) // 4) & (j == 0))(copies.relay)

        @pl.when(j == 0)
        def _():
            xv = x_ref[...]
            r = lax.rsqrt(jnp.mean(xv * xv, axis=-1, keepdims=True) + EPS)
            h_s[...] = (xv * r * g_ref[...]).astype(BF16)
            o_ref[...] = xv

        halves = [slice(0, tm // 2), slice(tm // 2, tm)]
        pre = [(_dot(h_s[r, :], wg_ref[0, 0]), _dot(h_s[r, :], wu_ref[0, 0])) for r in halves]
        act = [((gate / (1.0 + jnp.exp(-gate))) * up).astype(BF16) for gate, up in pre]
        for r, (gate, up) in zip(halves, pre):
            gate_ref[0, r, :] = gate
            up_ref[0, r, :] = up
        for r, a in zip(halves, act):
            o_ref[r, :] += _dot(a, wd_ref[0, 0])

        if n_gather:
            pl.when((i == n_tiles - 1) & (j == N_CHIPS - 1))(copies.finish)

    hid = pl.BlockSpec((1, tm, f), lambda i, j: (j, i, 0))
    hid_shape = jax.ShapeDtypeStruct((N_CHIPS, s, f), F32)
    return pl.pallas_call(
        body,
        name=name,
        grid=(n_tiles, N_CHIPS),
        in_specs=[
            pl.BlockSpec((tm, d), lambda i, j: (i, 0)),
            pl.BlockSpec((1, d), lambda i, j: (0, 0)),
            pl.BlockSpec((1, 1, d, f), lambda i, j: (j, layer, 0, 0)),
            pl.BlockSpec((1, 1, d, f), lambda i, j: (j, layer, 0, 0)),
            pl.BlockSpec((1, 1, f, d), lambda i, j: (j, layer, 0, 0)),
        ] + [ANY] * n_gather,
        out_specs=[pl.BlockSpec((tm, d), lambda i, j: (i, 0)), hid, hid] + [ANY] * n_gather,
        out_shape=[jax.ShapeDtypeStruct((s, d), F32), hid_shape, hid_shape]
        + [jax.ShapeDtypeStruct((N_CHIPS,) + w.shape, w.dtype) for w in gather],
        scratch_shapes=[pltpu.VMEM((tm, d), BF16)] + (_gather_scratch(n_gather) if n_gather else []),
        compiler_params=_params(2),
    )(x, gain, wg_s, wu_s, wd_s, *gather)


def _rms_bwd(xv, gain, dh):
    r = lax.rsqrt(jnp.mean(xv * xv, axis=-1, keepdims=True) + EPS)
    xhat = xv * r
    dxhat = dh * gain
    dx = r * (dxhat - xhat * jnp.mean(dxhat * xhat, axis=-1, keepdims=True))
    return dx, jnp.sum(dh * xhat, axis=0, keepdims=True)


def _ffn_bwd(x, dy, gain, gate_s, up_s, wg_s, wu_s, wd_s, layer, name):
    s, d = x.shape
    tm = TOKEN_TILE
    f = wg_s.shape[3]

    def body(x_ref, dy_ref, g_ref, gate_ref, up_ref, wg_ref, wu_ref, wd_ref, dx_ref, dgain_ref, h_ref, dyb_ref, dg_ref, du_ref, act_ref, acc_s):
        i, j = pl.program_id(0), pl.program_id(1)

        @pl.when((i == 0) & (j == 0))
        def _():
            dgain_ref[...] = jnp.zeros_like(dgain_ref)

        @pl.when(j == 0)
        def _():
            xv = x_ref[...]
            r = lax.rsqrt(jnp.mean(xv * xv, axis=-1, keepdims=True) + EPS)
            h_ref[...] = (xv * r * g_ref[...]).astype(BF16)
            dyb_ref[...] = dy_ref[...].astype(BF16)
            acc_s[...] = jnp.zeros_like(acc_s)

        halves = [slice(0, tm // 2), slice(tm // 2, tm)]
        pre = [(gate_ref[0, r, :], up_ref[0, r, :], _dot_nt(dyb_ref[r, :], wd_ref[0, 0])) for r in halves]
        grads = []
        for r, (gate, up, dact) in zip(halves, pre):
            sig = 1.0 / (1.0 + jnp.exp(-gate))
            silu = gate * sig
            dgate = (dact * up * (sig * (1.0 + gate * (1.0 - sig)))).astype(BF16)
            dup = (dact * silu).astype(BF16)
            act_ref[0, r, :] = (silu * up).astype(BF16)
            dg_ref[0, r, :] = dgate
            du_ref[0, r, :] = dup
            grads.append((dgate, dup))
        for r, (dgate, dup) in zip(halves, grads):
            acc_s[r, :] += _dot_nt(dgate, wg_ref[0, 0]) + _dot_nt(dup, wu_ref[0, 0])

        @pl.when(j == N_CHIPS - 1)
        def _():
            dxn, dgain = _rms_bwd(x_ref[...], g_ref[...], acc_s[...])
            dx_ref[...] = dy_ref[...] + dxn
            dgain_ref[...] += dgain

    tok = pl.BlockSpec((tm, d), lambda i, j: (i, 0))
    vec = pl.BlockSpec((1, d), lambda i, j: (0, 0))
    hid = pl.BlockSpec((1, tm, f), lambda i, j: (j, i, 0))
    hid_shape = jax.ShapeDtypeStruct((N_CHIPS, s, f), BF16)
    return pl.pallas_call(
        body,
        name=name,
        grid=(s // tm, N_CHIPS),
        in_specs=[
            tok, tok, vec, hid, hid,
            pl.BlockSpec((1, 1, d, f), lambda i, j: (j, layer, 0, 0)),
            pl.BlockSpec((1, 1, d, f), lambda i, j: (j, layer, 0, 0)),
            pl.BlockSpec((1, 1, f, d), lambda i, j: (j, layer, 0, 0)),
        ],
        out_specs=[tok, vec, tok, tok, hid, hid, hid],
        out_shape=[
            jax.ShapeDtypeStruct((s, d), F32),
            jax.ShapeDtypeStruct((1, d), F32),
            jax.ShapeDtypeStruct((s, d), BF16),
            jax.ShapeDtypeStruct((s, d), BF16),
            hid_shape, hid_shape, hid_shape,
        ],
        scratch_shapes=[pltpu.VMEM((tm, d), F32)],
        compiler_params=_params(2),
    )(x, dy, gain, gate_s, up_s, wg_s, wu_s, wd_s)


def _in_proj_bwd(x, dx_res, gain, dproj, w_s, layer, name):
    s, d = x.shape
    tm = TOKEN_TILE
    n = w_s.shape[3]

    def body(x_ref, r_ref, g_ref, dp_ref, w_ref, dx_ref, dgain_ref):
        @pl.when(pl.program_id(0) == 0)
        def _():
            dgain_ref[...] = jnp.zeros_like(dgain_ref)

        dh = _dot_nt(dp_ref[:, 0:n], w_ref[0, 0])
        for j in range(1, N_CHIPS):
            dh = dh + _dot_nt(dp_ref[:, j * n : (j + 1) * n], w_ref[j, 0])
        dxn, dgain = _rms_bwd(x_ref[...], g_ref[...], dh)
        dx_ref[...] = r_ref[...] + dxn
        dgain_ref[...] += dgain

    tok = pl.BlockSpec((tm, d), lambda i: (i, 0))
    vec = pl.BlockSpec((1, d), lambda i: (0, 0))
    return pl.pallas_call(
        body,
        name=name,
        grid=(s // tm,),
        in_specs=[tok, tok, vec, pl.BlockSpec((tm, N_CHIPS * n), lambda i: (i, 0)), pl.BlockSpec((N_CHIPS, 1, d, n), lambda i: (0, layer, 0, 0))],
        out_specs=[tok, vec],
        out_shape=[jax.ShapeDtypeStruct((s, d), F32), jax.ShapeDtypeStruct((1, d), F32)],
        compiler_params=_params(1),
    )(x, dx_res, gain, dproj, w_s)


def _loss_grad(y, target, name):
    s, d = y.shape
    tm = TOKEN_TILE

    def body(y_ref, t_ref, dy_ref, l_ref):
        @pl.when(pl.program_id(0) == 0)
        def _():
            l_ref[...] = jnp.zeros_like(l_ref)

        err = y_ref[...] - t_ref[...]
        dy_ref[...] = err / d
        l_ref[...] += jnp.sum(err * err, axis=0, keepdims=True) * (0.5 / d)

    tok = pl.BlockSpec((tm, d), lambda i: (i, 0))
    return pl.pallas_call(
        body,
        name=name,
        grid=(s // tm,),
        in_specs=[tok, tok],
        out_specs=[tok, pl.BlockSpec((1, d), lambda i: (0, 0))],
        out_shape=[jax.ShapeDtypeStruct((s, d), F32), jax.ShapeDtypeStruct((1, d), F32)],
        compiler_params=_params(1),
    )(y, target)


def _wgrad(a, b, a_spec, b_spec, n_blocks, k, n, name):
    n_tiles = a.shape[-2] // min(WGRAD_TILE, a.shape[-2])

    def body(a_ref, b_ref, o_ref):
        @pl.when(pl.program_id(1) == 0)
        def _():
            o_ref[...] = jnp.zeros_like(o_ref)

        av = a_ref[0] if len(a_ref.shape) == 3 else a_ref[...]
        bv = b_ref[0] if len(b_ref.shape) == 3 else b_ref[...]
        o_ref[0] += _dot_tn(av, bv)

    return pl.pallas_call(
        body,
        name=name,
        grid=(n_blocks, n_tiles),
        in_specs=[a_spec, b_spec],
        out_specs=pl.BlockSpec((1, k, n), lambda j, i: (j, 0, 0)),
        out_shape=jax.ShapeDtypeStruct((n_blocks, k, n), F32),
        compiler_params=_params(2),
    )(a, b)


def _mesh_position():
    return lax.axis_index("x"), lax.axis_index("y"), lax.axis_index("c")


def _other_chips(x, y):
    return [(1 - x, y), (x, 1 - y), (1 - x, 1 - y)]


def _half_rows(ref_rows, c):
    half = ref_rows // 2
    return pl.ds(c * half, half)


class _WeightGather:
    def __init__(self, ins, outs, sems):
        self.ins, self.outs = ins, outs
        send_sems, recv_sems, pass_send_sems, pass_recv_sems, self.local_sems = sems
        self.ici, self.d2d = (send_sems, recv_sems), (pass_send_sems, pass_recv_sems)
        self.x, self.y, self.c = _mesh_position()
        self.me = 2 * self.x + self.y
        self.sibling = (self.x, self.y, 1 - self.c)
        self.chips = _other_chips(self.x, self.y)

    def _copy(self, t, k, chip_index, core, to, sems, src=None):
        dst = self.outs[t].at[chip_index, :, _half_rows(self.ins[t].shape[1], core), :]
        return pltpu.make_async_remote_copy(
            src_ref=dst if src is None else src, dst_ref=dst, send_sem=sems[0].at[t, k], recv_sem=sems[1].at[t, k],
            device_id=to, device_id_type=MESH_ID,
        )

    def _own(self, t):
        return pltpu.make_async_copy(self.ins[t], self.outs[t].at[self.me], self.local_sems.at[t])

    def _sends(self):
        for t in range(len(self.ins)):
            mine = self.ins[t].at[:, _half_rows(self.ins[t].shape[1], self.c), :]
            for k, (px, py) in enumerate(self.chips):
                yield self._copy(t, k, self.me, self.c, (px, py, self.c), self.ici, src=mine)

    def _passes(self, core, sems):
        for t in range(len(self.ins)):
            for k, (px, py) in enumerate(self.chips):
                yield self._copy(t, k, 2 * px + py, core, self.sibling, sems)

    def begin(self):
        for t in range(len(self.ins)):
            self._own(t).start()
        for cp in self._sends():
            cp.start()

    def relay(self):
        for arrived, onward in zip(self._passes(self.c, self.ici), self._passes(self.c, self.d2d)):
            arrived.wait_recv()
            onward.start()

    def finish(self):
        for cp in self._passes(1 - self.c, self.d2d):
            cp.wait_recv()
        for cp in list(self._sends()) + list(self._passes(self.c, self.d2d)):
            cp.wait_send()
        for t in range(len(self.ins)):
            self._own(t).wait()


def _gather_scratch(n):
    sems = pltpu.SemaphoreType.DMA((n, N_CHIPS - 1))
    return [sems, sems, sems, sems, pltpu.SemaphoreType.DMA((n,))]


def _gather_weights(shards):
    n = len(shards)

    def body(*refs):
        gather = _WeightGather(refs[:n], refs[n : 2 * n], refs[2 * n :])
        gather.begin()
        gather.relay()
        gather.finish()

    return pl.pallas_call(
        body,
        name="gather_weights",
        in_specs=[ANY] * n,
        out_specs=[ANY] * n,
        out_shape=[jax.ShapeDtypeStruct((N_CHIPS,) + w.shape, w.dtype) for w in shards],
        scratch_shapes=_gather_scratch(n),
    )(*shards)


def _swap_halves(grads):
    n = len(grads)

    def body(*refs):
        ins, outs = refs[:n], refs[n : 2 * n]
        send_sems, recv_sems = refs[2 * n :]
        x, y, c = _mesh_position()
        copies = []
        for t in range(n):
            copies.append(pltpu.make_async_remote_copy(
                src_ref=ins[t].at[:, _half_rows(ins[t].shape[1], 1 - c), :], dst_ref=outs[t],
                send_sem=send_sems.at[t], recv_sem=recv_sems.at[t], device_id=(x, y, 1 - c), device_id_type=MESH_ID,
            ))
            copies[-1].start()
        for cp in copies:
            cp.wait()

    sems = pltpu.SemaphoreType.DMA((n,))
    return pl.pallas_call(
        body,
        name="swap_halves",
        in_specs=[ANY] * n,
        out_specs=[ANY] * n,
        out_shape=[jax.ShapeDtypeStruct((g.shape[0], g.shape[1] // 2, g.shape[2]), g.dtype) for g in grads],
        scratch_shapes=[sems, sems],
    )(*grads)


def _scatter_to_chips(parts):
    n = len(parts)

    def body(*refs):
        ins, outs = refs[:n], refs[n : 2 * n]
        send_sems, recv_sems = refs[2 * n :]
        x, y, c = _mesh_position()
        copies = []
        for t in range(n):
            for k, (px, py) in enumerate(_other_chips(x, y)):
                copies.append(pltpu.make_async_remote_copy(
                    src_ref=ins[t].at[2 * px + py], dst_ref=outs[t].at[k],
                    send_sem=send_sems.at[t, k], recv_sem=recv_sems.at[t, k], device_id=(px, py, c), device_id_type=MESH_ID,
                ))
                copies[-1].start()
        for cp in copies:
            cp.wait()

    sems = pltpu.SemaphoreType.DMA((n, N_CHIPS - 1))
    return pl.pallas_call(
        body,
        name="scatter_to_chips",
        in_specs=[ANY] * n,
        out_specs=[ANY] * n,
        out_shape=[jax.ShapeDtypeStruct((N_CHIPS - 1,) + p.shape[1:], p.dtype) for p in parts],
        scratch_shapes=[sems, sems],
    )(*parts)


def _join_halves(shards):
    n = len(shards)

    def body(*refs):
        outs = refs[n : 2 * n]
        send_sems, recv_sems = refs[2 * n :]
        x, y, c = _mesh_position()
        copies = []
        for t in range(n):
            mine = outs[t].at[:, _half_rows(outs[t].shape[1], c), :]
            copies.append(pltpu.make_async_remote_copy(
                src_ref=mine, dst_ref=mine, send_sem=send_sems.at[t], recv_sem=recv_sems.at[t],
                device_id=(x, y, 1 - c), device_id_type=MESH_ID,
            ))
            copies[-1].start()
        for cp in copies:
            cp.wait()

    sems = pltpu.SemaphoreType.DMA((n,))
    return pl.pallas_call(
        body,
        name="join_halves",
        in_specs=[ANY] * n,
        out_specs=[ANY] * n,
        out_shape=[jax.ShapeDtypeStruct(g.shape, g.dtype) for g in shards],
        input_output_aliases={t: t for t in range(n)},
        scratch_shapes=[sems, sems],
    )(*shards)


def _gather_small(pack):
    def body(p_ref, o_ref, send_sems, recv_sems, local_sem):
        x, y, c = _mesh_position()
        own = pltpu.make_async_copy(p_ref, o_ref.at[4 * x + 2 * y + c], local_sem)
        own.start()
        copies = []
        for k in range(1, N_DEV):
            px, py, pc = x ^ (k >> 2), y ^ ((k >> 1) & 1), c ^ (k & 1)
            send = pltpu.make_async_remote_copy(
                src_ref=p_ref, dst_ref=o_ref.at[4 * x + 2 * y + c], send_sem=send_sems.at[k - 1], recv_sem=recv_sems.at[k - 1],
                device_id=(px, py, pc), device_id_type=MESH_ID,
            )
            send.start()
            copies.append((send, 4 * px + 2 * py + pc))
        for send, peer_slot in copies:
            send.wait_send()
        for k in range(1, N_DEV):
            px, py, pc = x ^ (k >> 2), y ^ ((k >> 1) & 1), c ^ (k & 1)
            pltpu.make_async_remote_copy(
                src_ref=p_ref, dst_ref=o_ref.at[4 * px + 2 * py + pc], send_sem=send_sems.at[k - 1], recv_sem=recv_sems.at[k - 1],
                device_id=(px, py, pc), device_id_type=MESH_ID,
            ).wait_recv()
        own.wait()

    sems = pltpu.SemaphoreType.DMA((N_DEV - 1,))
    return pl.pallas_call(
        body,
        name="gather_small",
        in_specs=[VMEM_SPEC],
        out_specs=VMEM_SPEC,
        out_shape=jax.ShapeDtypeStruct((N_DEV,) + pack.shape, pack.dtype),
        scratch_shapes=[sems, sems, pltpu.SemaphoreType.DMA],
    )(pack)


def _row_tile(rows):
    for tile in (256, 128, 64, 32, 16, 8):
        if rows % tile == 0:
            return tile
    return rows


def _add_half(grad, received, half_index, name):
    slots, h, cdim = received.shape
    tile = _row_tile(h)
    per_half = h // tile

    def body(c_ref, g_ref, r_ref, o_ref, ob_ref):
        total = g_ref[...] + r_ref[...]
        o_ref[...] = total
        ob_ref[...] = total.astype(BF16)

    block = pl.BlockSpec((1, tile, cdim), lambda j, i, c: (j, i, 0))
    grid_spec = pltpu.PrefetchScalarGridSpec(
        num_scalar_prefetch=1,
        grid=(slots, per_half),
        in_specs=[pl.BlockSpec((1, tile, cdim), lambda j, i, c: (j, c[0] * per_half + i, 0)), block],
        out_specs=[block, block],
    )
    return pl.pallas_call(
        body, name=name, grid_spec=grid_spec,
        out_shape=[jax.ShapeDtypeStruct(received.shape, F32), jax.ShapeDtypeStruct(received.shape, BF16)],
        compiler_params=_params(2),
    )(half_index, grad, received)


def _add_chips(part, received, chip_index, core_index, layer, n_layers, shard, name):
    _, h, cdim = part.shape
    tile = _row_tile(h)
    per_half = h // tile

    def body(chip_ref, core_ref, p_ref, r_ref, *rest):
        o_ref = rest[-1]
        o_ref[0] = ((p_ref[0] + r_ref[0].astype(F32)) + r_ref[1].astype(F32)) + r_ref[2].astype(F32)

    in_specs = [
        pl.BlockSpec((1, tile, cdim), lambda i, chip, core: (chip[0], i, 0)),
        pl.BlockSpec((N_CHIPS - 1, tile, cdim), lambda i, chip, core: (0, i, 0)),
    ]
    operands = [chip_index, core_index, part, received]
    aliases = {}
    if shard is not None:
        in_specs.append(ANY)
        operands.append(shard)
        aliases = {4: 0}
    grid_spec = pltpu.PrefetchScalarGridSpec(
        num_scalar_prefetch=2,
        grid=(per_half,),
        in_specs=in_specs,
        out_specs=pl.BlockSpec((1, tile, cdim), lambda i, chip, core: (layer, core[0] * per_half + i, 0)),
    )
    return pl.pallas_call(
        body, name=name, grid_spec=grid_spec, out_shape=jax.ShapeDtypeStruct((n_layers, 2 * h, cdim), F32),
        input_output_aliases=aliases, compiler_params=_params(1),
    )(*operands)


def _adamw(w, g, m, v, name):
    rows, cdim = w.shape
    tile = _row_tile(rows)

    def body(w_ref, g_ref, m_ref, v_ref, d_ref, nm_ref, nv_ref):
        gv = g_ref[...]
        nm = ADAM_B1 * m_ref[...] + (1.0 - ADAM_B1) * gv
        nv = ADAM_B2 * v_ref[...] + (1.0 - ADAM_B2) * (gv * gv)
        m_hat = nm / (1.0 - ADAM_B1 ** ADAM_STEP)
        v_hat = nv / (1.0 - ADAM_B2 ** ADAM_STEP)
        d_ref[...] = -ADAM_LR * (m_hat / (jnp.sqrt(v_hat) + ADAM_EPS) + ADAM_WD * w_ref[...])
        nm_ref[...] = nm
        nv_ref[...] = nv

    spec = pl.BlockSpec((tile, cdim), lambda i: (i, 0))
    shape = jax.ShapeDtypeStruct((rows, cdim), F32)
    return pl.pallas_call(
        body, name=name, grid=(rows // tile,), in_specs=[spec] * 4, out_specs=[spec] * 3, out_shape=[shape] * 3,
        compiler_params=_params(1),
    )(w, g, m, v)


SMALL_ROWS, SMALL_COLS = 24, 1024
ROW_NORM_MIX, ROW_NORM_FFN, ROW_LOSS, ROW_Q_NORM, ROW_K_NORM, ROW_CONV = 0, 2, 4, 8, 10, 16


def _sum_small(gathered):
    def body(g_ref, o_ref, heads_ref, lanes_ref):
        total = g_ref[0]
        for dev in range(1, N_DEV):
            total = total + g_ref[dev]
        o_ref[...] = total
        heads = o_ref[8:16, 0:LANES]
        for grp in range(1, ATTN_DIM // LANES):
            heads = heads + o_ref[8:16, grp * LANES : (grp + 1) * LANES]
        heads_ref[...] = heads + pltpu.roll(heads, HEAD_DIM, 1)
        lanes_ref[...] = jnp.broadcast_to(jnp.sum(o_ref[0:8, :], axis=-1, keepdims=True), (8, LANES))

    return pl.pallas_call(
        body,
        name="sum_small",
        in_specs=[VMEM_SPEC],
        out_specs=[VMEM_SPEC] * 3,
        out_shape=[jax.ShapeDtypeStruct((SMALL_ROWS, SMALL_COLS), F32), jax.ShapeDtypeStruct((8, LANES), F32), jax.ShapeDtypeStruct((8, LANES), F32)],
    )(gathered)


def _pad_rows(a, rows):
    return jnp.pad(a, ((0, rows - a.shape[0]), (0, 0)))


def _pad_to(a, rows, cols):
    return jnp.pad(a, ((0, rows - a.shape[0]), (0, cols - a.shape[1])))


def _conv_taps(conv_s):
    return jnp.transpose(conv_s[:, 0, 0:8], (1, 0, 2)).reshape(8, -1)


def _local_step(x, target, norm_mix, q_norm, k_norm, norm_ffn, layer_weights):
    layer_weights = list(layer_weights)
    n_layers = norm_mix.shape[0]
    s, d = x.shape
    tw = min(WGRAD_TILE, s)
    n_in = layer_weights[0][0].shape[3]
    f = layer_weights[0][2].shape[3]
    saved = []
    for l in range(n_layers):
        win_s, wout_s, wg_s, wu_s, wd_s, conv_s = layer_weights[l]
        taps = _conv_taps(conv_s)
        q_gain = jnp.tile(q_norm[l][None, :], (1, 2))
        k_gain = jnp.tile(k_norm[l][None, :], (1, 2))
        h1, proj = _norm_matmul(x, norm_mix[l][None, :], win_s, 0, f"in_proj_{l}")
        qn, kn, vb = _qkv_prep(proj, q_gain, k_gain, f"qkv_prep_{l}")
        attn = _attn_fwd(qn, kn, vb, f"attn_fwd_{l}")
        conv = _conv_fwd(proj, taps, f"conv_fwd_{l}")
        x_mid = _out_proj(x, attn, conv, wout_s, 0, f"out_proj_{l}")
        pending = layer_weights[l + 1] if l + 1 < n_layers and isinstance(layer_weights[l + 1], list) else ()
        x_out, gate, up, *arrived = _ffn_fwd(x_mid, norm_ffn[l][None, :], wg_s, wu_s, wd_s, 0, f"ffn_fwd_{l}", gather=pending)
        if pending:
            layer_weights[l + 1] = tuple(arrived)
        saved.append(dict(x=x, h1=h1, proj=proj, qn=qn, kn=kn, vb=vb, attn=attn, conv=conv, x_mid=x_mid, q_gain=q_gain, k_gain=k_gain,
                          gate=gate, up=up, taps=taps))
        x = x_out

    dy, loss_lanes = _loss_grad(x, target, "loss_grad")
    grads = [None] * n_layers
    for l in reversed(range(n_layers)):
        sv = saved[l]
        win_s, wout_s, wg_s, wu_s, wd_s, _ = layer_weights[l]
        dx_mid, d_norm_ffn, h2, dyb, dgate, dup, act = _ffn_bwd(sv["x_mid"], dy, norm_ffn[l][None, :], sv["gate"], sv["up"], wg_s, wu_s, wd_s, 0, f"ffn_bwd_{l}")
        tok2 = pl.BlockSpec((tw, d), lambda j, i: (i, 0))
        hid = pl.BlockSpec((1, tw, f), lambda j, i: (j, i, 0))
        d_wg = _wgrad(h2, dgate, tok2, hid, N_CHIPS, d, f, f"wgrad_gate_{l}")
        d_wu = _wgrad(h2, dup, tok2, hid, N_CHIPS, d, f, f"wgrad_up_{l}")
        d_wd = _wgrad(act, dyb, hid, tok2, N_CHIPS, f, d, f"wgrad_down_{l}")
        d_attn, d_conv, dxb = _out_proj_bwd(dx_mid, wout_s, 0, f"out_proj_bwd_{l}")
        rows_out = wout_s.shape[2]
        mix_spec_a = pl.BlockSpec((tw, rows_out), lambda j, i: (i, j))
        d_wout_a = _wgrad(sv["attn"], dxb, mix_spec_a, tok2, ATTN_DIM // rows_out, rows_out, d, f"wgrad_out_attn_{l}")
        d_wout_c = _wgrad(sv["conv"], dxb, mix_spec_a, tok2, CONV_DIM // rows_out, rows_out, d, f"wgrad_out_conv_{l}")
        d_wout = jnp.concatenate([d_wout_a, d_wout_c], axis=0)
        dq, dk, dv = _attn_bwd(sv["qn"], sv["kn"], sv["vb"], d_attn, f"attn_bwd_{l}")
        dproj, d_conv_w = _conv_bwd(sv["proj"], sv["taps"], d_conv, f"conv_bwd_{l}")
        dproj, d_qg, d_kg = _qkv_prep_bwd(sv["proj"], sv["q_gain"], sv["k_gain"], dq, dk, dv, dproj, f"qkv_prep_bwd_{l}")
        d_win = _wgrad(sv["h1"], dproj, tok2, pl.BlockSpec((tw, n_in), lambda j, i: (i, j)), N_CHIPS, d, n_in, f"wgrad_in_{l}")
        dy, d_norm_mix = _in_proj_bwd(sv["x"], dx_mid, norm_mix[l][None, :], dproj, win_s, 0, f"in_proj_bwd_{l}")
        grads[l] = dict(norm_mix=d_norm_mix, norm_ffn=d_norm_ffn, q_norm=d_qg, k_norm=d_kg, conv_w=d_conv_w,
                        w_in=d_win, w_out=d_wout, w_gate=d_wg, w_up=d_wu, w_down=d_wd)
    return loss_lanes, dy, grads


BIG = ("w_in", "w_out", "w_gate", "w_up", "w_down")


def kernel(x, norm_mix, w_in, q_norm, k_norm, conv_w, w_out, norm_ffn, w_gate, w_up, w_down, loss_target, m_norm_mix, m_w_in, m_q_norm, m_k_norm, m_conv_w, m_w_out, m_norm_ffn, m_w_gate, m_w_up, m_w_down, v_norm_mix, v_w_in, v_q_norm, v_k_norm, v_conv_w, v_w_out, v_norm_ffn, v_w_gate, v_w_up, v_w_down):
    n_layers = norm_mix.shape[0]
    weights = dict(w_in=w_in, w_out=w_out, w_gate=w_gate, w_up=w_up, w_down=w_down)
    moments_m = dict(w_in=m_w_in, w_out=m_w_out, w_gate=m_w_gate, w_up=m_w_up, w_down=m_w_down)
    moments_v = dict(w_in=v_w_in, w_out=v_w_out, w_gate=v_w_gate, w_up=v_w_up, w_down=v_w_down)
    cx, cy, cc = _mesh_position()
    chip_index = (2 * cx + cy).astype(jnp.int32).reshape(1)
    core_index = cc.astype(jnp.int32).reshape(1)

    conv_pad = jnp.pad(conv_w, ((0, 0), (0, 16 - conv_w.shape[1]), (0, 0)))

    def shards_of(layer):
        return [weights[k][layer : layer + 1].astype(BF16) for k in BIG] + [conv_pad[layer : layer + 1]]

    layer_weights = [tuple(_gather_weights(shards_of(0)))] + [shards_of(layer) for layer in range(1, n_layers)]

    loss_lanes, grad_x, grads = _local_step(
        x[0], loss_target[0], norm_mix, q_norm, k_norm, norm_ffn, layer_weights)

    flat = [grads[l][k] for l in range(n_layers) for k in BIG]
    names = [f"{k}_{l}" for l in range(n_layers) for k in BIG]
    received = _swap_halves(flat)
    parts = [_add_half(g, r, core_index, f"add_half_{nm}") for g, r, nm in zip(flat, received, names)]
    from_chips = _scatter_to_chips([p_bf16 for _, p_bf16 in parts])
    shards = [None] * len(BIG)
    for n, ((p_f32, _), r, nm) in enumerate(zip(parts, from_chips, names)):
        layer, tensor = divmod(n, len(BIG))
        shards[tensor] = _add_chips(p_f32, r, chip_index, core_index, layer, n_layers, shards[tensor], f"add_chips_{nm}")
    big_grads = dict(zip(BIG, _join_halves(shards)))

    def lanes(a):
        return _pad_to(a, a.shape[0], SMALL_COLS)

    def tile_of(*groups):
        return _pad_rows(jnp.concatenate([lanes(jnp.concatenate(g, axis=0)) for g in groups], axis=0), 8)

    layers = range(n_layers)
    pack = jnp.concatenate([
        tile_of([grads[l]["norm_mix"] for l in layers], [grads[l]["norm_ffn"] for l in layers], [loss_lanes]),
        tile_of([grads[l]["q_norm"] for l in layers], [grads[l]["k_norm"] for l in layers]),
        tile_of([grads[l]["conv_w"][0:3] for l in layers]),
    ], axis=0)
    small, small_heads, small_lanes = _sum_small(_gather_small(pack))
    loss = small_lanes[ROW_LOSS, 0]
    d_model = norm_mix.shape[1]
    conv_cols = conv_w.shape[2]
    conv_all = small[ROW_CONV : ROW_CONV + 3 * n_layers, 0:CONV_DIM].reshape(n_layers, 3, CONV_DIM)
    small_grads = dict(
        norm_mix=small[ROW_NORM_MIX : ROW_NORM_MIX + n_layers, 0:d_model],
        norm_ffn=small[ROW_NORM_FFN : ROW_NORM_FFN + n_layers, 0:d_model],
        q_norm=small_heads[ROW_Q_NORM - 8 : ROW_Q_NORM - 8 + n_layers, 0:HEAD_DIM],
        k_norm=small_heads[ROW_K_NORM - 8 : ROW_K_NORM - 8 + n_layers, 0:HEAD_DIM],
        conv_w=lax.dynamic_slice_in_dim(conv_all, (2 * cx + cy) * conv_cols, conv_cols, axis=2),
    )

    out_grad, out_delta, out_m, out_v = {}, {}, {}, {}
    for k in BIG:
        shape = weights[k].shape
        view = (shape[0] * shape[1], shape[2])
        g = big_grads[k]
        delta, new_m, new_v = _adamw(weights[k].reshape(view), g.reshape(view), moments_m[k].reshape(view), moments_v[k].reshape(view), f"adamw_{k}")
        out_grad[k], out_delta[k], out_m[k], out_v[k] = g, delta.reshape(shape), new_m.reshape(shape), new_v.reshape(shape)

    small_w = dict(norm_mix=norm_mix, norm_ffn=norm_ffn, q_norm=q_norm, k_norm=k_norm, conv_w=conv_w)
    small_m = dict(norm_mix=m_norm_mix, norm_ffn=m_norm_ffn, q_norm=m_q_norm, k_norm=m_k_norm, conv_w=m_conv_w)
    small_v = dict(norm_mix=v_norm_mix, norm_ffn=v_norm_ffn, q_norm=v_q_norm, k_norm=v_k_norm, conv_w=v_conv_w)
    order = ("norm_mix", "norm_ffn", "q_norm", "k_norm", "conv_w")

    def packed(tree):
        parts2 = [_pad_to(tree[k].reshape(-1, tree[k].shape[-1]), tree[k].reshape(-1, tree[k].shape[-1]).shape[0], SMALL_COLS) for k in order]
        return _pad_rows(jnp.concatenate(parts2, axis=0), SMALL_ROWS)

    delta_p, m_p, v_p = _adamw(packed(small_w), packed(small_grads), packed(small_m), packed(small_v), "adamw_small")
    row = 0
    for k in order:
        shape = small_w[k].shape
        n_rows = 1
        for dim in shape[:-1]:
            n_rows *= dim
        cut = (slice(row, row + n_rows), slice(0, shape[-1]))
        out_grad[k] = small_grads[k]
        out_delta[k], out_m[k], out_v[k] = delta_p[cut].reshape(shape), m_p[cut].reshape(shape), v_p[cut].reshape(shape)
        row += n_rows

    names_out = ("norm_mix", "w_in", "q_norm", "k_norm", "conv_w", "w_out", "norm_ffn", "w_gate", "w_up", "w_down")
    return (loss, grad_x[None], *[out_grad[k] for k in names_out], *[out_delta[k] for k in names_out],
            *[out_m[k] for k in names_out], *[out_v[k] for k in names_out])
```

```python
import functools

import jax
import jax.numpy as jnp
from jax import lax
from jax.experimental import pallas as pl
from jax.experimental.pallas import tpu as pltpu

F32 = jnp.float32
BF16 = jnp.bfloat16

EPS = 1e-6
HEAD_DIM = 64
LANES = 128
ATTN_DIM = 512
CONV_DIM = 512
N_CHIPS = 4
N_DEV = 8
Q_SCALE = HEAD_DIM ** -0.5
ATTN_Q_TILE = 256
ATTN_TILE = 256
TOKEN_TILE = 512
WGRAD_TILE = 2048
VMEM_LIMIT = 56 * 1024 * 1024

ADAM_LR = 0.001
ADAM_B1 = 0.9
ADAM_B2 = 0.999
ADAM_EPS = 1e-08
ADAM_WD = 0.01
ADAM_STEP = 10

MESH_ID = pl.DeviceIdType.MESH
ANY = pl.BlockSpec(memory_space=pl.ANY)
VMEM_SPEC = pl.BlockSpec(memory_space=pltpu.VMEM)


def _params(n_axes):
    return pltpu.CompilerParams(dimension_semantics=("arbitrary",) * n_axes, vmem_limit_bytes=VMEM_LIMIT)


def _dot(a, b):
    return jnp.dot(a, b, preferred_element_type=F32)


def _dot_nt(a, b):
    return lax.dot_general(a, b, (((1,), (1,)), ((), ())), preferred_element_type=F32)


def _dot_tn(a, b):
    return lax.dot_general(a, b, (((0,), (0,)), ((), ())), preferred_element_type=F32)


SCORE_MAX = 80.0
UNDERFLOW_EXIT = 90.0


def _scores(q, k):
    return jnp.minimum(_dot_nt(q, k), SCORE_MAX)


def _softplus(z):
    return jnp.log(1.0 + jnp.exp(z))


def _norm_matmul(x, gain, w_s, layer, name):
    s, d = x.shape
    n_blocks, _, _, n = w_s.shape
    tm = TOKEN_TILE

    def body(x_ref, g_ref, w_ref, h_ref, o_ref):
        xv = x_ref[...]
        r = lax.rsqrt(jnp.mean(xv * xv, axis=-1, keepdims=True) + EPS)
        h = (xv * r * g_ref[...]).astype(BF16)
        h_ref[...] = h
        for j in range(n_blocks):
            o_ref[:, j * n : (j + 1) * n] = _dot(h, w_ref[j, 0])

    return pl.pallas_call(
        body,
        name=name,
        grid=(s // tm,),
        in_specs=[
            pl.BlockSpec((tm, d), lambda i: (i, 0)),
            pl.BlockSpec((1, d), lambda i: (0, 0)),
            pl.BlockSpec((n_blocks, 1, d, n), lambda i: (0, layer, 0, 0)),
        ],
        out_specs=[pl.BlockSpec((tm, d), lambda i: (i, 0)), pl.BlockSpec((tm, n_blocks * n), lambda i: (i, 0))],
        out_shape=[jax.ShapeDtypeStruct((s, d), BF16), jax.ShapeDtypeStruct((s, n_blocks * n), F32)],
        compiler_params=_params(1),
    )(x, gain, w_s)


def _head_norm(xv, gain, low):
    sq = xv * xv
    s_low = jnp.sum(jnp.where(low, sq, 0.0), axis=-1, keepdims=True)
    s_high = jnp.sum(jnp.where(low, 0.0, sq), axis=-1, keepdims=True)
    r = jnp.where(low, lax.rsqrt(s_low / HEAD_DIM + EPS), lax.rsqrt(s_high / HEAD_DIM + EPS))
    return xv * r * gain, r


def _qkv_prep(proj, q_gain, k_gain, name):
    s = proj.shape[0]
    tm = TOKEN_TILE

    def body(p_ref, qg_ref, kg_ref, q_ref, k_ref, v_ref):
        low = lax.broadcasted_iota(jnp.int32, (tm, LANES), 1) < HEAD_DIM
        for g in range(ATTN_DIM // LANES):
            cq = slice(LANES * g, LANES * (g + 1))
            ck = slice(ATTN_DIM + LANES * g, ATTN_DIM + LANES * (g + 1))
            cv = slice(2 * ATTN_DIM + LANES * g, 2 * ATTN_DIM + LANES * (g + 1))
            qn, _ = _head_norm(p_ref[:, cq], qg_ref[...], low)
            kn, _ = _head_norm(p_ref[:, ck], kg_ref[...], low)
            q_ref[:, cq] = (qn * Q_SCALE).astype(BF16)
            k_ref[:, cq] = kn.astype(BF16)
            v_ref[:, cq] = p_ref[:, cv].astype(BF16)

    out = jax.ShapeDtypeStruct((s, ATTN_DIM), BF16)
    return pl.pallas_call(
        body,
        name=name,
        grid=(s // tm,),
        in_specs=[
            pl.BlockSpec((tm, 3 * ATTN_DIM), lambda i: (i, 0)),
            pl.BlockSpec((1, LANES), lambda i: (0, 0)),
            pl.BlockSpec((1, LANES), lambda i: (0, 0)),
        ],
        out_specs=[pl.BlockSpec((tm, ATTN_DIM), lambda i: (i, 0))] * 3,
        out_shape=[out, out, out],
        compiler_params=_params(1),
    )(proj, q_gain, k_gain)


def _qkv_prep_bwd(proj, q_gain, k_gain, dq, dk, dv, dproj, name):
    s = proj.shape[0]
    tm = TOKEN_TILE

    def norm_bwd(xv, gain, dy, low):
        _, r = _head_norm(xv, gain, low)
        xhat = xv * r
        dxhat = dy * gain
        prod = dxhat * xhat
        m_low = jnp.sum(jnp.where(low, prod, 0.0), axis=-1, keepdims=True)
        m_high = jnp.sum(jnp.where(low, 0.0, prod), axis=-1, keepdims=True)
        mean = jnp.where(low, m_low, m_high) / HEAD_DIM
        return r * (dxhat - xhat * mean), jnp.sum(dy * xhat, axis=0, keepdims=True)

    def body(p_ref, qg_ref, kg_ref, dq_ref, dk_ref, dv_ref, dproj_ref, dp_ref, dqg_ref, dkg_ref):
        @pl.when(pl.program_id(0) == 0)
        def _():
            dqg_ref[...] = jnp.zeros_like(dqg_ref)
            dkg_ref[...] = jnp.zeros_like(dkg_ref)

        low = lax.broadcasted_iota(jnp.int32, (tm, LANES), 1) < HEAD_DIM
        for g in range(ATTN_DIM // LANES):
            cq = slice(LANES * g, LANES * (g + 1))
            ck = slice(ATTN_DIM + LANES * g, ATTN_DIM + LANES * (g + 1))
            cv = slice(2 * ATTN_DIM + LANES * g, 2 * ATTN_DIM + LANES * (g + 1))
            dxq, dgq = norm_bwd(p_ref[:, cq], qg_ref[...], dq_ref[:, cq] * Q_SCALE, low)
            dxk, dgk = norm_bwd(p_ref[:, ck], kg_ref[...], dk_ref[:, cq], low)
            dp_ref[:, cq] = dxq.astype(BF16)
            dp_ref[:, ck] = dxk.astype(BF16)
            dp_ref[:, cv] = dv_ref[:, cq].astype(BF16)
            dqg_ref[:, cq] += dgq
            dkg_ref[:, cq] += dgk

    grad_spec = pl.BlockSpec((tm, ATTN_DIM), lambda i: (i, 0))
    gain_spec = pl.BlockSpec((1, LANES), lambda i: (0, 0))
    sum_spec = pl.BlockSpec((1, ATTN_DIM), lambda i: (0, 0))
    return pl.pallas_call(
        body,
        name=name,
        grid=(s // tm,),
        in_specs=[pl.BlockSpec((tm, 3 * ATTN_DIM), lambda i: (i, 0)), gain_spec, gain_spec, grad_spec, grad_spec, grad_spec, ANY],
        out_specs=[pl.BlockSpec((tm, 3 * ATTN_DIM), lambda i: (i, 0)), sum_spec, sum_spec],
        out_shape=[
            jax.ShapeDtypeStruct(dproj.shape, BF16),
            jax.ShapeDtypeStruct((1, ATTN_DIM), F32),
            jax.ShapeDtypeStruct((1, ATTN_DIM), F32),
        ],
        input_output_aliases={6: 0},
        compiler_params=_params(1),
    )(proj, q_gain, k_gain, dq, dk, dv, dproj)


def _attn_tile_consts(t):
    row = lax.broadcasted_iota(jnp.int32, (t, t), 0)
    col = lax.broadcasted_iota(jnp.int32, (t, t), 1)
    return row, col


def _triangle_sum(v, triangle):
    return _dot(v.astype(BF16), triangle)


def _attn_fwd(qn, kn, vb, name):
    s = qn.shape[0]
    t = min(ATTN_TILE, s)
    tq = min(ATTN_Q_TILE, t)
    per_key_tile = t // tq

    def body(q_ref, k_ref, v_ref, o_ref):
        i = pl.program_id(1) // per_key_tile
        low = lax.broadcasted_iota(jnp.int32, (tq, LANES), 1) < HEAD_DIM
        row, col = _attn_tile_consts(t)
        suffix = (row > col).astype(BF16)
        first_row = (pl.program_id(1) % per_key_tile) * tq
        causal = lax.broadcasted_iota(jnp.int32, (tq, t), 1) < lax.broadcasted_iota(jnp.int32, (tq, t), 0) + first_row
        q = q_ref[...]
        zero_q = jnp.zeros_like(q)
        qh = (jnp.where(low, q, zero_q), jnp.where(low, zero_q, q))

        def step(kbs, carry, diagonal_first=False):
            chains = [(head, m) for head in range(2) for m in range(len(kbs))]
            masked = [diagonal_first and m == 0 for _, m in chains]
            ks = [k_ref[pl.ds(pl.multiple_of(kb * t, t), t), :] for kb in kbs]
            vs = [v_ref[pl.ds(pl.multiple_of(kb * t, t), t), :] for kb in kbs]
            z = [_scores(qh[head], ks[kb]) for head, kb in chains]
            sp = [_softplus(zc) for zc in z]
            sp = [jnp.where(causal, s_, 0.0) if mk else s_ for s_, mk in zip(sp, masked)]
            inside = [_triangle_sum(s_, suffix) for s_ in sp]
            after = [carry[head][1] for head in range(2)]
            log_a = []
            for n, (head, kb) in enumerate(chains):
                log_a.append(z[n] - sp[n] - inside[n] - after[head])
                after[head] = after[head] + jnp.sum(sp[n], axis=-1, keepdims=True)
            a = [jnp.exp(l_) for l_ in log_a]
            a = [jnp.where(causal, a_, 0.0) if mk else a_ for a_, mk in zip(a, masked)]
            acc = [carry[head][0] for head in range(2)]
            for n, (head, kb) in enumerate(chains):
                acc[head] = acc[head] + _dot(a[n].astype(BF16), vs[kb])
            return tuple((acc[head], after[head]) for head in range(2))

        def live(c):
            return jnp.minimum(jnp.min(c[0][1]), jnp.min(c[1][1])) < UNDERFLOW_EXIT

        zero = (jnp.zeros((tq, LANES), F32), jnp.zeros((tq, 1), F32))
        carry = lax.cond(i >= 1, lambda c: step((i, i - 1), c, True), lambda c: step((i,), c, True), (zero, zero))
        rest = jnp.maximum(i - 1, 0)
        carry = lax.cond((rest % 2 == 1) & live(carry), lambda c: step((i - 2,), c), lambda c: c, carry)
        pairs = rest // 2
        _, carry = lax.while_loop(
            lambda st: (st[0] < pairs) & live(st[1]),
            lambda st: (st[0] + 1, step((2 * (pairs - st[0]) - 1, 2 * (pairs - st[0]) - 2), st[1])),
            (jnp.int32(0), carry))
        o_ref[...] = jnp.where(low, carry[0][0], carry[1][0]).astype(BF16)

    return pl.pallas_call(
        body,
        name=name,
        grid=(ATTN_DIM // LANES, s // tq),
        in_specs=[
            pl.BlockSpec((tq, LANES), lambda p, i: (i, p)),
            pl.BlockSpec((s, LANES), lambda p, i: (0, p)),
            pl.BlockSpec((s, LANES), lambda p, i: (0, p)),
        ],
        out_specs=pl.BlockSpec((tq, LANES), lambda p, i: (i, p)),
        out_shape=jax.ShapeDtypeStruct((s, ATTN_DIM), BF16),
        compiler_params=_params(2),
    )(qn, kn, vb)


def _attn_bwd(qn, kn, vb, do, name, scatter=()):
    s = qn.shape[0]
    t = min(ATTN_TILE, s)
    nq = s // t
    n_scatter = len(scatter)
    n_pairs = ATTN_DIM // LANES

    def body(*refs):
        q_ref, k_ref, v_ref, do_ref = refs[:4]
        dq_ref, dk_ref, dv_ref = refs[4 + n_scatter : 7 + n_scatter]
        a_s, sg_s = refs[7 + 2 * n_scatter : 9 + 2 * n_scatter]
        i = pl.program_id(1)
        if n_scatter:
            copies = _ChipScatter(refs[4 : 4 + n_scatter], refs[7 + n_scatter : 7 + 2 * n_scatter], refs[9 + 2 * n_scatter :])
            pl.when((pl.program_id(0) == 0) & (i == 0))(copies.begin)

        @pl.when(i == 0)
        def _():
            dk_ref[...] = jnp.zeros_like(dk_ref)
            dv_ref[...] = jnp.zeros_like(dv_ref)

        low = lax.broadcasted_iota(jnp.int32, (t, LANES), 1) < HEAD_DIM
        row, col = _attn_tile_consts(t)
        suffix = (row > col).astype(BF16)
        prefix = (row < col).astype(BF16)
        causal = col < row
        q = q_ref[...]
        dob = do_ref[...]
        zero_q = jnp.zeros_like(q)
        heads = []
        for head in range(2):
            if head == 0:
                qh, doh = jnp.where(low, q, zero_q), jnp.where(low, dob, zero_q)
            else:
                qh, doh = jnp.where(low, zero_q, q), jnp.where(low, zero_q, dob)

            def rows_of(kb):
                return pl.ds(pl.multiple_of(kb * t, t), t)

            def pass1(kbs, after, diagonal_first=False):
                z = [_scores(qh, k_ref[rows_of(kb), :]) for kb in kbs]
                sp = [_softplus(z_) for z_ in z]
                if diagonal_first:
                    sp[0] = jnp.where(causal, sp[0], 0.0)
                inside = [_triangle_sum(s_, suffix) for s_ in sp]
                for n, kb in enumerate(kbs):
                    log_sg = z[n] - sp[n]
                    a = jnp.exp(log_sg - inside[n] - after)
                    sg = jnp.exp(log_sg)
                    if diagonal_first and n == 0:
                        a = jnp.where(causal, a, 0.0)
                        sg = jnp.where(causal, sg, 0.0)
                    a_s[kb] = a
                    sg_s[kb] = sg
                    after = after + jnp.sum(sp[n], axis=-1, keepdims=True)
                return after

            def live(after):
                return jnp.min(after) < UNDERFLOW_EXIT

            after = jnp.zeros((t, 1), F32)
            after = lax.cond(i >= 1, lambda c: pass1((i, i - 1), c, True), lambda c: pass1((i,), c, True), after)
            rest = jnp.maximum(i - 1, 0)
            take_single = (rest % 2 == 1) & live(after)
            after = lax.cond(take_single, lambda c: pass1((i - 2,), c), lambda c: c, after)
            pairs = rest // 2
            pairs_done, _ = lax.while_loop(
                lambda st: (st[0] < pairs) & live(st[1]),
                lambda st: (st[0] + 1, pass1((2 * (pairs - st[0]) - 1, 2 * (pairs - st[0]) - 2), st[1])),
                (jnp.int32(0), after))
            walked = jnp.minimum(i, 1) + 1 + take_single.astype(jnp.int32) + 2 * pairs_done
            first = i - walked + 1

            def pass2(kbs, carry):
                dq, before = carry
                ks = [k_ref[rows_of(kb), :] for kb in kbs]
                a = [a_s[kb] for kb in kbs]
                g = [a_ * _dot_nt(doh, v_ref[rows_of(kb), :]) for a_, kb in zip(a, kbs)]
                for n, kb in enumerate(kbs):
                    dv_ref[rows_of(kb), :] += _dot_tn(a[n].astype(BF16), doh)
                inside = [_triangle_sum(g_, prefix) for g_ in g]
                dz = []
                for n, kb in enumerate(kbs):
                    sg = sg_s[kb]
                    dz.append((g[n] - sg * (g[n] + inside[n] + before)).astype(BF16))
                    before = before + jnp.sum(g[n], axis=-1, keepdims=True)
                for n, kb in enumerate(kbs):
                    dk_ref[rows_of(kb), :] += _dot_tn(dz[n], qh)
                for n in range(len(kbs)):
                    dq = dq + _dot(dz[n], ks[n])
                return dq, before

            carry = (jnp.zeros((t, LANES), F32), jnp.zeros((t, 1), F32))
            carry = lax.fori_loop(0, walked // 2, lambda n, c: pass2((first + 2 * n, first + 2 * n + 1), c), carry)
            carry = lax.cond(walked % 2 == 1, lambda c: pass2((i,), c), lambda c: c, carry)
            heads.append(carry[0])
        dq_ref[...] = jnp.where(low, heads[0], heads[1])
        if n_scatter:
            pl.when((pl.program_id(0) == n_pairs - 1) & (i == nq - 1))(copies.finish)

    q_spec = pl.BlockSpec((t, LANES), lambda p, i: (i, p))
    kv_spec = pl.BlockSpec((s, LANES), lambda p, i: (0, p))
    return pl.pallas_call(
        body,
        name=name,
        grid=(n_pairs, nq),
        in_specs=[q_spec, kv_spec, kv_spec, q_spec] + [ANY] * n_scatter,
        out_specs=[q_spec, kv_spec, kv_spec] + [ANY] * n_scatter,
        out_shape=[jax.ShapeDtypeStruct((s, ATTN_DIM), F32)] * 3 + _scatter_shapes(scatter),
        scratch_shapes=[pltpu.VMEM((nq, t, t), F32), pltpu.VMEM((nq, t, t), F32)] + (_scatter_scratch(n_scatter) if n_scatter else []),
        compiler_params=_params(2),
    )(qn, kn, vb, do, *scatter)


CB_BLOCK, CC_BLOCK, CU_BLOCK = 3, 4, 5


def _shift_down(h, prev_rows, n):
    row = lax.broadcasted_iota(jnp.int32, h.shape, 0)
    out = pltpu.roll(h, n, 0)
    for r in range(n):
        out = jnp.where(row == r, prev_rows[len(prev_rows) - n + r], out)
    return out


def _shift_up(h, next_rows, n):
    tm = h.shape[0]
    row = lax.broadcasted_iota(jnp.int32, h.shape, 0)
    out = pltpu.roll(h, tm - n, 0)
    for r in range(n):
        out = jnp.where(row == tm - n + r, next_rows[r], out)
    return out


def _conv_fwd(proj, conv_w, name):
    s = proj.shape[0]
    tm = TOKEN_TILE
    nb = tm // 8

    def body(cb_ref, cc_ref, cu_ref, pc_ref, pu_ref, w_ref, o_ref):
        first = pl.program_id(0) == 0
        h = cc_ref[...] * cu_ref[...]
        prev = [jnp.where(first, 0.0, pc_ref[r : r + 1, :] * pu_ref[r : r + 1, :]) for r in (6, 7)]
        y = w_ref[0:1, :] * _shift_down(h, prev, 2) + w_ref[1:2, :] * _shift_down(h, prev, 1) + w_ref[2:3, :] * h
        o_ref[...] = (cb_ref[...] * y).astype(BF16)

    def col(block):
        return pl.BlockSpec((tm, CONV_DIM), lambda i: (i, block))

    def halo(block):
        return pl.BlockSpec((8, CONV_DIM), lambda i: (jnp.maximum(i * nb - 1, 0), block))

    return pl.pallas_call(
        body,
        name=name,
        grid=(s // tm,),
        in_specs=[col(CB_BLOCK), col(CC_BLOCK), col(CU_BLOCK), halo(CC_BLOCK), halo(CU_BLOCK), pl.BlockSpec((8, CONV_DIM), lambda i: (0, 0))],
        out_specs=pl.BlockSpec((tm, CONV_DIM), lambda i: (i, 0)),
        out_shape=jax.ShapeDtypeStruct((s, CONV_DIM), BF16),
        compiler_params=_params(1),
    )(proj, proj, proj, proj, proj, conv_w)


def _conv_bwd(proj, conv_w, dconv, name):
    s = proj.shape[0]
    tm = TOKEN_TILE
    nb = tm // 8
    n_tiles = s // tm

    def body(cb_ref, cc_ref, cu_ref, dy_ref, pc_ref, pu_ref, nb_ref, ndy_ref, w_ref, dp_ref, dw_ref):
        i = pl.program_id(0)

        @pl.when(i == 0)
        def _():
            dw_ref[...] = jnp.zeros_like(dw_ref)

        first = i == 0
        last = i == n_tiles - 1
        cc, cu, cb, dy = cc_ref[...], cu_ref[...], cb_ref[...], dy_ref[...]
        h = cc * cu
        prev = [jnp.where(first, 0.0, pc_ref[r : r + 1, :] * pu_ref[r : r + 1, :]) for r in (6, 7)]
        h1 = _shift_down(h, prev, 1)
        h2 = _shift_down(h, prev, 2)
        y = w_ref[0:1, :] * h2 + w_ref[1:2, :] * h1 + w_ref[2:3, :] * h
        dyb = dy * cb
        nxt = [jnp.where(last, 0.0, ndy_ref[r : r + 1, :] * nb_ref[r : r + 1, :]) for r in (0, 1)]
        dh = w_ref[2:3, :] * dyb + w_ref[1:2, :] * _shift_up(dyb, nxt, 1) + w_ref[0:1, :] * _shift_up(dyb, nxt, 2)
        dp_ref[:, 0:CONV_DIM] = (dy * y).astype(BF16)
        dp_ref[:, CONV_DIM : 2 * CONV_DIM] = (dh * cu).astype(BF16)
        dp_ref[:, 2 * CONV_DIM : 3 * CONV_DIM] = (dh * cc).astype(BF16)
        dw_ref[0:1, :] += jnp.sum(dyb * h2, axis=0, keepdims=True)
        dw_ref[1:2, :] += jnp.sum(dyb * h1, axis=0, keepdims=True)
        dw_ref[2:3, :] += jnp.sum(dyb * h, axis=0, keepdims=True)

    def col(block):
        return pl.BlockSpec((tm, CONV_DIM), lambda i: (i, block))

    def halo_prev(block):
        return pl.BlockSpec((8, CONV_DIM), lambda i: (jnp.maximum(i * nb - 1, 0), block))

    def halo_next(block):
        return pl.BlockSpec((8, CONV_DIM), lambda i: (jnp.minimum((i + 1) * nb, s // 8 - 1), block))

    return pl.pallas_call(
        body,
        name=name,
        grid=(n_tiles,),
        in_specs=[
            col(CB_BLOCK), col(CC_BLOCK), col(CU_BLOCK), col(0),
            halo_prev(CC_BLOCK), halo_prev(CU_BLOCK), halo_next(CB_BLOCK), halo_next(0),
            pl.BlockSpec((8, CONV_DIM), lambda i: (0, 0)),
        ],
        out_specs=[pl.BlockSpec((tm, 3 * CONV_DIM), lambda i: (i, 1)), pl.BlockSpec((8, CONV_DIM), lambda i: (0, 0))],
        out_shape=[jax.ShapeDtypeStruct((s, 3 * ATTN_DIM + 3 * CONV_DIM), BF16), jax.ShapeDtypeStruct((8, CONV_DIM), F32)],
        compiler_params=_params(1),
    )(proj, proj, proj, dconv, proj, proj, proj, dconv, conv_w)


def _out_proj(x, attn, conv, w_s, layer, name):
    s, d = x.shape
    tm = TOKEN_TILE
    rows = w_s.shape[2]

    def body(x_ref, a_ref, c_ref, w_ref, o_ref):
        acc = x_ref[...]
        for j in range(N_CHIPS):
            src = a_ref if j < 2 else c_ref
            cols = slice((j % 2) * rows, (j % 2 + 1) * rows)
            acc = acc + _dot(src[:, cols], w_ref[j, 0])
        o_ref[...] = acc

    return pl.pallas_call(
        body,
        name=name,
        grid=(s // tm,),
        in_specs=[
            pl.BlockSpec((tm, d), lambda i: (i, 0)),
            pl.BlockSpec((tm, ATTN_DIM), lambda i: (i, 0)),
            pl.BlockSpec((tm, CONV_DIM), lambda i: (i, 0)),
            pl.BlockSpec((N_CHIPS, 1, rows, d), lambda i: (0, layer, 0, 0)),
        ],
        out_specs=pl.BlockSpec((tm, d), lambda i: (i, 0)),
        out_shape=jax.ShapeDtypeStruct((s, d), F32),
        compiler_params=_params(1),
    )(x, attn, conv, w_s)


def _out_proj_bwd(dx, w_s, layer, name):
    s, d = dx.shape
    tm = TOKEN_TILE
    rows = w_s.shape[2]

    def body(dx_ref, w_ref, da_ref, dc_ref, dxb_ref):
        dxb = dx_ref[...].astype(BF16)
        dxb_ref[...] = dxb
        for j in range(N_CHIPS):
            cols = slice((j % 2) * rows, (j % 2 + 1) * rows)
            part = _dot_nt(dxb, w_ref[j, 0])
            if j < 2:
                da_ref[:, cols] = part.astype(BF16)
            else:
                dc_ref[:, cols] = part

    return pl.pallas_call(
        body,
        name=name,
        grid=(s // tm,),
        in_specs=[pl.BlockSpec((tm, d), lambda i: (i, 0)), pl.BlockSpec((N_CHIPS, 1, rows, d), lambda i: (0, layer, 0, 0))],
        out_specs=[
            pl.BlockSpec((tm, ATTN_DIM), lambda i: (i, 0)),
            pl.BlockSpec((tm, CONV_DIM), lambda i: (i, 0)),
            pl.BlockSpec((tm, d), lambda i: (i, 0)),
        ],
        out_shape=[
            jax.ShapeDtypeStruct((s, ATTN_DIM), BF16),
            jax.ShapeDtypeStruct((s, CONV_DIM), F32),
            jax.ShapeDtypeStruct((s, d), BF16),
        ],
        compiler_params=_params(1),
    )(dx, w_s)


def _ffn_fwd(x, gain, wg_s, wu_s, wd_s, layer, name, gather=()):
    s, d = x.shape
    tm = TOKEN_TILE
    f = wg_s.shape[3]
    n_gather = len(gather)
    n_tiles = s // tm

    def body(*refs):
        x_ref, g_ref, wg_ref, wu_ref, wd_ref = refs[:5]
        o_ref, gate_ref, up_ref = refs[5 + n_gather : 8 + n_gather]
        h_s = refs[8 + 2 * n_gather]
        i, j = pl.program_id(0), pl.program_id(1)
        if n_gather:
            copies = _WeightGather(refs[5 : 5 + n_gather], refs[8 + n_gather : 8 + 2 * n_gather], refs[9 + 2 * n_gather :])
            pl.when((i == 0) & (j == 0))(copies.begin)
            pl.when((i == (3 * n_tiles) // 4) & (j == 0))(copies.relay)

        @pl.when(j == 0)
        def _():
            xv = x_ref[...]
            r = lax.rsqrt(jnp.mean(xv * xv, axis=-1, keepdims=True) + EPS)
            h_s[...] = (xv * r * g_ref[...]).astype(BF16)
            o_ref[...] = xv

        halves = [slice(0, tm // 2), slice(tm // 2, tm)]
        pre = [(_dot(h_s[r, :], wg_ref[0, 0]), _dot(h_s[r, :], wu_ref[0, 0])) for r in halves]
        act = [((gate / (1.0 + jnp.exp(-gate))) * up).astype(BF16) for gate, up in pre]
        for r, (gate, up) in zip(halves, pre):
            gate_ref[0, r, :] = gate
            up_ref[0, r, :] = up
        for r, a in zip(halves, act):
            o_ref[r, :] += _dot(a, wd_ref[0, 0])

        if n_gather:
            pl.when((i == n_tiles - 1) & (j == N_CHIPS - 1))(copies.finish)

    hid = pl.BlockSpec((1, tm, f), lambda i, j: (j, i, 0))
    hid_shape = jax.ShapeDtypeStruct((N_CHIPS, s, f), F32)
    return pl.pallas_call(
        body,
        name=name,
        grid=(n_tiles, N_CHIPS),
        in_specs=[
            pl.BlockSpec((tm, d), lambda i, j: (i, 0)),
            pl.BlockSpec((1, d), lambda i, j: (0, 0)),
            pl.BlockSpec((1, 1, d, f), lambda i, j: (j, layer, 0, 0)),
            pl.BlockSpec((1, 1, d, f), lambda i, j: (j, layer, 0, 0)),
            pl.BlockSpec((1, 1, f, d), lambda i, j: (j, layer, 0, 0)),
        ] + [ANY] * n_gather,
        out_specs=[pl.BlockSpec((tm, d), lambda i, j: (i, 0)), hid, hid] + [ANY] * n_gather,
        out_shape=[jax.ShapeDtypeStruct((s, d), F32), hid_shape, hid_shape]
        + [jax.ShapeDtypeStruct((N_CHIPS,) + w.shape, w.dtype) for w in gather],
        scratch_shapes=[pltpu.VMEM((tm, d), BF16)] + (_gather_scratch(n_gather) if n_gather else []),
        compiler_params=_params(2),
    )(x, gain, wg_s, wu_s, wd_s, *gather)


def _rms_bwd(xv, gain, dh):
    r = lax.rsqrt(jnp.mean(xv * xv, axis=-1, keepdims=True) + EPS)
    xhat = xv * r
    dxhat = dh * gain
    dx = r * (dxhat - xhat * jnp.mean(dxhat * xhat, axis=-1, keepdims=True))
    return dx, jnp.sum(dh * xhat, axis=0, keepdims=True)


def _ffn_bwd(x, dy, gain, gate_s, up_s, wg_s, wu_s, wd_s, layer, name, scatter=()):
    s, d = x.shape
    tm = TOKEN_TILE
    f = wg_s.shape[3]

    n_scatter = len(scatter)
    n_tiles = s // tm

    def body(*refs):
        x_ref, dy_ref, g_ref, gate_ref, up_ref, wg_ref, wu_ref, wd_ref = refs[:8]
        dx_ref, dgain_ref, h_ref, dyb_ref, dg_ref, du_ref, act_ref = refs[8 + n_scatter : 15 + n_scatter]
        acc_s = refs[15 + 2 * n_scatter]
        i, j = pl.program_id(0), pl.program_id(1)
        if n_scatter:
            copies = _ChipScatter(refs[8 : 8 + n_scatter], refs[15 + n_scatter : 15 + 2 * n_scatter], refs[16 + 2 * n_scatter :])
            pl.when((i == 0) & (j == 0))(copies.begin)

        @pl.when((i == 0) & (j == 0))
        def _():
            dgain_ref[...] = jnp.zeros_like(dgain_ref)

        @pl.when(j == 0)
        def _():
            xv = x_ref[...]
            r = lax.rsqrt(jnp.mean(xv * xv, axis=-1, keepdims=True) + EPS)
            h_ref[...] = (xv * r * g_ref[...]).astype(BF16)
            dyb_ref[...] = dy_ref[...].astype(BF16)
            acc_s[...] = jnp.zeros_like(acc_s)

        halves = [slice(0, tm // 2), slice(tm // 2, tm)]
        pre = [(gate_ref[0, r, :], up_ref[0, r, :], _dot_nt(dyb_ref[r, :], wd_ref[0, 0])) for r in halves]
        grads = []
        for r, (gate, up, dact) in zip(halves, pre):
            sig = 1.0 / (1.0 + jnp.exp(-gate))
            silu = gate * sig
            dgate = (dact * up * (sig * (1.0 + gate * (1.0 - sig)))).astype(BF16)
            dup = (dact * silu).astype(BF16)
            act_ref[0, r, :] = (silu * up).astype(BF16)
            dg_ref[0, r, :] = dgate
            du_ref[0, r, :] = dup
            grads.append((dgate, dup))
        for r, (dgate, dup) in zip(halves, grads):
            acc_s[r, :] += _dot_nt(dgate, wg_ref[0, 0]) + _dot_nt(dup, wu_ref[0, 0])

        @pl.when(j == N_CHIPS - 1)
        def _():
            dxn, dgain = _rms_bwd(x_ref[...], g_ref[...], acc_s[...])
            dx_ref[...] = dy_ref[...] + dxn
            dgain_ref[...] += dgain

        if n_scatter:
            pl.when((i == n_tiles - 1) & (j == N_CHIPS - 1))(copies.finish)

    tok = pl.BlockSpec((tm, d), lambda i, j: (i, 0))
    vec = pl.BlockSpec((1, d), lambda i, j: (0, 0))
    hid = pl.BlockSpec((1, tm, f), lambda i, j: (j, i, 0))
    hid_shape = jax.ShapeDtypeStruct((N_CHIPS, s, f), BF16)
    return pl.pallas_call(
        body,
        name=name,
        grid=(n_tiles, N_CHIPS),
        in_specs=[
            tok, tok, vec, hid, hid,
            pl.BlockSpec((1, 1, d, f), lambda i, j: (j, layer, 0, 0)),
            pl.BlockSpec((1, 1, d, f), lambda i, j: (j, layer, 0, 0)),
            pl.BlockSpec((1, 1, f, d), lambda i, j: (j, layer, 0, 0)),
        ] + [ANY] * n_scatter,
        out_specs=[tok, vec, tok, tok, hid, hid, hid] + [ANY] * n_scatter,
        out_shape=[
            jax.ShapeDtypeStruct((s, d), F32),
            jax.ShapeDtypeStruct((1, d), F32),
            jax.ShapeDtypeStruct((s, d), BF16),
            jax.ShapeDtypeStruct((s, d), BF16),
            hid_shape, hid_shape, hid_shape,
        ] + _scatter_shapes(scatter),
        scratch_shapes=[pltpu.VMEM((tm, d), F32)] + (_scatter_scratch(n_scatter) if n_scatter else []),
        compiler_params=_params(2),
    )(x, dy, gain, gate_s, up_s, wg_s, wu_s, wd_s, *scatter)


def _in_proj_bwd(x, dx_res, gain, dproj, w_s, layer, name):
    s, d = x.shape
    tm = TOKEN_TILE
    n = w_s.shape[3]

    def body(x_ref, r_ref, g_ref, dp_ref, w_ref, dx_ref, dgain_ref):
        @pl.when(pl.program_id(0) == 0)
        def _():
            dgain_ref[...] = jnp.zeros_like(dgain_ref)

        dh = _dot_nt(dp_ref[:, 0:n], w_ref[0, 0])
        for j in range(1, N_CHIPS):
            dh = dh + _dot_nt(dp_ref[:, j * n : (j + 1) * n], w_ref[j, 0])
        dxn, dgain = _rms_bwd(x_ref[...], g_ref[...], dh)
        dx_ref[...] = r_ref[...] + dxn
        dgain_ref[...] += dgain

    tok = pl.BlockSpec((tm, d), lambda i: (i, 0))
    vec = pl.BlockSpec((1, d), lambda i: (0, 0))
    return pl.pallas_call(
        body,
        name=name,
        grid=(s // tm,),
        in_specs=[tok, tok, vec, pl.BlockSpec((tm, N_CHIPS * n), lambda i: (i, 0)), pl.BlockSpec((N_CHIPS, 1, d, n), lambda i: (0, layer, 0, 0))],
        out_specs=[tok, vec],
        out_shape=[jax.ShapeDtypeStruct((s, d), F32), jax.ShapeDtypeStruct((1, d), F32)],
        compiler_params=_params(1),
    )(x, dx_res, gain, dproj, w_s)


def _loss_grad(y, target, name):
    s, d = y.shape
    tm = TOKEN_TILE

    def body(y_ref, t_ref, dy_ref, l_ref):
        @pl.when(pl.program_id(0) == 0)
        def _():
            l_ref[...] = jnp.zeros_like(l_ref)

        err = y_ref[...] - t_ref[...]
        dy_ref[...] = err / d
        l_ref[...] += jnp.sum(err * err, axis=0, keepdims=True) * (0.5 / d)

    tok = pl.BlockSpec((tm, d), lambda i: (i, 0))
    return pl.pallas_call(
        body,
        name=name,
        grid=(s // tm,),
        in_specs=[tok, tok],
        out_specs=[tok, pl.BlockSpec((1, d), lambda i: (0, 0))],
        out_shape=[jax.ShapeDtypeStruct((s, d), F32), jax.ShapeDtypeStruct((1, d), F32)],
        compiler_params=_params(1),
    )(y, target)


def _wgrad(a, b, a_spec, b_spec, n_blocks, k, n, name):
    n_tiles = a.shape[-2] // min(WGRAD_TILE, a.shape[-2])

    def body(a_ref, b_ref, o_ref):
        @pl.when(pl.program_id(1) == 0)
        def _():
            o_ref[...] = jnp.zeros_like(o_ref)

        av = a_ref[0] if len(a_ref.shape) == 3 else a_ref[...]
        bv = b_ref[0] if len(b_ref.shape) == 3 else b_ref[...]
        o_ref[0] += _dot_tn(av, bv)

    return pl.pallas_call(
        body,
        name=name,
        grid=(n_blocks, n_tiles),
        in_specs=[a_spec, b_spec],
        out_specs=pl.BlockSpec((1, k, n), lambda j, i: (j, 0, 0)),
        out_shape=jax.ShapeDtypeStruct((n_blocks, k, n), F32),
        compiler_params=_params(2),
    )(a, b)


def _mesh_position():
    return lax.axis_index("x"), lax.axis_index("y"), lax.axis_index("c")


def _other_chips(x, y):
    return [(1 - x, y), (x, 1 - y), (1 - x, 1 - y)]


def _half_rows(ref_rows, c):
    half = ref_rows // 2
    return pl.ds(c * half, half)


class _WeightGather:
    def __init__(self, ins, outs, sems):
        self.ins, self.outs = ins, outs
        send_sems, recv_sems, pass_send_sems, pass_recv_sems, self.local_sems = sems
        self.ici, self.d2d = (send_sems, recv_sems), (pass_send_sems, pass_recv_sems)
        self.x, self.y, self.c = _mesh_position()
        self.me = 2 * self.x + self.y
        self.sibling = (self.x, self.y, 1 - self.c)
        self.chips = _other_chips(self.x, self.y)

    def _copy(self, t, k, chip_index, core, to, sems, src=None):
        dst = self.outs[t].at[chip_index, :, _half_rows(self.ins[t].shape[1], core), :]
        return pltpu.make_async_remote_copy(
            src_ref=dst if src is None else src, dst_ref=dst, send_sem=sems[0].at[t, k], recv_sem=sems[1].at[t, k],
            device_id=to, device_id_type=MESH_ID,
        )

    def _own(self, t):
        return pltpu.make_async_copy(self.ins[t], self.outs[t].at[self.me], self.local_sems.at[t])

    def _sends(self):
        for t in range(len(self.ins)):
            mine = self.ins[t].at[:, _half_rows(self.ins[t].shape[1], self.c), :]
            for k, (px, py) in enumerate(self.chips):
                yield self._copy(t, k, self.me, self.c, (px, py, self.c), self.ici, src=mine)

    def _passes(self, core, sems):
        for t in range(len(self.ins)):
            for k, (px, py) in enumerate(self.chips):
                yield self._copy(t, k, 2 * px + py, core, self.sibling, sems)

    def begin(self):
        for t in range(len(self.ins)):
            self._own(t).start()
        for cp in self._sends():
            cp.start()

    def relay(self):
        for arrived, onward in zip(self._passes(self.c, self.ici), self._passes(self.c, self.d2d)):
            arrived.wait_recv()
            onward.start()

    def finish(self):
        for cp in self._passes(1 - self.c, self.d2d):
            cp.wait_recv()
        for cp in list(self._sends()) + list(self._passes(self.c, self.d2d)):
            cp.wait_send()
        for t in range(len(self.ins)):
            self._own(t).wait()


def _gather_scratch(n):
    sems = pltpu.SemaphoreType.DMA((n, N_CHIPS - 1))
    return [sems, sems, sems, sems, pltpu.SemaphoreType.DMA((n,))]


def _gather_weights(shards):
    n = len(shards)

    def body(*refs):
        gather = _WeightGather(refs[:n], refs[n : 2 * n], refs[2 * n :])
        gather.begin()
        gather.relay()
        gather.finish()

    return pl.pallas_call(
        body,
        name="gather_weights",
        in_specs=[ANY] * n,
        out_specs=[ANY] * n,
        out_shape=[jax.ShapeDtypeStruct((N_CHIPS,) + w.shape, w.dtype) for w in shards],
        scratch_shapes=_gather_scratch(n),
    )(*shards)


def _swap_halves(grads, tag):
    n = len(grads)

    def body(*refs):
        ins, outs = refs[:n], refs[n : 2 * n]
        send_sems, recv_sems = refs[2 * n :]
        x, y, c = _mesh_position()
        copies = []
        for t in range(n):
            copies.append(pltpu.make_async_remote_copy(
                src_ref=ins[t].at[:, _half_rows(ins[t].shape[1], 1 - c), :], dst_ref=outs[t],
                send_sem=send_sems.at[t], recv_sem=recv_sems.at[t], device_id=(x, y, 1 - c), device_id_type=MESH_ID,
            ))
            copies[-1].start()
        for cp in copies:
            cp.wait()

    sems = pltpu.SemaphoreType.DMA((n,))
    return pl.pallas_call(
        body,
        name=f"swap_halves_{tag}",
        in_specs=[ANY] * n,
        out_specs=[ANY] * n,
        out_shape=[jax.ShapeDtypeStruct((g.shape[0], g.shape[1] // 2, g.shape[2]), g.dtype) for g in grads],
        scratch_shapes=[sems, sems],
    )(*grads)


class _ChipScatter:
    def __init__(self, ins, outs, sems):
        self.ins, self.outs = ins, outs
        self.send_sems, self.recv_sems = sems
        self.x, self.y, self.c = _mesh_position()

    def _copies(self):
        for t in range(len(self.ins)):
            for k, (px, py) in enumerate(_other_chips(self.x, self.y)):
                yield pltpu.make_async_remote_copy(
                    src_ref=self.ins[t].at[2 * px + py], dst_ref=self.outs[t].at[k],
                    send_sem=self.send_sems.at[t, k], recv_sem=self.recv_sems.at[t, k],
                    device_id=(px, py, self.c), device_id_type=MESH_ID,
                )

    def begin(self):
        for cp in self._copies():
            cp.start()

    def finish(self):
        for cp in self._copies():
            cp.wait()


def _scatter_scratch(n):
    sems = pltpu.SemaphoreType.DMA((n, N_CHIPS - 1))
    return [sems, sems]


def _scatter_shapes(parts):
    return [jax.ShapeDtypeStruct((N_CHIPS - 1,) + p.shape[1:], p.dtype) for p in parts]


def _scatter_to_chips(parts, tag):
    n = len(parts)

    def body(*refs):
        copies = _ChipScatter(refs[:n], refs[n : 2 * n], refs[2 * n :])
        copies.begin()
        copies.finish()

    return pl.pallas_call(
        body,
        name=f"scatter_to_chips_{tag}",
        in_specs=[ANY] * n,
        out_specs=[ANY] * n,
        out_shape=_scatter_shapes(parts),
        scratch_shapes=_scatter_scratch(n),
    )(*parts)


def _join_halves(shards):
    n = len(shards)

    def body(*refs):
        outs = refs[n : 2 * n]
        send_sems, recv_sems = refs[2 * n :]
        x, y, c = _mesh_position()
        copies = []
        for t in range(n):
            mine = outs[t].at[:, _half_rows(outs[t].shape[1], c), :]
            copies.append(pltpu.make_async_remote_copy(
                src_ref=mine, dst_ref=mine, send_sem=send_sems.at[t], recv_sem=recv_sems.at[t],
                device_id=(x, y, 1 - c), device_id_type=MESH_ID,
            ))
            copies[-1].start()
        for cp in copies:
            cp.wait()

    sems = pltpu.SemaphoreType.DMA((n,))
    return pl.pallas_call(
        body,
        name="join_halves",
        in_specs=[ANY] * n,
        out_specs=[ANY] * n,
        out_shape=[jax.ShapeDtypeStruct(g.shape, g.dtype) for g in shards],
        input_output_aliases={t: t for t in range(n)},
        scratch_shapes=[sems, sems],
    )(*shards)


def _gather_small(pack):
    def body(p_ref, o_ref, send_sems, recv_sems, local_sem):
        x, y, c = _mesh_position()
        own = pltpu.make_async_copy(p_ref, o_ref.at[4 * x + 2 * y + c], local_sem)
        own.start()
        copies = []
        for k in range(1, N_DEV):
            px, py, pc = x ^ (k >> 2), y ^ ((k >> 1) & 1), c ^ (k & 1)
            send = pltpu.make_async_remote_copy(
                src_ref=p_ref, dst_ref=o_ref.at[4 * x + 2 * y + c], send_sem=send_sems.at[k - 1], recv_sem=recv_sems.at[k - 1],
                device_id=(px, py, pc), device_id_type=MESH_ID,
            )
            send.start()
            copies.append((send, 4 * px + 2 * py + pc))
        for send, peer_slot in copies:
            send.wait_send()
        for k in range(1, N_DEV):
            px, py, pc = x ^ (k >> 2), y ^ ((k >> 1) & 1), c ^ (k & 1)
            pltpu.make_async_remote_copy(
                src_ref=p_ref, dst_ref=o_ref.at[4 * px + 2 * py + pc], send_sem=send_sems.at[k - 1], recv_sem=recv_sems.at[k - 1],
                device_id=(px, py, pc), device_id_type=MESH_ID,
            ).wait_recv()
        own.wait()

    sems = pltpu.SemaphoreType.DMA((N_DEV - 1,))
    return pl.pallas_call(
        body,
        name="gather_small",
        in_specs=[VMEM_SPEC],
        out_specs=VMEM_SPEC,
        out_shape=jax.ShapeDtypeStruct((N_DEV,) + pack.shape, pack.dtype),
        scratch_shapes=[sems, sems, pltpu.SemaphoreType.DMA],
    )(pack)


def _row_tile(rows):
    for tile in (256, 128, 64, 32, 16, 8):
        if rows % tile == 0:
            return tile
    return rows


def _add_half(grad, received, half_index, name):
    slots, h, cdim = received.shape
    tile = _row_tile(h)
    per_half = h // tile

    def body(c_ref, g_ref, r_ref, o_ref, ob_ref):
        total = g_ref[...] + r_ref[...]
        o_ref[...] = total
        ob_ref[...] = total.astype(BF16)

    block = pl.BlockSpec((1, tile, cdim), lambda j, i, c: (j, i, 0))
    grid_spec = pltpu.PrefetchScalarGridSpec(
        num_scalar_prefetch=1,
        grid=(slots, per_half),
        in_specs=[pl.BlockSpec((1, tile, cdim), lambda j, i, c: (j, c[0] * per_half + i, 0)), block],
        out_specs=[block, block],
    )
    return pl.pallas_call(
        body, name=name, grid_spec=grid_spec,
        out_shape=[jax.ShapeDtypeStruct(received.shape, F32), jax.ShapeDtypeStruct(received.shape, BF16)],
        compiler_params=_params(2),
    )(half_index, grad, received)


def _add_chips(part, received, chip_index, core_index, layer, n_layers, shard, name):
    _, h, cdim = part.shape
    tile = _row_tile(h)
    per_half = h // tile

    def body(chip_ref, core_ref, p_ref, r_ref, *rest):
        o_ref = rest[-1]
        o_ref[0] = ((p_ref[0] + r_ref[0].astype(F32)) + r_ref[1].astype(F32)) + r_ref[2].astype(F32)

    in_specs = [
        pl.BlockSpec((1, tile, cdim), lambda i, chip, core: (chip[0], i, 0)),
        pl.BlockSpec((N_CHIPS - 1, tile, cdim), lambda i, chip, core: (0, i, 0)),
    ]
    operands = [chip_index, core_index, part, received]
    aliases = {}
    if shard is not None:
        in_specs.append(ANY)
        operands.append(shard)
        aliases = {4: 0}
    grid_spec = pltpu.PrefetchScalarGridSpec(
        num_scalar_prefetch=2,
        grid=(per_half,),
        in_specs=in_specs,
        out_specs=pl.BlockSpec((1, tile, cdim), lambda i, chip, core: (layer, core[0] * per_half + i, 0)),
    )
    return pl.pallas_call(
        body, name=name, grid_spec=grid_spec, out_shape=jax.ShapeDtypeStruct((n_layers, 2 * h, cdim), F32),
        input_output_aliases=aliases, compiler_params=_params(1),
    )(*operands)


def _adamw(w, g, m, v, name):
    rows, cdim = w.shape
    tile = _row_tile(rows)

    def body(w_ref, g_ref, m_ref, v_ref, d_ref, nm_ref, nv_ref):
        gv = g_ref[...]
        nm = ADAM_B1 * m_ref[...] + (1.0 - ADAM_B1) * gv
        nv = ADAM_B2 * v_ref[...] + (1.0 - ADAM_B2) * (gv * gv)
        m_hat = nm / (1.0 - ADAM_B1 ** ADAM_STEP)
        v_hat = nv / (1.0 - ADAM_B2 ** ADAM_STEP)
        d_ref[...] = -ADAM_LR * (m_hat / (jnp.sqrt(v_hat) + ADAM_EPS) + ADAM_WD * w_ref[...])
        nm_ref[...] = nm
        nv_ref[...] = nv

    spec = pl.BlockSpec((tile, cdim), lambda i: (i, 0))
    shape = jax.ShapeDtypeStruct((rows, cdim), F32)
    return pl.pallas_call(
        body, name=name, grid=(rows // tile,), in_specs=[spec] * 4, out_specs=[spec] * 3, out_shape=[shape] * 3,
        compiler_params=_params(1),
    )(w, g, m, v)


SMALL_ROWS, SMALL_COLS = 24, 1024
ROW_NORM_MIX, ROW_NORM_FFN, ROW_LOSS, ROW_Q_NORM, ROW_K_NORM, ROW_CONV = 0, 2, 4, 8, 10, 16


def _sum_small(gathered):
    def body(g_ref, o_ref, heads_ref, lanes_ref):
        total = g_ref[0]
        for dev in range(1, N_DEV):
            total = total + g_ref[dev]
        o_ref[...] = total
        heads = o_ref[8:16, 0:LANES]
        for grp in range(1, ATTN_DIM // LANES):
            heads = heads + o_ref[8:16, grp * LANES : (grp + 1) * LANES]
        heads_ref[...] = heads + pltpu.roll(heads, HEAD_DIM, 1)
        lanes_ref[...] = jnp.broadcast_to(jnp.sum(o_ref[0:8, :], axis=-1, keepdims=True), (8, LANES))

    return pl.pallas_call(
        body,
        name="sum_small",
        in_specs=[VMEM_SPEC],
        out_specs=[VMEM_SPEC] * 3,
        out_shape=[jax.ShapeDtypeStruct((SMALL_ROWS, SMALL_COLS), F32), jax.ShapeDtypeStruct((8, LANES), F32), jax.ShapeDtypeStruct((8, LANES), F32)],
    )(gathered)


def _pad_rows(a, rows):
    return jnp.pad(a, ((0, rows - a.shape[0]), (0, 0)))


def _pad_to(a, rows, cols):
    return jnp.pad(a, ((0, rows - a.shape[0]), (0, cols - a.shape[1])))


def _conv_taps(conv_s):
    return jnp.transpose(conv_s[:, 0, 0:8], (1, 0, 2)).reshape(8, -1)


class _GradExchange:
    def __init__(self, chip_index, core_index, n_layers):
        self.chip_index, self.core_index, self.n_layers = chip_index, core_index, n_layers
        self.shards = {}
        self.pending = None

    def offer(self, layer, grads):
        assert self.pending is None
        names = list(grads)
        received = _swap_halves([grads[k] for k in names], f"{'_'.join(names)}_{layer}")
        parts = [_add_half(grads[k], r, self.core_index, f"add_half_{k}_{layer}") for k, r in zip(names, received)]
        self.pending = (layer, names, [p32 for p32, _ in parts], [p16 for _, p16 in parts])

    def payload(self):
        return () if self.pending is None else tuple(self.pending[3])

    def take(self, received):
        layer, names, parts, _ = self.pending
        self.pending = None
        for k, p, r in zip(names, parts, received):
            self.shards[k] = _add_chips(
                p, r, self.chip_index, self.core_index, layer, self.n_layers, self.shards.get(k), f"add_chips_{k}_{layer}")

    def finish(self):
        if self.pending is not None:
            layer, names = self.pending[0], self.pending[1]
            self.take(_scatter_to_chips(list(self.pending[3]), f"{'_'.join(names)}_{layer}"))
        return dict(zip(BIG, _join_halves([self.shards[k] for k in BIG])))


def _local_step(x, target, norm_mix, q_norm, k_norm, norm_ffn, layer_weights, exchange=None):
    layer_weights = list(layer_weights)

    def carrying(kernel_fn, n_out, *args):
        if exchange is None or exchange.pending is None:
            return kernel_fn(*args)
        out = kernel_fn(*args, scatter=exchange.payload())
        exchange.take(out[n_out:])
        return out[:n_out]

    n_layers = norm_mix.shape[0]
    s, d = x.shape
    tw = min(WGRAD_TILE, s)
    n_in = layer_weights[0][0].shape[3]
    f = layer_weights[0][2].shape[3]
    saved = []
    for l in range(n_layers):
        win_s, wout_s, wg_s, wu_s, wd_s, conv_s = layer_weights[l]
        taps = _conv_taps(conv_s)
        q_gain = jnp.tile(q_norm[l][None, :], (1, 2))
        k_gain = jnp.tile(k_norm[l][None, :], (1, 2))
        h1, proj = _norm_matmul(x, norm_mix[l][None, :], win_s, 0, f"in_proj_{l}")
        qn, kn, vb = _qkv_prep(proj, q_gain, k_gain, f"qkv_prep_{l}")
        attn = _attn_fwd(qn, kn, vb, f"attn_fwd_{l}")
        conv = _conv_fwd(proj, taps, f"conv_fwd_{l}")
        x_mid = _out_proj(x, attn, conv, wout_s, 0, f"out_proj_{l}")
        pending = layer_weights[l + 1] if l + 1 < n_layers and isinstance(layer_weights[l + 1], list) else ()
        x_out, gate, up, *arrived = _ffn_fwd(x_mid, norm_ffn[l][None, :], wg_s, wu_s, wd_s, 0, f"ffn_fwd_{l}", gather=pending)
        if pending:
            layer_weights[l + 1] = tuple(arrived)
        saved.append(dict(x=x, h1=h1, proj=proj, qn=qn, kn=kn, vb=vb, attn=attn, conv=conv, x_mid=x_mid, q_gain=q_gain, k_gain=k_gain,
                          gate=gate, up=up, taps=taps))
        x = x_out

    dy, loss_lanes = _loss_grad(x, target, "loss_grad")
    grads = [None] * n_layers
    for l in reversed(range(n_layers)):
        sv = saved[l]
        win_s, wout_s, wg_s, wu_s, wd_s, _ = layer_weights[l]
        dx_mid, d_norm_ffn, h2, dyb, dgate, dup, act = carrying(
            _ffn_bwd, 7, sv["x_mid"], dy, norm_ffn[l][None, :], sv["gate"], sv["up"], wg_s, wu_s, wd_s, 0, f"ffn_bwd_{l}")
        tok2 = pl.BlockSpec((tw, d), lambda j, i: (i, 0))
        hid = pl.BlockSpec((1, tw, f), lambda j, i: (j, i, 0))
        d_wg = _wgrad(h2, dgate, tok2, hid, N_CHIPS, d, f, f"wgrad_gate_{l}")
        d_wu = _wgrad(h2, dup, tok2, hid, N_CHIPS, d, f, f"wgrad_up_{l}")
        d_wd = _wgrad(act, dyb, hid, tok2, N_CHIPS, f, d, f"wgrad_down_{l}")
        if exchange is not None:
            exchange.offer(l, dict(w_gate=d_wg, w_up=d_wu, w_down=d_wd))
        d_attn, d_conv, dxb = _out_proj_bwd(dx_mid, wout_s, 0, f"out_proj_bwd_{l}")
        rows_out = wout_s.shape[2]
        mix_spec_a = pl.BlockSpec((tw, rows_out), lambda j, i: (i, j))
        d_wout_a = _wgrad(sv["attn"], dxb, mix_spec_a, tok2, ATTN_DIM // rows_out, rows_out, d, f"wgrad_out_attn_{l}")
        d_wout_c = _wgrad(sv["conv"], dxb, mix_spec_a, tok2, CONV_DIM // rows_out, rows_out, d, f"wgrad_out_conv_{l}")
        d_wout = jnp.concatenate([d_wout_a, d_wout_c], axis=0)
        dq, dk, dv = carrying(_attn_bwd, 3, sv["qn"], sv["kn"], sv["vb"], d_attn, f"attn_bwd_{l}")
        dproj, d_conv_w = _conv_bwd(sv["proj"], sv["taps"], d_conv, f"conv_bwd_{l}")
        dproj, d_qg, d_kg = _qkv_prep_bwd(sv["proj"], sv["q_gain"], sv["k_gain"], dq, dk, dv, dproj, f"qkv_prep_bwd_{l}")
        d_win = _wgrad(sv["h1"], dproj, tok2, pl.BlockSpec((tw, n_in), lambda j, i: (i, j)), N_CHIPS, d, n_in, f"wgrad_in_{l}")
        if exchange is not None:
            exchange.offer(l, dict(w_in=d_win, w_out=d_wout))
        dy, d_norm_mix = _in_proj_bwd(sv["x"], dx_mid, norm_mix[l][None, :], dproj, win_s, 0, f"in_proj_bwd_{l}")
        grads[l] = dict(norm_mix=d_norm_mix, norm_ffn=d_norm_ffn, q_norm=d_qg, k_norm=d_kg, conv_w=d_conv_w,
                        w_in=d_win, w_out=d_wout, w_gate=d_wg, w_up=d_wu, w_down=d_wd)
    return loss_lanes, dy, grads


BIG = ("w_in", "w_out", "w_gate", "w_up", "w_down")


def kernel(x, norm_mix, w_in, q_norm, k_norm, conv_w, w_out, norm_ffn, w_gate, w_up, w_down, loss_target, m_norm_mix, m_w_in, m_q_norm, m_k_norm, m_conv_w, m_w_out, m_norm_ffn, m_w_gate, m_w_up, m_w_down, v_norm_mix, v_w_in, v_q_norm, v_k_norm, v_conv_w, v_w_out, v_norm_ffn, v_w_gate, v_w_up, v_w_down):
    n_layers = norm_mix.shape[0]
    weights = dict(w_in=w_in, w_out=w_out, w_gate=w_gate, w_up=w_up, w_down=w_down)
    moments_m = dict(w_in=m_w_in, w_out=m_w_out, w_gate=m_w_gate, w_up=m_w_up, w_down=m_w_down)
    moments_v = dict(w_in=v_w_in, w_out=v_w_out, w_gate=v_w_gate, w_up=v_w_up, w_down=v_w_down)
    cx, cy, cc = _mesh_position()
    chip_index = (2 * cx + cy).astype(jnp.int32).reshape(1)
    core_index = cc.astype(jnp.int32).reshape(1)

    conv_pad = jnp.pad(conv_w, ((0, 0), (0, 16 - conv_w.shape[1]), (0, 0)))

    def shards_of(layer):
        return [weights[k][layer : layer + 1].astype(BF16) for k in BIG] + [conv_pad[layer : layer + 1]]

    layer_weights = [tuple(_gather_weights(shards_of(0)))] + [shards_of(layer) for layer in range(1, n_layers)]

    exchange = _GradExchange(chip_index, core_index, n_layers)
    loss_lanes, grad_x, grads = _local_step(
        x[0], loss_target[0], norm_mix, q_norm, k_norm, norm_ffn, layer_weights, exchange)

    big_grads = exchange.finish()

    def lanes(a):
        return _pad_to(a, a.shape[0], SMALL_COLS)

    def tile_of(*groups):
        return _pad_rows(jnp.concatenate([lanes(jnp.concatenate(g, axis=0)) for g in groups], axis=0), 8)

    layers = range(n_layers)
    pack = jnp.concatenate([
        tile_of([grads[l]["norm_mix"] for l in layers], [grads[l]["norm_ffn"] for l in layers], [loss_lanes]),
        tile_of([grads[l]["q_norm"] for l in layers], [grads[l]["k_norm"] for l in layers]),
        tile_of([grads[l]["conv_w"][0:3] for l in layers]),
    ], axis=0)
    small, small_heads, small_lanes = _sum_small(_gather_small(pack))
    loss = small_lanes[ROW_LOSS, 0]
    d_model = norm_mix.shape[1]
    conv_cols = conv_w.shape[2]
    conv_all = small[ROW_CONV : ROW_CONV + 3 * n_layers, 0:CONV_DIM].reshape(n_layers, 3, CONV_DIM)
    small_grads = dict(
        norm_mix=small[ROW_NORM_MIX : ROW_NORM_MIX + n_layers, 0:d_model],
        norm_ffn=small[ROW_NORM_FFN : ROW_NORM_FFN + n_layers, 0:d_model],
        q_norm=small_heads[ROW_Q_NORM - 8 : ROW_Q_NORM - 8 + n_layers, 0:HEAD_DIM],
        k_norm=small_heads[ROW_K_NORM - 8 : ROW_K_NORM - 8 + n_layers, 0:HEAD_DIM],
        conv_w=lax.dynamic_slice_in_dim(conv_all, (2 * cx + cy) * conv_cols, conv_cols, axis=2),
    )

    out_grad, out_delta, out_m, out_v = {}, {}, {}, {}
    for k in BIG:
        shape = weights[k].shape
        view = (shape[0] * shape[1], shape[2])
        g = big_grads[k]
        delta, new_m, new_v = _adamw(weights[k].reshape(view), g.reshape(view), moments_m[k].reshape(view), moments_v[k].reshape(view), f"adamw_{k}")
        out_grad[k], out_delta[k], out_m[k], out_v[k] = g, delta.reshape(shape), new_m.reshape(shape), new_v.reshape(shape)

    small_w = dict(norm_mix=norm_mix, norm_ffn=norm_ffn, q_norm=q_norm, k_norm=k_norm, conv_w=conv_w)
    small_m = dict(norm_mix=m_norm_mix, norm_ffn=m_norm_ffn, q_norm=m_q_norm, k_norm=m_k_norm, conv_w=m_conv_w)
    small_v = dict(norm_mix=v_norm_mix, norm_ffn=v_norm_ffn, q_norm=v_q_norm, k_norm=v_k_norm, conv_w=v_conv_w)
    order = ("norm_mix", "norm_ffn", "q_norm", "k_norm", "conv_w")

    def packed(tree):
        parts2 = [_pad_to(tree[k].reshape(-1, tree[k].shape[-1]), tree[k].reshape(-1, tree[k].shape[-1]).shape[0], SMALL_COLS) for k in order]
        return _pad_rows(jnp.concatenate(parts2, axis=0), SMALL_ROWS)

    delta_p, m_p, v_p = _adamw(packed(small_w), packed(small_grads), packed(small_m), packed(small_v), "adamw_small")
    row = 0
    for k in order:
        shape = small_w[k].shape
        n_rows = 1
        for dim in shape[:-1]:
            n_rows *= dim
        cut = (slice(row, row + n_rows), slice(0, shape[-1]))
        out_grad[k] = small_grads[k]
        out_delta[k], out_m[k], out_v[k] = delta_p[cut].reshape(shape), m_p[cut].reshape(shape), v_p[cut].reshape(shape)
        row += n_rows

    names_out = ("norm_mix", "w_in", "q_norm", "k_norm", "conv_w", "w_out", "norm_ffn", "w_gate", "w_up", "w_down")
    return (loss, grad_x[None], *[out_grad[k] for k in names_out], *[out_delta[k] for k in names_out],
            *[out_m[k] for k in names_out], *[out_v[k] for k in names_out])
```

```python
import functools

import jax
import jax.numpy as jnp
from jax import lax
from jax.experimental import pallas as pl
from jax.experimental.pallas import tpu as pltpu

F32 = jnp.float32
BF16 = jnp.bfloat16

EPS = 1e-6
HEAD_DIM = 64
LANES = 128
ATTN_DIM = 512
CONV_DIM = 512
N_CHIPS = 4
N_DEV = 8
Q_SCALE = HEAD_DIM ** -0.5
ATTN_Q_TILE = 256
ATTN_TILE = 256
TOKEN_TILE = 512
WGRAD_TILE = 2048
VMEM_LIMIT = 56 * 1024 * 1024

ADAM_LR = 0.001
ADAM_B1 = 0.9
ADAM_B2 = 0.999
ADAM_EPS = 1e-08
ADAM_WD = 0.01
ADAM_STEP = 10

MESH_ID = pl.DeviceIdType.MESH
ANY = pl.BlockSpec(memory_space=pl.ANY)
VMEM_SPEC = pl.BlockSpec(memory_space=pltpu.VMEM)


def _params(n_axes):
    return pltpu.CompilerParams(dimension_semantics=("arbitrary",) * n_axes, vmem_limit_bytes=VMEM_LIMIT)


def _dot(a, b):
    return jnp.dot(a, b, preferred_element_type=F32)


def _dot_nt(a, b):
    return lax.dot_general(a, b, (((1,), (1,)), ((), ())), preferred_element_type=F32)


def _dot_tn(a, b):
    return lax.dot_general(a, b, (((0,), (0,)), ((), ())), preferred_element_type=F32)


SCORE_MAX = 80.0
UNDERFLOW_EXIT = 90.0


def _scores(q, k):
    return jnp.minimum(_dot_nt(q, k), SCORE_MAX)


def _softplus(z):
    return jnp.log(1.0 + jnp.exp(z))


def _norm_matmul(x, gain, w_s, layer, name):
    s, d = x.shape
    n_blocks, _, _, n = w_s.shape
    tm = TOKEN_TILE

    def body(x_ref, g_ref, w_ref, h_ref, o_ref):
        xv = x_ref[...]
        r = lax.rsqrt(jnp.mean(xv * xv, axis=-1, keepdims=True) + EPS)
        h = (xv * r * g_ref[...]).astype(BF16)
        h_ref[...] = h
        for j in range(n_blocks):
            o_ref[:, j * n : (j + 1) * n] = _dot(h, w_ref[j, 0])

    return pl.pallas_call(
        body,
        name=name,
        grid=(s // tm,),
        in_specs=[
            pl.BlockSpec((tm, d), lambda i: (i, 0)),
            pl.BlockSpec((1, d), lambda i: (0, 0)),
            pl.BlockSpec((n_blocks, 1, d, n), lambda i: (0, layer, 0, 0)),
        ],
        out_specs=[pl.BlockSpec((tm, d), lambda i: (i, 0)), pl.BlockSpec((tm, n_blocks * n), lambda i: (i, 0))],
        out_shape=[jax.ShapeDtypeStruct((s, d), BF16), jax.ShapeDtypeStruct((s, n_blocks * n), F32)],
        compiler_params=_params(1),
    )(x, gain, w_s)


def _head_norm(xv, gain, low):
    sq = xv * xv
    s_low = jnp.sum(jnp.where(low, sq, 0.0), axis=-1, keepdims=True)
    s_high = jnp.sum(jnp.where(low, 0.0, sq), axis=-1, keepdims=True)
    r = jnp.where(low, lax.rsqrt(s_low / HEAD_DIM + EPS), lax.rsqrt(s_high / HEAD_DIM + EPS))
    return xv * r * gain, r


def _qkv_prep(proj, q_gain, k_gain, name):
    s = proj.shape[0]
    tm = TOKEN_TILE

    def body(p_ref, qg_ref, kg_ref, q_ref, k_ref, v_ref):
        low = lax.broadcasted_iota(jnp.int32, (tm, LANES), 1) < HEAD_DIM
        for g in range(ATTN_DIM // LANES):
            cq = slice(LANES * g, LANES * (g + 1))
            ck = slice(ATTN_DIM + LANES * g, ATTN_DIM + LANES * (g + 1))
            cv = slice(2 * ATTN_DIM + LANES * g, 2 * ATTN_DIM + LANES * (g + 1))
            qn, _ = _head_norm(p_ref[:, cq], qg_ref[...], low)
            kn, _ = _head_norm(p_ref[:, ck], kg_ref[...], low)
            q_ref[:, cq] = (qn * Q_SCALE).astype(BF16)
            k_ref[:, cq] = kn.astype(BF16)
            v_ref[:, cq] = p_ref[:, cv].astype(BF16)

    out = jax.ShapeDtypeStruct((s, ATTN_DIM), BF16)
    return pl.pallas_call(
        body,
        name=name,
        grid=(s // tm,),
        in_specs=[
            pl.BlockSpec((tm, 3 * ATTN_DIM), lambda i: (i, 0)),
            pl.BlockSpec((1, LANES), lambda i: (0, 0)),
            pl.BlockSpec((1, LANES), lambda i: (0, 0)),
        ],
        out_specs=[pl.BlockSpec((tm, ATTN_DIM), lambda i: (i, 0))] * 3,
        out_shape=[out, out, out],
        compiler_params=_params(1),
    )(proj, q_gain, k_gain)


def _qkv_prep_bwd(proj, q_gain, k_gain, dq, dk, dv, dproj, name):
    s = proj.shape[0]
    tm = TOKEN_TILE

    def norm_bwd(xv, gain, dy, low):
        _, r = _head_norm(xv, gain, low)
        xhat = xv * r
        dxhat = dy * gain
        prod = dxhat * xhat
        m_low = jnp.sum(jnp.where(low, prod, 0.0), axis=-1, keepdims=True)
        m_high = jnp.sum(jnp.where(low, 0.0, prod), axis=-1, keepdims=True)
        mean = jnp.where(low, m_low, m_high) / HEAD_DIM
        return r * (dxhat - xhat * mean), jnp.sum(dy * xhat, axis=0, keepdims=True)

    def body(p_ref, qg_ref, kg_ref, dq_ref, dk_ref, dv_ref, dproj_ref, dp_ref, dqg_ref, dkg_ref):
        @pl.when(pl.program_id(0) == 0)
        def _():
            dqg_ref[...] = jnp.zeros_like(dqg_ref)
            dkg_ref[...] = jnp.zeros_like(dkg_ref)

        low = lax.broadcasted_iota(jnp.int32, (tm, LANES), 1) < HEAD_DIM
        for g in range(ATTN_DIM // LANES):
            cq = slice(LANES * g, LANES * (g + 1))
            ck = slice(ATTN_DIM + LANES * g, ATTN_DIM + LANES * (g + 1))
            cv = slice(2 * ATTN_DIM + LANES * g, 2 * ATTN_DIM + LANES * (g + 1))
            dxq, dgq = norm_bwd(p_ref[:, cq], qg_ref[...], dq_ref[:, cq] * Q_SCALE, low)
            dxk, dgk = norm_bwd(p_ref[:, ck], kg_ref[...], dk_ref[:, cq], low)
            dp_ref[:, cq] = dxq.astype(BF16)
            dp_ref[:, ck] = dxk.astype(BF16)
            dp_ref[:, cv] = dv_ref[:, cq].astype(BF16)
            dqg_ref[:, cq] += dgq
            dkg_ref[:, cq] += dgk

    grad_spec = pl.BlockSpec((tm, ATTN_DIM), lambda i: (i, 0))
    gain_spec = pl.BlockSpec((1, LANES), lambda i: (0, 0))
    sum_spec = pl.BlockSpec((1, ATTN_DIM), lambda i: (0, 0))
    return pl.pallas_call(
        body,
        name=name,
        grid=(s // tm,),
        in_specs=[pl.BlockSpec((tm, 3 * ATTN_DIM), lambda i: (i, 0)), gain_spec, gain_spec, grad_spec, grad_spec, grad_spec, ANY],
        out_specs=[pl.BlockSpec((tm, 3 * ATTN_DIM), lambda i: (i, 0)), sum_spec, sum_spec],
        out_shape=[
            jax.ShapeDtypeStruct(dproj.shape, BF16),
            jax.ShapeDtypeStruct((1, ATTN_DIM), F32),
            jax.ShapeDtypeStruct((1, ATTN_DIM), F32),
        ],
        input_output_aliases={6: 0},
        compiler_params=_params(1),
    )(proj, q_gain, k_gain, dq, dk, dv, dproj)


def _attn_tile_consts(t):
    row = lax.broadcasted_iota(jnp.int32, (t, t), 0)
    col = lax.broadcasted_iota(jnp.int32, (t, t), 1)
    return row, col


def _triangle_sum(v, triangle):
    return _dot(v.astype(BF16), triangle)


def _attn_fwd(qn, kn, vb, name, gather=()):
    s = qn.shape[0]
    t = min(ATTN_TILE, s)
    tq = min(ATTN_Q_TILE, t)
    per_key_tile = t // tq
    n_gather = len(gather)
    n_pairs, n_blocks = ATTN_DIM // LANES, s // tq

    def body(*refs):
        q_ref, k_ref, v_ref = refs[:3]
        o_ref = refs[3 + n_gather]
        if n_gather:
            copies = _WeightGather(refs[3 : 3 + n_gather], refs[4 + n_gather : 4 + 2 * n_gather], refs[4 + 2 * n_gather :])
            first = (pl.program_id(0) == 0) & (pl.program_id(1) == 0)
            pl.when(first)(copies.begin)
            pl.when((pl.program_id(0) == n_pairs - 1) & (pl.program_id(1) == 0))(copies.relay)
        i = pl.program_id(1) // per_key_tile
        low = lax.broadcasted_iota(jnp.int32, (tq, LANES), 1) < HEAD_DIM
        row, col = _attn_tile_consts(t)
        suffix = (row > col).astype(BF16)
        first_row = (pl.program_id(1) % per_key_tile) * tq
        causal = lax.broadcasted_iota(jnp.int32, (tq, t), 1) < lax.broadcasted_iota(jnp.int32, (tq, t), 0) + first_row
        q = q_ref[...]
        zero_q = jnp.zeros_like(q)
        qh = (jnp.where(low, q, zero_q), jnp.where(low, zero_q, q))

        def step(kbs, carry, diagonal_first=False):
            chains = [(head, m) for head in range(2) for m in range(len(kbs))]
            masked = [diagonal_first and m == 0 for _, m in chains]
            ks = [k_ref[pl.ds(pl.multiple_of(kb * t, t), t), :] for kb in kbs]
            vs = [v_ref[pl.ds(pl.multiple_of(kb * t, t), t), :] for kb in kbs]
            z = [_scores(qh[head], ks[kb]) for head, kb in chains]
            sp = [_softplus(zc) for zc in z]
            sp = [jnp.where(causal, s_, 0.0) if mk else s_ for s_, mk in zip(sp, masked)]
            inside = [_triangle_sum(s_, suffix) for s_ in sp]
            after = [carry[head][1] for head in range(2)]
            log_a = []
            for n, (head, kb) in enumerate(chains):
                log_a.append(z[n] - sp[n] - inside[n] - after[head])
                after[head] = after[head] + jnp.sum(sp[n], axis=-1, keepdims=True)
            a = [jnp.exp(l_) for l_ in log_a]
            a = [jnp.where(causal, a_, 0.0) if mk else a_ for a_, mk in zip(a, masked)]
            acc = [carry[head][0] for head in range(2)]
            for n, (head, kb) in enumerate(chains):
                acc[head] = acc[head] + _dot(a[n].astype(BF16), vs[kb])
            return tuple((acc[head], after[head]) for head in range(2))

        def live(c):
            return jnp.minimum(jnp.min(c[0][1]), jnp.min(c[1][1])) < UNDERFLOW_EXIT

        zero = (jnp.zeros((tq, LANES), F32), jnp.zeros((tq, 1), F32))
        carry = lax.cond(i >= 1, lambda c: step((i, i - 1), c, True), lambda c: step((i,), c, True), (zero, zero))
        rest = jnp.maximum(i - 1, 0)
        carry = lax.cond((rest % 2 == 1) & live(carry), lambda c: step((i - 2,), c), lambda c: c, carry)
        pairs = rest // 2
        _, carry = lax.while_loop(
            lambda st: (st[0] < pairs) & live(st[1]),
            lambda st: (st[0] + 1, step((2 * (pairs - st[0]) - 1, 2 * (pairs - st[0]) - 2), st[1])),
            (jnp.int32(0), carry))
        o_ref[...] = jnp.where(low, carry[0][0], carry[1][0]).astype(BF16)
        if n_gather:
            pl.when((pl.program_id(0) == n_pairs - 1) & (pl.program_id(1) == n_blocks - 1))(copies.finish)

    out = pl.pallas_call(
        body,
        name=name,
        grid=(n_pairs, n_blocks),
        in_specs=[
            pl.BlockSpec((tq, LANES), lambda p, i: (i, p)),
            pl.BlockSpec((s, LANES), lambda p, i: (0, p)),
            pl.BlockSpec((s, LANES), lambda p, i: (0, p)),
        ] + [ANY] * n_gather,
        out_specs=[pl.BlockSpec((tq, LANES), lambda p, i: (i, p))] + [ANY] * n_gather,
        out_shape=[jax.ShapeDtypeStruct((s, ATTN_DIM), BF16)] + [jax.ShapeDtypeStruct((N_CHIPS,) + w.shape, w.dtype) for w in gather],
        scratch_shapes=_gather_scratch(n_gather) if n_gather else [],
        compiler_params=_params(2),
    )(qn, kn, vb, *gather)
    return out if n_gather else out[0]


def _attn_bwd(qn, kn, vb, do, name, scatter=()):
    s = qn.shape[0]
    t = min(ATTN_TILE, s)
    nq = s // t
    n_scatter = len(scatter)
    n_pairs = ATTN_DIM // LANES

    def body(*refs):
        q_ref, k_ref, v_ref, do_ref = refs[:4]
        dq_ref, dk_ref, dv_ref = refs[4 + n_scatter : 7 + n_scatter]
        a_s, sg_s = refs[7 + 2 * n_scatter : 9 + 2 * n_scatter]
        i = pl.program_id(1)
        if n_scatter:
            copies = _ChipScatter(refs[4 : 4 + n_scatter], refs[7 + n_scatter : 7 + 2 * n_scatter], refs[9 + 2 * n_scatter :])
            pl.when((pl.program_id(0) == 0) & (i == 0))(copies.begin)

        @pl.when(i == 0)
        def _():
            dk_ref[...] = jnp.zeros_like(dk_ref)
            dv_ref[...] = jnp.zeros_like(dv_ref)

        low = lax.broadcasted_iota(jnp.int32, (t, LANES), 1) < HEAD_DIM
        row, col = _attn_tile_consts(t)
        suffix = (row > col).astype(BF16)
        prefix = (row < col).astype(BF16)
        causal = col < row
        q = q_ref[...]
        dob = do_ref[...]
        zero_q = jnp.zeros_like(q)
        heads = []
        for head in range(2):
            if head == 0:
                qh, doh = jnp.where(low, q, zero_q), jnp.where(low, dob, zero_q)
            else:
                qh, doh = jnp.where(low, zero_q, q), jnp.where(low, zero_q, dob)

            def rows_of(kb):
                return pl.ds(pl.multiple_of(kb * t, t), t)

            def pass1(kbs, after, diagonal_first=False):
                z = [_scores(qh, k_ref[rows_of(kb), :]) for kb in kbs]
                sp = [_softplus(z_) for z_ in z]
                if diagonal_first:
                    sp[0] = jnp.where(causal, sp[0], 0.0)
                inside = [_triangle_sum(s_, suffix) for s_ in sp]
                for n, kb in enumerate(kbs):
                    log_sg = z[n] - sp[n]
                    a = jnp.exp(log_sg - inside[n] - after)
                    sg = jnp.exp(log_sg)
                    if diagonal_first and n == 0:
                        a = jnp.where(causal, a, 0.0)
                        sg = jnp.where(causal, sg, 0.0)
                    a_s[kb] = a
                    sg_s[kb] = sg
                    after = after + jnp.sum(sp[n], axis=-1, keepdims=True)
                return after

            def live(after):
                return jnp.min(after) < UNDERFLOW_EXIT

            after = jnp.zeros((t, 1), F32)
            after = lax.cond(i >= 1, lambda c: pass1((i, i - 1), c, True), lambda c: pass1((i,), c, True), after)
            rest = jnp.maximum(i - 1, 0)
            take_single = (rest % 2 == 1) & live(after)
            after = lax.cond(take_single, lambda c: pass1((i - 2,), c), lambda c: c, after)
            pairs = rest // 2
            pairs_done, _ = lax.while_loop(
                lambda st: (st[0] < pairs) & live(st[1]),
                lambda st: (st[0] + 1, pass1((2 * (pairs - st[0]) - 1, 2 * (pairs - st[0]) - 2), st[1])),
                (jnp.int32(0), after))
            walked = jnp.minimum(i, 1) + 1 + take_single.astype(jnp.int32) + 2 * pairs_done
            first = i - walked + 1

            def pass2(kbs, carry):
                dq, before = carry
                ks = [k_ref[rows_of(kb), :] for kb in kbs]
                a = [a_s[kb] for kb in kbs]
                g = [a_ * _dot_nt(doh, v_ref[rows_of(kb), :]) for a_, kb in zip(a, kbs)]
                for n, kb in enumerate(kbs):
                    dv_ref[rows_of(kb), :] += _dot_tn(a[n].astype(BF16), doh)
                inside = [_triangle_sum(g_, prefix) for g_ in g]
                dz = []
                for n, kb in enumerate(kbs):
                    sg = sg_s[kb]
                    dz.append((g[n] - sg * (g[n] + inside[n] + before)).astype(BF16))
                    before = before + jnp.sum(g[n], axis=-1, keepdims=True)
                for n, kb in enumerate(kbs):
                    dk_ref[rows_of(kb), :] += _dot_tn(dz[n], qh)
                for n in range(len(kbs)):
                    dq = dq + _dot(dz[n], ks[n])
                return dq, before

            carry = (jnp.zeros((t, LANES), F32), jnp.zeros((t, 1), F32))
            carry = lax.fori_loop(0, walked // 2, lambda n, c: pass2((first + 2 * n, first + 2 * n + 1), c), carry)
            carry = lax.cond(walked % 2 == 1, lambda c: pass2((i,), c), lambda c: c, carry)
            heads.append(carry[0])
        dq_ref[...] = jnp.where(low, heads[0], heads[1])
        if n_scatter:
            pl.when((pl.program_id(0) == n_pairs - 1) & (i == nq - 1))(copies.finish)

    q_spec = pl.BlockSpec((t, LANES), lambda p, i: (i, p))
    kv_spec = pl.BlockSpec((s, LANES), lambda p, i: (0, p))
    return pl.pallas_call(
        body,
        name=name,
        grid=(n_pairs, nq),
        in_specs=[q_spec, kv_spec, kv_spec, q_spec] + [ANY] * n_scatter,
        out_specs=[q_spec, kv_spec, kv_spec] + [ANY] * n_scatter,
        out_shape=[jax.ShapeDtypeStruct((s, ATTN_DIM), F32)] * 3 + _scatter_shapes(scatter),
        scratch_shapes=[pltpu.VMEM((nq, t, t), F32), pltpu.VMEM((nq, t, t), F32)] + (_scatter_scratch(n_scatter) if n_scatter else []),
        compiler_params=_params(2),
    )(qn, kn, vb, do, *scatter)


CB_BLOCK, CC_BLOCK, CU_BLOCK = 3, 4, 5


def _shift_down(h, prev_rows, n):
    row = lax.broadcasted_iota(jnp.int32, h.shape, 0)
    out = pltpu.roll(h, n, 0)
    for r in range(n):
        out = jnp.where(row == r, prev_rows[len(prev_rows) - n + r], out)
    return out


def _shift_up(h, next_rows, n):
    tm = h.shape[0]
    row = lax.broadcasted_iota(jnp.int32, h.shape, 0)
    out = pltpu.roll(h, tm - n, 0)
    for r in range(n):
        out = jnp.where(row == tm - n + r, next_rows[r], out)
    return out


def _conv_fwd(proj, conv_w, name):
    s = proj.shape[0]
    tm = TOKEN_TILE
    nb = tm // 8

    def body(cb_ref, cc_ref, cu_ref, pc_ref, pu_ref, w_ref, o_ref):
        first = pl.program_id(0) == 0
        h = cc_ref[...] * cu_ref[...]
        prev = [jnp.where(first, 0.0, pc_ref[r : r + 1, :] * pu_ref[r : r + 1, :]) for r in (6, 7)]
        y = w_ref[0:1, :] * _shift_down(h, prev, 2) + w_ref[1:2, :] * _shift_down(h, prev, 1) + w_ref[2:3, :] * h
        o_ref[...] = (cb_ref[...] * y).astype(BF16)

    def col(block):
        return pl.BlockSpec((tm, CONV_DIM), lambda i: (i, block))

    def halo(block):
        return pl.BlockSpec((8, CONV_DIM), lambda i: (jnp.maximum(i * nb - 1, 0), block))

    return pl.pallas_call(
        body,
        name=name,
        grid=(s // tm,),
        in_specs=[col(CB_BLOCK), col(CC_BLOCK), col(CU_BLOCK), halo(CC_BLOCK), halo(CU_BLOCK), pl.BlockSpec((8, CONV_DIM), lambda i: (0, 0))],
        out_specs=pl.BlockSpec((tm, CONV_DIM), lambda i: (i, 0)),
        out_shape=jax.ShapeDtypeStruct((s, CONV_DIM), BF16),
        compiler_params=_params(1),
    )(proj, proj, proj, proj, proj, conv_w)


def _conv_bwd(proj, conv_w, dconv, name):
    s = proj.shape[0]
    tm = TOKEN_TILE
    nb = tm // 8
    n_tiles = s // tm

    def body(cb_ref, cc_ref, cu_ref, dy_ref, pc_ref, pu_ref, nb_ref, ndy_ref, w_ref, dp_ref, dw_ref):
        i = pl.program_id(0)

        @pl.when(i == 0)
        def _():
            dw_ref[...] = jnp.zeros_like(dw_ref)

        first = i == 0
        last = i == n_tiles - 1
        cc, cu, cb, dy = cc_ref[...], cu_ref[...], cb_ref[...], dy_ref[...]
        h = cc * cu
        prev = [jnp.where(first, 0.0, pc_ref[r : r + 1, :] * pu_ref[r : r + 1, :]) for r in (6, 7)]
        h1 = _shift_down(h, prev, 1)
        h2 = _shift_down(h, prev, 2)
        y = w_ref[0:1, :] * h2 + w_ref[1:2, :] * h1 + w_ref[2:3, :] * h
        dyb = dy * cb
        nxt = [jnp.where(last, 0.0, ndy_ref[r : r + 1, :] * nb_ref[r : r + 1, :]) for r in (0, 1)]
        dh = w_ref[2:3, :] * dyb + w_ref[1:2, :] * _shift_up(dyb, nxt, 1) + w_ref[0:1, :] * _shift_up(dyb, nxt, 2)
        dp_ref[:, 0:CONV_DIM] = (dy * y).astype(BF16)
        dp_ref[:, CONV_DIM : 2 * CONV_DIM] = (dh * cu).astype(BF16)
        dp_ref[:, 2 * CONV_DIM : 3 * CONV_DIM] = (dh * cc).astype(BF16)
        dw_ref[0:1, :] += jnp.sum(dyb * h2, axis=0, keepdims=True)
        dw_ref[1:2, :] += jnp.sum(dyb * h1, axis=0, keepdims=True)
        dw_ref[2:3, :] += jnp.sum(dyb * h, axis=0, keepdims=True)

    def col(block):
        return pl.BlockSpec((tm, CONV_DIM), lambda i: (i, block))

    def halo_prev(block):
        return pl.BlockSpec((8, CONV_DIM), lambda i: (jnp.maximum(i * nb - 1, 0), block))

    def halo_next(block):
        return pl.BlockSpec((8, CONV_DIM), lambda i: (jnp.minimum((i + 1) * nb, s // 8 - 1), block))

    return pl.pallas_call(
        body,
        name=name,
        grid=(n_tiles,),
        in_specs=[
            col(CB_BLOCK), col(CC_BLOCK), col(CU_BLOCK), col(0),
            halo_prev(CC_BLOCK), halo_prev(CU_BLOCK), halo_next(CB_BLOCK), halo_next(0),
            pl.BlockSpec((8, CONV_DIM), lambda i: (0, 0)),
        ],
        out_specs=[pl.BlockSpec((tm, 3 * CONV_DIM), lambda i: (i, 1)), pl.BlockSpec((8, CONV_DIM), lambda i: (0, 0))],
        out_shape=[jax.ShapeDtypeStruct((s, 3 * ATTN_DIM + 3 * CONV_DIM), BF16), jax.ShapeDtypeStruct((8, CONV_DIM), F32)],
        compiler_params=_params(1),
    )(proj, proj, proj, dconv, proj, proj, proj, dconv, conv_w)


def _out_proj(x, attn, conv, w_s, layer, name):
    s, d = x.shape
    tm = TOKEN_TILE
    rows = w_s.shape[2]

    def body(x_ref, a_ref, c_ref, w_ref, o_ref):
        acc = x_ref[...]
        for j in range(N_CHIPS):
            src = a_ref if j < 2 else c_ref
            cols = slice((j % 2) * rows, (j % 2 + 1) * rows)
            acc = acc + _dot(src[:, cols], w_ref[j, 0])
        o_ref[...] = acc

    return pl.pallas_call(
        body,
        name=name,
        grid=(s // tm,),
        in_specs=[
            pl.BlockSpec((tm, d), lambda i: (i, 0)),
            pl.BlockSpec((tm, ATTN_DIM), lambda i: (i, 0)),
            pl.BlockSpec((tm, CONV_DIM), lambda i: (i, 0)),
            pl.BlockSpec((N_CHIPS, 1, rows, d), lambda i: (0, layer, 0, 0)),
        ],
        out_specs=pl.BlockSpec((tm, d), lambda i: (i, 0)),
        out_shape=jax.ShapeDtypeStruct((s, d), F32),
        compiler_params=_params(1),
    )(x, attn, conv, w_s)


def _out_proj_bwd(dx, w_s, layer, name):
    s, d = dx.shape
    tm = TOKEN_TILE
    rows = w_s.shape[2]

    def body(dx_ref, w_ref, da_ref, dc_ref, dxb_ref):
        dxb = dx_ref[...].astype(BF16)
        dxb_ref[...] = dxb
        for j in range(N_CHIPS):
            cols = slice((j % 2) * rows, (j % 2 + 1) * rows)
            part = _dot_nt(dxb, w_ref[j, 0])
            if j < 2:
                da_ref[:, cols] = part.astype(BF16)
            else:
                dc_ref[:, cols] = part

    return pl.pallas_call(
        body,
        name=name,
        grid=(s // tm,),
        in_specs=[pl.BlockSpec((tm, d), lambda i: (i, 0)), pl.BlockSpec((N_CHIPS, 1, rows, d), lambda i: (0, layer, 0, 0))],
        out_specs=[
            pl.BlockSpec((tm, ATTN_DIM), lambda i: (i, 0)),
            pl.BlockSpec((tm, CONV_DIM), lambda i: (i, 0)),
            pl.BlockSpec((tm, d), lambda i: (i, 0)),
        ],
        out_shape=[
            jax.ShapeDtypeStruct((s, ATTN_DIM), BF16),
            jax.ShapeDtypeStruct((s, CONV_DIM), F32),
            jax.ShapeDtypeStruct((s, d), BF16),
        ],
        compiler_params=_params(1),
    )(dx, w_s)


def _ffn_fwd(x, gain, wg_s, wu_s, wd_s, layer, name, gather=()):
    s, d = x.shape
    tm = TOKEN_TILE
    f = wg_s.shape[3]
    n_gather = len(gather)
    n_tiles = s // tm

    def body(*refs):
        x_ref, g_ref, wg_ref, wu_ref, wd_ref = refs[:5]
        o_ref, gate_ref, up_ref = refs[5 + n_gather : 8 + n_gather]
        h_s = refs[8 + 2 * n_gather]
        i, j = pl.program_id(0), pl.program_id(1)
        if n_gather:
            copies = _WeightGather(refs[5 : 5 + n_gather], refs[8 + n_gather : 8 + 2 * n_gather], refs[9 + 2 * n_gather :])
            pl.when((i == 0) & (j == 0))(copies.begin)
            pl.when((i == (3 * n_tiles) // 4) & (j == 0))(copies.relay)

        @pl.when(j == 0)
        def _():
            xv = x_ref[...]
            r = lax.rsqrt(jnp.mean(xv * xv, axis=-1, keepdims=True) + EPS)
            h_s[...] = (xv * r * g_ref[...]).astype(BF16)
            o_ref[...] = xv

        halves = [slice(0, tm // 2), slice(tm // 2, tm)]
        pre = [(_dot(h_s[r, :], wg_ref[0, 0]), _dot(h_s[r, :], wu_ref[0, 0])) for r in halves]
        act = [((gate / (1.0 + jnp.exp(-gate))) * up).astype(BF16) for gate, up in pre]
        for r, (gate, up) in zip(halves, pre):
            gate_ref[0, r, :] = gate
            up_ref[0, r, :] = up
        for r, a in zip(halves, act):
            o_ref[r, :] += _dot(a, wd_ref[0, 0])

        if n_gather:
            pl.when((i == n_tiles - 1) & (j == N_CHIPS - 1))(copies.finish)

    hid = pl.BlockSpec((1, tm, f), lambda i, j: (j, i, 0))
    hid_shape = jax.ShapeDtypeStruct((N_CHIPS, s, f), F32)
    return pl.pallas_call(
        body,
        name=name,
        grid=(n_tiles, N_CHIPS),
        in_specs=[
            pl.BlockSpec((tm, d), lambda i, j: (i, 0)),
            pl.BlockSpec((1, d), lambda i, j: (0, 0)),
            pl.BlockSpec((1, 1, d, f), lambda i, j: (j, layer, 0, 0)),
            pl.BlockSpec((1, 1, d, f), lambda i, j: (j, layer, 0, 0)),
            pl.BlockSpec((1, 1, f, d), lambda i, j: (j, layer, 0, 0)),
        ] + [ANY] * n_gather,
        out_specs=[pl.BlockSpec((tm, d), lambda i, j: (i, 0)), hid, hid] + [ANY] * n_gather,
        out_shape=[jax.ShapeDtypeStruct((s, d), F32), hid_shape, hid_shape]
        + [jax.ShapeDtypeStruct((N_CHIPS,) + w.shape, w.dtype) for w in gather],
        scratch_shapes=[pltpu.VMEM((tm, d), BF16)] + (_gather_scratch(n_gather) if n_gather else []),
        compiler_params=_params(2),
    )(x, gain, wg_s, wu_s, wd_s, *gather)


def _rms_bwd(xv, gain, dh):
    r = lax.rsqrt(jnp.mean(xv * xv, axis=-1, keepdims=True) + EPS)
    xhat = xv * r
    dxhat = dh * gain
    dx = r * (dxhat - xhat * jnp.mean(dxhat * xhat, axis=-1, keepdims=True))
    return dx, jnp.sum(dh * xhat, axis=0, keepdims=True)


def _ffn_bwd(x, dy, gain, gate_s, up_s, wg_s, wu_s, wd_s, layer, name, scatter=()):
    s, d = x.shape
    tm = TOKEN_TILE
    f = wg_s.shape[3]

    n_scatter = len(scatter)
    n_tiles = s // tm

    def body(*refs):
        x_ref, dy_ref, g_ref, gate_ref, up_ref, wg_ref, wu_ref, wd_ref = refs[:8]
        dx_ref, dgain_ref, h_ref, dyb_ref, dg_ref, du_ref, act_ref = refs[8 + n_scatter : 15 + n_scatter]
        acc_s = refs[15 + 2 * n_scatter]
        i, j = pl.program_id(0), pl.program_id(1)
        if n_scatter:
            copies = _ChipScatter(refs[8 : 8 + n_scatter], refs[15 + n_scatter : 15 + 2 * n_scatter], refs[16 + 2 * n_scatter :])
            pl.when((i == 0) & (j == 0))(copies.begin)

        @pl.when((i == 0) & (j == 0))
        def _():
            dgain_ref[...] = jnp.zeros_like(dgain_ref)

        @pl.when(j == 0)
        def _():
            xv = x_ref[...]
            r = lax.rsqrt(jnp.mean(xv * xv, axis=-1, keepdims=True) + EPS)
            h_ref[...] = (xv * r * g_ref[...]).astype(BF16)
            dyb_ref[...] = dy_ref[...].astype(BF16)
            acc_s[...] = jnp.zeros_like(acc_s)

        halves = [slice(0, tm // 2), slice(tm // 2, tm)]
        pre = [(gate_ref[0, r, :], up_ref[0, r, :], _dot_nt(dyb_ref[r, :], wd_ref[0, 0])) for r in halves]
        grads = []
        for r, (gate, up, dact) in zip(halves, pre):
            sig = 1.0 / (1.0 + jnp.exp(-gate))
            silu = gate * sig
            dgate = (dact * up * (sig * (1.0 + gate * (1.0 - sig)))).astype(BF16)
            dup = (dact * silu).astype(BF16)
            act_ref[0, r, :] = (silu * up).astype(BF16)
            dg_ref[0, r, :] = dgate
            du_ref[0, r, :] = dup
            grads.append((dgate, dup))
        for r, (dgate, dup) in zip(halves, grads):
            acc_s[r, :] += _dot_nt(dgate, wg_ref[0, 0]) + _dot_nt(dup, wu_ref[0, 0])

        @pl.when(j == N_CHIPS - 1)
        def _():
            dxn, dgain = _rms_bwd(x_ref[...], g_ref[...], acc_s[...])
            dx_ref[...] = dy_ref[...] + dxn
            dgain_ref[...] += dgain

        if n_scatter:
            pl.when((i == n_tiles - 1) & (j == N_CHIPS - 1))(copies.finish)

    tok = pl.BlockSpec((tm, d), lambda i, j: (i, 0))
    vec = pl.BlockSpec((1, d), lambda i, j: (0, 0))
    hid = pl.BlockSpec((1, tm, f), lambda i, j: (j, i, 0))
    hid_shape = jax.ShapeDtypeStruct((N_CHIPS, s, f), BF16)
    return pl.pallas_call(
        body,
        name=name,
        grid=(n_tiles, N_CHIPS),
        in_specs=[
            tok, tok, vec, hid, hid,
            pl.BlockSpec((1, 1, d, f), lambda i, j: (j, layer, 0, 0)),
            pl.BlockSpec((1, 1, d, f), lambda i, j: (j, layer, 0, 0)),
            pl.BlockSpec((1, 1, f, d), lambda i, j: (j, layer, 0, 0)),
        ] + [ANY] * n_scatter,
        out_specs=[tok, vec, tok, tok, hid, hid, hid] + [ANY] * n_scatter,
        out_shape=[
            jax.ShapeDtypeStruct((s, d), F32),
            jax.ShapeDtypeStruct((1, d), F32),
            jax.ShapeDtypeStruct((s, d), BF16),
            jax.ShapeDtypeStruct((s, d), BF16),
            hid_shape, hid_shape, hid_shape,
        ] + _scatter_shapes(scatter),
        scratch_shapes=[pltpu.VMEM((tm, d), F32)] + (_scatter_scratch(n_scatter) if n_scatter else []),
        compiler_params=_params(2),
    )(x, dy, gain, gate_s, up_s, wg_s, wu_s, wd_s, *scatter)


def _in_proj_bwd(x, dx_res, gain, dproj, w_s, layer, name):
    s, d = x.shape
    tm = TOKEN_TILE
    n = w_s.shape[3]

    def body(x_ref, r_ref, g_ref, dp_ref, w_ref, dx_ref, dgain_ref):
        @pl.when(pl.program_id(0) == 0)
        def _():
            dgain_ref[...] = jnp.zeros_like(dgain_ref)

        dh = _dot_nt(dp_ref[:, 0:n], w_ref[0, 0])
        for j in range(1, N_CHIPS):
            dh = dh + _dot_nt(dp_ref[:, j * n : (j + 1) * n], w_ref[j, 0])
        dxn, dgain = _rms_bwd(x_ref[...], g_ref[...], dh)
        dx_ref[...] = r_ref[...] + dxn
        dgain_ref[...] += dgain

    tok = pl.BlockSpec((tm, d), lambda i: (i, 0))
    vec = pl.BlockSpec((1, d), lambda i: (0, 0))
    return pl.pallas_call(
        body,
        name=name,
        grid=(s // tm,),
        in_specs=[tok, tok, vec, pl.BlockSpec((tm, N_CHIPS * n), lambda i: (i, 0)), pl.BlockSpec((N_CHIPS, 1, d, n), lambda i: (0, layer, 0, 0))],
        out_specs=[tok, vec],
        out_shape=[jax.ShapeDtypeStruct((s, d), F32), jax.ShapeDtypeStruct((1, d), F32)],
        compiler_params=_params(1),
    )(x, dx_res, gain, dproj, w_s)


def _loss_grad(y, target, name):
    s, d = y.shape
    tm = TOKEN_TILE

    def body(y_ref, t_ref, dy_ref, l_ref):
        @pl.when(pl.program_id(0) == 0)
        def _():
            l_ref[...] = jnp.zeros_like(l_ref)

        err = y_ref[...] - t_ref[...]
        dy_ref[...] = err / d
        l_ref[...] += jnp.sum(err * err, axis=0, keepdims=True) * (0.5 / d)

    tok = pl.BlockSpec((tm, d), lambda i: (i, 0))
    return pl.pallas_call(
        body,
        name=name,
        grid=(s // tm,),
        in_specs=[tok, tok],
        out_specs=[tok, pl.BlockSpec((1, d), lambda i: (0, 0))],
        out_shape=[jax.ShapeDtypeStruct((s, d), F32), jax.ShapeDtypeStruct((1, d), F32)],
        compiler_params=_params(1),
    )(y, target)


def _wgrad(a, b, a_spec, b_spec, n_blocks, k, n, name):
    n_tiles = a.shape[-2] // min(WGRAD_TILE, a.shape[-2])

    def body(a_ref, b_ref, o_ref):
        @pl.when(pl.program_id(1) == 0)
        def _():
            o_ref[...] = jnp.zeros_like(o_ref)

        av = a_ref[0] if len(a_ref.shape) == 3 else a_ref[...]
        bv = b_ref[0] if len(b_ref.shape) == 3 else b_ref[...]
        o_ref[0] += _dot_tn(av, bv)

    return pl.pallas_call(
        body,
        name=name,
        grid=(n_blocks, n_tiles),
        in_specs=[a_spec, b_spec],
        out_specs=pl.BlockSpec((1, k, n), lambda j, i: (j, 0, 0)),
        out_shape=jax.ShapeDtypeStruct((n_blocks, k, n), F32),
        compiler_params=_params(2),
    )(a, b)


def _mesh_position():
    return lax.axis_index("x"), lax.axis_index("y"), lax.axis_index("c")


def _other_chips(x, y):
    return [(1 - x, y), (x, 1 - y), (1 - x, 1 - y)]


def _half_rows(ref_rows, c):
    half = ref_rows // 2
    return pl.ds(c * half, half)


class _WeightGather:
    def __init__(self, ins, outs, sems):
        self.ins, self.outs = ins, outs
        send_sems, recv_sems, pass_send_sems, pass_recv_sems, self.local_sems = sems
        self.ici, self.d2d = (send_sems, recv_sems), (pass_send_sems, pass_recv_sems)
        self.x, self.y, self.c = _mesh_position()
        self.me = 2 * self.x + self.y
        self.sibling = (self.x, self.y, 1 - self.c)
        self.chips = _other_chips(self.x, self.y)

    def _copy(self, t, k, chip_index, core, to, sems, src=None):
        dst = self.outs[t].at[chip_index, :, _half_rows(self.ins[t].shape[1], core), :]
        return pltpu.make_async_remote_copy(
            src_ref=dst if src is None else src, dst_ref=dst, send_sem=sems[0].at[t, k], recv_sem=sems[1].at[t, k],
            device_id=to, device_id_type=MESH_ID,
        )

    def _own(self, t):
        return pltpu.make_async_copy(self.ins[t], self.outs[t].at[self.me], self.local_sems.at[t])

    def _sends(self):
        for t in range(len(self.ins)):
            mine = self.ins[t].at[:, _half_rows(self.ins[t].shape[1], self.c), :]
            for k, (px, py) in enumerate(self.chips):
                yield self._copy(t, k, self.me, self.c, (px, py, self.c), self.ici, src=mine)

    def _passes(self, core, sems):
        for t in range(len(self.ins)):
            for k, (px, py) in enumerate(self.chips):
                yield self._copy(t, k, 2 * px + py, core, self.sibling, sems)

    def begin(self):
        for t in range(len(self.ins)):
            self._own(t).start()
        for cp in self._sends():
            cp.start()

    def relay(self):
        for arrived, onward in zip(self._passes(self.c, self.ici), self._passes(self.c, self.d2d)):
            arrived.wait_recv()
            onward.start()

    def finish(self):
        for cp in self._passes(1 - self.c, self.d2d):
            cp.wait_recv()
        for cp in list(self._sends()) + list(self._passes(self.c, self.d2d)):
            cp.wait_send()
        for t in range(len(self.ins)):
            self._own(t).wait()


def _gather_scratch(n):
    sems = pltpu.SemaphoreType.DMA((n, N_CHIPS - 1))
    return [sems, sems, sems, sems, pltpu.SemaphoreType.DMA((n,))]


def _gather_weights(shards):
    n = len(shards)

    def body(*refs):
        gather = _WeightGather(refs[:n], refs[n : 2 * n], refs[2 * n :])
        gather.begin()
        gather.relay()
        gather.finish()

    return pl.pallas_call(
        body,
        name="gather_weights",
        in_specs=[ANY] * n,
        out_specs=[ANY] * n,
        out_shape=[jax.ShapeDtypeStruct((N_CHIPS,) + w.shape, w.dtype) for w in shards],
        scratch_shapes=_gather_scratch(n),
    )(*shards)


def _swap_halves(grads, tag):
    n = len(grads)

    def body(*refs):
        ins, outs = refs[:n], refs[n : 2 * n]
        send_sems, recv_sems = refs[2 * n :]
        x, y, c = _mesh_position()
        copies = []
        for t in range(n):
            copies.append(pltpu.make_async_remote_copy(
                src_ref=ins[t].at[:, _half_rows(ins[t].shape[1], 1 - c), :], dst_ref=outs[t],
                send_sem=send_sems.at[t], recv_sem=recv_sems.at[t], device_id=(x, y, 1 - c), device_id_type=MESH_ID,
            ))
            copies[-1].start()
        for cp in copies:
            cp.wait()

    sems = pltpu.SemaphoreType.DMA((n,))
    return pl.pallas_call(
        body,
        name=f"swap_halves_{tag}",
        in_specs=[ANY] * n,
        out_specs=[ANY] * n,
        out_shape=[jax.ShapeDtypeStruct((g.shape[0], g.shape[1] // 2, g.shape[2]), g.dtype) for g in grads],
        scratch_shapes=[sems, sems],
    )(*grads)


class _ChipScatter:
    def __init__(self, ins, outs, sems):
        self.ins, self.outs = ins, outs
        self.send_sems, self.recv_sems = sems
        self.x, self.y, self.c = _mesh_position()

    def _copies(self):
        for t in range(len(self.ins)):
            for k, (px, py) in enumerate(_other_chips(self.x, self.y)):
                yield pltpu.make_async_remote_copy(
                    src_ref=self.ins[t].at[2 * px + py], dst_ref=self.outs[t].at[k],
                    send_sem=self.send_sems.at[t, k], recv_sem=self.recv_sems.at[t, k],
                    device_id=(px, py, self.c), device_id_type=MESH_ID,
                )

    def begin(self):
        for cp in self._copies():
            cp.start()

    def finish(self):
        for cp in self._copies():
            cp.wait()


def _scatter_scratch(n):
    sems = pltpu.SemaphoreType.DMA((n, N_CHIPS - 1))
    return [sems, sems]


def _scatter_shapes(parts):
    return [jax.ShapeDtypeStruct((N_CHIPS - 1,) + p.shape[1:], p.dtype) for p in parts]


def _scatter_to_chips(parts, tag):
    n = len(parts)

    def body(*refs):
        copies = _ChipScatter(refs[:n], refs[n : 2 * n], refs[2 * n :])
        copies.begin()
        copies.finish()

    return pl.pallas_call(
        body,
        name=f"scatter_to_chips_{tag}",
        in_specs=[ANY] * n,
        out_specs=[ANY] * n,
        out_shape=_scatter_shapes(parts),
        scratch_shapes=_scatter_scratch(n),
    )(*parts)


def _join_halves(shards):
    n = len(shards)

    def body(*refs):
        outs = refs[n : 2 * n]
        send_sems, recv_sems = refs[2 * n :]
        x, y, c = _mesh_position()
        copies = []
        for t in range(n):
            mine = outs[t].at[:, _half_rows(outs[t].shape[1], c), :]
            copies.append(pltpu.make_async_remote_copy(
                src_ref=mine, dst_ref=mine, send_sem=send_sems.at[t], recv_sem=recv_sems.at[t],
                device_id=(x, y, 1 - c), device_id_type=MESH_ID,
            ))
            copies[-1].start()
        for cp in copies:
            cp.wait()

    sems = pltpu.SemaphoreType.DMA((n,))
    return pl.pallas_call(
        body,
        name="join_halves",
        in_specs=[ANY] * n,
        out_specs=[ANY] * n,
        out_shape=[jax.ShapeDtypeStruct(g.shape, g.dtype) for g in shards],
        input_output_aliases={t: t for t in range(n)},
        scratch_shapes=[sems, sems],
    )(*shards)


def _gather_small(pack):
    def body(p_ref, o_ref, send_sems, recv_sems, local_sem):
        x, y, c = _mesh_position()
        own = pltpu.make_async_copy(p_ref, o_ref.at[4 * x + 2 * y + c], local_sem)
        own.start()
        copies = []
        for k in range(1, N_DEV):
            px, py, pc = x ^ (k >> 2), y ^ ((k >> 1) & 1), c ^ (k & 1)
            send = pltpu.make_async_remote_copy(
                src_ref=p_ref, dst_ref=o_ref.at[4 * x + 2 * y + c], send_sem=send_sems.at[k - 1], recv_sem=recv_sems.at[k - 1],
                device_id=(px, py, pc), device_id_type=MESH_ID,
            )
            send.start()
            copies.append((send, 4 * px + 2 * py + pc))
        for send, peer_slot in copies:
            send.wait_send()
        for k in range(1, N_DEV):
            px, py, pc = x ^ (k >> 2), y ^ ((k >> 1) & 1), c ^ (k & 1)
            pltpu.make_async_remote_copy(
                src_ref=p_ref, dst_ref=o_ref.at[4 * px + 2 * py + pc], send_sem=send_sems.at[k - 1], recv_sem=recv_sems.at[k - 1],
                device_id=(px, py, pc), device_id_type=MESH_ID,
            ).wait_recv()
        own.wait()

    sems = pltpu.SemaphoreType.DMA((N_DEV - 1,))
    return pl.pallas_call(
        body,
        name="gather_small",
        in_specs=[VMEM_SPEC],
        out_specs=VMEM_SPEC,
        out_shape=jax.ShapeDtypeStruct((N_DEV,) + pack.shape, pack.dtype),
        scratch_shapes=[sems, sems, pltpu.SemaphoreType.DMA],
    )(pack)


def _row_tile(rows):
    for tile in range(min(rows, 512) // 8 * 8, 0, -8):
        if rows % tile == 0:
            return tile
    return rows


def _add_half(grad, received, half_index, name):
    slots, h, cdim = received.shape
    tile = _row_tile(h)
    per_half = h // tile

    def body(c_ref, g_ref, r_ref, o_ref, ob_ref):
        total = g_ref[...] + r_ref[...]
        o_ref[...] = total
        ob_ref[...] = total.astype(BF16)

    block = pl.BlockSpec((1, tile, cdim), lambda j, i, c: (j, i, 0))
    grid_spec = pltpu.PrefetchScalarGridSpec(
        num_scalar_prefetch=1,
        grid=(slots, per_half),
        in_specs=[pl.BlockSpec((1, tile, cdim), lambda j, i, c: (j, c[0] * per_half + i, 0)), block],
        out_specs=[block, block],
    )
    return pl.pallas_call(
        body, name=name, grid_spec=grid_spec,
        out_shape=[jax.ShapeDtypeStruct(received.shape, F32), jax.ShapeDtypeStruct(received.shape, BF16)],
        compiler_params=_params(2),
    )(half_index, grad, received)


def _add_chips(part, received, chip_index, core_index, layer, n_layers, shard, name):
    _, h, cdim = part.shape
    tile = _row_tile(h)
    per_half = h // tile

    def body(chip_ref, core_ref, p_ref, r_ref, *rest):
        o_ref = rest[-1]
        o_ref[0] = ((p_ref[0] + r_ref[0].astype(F32)) + r_ref[1].astype(F32)) + r_ref[2].astype(F32)

    in_specs = [
        pl.BlockSpec((1, tile, cdim), lambda i, chip, core: (chip[0], i, 0)),
        pl.BlockSpec((N_CHIPS - 1, tile, cdim), lambda i, chip, core: (0, i, 0)),
    ]
    operands = [chip_index, core_index, part, received]
    aliases = {}
    if shard is not None:
        in_specs.append(ANY)
        operands.append(shard)
        aliases = {4: 0}
    grid_spec = pltpu.PrefetchScalarGridSpec(
        num_scalar_prefetch=2,
        grid=(per_half,),
        in_specs=in_specs,
        out_specs=pl.BlockSpec((1, tile, cdim), lambda i, chip, core: (layer, core[0] * per_half + i, 0)),
    )
    return pl.pallas_call(
        body, name=name, grid_spec=grid_spec, out_shape=jax.ShapeDtypeStruct((n_layers, 2 * h, cdim), F32),
        input_output_aliases=aliases, compiler_params=_params(1),
    )(*operands)


def _adamw(w, g, m, v, name):
    rows, cdim = w.shape
    tile = _row_tile(rows)

    def body(w_ref, g_ref, m_ref, v_ref, d_ref, nm_ref, nv_ref):
        gv = g_ref[...]
        nm = ADAM_B1 * m_ref[...] + (1.0 - ADAM_B1) * gv
        nv = ADAM_B2 * v_ref[...] + (1.0 - ADAM_B2) * (gv * gv)
        m_hat = nm / (1.0 - ADAM_B1 ** ADAM_STEP)
        v_hat = nv / (1.0 - ADAM_B2 ** ADAM_STEP)
        d_ref[...] = -ADAM_LR * (m_hat / (jnp.sqrt(v_hat) + ADAM_EPS) + ADAM_WD * w_ref[...])
        nm_ref[...] = nm
        nv_ref[...] = nv

    spec = pl.BlockSpec((tile, cdim), lambda i: (i, 0))
    shape = jax.ShapeDtypeStruct((rows, cdim), F32)
    return pl.pallas_call(
        body, name=name, grid=(rows // tile,), in_specs=[spec] * 4, out_specs=[spec] * 3, out_shape=[shape] * 3,
        compiler_params=_params(1),
    )(w, g, m, v)


SMALL_ROWS, SMALL_COLS = 24, 1024
ROW_NORM_MIX, ROW_NORM_FFN, ROW_LOSS, ROW_Q_NORM, ROW_K_NORM, ROW_CONV = 0, 2, 4, 8, 10, 16


def _sum_small(gathered):
    def body(g_ref, o_ref, heads_ref, lanes_ref):
        total = g_ref[0]
        for dev in range(1, N_DEV):
            total = total + g_ref[dev]
        o_ref[...] = total
        heads = o_ref[8:16, 0:LANES]
        for grp in range(1, ATTN_DIM // LANES):
            heads = heads + o_ref[8:16, grp * LANES : (grp + 1) * LANES]
        heads_ref[...] = heads + pltpu.roll(heads, HEAD_DIM, 1)
        lanes_ref[...] = jnp.broadcast_to(jnp.sum(o_ref[0:8, :], axis=-1, keepdims=True), (8, LANES))

    return pl.pallas_call(
        body,
        name="sum_small",
        in_specs=[VMEM_SPEC],
        out_specs=[VMEM_SPEC] * 3,
        out_shape=[jax.ShapeDtypeStruct((SMALL_ROWS, SMALL_COLS), F32), jax.ShapeDtypeStruct((8, LANES), F32), jax.ShapeDtypeStruct((8, LANES), F32)],
    )(gathered)


def _pad_rows(a, rows):
    return jnp.pad(a, ((0, rows - a.shape[0]), (0, 0)))


def _pad_to(a, rows, cols):
    return jnp.pad(a, ((0, rows - a.shape[0]), (0, cols - a.shape[1])))


def _conv_taps(conv_s):
    return jnp.transpose(conv_s[:, 0, 0:8], (1, 0, 2)).reshape(8, -1)


class _GradExchange:
    def __init__(self, chip_index, core_index, n_layers):
        self.chip_index, self.core_index, self.n_layers = chip_index, core_index, n_layers
        self.shards = {}
        self.pending = None

    def offer(self, layer, grads):
        assert self.pending is None
        names = list(grads)
        received = _swap_halves([grads[k] for k in names], f"{'_'.join(names)}_{layer}")
        parts = [_add_half(grads[k], r, self.core_index, f"add_half_{k}_{layer}") for k, r in zip(names, received)]
        self.pending = (layer, names, [p32 for p32, _ in parts], [p16 for _, p16 in parts])

    def payload(self):
        return () if self.pending is None else tuple(self.pending[3])

    def take(self, received):
        layer, names, parts, _ = self.pending
        self.pending = None
        for k, p, r in zip(names, parts, received):
            self.shards[k] = _add_chips(
                p, r, self.chip_index, self.core_index, layer, self.n_layers, self.shards.get(k), f"add_chips_{k}_{layer}")

    def finish(self):
        if self.pending is not None:
            layer, names = self.pending[0], self.pending[1]
            self.take(_scatter_to_chips(list(self.pending[3]), f"{'_'.join(names)}_{layer}"))
        return dict(zip(BIG, _join_halves([self.shards[k] for k in BIG])))


def _local_step(x, target, norm_mix, q_norm, k_norm, norm_ffn, layer_weights, exchange=None):
    layer_weights = list(layer_weights)

    def carrying(kernel_fn, n_out, *args):
        if exchange is None or exchange.pending is None:
            return kernel_fn(*args)
        out = kernel_fn(*args, scatter=exchange.payload())
        exchange.take(out[n_out:])
        return out[:n_out]

    n_layers = norm_mix.shape[0]
    s, d = x.shape
    tw = min(WGRAD_TILE, s)
    n_in = layer_weights[0][0].shape[-1]
    f = layer_weights[0][2].shape[-1]
    saved = []
    for l in range(n_layers):
        weights = list(layer_weights[l])
        q_gain = jnp.tile(q_norm[l][None, :], (1, 2))
        k_gain = jnp.tile(k_norm[l][None, :], (1, 2))
        h1, proj = _norm_matmul(x, norm_mix[l][None, :], weights[0], 0, f"in_proj_{l}")
        qn, kn, vb = _qkv_prep(proj, q_gain, k_gain, f"qkv_prep_{l}")
        missing = [n for n, w in enumerate(weights) if w.ndim == 3]
        if missing:
            attn, *arrived = _attn_fwd(qn, kn, vb, f"attn_fwd_{l}", gather=tuple(weights[n] for n in missing))
            for n, w in zip(missing, arrived):
                weights[n] = w
            layer_weights[l] = tuple(weights)
        else:
            attn = _attn_fwd(qn, kn, vb, f"attn_fwd_{l}")
        _, wout_s, wg_s, wu_s, wd_s, conv_s = weights
        taps = _conv_taps(conv_s)
        conv = _conv_fwd(proj, taps, f"conv_fwd_{l}")
        x_mid = _out_proj(x, attn, conv, wout_s, 0, f"out_proj_{l}")
        pending = ()
        if l + 1 < n_layers and all(w.ndim == 3 for w in layer_weights[l + 1]):
            pending = tuple(layer_weights[l + 1])
        x_out, gate, up, *arrived = _ffn_fwd(x_mid, norm_ffn[l][None, :], wg_s, wu_s, wd_s, 0, f"ffn_fwd_{l}", gather=pending)
        if pending:
            layer_weights[l + 1] = tuple(arrived)
        saved.append(dict(x=x, h1=h1, proj=proj, qn=qn, kn=kn, vb=vb, attn=attn, conv=conv, x_mid=x_mid, q_gain=q_gain, k_gain=k_gain,
                          gate=gate, up=up, taps=taps))
        x = x_out

    dy, loss_lanes = _loss_grad(x, target, "loss_grad")
    grads = [None] * n_layers
    for l in reversed(range(n_layers)):
        sv = saved[l]
        win_s, wout_s, wg_s, wu_s, wd_s, _ = layer_weights[l]
        dx_mid, d_norm_ffn, h2, dyb, dgate, dup, act = carrying(
            _ffn_bwd, 7, sv["x_mid"], dy, norm_ffn[l][None, :], sv["gate"], sv["up"], wg_s, wu_s, wd_s, 0, f"ffn_bwd_{l}")
        tok2 = pl.BlockSpec((tw, d), lambda j, i: (i, 0))
        hid = pl.BlockSpec((1, tw, f), lambda j, i: (j, i, 0))
        d_wg = _wgrad(h2, dgate, tok2, hid, N_CHIPS, d, f, f"wgrad_gate_{l}")
        d_wu = _wgrad(h2, dup, tok2, hid, N_CHIPS, d, f, f"wgrad_up_{l}")
        d_wd = _wgrad(act, dyb, hid, tok2, N_CHIPS, f, d, f"wgrad_down_{l}")
        if exchange is not None:
            exchange.offer(l, dict(w_gate=d_wg, w_up=d_wu, w_down=d_wd))
        d_attn, d_conv, dxb = _out_proj_bwd(dx_mid, wout_s, 0, f"out_proj_bwd_{l}")
        rows_out = wout_s.shape[2]
        mix_spec_a = pl.BlockSpec((tw, rows_out), lambda j, i: (i, j))
        d_wout_a = _wgrad(sv["attn"], dxb, mix_spec_a, tok2, ATTN_DIM // rows_out, rows_out, d, f"wgrad_out_attn_{l}")
        d_wout_c = _wgrad(sv["conv"], dxb, mix_spec_a, tok2, CONV_DIM // rows_out, rows_out, d, f"wgrad_out_conv_{l}")
        d_wout = jnp.concatenate([d_wout_a, d_wout_c], axis=0)
        dq, dk, dv = carrying(_attn_bwd, 3, sv["qn"], sv["kn"], sv["vb"], d_attn, f"attn_bwd_{l}")
        dproj, d_conv_w = _conv_bwd(sv["proj"], sv["taps"], d_conv, f"conv_bwd_{l}")
        dproj, d_qg, d_kg = _qkv_prep_bwd(sv["proj"], sv["q_gain"], sv["k_gain"], dq, dk, dv, dproj, f"qkv_prep_bwd_{l}")
        d_win = _wgrad(sv["h1"], dproj, tok2, pl.BlockSpec((tw, n_in), lambda j, i: (i, j)), N_CHIPS, d, n_in, f"wgrad_in_{l}")
        if exchange is not None:
            exchange.offer(l, dict(w_in=d_win, w_out=d_wout))
        dy, d_norm_mix = _in_proj_bwd(sv["x"], dx_mid, norm_mix[l][None, :], dproj, win_s, 0, f"in_proj_bwd_{l}")
        grads[l] = dict(norm_mix=d_norm_mix, norm_ffn=d_norm_ffn, q_norm=d_qg, k_norm=d_kg, conv_w=d_conv_w,
                        w_in=d_win, w_out=d_wout, w_gate=d_wg, w_up=d_wu, w_down=d_wd)
    return loss_lanes, dy, grads


BIG = ("w_in", "w_out", "w_gate", "w_up", "w_down")


def kernel(x, norm_mix, w_in, q_norm, k_norm, conv_w, w_out, norm_ffn, w_gate, w_up, w_down, loss_target, m_norm_mix, m_w_in, m_q_norm, m_k_norm, m_conv_w, m_w_out, m_norm_ffn, m_w_gate, m_w_up, m_w_down, v_norm_mix, v_w_in, v_q_norm, v_k_norm, v_conv_w, v_w_out, v_norm_ffn, v_w_gate, v_w_up, v_w_down):
    n_layers = norm_mix.shape[0]
    weights = dict(w_in=w_in, w_out=w_out, w_gate=w_gate, w_up=w_up, w_down=w_down)
    moments_m = dict(w_in=m_w_in, w_out=m_w_out, w_gate=m_w_gate, w_up=m_w_up, w_down=m_w_down)
    moments_v = dict(w_in=v_w_in, w_out=v_w_out, w_gate=v_w_gate, w_up=v_w_up, w_down=v_w_down)
    cx, cy, cc = _mesh_position()
    chip_index = (2 * cx + cy).astype(jnp.int32).reshape(1)
    core_index = cc.astype(jnp.int32).reshape(1)

    conv_pad = jnp.pad(conv_w, ((0, 0), (0, 16 - conv_w.shape[1]), (0, 0)))

    def shards_of(layer):
        return [weights[k][layer : layer + 1].astype(BF16) for k in BIG] + [conv_pad[layer : layer + 1]]

    first = shards_of(0)
    layer_weights = [tuple(_gather_weights(first[:1])) + tuple(first[1:])] + [tuple(shards_of(layer)) for layer in range(1, n_layers)]

    exchange = _GradExchange(chip_index, core_index, n_layers)
    loss_lanes, grad_x, grads = _local_step(
        x[0], loss_target[0], norm_mix, q_norm, k_norm, norm_ffn, layer_weights, exchange)

    big_grads = exchange.finish()

    def lanes(a):
        return _pad_to(a, a.shape[0], SMALL_COLS)

    def tile_of(*groups):
        return _pad_rows(jnp.concatenate([lanes(jnp.concatenate(g, axis=0)) for g in groups], axis=0), 8)

    layers = range(n_layers)
    pack = jnp.concatenate([
        tile_of([grads[l]["norm_mix"] for l in layers], [grads[l]["norm_ffn"] for l in layers], [loss_lanes]),
        tile_of([grads[l]["q_norm"] for l in layers], [grads[l]["k_norm"] for l in layers]),
        tile_of([grads[l]["conv_w"][0:3] for l in layers]),
    ], axis=0)
    small, small_heads, small_lanes = _sum_small(_gather_small(pack))
    loss = small_lanes[ROW_LOSS, 0]
    d_model = norm_mix.shape[1]
    conv_cols = conv_w.shape[2]
    conv_all = small[ROW_CONV : ROW_CONV + 3 * n_layers, 0:CONV_DIM].reshape(n_layers, 3, CONV_DIM)
    small_grads = dict(
        norm_mix=small[ROW_NORM_MIX : ROW_NORM_MIX + n_layers, 0:d_model],
        norm_ffn=small[ROW_NORM_FFN : ROW_NORM_FFN + n_layers, 0:d_model],
        q_norm=small_heads[ROW_Q_NORM - 8 : ROW_Q_NORM - 8 + n_layers, 0:HEAD_DIM],
        k_norm=small_heads[ROW_K_NORM - 8 : ROW_K_NORM - 8 + n_layers, 0:HEAD_DIM],
        conv_w=lax.dynamic_slice_in_dim(conv_all, (2 * cx + cy) * conv_cols, conv_cols, axis=2),
    )

    out_grad, out_delta, out_m, out_v = {}, {}, {}, {}
    for k in BIG:
        shape = weights[k].shape
        view = (shape[0] * shape[1], shape[2])
        g = big_grads[k]
        delta, new_m, new_v = _adamw(weights[k].reshape(view), g.reshape(view), moments_m[k].reshape(view), moments_v[k].reshape(view), f"adamw_{k}")
        out_grad[k], out_delta[k], out_m[k], out_v[k] = g, delta.reshape(shape), new_m.reshape(shape), new_v.reshape(shape)

    small_w = dict(norm_mix=norm_mix, norm_ffn=norm_ffn, q_norm=q_norm, k_norm=k_norm, conv_w=conv_w)
    small_m = dict(norm_mix=m_norm_mix, norm_ffn=m_norm_ffn, q_norm=m_q_norm, k_norm=m_k_norm, conv_w=m_conv_w)
    small_v = dict(norm_mix=v_norm_mix, norm_ffn=v_norm_ffn, q_norm=v_q_norm, k_norm=v_k_norm, conv_w=v_conv_w)
    order = ("norm_mix", "norm_ffn", "q_norm", "k_norm", "conv_w")

    def packed(tree):
        parts2 = [_pad_to(tree[k].reshape(-1, tree[k].shape[-1]), tree[k].reshape(-1, tree[k].shape[-1]).shape[0], SMALL_COLS) for k in order]
        return _pad_rows(jnp.concatenate(parts2, axis=0), SMALL_ROWS)

    delta_p, m_p, v_p = _adamw(packed(small_w), packed(small_grads), packed(small_m), packed(small_v), "adamw_small")
    row = 0
    for k in order:
        shape = small_w[k].shape
        n_rows = 1
        for dim in shape[:-1]:
            n_rows *= dim
        cut = (slice(row, row + n_rows), slice(0, shape[-1]))
        out_grad[k] = small_grads[k]
        out_delta[k], out_m[k], out_v[k] = delta_p[cut].reshape(shape), m_p[cut].reshape(shape), v_p[cut].reshape(shape)
        row += n_rows

    names_out = ("norm_mix", "w_in", "q_norm", "k_norm", "conv_w", "w_out", "norm_ffn", "w_gate", "w_up", "w_down")
    return (loss, grad_x[None], *[out_grad[k] for k in names_out], *[out_delta[k] for k in names_out],
            *[out_m[k] for k in names_out], *[out_v[k] for k in names_out])
```

```python
import functools

import jax
import jax.numpy as jnp
from jax import lax
from jax.experimental import pallas as pl
from jax.experimental.pallas import tpu as pltpu

F32 = jnp.float32
BF16 = jnp.bfloat16

EPS = 1e-6
HEAD_DIM = 64
LANES = 128
ATTN_DIM = 512
CONV_DIM = 512
N_CHIPS = 4
N_DEV = 8
Q_SCALE = HEAD_DIM ** -0.5
ATTN_Q_TILE = 256
ATTN_TILE = 256
TOKEN_TILE = 512
WGRAD_TILE = 2048
VMEM_LIMIT = 56 * 1024 * 1024

ADAM_LR = 0.001
ADAM_B1 = 0.9
ADAM_B2 = 0.999
ADAM_EPS = 1e-08
ADAM_WD = 0.01
ADAM_STEP = 10

MESH_ID = pl.DeviceIdType.MESH
ANY = pl.BlockSpec(memory_space=pl.ANY)
VMEM_SPEC = pl.BlockSpec(memory_space=pltpu.VMEM)


def _params(n_axes):
    return pltpu.CompilerParams(dimension_semantics=("arbitrary",) * n_axes, vmem_limit_bytes=VMEM_LIMIT)


def _dot(a, b):
    return jnp.dot(a, b, preferred_element_type=F32)


def _dot_nt(a, b):
    return lax.dot_general(a, b, (((1,), (1,)), ((), ())), preferred_element_type=F32)


def _dot_tn(a, b):
    return lax.dot_general(a, b, (((0,), (0,)), ((), ())), preferred_element_type=F32)


SCORE_MAX = 80.0
UNDERFLOW_EXIT = 90.0


def _scores(q, k):
    return jnp.minimum(_dot_nt(q, k), SCORE_MAX)


def _softplus(z):
    return jnp.log(1.0 + jnp.exp(z))


def _norm_matmul(x, gain, w_s, layer, name):
    s, d = x.shape
    n_blocks, _, _, n = w_s.shape
    tm = TOKEN_TILE

    def body(x_ref, g_ref, w_ref, h_ref, o_ref):
        xv = x_ref[...]
        r = lax.rsqrt(jnp.mean(xv * xv, axis=-1, keepdims=True) + EPS)
        h = (xv * r * g_ref[...]).astype(BF16)
        h_ref[...] = h
        for j in range(n_blocks):
            o_ref[:, j * n : (j + 1) * n] = _dot(h, w_ref[j, 0])

    return pl.pallas_call(
        body,
        name=name,
        grid=(s // tm,),
        in_specs=[
            pl.BlockSpec((tm, d), lambda i: (i, 0)),
            pl.BlockSpec((1, d), lambda i: (0, 0)),
            pl.BlockSpec((n_blocks, 1, d, n), lambda i: (0, layer, 0, 0)),
        ],
        out_specs=[pl.BlockSpec((tm, d), lambda i: (i, 0)), pl.BlockSpec((tm, n_blocks * n), lambda i: (i, 0))],
        out_shape=[jax.ShapeDtypeStruct((s, d), BF16), jax.ShapeDtypeStruct((s, n_blocks * n), F32)],
        compiler_params=_params(1),
    )(x, gain, w_s)


def _head_norm(xv, gain, low):
    sq = xv * xv
    s_low = jnp.sum(jnp.where(low, sq, 0.0), axis=-1, keepdims=True)
    s_high = jnp.sum(jnp.where(low, 0.0, sq), axis=-1, keepdims=True)
    r = jnp.where(low, lax.rsqrt(s_low / HEAD_DIM + EPS), lax.rsqrt(s_high / HEAD_DIM + EPS))
    return xv * r * gain, r


def _qkv_prep(proj, q_gain, k_gain, name):
    s = proj.shape[0]
    tm = TOKEN_TILE

    def body(p_ref, qg_ref, kg_ref, q_ref, k_ref, v_ref):
        low = lax.broadcasted_iota(jnp.int32, (tm, LANES), 1) < HEAD_DIM
        for g in range(ATTN_DIM // LANES):
            cq = slice(LANES * g, LANES * (g + 1))
            ck = slice(ATTN_DIM + LANES * g, ATTN_DIM + LANES * (g + 1))
            cv = slice(2 * ATTN_DIM + LANES * g, 2 * ATTN_DIM + LANES * (g + 1))
            qn, _ = _head_norm(p_ref[:, cq], qg_ref[...], low)
            kn, _ = _head_norm(p_ref[:, ck], kg_ref[...], low)
            q_ref[:, cq] = (qn * Q_SCALE).astype(BF16)
            k_ref[:, cq] = kn.astype(BF16)
            v_ref[:, cq] = p_ref[:, cv].astype(BF16)

    out = jax.ShapeDtypeStruct((s, ATTN_DIM), BF16)
    return pl.pallas_call(
        body,
        name=name,
        grid=(s // tm,),
        in_specs=[
            pl.BlockSpec((tm, 3 * ATTN_DIM), lambda i: (i, 0)),
            pl.BlockSpec((1, LANES), lambda i: (0, 0)),
            pl.BlockSpec((1, LANES), lambda i: (0, 0)),
        ],
        out_specs=[pl.BlockSpec((tm, ATTN_DIM), lambda i: (i, 0))] * 3,
        out_shape=[out, out, out],
        compiler_params=_params(1),
    )(proj, q_gain, k_gain)


def _qkv_prep_bwd(proj, q_gain, k_gain, dq, dk, dv, dproj, name):
    s = proj.shape[0]
    tm = TOKEN_TILE

    def norm_bwd(xv, gain, dy, low):
        _, r = _head_norm(xv, gain, low)
        xhat = xv * r
        dxhat = dy * gain
        prod = dxhat * xhat
        m_low = jnp.sum(jnp.where(low, prod, 0.0), axis=-1, keepdims=True)
        m_high = jnp.sum(jnp.where(low, 0.0, prod), axis=-1, keepdims=True)
        mean = jnp.where(low, m_low, m_high) / HEAD_DIM
        return r * (dxhat - xhat * mean), jnp.sum(dy * xhat, axis=0, keepdims=True)

    def body(p_ref, qg_ref, kg_ref, dq_ref, dk_ref, dv_ref, dproj_ref, dp_ref, dqg_ref, dkg_ref):
        @pl.when(pl.program_id(0) == 0)
        def _():
            dqg_ref[...] = jnp.zeros_like(dqg_ref)
            dkg_ref[...] = jnp.zeros_like(dkg_ref)

        low = lax.broadcasted_iota(jnp.int32, (tm, LANES), 1) < HEAD_DIM
        for g in range(ATTN_DIM // LANES):
            cq = slice(LANES * g, LANES * (g + 1))
            ck = slice(ATTN_DIM + LANES * g, ATTN_DIM + LANES * (g + 1))
            cv = slice(2 * ATTN_DIM + LANES * g, 2 * ATTN_DIM + LANES * (g + 1))
            dxq, dgq = norm_bwd(p_ref[:, cq], qg_ref[...], dq_ref[:, cq] * Q_SCALE, low)
            dxk, dgk = norm_bwd(p_ref[:, ck], kg_ref[...], dk_ref[:, cq], low)
            dp_ref[:, cq] = dxq.astype(BF16)
            dp_ref[:, ck] = dxk.astype(BF16)
            dp_ref[:, cv] = dv_ref[:, cq].astype(BF16)
            dqg_ref[:, cq] += dgq
            dkg_ref[:, cq] += dgk

    grad_spec = pl.BlockSpec((tm, ATTN_DIM), lambda i: (i, 0))
    gain_spec = pl.BlockSpec((1, LANES), lambda i: (0, 0))
    sum_spec = pl.BlockSpec((1, ATTN_DIM), lambda i: (0, 0))
    return pl.pallas_call(
        body,
        name=name,
        grid=(s // tm,),
        in_specs=[pl.BlockSpec((tm, 3 * ATTN_DIM), lambda i: (i, 0)), gain_spec, gain_spec, grad_spec, grad_spec, grad_spec, ANY],
        out_specs=[pl.BlockSpec((tm, 3 * ATTN_DIM), lambda i: (i, 0)), sum_spec, sum_spec],
        out_shape=[
            jax.ShapeDtypeStruct(dproj.shape, BF16),
            jax.ShapeDtypeStruct((1, ATTN_DIM), F32),
            jax.ShapeDtypeStruct((1, ATTN_DIM), F32),
        ],
        input_output_aliases={6: 0},
        compiler_params=_params(1),
    )(proj, q_gain, k_gain, dq, dk, dv, dproj)


def _attn_tile_consts(t):
    row = lax.broadcasted_iota(jnp.int32, (t, t), 0)
    col = lax.broadcasted_iota(jnp.int32, (t, t), 1)
    return row, col


def _triangle_sum(v, triangle):
    return _dot(v.astype(BF16), triangle)


def _attn_fwd(qn, kn, vb, name, gather=()):
    s = qn.shape[0]
    t = min(ATTN_TILE, s)
    tq = min(ATTN_Q_TILE, t)
    per_key_tile = t // tq
    n_gather = len(gather)
    n_pairs, n_blocks = ATTN_DIM // LANES, s // tq

    def body(*refs):
        q_ref, k_ref, v_ref = refs[:3]
        o_ref = refs[3 + n_gather]
        if n_gather:
            copies = _WeightGather(refs[3 : 3 + n_gather], refs[4 + n_gather : 4 + 2 * n_gather], refs[4 + 2 * n_gather :])
            first = (pl.program_id(0) == 0) & (pl.program_id(1) == 0)
            pl.when(first)(copies.begin)
            pl.when((pl.program_id(0) == n_pairs - 1) & (pl.program_id(1) == 0))(copies.relay)
        i = pl.program_id(1) // per_key_tile
        low = lax.broadcasted_iota(jnp.int32, (tq, LANES), 1) < HEAD_DIM
        row, col = _attn_tile_consts(t)
        suffix = (row > col).astype(BF16)
        first_row = (pl.program_id(1) % per_key_tile) * tq
        causal = lax.broadcasted_iota(jnp.int32, (tq, t), 1) < lax.broadcasted_iota(jnp.int32, (tq, t), 0) + first_row
        q = q_ref[...]
        zero_q = jnp.zeros_like(q)
        qh = (jnp.where(low, q, zero_q), jnp.where(low, zero_q, q))

        def step(kbs, carry, diagonal_first=False):
            chains = [(head, m) for head in range(2) for m in range(len(kbs))]
            masked = [diagonal_first and m == 0 for _, m in chains]
            ks = [k_ref[pl.ds(pl.multiple_of(kb * t, t), t), :] for kb in kbs]
            vs = [v_ref[pl.ds(pl.multiple_of(kb * t, t), t), :] for kb in kbs]
            z = [_scores(qh[head], ks[kb]) for head, kb in chains]
            sp = [_softplus(zc) for zc in z]
            sp = [jnp.where(causal, s_, 0.0) if mk else s_ for s_, mk in zip(sp, masked)]
            inside = [_triangle_sum(s_, suffix) for s_ in sp]
            after = [carry[head][1] for head in range(2)]
            log_a = []
            for n, (head, kb) in enumerate(chains):
                log_a.append(z[n] - sp[n] - inside[n] - after[head])
                after[head] = after[head] + jnp.sum(sp[n], axis=-1, keepdims=True)
            a = [jnp.exp(l_) for l_ in log_a]
            a = [jnp.where(causal, a_, 0.0) if mk else a_ for a_, mk in zip(a, masked)]
            acc = [carry[head][0] for head in range(2)]
            for n, (head, kb) in enumerate(chains):
                acc[head] = acc[head] + _dot(a[n].astype(BF16), vs[kb])
            return tuple((acc[head], after[head]) for head in range(2))

        def live(c):
            return jnp.minimum(jnp.min(c[0][1]), jnp.min(c[1][1])) < UNDERFLOW_EXIT

        zero = (jnp.zeros((tq, LANES), F32), jnp.zeros((tq, 1), F32))
        carry = lax.cond(i >= 1, lambda c: step((i, i - 1), c, True), lambda c: step((i,), c, True), (zero, zero))
        rest = jnp.maximum(i - 1, 0)
        carry = lax.cond((rest % 2 == 1) & live(carry), lambda c: step((i - 2,), c), lambda c: c, carry)
        pairs = rest // 2
        _, carry = lax.while_loop(
            lambda st: (st[0] < pairs) & live(st[1]),
            lambda st: (st[0] + 1, step((2 * (pairs - st[0]) - 1, 2 * (pairs - st[0]) - 2), st[1])),
            (jnp.int32(0), carry))
        o_ref[...] = jnp.where(low, carry[0][0], carry[1][0]).astype(BF16)
        if n_gather:
            pl.when((pl.program_id(0) == n_pairs - 1) & (pl.program_id(1) == n_blocks - 1))(copies.finish)

    out = pl.pallas_call(
        body,
        name=name,
        grid=(n_pairs, n_blocks),
        in_specs=[
            pl.BlockSpec((tq, LANES), lambda p, i: (i, p)),
            pl.BlockSpec((s, LANES), lambda p, i: (0, p)),
            pl.BlockSpec((s, LANES), lambda p, i: (0, p)),
        ] + [ANY] * n_gather,
        out_specs=[pl.BlockSpec((tq, LANES), lambda p, i: (i, p))] + [ANY] * n_gather,
        out_shape=[jax.ShapeDtypeStruct((s, ATTN_DIM), BF16)] + [jax.ShapeDtypeStruct((N_CHIPS,) + w.shape, w.dtype) for w in gather],
        scratch_shapes=_gather_scratch(n_gather) if n_gather else [],
        compiler_params=_params(2),
    )(qn, kn, vb, *gather)
    return out if n_gather else out[0]


def _attn_bwd(qn, kn, vb, do, name, scatter=()):
    s = qn.shape[0]
    t = min(ATTN_TILE, s)
    nq = s // t
    n_scatter = len(scatter)
    n_pairs = ATTN_DIM // LANES

    def body(*refs):
        q_ref, k_ref, v_ref, do_ref = refs[:4]
        dq_ref, dk_ref, dv_ref = refs[4 + n_scatter : 7 + n_scatter]
        a_s, sg_s, a_f, sg_f = refs[7 + 2 * n_scatter : 11 + 2 * n_scatter]
        i = pl.program_id(1)
        if n_scatter:
            copies = _ChipScatter(refs[4 : 4 + n_scatter], refs[7 + n_scatter : 7 + 2 * n_scatter], refs[11 + 2 * n_scatter :])
            pl.when((pl.program_id(0) == 0) & (i == 0))(copies.begin)

        @pl.when(i == 0)
        def _():
            dk_ref[...] = jnp.zeros_like(dk_ref)
            dv_ref[...] = jnp.zeros_like(dv_ref)

        low = lax.broadcasted_iota(jnp.int32, (t, LANES), 1) < HEAD_DIM
        row, col = _attn_tile_consts(t)
        suffix = (row > col).astype(BF16)
        prefix = (row < col).astype(BF16)
        causal = col < row
        q = q_ref[...]
        dob = do_ref[...]
        zero_q = jnp.zeros_like(q)
        qhs = (jnp.where(low, q, zero_q), jnp.where(low, zero_q, q))
        dohs = (jnp.where(low, dob, zero_q), jnp.where(low, zero_q, dob))

        def rows_of(kb):
            return pl.ds(pl.multiple_of(kb * t, t), t)

        pair = [(head, m) for head in range(2) for m in range(2)]

        def short_pass1():
            z = [_scores(qhs[head], k_ref[rows_of(i - m), :]) for head, m in pair]
            sp = [_softplus(z_) for z_ in z]
            sp = [jnp.where(causal, s_, 0.0) if m == 0 else s_ for s_, (_, m) in zip(sp, pair)]
            inside = [_triangle_sum(s_, suffix) for s_ in sp]
            after = [jnp.zeros((t, 1), F32), jnp.zeros((t, 1), F32)]
            for n, (head, m) in enumerate(pair):
                log_sg = z[n] - sp[n]
                a = jnp.exp(log_sg - inside[n] - after[head])
                sg = jnp.exp(log_sg)
                if m == 0:
                    a = jnp.where(causal, a, 0.0)
                    sg = jnp.where(causal, sg, 0.0)
                a_f[n] = a
                sg_f[n] = sg
                after[head] = after[head] + jnp.sum(sp[n], axis=-1, keepdims=True)
            return jnp.minimum(jnp.min(after[0]), jnp.min(after[1])) >= UNDERFLOW_EXIT

        def short_pass2():
            order = [(head, m) for head in range(2) for m in (1, 0)]
            a = {c: a_f[pair.index(c)] for c in order}
            g = {c: a[c] * _dot_nt(dohs[c[0]], v_ref[rows_of(i - c[1]), :]) for c in order}
            for m in (1, 0):
                dv_ref[rows_of(i - m), :] += _dot_tn(a[(0, m)].astype(BF16), dohs[0]) + _dot_tn(a[(1, m)].astype(BF16), dohs[1])
            inside = {c: _triangle_sum(g[c], prefix) for c in order}
            before = [jnp.zeros((t, 1), F32), jnp.zeros((t, 1), F32)]
            dz = {}
            for c in order:
                sg = sg_f[pair.index(c)]
                dz[c] = (g[c] - sg * (g[c] + inside[c] + before[c[0]])).astype(BF16)
                before[c[0]] = before[c[0]] + jnp.sum(g[c], axis=-1, keepdims=True)
            for m in (1, 0):
                dk_ref[rows_of(i - m), :] += _dot_tn(dz[(0, m)], qhs[0]) + _dot_tn(dz[(1, m)], qhs[1])
            dq = [_dot(dz[(head, 1)], k_ref[rows_of(i - 1), :]) + _dot(dz[(head, 0)], k_ref[rows_of(i), :]) for head in range(2)]
            dq_ref[...] = jnp.where(low, dq[0], dq[1])

        def general_walk():
            heads = []
            for head in range(2):
                qh, doh = qhs[head], dohs[head]

                def pass1(kbs, after, diagonal_first=False):
                    z = [_scores(qh, k_ref[rows_of(kb), :]) for kb in kbs]
                    sp = [_softplus(z_) for z_ in z]
                    if diagonal_first:
                        sp[0] = jnp.where(causal, sp[0], 0.0)
                    inside = [_triangle_sum(s_, suffix) for s_ in sp]
                    for n, kb in enumerate(kbs):
                        log_sg = z[n] - sp[n]
                        a = jnp.exp(log_sg - inside[n] - after)
                        sg = jnp.exp(log_sg)
                        if diagonal_first and n == 0:
                            a = jnp.where(causal, a, 0.0)
                            sg = jnp.where(causal, sg, 0.0)
                        a_s[kb] = a
                        sg_s[kb] = sg
                        after = after + jnp.sum(sp[n], axis=-1, keepdims=True)
                    return after

                def live(after):
                    return jnp.min(after) < UNDERFLOW_EXIT

                after = jnp.zeros((t, 1), F32)
                after = lax.cond(i >= 1, lambda c: pass1((i, i - 1), c, True), lambda c: pass1((i,), c, True), after)
                rest = jnp.maximum(i - 1, 0)
                take_single = (rest % 2 == 1) & live(after)
                after = lax.cond(take_single, lambda c: pass1((i - 2,), c), lambda c: c, after)
                pairs = rest // 2
                pairs_done, _ = lax.while_loop(
                    lambda st: (st[0] < pairs) & live(st[1]),
                    lambda st: (st[0] + 1, pass1((2 * (pairs - st[0]) - 1, 2 * (pairs - st[0]) - 2), st[1])),
                    (jnp.int32(0), after))
                walked = jnp.minimum(i, 1) + 1 + take_single.astype(jnp.int32) + 2 * pairs_done
                first = i - walked + 1

                def pass2(kbs, carry):
                    dq, before = carry
                    ks = [k_ref[rows_of(kb), :] for kb in kbs]
                    a = [a_s[kb] for kb in kbs]
                    g = [a_ * _dot_nt(doh, v_ref[rows_of(kb), :]) for a_, kb in zip(a, kbs)]
                    for n, kb in enumerate(kbs):
                        dv_ref[rows_of(kb), :] += _dot_tn(a[n].astype(BF16), doh)
                    inside = [_triangle_sum(g_, prefix) for g_ in g]
                    dz = []
                    for n, kb in enumerate(kbs):
                        sg = sg_s[kb]
                        dz.append((g[n] - sg * (g[n] + inside[n] + before)).astype(BF16))
                        before = before + jnp.sum(g[n], axis=-1, keepdims=True)
                    for n, kb in enumerate(kbs):
                        dk_ref[rows_of(kb), :] += _dot_tn(dz[n], qh)
                    for n in range(len(kbs)):
                        dq = dq + _dot(dz[n], ks[n])
                    return dq, before

                carry = (jnp.zeros((t, LANES), F32), jnp.zeros((t, 1), F32))
                carry = lax.fori_loop(0, walked // 2, lambda n, c: pass2((first + 2 * n, first + 2 * n + 1), c), carry)
                carry = lax.cond(walked % 2 == 1, lambda c: pass2((i,), c), lambda c: c, carry)
                heads.append(carry[0])
            dq_ref[...] = jnp.where(low, heads[0], heads[1])

        short = lax.cond(i >= 1, short_pass1, lambda: jnp.bool_(False))
        pl.when(short)(short_pass2)
        pl.when(jnp.logical_not(short))(general_walk)
        if n_scatter:
            pl.when((pl.program_id(0) == n_pairs - 1) & (i == nq - 1))(copies.finish)

    q_spec = pl.BlockSpec((t, LANES), lambda p, i: (i, p))
    kv_spec = pl.BlockSpec((s, LANES), lambda p, i: (0, p))
    return pl.pallas_call(
        body,
        name=name,
        grid=(n_pairs, nq),
        in_specs=[q_spec, kv_spec, kv_spec, q_spec] + [ANY] * n_scatter,
        out_specs=[q_spec, kv_spec, kv_spec] + [ANY] * n_scatter,
        out_shape=[jax.ShapeDtypeStruct((s, ATTN_DIM), F32)] * 3 + _scatter_shapes(scatter),
        scratch_shapes=[pltpu.VMEM((nq, t, t), F32), pltpu.VMEM((nq, t, t), F32), pltpu.VMEM((4, t, t), F32), pltpu.VMEM((4, t, t), F32)]
        + (_scatter_scratch(n_scatter) if n_scatter else []),
        compiler_params=_params(2),
    )(qn, kn, vb, do, *scatter)


CB_BLOCK, CC_BLOCK, CU_BLOCK = 3, 4, 5


def _shift_down(h, prev_rows, n):
    row = lax.broadcasted_iota(jnp.int32, h.shape, 0)
    out = pltpu.roll(h, n, 0)
    for r in range(n):
        out = jnp.where(row == r, prev_rows[len(prev_rows) - n + r], out)
    return out


def _shift_up(h, next_rows, n):
    tm = h.shape[0]
    row = lax.broadcasted_iota(jnp.int32, h.shape, 0)
    out = pltpu.roll(h, tm - n, 0)
    for r in range(n):
        out = jnp.where(row == tm - n + r, next_rows[r], out)
    return out


def _conv_fwd(proj, conv_w, name):
    s = proj.shape[0]
    tm = TOKEN_TILE
    nb = tm // 8

    def body(cb_ref, cc_ref, cu_ref, pc_ref, pu_ref, w_ref, o_ref):
        first = pl.program_id(0) == 0
        h = cc_ref[...] * cu_ref[...]
        prev = [jnp.where(first, 0.0, pc_ref[r : r + 1, :] * pu_ref[r : r + 1, :]) for r in (6, 7)]
        y = w_ref[0:1, :] * _shift_down(h, prev, 2) + w_ref[1:2, :] * _shift_down(h, prev, 1) + w_ref[2:3, :] * h
        o_ref[...] = (cb_ref[...] * y).astype(BF16)

    def col(block):
        return pl.BlockSpec((tm, CONV_DIM), lambda i: (i, block))

    def halo(block):
        return pl.BlockSpec((8, CONV_DIM), lambda i: (jnp.maximum(i * nb - 1, 0), block))

    return pl.pallas_call(
        body,
        name=name,
        grid=(s // tm,),
        in_specs=[col(CB_BLOCK), col(CC_BLOCK), col(CU_BLOCK), halo(CC_BLOCK), halo(CU_BLOCK), pl.BlockSpec((8, CONV_DIM), lambda i: (0, 0))],
        out_specs=pl.BlockSpec((tm, CONV_DIM), lambda i: (i, 0)),
        out_shape=jax.ShapeDtypeStruct((s, CONV_DIM), BF16),
        compiler_params=_params(1),
    )(proj, proj, proj, proj, proj, conv_w)


def _conv_bwd(proj, conv_w, dconv, name):
    s = proj.shape[0]
    tm = TOKEN_TILE
    nb = tm // 8
    n_tiles = s // tm

    def body(cb_ref, cc_ref, cu_ref, dy_ref, pc_ref, pu_ref, nb_ref, ndy_ref, w_ref, dp_ref, dw_ref):
        i = pl.program_id(0)

        @pl.when(i == 0)
        def _():
            dw_ref[...] = jnp.zeros_like(dw_ref)

        first = i == 0
        last = i == n_tiles - 1
        cc, cu, cb, dy = cc_ref[...], cu_ref[...], cb_ref[...], dy_ref[...]
        h = cc * cu
        prev = [jnp.where(first, 0.0, pc_ref[r : r + 1, :] * pu_ref[r : r + 1, :]) for r in (6, 7)]
        h1 = _shift_down(h, prev, 1)
        h2 = _shift_down(h, prev, 2)
        y = w_ref[0:1, :] * h2 + w_ref[1:2, :] * h1 + w_ref[2:3, :] * h
        dyb = dy * cb
        nxt = [jnp.where(last, 0.0, ndy_ref[r : r + 1, :] * nb_ref[r : r + 1, :]) for r in (0, 1)]
        dh = w_ref[2:3, :] * dyb + w_ref[1:2, :] * _shift_up(dyb, nxt, 1) + w_ref[0:1, :] * _shift_up(dyb, nxt, 2)
        dp_ref[:, 0:CONV_DIM] = (dy * y).astype(BF16)
        dp_ref[:, CONV_DIM : 2 * CONV_DIM] = (dh * cu).astype(BF16)
        dp_ref[:, 2 * CONV_DIM : 3 * CONV_DIM] = (dh * cc).astype(BF16)
        dw_ref[0:1, :] += jnp.sum(dyb * h2, axis=0, keepdims=True)
        dw_ref[1:2, :] += jnp.sum(dyb * h1, axis=0, keepdims=True)
        dw_ref[2:3, :] += jnp.sum(dyb * h, axis=0, keepdims=True)

    def col(block):
        return pl.BlockSpec((tm, CONV_DIM), lambda i: (i, block))

    def halo_prev(block):
        return pl.BlockSpec((8, CONV_DIM), lambda i: (jnp.maximum(i * nb - 1, 0), block))

    def halo_next(block):
        return pl.BlockSpec((8, CONV_DIM), lambda i: (jnp.minimum((i + 1) * nb, s // 8 - 1), block))

    return pl.pallas_call(
        body,
        name=name,
        grid=(n_tiles,),
        in_specs=[
            col(CB_BLOCK), col(CC_BLOCK), col(CU_BLOCK), col(0),
            halo_prev(CC_BLOCK), halo_prev(CU_BLOCK), halo_next(CB_BLOCK), halo_next(0),
            pl.BlockSpec((8, CONV_DIM), lambda i: (0, 0)),
        ],
        out_specs=[pl.BlockSpec((tm, 3 * CONV_DIM), lambda i: (i, 1)), pl.BlockSpec((8, CONV_DIM), lambda i: (0, 0))],
        out_shape=[jax.ShapeDtypeStruct((s, 3 * ATTN_DIM + 3 * CONV_DIM), BF16), jax.ShapeDtypeStruct((8, CONV_DIM), F32)],
        compiler_params=_params(1),
    )(proj, proj, proj, dconv, proj, proj, proj, dconv, conv_w)


def _out_proj(x, attn, conv, w_s, layer, name):
    s, d = x.shape
    tm = TOKEN_TILE
    rows = w_s.shape[2]

    def body(x_ref, a_ref, c_ref, w_ref, o_ref):
        acc = x_ref[...]
        for j in range(N_CHIPS):
            src = a_ref if j < 2 else c_ref
            cols = slice((j % 2) * rows, (j % 2 + 1) * rows)
            acc = acc + _dot(src[:, cols], w_ref[j, 0])
        o_ref[...] = acc

    return pl.pallas_call(
        body,
        name=name,
        grid=(s // tm,),
        in_specs=[
            pl.BlockSpec((tm, d), lambda i: (i, 0)),
            pl.BlockSpec((tm, ATTN_DIM), lambda i: (i, 0)),
            pl.BlockSpec((tm, CONV_DIM), lambda i: (i, 0)),
            pl.BlockSpec((N_CHIPS, 1, rows, d), lambda i: (0, layer, 0, 0)),
        ],
        out_specs=pl.BlockSpec((tm, d), lambda i: (i, 0)),
        out_shape=jax.ShapeDtypeStruct((s, d), F32),
        compiler_params=_params(1),
    )(x, attn, conv, w_s)


def _out_proj_bwd(dx, w_s, layer, name):
    s, d = dx.shape
    tm = TOKEN_TILE
    rows = w_s.shape[2]

    def body(dx_ref, w_ref, da_ref, dc_ref, dxb_ref):
        dxb = dx_ref[...].astype(BF16)
        dxb_ref[...] = dxb
        for j in range(N_CHIPS):
            cols = slice((j % 2) * rows, (j % 2 + 1) * rows)
            part = _dot_nt(dxb, w_ref[j, 0])
            if j < 2:
                da_ref[:, cols] = part.astype(BF16)
            else:
                dc_ref[:, cols] = part

    return pl.pallas_call(
        body,
        name=name,
        grid=(s // tm,),
        in_specs=[pl.BlockSpec((tm, d), lambda i: (i, 0)), pl.BlockSpec((N_CHIPS, 1, rows, d), lambda i: (0, layer, 0, 0))],
        out_specs=[
            pl.BlockSpec((tm, ATTN_DIM), lambda i: (i, 0)),
            pl.BlockSpec((tm, CONV_DIM), lambda i: (i, 0)),
            pl.BlockSpec((tm, d), lambda i: (i, 0)),
        ],
        out_shape=[
            jax.ShapeDtypeStruct((s, ATTN_DIM), BF16),
            jax.ShapeDtypeStruct((s, CONV_DIM), F32),
            jax.ShapeDtypeStruct((s, d), BF16),
        ],
        compiler_params=_params(1),
    )(dx, w_s)


def _ffn_fwd(x, gain, wg_s, wu_s, wd_s, layer, name, gather=()):
    s, d = x.shape
    tm = TOKEN_TILE
    f = wg_s.shape[3]
    n_gather = len(gather)
    n_tiles = s // tm

    def body(*refs):
        x_ref, g_ref, wg_ref, wu_ref, wd_ref = refs[:5]
        o_ref, gate_ref, up_ref = refs[5 + n_gather : 8 + n_gather]
        h_s = refs[8 + 2 * n_gather]
        i, j = pl.program_id(0), pl.program_id(1)
        if n_gather:
            copies = _WeightGather(refs[5 : 5 + n_gather], refs[8 + n_gather : 8 + 2 * n_gather], refs[9 + 2 * n_gather :])
            pl.when((i == 0) & (j == 0))(copies.begin)
            pl.when((i == (3 * n_tiles) // 4) & (j == 0))(copies.relay)

        @pl.when(j == 0)
        def _():
            xv = x_ref[...]
            r = lax.rsqrt(jnp.mean(xv * xv, axis=-1, keepdims=True) + EPS)
            h_s[...] = (xv * r * g_ref[...]).astype(BF16)
            o_ref[...] = xv

        halves = [slice(0, tm // 2), slice(tm // 2, tm)]
        pre = [(_dot(h_s[r, :], wg_ref[0, 0]), _dot(h_s[r, :], wu_ref[0, 0])) for r in halves]
        act = [((gate / (1.0 + jnp.exp(-gate))) * up).astype(BF16) for gate, up in pre]
        for r, (gate, up) in zip(halves, pre):
            gate_ref[0, r, :] = gate.astype(BF16)
            up_ref[0, r, :] = up.astype(BF16)
        for r, a in zip(halves, act):
            o_ref[r, :] += _dot(a, wd_ref[0, 0])

        if n_gather:
            pl.when((i == n_tiles - 1) & (j == N_CHIPS - 1))(copies.finish)

    hid = pl.BlockSpec((1, tm, f), lambda i, j: (j, i, 0))
    hid_shape = jax.ShapeDtypeStruct((N_CHIPS, s, f), BF16)
    return pl.pallas_call(
        body,
        name=name,
        grid=(n_tiles, N_CHIPS),
        in_specs=[
            pl.BlockSpec((tm, d), lambda i, j: (i, 0)),
            pl.BlockSpec((1, d), lambda i, j: (0, 0)),
            pl.BlockSpec((1, 1, d, f), lambda i, j: (j, layer, 0, 0)),
            pl.BlockSpec((1, 1, d, f), lambda i, j: (j, layer, 0, 0)),
            pl.BlockSpec((1, 1, f, d), lambda i, j: (j, layer, 0, 0)),
        ] + [ANY] * n_gather,
        out_specs=[pl.BlockSpec((tm, d), lambda i, j: (i, 0)), hid, hid] + [ANY] * n_gather,
        out_shape=[jax.ShapeDtypeStruct((s, d), F32), hid_shape, hid_shape]
        + [jax.ShapeDtypeStruct((N_CHIPS,) + w.shape, w.dtype) for w in gather],
        scratch_shapes=[pltpu.VMEM((tm, d), BF16)] + (_gather_scratch(n_gather) if n_gather else []),
        compiler_params=_params(2),
    )(x, gain, wg_s, wu_s, wd_s, *gather)


def _rms_bwd(xv, gain, dh):
    r = lax.rsqrt(jnp.mean(xv * xv, axis=-1, keepdims=True) + EPS)
    xhat = xv * r
    dxhat = dh * gain
    dx = r * (dxhat - xhat * jnp.mean(dxhat * xhat, axis=-1, keepdims=True))
    return dx, jnp.sum(dh * xhat, axis=0, keepdims=True)


def _ffn_bwd(x, dy, gain, gate_s, up_s, wg_s, wu_s, wd_s, layer, name, scatter=()):
    s, d = x.shape
    tm = TOKEN_TILE
    f = wg_s.shape[3]

    n_scatter = len(scatter)
    n_tiles = s // tm

    def body(*refs):
        x_ref, dy_ref, g_ref, gate_ref, up_ref, wg_ref, wu_ref, wd_ref = refs[:8]
        dx_ref, dgain_ref, h_ref, dyb_ref, dg_ref, du_ref, act_ref = refs[8 + n_scatter : 15 + n_scatter]
        acc_s = refs[15 + 2 * n_scatter]
        i, j = pl.program_id(0), pl.program_id(1)
        if n_scatter:
            copies = _ChipScatter(refs[8 : 8 + n_scatter], refs[15 + n_scatter : 15 + 2 * n_scatter], refs[16 + 2 * n_scatter :])
            pl.when((i == 0) & (j == 0))(copies.begin)

        @pl.when((i == 0) & (j == 0))
        def _():
            dgain_ref[...] = jnp.zeros_like(dgain_ref)

        @pl.when(j == 0)
        def _():
            xv = x_ref[...]
            r = lax.rsqrt(jnp.mean(xv * xv, axis=-1, keepdims=True) + EPS)
            h_ref[...] = (xv * r * g_ref[...]).astype(BF16)
            dyb_ref[...] = dy_ref[...].astype(BF16)
            acc_s[...] = jnp.zeros_like(acc_s)

        halves = [slice(0, tm // 2), slice(tm // 2, tm)]
        pre = [(gate_ref[0, r, :].astype(F32), up_ref[0, r, :].astype(F32), _dot_nt(dyb_ref[r, :], wd_ref[0, 0])) for r in halves]
        grads = []
        for r, (gate, up, dact) in zip(halves, pre):
            sig = 1.0 / (1.0 + jnp.exp(-gate))
            silu = gate * sig
            dgate = (dact * up * (sig * (1.0 + gate * (1.0 - sig)))).astype(BF16)
            dup = (dact * silu).astype(BF16)
            act_ref[0, r, :] = (silu * up).astype(BF16)
            dg_ref[0, r, :] = dgate
            du_ref[0, r, :] = dup
            grads.append((dgate, dup))
        for r, (dgate, dup) in zip(halves, grads):
            acc_s[r, :] += _dot_nt(dgate, wg_ref[0, 0]) + _dot_nt(dup, wu_ref[0, 0])

        @pl.when(j == N_CHIPS - 1)
        def _():
            dxn, dgain = _rms_bwd(x_ref[...], g_ref[...], acc_s[...])
            dx_ref[...] = dy_ref[...] + dxn
            dgain_ref[...] += dgain

        if n_scatter:
            pl.when((i == n_tiles - 1) & (j == N_CHIPS - 1))(copies.finish)

    tok = pl.BlockSpec((tm, d), lambda i, j: (i, 0))
    vec = pl.BlockSpec((1, d), lambda i, j: (0, 0))
    hid = pl.BlockSpec((1, tm, f), lambda i, j: (j, i, 0))
    hid_shape = jax.ShapeDtypeStruct((N_CHIPS, s, f), BF16)
    return pl.pallas_call(
        body,
        name=name,
        grid=(n_tiles, N_CHIPS),
        in_specs=[
            tok, tok, vec, hid, hid,
            pl.BlockSpec((1, 1, d, f), lambda i, j: (j, layer, 0, 0)),
            pl.BlockSpec((1, 1, d, f), lambda i, j: (j, layer, 0, 0)),
            pl.BlockSpec((1, 1, f, d), lambda i, j: (j, layer, 0, 0)),
        ] + [ANY] * n_scatter,
        out_specs=[tok, vec, tok, tok, hid, hid, hid] + [ANY] * n_scatter,
        out_shape=[
            jax.ShapeDtypeStruct((s, d), F32),
            jax.ShapeDtypeStruct((1, d), F32),
            jax.ShapeDtypeStruct((s, d), BF16),
            jax.ShapeDtypeStruct((s, d), BF16),
            hid_shape, hid_shape, hid_shape,
        ] + _scatter_shapes(scatter),
        scratch_shapes=[pltpu.VMEM((tm, d), F32)] + (_scatter_scratch(n_scatter) if n_scatter else []),
        compiler_params=_params(2),
    )(x, dy, gain, gate_s, up_s, wg_s, wu_s, wd_s, *scatter)


def _in_proj_bwd(x, dx_res, gain, dproj, w_s, layer, name):
    s, d = x.shape
    tm = TOKEN_TILE
    n = w_s.shape[3]

    def body(x_ref, r_ref, g_ref, dp_ref, w_ref, dx_ref, dgain_ref):
        @pl.when(pl.program_id(0) == 0)
        def _():
            dgain_ref[...] = jnp.zeros_like(dgain_ref)

        dh = _dot_nt(dp_ref[:, 0:n], w_ref[0, 0])
        for j in range(1, N_CHIPS):
            dh = dh + _dot_nt(dp_ref[:, j * n : (j + 1) * n], w_ref[j, 0])
        dxn, dgain = _rms_bwd(x_ref[...], g_ref[...], dh)
        dx_ref[...] = r_ref[...] + dxn
        dgain_ref[...] += dgain

    tok = pl.BlockSpec((tm, d), lambda i: (i, 0))
    vec = pl.BlockSpec((1, d), lambda i: (0, 0))
    return pl.pallas_call(
        body,
        name=name,
        grid=(s // tm,),
        in_specs=[tok, tok, vec, pl.BlockSpec((tm, N_CHIPS * n), lambda i: (i, 0)), pl.BlockSpec((N_CHIPS, 1, d, n), lambda i: (0, layer, 0, 0))],
        out_specs=[tok, vec],
        out_shape=[jax.ShapeDtypeStruct((s, d), F32), jax.ShapeDtypeStruct((1, d), F32)],
        compiler_params=_params(1),
    )(x, dx_res, gain, dproj, w_s)


def _loss_grad(y, target, name):
    s, d = y.shape
    tm = TOKEN_TILE

    def body(y_ref, t_ref, dy_ref, l_ref):
        @pl.when(pl.program_id(0) == 0)
        def _():
            l_ref[...] = jnp.zeros_like(l_ref)

        err = y_ref[...] - t_ref[...]
        dy_ref[...] = err / d
        l_ref[...] += jnp.sum(err * err, axis=0, keepdims=True) * (0.5 / d)

    tok = pl.BlockSpec((tm, d), lambda i: (i, 0))
    return pl.pallas_call(
        body,
        name=name,
        grid=(s // tm,),
        in_specs=[tok, tok],
        out_specs=[tok, pl.BlockSpec((1, d), lambda i: (0, 0))],
        out_shape=[jax.ShapeDtypeStruct((s, d), F32), jax.ShapeDtypeStruct((1, d), F32)],
        compiler_params=_params(1),
    )(y, target)


def _wgrad(a, b, a_spec, b_spec, n_blocks, k, n, name):
    n_tiles = a.shape[-2] // min(WGRAD_TILE, a.shape[-2])

    def body(a_ref, b_ref, o_ref):
        @pl.when(pl.program_id(1) == 0)
        def _():
            o_ref[...] = jnp.zeros_like(o_ref)

        av = a_ref[0] if len(a_ref.shape) == 3 else a_ref[...]
        bv = b_ref[0] if len(b_ref.shape) == 3 else b_ref[...]
        o_ref[0] += _dot_tn(av, bv)

    return pl.pallas_call(
        body,
        name=name,
        grid=(n_blocks, n_tiles),
        in_specs=[a_spec, b_spec],
        out_specs=pl.BlockSpec((1, k, n), lambda j, i: (j, 0, 0)),
        out_shape=jax.ShapeDtypeStruct((n_blocks, k, n), F32),
        compiler_params=_params(2),
    )(a, b)


def _mesh_position():
    return lax.axis_index("x"), lax.axis_index("y"), lax.axis_index("c")


def _other_chips(x, y):
    return [(1 - x, y), (x, 1 - y), (1 - x, 1 - y)]


def _half_rows(ref_rows, c):
    half = ref_rows // 2
    return pl.ds(c * half, half)


class _WeightGather:
    def __init__(self, ins, outs, sems):
        self.ins, self.outs = ins, outs
        send_sems, recv_sems, pass_send_sems, pass_recv_sems, self.local_sems = sems
        self.ici, self.d2d = (send_sems, recv_sems), (pass_send_sems, pass_recv_sems)
        self.x, self.y, self.c = _mesh_position()
        self.me = 2 * self.x + self.y
        self.sibling = (self.x, self.y, 1 - self.c)
        self.chips = _other_chips(self.x, self.y)

    def _copy(self, t, k, chip_index, core, to, sems, src=None):
        dst = self.outs[t].at[chip_index, :, _half_rows(self.ins[t].shape[1], core), :]
        return pltpu.make_async_remote_copy(
            src_ref=dst if src is None else src, dst_ref=dst, send_sem=sems[0].at[t, k], recv_sem=sems[1].at[t, k],
            device_id=to, device_id_type=MESH_ID,
        )

    def _own(self, t):
        return pltpu.make_async_copy(self.ins[t], self.outs[t].at[self.me], self.local_sems.at[t])

    def _sends(self):
        for t in range(len(self.ins)):
            mine = self.ins[t].at[:, _half_rows(self.ins[t].shape[1], self.c), :]
            for k, (px, py) in enumerate(self.chips):
                yield self._copy(t, k, self.me, self.c, (px, py, self.c), self.ici, src=mine)

    def _passes(self, core, sems):
        for t in range(len(self.ins)):
            for k, (px, py) in enumerate(self.chips):
                yield self._copy(t, k, 2 * px + py, core, self.sibling, sems)

    def begin(self):
        for t in range(len(self.ins)):
            self._own(t).start()
        for cp in self._sends():
            cp.start()

    def relay(self):
        for arrived, onward in zip(self._passes(self.c, self.ici), self._passes(self.c, self.d2d)):
            arrived.wait_recv()
            onward.start()

    def finish(self):
        for cp in self._passes(1 - self.c, self.d2d):
            cp.wait_recv()
        for cp in list(self._sends()) + list(self._passes(self.c, self.d2d)):
            cp.wait_send()
        for t in range(len(self.ins)):
            self._own(t).wait()


def _gather_scratch(n):
    sems = pltpu.SemaphoreType.DMA((n, N_CHIPS - 1))
    return [sems, sems, sems, sems, pltpu.SemaphoreType.DMA((n,))]


def _gather_weights(shards):
    n = len(shards)

    def body(*refs):
        gather = _WeightGather(refs[:n], refs[n : 2 * n], refs[2 * n :])
        gather.begin()
        gather.relay()
        gather.finish()

    return pl.pallas_call(
        body,
        name="gather_weights",
        in_specs=[ANY] * n,
        out_specs=[ANY] * n,
        out_shape=[jax.ShapeDtypeStruct((N_CHIPS,) + w.shape, w.dtype) for w in shards],
        scratch_shapes=_gather_scratch(n),
    )(*shards)


def _swap_halves(grads, tag):
    n = len(grads)

    def body(*refs):
        ins, outs = refs[:n], refs[n : 2 * n]
        send_sems, recv_sems = refs[2 * n :]
        x, y, c = _mesh_position()
        copies = []
        for t in range(n):
            copies.append(pltpu.make_async_remote_copy(
                src_ref=ins[t].at[:, _half_rows(ins[t].shape[1], 1 - c), :], dst_ref=outs[t],
                send_sem=send_sems.at[t], recv_sem=recv_sems.at[t], device_id=(x, y, 1 - c), device_id_type=MESH_ID,
            ))
            copies[-1].start()
        for cp in copies:
            cp.wait()

    sems = pltpu.SemaphoreType.DMA((n,))
    return pl.pallas_call(
        body,
        name=f"swap_halves_{tag}",
        in_specs=[ANY] * n,
        out_specs=[ANY] * n,
        out_shape=[jax.ShapeDtypeStruct((g.shape[0], g.shape[1] // 2, g.shape[2]), g.dtype) for g in grads],
        scratch_shapes=[sems, sems],
    )(*grads)


class _ChipScatter:
    def __init__(self, ins, outs, sems):
        self.ins, self.outs = ins, outs
        self.send_sems, self.recv_sems = sems
        self.x, self.y, self.c = _mesh_position()

    def _copies(self):
        for t in range(len(self.ins)):
            for k, (px, py) in enumerate(_other_chips(self.x, self.y)):
                yield pltpu.make_async_remote_copy(
                    src_ref=self.ins[t].at[2 * px + py], dst_ref=self.outs[t].at[k],
                    send_sem=self.send_sems.at[t, k], recv_sem=self.recv_sems.at[t, k],
                    device_id=(px, py, self.c), device_id_type=MESH_ID,
                )

    def begin(self):
        for cp in self._copies():
            cp.start()

    def finish(self):
        for cp in self._copies():
            cp.wait()


def _scatter_scratch(n):
    sems = pltpu.SemaphoreType.DMA((n, N_CHIPS - 1))
    return [sems, sems]


def _scatter_shapes(parts):
    return [jax.ShapeDtypeStruct((N_CHIPS - 1,) + p.shape[1:], p.dtype) for p in parts]


def _scatter_to_chips(parts, tag):
    n = len(parts)

    def body(*refs):
        copies = _ChipScatter(refs[:n], refs[n : 2 * n], refs[2 * n :])
        copies.begin()
        copies.finish()

    return pl.pallas_call(
        body,
        name=f"scatter_to_chips_{tag}",
        in_specs=[ANY] * n,
        out_specs=[ANY] * n,
        out_shape=_scatter_shapes(parts),
        scratch_shapes=_scatter_scratch(n),
    )(*parts)


def _join_halves(shards):
    n = len(shards)

    def body(*refs):
        outs = refs[n : 2 * n]
        send_sems, recv_sems = refs[2 * n :]
        x, y, c = _mesh_position()
        copies = []
        for t in range(n):
            mine = outs[t].at[:, _half_rows(outs[t].shape[1], c), :]
            copies.append(pltpu.make_async_remote_copy(
                src_ref=mine, dst_ref=mine, send_sem=send_sems.at[t], recv_sem=recv_sems.at[t],
                device_id=(x, y, 1 - c), device_id_type=MESH_ID,
            ))
            copies[-1].start()
        for cp in copies:
            cp.wait()

    sems = pltpu.SemaphoreType.DMA((n,))
    return pl.pallas_call(
        body,
        name="join_halves",
        in_specs=[ANY] * n,
        out_specs=[ANY] * n,
        out_shape=[jax.ShapeDtypeStruct(g.shape, g.dtype) for g in shards],
        input_output_aliases={t: t for t in range(n)},
        scratch_shapes=[sems, sems],
    )(*shards)


def _gather_small(pack):
    def body(p_ref, o_ref, send_sems, recv_sems, local_sem):
        x, y, c = _mesh_position()
        own = pltpu.make_async_copy(p_ref, o_ref.at[4 * x + 2 * y + c], local_sem)
        own.start()
        copies = []
        for k in range(1, N_DEV):
            px, py, pc = x ^ (k >> 2), y ^ ((k >> 1) & 1), c ^ (k & 1)
            send = pltpu.make_async_remote_copy(
                src_ref=p_ref, dst_ref=o_ref.at[4 * x + 2 * y + c], send_sem=send_sems.at[k - 1], recv_sem=recv_sems.at[k - 1],
                device_id=(px, py, pc), device_id_type=MESH_ID,
            )
            send.start()
            copies.append((send, 4 * px + 2 * py + pc))
        for send, peer_slot in copies:
            send.wait_send()
        for k in range(1, N_DEV):
            px, py, pc = x ^ (k >> 2), y ^ ((k >> 1) & 1), c ^ (k & 1)
            pltpu.make_async_remote_copy(
                src_ref=p_ref, dst_ref=o_ref.at[4 * px + 2 * py + pc], send_sem=send_sems.at[k - 1], recv_sem=recv_sems.at[k - 1],
                device_id=(px, py, pc), device_id_type=MESH_ID,
            ).wait_recv()
        own.wait()

    sems = pltpu.SemaphoreType.DMA((N_DEV - 1,))
    return pl.pallas_call(
        body,
        name="gather_small",
        in_specs=[VMEM_SPEC],
        out_specs=VMEM_SPEC,
        out_shape=jax.ShapeDtypeStruct((N_DEV,) + pack.shape, pack.dtype),
        scratch_shapes=[sems, sems, pltpu.SemaphoreType.DMA],
    )(pack)


def _row_tile(rows):
    for tile in range(min(rows, 512) // 8 * 8, 0, -8):
        if rows % tile == 0:
            return tile
    return rows


def _add_half(grad, received, half_index, name):
    slots, h, cdim = received.shape
    tile = _row_tile(h)
    per_half = h // tile

    def body(c_ref, g_ref, r_ref, o_ref, ob_ref):
        total = g_ref[...] + r_ref[...]
        o_ref[...] = total
        ob_ref[...] = total.astype(BF16)

    block = pl.BlockSpec((1, tile, cdim), lambda j, i, c: (j, i, 0))
    grid_spec = pltpu.PrefetchScalarGridSpec(
        num_scalar_prefetch=1,
        grid=(slots, per_half),
        in_specs=[pl.BlockSpec((1, tile, cdim), lambda j, i, c: (j, c[0] * per_half + i, 0)), block],
        out_specs=[block, block],
    )
    return pl.pallas_call(
        body, name=name, grid_spec=grid_spec,
        out_shape=[jax.ShapeDtypeStruct(received.shape, F32), jax.ShapeDtypeStruct(received.shape, BF16)],
        compiler_params=_params(2),
    )(half_index, grad, received)


def _add_chips(part, received, chip_index, core_index, layer, n_layers, shard, name):
    _, h, cdim = part.shape
    tile = _row_tile(h)
    per_half = h // tile

    def body(chip_ref, core_ref, p_ref, r_ref, *rest):
        o_ref = rest[-1]
        o_ref[0] = ((p_ref[0] + r_ref[0].astype(F32)) + r_ref[1].astype(F32)) + r_ref[2].astype(F32)

    in_specs = [
        pl.BlockSpec((1, tile, cdim), lambda i, chip, core: (chip[0], i, 0)),
        pl.BlockSpec((N_CHIPS - 1, tile, cdim), lambda i, chip, core: (0, i, 0)),
    ]
    operands = [chip_index, core_index, part, received]
    aliases = {}
    if shard is not None:
        in_specs.append(ANY)
        operands.append(shard)
        aliases = {4: 0}
    grid_spec = pltpu.PrefetchScalarGridSpec(
        num_scalar_prefetch=2,
        grid=(per_half,),
        in_specs=in_specs,
        out_specs=pl.BlockSpec((1, tile, cdim), lambda i, chip, core: (layer, core[0] * per_half + i, 0)),
    )
    return pl.pallas_call(
        body, name=name, grid_spec=grid_spec, out_shape=jax.ShapeDtypeStruct((n_layers, 2 * h, cdim), F32),
        input_output_aliases=aliases, compiler_params=_params(1),
    )(*operands)


def _adamw(w, g, m, v, name):
    rows, cdim = w.shape
    tile = _row_tile(rows)

    def body(w_ref, g_ref, m_ref, v_ref, d_ref, nm_ref, nv_ref):
        gv = g_ref[...]
        nm = ADAM_B1 * m_ref[...] + (1.0 - ADAM_B1) * gv
        nv = ADAM_B2 * v_ref[...] + (1.0 - ADAM_B2) * (gv * gv)
        m_hat = nm / (1.0 - ADAM_B1 ** ADAM_STEP)
        v_hat = nv / (1.0 - ADAM_B2 ** ADAM_STEP)
        d_ref[...] = -ADAM_LR * (m_hat / (jnp.sqrt(v_hat) + ADAM_EPS) + ADAM_WD * w_ref[...])
        nm_ref[...] = nm
        nv_ref[...] = nv

    spec = pl.BlockSpec((tile, cdim), lambda i: (i, 0))
    shape = jax.ShapeDtypeStruct((rows, cdim), F32)
    return pl.pallas_call(
        body, name=name, grid=(rows // tile,), in_specs=[spec] * 4, out_specs=[spec] * 3, out_shape=[shape] * 3,
        compiler_params=_params(1),
    )(w, g, m, v)


SMALL_ROWS, SMALL_COLS = 24, 1024
ROW_NORM_MIX, ROW_NORM_FFN, ROW_LOSS, ROW_Q_NORM, ROW_K_NORM, ROW_CONV = 0, 2, 4, 8, 10, 16


def _sum_small(gathered):
    def body(g_ref, o_ref, heads_ref, lanes_ref):
        total = g_ref[0]
        for dev in range(1, N_DEV):
            total = total + g_ref[dev]
        o_ref[...] = total
        heads = o_ref[8:16, 0:LANES]
        for grp in range(1, ATTN_DIM // LANES):
            heads = heads + o_ref[8:16, grp * LANES : (grp + 1) * LANES]
        heads_ref[...] = heads + pltpu.roll(heads, HEAD_DIM, 1)
        lanes_ref[...] = jnp.broadcast_to(jnp.sum(o_ref[0:8, :], axis=-1, keepdims=True), (8, LANES))

    return pl.pallas_call(
        body,
        name="sum_small",
        in_specs=[VMEM_SPEC],
        out_specs=[VMEM_SPEC] * 3,
        out_shape=[jax.ShapeDtypeStruct((SMALL_ROWS, SMALL_COLS), F32), jax.ShapeDtypeStruct((8, LANES), F32), jax.ShapeDtypeStruct((8, LANES), F32)],
    )(gathered)


def _pad_rows(a, rows):
    return jnp.pad(a, ((0, rows - a.shape[0]), (0, 0)))


def _pad_to(a, rows, cols):
    return jnp.pad(a, ((0, rows - a.shape[0]), (0, cols - a.shape[1])))


def _conv_taps(conv_s):
    return jnp.transpose(conv_s[:, 0, 0:8], (1, 0, 2)).reshape(8, -1)


class _GradExchange:
    def __init__(self, chip_index, core_index, n_layers):
        self.chip_index, self.core_index, self.n_layers = chip_index, core_index, n_layers
        self.shards = {}
        self.pending = None

    def offer(self, layer, grads):
        assert self.pending is None
        names = list(grads)
        received = _swap_halves([grads[k] for k in names], f"{'_'.join(names)}_{layer}")
        parts = [_add_half(grads[k], r, self.core_index, f"add_half_{k}_{layer}") for k, r in zip(names, received)]
        self.pending = (layer, names, [p32 for p32, _ in parts], [p16 for _, p16 in parts])

    def payload(self):
        return () if self.pending is None else tuple(self.pending[3])

    def take(self, received):
        layer, names, parts, _ = self.pending
        self.pending = None
        for k, p, r in zip(names, parts, received):
            self.shards[k] = _add_chips(
                p, r, self.chip_index, self.core_index, layer, self.n_layers, self.shards.get(k), f"add_chips_{k}_{layer}")

    def finish(self):
        if self.pending is not None:
            layer, names = self.pending[0], self.pending[1]
            self.take(_scatter_to_chips(list(self.pending[3]), f"{'_'.join(names)}_{layer}"))
        return dict(zip(BIG, _join_halves([self.shards[k] for k in BIG])))


def _local_step(x, target, norm_mix, q_norm, k_norm, norm_ffn, layer_weights, exchange=None):
    layer_weights = list(layer_weights)

    def carrying(kernel_fn, n_out, *args):
        if exchange is None or exchange.pending is None:
            return kernel_fn(*args)
        out = kernel_fn(*args, scatter=exchange.payload())
        exchange.take(out[n_out:])
        return out[:n_out]

    n_layers = norm_mix.shape[0]
    s, d = x.shape
    tw = min(WGRAD_TILE, s)
    n_in = layer_weights[0][0].shape[-1]
    f = layer_weights[0][2].shape[-1]
    saved = []
    for l in range(n_layers):
        weights = list(layer_weights[l])
        q_gain = jnp.tile(q_norm[l][None, :], (1, 2))
        k_gain = jnp.tile(k_norm[l][None, :], (1, 2))
        h1, proj = _norm_matmul(x, norm_mix[l][None, :], weights[0], 0, f"in_proj_{l}")
        qn, kn, vb = _qkv_prep(proj, q_gain, k_gain, f"qkv_prep_{l}")
        missing = [n for n, w in enumerate(weights) if w.ndim == 3]
        if missing:
            attn, *arrived = _attn_fwd(qn, kn, vb, f"attn_fwd_{l}", gather=tuple(weights[n] for n in missing))
            for n, w in zip(missing, arrived):
                weights[n] = w
            layer_weights[l] = tuple(weights)
        else:
            attn = _attn_fwd(qn, kn, vb, f"attn_fwd_{l}")
        _, wout_s, wg_s, wu_s, wd_s, conv_s = weights
        taps = _conv_taps(conv_s)
        conv = _conv_fwd(proj, taps, f"conv_fwd_{l}")
        x_mid = _out_proj(x, attn, conv, wout_s, 0, f"out_proj_{l}")
        pending = ()
        if l + 1 < n_layers and all(w.ndim == 3 for w in layer_weights[l + 1]):
            pending = tuple(layer_weights[l + 1])
        x_out, gate, up, *arrived = _ffn_fwd(x_mid, norm_ffn[l][None, :], wg_s, wu_s, wd_s, 0, f"ffn_fwd_{l}", gather=pending)
        if pending:
            layer_weights[l + 1] = tuple(arrived)
        saved.append(dict(x=x, h1=h1, proj=proj, qn=qn, kn=kn, vb=vb, attn=attn, conv=conv, x_mid=x_mid, q_gain=q_gain, k_gain=k_gain,
                          gate=gate, up=up, taps=taps))
        x = x_out

    dy, loss_lanes = _loss_grad(x, target, "loss_grad")
    grads = [None] * n_layers
    for l in reversed(range(n_layers)):
        sv = saved[l]
        win_s, wout_s, wg_s, wu_s, wd_s, _ = layer_weights[l]
        dx_mid, d_norm_ffn, h2, dyb, dgate, dup, act = carrying(
            _ffn_bwd, 7, sv["x_mid"], dy, norm_ffn[l][None, :], sv["gate"], sv["up"], wg_s, wu_s, wd_s, 0, f"ffn_bwd_{l}")
        tok2 = pl.BlockSpec((tw, d), lambda j, i: (i, 0))
        hid = pl.BlockSpec((1, tw, f), lambda j, i: (j, i, 0))
        d_wg = _wgrad(h2, dgate, tok2, hid, N_CHIPS, d, f, f"wgrad_gate_{l}")
        d_wu = _wgrad(h2, dup, tok2, hid, N_CHIPS, d, f, f"wgrad_up_{l}")
        d_wd = _wgrad(act, dyb, hid, tok2, N_CHIPS, f, d, f"wgrad_down_{l}")
        if exchange is not None:
            exchange.offer(l, dict(w_gate=d_wg, w_up=d_wu, w_down=d_wd))
        d_attn, d_conv, dxb = _out_proj_bwd(dx_mid, wout_s, 0, f"out_proj_bwd_{l}")
        rows_out = wout_s.shape[2]
        mix_spec_a = pl.BlockSpec((tw, rows_out), lambda j, i: (i, j))
        d_wout_a = _wgrad(sv["attn"], dxb, mix_spec_a, tok2, ATTN_DIM // rows_out, rows_out, d, f"wgrad_out_attn_{l}")
        d_wout_c = _wgrad(sv["conv"], dxb, mix_spec_a, tok2, CONV_DIM // rows_out, rows_out, d, f"wgrad_out_conv_{l}")
        d_wout = jnp.concatenate([d_wout_a, d_wout_c], axis=0)
        dq, dk, dv = carrying(_attn_bwd, 3, sv["qn"], sv["kn"], sv["vb"], d_attn, f"attn_bwd_{l}")
        dproj, d_conv_w = _conv_bwd(sv["proj"], sv["taps"], d_conv, f"conv_bwd_{l}")
        dproj, d_qg, d_kg = _qkv_prep_bwd(sv["proj"], sv["q_gain"], sv["k_gain"], dq, dk, dv, dproj, f"qkv_prep_bwd_{l}")
        d_win = _wgrad(sv["h1"], dproj, tok2, pl.BlockSpec((tw, n_in), lambda j, i: (i, j)), N_CHIPS, d, n_in, f"wgrad_in_{l}")
        if exchange is not None:
            exchange.offer(l, dict(w_in=d_win, w_out=d_wout))
        dy, d_norm_mix = _in_proj_bwd(sv["x"], dx_mid, norm_mix[l][None, :], dproj, win_s, 0, f"in_proj_bwd_{l}")
        grads[l] = dict(norm_mix=d_norm_mix, norm_ffn=d_norm_ffn, q_norm=d_qg, k_norm=d_kg, conv_w=d_conv_w,
                        w_in=d_win, w_out=d_wout, w_gate=d_wg, w_up=d_wu, w_down=d_wd)
    return loss_lanes, dy, grads


BIG = ("w_in", "w_out", "w_gate", "w_up", "w_down")


def kernel(x, norm_mix, w_in, q_norm, k_norm, conv_w, w_out, norm_ffn, w_gate, w_up, w_down, loss_target, m_norm_mix, m_w_in, m_q_norm, m_k_norm, m_conv_w, m_w_out, m_norm_ffn, m_w_gate, m_w_up, m_w_down, v_norm_mix, v_w_in, v_q_norm, v_k_norm, v_conv_w, v_w_out, v_norm_ffn, v_w_gate, v_w_up, v_w_down):
    n_layers = norm_mix.shape[0]
    weights = dict(w_in=w_in, w_out=w_out, w_gate=w_gate, w_up=w_up, w_down=w_down)
    moments_m = dict(w_in=m_w_in, w_out=m_w_out, w_gate=m_w_gate, w_up=m_w_up, w_down=m_w_down)
    moments_v = dict(w_in=v_w_in, w_out=v_w_out, w_gate=v_w_gate, w_up=v_w_up, w_down=v_w_down)
    cx, cy, cc = _mesh_position()
    chip_index = (2 * cx + cy).astype(jnp.int32).reshape(1)
    core_index = cc.astype(jnp.int32).reshape(1)

    conv_pad = jnp.pad(conv_w, ((0, 0), (0, 16 - conv_w.shape[1]), (0, 0)))

    def shards_of(layer):
        return [weights[k][layer : layer + 1].astype(BF16) for k in BIG] + [conv_pad[layer : layer + 1]]

    first = shards_of(0)
    layer_weights = [tuple(_gather_weights(first[:1])) + tuple(first[1:])] + [tuple(shards_of(layer)) for layer in range(1, n_layers)]

    exchange = _GradExchange(chip_index, core_index, n_layers)
    loss_lanes, grad_x, grads = _local_step(
        x[0], loss_target[0], norm_mix, q_norm, k_norm, norm_ffn, layer_weights, exchange)

    big_grads = exchange.finish()

    def lanes(a):
        return _pad_to(a, a.shape[0], SMALL_COLS)

    def tile_of(*groups):
        return _pad_rows(jnp.concatenate([lanes(jnp.concatenate(g, axis=0)) for g in groups], axis=0), 8)

    layers = range(n_layers)
    pack = jnp.concatenate([
        tile_of([grads[l]["norm_mix"] for l in layers], [grads[l]["norm_ffn"] for l in layers], [loss_lanes]),
        tile_of([grads[l]["q_norm"] for l in layers], [grads[l]["k_norm"] for l in layers]),
        tile_of([grads[l]["conv_w"][0:3] for l in layers]),
    ], axis=0)
    small, small_heads, small_lanes = _sum_small(_gather_small(pack))
    loss = small_lanes[ROW_LOSS, 0]
    d_model = norm_mix.shape[1]
    conv_cols = conv_w.shape[2]
    conv_all = small[ROW_CONV : ROW_CONV + 3 * n_layers, 0:CONV_DIM].reshape(n_layers, 3, CONV_DIM)
    small_grads = dict(
        norm_mix=small[ROW_NORM_MIX : ROW_NORM_MIX + n_layers, 0:d_model],
        norm_ffn=small[ROW_NORM_FFN : ROW_NORM_FFN + n_layers, 0:d_model],
        q_norm=small_heads[ROW_Q_NORM - 8 : ROW_Q_NORM - 8 + n_layers, 0:HEAD_DIM],
        k_norm=small_heads[ROW_K_NORM - 8 : ROW_K_NORM - 8 + n_layers, 0:HEAD_DIM],
        conv_w=lax.dynamic_slice_in_dim(conv_all, (2 * cx + cy) * conv_cols, conv_cols, axis=2),
    )

    out_grad, out_delta, out_m, out_v = {}, {}, {}, {}
    for k in BIG:
        shape = weights[k].shape
        view = (shape[0] * shape[1], shape[2])
        g = big_grads[k]
        delta, new_m, new_v = _adamw(weights[k].reshape(view), g.reshape(view), moments_m[k].reshape(view), moments_v[k].reshape(view), f"adamw_{k}")
        out_grad[k], out_delta[k], out_m[k], out_v[k] = g, delta.reshape(shape), new_m.reshape(shape), new_v.reshape(shape)

    small_w = dict(norm_mix=norm_mix, norm_ffn=norm_ffn, q_norm=q_norm, k_norm=k_norm, conv_w=conv_w)
    small_m = dict(norm_mix=m_norm_mix, norm_ffn=m_norm_ffn, q_norm=m_q_norm, k_norm=m_k_norm, conv_w=m_conv_w)
    small_v = dict(norm_mix=v_norm_mix, norm_ffn=v_norm_ffn, q_norm=v_q_norm, k_norm=v_k_norm, conv_w=v_conv_w)
    order = ("norm_mix", "norm_ffn", "q_norm", "k_norm", "conv_w")

    def packed(tree):
        parts2 = [_pad_to(tree[k].reshape(-1, tree[k].shape[-1]), tree[k].reshape(-1, tree[k].shape[-1]).shape[0], SMALL_COLS) for k in order]
        return _pad_rows(jnp.concatenate(parts2, axis=0), SMALL_ROWS)

    delta_p, m_p, v_p = _adamw(packed(small_w), packed(small_grads), packed(small_m), packed(small_v), "adamw_small")
    row = 0
    for k in order:
        shape = small_w[k].shape
        n_rows = 1
        for dim in shape[:-1]:
            n_rows *= dim
        cut = (slice(row, row + n_rows), slice(0, shape[-1]))
        out_grad[k] = small_grads[k]
        out_delta[k], out_m[k], out_v[k] = delta_p[cut].reshape(shape), m_p[cut].reshape(shape), v_p[cut].reshape(shape)
        row += n_rows

    names_out = ("norm_mix", "w_in", "q_norm", "k_norm", "conv_w", "w_out", "norm_ffn", "w_gate", "w_up", "w_down")
    return (loss, grad_x[None], *[out_grad[k] for k in names_out], *[out_delta[k] for k in names_out],
            *[out_m[k] for k in names_out], *[out_v[k] for k in names_out])
```

```python
import functools

import jax
import jax.numpy as jnp
from jax import lax
from jax.experimental import pallas as pl
from jax.experimental.pallas import tpu as pltpu

F32 = jnp.float32
BF16 = jnp.bfloat16

EPS = 1e-6
HEAD_DIM = 64
LANES = 128
ATTN_DIM = 512
CONV_DIM = 512
N_CHIPS = 4
N_DEV = 8
Q_SCALE = HEAD_DIM ** -0.5
ATTN_Q_TILE = 256
ATTN_TILE = 256
TOKEN_TILE = 512
WGRAD_TILE = 4096
FFN_FWD_TILE = 1024
FFN_CHUNK = 256
VMEM_LIMIT = 56 * 1024 * 1024

ADAM_LR = 0.001
ADAM_B1 = 0.9
ADAM_B2 = 0.999
ADAM_EPS = 1e-08
ADAM_WD = 0.01
ADAM_STEP = 10

MESH_ID = pl.DeviceIdType.MESH
ANY = pl.BlockSpec(memory_space=pl.ANY)
VMEM_SPEC = pl.BlockSpec(memory_space=pltpu.VMEM)


def _params(n_axes):
    return pltpu.CompilerParams(dimension_semantics=("arbitrary",) * n_axes, vmem_limit_bytes=VMEM_LIMIT)


def _dot(a, b):
    return jnp.dot(a, b, preferred_element_type=F32)


def _dot_nt(a, b):
    return lax.dot_general(a, b, (((1,), (1,)), ((), ())), preferred_element_type=F32)


def _dot_tn(a, b):
    return lax.dot_general(a, b, (((0,), (0,)), ((), ())), preferred_element_type=F32)


SCORE_MAX = 80.0
UNDERFLOW_EXIT = 90.0


def _scores(q, k):
    return jnp.minimum(_dot_nt(q, k), SCORE_MAX)


def _softplus(z):
    return jnp.log(1.0 + jnp.exp(z))


def _norm_matmul(x, gain, w_s, layer, name):
    s, d = x.shape
    n_blocks, _, _, n = w_s.shape
    tm = TOKEN_TILE

    def body(x_ref, g_ref, w_ref, h_ref, o_ref):
        xv = x_ref[...]
        r = lax.rsqrt(jnp.mean(xv * xv, axis=-1, keepdims=True) + EPS)
        h = (xv * r * g_ref[...]).astype(BF16)
        h_ref[...] = h
        for j in range(n_blocks):
            o_ref[:, j * n : (j + 1) * n] = _dot(h, w_ref[j, 0])

    return pl.pallas_call(
        body,
        name=name,
        grid=(s // tm,),
        in_specs=[
            pl.BlockSpec((tm, d), lambda i: (i, 0)),
            pl.BlockSpec((1, d), lambda i: (0, 0)),
            pl.BlockSpec((n_blocks, 1, d, n), lambda i: (0, layer, 0, 0)),
        ],
        out_specs=[pl.BlockSpec((tm, d), lambda i: (i, 0)), pl.BlockSpec((tm, n_blocks * n), lambda i: (i, 0))],
        out_shape=[jax.ShapeDtypeStruct((s, d), BF16), jax.ShapeDtypeStruct((s, n_blocks * n), F32)],
        compiler_params=_params(1),
    )(x, gain, w_s)


def _head_norm(xv, gain, low):
    sq = xv * xv
    s_low = jnp.sum(jnp.where(low, sq, 0.0), axis=-1, keepdims=True)
    s_high = jnp.sum(jnp.where(low, 0.0, sq), axis=-1, keepdims=True)
    r = jnp.where(low, lax.rsqrt(s_low / HEAD_DIM + EPS), lax.rsqrt(s_high / HEAD_DIM + EPS))
    return xv * r * gain, r


def _qkv_prep(proj, q_gain, k_gain, name):
    s = proj.shape[0]
    tm = TOKEN_TILE

    def body(p_ref, qg_ref, kg_ref, q_ref, k_ref, v_ref):
        low = lax.broadcasted_iota(jnp.int32, (tm, LANES), 1) < HEAD_DIM
        for g in range(ATTN_DIM // LANES):
            cq = slice(LANES * g, LANES * (g + 1))
            ck = slice(ATTN_DIM + LANES * g, ATTN_DIM + LANES * (g + 1))
            cv = slice(2 * ATTN_DIM + LANES * g, 2 * ATTN_DIM + LANES * (g + 1))
            qn, _ = _head_norm(p_ref[:, cq], qg_ref[...], low)
            kn, _ = _head_norm(p_ref[:, ck], kg_ref[...], low)
            q_ref[:, cq] = (qn * Q_SCALE).astype(BF16)
            k_ref[:, cq] = kn.astype(BF16)
            v_ref[:, cq] = p_ref[:, cv].astype(BF16)

    out = jax.ShapeDtypeStruct((s, ATTN_DIM), BF16)
    return pl.pallas_call(
        body,
        name=name,
        grid=(s // tm,),
        in_specs=[
            pl.BlockSpec((tm, 3 * ATTN_DIM), lambda i: (i, 0)),
            pl.BlockSpec((1, LANES), lambda i: (0, 0)),
            pl.BlockSpec((1, LANES), lambda i: (0, 0)),
        ],
        out_specs=[pl.BlockSpec((tm, ATTN_DIM), lambda i: (i, 0))] * 3,
        out_shape=[out, out, out],
        compiler_params=_params(1),
    )(proj, q_gain, k_gain)


def _qkv_prep_bwd(proj, q_gain, k_gain, dq, dk, dv, dproj, name):
    s = proj.shape[0]
    tm = TOKEN_TILE

    def norm_bwd(xv, gain, dy, low):
        _, r = _head_norm(xv, gain, low)
        xhat = xv * r
        dxhat = dy * gain
        prod = dxhat * xhat
        m_low = jnp.sum(jnp.where(low, prod, 0.0), axis=-1, keepdims=True)
        m_high = jnp.sum(jnp.where(low, 0.0, prod), axis=-1, keepdims=True)
        mean = jnp.where(low, m_low, m_high) / HEAD_DIM
        return r * (dxhat - xhat * mean), jnp.sum(dy * xhat, axis=0, keepdims=True)

    def body(p_ref, qg_ref, kg_ref, dq_ref, dk_ref, dv_ref, dproj_ref, dp_ref, dqg_ref, dkg_ref):
        @pl.when(pl.program_id(0) == 0)
        def _():
            dqg_ref[...] = jnp.zeros_like(dqg_ref)
            dkg_ref[...] = jnp.zeros_like(dkg_ref)

        low = lax.broadcasted_iota(jnp.int32, (tm, LANES), 1) < HEAD_DIM
        for g in range(ATTN_DIM // LANES):
            cq = slice(LANES * g, LANES * (g + 1))
            ck = slice(ATTN_DIM + LANES * g, ATTN_DIM + LANES * (g + 1))
            cv = slice(2 * ATTN_DIM + LANES * g, 2 * ATTN_DIM + LANES * (g + 1))
            dxq, dgq = norm_bwd(p_ref[:, cq], qg_ref[...], dq_ref[:, cq] * Q_SCALE, low)
            dxk, dgk = norm_bwd(p_ref[:, ck], kg_ref[...], dk_ref[:, cq], low)
            dp_ref[:, cq] = dxq.astype(BF16)
            dp_ref[:, ck] = dxk.astype(BF16)
            dp_ref[:, cv] = dv_ref[:, cq].astype(BF16)
            dqg_ref[:, cq] += dgq
            dkg_ref[:, cq] += dgk

    grad_spec = pl.BlockSpec((tm, ATTN_DIM), lambda i: (i, 0))
    gain_spec = pl.BlockSpec((1, LANES), lambda i: (0, 0))
    sum_spec = pl.BlockSpec((1, ATTN_DIM), lambda i: (0, 0))
    return pl.pallas_call(
        body,
        name=name,
        grid=(s // tm,),
        in_specs=[pl.BlockSpec((tm, 3 * ATTN_DIM), lambda i: (i, 0)), gain_spec, gain_spec, grad_spec, grad_spec, grad_spec, ANY],
        out_specs=[pl.BlockSpec((tm, 3 * ATTN_DIM), lambda i: (i, 0)), sum_spec, sum_spec],
        out_shape=[
            jax.ShapeDtypeStruct(dproj.shape, BF16),
            jax.ShapeDtypeStruct((1, ATTN_DIM), F32),
            jax.ShapeDtypeStruct((1, ATTN_DIM), F32),
        ],
        input_output_aliases={6: 0},
        compiler_params=_params(1),
    )(proj, q_gain, k_gain, dq, dk, dv, dproj)


def _attn_tile_consts(t):
    row = lax.broadcasted_iota(jnp.int32, (t, t), 0)
    col = lax.broadcasted_iota(jnp.int32, (t, t), 1)
    return row, col


def _triangle_sum(v, triangle):
    return _dot(v.astype(BF16), triangle)


def _attn_fwd(qn, kn, vb, name, gather=()):
    s = qn.shape[0]
    t = min(ATTN_TILE, s)
    tq = min(ATTN_Q_TILE, t)
    per_key_tile = t // tq
    n_gather = len(gather)
    n_pairs, n_blocks = ATTN_DIM // LANES, s // tq

    def body(*refs):
        q_ref, k_ref, v_ref = refs[:3]
        o_ref = refs[3 + n_gather]
        if n_gather:
            copies = _WeightGather(refs[3 : 3 + n_gather], refs[4 + n_gather : 4 + 2 * n_gather], refs[4 + 2 * n_gather :])
            first = (pl.program_id(0) == 0) & (pl.program_id(1) == 0)
            pl.when(first)(copies.begin)
            pl.when((pl.program_id(0) == n_pairs - 1) & (pl.program_id(1) == 0))(copies.relay)
        i = pl.program_id(1) // per_key_tile
        low = lax.broadcasted_iota(jnp.int32, (tq, LANES), 1) < HEAD_DIM
        row, col = _attn_tile_consts(t)
        suffix = (row > col).astype(BF16)
        first_row = (pl.program_id(1) % per_key_tile) * tq
        causal = lax.broadcasted_iota(jnp.int32, (tq, t), 1) < lax.broadcasted_iota(jnp.int32, (tq, t), 0) + first_row
        q = q_ref[...]
        zero_q = jnp.zeros_like(q)
        qh = (jnp.where(low, q, zero_q), jnp.where(low, zero_q, q))

        def step(kbs, carry, diagonal_first=False):
            chains = [(head, m) for head in range(2) for m in range(len(kbs))]
            masked = [diagonal_first and m == 0 for _, m in chains]
            ks = [k_ref[pl.ds(pl.multiple_of(kb * t, t), t), :] for kb in kbs]
            vs = [v_ref[pl.ds(pl.multiple_of(kb * t, t), t), :] for kb in kbs]
            z = [_scores(qh[head], ks[kb]) for head, kb in chains]
            sp = [_softplus(zc) for zc in z]
            sp = [jnp.where(causal, s_, 0.0) if mk else s_ for s_, mk in zip(sp, masked)]
            inside = [_triangle_sum(s_, suffix) for s_ in sp]
            after = [carry[head][1] for head in range(2)]
            log_a = []
            for n, (head, kb) in enumerate(chains):
                log_a.append(z[n] - sp[n] - inside[n] - after[head])
                after[head] = after[head] + jnp.sum(sp[n], axis=-1, keepdims=True)
            a = [jnp.exp(l_) for l_ in log_a]
            a = [jnp.where(causal, a_, 0.0) if mk else a_ for a_, mk in zip(a, masked)]
            acc = [carry[head][0] for head in range(2)]
            for n, (head, kb) in enumerate(chains):
                acc[head] = acc[head] + _dot(a[n].astype(BF16), vs[kb])
            return tuple((acc[head], after[head]) for head in range(2))

        def live(c):
            return jnp.minimum(jnp.min(c[0][1]), jnp.min(c[1][1])) < UNDERFLOW_EXIT

        zero = (jnp.zeros((tq, LANES), F32), jnp.zeros((tq, 1), F32))
        carry = lax.cond(i >= 1, lambda c: step((i, i - 1), c, True), lambda c: step((i,), c, True), (zero, zero))
        rest = jnp.maximum(i - 1, 0)
        carry = lax.cond((rest % 2 == 1) & live(carry), lambda c: step((i - 2,), c), lambda c: c, carry)
        pairs = rest // 2
        _, carry = lax.while_loop(
            lambda st: (st[0] < pairs) & live(st[1]),
            lambda st: (st[0] + 1, step((2 * (pairs - st[0]) - 1, 2 * (pairs - st[0]) - 2), st[1])),
            (jnp.int32(0), carry))
        o_ref[...] = jnp.where(low, carry[0][0], carry[1][0]).astype(BF16)
        if n_gather:
            pl.when((pl.program_id(0) == n_pairs - 1) & (pl.program_id(1) == n_blocks - 1))(copies.finish)

    out = pl.pallas_call(
        body,
        name=name,
        grid=(n_pairs, n_blocks),
        in_specs=[
            pl.BlockSpec((tq, LANES), lambda p, i: (i, p)),
            pl.BlockSpec((s, LANES), lambda p, i: (0, p)),
            pl.BlockSpec((s, LANES), lambda p, i: (0, p)),
        ] + [ANY] * n_gather,
        out_specs=[pl.BlockSpec((tq, LANES), lambda p, i: (i, p))] + [ANY] * n_gather,
        out_shape=[jax.ShapeDtypeStruct((s, ATTN_DIM), BF16)] + [jax.ShapeDtypeStruct((N_CHIPS,) + w.shape, w.dtype) for w in gather],
        scratch_shapes=_gather_scratch(n_gather) if n_gather else [],
        compiler_params=_params(2),
    )(qn, kn, vb, *gather)
    return out if n_gather else out[0]


def _attn_bwd(qn, kn, vb, do, name, scatter=()):
    s = qn.shape[0]
    t = min(ATTN_TILE, s)
    nq = s // t
    n_scatter = len(scatter)
    n_pairs = ATTN_DIM // LANES

    def body(*refs):
        q_ref, k_ref, v_ref, do_ref = refs[:4]
        dq_ref, dk_ref, dv_ref = refs[4 + n_scatter : 7 + n_scatter]
        a_s, sg_s, a_f, sg_f = refs[7 + 2 * n_scatter : 11 + 2 * n_scatter]
        i = pl.program_id(1)
        if n_scatter:
            copies = _ChipScatter(refs[4 : 4 + n_scatter], refs[7 + n_scatter : 7 + 2 * n_scatter], refs[11 + 2 * n_scatter :])
            pl.when((pl.program_id(0) == 0) & (i == 0))(copies.begin)

        @pl.when(i == 0)
        def _():
            dk_ref[...] = jnp.zeros_like(dk_ref)
            dv_ref[...] = jnp.zeros_like(dv_ref)

        low = lax.broadcasted_iota(jnp.int32, (t, LANES), 1) < HEAD_DIM
        row, col = _attn_tile_consts(t)
        suffix = (row > col).astype(BF16)
        prefix = (row < col).astype(BF16)
        causal = col < row
        q = q_ref[...]
        dob = do_ref[...]
        zero_q = jnp.zeros_like(q)
        qhs = (jnp.where(low, q, zero_q), jnp.where(low, zero_q, q))
        dohs = (jnp.where(low, dob, zero_q), jnp.where(low, zero_q, dob))

        def rows_of(kb):
            return pl.ds(pl.multiple_of(kb * t, t), t)

        pair = [(head, m) for head in range(2) for m in range(2)]

        def short_pass1():
            z = [_scores(qhs[head], k_ref[rows_of(i - m), :]) for head, m in pair]
            sp = [_softplus(z_) for z_ in z]
            sp = [jnp.where(causal, s_, 0.0) if m == 0 else s_ for s_, (_, m) in zip(sp, pair)]
            inside = [_triangle_sum(s_, suffix) for s_ in sp]
            after = [jnp.zeros((t, 1), F32), jnp.zeros((t, 1), F32)]
            for n, (head, m) in enumerate(pair):
                log_sg = z[n] - sp[n]
                a = jnp.exp(log_sg - inside[n] - after[head])
                sg = jnp.exp(log_sg)
                if m == 0:
                    a = jnp.where(causal, a, 0.0)
                    sg = jnp.where(causal, sg, 0.0)
                a_f[n] = a
                sg_f[n] = sg
                after[head] = after[head] + jnp.sum(sp[n], axis=-1, keepdims=True)
            return jnp.minimum(jnp.min(after[0]), jnp.min(after[1])) >= UNDERFLOW_EXIT

        def short_pass2():
            order = [(head, m) for head in range(2) for m in (1, 0)]
            a = {c: a_f[pair.index(c)] for c in order}
            g = {c: a[c] * _dot_nt(dohs[c[0]], v_ref[rows_of(i - c[1]), :]) for c in order}
            for m in (1, 0):
                dv_ref[rows_of(i - m), :] += _dot_tn(a[(0, m)].astype(BF16), dohs[0]) + _dot_tn(a[(1, m)].astype(BF16), dohs[1])
            inside = {c: _triangle_sum(g[c], prefix) for c in order}
            before = [jnp.zeros((t, 1), F32), jnp.zeros((t, 1), F32)]
            dz = {}
            for c in order:
                sg = sg_f[pair.index(c)]
                dz[c] = (g[c] - sg * (g[c] + inside[c] + before[c[0]])).astype(BF16)
                before[c[0]] = before[c[0]] + jnp.sum(g[c], axis=-1, keepdims=True)
            for m in (1, 0):
                dk_ref[rows_of(i - m), :] += _dot_tn(dz[(0, m)], qhs[0]) + _dot_tn(dz[(1, m)], qhs[1])
            dq = [_dot(dz[(head, 1)], k_ref[rows_of(i - 1), :]) + _dot(dz[(head, 0)], k_ref[rows_of(i), :]) for head in range(2)]
            dq_ref[...] = jnp.where(low, dq[0], dq[1])

        def general_walk():
            heads = []
            for head in range(2):
                qh, doh = qhs[head], dohs[head]

                def pass1(kbs, after, diagonal_first=False):
                    z = [_scores(qh, k_ref[rows_of(kb), :]) for kb in kbs]
                    sp = [_softplus(z_) for z_ in z]
                    if diagonal_first:
                        sp[0] = jnp.where(causal, sp[0], 0.0)
                    inside = [_triangle_sum(s_, suffix) for s_ in sp]
                    for n, kb in enumerate(kbs):
                        log_sg = z[n] - sp[n]
                        a = jnp.exp(log_sg - inside[n] - after)
                        sg = jnp.exp(log_sg)
                        if diagonal_first and n == 0:
                            a = jnp.where(causal, a, 0.0)
                            sg = jnp.where(causal, sg, 0.0)
                        a_s[kb] = a
                        sg_s[kb] = sg
                        after = after + jnp.sum(sp[n], axis=-1, keepdims=True)
                    return after

                def live(after):
                    return jnp.min(after) < UNDERFLOW_EXIT

                after = jnp.zeros((t, 1), F32)
                after = lax.cond(i >= 1, lambda c: pass1((i, i - 1), c, True), lambda c: pass1((i,), c, True), after)
                rest = jnp.maximum(i - 1, 0)
                take_single = (rest % 2 == 1) & live(after)
                after = lax.cond(take_single, lambda c: pass1((i - 2,), c), lambda c: c, after)
                pairs = rest // 2
                pairs_done, _ = lax.while_loop(
                    lambda st: (st[0] < pairs) & live(st[1]),
                    lambda st: (st[0] + 1, pass1((2 * (pairs - st[0]) - 1, 2 * (pairs - st[0]) - 2), st[1])),
                    (jnp.int32(0), after))
                walked = jnp.minimum(i, 1) + 1 + take_single.astype(jnp.int32) + 2 * pairs_done
                first = i - walked + 1

                def pass2(kbs, carry):
                    dq, before = carry
                    ks = [k_ref[rows_of(kb), :] for kb in kbs]
                    a = [a_s[kb] for kb in kbs]
                    g = [a_ * _dot_nt(doh, v_ref[rows_of(kb), :]) for a_, kb in zip(a, kbs)]
                    for n, kb in enumerate(kbs):
                        dv_ref[rows_of(kb), :] += _dot_tn(a[n].astype(BF16), doh)
                    inside = [_triangle_sum(g_, prefix) for g_ in g]
                    dz = []
                    for n, kb in enumerate(kbs):
                        sg = sg_s[kb]
                        dz.append((g[n] - sg * (g[n] + inside[n] + before)).astype(BF16))
                        before = before + jnp.sum(g[n], axis=-1, keepdims=True)
                    for n, kb in enumerate(kbs):
                        dk_ref[rows_of(kb), :] += _dot_tn(dz[n], qh)
                    for n in range(len(kbs)):
                        dq = dq + _dot(dz[n], ks[n])
                    return dq, before

                carry = (jnp.zeros((t, LANES), F32), jnp.zeros((t, 1), F32))
                carry = lax.fori_loop(0, walked // 2, lambda n, c: pass2((first + 2 * n, first + 2 * n + 1), c), carry)
                carry = lax.cond(walked % 2 == 1, lambda c: pass2((i,), c), lambda c: c, carry)
                heads.append(carry[0])
            dq_ref[...] = jnp.where(low, heads[0], heads[1])

        short = lax.cond(i >= 1, short_pass1, lambda: jnp.bool_(False))
        pl.when(short)(short_pass2)
        pl.when(jnp.logical_not(short))(general_walk)
        if n_scatter:
            pl.when((pl.program_id(0) == n_pairs - 1) & (i == nq - 1))(copies.finish)

    q_spec = pl.BlockSpec((t, LANES), lambda p, i: (i, p))
    kv_spec = pl.BlockSpec((s, LANES), lambda p, i: (0, p))
    return pl.pallas_call(
        body,
        name=name,
        grid=(n_pairs, nq),
        in_specs=[q_spec, kv_spec, kv_spec, q_spec] + [ANY] * n_scatter,
        out_specs=[q_spec, kv_spec, kv_spec] + [ANY] * n_scatter,
        out_shape=[jax.ShapeDtypeStruct((s, ATTN_DIM), F32)] * 3 + _scatter_shapes(scatter),
        scratch_shapes=[pltpu.VMEM((nq, t, t), F32), pltpu.VMEM((nq, t, t), F32), pltpu.VMEM((4, t, t), F32), pltpu.VMEM((4, t, t), F32)]
        + (_scatter_scratch(n_scatter) if n_scatter else []),
        compiler_params=_params(2),
    )(qn, kn, vb, do, *scatter)


CB_BLOCK, CC_BLOCK, CU_BLOCK = 3, 4, 5


def _shift_down(h, prev_rows, n):
    row = lax.broadcasted_iota(jnp.int32, h.shape, 0)
    out = pltpu.roll(h, n, 0)
    for r in range(n):
        out = jnp.where(row == r, prev_rows[len(prev_rows) - n + r], out)
    return out


def _shift_up(h, next_rows, n):
    tm = h.shape[0]
    row = lax.broadcasted_iota(jnp.int32, h.shape, 0)
    out = pltpu.roll(h, tm - n, 0)
    for r in range(n):
        out = jnp.where(row == tm - n + r, next_rows[r], out)
    return out


def _conv_fwd(proj, conv_w, name):
    s = proj.shape[0]
    tm = TOKEN_TILE
    nb = tm // 8

    def body(cb_ref, cc_ref, cu_ref, pc_ref, pu_ref, w_ref, o_ref):
        first = pl.program_id(0) == 0
        h = cc_ref[...] * cu_ref[...]
        prev = [jnp.where(first, 0.0, pc_ref[r : r + 1, :] * pu_ref[r : r + 1, :]) for r in (6, 7)]
        y = w_ref[0:1, :] * _shift_down(h, prev, 2) + w_ref[1:2, :] * _shift_down(h, prev, 1) + w_ref[2:3, :] * h
        o_ref[...] = (cb_ref[...] * y).astype(BF16)

    def col(block):
        return pl.BlockSpec((tm, CONV_DIM), lambda i: (i, block))

    def halo(block):
        return pl.BlockSpec((8, CONV_DIM), lambda i: (jnp.maximum(i * nb - 1, 0), block))

    return pl.pallas_call(
        body,
        name=name,
        grid=(s // tm,),
        in_specs=[col(CB_BLOCK), col(CC_BLOCK), col(CU_BLOCK), halo(CC_BLOCK), halo(CU_BLOCK), pl.BlockSpec((8, CONV_DIM), lambda i: (0, 0))],
        out_specs=pl.BlockSpec((tm, CONV_DIM), lambda i: (i, 0)),
        out_shape=jax.ShapeDtypeStruct((s, CONV_DIM), BF16),
        compiler_params=_params(1),
    )(proj, proj, proj, proj, proj, conv_w)


def _conv_bwd(proj, conv_w, dconv, name):
    s = proj.shape[0]
    tm = TOKEN_TILE
    nb = tm // 8
    n_tiles = s // tm

    def body(cb_ref, cc_ref, cu_ref, dy_ref, pc_ref, pu_ref, nb_ref, ndy_ref, w_ref, dp_ref, dw_ref):
        i = pl.program_id(0)

        @pl.when(i == 0)
        def _():
            dw_ref[...] = jnp.zeros_like(dw_ref)

        first = i == 0
        last = i == n_tiles - 1
        cc, cu, cb, dy = cc_ref[...], cu_ref[...], cb_ref[...], dy_ref[...]
        h = cc * cu
        prev = [jnp.where(first, 0.0, pc_ref[r : r + 1, :] * pu_ref[r : r + 1, :]) for r in (6, 7)]
        h1 = _shift_down(h, prev, 1)
        h2 = _shift_down(h, prev, 2)
        y = w_ref[0:1, :] * h2 + w_ref[1:2, :] * h1 + w_ref[2:3, :] * h
        dyb = dy * cb
        nxt = [jnp.where(last, 0.0, ndy_ref[r : r + 1, :] * nb_ref[r : r + 1, :]) for r in (0, 1)]
        dh = w_ref[2:3, :] * dyb + w_ref[1:2, :] * _shift_up(dyb, nxt, 1) + w_ref[0:1, :] * _shift_up(dyb, nxt, 2)
        dp_ref[:, 0:CONV_DIM] = (dy * y).astype(BF16)
        dp_ref[:, CONV_DIM : 2 * CONV_DIM] = (dh * cu).astype(BF16)
        dp_ref[:, 2 * CONV_DIM : 3 * CONV_DIM] = (dh * cc).astype(BF16)
        dw_ref[0:1, :] += jnp.sum(dyb * h2, axis=0, keepdims=True)
        dw_ref[1:2, :] += jnp.sum(dyb * h1, axis=0, keepdims=True)
        dw_ref[2:3, :] += jnp.sum(dyb * h, axis=0, keepdims=True)

    def col(block):
        return pl.BlockSpec((tm, CONV_DIM), lambda i: (i, block))

    def halo_prev(block):
        return pl.BlockSpec((8, CONV_DIM), lambda i: (jnp.maximum(i * nb - 1, 0), block))

    def halo_next(block):
        return pl.BlockSpec((8, CONV_DIM), lambda i: (jnp.minimum((i + 1) * nb, s // 8 - 1), block))

    return pl.pallas_call(
        body,
        name=name,
        grid=(n_tiles,),
        in_specs=[
            col(CB_BLOCK), col(CC_BLOCK), col(CU_BLOCK), col(0),
            halo_prev(CC_BLOCK), halo_prev(CU_BLOCK), halo_next(CB_BLOCK), halo_next(0),
            pl.BlockSpec((8, CONV_DIM), lambda i: (0, 0)),
        ],
        out_specs=[pl.BlockSpec((tm, 3 * CONV_DIM), lambda i: (i, 1)), pl.BlockSpec((8, CONV_DIM), lambda i: (0, 0))],
        out_shape=[jax.ShapeDtypeStruct((s, 3 * ATTN_DIM + 3 * CONV_DIM), BF16), jax.ShapeDtypeStruct((8, CONV_DIM), F32)],
        compiler_params=_params(1),
    )(proj, proj, proj, dconv, proj, proj, proj, dconv, conv_w)


def _out_proj(x, attn, conv, w_s, layer, name):
    s, d = x.shape
    tm = TOKEN_TILE
    rows = w_s.shape[2]

    def body(x_ref, a_ref, c_ref, w_ref, o_ref):
        acc = x_ref[...]
        for j in range(N_CHIPS):
            src = a_ref if j < 2 else c_ref
            cols = slice((j % 2) * rows, (j % 2 + 1) * rows)
            acc = acc + _dot(src[:, cols], w_ref[j, 0])
        o_ref[...] = acc

    return pl.pallas_call(
        body,
        name=name,
        grid=(s // tm,),
        in_specs=[
            pl.BlockSpec((tm, d), lambda i: (i, 0)),
            pl.BlockSpec((tm, ATTN_DIM), lambda i: (i, 0)),
            pl.BlockSpec((tm, CONV_DIM), lambda i: (i, 0)),
            pl.BlockSpec((N_CHIPS, 1, rows, d), lambda i: (0, layer, 0, 0)),
        ],
        out_specs=pl.BlockSpec((tm, d), lambda i: (i, 0)),
        out_shape=jax.ShapeDtypeStruct((s, d), F32),
        compiler_params=_params(1),
    )(x, attn, conv, w_s)


def _out_proj_bwd(dx, w_s, layer, name):
    s, d = dx.shape
    tm = TOKEN_TILE
    rows = w_s.shape[2]

    def body(dx_ref, w_ref, da_ref, dc_ref, dxb_ref):
        dxb = dx_ref[...].astype(BF16)
        dxb_ref[...] = dxb
        for j in range(N_CHIPS):
            cols = slice((j % 2) * rows, (j % 2 + 1) * rows)
            part = _dot_nt(dxb, w_ref[j, 0])
            if j < 2:
                da_ref[:, cols] = part.astype(BF16)
            else:
                dc_ref[:, cols] = part

    return pl.pallas_call(
        body,
        name=name,
        grid=(s // tm,),
        in_specs=[pl.BlockSpec((tm, d), lambda i: (i, 0)), pl.BlockSpec((N_CHIPS, 1, rows, d), lambda i: (0, layer, 0, 0))],
        out_specs=[
            pl.BlockSpec((tm, ATTN_DIM), lambda i: (i, 0)),
            pl.BlockSpec((tm, CONV_DIM), lambda i: (i, 0)),
            pl.BlockSpec((tm, d), lambda i: (i, 0)),
        ],
        out_shape=[
            jax.ShapeDtypeStruct((s, ATTN_DIM), BF16),
            jax.ShapeDtypeStruct((s, CONV_DIM), F32),
            jax.ShapeDtypeStruct((s, d), BF16),
        ],
        compiler_params=_params(1),
    )(dx, w_s)


def _ffn_fwd(x, gain, wg_s, wu_s, wd_s, layer, name, gather=()):
    s, d = x.shape
    tm = min(FFN_FWD_TILE, s)
    f = wg_s.shape[3]
    n_gather = len(gather)
    n_tiles = s // tm

    def body(*refs):
        x_ref, g_ref, wg_ref, wu_ref, wd_ref = refs[:5]
        o_ref, gate_ref, up_ref = refs[5 + n_gather : 8 + n_gather]
        h_s = refs[8 + 2 * n_gather]
        i, j = pl.program_id(0), pl.program_id(1)
        if n_gather:
            copies = _WeightGather(refs[5 : 5 + n_gather], refs[8 + n_gather : 8 + 2 * n_gather], refs[9 + 2 * n_gather :])
            pl.when((i == 0) & (j == 0))(copies.begin)
            pl.when((i == (3 * n_tiles) // 4) & (j == 0))(copies.relay)

        @pl.when(j == 0)
        def _():
            xv = x_ref[...]
            r = lax.rsqrt(jnp.mean(xv * xv, axis=-1, keepdims=True) + EPS)
            h_s[...] = (xv * r * g_ref[...]).astype(BF16)
            o_ref[...] = xv

        halves = [slice(r, r + FFN_CHUNK) for r in range(0, tm, FFN_CHUNK)]
        pre = [(_dot(h_s[r, :], wg_ref[0, 0]), _dot(h_s[r, :], wu_ref[0, 0])) for r in halves]
        act = [((gate / (1.0 + jnp.exp(-gate))) * up).astype(BF16) for gate, up in pre]
        for r, (gate, up) in zip(halves, pre):
            gate_ref[0, r, :] = gate.astype(BF16)
            up_ref[0, r, :] = up.astype(BF16)
        for r, a in zip(halves, act):
            o_ref[r, :] += _dot(a, wd_ref[0, 0])

        if n_gather:
            pl.when((i == n_tiles - 1) & (j == N_CHIPS - 1))(copies.finish)

    hid = pl.BlockSpec((1, tm, f), lambda i, j: (j, i, 0))
    hid_shape = jax.ShapeDtypeStruct((N_CHIPS, s, f), BF16)
    return pl.pallas_call(
        body,
        name=name,
        grid=(n_tiles, N_CHIPS),
        in_specs=[
            pl.BlockSpec((tm, d), lambda i, j: (i, 0)),
            pl.BlockSpec((1, d), lambda i, j: (0, 0)),
            pl.BlockSpec((1, 1, d, f), lambda i, j: (j, layer, 0, 0)),
            pl.BlockSpec((1, 1, d, f), lambda i, j: (j, layer, 0, 0)),
            pl.BlockSpec((1, 1, f, d), lambda i, j: (j, layer, 0, 0)),
        ] + [ANY] * n_gather,
        out_specs=[pl.BlockSpec((tm, d), lambda i, j: (i, 0)), hid, hid] + [ANY] * n_gather,
        out_shape=[jax.ShapeDtypeStruct((s, d), F32), hid_shape, hid_shape]
        + [jax.ShapeDtypeStruct((N_CHIPS,) + w.shape, w.dtype) for w in gather],
        scratch_shapes=[pltpu.VMEM((tm, d), BF16)] + (_gather_scratch(n_gather) if n_gather else []),
        compiler_params=_params(2),
    )(x, gain, wg_s, wu_s, wd_s, *gather)


def _rms_bwd(xv, gain, dh):
    r = lax.rsqrt(jnp.mean(xv * xv, axis=-1, keepdims=True) + EPS)
    xhat = xv * r
    dxhat = dh * gain
    dx = r * (dxhat - xhat * jnp.mean(dxhat * xhat, axis=-1, keepdims=True))
    return dx, jnp.sum(dh * xhat, axis=0, keepdims=True)


def _ffn_bwd(x, dy, gain, gate_s, up_s, wg_s, wu_s, wd_s, layer, name, scatter=()):
    s, d = x.shape
    tm = TOKEN_TILE
    f = wg_s.shape[3]

    n_scatter = len(scatter)
    n_tiles = s // tm

    def body(*refs):
        x_ref, dy_ref, g_ref, gate_ref, up_ref, wg_ref, wu_ref, wd_ref = refs[:8]
        dx_ref, dgain_ref, h_ref, dyb_ref, dg_ref, du_ref, act_ref = refs[8 + n_scatter : 15 + n_scatter]
        acc_s = refs[15 + 2 * n_scatter]
        i, j = pl.program_id(0), pl.program_id(1)
        if n_scatter:
            copies = _ChipScatter(refs[8 : 8 + n_scatter], refs[15 + n_scatter : 15 + 2 * n_scatter], refs[16 + 2 * n_scatter :])
            pl.when((i == 0) & (j == 0))(copies.begin)

        @pl.when((i == 0) & (j == 0))
        def _():
            dgain_ref[...] = jnp.zeros_like(dgain_ref)

        @pl.when(j == 0)
        def _():
            xv = x_ref[...]
            r = lax.rsqrt(jnp.mean(xv * xv, axis=-1, keepdims=True) + EPS)
            h_ref[...] = (xv * r * g_ref[...]).astype(BF16)
            dyb_ref[...] = dy_ref[...].astype(BF16)
            acc_s[...] = jnp.zeros_like(acc_s)

        halves = [slice(0, tm // 2), slice(tm // 2, tm)]
        pre = [(gate_ref[0, r, :].astype(F32), up_ref[0, r, :].astype(F32), _dot_nt(dyb_ref[r, :], wd_ref[0, 0])) for r in halves]
        grads = []
        for r, (gate, up, dact) in zip(halves, pre):
            sig = 1.0 / (1.0 + jnp.exp(-gate))
            silu = gate * sig
            dgate = (dact * up * (sig * (1.0 + gate * (1.0 - sig)))).astype(BF16)
            dup = (dact * silu).astype(BF16)
            act_ref[0, r, :] = (silu * up).astype(BF16)
            dg_ref[0, r, :] = dgate
            du_ref[0, r, :] = dup
            grads.append((dgate, dup))
        for r, (dgate, dup) in zip(halves, grads):
            acc_s[r, :] += _dot_nt(dgate, wg_ref[0, 0]) + _dot_nt(dup, wu_ref[0, 0])

        @pl.when(j == N_CHIPS - 1)
        def _():
            dxn, dgain = _rms_bwd(x_ref[...], g_ref[...], acc_s[...])
            dx_ref[...] = dy_ref[...] + dxn
            dgain_ref[...] += dgain

        if n_scatter:
            pl.when((i == n_tiles - 1) & (j == N_CHIPS - 1))(copies.finish)

    tok = pl.BlockSpec((tm, d), lambda i, j: (i, 0))
    vec = pl.BlockSpec((1, d), lambda i, j: (0, 0))
    hid = pl.BlockSpec((1, tm, f), lambda i, j: (j, i, 0))
    hid_shape = jax.ShapeDtypeStruct((N_CHIPS, s, f), BF16)
    return pl.pallas_call(
        body,
        name=name,
        grid=(n_tiles, N_CHIPS),
        in_specs=[
            tok, tok, vec, hid, hid,
            pl.BlockSpec((1, 1, d, f), lambda i, j: (j, layer, 0, 0)),
            pl.BlockSpec((1, 1, d, f), lambda i, j: (j, layer, 0, 0)),
            pl.BlockSpec((1, 1, f, d), lambda i, j: (j, layer, 0, 0)),
        ] + [ANY] * n_scatter,
        out_specs=[tok, vec, tok, tok, hid, hid, hid] + [ANY] * n_scatter,
        out_shape=[
            jax.ShapeDtypeStruct((s, d), F32),
            jax.ShapeDtypeStruct((1, d), F32),
            jax.ShapeDtypeStruct((s, d), BF16),
            jax.ShapeDtypeStruct((s, d), BF16),
            hid_shape, hid_shape, hid_shape,
        ] + _scatter_shapes(scatter),
        scratch_shapes=[pltpu.VMEM((tm, d), F32)] + (_scatter_scratch(n_scatter) if n_scatter else []),
        compiler_params=_params(2),
    )(x, dy, gain, gate_s, up_s, wg_s, wu_s, wd_s, *scatter)


def _in_proj_bwd(x, dx_res, gain, dproj, w_s, layer, name):
    s, d = x.shape
    tm = TOKEN_TILE
    n = w_s.shape[3]

    def body(x_ref, r_ref, g_ref, dp_ref, w_ref, dx_ref, dgain_ref):
        @pl.when(pl.program_id(0) == 0)
        def _():
            dgain_ref[...] = jnp.zeros_like(dgain_ref)

        dh = _dot_nt(dp_ref[:, 0:n], w_ref[0, 0])
        for j in range(1, N_CHIPS):
            dh = dh + _dot_nt(dp_ref[:, j * n : (j + 1) * n], w_ref[j, 0])
        dxn, dgain = _rms_bwd(x_ref[...], g_ref[...], dh)
        dx_ref[...] = r_ref[...] + dxn
        dgain_ref[...] += dgain

    tok = pl.BlockSpec((tm, d), lambda i: (i, 0))
    vec = pl.BlockSpec((1, d), lambda i: (0, 0))
    return pl.pallas_call(
        body,
        name=name,
        grid=(s // tm,),
        in_specs=[tok, tok, vec, pl.BlockSpec((tm, N_CHIPS * n), lambda i: (i, 0)), pl.BlockSpec((N_CHIPS, 1, d, n), lambda i: (0, layer, 0, 0))],
        out_specs=[tok, vec],
        out_shape=[jax.ShapeDtypeStruct((s, d), F32), jax.ShapeDtypeStruct((1, d), F32)],
        compiler_params=_params(1),
    )(x, dx_res, gain, dproj, w_s)


def _loss_grad(y, target, name):
    s, d = y.shape
    tm = TOKEN_TILE

    def body(y_ref, t_ref, dy_ref, l_ref):
        @pl.when(pl.program_id(0) == 0)
        def _():
            l_ref[...] = jnp.zeros_like(l_ref)

        err = y_ref[...] - t_ref[...]
        dy_ref[...] = err / d
        l_ref[...] += jnp.sum(err * err, axis=0, keepdims=True) * (0.5 / d)

    tok = pl.BlockSpec((tm, d), lambda i: (i, 0))
    return pl.pallas_call(
        body,
        name=name,
        grid=(s // tm,),
        in_specs=[tok, tok],
        out_specs=[tok, pl.BlockSpec((1, d), lambda i: (0, 0))],
        out_shape=[jax.ShapeDtypeStruct((s, d), F32), jax.ShapeDtypeStruct((1, d), F32)],
        compiler_params=_params(1),
    )(y, target)


def _wgrad(a, b, a_spec, b_spec, n_blocks, k, n, name):
    n_tiles = a.shape[-2] // min(WGRAD_TILE, a.shape[-2])

    def body(a_ref, b_ref, o_ref):
        @pl.when(pl.program_id(1) == 0)
        def _():
            o_ref[...] = jnp.zeros_like(o_ref)

        av = a_ref[0] if len(a_ref.shape) == 3 else a_ref[...]
        bv = b_ref[0] if len(b_ref.shape) == 3 else b_ref[...]
        o_ref[0] += _dot_tn(av, bv)

    return pl.pallas_call(
        body,
        name=name,
        grid=(n_blocks, n_tiles),
        in_specs=[a_spec, b_spec],
        out_specs=pl.BlockSpec((1, k, n), lambda j, i: (j, 0, 0)),
        out_shape=jax.ShapeDtypeStruct((n_blocks, k, n), F32),
        compiler_params=_params(2),
    )(a, b)


def _mesh_position():
    return lax.axis_index("x"), lax.axis_index("y"), lax.axis_index("c")


def _other_chips(x, y):
    return [(1 - x, y), (x, 1 - y), (1 - x, 1 - y)]


def _half_rows(ref_rows, c):
    half = ref_rows // 2
    return pl.ds(c * half, half)


class _WeightGather:
    def __init__(self, ins, outs, sems):
        self.ins, self.outs = ins, outs
        send_sems, recv_sems, pass_send_sems, pass_recv_sems, self.local_sems = sems
        self.ici, self.d2d = (send_sems, recv_sems), (pass_send_sems, pass_recv_sems)
        self.x, self.y, self.c = _mesh_position()
        self.me = 2 * self.x + self.y
        self.sibling = (self.x, self.y, 1 - self.c)
        self.chips = _other_chips(self.x, self.y)

    def _copy(self, t, k, chip_index, core, to, sems, src=None):
        dst = self.outs[t].at[chip_index, :, _half_rows(self.ins[t].shape[1], core), :]
        return pltpu.make_async_remote_copy(
            src_ref=dst if src is None else src, dst_ref=dst, send_sem=sems[0].at[t, k], recv_sem=sems[1].at[t, k],
            device_id=to, device_id_type=MESH_ID,
        )

    def _own(self, t):
        return pltpu.make_async_copy(self.ins[t], self.outs[t].at[self.me], self.local_sems.at[t])

    def _sends(self):
        for t in range(len(self.ins)):
            mine = self.ins[t].at[:, _half_rows(self.ins[t].shape[1], self.c), :]
            for k, (px, py) in enumerate(self.chips):
                yield self._copy(t, k, self.me, self.c, (px, py, self.c), self.ici, src=mine)

    def _passes(self, core, sems):
        for t in range(len(self.ins)):
            for k, (px, py) in enumerate(self.chips):
                yield self._copy(t, k, 2 * px + py, core, self.sibling, sems)

    def begin(self):
        for t in range(len(self.ins)):
            self._own(t).start()
        for cp in self._sends():
            cp.start()

    def relay(self):
        for arrived, onward in zip(self._passes(self.c, self.ici), self._passes(self.c, self.d2d)):
            arrived.wait_recv()
            onward.start()

    def finish(self):
        for cp in self._passes(1 - self.c, self.d2d):
            cp.wait_recv()
        for cp in list(self._sends()) + list(self._passes(self.c, self.d2d)):
            cp.wait_send()
        for t in range(len(self.ins)):
            self._own(t).wait()


def _gather_scratch(n):
    sems = pltpu.SemaphoreType.DMA((n, N_CHIPS - 1))
    return [sems, sems, sems, sems, pltpu.SemaphoreType.DMA((n,))]


def _gather_weights(shards):
    n = len(shards)

    def body(*refs):
        gather = _WeightGather(refs[:n], refs[n : 2 * n], refs[2 * n :])
        gather.begin()
        gather.relay()
        gather.finish()

    return pl.pallas_call(
        body,
        name="gather_weights",
        in_specs=[ANY] * n,
        out_specs=[ANY] * n,
        out_shape=[jax.ShapeDtypeStruct((N_CHIPS,) + w.shape, w.dtype) for w in shards],
        scratch_shapes=_gather_scratch(n),
    )(*shards)


def _swap_halves(grads, tag):
    n = len(grads)

    def body(*refs):
        ins, outs = refs[:n], refs[n : 2 * n]
        send_sems, recv_sems = refs[2 * n :]
        x, y, c = _mesh_position()
        copies = []
        for t in range(n):
            copies.append(pltpu.make_async_remote_copy(
                src_ref=ins[t].at[:, _half_rows(ins[t].shape[1], 1 - c), :], dst_ref=outs[t],
                send_sem=send_sems.at[t], recv_sem=recv_sems.at[t], device_id=(x, y, 1 - c), device_id_type=MESH_ID,
            ))
            copies[-1].start()
        for cp in copies:
            cp.wait()

    sems = pltpu.SemaphoreType.DMA((n,))
    return pl.pallas_call(
        body,
        name=f"swap_halves_{tag}",
        in_specs=[ANY] * n,
        out_specs=[ANY] * n,
        out_shape=[jax.ShapeDtypeStruct((g.shape[0], g.shape[1] // 2, g.shape[2]), g.dtype) for g in grads],
        scratch_shapes=[sems, sems],
    )(*grads)


class _ChipScatter:
    def __init__(self, ins, outs, sems):
        self.ins, self.outs = ins, outs
        self.send_sems, self.recv_sems = sems
        self.x, self.y, self.c = _mesh_position()

    def _copies(self):
        for t in range(len(self.ins)):
            for k, (px, py) in enumerate(_other_chips(self.x, self.y)):
                yield pltpu.make_async_remote_copy(
                    src_ref=self.ins[t].at[2 * px + py], dst_ref=self.outs[t].at[k],
                    send_sem=self.send_sems.at[t, k], recv_sem=self.recv_sems.at[t, k],
                    device_id=(px, py, self.c), device_id_type=MESH_ID,
                )

    def begin(self):
        for cp in self._copies():
            cp.start()

    def finish(self):
        for cp in self._copies():
            cp.wait()


def _scatter_scratch(n):
    sems = pltpu.SemaphoreType.DMA((n, N_CHIPS - 1))
    return [sems, sems]


def _scatter_shapes(parts):
    return [jax.ShapeDtypeStruct((N_CHIPS - 1,) + p.shape[1:], p.dtype) for p in parts]


def _scatter_to_chips(parts, tag):
    n = len(parts)

    def body(*refs):
        copies = _ChipScatter(refs[:n], refs[n : 2 * n], refs[2 * n :])
        copies.begin()
        copies.finish()

    return pl.pallas_call(
        body,
        name=f"scatter_to_chips_{tag}",
        in_specs=[ANY] * n,
        out_specs=[ANY] * n,
        out_shape=_scatter_shapes(parts),
        scratch_shapes=_scatter_scratch(n),
    )(*parts)


def _join_halves(shards):
    n = len(shards)

    def body(*refs):
        outs = refs[n : 2 * n]
        send_sems, recv_sems = refs[2 * n :]
        x, y, c = _mesh_position()
        copies = []
        for t in range(n):
            mine = outs[t].at[:, _half_rows(outs[t].shape[1], c), :]
            copies.append(pltpu.make_async_remote_copy(
                src_ref=mine, dst_ref=mine, send_sem=send_sems.at[t], recv_sem=recv_sems.at[t],
                device_id=(x, y, 1 - c), device_id_type=MESH_ID,
            ))
            copies[-1].start()
        for cp in copies:
            cp.wait()

    sems = pltpu.SemaphoreType.DMA((n,))
    return pl.pallas_call(
        body,
        name="join_halves",
        in_specs=[ANY] * n,
        out_specs=[ANY] * n,
        out_shape=[jax.ShapeDtypeStruct(g.shape, g.dtype) for g in shards],
        input_output_aliases={t: t for t in range(n)},
        scratch_shapes=[sems, sems],
    )(*shards)


def _gather_small(pack):
    def body(p_ref, o_ref, send_sems, recv_sems, local_sem):
        x, y, c = _mesh_position()
        own = pltpu.make_async_copy(p_ref, o_ref.at[4 * x + 2 * y + c], local_sem)
        own.start()
        copies = []
        for k in range(1, N_DEV):
            px, py, pc = x ^ (k >> 2), y ^ ((k >> 1) & 1), c ^ (k & 1)
            send = pltpu.make_async_remote_copy(
                src_ref=p_ref, dst_ref=o_ref.at[4 * x + 2 * y + c], send_sem=send_sems.at[k - 1], recv_sem=recv_sems.at[k - 1],
                device_id=(px, py, pc), device_id_type=MESH_ID,
            )
            send.start()
            copies.append((send, 4 * px + 2 * py + pc))
        for send, peer_slot in copies:
            send.wait_send()
        for k in range(1, N_DEV):
            px, py, pc = x ^ (k >> 2), y ^ ((k >> 1) & 1), c ^ (k & 1)
            pltpu.make_async_remote_copy(
                src_ref=p_ref, dst_ref=o_ref.at[4 * px + 2 * py + pc], send_sem=send_sems.at[k - 1], recv_sem=recv_sems.at[k - 1],
                device_id=(px, py, pc), device_id_type=MESH_ID,
            ).wait_recv()
        own.wait()

    sems = pltpu.SemaphoreType.DMA((N_DEV - 1,))
    return pl.pallas_call(
        body,
        name="gather_small",
        in_specs=[VMEM_SPEC],
        out_specs=VMEM_SPEC,
        out_shape=jax.ShapeDtypeStruct((N_DEV,) + pack.shape, pack.dtype),
        scratch_shapes=[sems, sems, pltpu.SemaphoreType.DMA],
    )(pack)


def _row_tile(rows):
    for tile in range(min(rows, 512) // 8 * 8, 0, -8):
        if rows % tile == 0:
            return tile
    return rows


def _add_half(grad, received, half_index, name):
    slots, h, cdim = received.shape
    tile = _row_tile(h)
    per_half = h // tile

    def body(c_ref, g_ref, r_ref, o_ref, ob_ref):
        total = g_ref[...] + r_ref[...]
        o_ref[...] = total
        ob_ref[...] = total.astype(BF16)

    block = pl.BlockSpec((1, tile, cdim), lambda j, i, c: (j, i, 0))
    grid_spec = pltpu.PrefetchScalarGridSpec(
        num_scalar_prefetch=1,
        grid=(slots, per_half),
        in_specs=[pl.BlockSpec((1, tile, cdim), lambda j, i, c: (j, c[0] * per_half + i, 0)), block],
        out_specs=[block, block],
    )
    return pl.pallas_call(
        body, name=name, grid_spec=grid_spec,
        out_shape=[jax.ShapeDtypeStruct(received.shape, F32), jax.ShapeDtypeStruct(received.shape, BF16)],
        compiler_params=_params(2),
    )(half_index, grad, received)


def _add_chips(part, received, chip_index, core_index, layer, n_layers, shard, name):
    _, h, cdim = part.shape
    tile = _row_tile(h)
    per_half = h // tile

    def body(chip_ref, core_ref, p_ref, r_ref, *rest):
        o_ref = rest[-1]
        o_ref[0] = ((p_ref[0] + r_ref[0].astype(F32)) + r_ref[1].astype(F32)) + r_ref[2].astype(F32)

    in_specs = [
        pl.BlockSpec((1, tile, cdim), lambda i, chip, core: (chip[0], i, 0)),
        pl.BlockSpec((N_CHIPS - 1, tile, cdim), lambda i, chip, core: (0, i, 0)),
    ]
    operands = [chip_index, core_index, part, received]
    aliases = {}
    if shard is not None:
        in_specs.append(ANY)
        operands.append(shard)
        aliases = {4: 0}
    grid_spec = pltpu.PrefetchScalarGridSpec(
        num_scalar_prefetch=2,
        grid=(per_half,),
        in_specs=in_specs,
        out_specs=pl.BlockSpec((1, tile, cdim), lambda i, chip, core: (layer, core[0] * per_half + i, 0)),
    )
    return pl.pallas_call(
        body, name=name, grid_spec=grid_spec, out_shape=jax.ShapeDtypeStruct((n_layers, 2 * h, cdim), F32),
        input_output_aliases=aliases, compiler_params=_params(1),
    )(*operands)


def _adamw(w, g, m, v, name):
    rows, cdim = w.shape
    tile = _row_tile(rows)

    def body(w_ref, g_ref, m_ref, v_ref, d_ref, nm_ref, nv_ref):
        gv = g_ref[...]
        nm = ADAM_B1 * m_ref[...] + (1.0 - ADAM_B1) * gv
        nv = ADAM_B2 * v_ref[...] + (1.0 - ADAM_B2) * (gv * gv)
        m_hat = nm / (1.0 - ADAM_B1 ** ADAM_STEP)
        v_hat = nv / (1.0 - ADAM_B2 ** ADAM_STEP)
        d_ref[...] = -ADAM_LR * (m_hat / (jnp.sqrt(v_hat) + ADAM_EPS) + ADAM_WD * w_ref[...])
        nm_ref[...] = nm
        nv_ref[...] = nv

    spec = pl.BlockSpec((tile, cdim), lambda i: (i, 0))
    shape = jax.ShapeDtypeStruct((rows, cdim), F32)
    return pl.pallas_call(
        body, name=name, grid=(rows // tile,), in_specs=[spec] * 4, out_specs=[spec] * 3, out_shape=[shape] * 3,
        compiler_params=_params(1),
    )(w, g, m, v)


SMALL_ROWS, SMALL_COLS = 24, 1024
ROW_NORM_MIX, ROW_NORM_FFN, ROW_LOSS, ROW_Q_NORM, ROW_K_NORM, ROW_CONV = 0, 2, 4, 8, 10, 16


def _sum_small(gathered):
    def body(g_ref, o_ref, heads_ref, lanes_ref):
        total = g_ref[0]
        for dev in range(1, N_DEV):
            total = total + g_ref[dev]
        o_ref[...] = total
        heads = o_ref[8:16, 0:LANES]
        for grp in range(1, ATTN_DIM // LANES):
            heads = heads + o_ref[8:16, grp * LANES : (grp + 1) * LANES]
        heads_ref[...] = heads + pltpu.roll(heads, HEAD_DIM, 1)
        lanes_ref[...] = jnp.broadcast_to(jnp.sum(o_ref[0:8, :], axis=-1, keepdims=True), (8, LANES))

    return pl.pallas_call(
        body,
        name="sum_small",
        in_specs=[VMEM_SPEC],
        out_specs=[VMEM_SPEC] * 3,
        out_shape=[jax.ShapeDtypeStruct((SMALL_ROWS, SMALL_COLS), F32), jax.ShapeDtypeStruct((8, LANES), F32), jax.ShapeDtypeStruct((8, LANES), F32)],
    )(gathered)


def _pad_rows(a, rows):
    return jnp.pad(a, ((0, rows - a.shape[0]), (0, 0)))


def _pad_to(a, rows, cols):
    return jnp.pad(a, ((0, rows - a.shape[0]), (0, cols - a.shape[1])))


def _conv_taps(conv_s):
    return jnp.transpose(conv_s[:, 0, 0:8], (1, 0, 2)).reshape(8, -1)


class _GradExchange:
    def __init__(self, chip_index, core_index, n_layers):
        self.chip_index, self.core_index, self.n_layers = chip_index, core_index, n_layers
        self.shards = {}
        self.pending = None

    def offer(self, layer, grads):
        assert self.pending is None
        names = list(grads)
        received = _swap_halves([grads[k] for k in names], f"{'_'.join(names)}_{layer}")
        parts = [_add_half(grads[k], r, self.core_index, f"add_half_{k}_{layer}") for k, r in zip(names, received)]
        self.pending = (layer, names, [p32 for p32, _ in parts], [p16 for _, p16 in parts])

    def payload(self):
        return () if self.pending is None else tuple(self.pending[3])

    def take(self, received):
        layer, names, parts, _ = self.pending
        self.pending = None
        for k, p, r in zip(names, parts, received):
            self.shards[k] = _add_chips(
                p, r, self.chip_index, self.core_index, layer, self.n_layers, self.shards.get(k), f"add_chips_{k}_{layer}")

    def finish(self):
        if self.pending is not None:
            layer, names = self.pending[0], self.pending[1]
            self.take(_scatter_to_chips(list(self.pending[3]), f"{'_'.join(names)}_{layer}"))
        return dict(zip(BIG, _join_halves([self.shards[k] for k in BIG])))


def _local_step(x, target, norm_mix, q_norm, k_norm, norm_ffn, layer_weights, exchange=None):
    layer_weights = list(layer_weights)

    def carrying(kernel_fn, n_out, *args):
        if exchange is None or exchange.pending is None:
            return kernel_fn(*args)
        out = kernel_fn(*args, scatter=exchange.payload())
        exchange.take(out[n_out:])
        return out[:n_out]

    n_layers = norm_mix.shape[0]
    s, d = x.shape
    tw = min(WGRAD_TILE, s)
    n_in = layer_weights[0][0].shape[-1]
    f = layer_weights[0][2].shape[-1]
    saved = []
    for l in range(n_layers):
        weights = list(layer_weights[l])
        q_gain = jnp.tile(q_norm[l][None, :], (1, 2))
        k_gain = jnp.tile(k_norm[l][None, :], (1, 2))
        h1, proj = _norm_matmul(x, norm_mix[l][None, :], weights[0], 0, f"in_proj_{l}")
        qn, kn, vb = _qkv_prep(proj, q_gain, k_gain, f"qkv_prep_{l}")
        missing = [n for n, w in enumerate(weights) if w.ndim == 3]
        if missing:
            attn, *arrived = _attn_fwd(qn, kn, vb, f"attn_fwd_{l}", gather=tuple(weights[n] for n in missing))
            for n, w in zip(missing, arrived):
                weights[n] = w
            layer_weights[l] = tuple(weights)
        else:
            attn = _attn_fwd(qn, kn, vb, f"attn_fwd_{l}")
        _, wout_s, wg_s, wu_s, wd_s, conv_s = weights
        taps = _conv_taps(conv_s)
        conv = _conv_fwd(proj, taps, f"conv_fwd_{l}")
        x_mid = _out_proj(x, attn, conv, wout_s, 0, f"out_proj_{l}")
        pending = ()
        if l + 1 < n_layers and all(w.ndim == 3 for w in layer_weights[l + 1]):
            pending = tuple(layer_weights[l + 1])
        x_out, gate, up, *arrived = _ffn_fwd(x_mid, norm_ffn[l][None, :], wg_s, wu_s, wd_s, 0, f"ffn_fwd_{l}", gather=pending)
        if pending:
            layer_weights[l + 1] = tuple(arrived)
        saved.append(dict(x=x, h1=h1, proj=proj, qn=qn, kn=kn, vb=vb, attn=attn, conv=conv, x_mid=x_mid, q_gain=q_gain, k_gain=k_gain,
                          gate=gate, up=up, taps=taps))
        x = x_out

    dy, loss_lanes = _loss_grad(x, target, "loss_grad")
    grads = [None] * n_layers
    for l in reversed(range(n_layers)):
        sv = saved[l]
        win_s, wout_s, wg_s, wu_s, wd_s, _ = layer_weights[l]
        dx_mid, d_norm_ffn, h2, dyb, dgate, dup, act = carrying(
            _ffn_bwd, 7, sv["x_mid"], dy, norm_ffn[l][None, :], sv["gate"], sv["up"], wg_s, wu_s, wd_s, 0, f"ffn_bwd_{l}")
        tok2 = pl.BlockSpec((tw, d), lambda j, i: (i, 0))
        hid = pl.BlockSpec((1, tw, f), lambda j, i: (j, i, 0))
        d_wg = _wgrad(h2, dgate, tok2, hid, N_CHIPS, d, f, f"wgrad_gate_{l}")
        d_wu = _wgrad(h2, dup, tok2, hid, N_CHIPS, d, f, f"wgrad_up_{l}")
        d_wd = _wgrad(act, dyb, hid, tok2, N_CHIPS, f, d, f"wgrad_down_{l}")
        if exchange is not None:
            exchange.offer(l, dict(w_gate=d_wg, w_up=d_wu, w_down=d_wd))
        d_attn, d_conv, dxb = _out_proj_bwd(dx_mid, wout_s, 0, f"out_proj_bwd_{l}")
        rows_out = wout_s.shape[2]
        mix_spec_a = pl.BlockSpec((tw, rows_out), lambda j, i: (i, j))
        d_wout_a = _wgrad(sv["attn"], dxb, mix_spec_a, tok2, ATTN_DIM // rows_out, rows_out, d, f"wgrad_out_attn_{l}")
        d_wout_c = _wgrad(sv["conv"], dxb, mix_spec_a, tok2, CONV_DIM // rows_out, rows_out, d, f"wgrad_out_conv_{l}")
        d_wout = jnp.concatenate([d_wout_a, d_wout_c], axis=0)
        dq, dk, dv = carrying(_attn_bwd, 3, sv["qn"], sv["kn"], sv["vb"], d_attn, f"attn_bwd_{l}")
        dproj, d_conv_w = _conv_bwd(sv["proj"], sv["taps"], d_conv, f"conv_bwd_{l}")
        dproj, d_qg, d_kg = _qkv_prep_bwd(sv["proj"], sv["q_gain"], sv["k_gain"], dq, dk, dv, dproj, f"qkv_prep_bwd_{l}")
        d_win = _wgrad(sv["h1"], dproj, tok2, pl.BlockSpec((tw, n_in), lambda j, i: (i, j)), N_CHIPS, d, n_in, f"wgrad_in_{l}")
        if exchange is not None:
            exchange.offer(l, dict(w_in=d_win, w_out=d_wout))
        dy, d_norm_mix = _in_proj_bwd(sv["x"], dx_mid, norm_mix[l][None, :], dproj, win_s, 0, f"in_proj_bwd_{l}")
        grads[l] = dict(norm_mix=d_norm_mix, norm_ffn=d_norm_ffn, q_norm=d_qg, k_norm=d_kg, conv_w=d_conv_w,
                        w_in=d_win, w_out=d_wout, w_gate=d_wg, w_up=d_wu, w_down=d_wd)
    return loss_lanes, dy, grads


BIG = ("w_in", "w_out", "w_gate", "w_up", "w_down")


def kernel(x, norm_mix, w_in, q_norm, k_norm, conv_w, w_out, norm_ffn, w_gate, w_up, w_down, loss_target, m_norm_mix, m_w_in, m_q_norm, m_k_norm, m_conv_w, m_w_out, m_norm_ffn, m_w_gate, m_w_up, m_w_down, v_norm_mix, v_w_in, v_q_norm, v_k_norm, v_conv_w, v_w_out, v_norm_ffn, v_w_gate, v_w_up, v_w_down):
    n_layers = norm_mix.shape[0]
    weights = dict(w_in=w_in, w_out=w_out, w_gate=w_gate, w_up=w_up, w_down=w_down)
    moments_m = dict(w_in=m_w_in, w_out=m_w_out, w_gate=m_w_gate, w_up=m_w_up, w_down=m_w_down)
    moments_v = dict(w_in=v_w_in, w_out=v_w_out, w_gate=v_w_gate, w_up=v_w_up, w_down=v_w_down)
    cx, cy, cc = _mesh_position()
    chip_index = (2 * cx + cy).astype(jnp.int32).reshape(1)
    core_index = cc.astype(jnp.int32).reshape(1)

    conv_pad = jnp.pad(conv_w, ((0, 0), (0, 16 - conv_w.shape[1]), (0, 0)))

    def shards_of(layer):
        return [weights[k][layer : layer + 1].astype(BF16) for k in BIG] + [conv_pad[layer : layer + 1]]

    first = shards_of(0)
    layer_weights = [tuple(_gather_weights(first[:1])) + tuple(first[1:])] + [tuple(shards_of(layer)) for layer in range(1, n_layers)]

    exchange = _GradExchange(chip_index, core_index, n_layers)
    loss_lanes, grad_x, grads = _local_step(
        x[0], loss_target[0], norm_mix, q_norm, k_norm, norm_ffn, layer_weights, exchange)

    big_grads = exchange.finish()

    def lanes(a):
        return _pad_to(a, a.shape[0], SMALL_COLS)

    def tile_of(*groups):
        return _pad_rows(jnp.concatenate([lanes(jnp.concatenate(g, axis=0)) for g in groups], axis=0), 8)

    layers = range(n_layers)
    pack = jnp.concatenate([
        tile_of([grads[l]["norm_mix"] for l in layers], [grads[l]["norm_ffn"] for l in layers], [loss_lanes]),
        tile_of([grads[l]["q_norm"] for l in layers], [grads[l]["k_norm"] for l in layers]),
        tile_of([grads[l]["conv_w"][0:3] for l in layers]),
    ], axis=0)
    small, small_heads, small_lanes = _sum_small(_gather_small(pack))
    loss = small_lanes[ROW_LOSS, 0]
    d_model = norm_mix.shape[1]
    conv_cols = conv_w.shape[2]
    conv_all = small[ROW_CONV : ROW_CONV + 3 * n_layers, 0:CONV_DIM].reshape(n_layers, 3, CONV_DIM)
    small_grads = dict(
        norm_mix=small[ROW_NORM_MIX : ROW_NORM_MIX + n_layers, 0:d_model],
        norm_ffn=small[ROW_NORM_FFN : ROW_NORM_FFN + n_layers, 0:d_model],
        q_norm=small_heads[ROW_Q_NORM - 8 : ROW_Q_NORM - 8 + n_layers, 0:HEAD_DIM],
        k_norm=small_heads[ROW_K_NORM - 8 : ROW_K_NORM - 8 + n_layers, 0:HEAD_DIM],
        conv_w=lax.dynamic_slice_in_dim(conv_all, (2 * cx + cy) * conv_cols, conv_cols, axis=2),
    )

    out_grad, out_delta, out_m, out_v = {}, {}, {}, {}
    for k in BIG:
        shape = weights[k].shape
        view = (shape[0] * shape[1], shape[2])
        g = big_grads[k]
        delta, new_m, new_v = _adamw(weights[k].reshape(view), g.reshape(view), moments_m[k].reshape(view), moments_v[k].reshape(view), f"adamw_{k}")
        out_grad[k], out_delta[k], out_m[k], out_v[k] = g, delta.reshape(shape), new_m.reshape(shape), new_v.reshape(shape)

    small_w = dict(norm_mix=norm_mix, norm_ffn=norm_ffn, q_norm=q_norm, k_norm=k_norm, conv_w=conv_w)
    small_m = dict(norm_mix=m_norm_mix, norm_ffn=m_norm_ffn, q_norm=m_q_norm, k_norm=m_k_norm, conv_w=m_conv_w)
    small_v = dict(norm_mix=v_norm_mix, norm_ffn=v_norm_ffn, q_norm=v_q_norm, k_norm=v_k_norm, conv_w=v_conv_w)
    order = ("norm_mix", "norm_ffn", "q_norm", "k_norm", "conv_w")

    def packed(tree):
        parts2 = [_pad_to(tree[k].reshape(-1, tree[k].shape[-1]), tree[k].reshape(-1, tree[k].shape[-1]).shape[0], SMALL_COLS) for k in order]
        return _pad_rows(jnp.concatenate(parts2, axis=0), SMALL_ROWS)

    delta_p, m_p, v_p = _adamw(packed(small_w), packed(small_grads), packed(small_m), packed(small_v), "adamw_small")
    row = 0
    for k in order:
        shape = small_w[k].shape
        n_rows = 1
        for dim in shape[:-1]:
            n_rows *= dim
        cut = (slice(row, row + n_rows), slice(0, shape[-1]))
        out_grad[k] = small_grads[k]
        out_delta[k], out_m[k], out_v[k] = delta_p[cut].reshape(shape), m_p[cut].reshape(shape), v_p[cut].reshape(shape)
        row += n_rows

    names_out = ("norm_mix", "w_in", "q_norm", "k_norm", "conv_w", "w_out", "norm_ffn", "w_gate", "w_up", "w_down")
    return (loss, grad_x[None], *[out_grad[k] for k in names_out], *[out_delta[k] for k in names_out],
            *[out_m[k] for k in names_out], *[out_v[k] for k in names_out])
```

```python
import functools

import jax
import jax.numpy as jnp
from jax import lax
from jax.experimental import pallas as pl
from jax.experimental.pallas import tpu as pltpu

F32 = jnp.float32
BF16 = jnp.bfloat16

EPS = 1e-6
HEAD_DIM = 64
LANES = 128
ATTN_DIM = 512
CONV_DIM = 512
N_CHIPS = 4
N_DEV = 8
Q_SCALE = HEAD_DIM ** -0.5
ATTN_Q_TILE = 256
ATTN_TILE = 256
TOKEN_TILE = 512
WGRAD_TILE = 4096
FFN_FWD_TILE = 1024
FFN_CHUNK = 256
VMEM_LIMIT = 56 * 1024 * 1024

ADAM_LR = 0.001
ADAM_B1 = 0.9
ADAM_B2 = 0.999
ADAM_EPS = 1e-08
ADAM_WD = 0.01
ADAM_STEP = 10

MESH_ID = pl.DeviceIdType.MESH
ANY = pl.BlockSpec(memory_space=pl.ANY)
VMEM_SPEC = pl.BlockSpec(memory_space=pltpu.VMEM)


def _params(n_axes):
    return pltpu.CompilerParams(dimension_semantics=("arbitrary",) * n_axes, vmem_limit_bytes=VMEM_LIMIT)


def _dot(a, b):
    return jnp.dot(a, b, preferred_element_type=F32)


def _dot_nt(a, b):
    return lax.dot_general(a, b, (((1,), (1,)), ((), ())), preferred_element_type=F32)


def _dot_tn(a, b):
    return lax.dot_general(a, b, (((0,), (0,)), ((), ())), preferred_element_type=F32)


SCORE_MAX = 80.0
UNDERFLOW_EXIT = 90.0


def _scores(q, k):
    return jnp.minimum(_dot_nt(q, k), SCORE_MAX)


def _softplus(z):
    return jnp.log(1.0 + jnp.exp(z))


def _norm_matmul(x, gain, w_s, layer, name):
    s, d = x.shape
    n_blocks, _, _, n = w_s.shape
    tm = TOKEN_TILE

    def body(x_ref, g_ref, w_ref, h_ref, o_ref):
        xv = x_ref[...]
        r = lax.rsqrt(jnp.mean(xv * xv, axis=-1, keepdims=True) + EPS)
        h = (xv * r * g_ref[...]).astype(BF16)
        h_ref[...] = h
        for j in range(n_blocks):
            o_ref[:, j * n : (j + 1) * n] = _dot(h, w_ref[j, 0])

    return pl.pallas_call(
        body,
        name=name,
        grid=(s // tm,),
        in_specs=[
            pl.BlockSpec((tm, d), lambda i: (i, 0)),
            pl.BlockSpec((1, d), lambda i: (0, 0)),
            pl.BlockSpec((n_blocks, 1, d, n), lambda i: (0, layer, 0, 0)),
        ],
        out_specs=[pl.BlockSpec((tm, d), lambda i: (i, 0)), pl.BlockSpec((tm, n_blocks * n), lambda i: (i, 0))],
        out_shape=[jax.ShapeDtypeStruct((s, d), BF16), jax.ShapeDtypeStruct((s, n_blocks * n), F32)],
        compiler_params=_params(1),
    )(x, gain, w_s)


def _head_norm(xv, gain, low):
    sq = xv * xv
    s_low = jnp.sum(jnp.where(low, sq, 0.0), axis=-1, keepdims=True)
    s_high = jnp.sum(jnp.where(low, 0.0, sq), axis=-1, keepdims=True)
    r = jnp.where(low, lax.rsqrt(s_low / HEAD_DIM + EPS), lax.rsqrt(s_high / HEAD_DIM + EPS))
    return xv * r * gain, r


def _qkv_prep(proj, q_gain, k_gain, name):
    s = proj.shape[0]
    tm = TOKEN_TILE

    def body(p_ref, qg_ref, kg_ref, q_ref, k_ref, v_ref):
        low = lax.broadcasted_iota(jnp.int32, (tm, LANES), 1) < HEAD_DIM
        for g in range(ATTN_DIM // LANES):
            cq = slice(LANES * g, LANES * (g + 1))
            ck = slice(ATTN_DIM + LANES * g, ATTN_DIM + LANES * (g + 1))
            cv = slice(2 * ATTN_DIM + LANES * g, 2 * ATTN_DIM + LANES * (g + 1))
            qn, _ = _head_norm(p_ref[:, cq], qg_ref[...], low)
            kn, _ = _head_norm(p_ref[:, ck], kg_ref[...], low)
            q_ref[:, cq] = (qn * Q_SCALE).astype(BF16)
            k_ref[:, cq] = kn.astype(BF16)
            v_ref[:, cq] = p_ref[:, cv].astype(BF16)

    out = jax.ShapeDtypeStruct((s, ATTN_DIM), BF16)
    return pl.pallas_call(
        body,
        name=name,
        grid=(s // tm,),
        in_specs=[
            pl.BlockSpec((tm, 3 * ATTN_DIM), lambda i: (i, 0)),
            pl.BlockSpec((1, LANES), lambda i: (0, 0)),
            pl.BlockSpec((1, LANES), lambda i: (0, 0)),
        ],
        out_specs=[pl.BlockSpec((tm, ATTN_DIM), lambda i: (i, 0))] * 3,
        out_shape=[out, out, out],
        compiler_params=_params(1),
    )(proj, q_gain, k_gain)


def _qkv_prep_bwd(proj, q_gain, k_gain, dq, dk, dv, dproj, name):
    s = proj.shape[0]
    tm = TOKEN_TILE

    def norm_bwd(xv, gain, dy, low):
        _, r = _head_norm(xv, gain, low)
        xhat = xv * r
        dxhat = dy * gain
        prod = dxhat * xhat
        m_low = jnp.sum(jnp.where(low, prod, 0.0), axis=-1, keepdims=True)
        m_high = jnp.sum(jnp.where(low, 0.0, prod), axis=-1, keepdims=True)
        mean = jnp.where(low, m_low, m_high) / HEAD_DIM
        return r * (dxhat - xhat * mean), jnp.sum(dy * xhat, axis=0, keepdims=True)

    def body(p_ref, qg_ref, kg_ref, dq_ref, dk_ref, dv_ref, dproj_ref, dp_ref, dqg_ref, dkg_ref):
        @pl.when(pl.program_id(0) == 0)
        def _():
            dqg_ref[...] = jnp.zeros_like(dqg_ref)
            dkg_ref[...] = jnp.zeros_like(dkg_ref)

        low = lax.broadcasted_iota(jnp.int32, (tm, LANES), 1) < HEAD_DIM
        for g in range(ATTN_DIM // LANES):
            cq = slice(LANES * g, LANES * (g + 1))
            ck = slice(ATTN_DIM + LANES * g, ATTN_DIM + LANES * (g + 1))
            cv = slice(2 * ATTN_DIM + LANES * g, 2 * ATTN_DIM + LANES * (g + 1))
            dxq, dgq = norm_bwd(p_ref[:, cq], qg_ref[...], dq_ref[:, cq] * Q_SCALE, low)
            dxk, dgk = norm_bwd(p_ref[:, ck], kg_ref[...], dk_ref[:, cq], low)
            dp_ref[:, cq] = dxq.astype(BF16)
            dp_ref[:, ck] = dxk.astype(BF16)
            dp_ref[:, cv] = dv_ref[:, cq].astype(BF16)
            dqg_ref[:, cq] += dgq
            dkg_ref[:, cq] += dgk

    grad_spec = pl.BlockSpec((tm, ATTN_DIM), lambda i: (i, 0))
    gain_spec = pl.BlockSpec((1, LANES), lambda i: (0, 0))
    sum_spec = pl.BlockSpec((1, ATTN_DIM), lambda i: (0, 0))
    return pl.pallas_call(
        body,
        name=name,
        grid=(s // tm,),
        in_specs=[pl.BlockSpec((tm, 3 * ATTN_DIM), lambda i: (i, 0)), gain_spec, gain_spec, grad_spec, grad_spec, grad_spec, ANY],
        out_specs=[pl.BlockSpec((tm, 3 * ATTN_DIM), lambda i: (i, 0)), sum_spec, sum_spec],
        out_shape=[
            jax.ShapeDtypeStruct(dproj.shape, BF16),
            jax.ShapeDtypeStruct((1, ATTN_DIM), F32),
            jax.ShapeDtypeStruct((1, ATTN_DIM), F32),
        ],
        input_output_aliases={6: 0},
        compiler_params=_params(1),
    )(proj, q_gain, k_gain, dq, dk, dv, dproj)


def _attn_tile_consts(t):
    row = lax.broadcasted_iota(jnp.int32, (t, t), 0)
    col = lax.broadcasted_iota(jnp.int32, (t, t), 1)
    return row, col


def _triangle_sum(v, triangle):
    return _dot(v.astype(BF16), triangle)


def _attn_fwd(qn, kn, vb, name, gather=()):
    s = qn.shape[0]
    t = min(ATTN_TILE, s)
    tq = min(ATTN_Q_TILE, t)
    per_key_tile = t // tq
    n_gather = len(gather)
    n_pairs, n_blocks = ATTN_DIM // LANES, s // tq

    def body(*refs):
        q_ref, k_ref, v_ref = refs[:3]
        o_ref = refs[3 + n_gather]
        if n_gather:
            copies = _WeightGather(refs[3 : 3 + n_gather], refs[4 + n_gather : 4 + 2 * n_gather], refs[4 + 2 * n_gather :])
            first = (pl.program_id(0) == 0) & (pl.program_id(1) == 0)
            pl.when(first)(copies.begin)
            pl.when((pl.program_id(0) == n_pairs - 1) & (pl.program_id(1) == 0))(copies.relay)
        i = pl.program_id(1) // per_key_tile
        low = lax.broadcasted_iota(jnp.int32, (tq, LANES), 1) < HEAD_DIM
        row, col = _attn_tile_consts(t)
        suffix = (row > col).astype(BF16)
        first_row = (pl.program_id(1) % per_key_tile) * tq
        causal = lax.broadcasted_iota(jnp.int32, (tq, t), 1) < lax.broadcasted_iota(jnp.int32, (tq, t), 0) + first_row
        q = q_ref[...]
        zero_q = jnp.zeros_like(q)
        qh = (jnp.where(low, q, zero_q), jnp.where(low, zero_q, q))

        def step(kbs, carry, diagonal_first=False):
            chains = [(head, m) for head in range(2) for m in range(len(kbs))]
            masked = [diagonal_first and m == 0 for _, m in chains]
            ks = [k_ref[pl.ds(pl.multiple_of(kb * t, t), t), :] for kb in kbs]
            vs = [v_ref[pl.ds(pl.multiple_of(kb * t, t), t), :] for kb in kbs]
            z = [_scores(qh[head], ks[kb]) for head, kb in chains]
            sp = [_softplus(zc) for zc in z]
            sp = [jnp.where(causal, s_, 0.0) if mk else s_ for s_, mk in zip(sp, masked)]
            inside = [_triangle_sum(s_, suffix) for s_ in sp]
            after = [carry[head][1] for head in range(2)]
            log_a = []
            for n, (head, kb) in enumerate(chains):
                log_a.append(z[n] - sp[n] - inside[n] - after[head])
                after[head] = after[head] + jnp.sum(sp[n], axis=-1, keepdims=True)
            a = [jnp.exp(l_) for l_ in log_a]
            a = [jnp.where(causal, a_, 0.0) if mk else a_ for a_, mk in zip(a, masked)]
            acc = [carry[head][0] for head in range(2)]
            for n, (head, kb) in enumerate(chains):
                acc[head] = acc[head] + _dot(a[n].astype(BF16), vs[kb])
            return tuple((acc[head], after[head]) for head in range(2))

        def live(c):
            return jnp.minimum(jnp.min(c[0][1]), jnp.min(c[1][1])) < UNDERFLOW_EXIT

        zero = (jnp.zeros((tq, LANES), F32), jnp.zeros((tq, 1), F32))
        carry = lax.cond(i >= 1, lambda c: step((i, i - 1), c, True), lambda c: step((i,), c, True), (zero, zero))
        rest = jnp.maximum(i - 1, 0)
        carry = lax.cond((rest % 2 == 1) & live(carry), lambda c: step((i - 2,), c), lambda c: c, carry)
        pairs = rest // 2
        _, carry = lax.while_loop(
            lambda st: (st[0] < pairs) & live(st[1]),
            lambda st: (st[0] + 1, step((2 * (pairs - st[0]) - 1, 2 * (pairs - st[0]) - 2), st[1])),
            (jnp.int32(0), carry))
        o_ref[...] = jnp.where(low, carry[0][0], carry[1][0]).astype(BF16)
        if n_gather:
            pl.when((pl.program_id(0) == n_pairs - 1) & (pl.program_id(1) == n_blocks - 1))(copies.finish)

    out = pl.pallas_call(
        body,
        name=name,
        grid=(n_pairs, n_blocks),
        in_specs=[
            pl.BlockSpec((tq, LANES), lambda p, i: (i, p)),
            pl.BlockSpec((s, LANES), lambda p, i: (0, p)),
            pl.BlockSpec((s, LANES), lambda p, i: (0, p)),
        ] + [ANY] * n_gather,
        out_specs=[pl.BlockSpec((tq, LANES), lambda p, i: (i, p))] + [ANY] * n_gather,
        out_shape=[jax.ShapeDtypeStruct((s, ATTN_DIM), BF16)] + [jax.ShapeDtypeStruct((N_CHIPS,) + w.shape, w.dtype) for w in gather],
        scratch_shapes=_gather_scratch(n_gather) if n_gather else [],
        compiler_params=_params(2),
    )(qn, kn, vb, *gather)
    return out if n_gather else out[0]


def _attn_bwd(qn, kn, vb, do, name, scatter=()):
    s = qn.shape[0]
    t = min(ATTN_TILE, s)
    nq = s // t
    n_scatter = len(scatter)
    n_pairs = ATTN_DIM // LANES

    def body(*refs):
        q_ref, k_ref, v_ref, do_ref = refs[:4]
        dq_ref, dk_ref, dv_ref = refs[4 + n_scatter : 7 + n_scatter]
        a_s, sg_s, a_f, sg_f = refs[7 + 2 * n_scatter : 11 + 2 * n_scatter]
        i = pl.program_id(1)
        if n_scatter:
            copies = _ChipScatter(refs[4 : 4 + n_scatter], refs[7 + n_scatter : 7 + 2 * n_scatter], refs[11 + 2 * n_scatter :])
            pl.when((pl.program_id(0) == 0) & (i == 0))(copies.begin)

        @pl.when(i == 0)
        def _():
            dk_ref[...] = jnp.zeros_like(dk_ref)
            dv_ref[...] = jnp.zeros_like(dv_ref)

        low = lax.broadcasted_iota(jnp.int32, (t, LANES), 1) < HEAD_DIM
        row, col = _attn_tile_consts(t)
        suffix = (row > col).astype(BF16)
        prefix = (row < col).astype(BF16)
        causal = col < row
        q = q_ref[...]
        dob = do_ref[...]
        zero_q = jnp.zeros_like(q)
        qhs = (jnp.where(low, q, zero_q), jnp.where(low, zero_q, q))
        dohs = (jnp.where(low, dob, zero_q), jnp.where(low, zero_q, dob))

        def rows_of(kb):
            return pl.ds(pl.multiple_of(kb * t, t), t)

        pair = [(head, m) for head in range(2) for m in range(2)]

        def short_pass1():
            z = [_scores(qhs[head], k_ref[rows_of(i - m), :]) for head, m in pair]
            sp = [_softplus(z_) for z_ in z]
            sp = [jnp.where(causal, s_, 0.0) if m == 0 else s_ for s_, (_, m) in zip(sp, pair)]
            inside = [_triangle_sum(s_, suffix) for s_ in sp]
            after = [jnp.zeros((t, 1), F32), jnp.zeros((t, 1), F32)]
            for n, (head, m) in enumerate(pair):
                log_sg = z[n] - sp[n]
                a = jnp.exp(log_sg - inside[n] - after[head])
                sg = jnp.exp(log_sg)
                if m == 0:
                    a = jnp.where(causal, a, 0.0)
                    sg = jnp.where(causal, sg, 0.0)
                a_f[n] = a
                sg_f[n] = sg
                after[head] = after[head] + jnp.sum(sp[n], axis=-1, keepdims=True)
            return jnp.minimum(jnp.min(after[0]), jnp.min(after[1])) >= UNDERFLOW_EXIT

        def short_pass2():
            order = [(head, m) for head in range(2) for m in (1, 0)]
            a = {c: a_f[pair.index(c)] for c in order}
            g = {c: a[c] * _dot_nt(dohs[c[0]], v_ref[rows_of(i - c[1]), :]) for c in order}
            for m in (1, 0):
                dv_ref[rows_of(i - m), :] += _dot_tn(a[(0, m)].astype(BF16), dohs[0]) + _dot_tn(a[(1, m)].astype(BF16), dohs[1])
            inside = {c: _triangle_sum(g[c], prefix) for c in order}
            before = [jnp.zeros((t, 1), F32), jnp.zeros((t, 1), F32)]
            dz = {}
            for c in order:
                sg = sg_f[pair.index(c)]
                dz[c] = (g[c] - sg * (g[c] + inside[c] + before[c[0]])).astype(BF16)
                before[c[0]] = before[c[0]] + jnp.sum(g[c], axis=-1, keepdims=True)
            for m in (1, 0):
                dk_ref[rows_of(i - m), :] += _dot_tn(dz[(0, m)], qhs[0]) + _dot_tn(dz[(1, m)], qhs[1])
            dq = [_dot(dz[(head, 1)], k_ref[rows_of(i - 1), :]) + _dot(dz[(head, 0)], k_ref[rows_of(i), :]) for head in range(2)]
            dq_ref[...] = jnp.where(low, dq[0], dq[1])

        def general_walk():
            heads = []
            for head in range(2):
                qh, doh = qhs[head], dohs[head]

                def pass1(kbs, after, diagonal_first=False):
                    z = [_scores(qh, k_ref[rows_of(kb), :]) for kb in kbs]
                    sp = [_softplus(z_) for z_ in z]
                    if diagonal_first:
                        sp[0] = jnp.where(causal, sp[0], 0.0)
                    inside = [_triangle_sum(s_, suffix) for s_ in sp]
                    for n, kb in enumerate(kbs):
                        log_sg = z[n] - sp[n]
                        a = jnp.exp(log_sg - inside[n] - after)
                        sg = jnp.exp(log_sg)
                        if diagonal_first and n == 0:
                            a = jnp.where(causal, a, 0.0)
                            sg = jnp.where(causal, sg, 0.0)
                        a_s[kb] = a
                        sg_s[kb] = sg
                        after = after + jnp.sum(sp[n], axis=-1, keepdims=True)
                    return after

                def live(after):
                    return jnp.min(after) < UNDERFLOW_EXIT

                after = jnp.zeros((t, 1), F32)
                after = lax.cond(i >= 1, lambda c: pass1((i, i - 1), c, True), lambda c: pass1((i,), c, True), after)
                rest = jnp.maximum(i - 1, 0)
                take_single = (rest % 2 == 1) & live(after)
                after = lax.cond(take_single, lambda c: pass1((i - 2,), c), lambda c: c, after)
                pairs = rest // 2
                pairs_done, _ = lax.while_loop(
                    lambda st: (st[0] < pairs) & live(st[1]),
                    lambda st: (st[0] + 1, pass1((2 * (pairs - st[0]) - 1, 2 * (pairs - st[0]) - 2), st[1])),
                    (jnp.int32(0), after))
                walked = jnp.minimum(i, 1) + 1 + take_single.astype(jnp.int32) + 2 * pairs_done
                first = i - walked + 1

                def pass2(kbs, carry):
                    dq, before = carry
                    ks = [k_ref[rows_of(kb), :] for kb in kbs]
                    a = [a_s[kb] for kb in kbs]
                    g = [a_ * _dot_nt(doh, v_ref[rows_of(kb), :]) for a_, kb in zip(a, kbs)]
                    for n, kb in enumerate(kbs):
                        dv_ref[rows_of(kb), :] += _dot_tn(a[n].astype(BF16), doh)
                    inside = [_triangle_sum(g_, prefix) for g_ in g]
                    dz = []
                    for n, kb in enumerate(kbs):
                        sg = sg_s[kb]
                        dz.append((g[n] - sg * (g[n] + inside[n] + before)).astype(BF16))
                        before = before + jnp.sum(g[n], axis=-1, keepdims=True)
                    for n, kb in enumerate(kbs):
                        dk_ref[rows_of(kb), :] += _dot_tn(dz[n], qh)
                    for n in range(len(kbs)):
                        dq = dq + _dot(dz[n], ks[n])
                    return dq, before

                carry = (jnp.zeros((t, LANES), F32), jnp.zeros((t, 1), F32))
                carry = lax.fori_loop(0, walked // 2, lambda n, c: pass2((first + 2 * n, first + 2 * n + 1), c), carry)
                carry = lax.cond(walked % 2 == 1, lambda c: pass2((i,), c), lambda c: c, carry)
                heads.append(carry[0])
            dq_ref[...] = jnp.where(low, heads[0], heads[1])

        short = lax.cond(i >= 1, short_pass1, lambda: jnp.bool_(False))
        pl.when(short)(short_pass2)
        pl.when(jnp.logical_not(short))(general_walk)
        if n_scatter:
            pl.when((pl.program_id(0) == n_pairs - 1) & (i == nq - 1))(copies.finish)

    q_spec = pl.BlockSpec((t, LANES), lambda p, i: (i, p))
    kv_spec = pl.BlockSpec((s, LANES), lambda p, i: (0, p))
    return pl.pallas_call(
        body,
        name=name,
        grid=(n_pairs, nq),
        in_specs=[q_spec, kv_spec, kv_spec, q_spec] + [ANY] * n_scatter,
        out_specs=[q_spec, kv_spec, kv_spec] + [ANY] * n_scatter,
        out_shape=[jax.ShapeDtypeStruct((s, ATTN_DIM), F32)] * 3 + _scatter_shapes(scatter),
        scratch_shapes=[pltpu.VMEM((nq, t, t), F32), pltpu.VMEM((nq, t, t), F32), pltpu.VMEM((4, t, t), F32), pltpu.VMEM((4, t, t), F32)]
        + (_scatter_scratch(n_scatter) if n_scatter else []),
        compiler_params=_params(2),
    )(qn, kn, vb, do, *scatter)


CB_BLOCK, CC_BLOCK, CU_BLOCK = 3, 4, 5


def _shift_down(h, prev_rows, n):
    row = lax.broadcasted_iota(jnp.int32, h.shape, 0)
    out = pltpu.roll(h, n, 0)
    for r in range(n):
        out = jnp.where(row == r, prev_rows[len(prev_rows) - n + r], out)
    return out


def _shift_up(h, next_rows, n):
    tm = h.shape[0]
    row = lax.broadcasted_iota(jnp.int32, h.shape, 0)
    out = pltpu.roll(h, tm - n, 0)
    for r in range(n):
        out = jnp.where(row == tm - n + r, next_rows[r], out)
    return out


def _conv_fwd(proj, conv_w, name):
    s = proj.shape[0]
    tm = TOKEN_TILE
    nb = tm // 8

    def body(cb_ref, cc_ref, cu_ref, pc_ref, pu_ref, w_ref, o_ref):
        first = pl.program_id(0) == 0
        h = cc_ref[...] * cu_ref[...]
        prev = [jnp.where(first, 0.0, pc_ref[r : r + 1, :] * pu_ref[r : r + 1, :]) for r in (6, 7)]
        y = w_ref[0:1, :] * _shift_down(h, prev, 2) + w_ref[1:2, :] * _shift_down(h, prev, 1) + w_ref[2:3, :] * h
        o_ref[...] = (cb_ref[...] * y).astype(BF16)

    def col(block):
        return pl.BlockSpec((tm, CONV_DIM), lambda i: (i, block))

    def halo(block):
        return pl.BlockSpec((8, CONV_DIM), lambda i: (jnp.maximum(i * nb - 1, 0), block))

    return pl.pallas_call(
        body,
        name=name,
        grid=(s // tm,),
        in_specs=[col(CB_BLOCK), col(CC_BLOCK), col(CU_BLOCK), halo(CC_BLOCK), halo(CU_BLOCK), pl.BlockSpec((8, CONV_DIM), lambda i: (0, 0))],
        out_specs=pl.BlockSpec((tm, CONV_DIM), lambda i: (i, 0)),
        out_shape=jax.ShapeDtypeStruct((s, CONV_DIM), BF16),
        compiler_params=_params(1),
    )(proj, proj, proj, proj, proj, conv_w)


def _conv_bwd(proj, conv_w, dconv, name):
    s = proj.shape[0]
    tm = TOKEN_TILE
    nb = tm // 8
    n_tiles = s // tm

    def body(cb_ref, cc_ref, cu_ref, dy_ref, pc_ref, pu_ref, nb_ref, ndy_ref, w_ref, dp_ref, dw_ref):
        i = pl.program_id(0)

        @pl.when(i == 0)
        def _():
            dw_ref[...] = jnp.zeros_like(dw_ref)

        first = i == 0
        last = i == n_tiles - 1
        cc, cu, cb, dy = cc_ref[...], cu_ref[...], cb_ref[...], dy_ref[...]
        h = cc * cu
        prev = [jnp.where(first, 0.0, pc_ref[r : r + 1, :] * pu_ref[r : r + 1, :]) for r in (6, 7)]
        h1 = _shift_down(h, prev, 1)
        h2 = _shift_down(h, prev, 2)
        y = w_ref[0:1, :] * h2 + w_ref[1:2, :] * h1 + w_ref[2:3, :] * h
        dyb = dy * cb
        nxt = [jnp.where(last, 0.0, ndy_ref[r : r + 1, :] * nb_ref[r : r + 1, :]) for r in (0, 1)]
        dh = w_ref[2:3, :] * dyb + w_ref[1:2, :] * _shift_up(dyb, nxt, 1) + w_ref[0:1, :] * _shift_up(dyb, nxt, 2)
        dp_ref[:, 0:CONV_DIM] = (dy * y).astype(BF16)
        dp_ref[:, CONV_DIM : 2 * CONV_DIM] = (dh * cu).astype(BF16)
        dp_ref[:, 2 * CONV_DIM : 3 * CONV_DIM] = (dh * cc).astype(BF16)
        dw_ref[0:1, :] += jnp.sum(dyb * h2, axis=0, keepdims=True)
        dw_ref[1:2, :] += jnp.sum(dyb * h1, axis=0, keepdims=True)
        dw_ref[2:3, :] += jnp.sum(dyb * h, axis=0, keepdims=True)

    def col(block):
        return pl.BlockSpec((tm, CONV_DIM), lambda i: (i, block))

    def halo_prev(block):
        return pl.BlockSpec((8, CONV_DIM), lambda i: (jnp.maximum(i * nb - 1, 0), block))

    def halo_next(block):
        return pl.BlockSpec((8, CONV_DIM), lambda i: (jnp.minimum((i + 1) * nb, s // 8 - 1), block))

    return pl.pallas_call(
        body,
        name=name,
        grid=(n_tiles,),
        in_specs=[
            col(CB_BLOCK), col(CC_BLOCK), col(CU_BLOCK), col(0),
            halo_prev(CC_BLOCK), halo_prev(CU_BLOCK), halo_next(CB_BLOCK), halo_next(0),
            pl.BlockSpec((8, CONV_DIM), lambda i: (0, 0)),
        ],
        out_specs=[pl.BlockSpec((tm, 3 * CONV_DIM), lambda i: (i, 1)), pl.BlockSpec((8, CONV_DIM), lambda i: (0, 0))],
        out_shape=[jax.ShapeDtypeStruct((s, 3 * ATTN_DIM + 3 * CONV_DIM), BF16), jax.ShapeDtypeStruct((8, CONV_DIM), F32)],
        compiler_params=_params(1),
    )(proj, proj, proj, dconv, proj, proj, proj, dconv, conv_w)


def _out_proj(x, attn, conv, w_s, layer, name):
    s, d = x.shape
    tm = TOKEN_TILE
    rows = w_s.shape[2]

    def body(x_ref, a_ref, c_ref, w_ref, o_ref):
        acc = x_ref[...]
        for j in range(N_CHIPS):
            src = a_ref if j < 2 else c_ref
            cols = slice((j % 2) * rows, (j % 2 + 1) * rows)
            acc = acc + _dot(src[:, cols], w_ref[j, 0])
        o_ref[...] = acc

    return pl.pallas_call(
        body,
        name=name,
        grid=(s // tm,),
        in_specs=[
            pl.BlockSpec((tm, d), lambda i: (i, 0)),
            pl.BlockSpec((tm, ATTN_DIM), lambda i: (i, 0)),
            pl.BlockSpec((tm, CONV_DIM), lambda i: (i, 0)),
            pl.BlockSpec((N_CHIPS, 1, rows, d), lambda i: (0, layer, 0, 0)),
        ],
        out_specs=pl.BlockSpec((tm, d), lambda i: (i, 0)),
        out_shape=jax.ShapeDtypeStruct((s, d), F32),
        compiler_params=_params(1),
    )(x, attn, conv, w_s)


def _out_proj_bwd(dx, w_s, layer, name):
    s, d = dx.shape
    tm = TOKEN_TILE
    rows = w_s.shape[2]

    def body(dx_ref, w_ref, da_ref, dc_ref, dxb_ref):
        dxb = dx_ref[...].astype(BF16)
        dxb_ref[...] = dxb
        for j in range(N_CHIPS):
            cols = slice((j % 2) * rows, (j % 2 + 1) * rows)
            part = _dot_nt(dxb, w_ref[j, 0])
            if j < 2:
                da_ref[:, cols] = part.astype(BF16)
            else:
                dc_ref[:, cols] = part

    return pl.pallas_call(
        body,
        name=name,
        grid=(s // tm,),
        in_specs=[pl.BlockSpec((tm, d), lambda i: (i, 0)), pl.BlockSpec((N_CHIPS, 1, rows, d), lambda i: (0, layer, 0, 0))],
        out_specs=[
            pl.BlockSpec((tm, ATTN_DIM), lambda i: (i, 0)),
            pl.BlockSpec((tm, CONV_DIM), lambda i: (i, 0)),
            pl.BlockSpec((tm, d), lambda i: (i, 0)),
        ],
        out_shape=[
            jax.ShapeDtypeStruct((s, ATTN_DIM), BF16),
            jax.ShapeDtypeStruct((s, CONV_DIM), F32),
            jax.ShapeDtypeStruct((s, d), BF16),
        ],
        compiler_params=_params(1),
    )(dx, w_s)


def _ffn_fwd(x, gain, wg_s, wu_s, wd_s, layer, name, gather=()):
    s, d = x.shape
    tm = min(FFN_FWD_TILE, s)
    f = wg_s.shape[3]
    n_gather = len(gather)
    n_tiles = s // tm

    def body(*refs):
        x_ref, g_ref, wg_ref, wu_ref, wd_ref = refs[:5]
        o_ref, gate_ref, up_ref = refs[5 + n_gather : 8 + n_gather]
        h_s = refs[8 + 2 * n_gather]
        i, j = pl.program_id(0), pl.program_id(1)
        if n_gather:
            copies = _WeightGather(refs[5 : 5 + n_gather], refs[8 + n_gather : 8 + 2 * n_gather], refs[9 + 2 * n_gather :])
            pl.when((i == 0) & (j == 0))(copies.begin)
            pl.when((i == (3 * n_tiles) // 4) & (j == 0))(copies.relay)

        @pl.when(j == 0)
        def _():
            xv = x_ref[...]
            r = lax.rsqrt(jnp.mean(xv * xv, axis=-1, keepdims=True) + EPS)
            h_s[...] = (xv * r * g_ref[...]).astype(BF16)
            o_ref[...] = xv

        halves = [slice(r, r + FFN_CHUNK) for r in range(0, tm, FFN_CHUNK)]
        pre = [(_dot(h_s[r, :], wg_ref[j, 0]), _dot(h_s[r, :], wu_ref[j, 0])) for r in halves]
        act = [((gate / (1.0 + jnp.exp(-gate))) * up).astype(BF16) for gate, up in pre]
        for r, (gate, up) in zip(halves, pre):
            gate_ref[0, r, :] = gate.astype(BF16)
            up_ref[0, r, :] = up.astype(BF16)
        for r, a in zip(halves, act):
            o_ref[r, :] += _dot(a, wd_ref[j, 0])

        if n_gather:
            pl.when((i == n_tiles - 1) & (j == N_CHIPS - 1))(copies.finish)

    hid = pl.BlockSpec((1, tm, f), lambda i, j: (j, i, 0))
    hid_shape = jax.ShapeDtypeStruct((N_CHIPS, s, f), BF16)
    return pl.pallas_call(
        body,
        name=name,
        grid=(n_tiles, N_CHIPS),
        in_specs=[
            pl.BlockSpec((tm, d), lambda i, j: (i, 0)),
            pl.BlockSpec((1, d), lambda i, j: (0, 0)),
            _resident((N_CHIPS, 1, d, f), layer),
            _resident((N_CHIPS, 1, d, f), layer),
            _resident((N_CHIPS, 1, f, d), layer),
        ] + [ANY] * n_gather,
        out_specs=[pl.BlockSpec((tm, d), lambda i, j: (i, 0)), hid, hid] + [ANY] * n_gather,
        out_shape=[jax.ShapeDtypeStruct((s, d), F32), hid_shape, hid_shape]
        + [jax.ShapeDtypeStruct((N_CHIPS,) + w.shape, w.dtype) for w in gather],
        scratch_shapes=[pltpu.VMEM((tm, d), BF16)] + (_gather_scratch(n_gather) if n_gather else []),
        compiler_params=_params(2),
    )(x, gain, wg_s, wu_s, wd_s, *gather)


def _resident(block, layer):
    return pl.BlockSpec(block, lambda i, j: (0, layer, 0, 0), pipeline_mode=pl.Buffered(1))


def _rms_bwd(xv, gain, dh):
    r = lax.rsqrt(jnp.mean(xv * xv, axis=-1, keepdims=True) + EPS)
    xhat = xv * r
    dxhat = dh * gain
    dx = r * (dxhat - xhat * jnp.mean(dxhat * xhat, axis=-1, keepdims=True))
    return dx, jnp.sum(dh * xhat, axis=0, keepdims=True)


def _ffn_bwd(x, dy, gain, gate_s, up_s, wg_s, wu_s, wd_s, layer, name, scatter=()):
    s, d = x.shape
    tm = TOKEN_TILE
    f = wg_s.shape[3]

    n_scatter = len(scatter)
    n_tiles = s // tm

    def body(*refs):
        x_ref, dy_ref, g_ref, gate_ref, up_ref, wg_ref, wu_ref, wd_ref = refs[:8]
        dx_ref, dgain_ref, h_ref, dyb_ref, dg_ref, du_ref, act_ref = refs[8 + n_scatter : 15 + n_scatter]
        acc_s = refs[15 + 2 * n_scatter]
        i, j = pl.program_id(0), pl.program_id(1)
        if n_scatter:
            copies = _ChipScatter(refs[8 : 8 + n_scatter], refs[15 + n_scatter : 15 + 2 * n_scatter], refs[16 + 2 * n_scatter :])
            pl.when((i == 0) & (j == 0))(copies.begin)

        @pl.when((i == 0) & (j == 0))
        def _():
            dgain_ref[...] = jnp.zeros_like(dgain_ref)

        @pl.when(j == 0)
        def _():
            xv = x_ref[...]
            r = lax.rsqrt(jnp.mean(xv * xv, axis=-1, keepdims=True) + EPS)
            h_ref[...] = (xv * r * g_ref[...]).astype(BF16)
            dyb_ref[...] = dy_ref[...].astype(BF16)
            acc_s[...] = jnp.zeros_like(acc_s)

        halves = [slice(0, tm // 2), slice(tm // 2, tm)]
        pre = [(gate_ref[0, r, :].astype(F32), up_ref[0, r, :].astype(F32), _dot_nt(dyb_ref[r, :], wd_ref[j, 0])) for r in halves]
        grads = []
        for r, (gate, up, dact) in zip(halves, pre):
            sig = 1.0 / (1.0 + jnp.exp(-gate))
            silu = gate * sig
            dgate = (dact * up * (sig * (1.0 + gate * (1.0 - sig)))).astype(BF16)
            dup = (dact * silu).astype(BF16)
            act_ref[0, r, :] = (silu * up).astype(BF16)
            dg_ref[0, r, :] = dgate
            du_ref[0, r, :] = dup
            grads.append((dgate, dup))
        for r, (dgate, dup) in zip(halves, grads):
            acc_s[r, :] += _dot_nt(dgate, wg_ref[j, 0]) + _dot_nt(dup, wu_ref[j, 0])

        @pl.when(j == N_CHIPS - 1)
        def _():
            dxn, dgain = _rms_bwd(x_ref[...], g_ref[...], acc_s[...])
            dx_ref[...] = dy_ref[...] + dxn
            dgain_ref[...] += dgain

        if n_scatter:
            pl.when((i == n_tiles - 1) & (j == N_CHIPS - 1))(copies.finish)

    tok = pl.BlockSpec((tm, d), lambda i, j: (i, 0))
    vec = pl.BlockSpec((1, d), lambda i, j: (0, 0))
    hid = pl.BlockSpec((1, tm, f), lambda i, j: (j, i, 0))
    hid_shape = jax.ShapeDtypeStruct((N_CHIPS, s, f), BF16)
    return pl.pallas_call(
        body,
        name=name,
        grid=(n_tiles, N_CHIPS),
        in_specs=[
            tok, tok, vec, hid, hid,
            _resident((N_CHIPS, 1, d, f), layer),
            _resident((N_CHIPS, 1, d, f), layer),
            _resident((N_CHIPS, 1, f, d), layer),
        ] + [ANY] * n_scatter,
        out_specs=[tok, vec, tok, tok, hid, hid, hid] + [ANY] * n_scatter,
        out_shape=[
            jax.ShapeDtypeStruct((s, d), F32),
            jax.ShapeDtypeStruct((1, d), F32),
            jax.ShapeDtypeStruct((s, d), BF16),
            jax.ShapeDtypeStruct((s, d), BF16),
            hid_shape, hid_shape, hid_shape,
        ] + _scatter_shapes(scatter),
        scratch_shapes=[pltpu.VMEM((tm, d), F32)] + (_scatter_scratch(n_scatter) if n_scatter else []),
        compiler_params=_params(2),
    )(x, dy, gain, gate_s, up_s, wg_s, wu_s, wd_s, *scatter)


def _in_proj_bwd(x, dx_res, gain, dproj, w_s, layer, name):
    s, d = x.shape
    tm = TOKEN_TILE
    n = w_s.shape[3]

    def body(x_ref, r_ref, g_ref, dp_ref, w_ref, dx_ref, dgain_ref):
        @pl.when(pl.program_id(0) == 0)
        def _():
            dgain_ref[...] = jnp.zeros_like(dgain_ref)

        dh = _dot_nt(dp_ref[:, 0:n], w_ref[0, 0])
        for j in range(1, N_CHIPS):
            dh = dh + _dot_nt(dp_ref[:, j * n : (j + 1) * n], w_ref[j, 0])
        dxn, dgain = _rms_bwd(x_ref[...], g_ref[...], dh)
        dx_ref[...] = r_ref[...] + dxn
        dgain_ref[...] += dgain

    tok = pl.BlockSpec((tm, d), lambda i: (i, 0))
    vec = pl.BlockSpec((1, d), lambda i: (0, 0))
    return pl.pallas_call(
        body,
        name=name,
        grid=(s // tm,),
        in_specs=[tok, tok, vec, pl.BlockSpec((tm, N_CHIPS * n), lambda i: (i, 0)), pl.BlockSpec((N_CHIPS, 1, d, n), lambda i: (0, layer, 0, 0))],
        out_specs=[tok, vec],
        out_shape=[jax.ShapeDtypeStruct((s, d), F32), jax.ShapeDtypeStruct((1, d), F32)],
        compiler_params=_params(1),
    )(x, dx_res, gain, dproj, w_s)


def _loss_grad(y, target, name):
    s, d = y.shape
    tm = TOKEN_TILE

    def body(y_ref, t_ref, dy_ref, l_ref):
        @pl.when(pl.program_id(0) == 0)
        def _():
            l_ref[...] = jnp.zeros_like(l_ref)

        err = y_ref[...] - t_ref[...]
        dy_ref[...] = err / d
        l_ref[...] += jnp.sum(err * err, axis=0, keepdims=True) * (0.5 / d)

    tok = pl.BlockSpec((tm, d), lambda i: (i, 0))
    return pl.pallas_call(
        body,
        name=name,
        grid=(s // tm,),
        in_specs=[tok, tok],
        out_specs=[tok, pl.BlockSpec((1, d), lambda i: (0, 0))],
        out_shape=[jax.ShapeDtypeStruct((s, d), F32), jax.ShapeDtypeStruct((1, d), F32)],
        compiler_params=_params(1),
    )(y, target)


def _wgrad(a, b, a_spec, b_spec, n_blocks, k, n, name):
    n_tiles = a.shape[-2] // min(WGRAD_TILE, a.shape[-2])

    def body(a_ref, b_ref, o_ref):
        @pl.when(pl.program_id(1) == 0)
        def _():
            o_ref[...] = jnp.zeros_like(o_ref)

        av = a_ref[0] if len(a_ref.shape) == 3 else a_ref[...]
        bv = b_ref[0] if len(b_ref.shape) == 3 else b_ref[...]
        o_ref[0] += _dot_tn(av, bv)

    return pl.pallas_call(
        body,
        name=name,
        grid=(n_blocks, n_tiles),
        in_specs=[a_spec, b_spec],
        out_specs=pl.BlockSpec((1, k, n), lambda j, i: (j, 0, 0)),
        out_shape=jax.ShapeDtypeStruct((n_blocks, k, n), F32),
        compiler_params=_params(2),
    )(a, b)


def _mesh_position():
    return lax.axis_index("x"), lax.axis_index("y"), lax.axis_index("c")


def _other_chips(x, y):
    return [(1 - x, y), (x, 1 - y), (1 - x, 1 - y)]


def _half_rows(ref_rows, c):
    half = ref_rows // 2
    return pl.ds(c * half, half)


class _WeightGather:
    def __init__(self, ins, outs, sems):
        self.ins, self.outs = ins, outs
        send_sems, recv_sems, pass_send_sems, pass_recv_sems, self.local_sems = sems
        self.ici, self.d2d = (send_sems, recv_sems), (pass_send_sems, pass_recv_sems)
        self.x, self.y, self.c = _mesh_position()
        self.me = 2 * self.x + self.y
        self.sibling = (self.x, self.y, 1 - self.c)
        self.chips = _other_chips(self.x, self.y)

    def _copy(self, t, k, chip_index, core, to, sems, src=None):
        dst = self.outs[t].at[chip_index, :, _half_rows(self.ins[t].shape[1], core), :]
        return pltpu.make_async_remote_copy(
            src_ref=dst if src is None else src, dst_ref=dst, send_sem=sems[0].at[t, k], recv_sem=sems[1].at[t, k],
            device_id=to, device_id_type=MESH_ID,
        )

    def _own(self, t):
        return pltpu.make_async_copy(self.ins[t], self.outs[t].at[self.me], self.local_sems.at[t])

    def _sends(self):
        for t in range(len(self.ins)):
            mine = self.ins[t].at[:, _half_rows(self.ins[t].shape[1], self.c), :]
            for k, (px, py) in enumerate(self.chips):
                yield self._copy(t, k, self.me, self.c, (px, py, self.c), self.ici, src=mine)

    def _passes(self, core, sems):
        for t in range(len(self.ins)):
            for k, (px, py) in enumerate(self.chips):
                yield self._copy(t, k, 2 * px + py, core, self.sibling, sems)

    def begin(self):
        for t in range(len(self.ins)):
            self._own(t).start()
        for cp in self._sends():
            cp.start()

    def relay(self):
        for arrived, onward in zip(self._passes(self.c, self.ici), self._passes(self.c, self.d2d)):
            arrived.wait_recv()
            onward.start()

    def finish(self):
        for cp in self._passes(1 - self.c, self.d2d):
            cp.wait_recv()
        for cp in list(self._sends()) + list(self._passes(self.c, self.d2d)):
            cp.wait_send()
        for t in range(len(self.ins)):
            self._own(t).wait()


def _gather_scratch(n):
    sems = pltpu.SemaphoreType.DMA((n, N_CHIPS - 1))
    return [sems, sems, sems, sems, pltpu.SemaphoreType.DMA((n,))]


def _gather_weights(shards):
    n = len(shards)

    def body(*refs):
        gather = _WeightGather(refs[:n], refs[n : 2 * n], refs[2 * n :])
        gather.begin()
        gather.relay()
        gather.finish()

    return pl.pallas_call(
        body,
        name="gather_weights",
        in_specs=[ANY] * n,
        out_specs=[ANY] * n,
        out_shape=[jax.ShapeDtypeStruct((N_CHIPS,) + w.shape, w.dtype) for w in shards],
        scratch_shapes=_gather_scratch(n),
    )(*shards)


def _swap_halves(grads, tag):
    n = len(grads)

    def body(*refs):
        ins, outs = refs[:n], refs[n : 2 * n]
        send_sems, recv_sems = refs[2 * n :]
        x, y, c = _mesh_position()
        copies = []
        for t in range(n):
            copies.append(pltpu.make_async_remote_copy(
                src_ref=ins[t].at[:, _half_rows(ins[t].shape[1], 1 - c), :], dst_ref=outs[t],
                send_sem=send_sems.at[t], recv_sem=recv_sems.at[t], device_id=(x, y, 1 - c), device_id_type=MESH_ID,
            ))
            copies[-1].start()
        for cp in copies:
            cp.wait()

    sems = pltpu.SemaphoreType.DMA((n,))
    return pl.pallas_call(
        body,
        name=f"swap_halves_{tag}",
        in_specs=[ANY] * n,
        out_specs=[ANY] * n,
        out_shape=[jax.ShapeDtypeStruct((g.shape[0], g.shape[1] // 2, g.shape[2]), g.dtype) for g in grads],
        scratch_shapes=[sems, sems],
    )(*grads)


class _ChipScatter:
    def __init__(self, ins, outs, sems):
        self.ins, self.outs = ins, outs
        self.send_sems, self.recv_sems = sems
        self.x, self.y, self.c = _mesh_position()

    def _copies(self):
        for t in range(len(self.ins)):
            for k, (px, py) in enumerate(_other_chips(self.x, self.y)):
                yield pltpu.make_async_remote_copy(
                    src_ref=self.ins[t].at[2 * px + py], dst_ref=self.outs[t].at[k],
                    send_sem=self.send_sems.at[t, k], recv_sem=self.recv_sems.at[t, k],
                    device_id=(px, py, self.c), device_id_type=MESH_ID,
                )

    def begin(self):
        for cp in self._copies():
            cp.start()

    def finish(self):
        for cp in self._copies():
            cp.wait()


def _scatter_scratch(n):
    sems = pltpu.SemaphoreType.DMA((n, N_CHIPS - 1))
    return [sems, sems]


def _scatter_shapes(parts):
    return [jax.ShapeDtypeStruct((N_CHIPS - 1,) + p.shape[1:], p.dtype) for p in parts]


def _scatter_to_chips(parts, tag):
    n = len(parts)

    def body(*refs):
        copies = _ChipScatter(refs[:n], refs[n : 2 * n], refs[2 * n :])
        copies.begin()
        copies.finish()

    return pl.pallas_call(
        body,
        name=f"scatter_to_chips_{tag}",
        in_specs=[ANY] * n,
        out_specs=[ANY] * n,
        out_shape=_scatter_shapes(parts),
        scratch_shapes=_scatter_scratch(n),
    )(*parts)


def _join_halves(shards):
    n = len(shards)

    def body(*refs):
        outs = refs[n : 2 * n]
        send_sems, recv_sems = refs[2 * n :]
        x, y, c = _mesh_position()
        copies = []
        for t in range(n):
            mine = outs[t].at[:, _half_rows(outs[t].shape[1], c), :]
            copies.append(pltpu.make_async_remote_copy(
                src_ref=mine, dst_ref=mine, send_sem=send_sems.at[t], recv_sem=recv_sems.at[t],
                device_id=(x, y, 1 - c), device_id_type=MESH_ID,
            ))
            copies[-1].start()
        for cp in copies:
            cp.wait()

    sems = pltpu.SemaphoreType.DMA((n,))
    return pl.pallas_call(
        body,
        name="join_halves",
        in_specs=[ANY] * n,
        out_specs=[ANY] * n,
        out_shape=[jax.ShapeDtypeStruct(g.shape, g.dtype) for g in shards],
        input_output_aliases={t: t for t in range(n)},
        scratch_shapes=[sems, sems],
    )(*shards)


def _gather_small(pack):
    def body(p_ref, o_ref, send_sems, recv_sems, local_sem):
        x, y, c = _mesh_position()
        own = pltpu.make_async_copy(p_ref, o_ref.at[4 * x + 2 * y + c], local_sem)
        own.start()
        copies = []
        for k in range(1, N_DEV):
            px, py, pc = x ^ (k >> 2), y ^ ((k >> 1) & 1), c ^ (k & 1)
            send = pltpu.make_async_remote_copy(
                src_ref=p_ref, dst_ref=o_ref.at[4 * x + 2 * y + c], send_sem=send_sems.at[k - 1], recv_sem=recv_sems.at[k - 1],
                device_id=(px, py, pc), device_id_type=MESH_ID,
            )
            send.start()
            copies.append((send, 4 * px + 2 * py + pc))
        for send, peer_slot in copies:
            send.wait_send()
        for k in range(1, N_DEV):
            px, py, pc = x ^ (k >> 2), y ^ ((k >> 1) & 1), c ^ (k & 1)
            pltpu.make_async_remote_copy(
                src_ref=p_ref, dst_ref=o_ref.at[4 * px + 2 * py + pc], send_sem=send_sems.at[k - 1], recv_sem=recv_sems.at[k - 1],
                device_id=(px, py, pc), device_id_type=MESH_ID,
            ).wait_recv()
        own.wait()

    sems = pltpu.SemaphoreType.DMA((N_DEV - 1,))
    return pl.pallas_call(
        body,
        name="gather_small",
        in_specs=[VMEM_SPEC],
        out_specs=VMEM_SPEC,
        out_shape=jax.ShapeDtypeStruct((N_DEV,) + pack.shape, pack.dtype),
        scratch_shapes=[sems, sems, pltpu.SemaphoreType.DMA],
    )(pack)


def _row_tile(rows):
    for tile in range(min(rows, 512) // 8 * 8, 0, -8):
        if rows % tile == 0:
            return tile
    return rows


def _add_half(grad, received, half_index, name):
    slots, h, cdim = received.shape
    tile = _row_tile(h)
    per_half = h // tile

    def body(c_ref, g_ref, r_ref, o_ref, ob_ref):
        total = g_ref[...] + r_ref[...]
        o_ref[...] = total
        ob_ref[...] = total.astype(BF16)

    block = pl.BlockSpec((1, tile, cdim), lambda j, i, c: (j, i, 0))
    grid_spec = pltpu.PrefetchScalarGridSpec(
        num_scalar_prefetch=1,
        grid=(slots, per_half),
        in_specs=[pl.BlockSpec((1, tile, cdim), lambda j, i, c: (j, c[0] * per_half + i, 0)), block],
        out_specs=[block, block],
    )
    return pl.pallas_call(
        body, name=name, grid_spec=grid_spec,
        out_shape=[jax.ShapeDtypeStruct(received.shape, F32), jax.ShapeDtypeStruct(received.shape, BF16)],
        compiler_params=_params(2),
    )(half_index, grad, received)


def _add_chips(part, received, chip_index, core_index, layer, n_layers, shard, name):
    _, h, cdim = part.shape
    tile = _row_tile(h)
    per_half = h // tile

    def body(chip_ref, core_ref, p_ref, r_ref, *rest):
        o_ref = rest[-1]
        o_ref[0] = ((p_ref[0] + r_ref[0].astype(F32)) + r_ref[1].astype(F32)) + r_ref[2].astype(F32)

    in_specs = [
        pl.BlockSpec((1, tile, cdim), lambda i, chip, core: (chip[0], i, 0)),
        pl.BlockSpec((N_CHIPS - 1, tile, cdim), lambda i, chip, core: (0, i, 0)),
    ]
    operands = [chip_index, core_index, part, received]
    aliases = {}
    if shard is not None:
        in_specs.append(ANY)
        operands.append(shard)
        aliases = {4: 0}
    grid_spec = pltpu.PrefetchScalarGridSpec(
        num_scalar_prefetch=2,
        grid=(per_half,),
        in_specs=in_specs,
        out_specs=pl.BlockSpec((1, tile, cdim), lambda i, chip, core: (layer, core[0] * per_half + i, 0)),
    )
    return pl.pallas_call(
        body, name=name, grid_spec=grid_spec, out_shape=jax.ShapeDtypeStruct((n_layers, 2 * h, cdim), F32),
        input_output_aliases=aliases, compiler_params=_params(1),
    )(*operands)


def _adamw(w, g, m, v, name):
    rows, cdim = w.shape
    tile = _row_tile(rows)

    def body(w_ref, g_ref, m_ref, v_ref, d_ref, nm_ref, nv_ref):
        gv = g_ref[...]
        nm = ADAM_B1 * m_ref[...] + (1.0 - ADAM_B1) * gv
        nv = ADAM_B2 * v_ref[...] + (1.0 - ADAM_B2) * (gv * gv)
        m_hat = nm / (1.0 - ADAM_B1 ** ADAM_STEP)
        v_hat = nv / (1.0 - ADAM_B2 ** ADAM_STEP)
        d_ref[...] = -ADAM_LR * (m_hat / (jnp.sqrt(v_hat) + ADAM_EPS) + ADAM_WD * w_ref[...])
        nm_ref[...] = nm
        nv_ref[...] = nv

    spec = pl.BlockSpec((tile, cdim), lambda i: (i, 0))
    shape = jax.ShapeDtypeStruct((rows, cdim), F32)
    return pl.pallas_call(
        body, name=name, grid=(rows // tile,), in_specs=[spec] * 4, out_specs=[spec] * 3, out_shape=[shape] * 3,
        compiler_params=_params(1),
    )(w, g, m, v)


SMALL_ROWS, SMALL_COLS = 24, 1024
ROW_NORM_MIX, ROW_NORM_FFN, ROW_LOSS, ROW_Q_NORM, ROW_K_NORM, ROW_CONV = 0, 2, 4, 8, 10, 16


def _sum_small(gathered):
    def body(g_ref, o_ref, heads_ref, lanes_ref):
        total = g_ref[0]
        for dev in range(1, N_DEV):
            total = total + g_ref[dev]
        o_ref[...] = total
        heads = o_ref[8:16, 0:LANES]
        for grp in range(1, ATTN_DIM // LANES):
            heads = heads + o_ref[8:16, grp * LANES : (grp + 1) * LANES]
        heads_ref[...] = heads + pltpu.roll(heads, HEAD_DIM, 1)
        lanes_ref[...] = jnp.broadcast_to(jnp.sum(o_ref[0:8, :], axis=-1, keepdims=True), (8, LANES))

    return pl.pallas_call(
        body,
        name="sum_small",
        in_specs=[VMEM_SPEC],
        out_specs=[VMEM_SPEC] * 3,
        out_shape=[jax.ShapeDtypeStruct((SMALL_ROWS, SMALL_COLS), F32), jax.ShapeDtypeStruct((8, LANES), F32), jax.ShapeDtypeStruct((8, LANES), F32)],
    )(gathered)


def _pad_rows(a, rows):
    return jnp.pad(a, ((0, rows - a.shape[0]), (0, 0)))


def _pad_to(a, rows, cols):
    return jnp.pad(a, ((0, rows - a.shape[0]), (0, cols - a.shape[1])))


def _conv_taps(conv_s):
    return jnp.transpose(conv_s[:, 0, 0:8], (1, 0, 2)).reshape(8, -1)


class _GradExchange:
    def __init__(self, chip_index, core_index, n_layers):
        self.chip_index, self.core_index, self.n_layers = chip_index, core_index, n_layers
        self.shards = {}
        self.pending = None

    def offer(self, layer, grads):
        assert self.pending is None
        names = list(grads)
        received = _swap_halves([grads[k] for k in names], f"{'_'.join(names)}_{layer}")
        parts = [_add_half(grads[k], r, self.core_index, f"add_half_{k}_{layer}") for k, r in zip(names, received)]
        self.pending = (layer, names, [p32 for p32, _ in parts], [p16 for _, p16 in parts])

    def payload(self):
        return () if self.pending is None else tuple(self.pending[3])

    def take(self, received):
        layer, names, parts, _ = self.pending
        self.pending = None
        for k, p, r in zip(names, parts, received):
            self.shards[k] = _add_chips(
                p, r, self.chip_index, self.core_index, layer, self.n_layers, self.shards.get(k), f"add_chips_{k}_{layer}")

    def finish(self):
        if self.pending is not None:
            layer, names = self.pending[0], self.pending[1]
            self.take(_scatter_to_chips(list(self.pending[3]), f"{'_'.join(names)}_{layer}"))
        return dict(zip(BIG, _join_halves([self.shards[k] for k in BIG])))


def _local_step(x, target, norm_mix, q_norm, k_norm, norm_ffn, layer_weights, exchange=None):
    layer_weights = list(layer_weights)

    def carrying(kernel_fn, n_out, *args):
        if exchange is None or exchange.pending is None:
            return kernel_fn(*args)
        out = kernel_fn(*args, scatter=exchange.payload())
        exchange.take(out[n_out:])
        return out[:n_out]

    n_layers = norm_mix.shape[0]
    s, d = x.shape
    tw = min(WGRAD_TILE, s)
    n_in = layer_weights[0][0].shape[-1]
    f = layer_weights[0][2].shape[-1]
    saved = []
    for l in range(n_layers):
        weights = list(layer_weights[l])
        q_gain = jnp.tile(q_norm[l][None, :], (1, 2))
        k_gain = jnp.tile(k_norm[l][None, :], (1, 2))
        h1, proj = _norm_matmul(x, norm_mix[l][None, :], weights[0], 0, f"in_proj_{l}")
        qn, kn, vb = _qkv_prep(proj, q_gain, k_gain, f"qkv_prep_{l}")
        missing = [n for n, w in enumerate(weights) if w.ndim == 3]
        if missing:
            attn, *arrived = _attn_fwd(qn, kn, vb, f"attn_fwd_{l}", gather=tuple(weights[n] for n in missing))
            for n, w in zip(missing, arrived):
                weights[n] = w
            layer_weights[l] = tuple(weights)
        else:
            attn = _attn_fwd(qn, kn, vb, f"attn_fwd_{l}")
        _, wout_s, wg_s, wu_s, wd_s, conv_s = weights
        taps = _conv_taps(conv_s)
        conv = _conv_fwd(proj, taps, f"conv_fwd_{l}")
        x_mid = _out_proj(x, attn, conv, wout_s, 0, f"out_proj_{l}")
        pending = ()
        if l + 1 < n_layers and all(w.ndim == 3 for w in layer_weights[l + 1]):
            pending = tuple(layer_weights[l + 1])
        x_out, gate, up, *arrived = _ffn_fwd(x_mid, norm_ffn[l][None, :], wg_s, wu_s, wd_s, 0, f"ffn_fwd_{l}", gather=pending)
        if pending:
            layer_weights[l + 1] = tuple(arrived)
        saved.append(dict(x=x, h1=h1, proj=proj, qn=qn, kn=kn, vb=vb, attn=attn, conv=conv, x_mid=x_mid, q_gain=q_gain, k_gain=k_gain,
                          gate=gate, up=up, taps=taps))
        x = x_out

    dy, loss_lanes = _loss_grad(x, target, "loss_grad")
    grads = [None] * n_layers
    for l in reversed(range(n_layers)):
        sv = saved[l]
        win_s, wout_s, wg_s, wu_s, wd_s, _ = layer_weights[l]
        dx_mid, d_norm_ffn, h2, dyb, dgate, dup, act = carrying(
            _ffn_bwd, 7, sv["x_mid"], dy, norm_ffn[l][None, :], sv["gate"], sv["up"], wg_s, wu_s, wd_s, 0, f"ffn_bwd_{l}")
        tok2 = pl.BlockSpec((tw, d), lambda j, i: (i, 0))
        hid = pl.BlockSpec((1, tw, f), lambda j, i: (j, i, 0))
        d_wg = _wgrad(h2, dgate, tok2, hid, N_CHIPS, d, f, f"wgrad_gate_{l}")
        d_wu = _wgrad(h2, dup, tok2, hid, N_CHIPS, d, f, f"wgrad_up_{l}")
        d_wd = _wgrad(act, dyb, hid, tok2, N_CHIPS, f, d, f"wgrad_down_{l}")
        if exchange is not None:
            exchange.offer(l, dict(w_gate=d_wg, w_up=d_wu, w_down=d_wd))
        d_attn, d_conv, dxb = _out_proj_bwd(dx_mid, wout_s, 0, f"out_proj_bwd_{l}")
        rows_out = wout_s.shape[2]
        mix_spec_a = pl.BlockSpec((tw, rows_out), lambda j, i: (i, j))
        d_wout_a = _wgrad(sv["attn"], dxb, mix_spec_a, tok2, ATTN_DIM // rows_out, rows_out, d, f"wgrad_out_attn_{l}")
        d_wout_c = _wgrad(sv["conv"], dxb, mix_spec_a, tok2, CONV_DIM // rows_out, rows_out, d, f"wgrad_out_conv_{l}")
        d_wout = jnp.concatenate([d_wout_a, d_wout_c], axis=0)
        dq, dk, dv = carrying(_attn_bwd, 3, sv["qn"], sv["kn"], sv["vb"], d_attn, f"attn_bwd_{l}")
        dproj, d_conv_w = _conv_bwd(sv["proj"], sv["taps"], d_conv, f"conv_bwd_{l}")
        dproj, d_qg, d_kg = _qkv_prep_bwd(sv["proj"], sv["q_gain"], sv["k_gain"], dq, dk, dv, dproj, f"qkv_prep_bwd_{l}")
        d_win = _wgrad(sv["h1"], dproj, tok2, pl.BlockSpec((tw, n_in), lambda j, i: (i, j)), N_CHIPS, d, n_in, f"wgrad_in_{l}")
        if exchange is not None:
            exchange.offer(l, dict(w_in=d_win, w_out=d_wout))
        dy, d_norm_mix = _in_proj_bwd(sv["x"], dx_mid, norm_mix[l][None, :], dproj, win_s, 0, f"in_proj_bwd_{l}")
        grads[l] = dict(norm_mix=d_norm_mix, norm_ffn=d_norm_ffn, q_norm=d_qg, k_norm=d_kg, conv_w=d_conv_w,
                        w_in=d_win, w_out=d_wout, w_gate=d_wg, w_up=d_wu, w_down=d_wd)
    return loss_lanes, dy, grads


BIG = ("w_in", "w_out", "w_gate", "w_up", "w_down")


def kernel(x, norm_mix, w_in, q_norm, k_norm, conv_w, w_out, norm_ffn, w_gate, w_up, w_down, loss_target, m_norm_mix, m_w_in, m_q_norm, m_k_norm, m_conv_w, m_w_out, m_norm_ffn, m_w_gate, m_w_up, m_w_down, v_norm_mix, v_w_in, v_q_norm, v_k_norm, v_conv_w, v_w_out, v_norm_ffn, v_w_gate, v_w_up, v_w_down):
    n_layers = norm_mix.shape[0]
    weights = dict(w_in=w_in, w_out=w_out, w_gate=w_gate, w_up=w_up, w_down=w_down)
    moments_m = dict(w_in=m_w_in, w_out=m_w_out, w_gate=m_w_gate, w_up=m_w_up, w_down=m_w_down)
    moments_v = dict(w_in=v_w_in, w_out=v_w_out, w_gate=v_w_gate, w_up=v_w_up, w_down=v_w_down)
    cx, cy, cc = _mesh_position()
    chip_index = (2 * cx + cy).astype(jnp.int32).reshape(1)
    core_index = cc.astype(jnp.int32).reshape(1)

    conv_pad = jnp.pad(conv_w, ((0, 0), (0, 16 - conv_w.shape[1]), (0, 0)))

    def shards_of(layer):
        return [weights[k][layer : layer + 1].astype(BF16) for k in BIG] + [conv_pad[layer : layer + 1]]

    first = shards_of(0)
    layer_weights = [tuple(_gather_weights(first[:1])) + tuple(first[1:])] + [tuple(shards_of(layer)) for layer in range(1, n_layers)]

    exchange = _GradExchange(chip_index, core_index, n_layers)
    loss_lanes, grad_x, grads = _local_step(
        x[0], loss_target[0], norm_mix, q_norm, k_norm, norm_ffn, layer_weights, exchange)

    big_grads = exchange.finish()

    def lanes(a):
        return _pad_to(a, a.shape[0], SMALL_COLS)

    def tile_of(*groups):
        return _pad_rows(jnp.concatenate([lanes(jnp.concatenate(g, axis=0)) for g in groups], axis=0), 8)

    layers = range(n_layers)
    pack = jnp.concatenate([
        tile_of([grads[l]["norm_mix"] for l in layers], [grads[l]["norm_ffn"] for l in layers], [loss_lanes]),
        tile_of([grads[l]["q_norm"] for l in layers], [grads[l]["k_norm"] for l in layers]),
        tile_of([grads[l]["conv_w"][0:3] for l in layers]),
    ], axis=0)
    small, small_heads, small_lanes = _sum_small(_gather_small(pack))
    loss = small_lanes[ROW_LOSS, 0]
    d_model = norm_mix.shape[1]
    conv_cols = conv_w.shape[2]
    conv_all = small[ROW_CONV : ROW_CONV + 3 * n_layers, 0:CONV_DIM].reshape(n_layers, 3, CONV_DIM)
    small_grads = dict(
        norm_mix=small[ROW_NORM_MIX : ROW_NORM_MIX + n_layers, 0:d_model],
        norm_ffn=small[ROW_NORM_FFN : ROW_NORM_FFN + n_layers, 0:d_model],
        q_norm=small_heads[ROW_Q_NORM - 8 : ROW_Q_NORM - 8 + n_layers, 0:HEAD_DIM],
        k_norm=small_heads[ROW_K_NORM - 8 : ROW_K_NORM - 8 + n_layers, 0:HEAD_DIM],
        conv_w=lax.dynamic_slice_in_dim(conv_all, (2 * cx + cy) * conv_cols, conv_cols, axis=2),
    )

    out_grad, out_delta, out_m, out_v = {}, {}, {}, {}
    for k in BIG:
        shape = weights[k].shape
        view = (shape[0] * shape[1], shape[2])
        g = big_grads[k]
        delta, new_m, new_v = _adamw(weights[k].reshape(view), g.reshape(view), moments_m[k].reshape(view), moments_v[k].reshape(view), f"adamw_{k}")
        out_grad[k], out_delta[k], out_m[k], out_v[k] = g, delta.reshape(shape), new_m.reshape(shape), new_v.reshape(shape)

    small_w = dict(norm_mix=norm_mix, norm_ffn=norm_ffn, q_norm=q_norm, k_norm=k_norm, conv_w=conv_w)
    small_m = dict(norm_mix=m_norm_mix, norm_ffn=m_norm_ffn, q_norm=m_q_norm, k_norm=m_k_norm, conv_w=m_conv_w)
    small_v = dict(norm_mix=v_norm_mix, norm_ffn=v_norm_ffn, q_norm=v_q_norm, k_norm=v_k_norm, conv_w=v_conv_w)
    order = ("norm_mix", "norm_ffn", "q_norm", "k_norm", "conv_w")

    def packed(tree):
        parts2 = [_pad_to(tree[k].reshape(-1, tree[k].shape[-1]), tree[k].reshape(-1, tree[k].shape[-1]).shape[0], SMALL_COLS) for k in order]
        return _pad_rows(jnp.concatenate(parts2, axis=0), SMALL_ROWS)

    delta_p, m_p, v_p = _adamw(packed(small_w), packed(small_grads), packed(small_m), packed(small_v), "adamw_small")
    row = 0
    for k in order:
        shape = small_w[k].shape
        n_rows = 1
        for dim in shape[:-1]:
            n_rows *= dim
        cut = (slice(row, row + n_rows), slice(0, shape[-1]))
        out_grad[k] = small_grads[k]
        out_delta[k], out_m[k], out_v[k] = delta_p[cut].reshape(shape), m_p[cut].reshape(shape), v_p[cut].reshape(shape)
        row += n_rows

    names_out = ("norm_mix", "w_in", "q_norm", "k_norm", "conv_w", "w_out", "norm_ffn", "w_gate", "w_up", "w_down")
    return (loss, grad_x[None], *[out_grad[k] for k in names_out], *[out_delta[k] for k in names_out],
            *[out_m[k] for k in names_out], *[out_v[k] for k in names_out])
```

```python
import functools

import jax
import jax.numpy as jnp
from jax import lax
from jax.experimental import pallas as pl
from jax.experimental.pallas import tpu as pltpu

F32 = jnp.float32
BF16 = jnp.bfloat16

EPS = 1e-6
HEAD_DIM = 64
LANES = 128
ATTN_DIM = 512
CONV_DIM = 512
N_CHIPS = 4
N_DEV = 8
Q_SCALE = HEAD_DIM ** -0.5
ATTN_Q_TILE = 256
ATTN_TILE = 256
TOKEN_TILE = 512
WGRAD_TILE = 4096
FFN_FWD_TILE = 1024
FFN_CHUNK = 256
VMEM_LIMIT = 56 * 1024 * 1024

ADAM_LR = 0.001
ADAM_B1 = 0.9
ADAM_B2 = 0.999
ADAM_EPS = 1e-08
ADAM_WD = 0.01
ADAM_STEP = 10

MESH_ID = pl.DeviceIdType.MESH
ANY = pl.BlockSpec(memory_space=pl.ANY)
VMEM_SPEC = pl.BlockSpec(memory_space=pltpu.VMEM)


def _params(n_axes):
    return pltpu.CompilerParams(dimension_semantics=("arbitrary",) * n_axes, vmem_limit_bytes=VMEM_LIMIT)


def _dot(a, b):
    return jnp.dot(a, b, preferred_element_type=F32)


def _dot_nt(a, b):
    return lax.dot_general(a, b, (((1,), (1,)), ((), ())), preferred_element_type=F32)


def _dot_tn(a, b):
    return lax.dot_general(a, b, (((0,), (0,)), ((), ())), preferred_element_type=F32)


SCORE_MAX = 80.0
UNDERFLOW_EXIT = 90.0


def _scores(q, k):
    return jnp.minimum(_dot_nt(q, k), SCORE_MAX)


def _softplus(z):
    return jnp.log(1.0 + jnp.exp(z))


def _head_norm(xv, gain, low):
    sq = xv * xv
    s_low = jnp.sum(jnp.where(low, sq, 0.0), axis=-1, keepdims=True)
    s_high = jnp.sum(jnp.where(low, 0.0, sq), axis=-1, keepdims=True)
    r = jnp.where(low, lax.rsqrt(s_low / HEAD_DIM + EPS), lax.rsqrt(s_high / HEAD_DIM + EPS))
    return xv * r * gain, r


def _in_proj(x, gain, w_s, layer, q_gain, k_gain, name):
    s, d = x.shape
    n_blocks, _, _, n = w_s.shape
    tm = TOKEN_TILE

    def body(x_ref, g_ref, w_ref, qg_ref, kg_ref, h_ref, o_ref, q_ref, k_ref, v_ref):
        xv = x_ref[...]
        r = lax.rsqrt(jnp.mean(xv * xv, axis=-1, keepdims=True) + EPS)
        h = (xv * r * g_ref[...]).astype(BF16)
        h_ref[...] = h
        for j in range(n_blocks):
            o_ref[:, j * n : (j + 1) * n] = _dot(h, w_ref[j, 0])
        low = lax.broadcasted_iota(jnp.int32, (tm, LANES), 1) < HEAD_DIM
        for g in range(ATTN_DIM // LANES):
            cq = slice(LANES * g, LANES * (g + 1))
            ck = slice(ATTN_DIM + LANES * g, ATTN_DIM + LANES * (g + 1))
            cv = slice(2 * ATTN_DIM + LANES * g, 2 * ATTN_DIM + LANES * (g + 1))
            qn, _ = _head_norm(o_ref[:, cq], qg_ref[...], low)
            kn, _ = _head_norm(o_ref[:, ck], kg_ref[...], low)
            q_ref[:, cq] = (qn * Q_SCALE).astype(BF16)
            k_ref[:, cq] = kn.astype(BF16)
            v_ref[:, cq] = o_ref[:, cv].astype(BF16)

    head_spec = pl.BlockSpec((tm, ATTN_DIM), lambda i: (i, 0))
    head_shape = jax.ShapeDtypeStruct((s, ATTN_DIM), BF16)
    gain_spec = pl.BlockSpec((1, LANES), lambda i: (0, 0))
    return pl.pallas_call(
        body,
        name=name,
        grid=(s // tm,),
        in_specs=[
            pl.BlockSpec((tm, d), lambda i: (i, 0)),
            pl.BlockSpec((1, d), lambda i: (0, 0)),
            pl.BlockSpec((n_blocks, 1, d, n), lambda i: (0, layer, 0, 0)),
            gain_spec, gain_spec,
        ],
        out_specs=[pl.BlockSpec((tm, d), lambda i: (i, 0)), pl.BlockSpec((tm, n_blocks * n), lambda i: (i, 0)), head_spec, head_spec, head_spec],
        out_shape=[jax.ShapeDtypeStruct((s, d), BF16), jax.ShapeDtypeStruct((s, n_blocks * n), F32), head_shape, head_shape, head_shape],
        compiler_params=_params(1),
    )(x, gain, w_s, q_gain, k_gain)


def _qkv_prep_bwd(proj, q_gain, k_gain, dq, dk, dv, dproj, name):
    s = proj.shape[0]
    tm = TOKEN_TILE

    def norm_bwd(xv, gain, dy, low):
        _, r = _head_norm(xv, gain, low)
        xhat = xv * r
        dxhat = dy * gain
        prod = dxhat * xhat
        m_low = jnp.sum(jnp.where(low, prod, 0.0), axis=-1, keepdims=True)
        m_high = jnp.sum(jnp.where(low, 0.0, prod), axis=-1, keepdims=True)
        mean = jnp.where(low, m_low, m_high) / HEAD_DIM
        return r * (dxhat - xhat * mean), jnp.sum(dy * xhat, axis=0, keepdims=True)

    def body(p_ref, qg_ref, kg_ref, dq_ref, dk_ref, dv_ref, dproj_ref, dp_ref, dqg_ref, dkg_ref):
        @pl.when(pl.program_id(0) == 0)
        def _():
            dqg_ref[...] = jnp.zeros_like(dqg_ref)
            dkg_ref[...] = jnp.zeros_like(dkg_ref)

        low = lax.broadcasted_iota(jnp.int32, (tm, LANES), 1) < HEAD_DIM
        for g in range(ATTN_DIM // LANES):
            cq = slice(LANES * g, LANES * (g + 1))
            ck = slice(ATTN_DIM + LANES * g, ATTN_DIM + LANES * (g + 1))
            cv = slice(2 * ATTN_DIM + LANES * g, 2 * ATTN_DIM + LANES * (g + 1))
            dxq, dgq = norm_bwd(p_ref[:, cq], qg_ref[...], dq_ref[:, cq] * Q_SCALE, low)
            dxk, dgk = norm_bwd(p_ref[:, ck], kg_ref[...], dk_ref[:, cq], low)
            dp_ref[:, cq] = dxq.astype(BF16)
            dp_ref[:, ck] = dxk.astype(BF16)
            dp_ref[:, cv] = dv_ref[:, cq].astype(BF16)
            dqg_ref[:, cq] += dgq
            dkg_ref[:, cq] += dgk

    grad_spec = pl.BlockSpec((tm, ATTN_DIM), lambda i: (i, 0))
    gain_spec = pl.BlockSpec((1, LANES), lambda i: (0, 0))
    sum_spec = pl.BlockSpec((1, ATTN_DIM), lambda i: (0, 0))
    return pl.pallas_call(
        body,
        name=name,
        grid=(s // tm,),
        in_specs=[pl.BlockSpec((tm, 3 * ATTN_DIM), lambda i: (i, 0)), gain_spec, gain_spec, grad_spec, grad_spec, grad_spec, ANY],
        out_specs=[pl.BlockSpec((tm, 3 * ATTN_DIM), lambda i: (i, 0)), sum_spec, sum_spec],
        out_shape=[
            jax.ShapeDtypeStruct(dproj.shape, BF16),
            jax.ShapeDtypeStruct((1, ATTN_DIM), F32),
            jax.ShapeDtypeStruct((1, ATTN_DIM), F32),
        ],
        input_output_aliases={6: 0},
        compiler_params=_params(1),
    )(proj, q_gain, k_gain, dq, dk, dv, dproj)


def _attn_tile_consts(t):
    row = lax.broadcasted_iota(jnp.int32, (t, t), 0)
    col = lax.broadcasted_iota(jnp.int32, (t, t), 1)
    return row, col


def _triangle_sum(v, triangle):
    return _dot(v.astype(BF16), triangle)


def _attn_fwd(qn, kn, vb, name, gather=()):
    s = qn.shape[0]
    t = min(ATTN_TILE, s)
    tq = min(ATTN_Q_TILE, t)
    per_key_tile = t // tq
    n_gather = len(gather)
    n_pairs, n_blocks = ATTN_DIM // LANES, s // tq

    def body(*refs):
        q_ref, k_ref, v_ref = refs[:3]
        o_ref = refs[3 + n_gather]
        if n_gather:
            copies = _WeightGather(refs[3 : 3 + n_gather], refs[4 + n_gather : 4 + 2 * n_gather], refs[4 + 2 * n_gather :])
            first = (pl.program_id(0) == 0) & (pl.program_id(1) == 0)
            pl.when(first)(copies.begin)
            pl.when((pl.program_id(0) == n_pairs - 1) & (pl.program_id(1) == 0))(copies.relay)
        i = pl.program_id(1) // per_key_tile
        low = lax.broadcasted_iota(jnp.int32, (tq, LANES), 1) < HEAD_DIM
        row, col = _attn_tile_consts(t)
        suffix = (row > col).astype(BF16)
        first_row = (pl.program_id(1) % per_key_tile) * tq
        causal = lax.broadcasted_iota(jnp.int32, (tq, t), 1) < lax.broadcasted_iota(jnp.int32, (tq, t), 0) + first_row
        q = q_ref[...]
        zero_q = jnp.zeros_like(q)
        qh = (jnp.where(low, q, zero_q), jnp.where(low, zero_q, q))

        def step(kbs, carry, diagonal_first=False):
            chains = [(head, m) for head in range(2) for m in range(len(kbs))]
            masked = [diagonal_first and m == 0 for _, m in chains]
            ks = [k_ref[pl.ds(pl.multiple_of(kb * t, t), t), :] for kb in kbs]
            vs = [v_ref[pl.ds(pl.multiple_of(kb * t, t), t), :] for kb in kbs]
            z = [_scores(qh[head], ks[kb]) for head, kb in chains]
            sp = [_softplus(zc) for zc in z]
            sp = [jnp.where(causal, s_, 0.0) if mk else s_ for s_, mk in zip(sp, masked)]
            inside = [_triangle_sum(s_, suffix) for s_ in sp]
            after = [carry[head][1] for head in range(2)]
            log_a = []
            for n, (head, kb) in enumerate(chains):
                log_a.append(z[n] - sp[n] - inside[n] - after[head])
                after[head] = after[head] + jnp.sum(sp[n], axis=-1, keepdims=True)
            a = [jnp.exp(l_) for l_ in log_a]
            a = [jnp.where(causal, a_, 0.0) if mk else a_ for a_, mk in zip(a, masked)]
            acc = [carry[head][0] for head in range(2)]
            for n, (head, kb) in enumerate(chains):
                acc[head] = acc[head] + _dot(a[n].astype(BF16), vs[kb])
            return tuple((acc[head], after[head]) for head in range(2))

        def live(c):
            return jnp.minimum(jnp.min(c[0][1]), jnp.min(c[1][1])) < UNDERFLOW_EXIT

        zero = (jnp.zeros((tq, LANES), F32), jnp.zeros((tq, 1), F32))
        start = lax.cond(i >= 1, lambda c: step((i, i - 1), c, True), lambda c: step((i,), c, True), (zero, zero))
        o_ref[...] = jnp.where(low, start[0][0], start[1][0]).astype(BF16)
        rest = jnp.maximum(i - 1, 0)

        @pl.when((rest > 0) & live(start))
        def _():
            carry = lax.cond(rest % 2 == 1, lambda c: step((i - 2,), c), lambda c: c, start)
            pairs = rest // 2
            _, carry = lax.while_loop(
                lambda st: (st[0] < pairs) & live(st[1]),
                lambda st: (st[0] + 1, step((2 * (pairs - st[0]) - 1, 2 * (pairs - st[0]) - 2), st[1])),
                (jnp.int32(0), carry))
            o_ref[...] = jnp.where(low, carry[0][0], carry[1][0]).astype(BF16)

        if n_gather:
            pl.when((pl.program_id(0) == n_pairs - 1) & (pl.program_id(1) == n_blocks - 1))(copies.finish)

    out = pl.pallas_call(
        body,
        name=name,
        grid=(n_pairs, n_blocks),
        in_specs=[
            pl.BlockSpec((tq, LANES), lambda p, i: (i, p)),
            pl.BlockSpec((s, LANES), lambda p, i: (0, p)),
            pl.BlockSpec((s, LANES), lambda p, i: (0, p)),
        ] + [ANY] * n_gather,
        out_specs=[pl.BlockSpec((tq, LANES), lambda p, i: (i, p))] + [ANY] * n_gather,
        out_shape=[jax.ShapeDtypeStruct((s, ATTN_DIM), BF16)] + [jax.ShapeDtypeStruct((N_CHIPS,) + w.shape, w.dtype) for w in gather],
        scratch_shapes=_gather_scratch(n_gather) if n_gather else [],
        compiler_params=_params(2),
    )(qn, kn, vb, *gather)
    return out if n_gather else out[0]


def _attn_bwd(qn, kn, vb, do, name, scatter=()):
    s = qn.shape[0]
    t = min(ATTN_TILE, s)
    nq = s // t
    n_scatter = len(scatter)
    n_pairs = ATTN_DIM // LANES

    def body(*refs):
        q_ref, k_ref, v_ref, do_ref = refs[:4]
        dq_ref, dk_ref, dv_ref = refs[4 + n_scatter : 7 + n_scatter]
        a_s, sg_s, a_f, sg_f = refs[7 + 2 * n_scatter : 11 + 2 * n_scatter]
        i = pl.program_id(1)
        if n_scatter:
            copies = _ChipScatter(refs[4 : 4 + n_scatter], refs[7 + n_scatter : 7 + 2 * n_scatter], refs[11 + 2 * n_scatter :])
            pl.when((pl.program_id(0) == 0) & (i == 0))(copies.begin)

        @pl.when(i == 0)
        def _():
            dk_ref[...] = jnp.zeros_like(dk_ref)
            dv_ref[...] = jnp.zeros_like(dv_ref)

        low = lax.broadcasted_iota(jnp.int32, (t, LANES), 1) < HEAD_DIM
        row, col = _attn_tile_consts(t)
        suffix = (row > col).astype(BF16)
        prefix = (row < col).astype(BF16)
        causal = col < row
        q = q_ref[...]
        dob = do_ref[...]
        zero_q = jnp.zeros_like(q)
        qhs = (jnp.where(low, q, zero_q), jnp.where(low, zero_q, q))
        dohs = (jnp.where(low, dob, zero_q), jnp.where(low, zero_q, dob))

        def rows_of(kb):
            return pl.ds(pl.multiple_of(kb * t, t), t)

        pair = [(head, m) for head in range(2) for m in range(2)]

        def short_pass1():
            z = [_scores(qhs[head], k_ref[rows_of(i - m), :]) for head, m in pair]
            sp = [_softplus(z_) for z_ in z]
            sp = [jnp.where(causal, s_, 0.0) if m == 0 else s_ for s_, (_, m) in zip(sp, pair)]
            inside = [_triangle_sum(s_, suffix) for s_ in sp]
            after = [jnp.zeros((t, 1), F32), jnp.zeros((t, 1), F32)]
            for n, (head, m) in enumerate(pair):
                log_sg = z[n] - sp[n]
                a = jnp.exp(log_sg - inside[n] - after[head])
                sg = jnp.exp(log_sg)
                if m == 0:
                    a = jnp.where(causal, a, 0.0)
                    sg = jnp.where(causal, sg, 0.0)
                a_f[n] = a
                sg_f[n] = sg
                after[head] = after[head] + jnp.sum(sp[n], axis=-1, keepdims=True)
            return jnp.minimum(jnp.min(after[0]), jnp.min(after[1])) >= UNDERFLOW_EXIT

        def short_pass2():
            order = [(head, m) for head in range(2) for m in (1, 0)]
            a = {c: a_f[pair.index(c)] for c in order}
            g = {c: a[c] * _dot_nt(dohs[c[0]], v_ref[rows_of(i - c[1]), :]) for c in order}
            for m in (1, 0):
                dv_ref[rows_of(i - m), :] += _dot_tn(a[(0, m)].astype(BF16), dohs[0]) + _dot_tn(a[(1, m)].astype(BF16), dohs[1])
            inside = {c: _triangle_sum(g[c], prefix) for c in order}
            before = [jnp.zeros((t, 1), F32), jnp.zeros((t, 1), F32)]
            dz = {}
            for c in order:
                sg = sg_f[pair.index(c)]
                dz[c] = (g[c] - sg * (g[c] + inside[c] + before[c[0]])).astype(BF16)
                before[c[0]] = before[c[0]] + jnp.sum(g[c], axis=-1, keepdims=True)
            for m in (1, 0):
                dk_ref[rows_of(i - m), :] += _dot_tn(dz[(0, m)], qhs[0]) + _dot_tn(dz[(1, m)], qhs[1])
            dq = [_dot(dz[(head, 1)], k_ref[rows_of(i - 1), :]) + _dot(dz[(head, 0)], k_ref[rows_of(i), :]) for head in range(2)]
            dq_ref[...] = jnp.where(low, dq[0], dq[1])

        def general_walk():
            heads = []
            for head in range(2):
                qh, doh = qhs[head], dohs[head]

                def pass1(kbs, after, diagonal_first=False):
                    z = [_scores(qh, k_ref[rows_of(kb), :]) for kb in kbs]
                    sp = [_softplus(z_) for z_ in z]
                    if diagonal_first:
                        sp[0] = jnp.where(causal, sp[0], 0.0)
                    inside = [_triangle_sum(s_, suffix) for s_ in sp]
                    for n, kb in enumerate(kbs):
                        log_sg = z[n] - sp[n]
                        a = jnp.exp(log_sg - inside[n] - after)
                        sg = jnp.exp(log_sg)
                        if diagonal_first and n == 0:
                            a = jnp.where(causal, a, 0.0)
                            sg = jnp.where(causal, sg, 0.0)
                        a_s[kb] = a
                        sg_s[kb] = sg
                        after = after + jnp.sum(sp[n], axis=-1, keepdims=True)
                    return after

                def live(after):
                    return jnp.min(after) < UNDERFLOW_EXIT

                after = jnp.zeros((t, 1), F32)
                after = lax.cond(i >= 1, lambda c: pass1((i, i - 1), c, True), lambda c: pass1((i,), c, True), after)
                rest = jnp.maximum(i - 1, 0)
                take_single = (rest % 2 == 1) & live(after)
                after = lax.cond(take_single, lambda c: pass1((i - 2,), c), lambda c: c, after)
                pairs = rest // 2
                pairs_done, _ = lax.while_loop(
                    lambda st: (st[0] < pairs) & live(st[1]),
                    lambda st: (st[0] + 1, pass1((2 * (pairs - st[0]) - 1, 2 * (pairs - st[0]) - 2), st[1])),
                    (jnp.int32(0), after))
                walked = jnp.minimum(i, 1) + 1 + take_single.astype(jnp.int32) + 2 * pairs_done
                first = i - walked + 1

                def pass2(kbs, carry):
                    dq, before = carry
                    ks = [k_ref[rows_of(kb), :] for kb in kbs]
                    a = [a_s[kb] for kb in kbs]
                    g = [a_ * _dot_nt(doh, v_ref[rows_of(kb), :]) for a_, kb in zip(a, kbs)]
                    for n, kb in enumerate(kbs):
                        dv_ref[rows_of(kb), :] += _dot_tn(a[n].astype(BF16), doh)
                    inside = [_triangle_sum(g_, prefix) for g_ in g]
                    dz = []
                    for n, kb in enumerate(kbs):
                        sg = sg_s[kb]
                        dz.append((g[n] - sg * (g[n] + inside[n] + before)).astype(BF16))
                        before = before + jnp.sum(g[n], axis=-1, keepdims=True)
                    for n, kb in enumerate(kbs):
                        dk_ref[rows_of(kb), :] += _dot_tn(dz[n], qh)
                    for n in range(len(kbs)):
                        dq = dq + _dot(dz[n], ks[n])
                    return dq, before

                carry = (jnp.zeros((t, LANES), F32), jnp.zeros((t, 1), F32))
                carry = lax.fori_loop(0, walked // 2, lambda n, c: pass2((first + 2 * n, first + 2 * n + 1), c), carry)
                carry = lax.cond(walked % 2 == 1, lambda c: pass2((i,), c), lambda c: c, carry)
                heads.append(carry[0])
            dq_ref[...] = jnp.where(low, heads[0], heads[1])

        short = lax.cond(i >= 1, short_pass1, lambda: jnp.bool_(False))
        pl.when(short)(short_pass2)
        pl.when(jnp.logical_not(short))(general_walk)
        if n_scatter:
            pl.when((pl.program_id(0) == n_pairs - 1) & (i == nq - 1))(copies.finish)

    q_spec = pl.BlockSpec((t, LANES), lambda p, i: (i, p))
    kv_spec = pl.BlockSpec((s, LANES), lambda p, i: (0, p))
    return pl.pallas_call(
        body,
        name=name,
        grid=(n_pairs, nq),
        in_specs=[q_spec, kv_spec, kv_spec, q_spec] + [ANY] * n_scatter,
        out_specs=[q_spec, kv_spec, kv_spec] + [ANY] * n_scatter,
        out_shape=[jax.ShapeDtypeStruct((s, ATTN_DIM), F32)] * 3 + _scatter_shapes(scatter),
        scratch_shapes=[pltpu.VMEM((nq, t, t), F32), pltpu.VMEM((nq, t, t), F32), pltpu.VMEM((4, t, t), F32), pltpu.VMEM((4, t, t), F32)]
        + (_scatter_scratch(n_scatter) if n_scatter else []),
        compiler_params=_params(2),
    )(qn, kn, vb, do, *scatter)


CB_BLOCK, CC_BLOCK, CU_BLOCK = 3, 4, 5


def _shift_down(h, prev_rows, n):
    row = lax.broadcasted_iota(jnp.int32, h.shape, 0)
    out = pltpu.roll(h, n, 0)
    for r in range(n):
        out = jnp.where(row == r, prev_rows[len(prev_rows) - n + r], out)
    return out


def _shift_up(h, next_rows, n):
    tm = h.shape[0]
    row = lax.broadcasted_iota(jnp.int32, h.shape, 0)
    out = pltpu.roll(h, tm - n, 0)
    for r in range(n):
        out = jnp.where(row == tm - n + r, next_rows[r], out)
    return out


def _conv_fwd(proj, conv_w, name):
    s = proj.shape[0]
    tm = TOKEN_TILE
    nb = tm // 8

    def body(cb_ref, cc_ref, cu_ref, pc_ref, pu_ref, w_ref, o_ref):
        first = pl.program_id(0) == 0
        h = cc_ref[...] * cu_ref[...]
        prev = [jnp.where(first, 0.0, pc_ref[r : r + 1, :] * pu_ref[r : r + 1, :]) for r in (6, 7)]
        y = w_ref[0:1, :] * _shift_down(h, prev, 2) + w_ref[1:2, :] * _shift_down(h, prev, 1) + w_ref[2:3, :] * h
        o_ref[...] = (cb_ref[...] * y).astype(BF16)

    def col(block):
        return pl.BlockSpec((tm, CONV_DIM), lambda i: (i, block))

    def halo(block):
        return pl.BlockSpec((8, CONV_DIM), lambda i: (jnp.maximum(i * nb - 1, 0), block))

    return pl.pallas_call(
        body,
        name=name,
        grid=(s // tm,),
        in_specs=[col(CB_BLOCK), col(CC_BLOCK), col(CU_BLOCK), halo(CC_BLOCK), halo(CU_BLOCK), pl.BlockSpec((8, CONV_DIM), lambda i: (0, 0))],
        out_specs=pl.BlockSpec((tm, CONV_DIM), lambda i: (i, 0)),
        out_shape=jax.ShapeDtypeStruct((s, CONV_DIM), BF16),
        compiler_params=_params(1),
    )(proj, proj, proj, proj, proj, conv_w)


def _conv_bwd(proj, conv_w, dconv, name):
    s = proj.shape[0]
    tm = TOKEN_TILE
    nb = tm // 8
    n_tiles = s // tm

    def body(cb_ref, cc_ref, cu_ref, dy_ref, pc_ref, pu_ref, nb_ref, ndy_ref, w_ref, dp_ref, dw_ref):
        i = pl.program_id(0)

        @pl.when(i == 0)
        def _():
            dw_ref[...] = jnp.zeros_like(dw_ref)

        first = i == 0
        last = i == n_tiles - 1
        cc, cu, cb, dy = cc_ref[...], cu_ref[...], cb_ref[...], dy_ref[...]
        h = cc * cu
        prev = [jnp.where(first, 0.0, pc_ref[r : r + 1, :] * pu_ref[r : r + 1, :]) for r in (6, 7)]
        h1 = _shift_down(h, prev, 1)
        h2 = _shift_down(h, prev, 2)
        y = w_ref[0:1, :] * h2 + w_ref[1:2, :] * h1 + w_ref[2:3, :] * h
        dyb = dy * cb
        nxt = [jnp.where(last, 0.0, ndy_ref[r : r + 1, :] * nb_ref[r : r + 1, :]) for r in (0, 1)]
        dh = w_ref[2:3, :] * dyb + w_ref[1:2, :] * _shift_up(dyb, nxt, 1) + w_ref[0:1, :] * _shift_up(dyb, nxt, 2)
        dp_ref[:, 0:CONV_DIM] = (dy * y).astype(BF16)
        dp_ref[:, CONV_DIM : 2 * CONV_DIM] = (dh * cu).astype(BF16)
        dp_ref[:, 2 * CONV_DIM : 3 * CONV_DIM] = (dh * cc).astype(BF16)
        dw_ref[0:1, :] += jnp.sum(dyb * h2, axis=0, keepdims=True)
        dw_ref[1:2, :] += jnp.sum(dyb * h1, axis=0, keepdims=True)
        dw_ref[2:3, :] += jnp.sum(dyb * h, axis=0, keepdims=True)

    def col(block):
        return pl.BlockSpec((tm, CONV_DIM), lambda i: (i, block))

    def halo_prev(block):
        return pl.BlockSpec((8, CONV_DIM), lambda i: (jnp.maximum(i * nb - 1, 0), block))

    def halo_next(block):
        return pl.BlockSpec((8, CONV_DIM), lambda i: (jnp.minimum((i + 1) * nb, s // 8 - 1), block))

    return pl.pallas_call(
        body,
        name=name,
        grid=(n_tiles,),
        in_specs=[
            col(CB_BLOCK), col(CC_BLOCK), col(CU_BLOCK), col(0),
            halo_prev(CC_BLOCK), halo_prev(CU_BLOCK), halo_next(CB_BLOCK), halo_next(0),
            pl.BlockSpec((8, CONV_DIM), lambda i: (0, 0)),
        ],
        out_specs=[pl.BlockSpec((tm, 3 * CONV_DIM), lambda i: (i, 1)), pl.BlockSpec((8, CONV_DIM), lambda i: (0, 0))],
        out_shape=[jax.ShapeDtypeStruct((s, 3 * ATTN_DIM + 3 * CONV_DIM), BF16), jax.ShapeDtypeStruct((8, CONV_DIM), F32)],
        compiler_params=_params(1),
    )(proj, proj, proj, dconv, proj, proj, proj, dconv, conv_w)


def _out_proj(x, attn, conv, w_s, layer, name):
    s, d = x.shape
    tm = TOKEN_TILE
    rows = w_s.shape[2]

    def body(x_ref, a_ref, c_ref, w_ref, o_ref):
        acc = x_ref[...]
        for j in range(N_CHIPS):
            src = a_ref if j < 2 else c_ref
            cols = slice((j % 2) * rows, (j % 2 + 1) * rows)
            acc = acc + _dot(src[:, cols], w_ref[j, 0])
        o_ref[...] = acc

    return pl.pallas_call(
        body,
        name=name,
        grid=(s // tm,),
        in_specs=[
            pl.BlockSpec((tm, d), lambda i: (i, 0)),
            pl.BlockSpec((tm, ATTN_DIM), lambda i: (i, 0)),
            pl.BlockSpec((tm, CONV_DIM), lambda i: (i, 0)),
            pl.BlockSpec((N_CHIPS, 1, rows, d), lambda i: (0, layer, 0, 0)),
        ],
        out_specs=pl.BlockSpec((tm, d), lambda i: (i, 0)),
        out_shape=jax.ShapeDtypeStruct((s, d), F32),
        compiler_params=_params(1),
    )(x, attn, conv, w_s)


def _out_proj_bwd(dx, w_s, layer, name):
    s, d = dx.shape
    tm = TOKEN_TILE
    rows = w_s.shape[2]

    def body(dx_ref, w_ref, da_ref, dc_ref, dxb_ref):
        dxb = dx_ref[...].astype(BF16)
        dxb_ref[...] = dxb
        for j in range(N_CHIPS):
            cols = slice((j % 2) * rows, (j % 2 + 1) * rows)
            part = _dot_nt(dxb, w_ref[j, 0])
            if j < 2:
                da_ref[:, cols] = part.astype(BF16)
            else:
                dc_ref[:, cols] = part

    return pl.pallas_call(
        body,
        name=name,
        grid=(s // tm,),
        in_specs=[pl.BlockSpec((tm, d), lambda i: (i, 0)), pl.BlockSpec((N_CHIPS, 1, rows, d), lambda i: (0, layer, 0, 0))],
        out_specs=[
            pl.BlockSpec((tm, ATTN_DIM), lambda i: (i, 0)),
            pl.BlockSpec((tm, CONV_DIM), lambda i: (i, 0)),
            pl.BlockSpec((tm, d), lambda i: (i, 0)),
        ],
        out_shape=[
            jax.ShapeDtypeStruct((s, ATTN_DIM), BF16),
            jax.ShapeDtypeStruct((s, CONV_DIM), F32),
            jax.ShapeDtypeStruct((s, d), BF16),
        ],
        compiler_params=_params(1),
    )(dx, w_s)


def _ffn_fwd(x, gain, wg_s, wu_s, wd_s, layer, name, gather=()):
    s, d = x.shape
    tm = min(FFN_FWD_TILE, s)
    f = wg_s.shape[3]
    n_gather = len(gather)
    n_tiles = s // tm

    def body(*refs):
        x_ref, g_ref, wg_ref, wu_ref, wd_ref = refs[:5]
        o_ref, gate_ref, up_ref = refs[5 + n_gather : 8 + n_gather]
        h_s = refs[8 + 2 * n_gather]
        i, j = pl.program_id(0), pl.program_id(1)
        if n_gather:
            copies = _WeightGather(refs[5 : 5 + n_gather], refs[8 + n_gather : 8 + 2 * n_gather], refs[9 + 2 * n_gather :])
            pl.when((i == 0) & (j == 0))(copies.begin)
            pl.when((i == (3 * n_tiles) // 4) & (j == 0))(copies.relay)

        @pl.when(j == 0)
        def _():
            xv = x_ref[...]
            r = lax.rsqrt(jnp.mean(xv * xv, axis=-1, keepdims=True) + EPS)
            h_s[...] = (xv * r * g_ref[...]).astype(BF16)
            o_ref[...] = xv

        halves = [slice(r, r + FFN_CHUNK) for r in range(0, tm, FFN_CHUNK)]
        pre = [(_dot(h_s[r, :], wg_ref[j, 0]), _dot(h_s[r, :], wu_ref[j, 0])) for r in halves]
        act = [((gate / (1.0 + jnp.exp(-gate))) * up).astype(BF16) for gate, up in pre]
        for r, (gate, up) in zip(halves, pre):
            gate_ref[0, r, :] = gate.astype(BF16)
            up_ref[0, r, :] = up.astype(BF16)
        for r, a in zip(halves, act):
            o_ref[r, :] += _dot(a, wd_ref[j, 0])

        if n_gather:
            pl.when((i == n_tiles - 1) & (j == N_CHIPS - 1))(copies.finish)

    hid = pl.BlockSpec((1, tm, f), lambda i, j: (j, i, 0))
    hid_shape = jax.ShapeDtypeStruct((N_CHIPS, s, f), BF16)
    return pl.pallas_call(
        body,
        name=name,
        grid=(n_tiles, N_CHIPS),
        in_specs=[
            pl.BlockSpec((tm, d), lambda i, j: (i, 0)),
            pl.BlockSpec((1, d), lambda i, j: (0, 0)),
            _resident((N_CHIPS, 1, d, f), layer),
            _resident((N_CHIPS, 1, d, f), layer),
            _resident((N_CHIPS, 1, f, d), layer),
        ] + [ANY] * n_gather,
        out_specs=[pl.BlockSpec((tm, d), lambda i, j: (i, 0)), hid, hid] + [ANY] * n_gather,
        out_shape=[jax.ShapeDtypeStruct((s, d), F32), hid_shape, hid_shape]
        + [jax.ShapeDtypeStruct((N_CHIPS,) + w.shape, w.dtype) for w in gather],
        scratch_shapes=[pltpu.VMEM((tm, d), BF16)] + (_gather_scratch(n_gather) if n_gather else []),
        compiler_params=_params(2),
    )(x, gain, wg_s, wu_s, wd_s, *gather)


def _resident(block, layer):
    return pl.BlockSpec(block, lambda i, j: (0, layer, 0, 0), pipeline_mode=pl.Buffered(1))


def _rms_bwd(xv, gain, dh):
    r = lax.rsqrt(jnp.mean(xv * xv, axis=-1, keepdims=True) + EPS)
    xhat = xv * r
    dxhat = dh * gain
    dx = r * (dxhat - xhat * jnp.mean(dxhat * xhat, axis=-1, keepdims=True))
    return dx, jnp.sum(dh * xhat, axis=0, keepdims=True)


def _ffn_bwd(x, dy, gain, gate_s, up_s, wg_s, wu_s, wd_s, layer, name, scatter=()):
    s, d = x.shape
    tm = TOKEN_TILE
    f = wg_s.shape[3]

    n_scatter = len(scatter)
    n_tiles = s // tm

    def body(*refs):
        x_ref, dy_ref, g_ref, gate_ref, up_ref, wg_ref, wu_ref, wd_ref = refs[:8]
        dx_ref, dgain_ref, h_ref, dyb_ref, dg_ref, du_ref, act_ref = refs[8 + n_scatter : 15 + n_scatter]
        acc_s = refs[15 + 2 * n_scatter]
        i, j = pl.program_id(0), pl.program_id(1)
        if n_scatter:
            copies = _ChipScatter(refs[8 : 8 + n_scatter], refs[15 + n_scatter : 15 + 2 * n_scatter], refs[16 + 2 * n_scatter :])
            pl.when((i == 0) & (j == 0))(copies.begin)

        @pl.when((i == 0) & (j == 0))
        def _():
            dgain_ref[...] = jnp.zeros_like(dgain_ref)

        @pl.when(j == 0)
        def _():
            xv = x_ref[...]
            r = lax.rsqrt(jnp.mean(xv * xv, axis=-1, keepdims=True) + EPS)
            h_ref[...] = (xv * r * g_ref[...]).astype(BF16)
            dyb_ref[...] = dy_ref[...].astype(BF16)
            acc_s[...] = jnp.zeros_like(acc_s)

        halves = [slice(0, tm // 2), slice(tm // 2, tm)]
        pre = [(gate_ref[0, r, :].astype(F32), up_ref[0, r, :].astype(F32), _dot_nt(dyb_ref[r, :], wd_ref[j, 0])) for r in halves]
        grads = []
        for r, (gate, up, dact) in zip(halves, pre):
            sig = 1.0 / (1.0 + jnp.exp(-gate))
            silu = gate * sig
            dgate = (dact * up * (sig * (1.0 + gate * (1.0 - sig)))).astype(BF16)
            dup = (dact * silu).astype(BF16)
            act_ref[0, r, :] = (silu * up).astype(BF16)
            dg_ref[0, r, :] = dgate
            du_ref[0, r, :] = dup
            grads.append((dgate, dup))
        for r, (dgate, dup) in zip(halves, grads):
            acc_s[r, :] += _dot_nt(dgate, wg_ref[j, 0]) + _dot_nt(dup, wu_ref[j, 0])

        @pl.when(j == N_CHIPS - 1)
        def _():
            dxn, dgain = _rms_bwd(x_ref[...], g_ref[...], acc_s[...])
            dx_ref[...] = dy_ref[...] + dxn
            dgain_ref[...] += dgain

        if n_scatter:
            pl.when((i == n_tiles - 1) & (j == N_CHIPS - 1))(copies.finish)

    tok = pl.BlockSpec((tm, d), lambda i, j: (i, 0))
    vec = pl.BlockSpec((1, d), lambda i, j: (0, 0))
    hid = pl.BlockSpec((1, tm, f), lambda i, j: (j, i, 0))
    hid_shape = jax.ShapeDtypeStruct((N_CHIPS, s, f), BF16)
    return pl.pallas_call(
        body,
        name=name,
        grid=(n_tiles, N_CHIPS),
        in_specs=[
            tok, tok, vec, hid, hid,
            _resident((N_CHIPS, 1, d, f), layer),
            _resident((N_CHIPS, 1, d, f), layer),
            _resident((N_CHIPS, 1, f, d), layer),
        ] + [ANY] * n_scatter,
        out_specs=[tok, vec, tok, tok, hid, hid, hid] + [ANY] * n_scatter,
        out_shape=[
            jax.ShapeDtypeStruct((s, d), F32),
            jax.ShapeDtypeStruct((1, d), F32),
            jax.ShapeDtypeStruct((s, d), BF16),
            jax.ShapeDtypeStruct((s, d), BF16),
            hid_shape, hid_shape, hid_shape,
        ] + _scatter_shapes(scatter),
        scratch_shapes=[pltpu.VMEM((tm, d), F32)] + (_scatter_scratch(n_scatter) if n_scatter else []),
        compiler_params=_params(2),
    )(x, dy, gain, gate_s, up_s, wg_s, wu_s, wd_s, *scatter)


def _in_proj_bwd(x, dx_res, gain, dproj, w_s, layer, name):
    s, d = x.shape
    tm = TOKEN_TILE
    n = w_s.shape[3]

    def body(x_ref, r_ref, g_ref, dp_ref, w_ref, dx_ref, dgain_ref):
        @pl.when(pl.program_id(0) == 0)
        def _():
            dgain_ref[...] = jnp.zeros_like(dgain_ref)

        dh = _dot_nt(dp_ref[:, 0:n], w_ref[0, 0])
        for j in range(1, N_CHIPS):
            dh = dh + _dot_nt(dp_ref[:, j * n : (j + 1) * n], w_ref[j, 0])
        dxn, dgain = _rms_bwd(x_ref[...], g_ref[...], dh)
        dx_ref[...] = r_ref[...] + dxn
        dgain_ref[...] += dgain

    tok = pl.BlockSpec((tm, d), lambda i: (i, 0))
    vec = pl.BlockSpec((1, d), lambda i: (0, 0))
    return pl.pallas_call(
        body,
        name=name,
        grid=(s // tm,),
        in_specs=[tok, tok, vec, pl.BlockSpec((tm, N_CHIPS * n), lambda i: (i, 0)), pl.BlockSpec((N_CHIPS, 1, d, n), lambda i: (0, layer, 0, 0))],
        out_specs=[tok, vec],
        out_shape=[jax.ShapeDtypeStruct((s, d), F32), jax.ShapeDtypeStruct((1, d), F32)],
        compiler_params=_params(1),
    )(x, dx_res, gain, dproj, w_s)


def _loss_grad(y, target, name):
    s, d = y.shape
    tm = TOKEN_TILE

    def body(y_ref, t_ref, dy_ref, l_ref):
        @pl.when(pl.program_id(0) == 0)
        def _():
            l_ref[...] = jnp.zeros_like(l_ref)

        err = y_ref[...] - t_ref[...]
        dy_ref[...] = err / d
        l_ref[...] += jnp.sum(err * err, axis=0, keepdims=True) * (0.5 / d)

    tok = pl.BlockSpec((tm, d), lambda i: (i, 0))
    return pl.pallas_call(
        body,
        name=name,
        grid=(s // tm,),
        in_specs=[tok, tok],
        out_specs=[tok, pl.BlockSpec((1, d), lambda i: (0, 0))],
        out_shape=[jax.ShapeDtypeStruct((s, d), F32), jax.ShapeDtypeStruct((1, d), F32)],
        compiler_params=_params(1),
    )(y, target)


def _wgrad(a, b, a_spec, b_spec, n_blocks, k, n, name):
    n_tiles = a.shape[-2] // min(WGRAD_TILE, a.shape[-2])

    def body(a_ref, b_ref, o_ref):
        @pl.when(pl.program_id(1) == 0)
        def _():
            o_ref[...] = jnp.zeros_like(o_ref)

        av = a_ref[0] if len(a_ref.shape) == 3 else a_ref[...]
        bv = b_ref[0] if len(b_ref.shape) == 3 else b_ref[...]
        o_ref[0] += _dot_tn(av, bv)

    return pl.pallas_call(
        body,
        name=name,
        grid=(n_blocks, n_tiles),
        in_specs=[a_spec, b_spec],
        out_specs=pl.BlockSpec((1, k, n), lambda j, i: (j, 0, 0)),
        out_shape=jax.ShapeDtypeStruct((n_blocks, k, n), F32),
        compiler_params=_params(2),
    )(a, b)


def _mesh_position():
    return lax.axis_index("x"), lax.axis_index("y"), lax.axis_index("c")


def _other_chips(x, y):
    return [(1 - x, y), (x, 1 - y), (1 - x, 1 - y)]


def _half_rows(ref_rows, c):
    half = ref_rows // 2
    return pl.ds(c * half, half)


class _WeightGather:
    def __init__(self, ins, outs, sems):
        self.ins, self.outs = ins, outs
        send_sems, recv_sems, pass_send_sems, pass_recv_sems, self.local_sems = sems
        self.ici, self.d2d = (send_sems, recv_sems), (pass_send_sems, pass_recv_sems)
        self.x, self.y, self.c = _mesh_position()
        self.me = 2 * self.x + self.y
        self.sibling = (self.x, self.y, 1 - self.c)
        self.chips = _other_chips(self.x, self.y)

    def _copy(self, t, k, chip_index, core, to, sems, src=None):
        dst = self.outs[t].at[chip_index, :, _half_rows(self.ins[t].shape[1], core), :]
        return pltpu.make_async_remote_copy(
            src_ref=dst if src is None else src, dst_ref=dst, send_sem=sems[0].at[t, k], recv_sem=sems[1].at[t, k],
            device_id=to, device_id_type=MESH_ID,
        )

    def _own(self, t):
        return pltpu.make_async_copy(self.ins[t], self.outs[t].at[self.me], self.local_sems.at[t])

    def _sends(self):
        for t in range(len(self.ins)):
            mine = self.ins[t].at[:, _half_rows(self.ins[t].shape[1], self.c), :]
            for k, (px, py) in enumerate(self.chips):
                yield self._copy(t, k, self.me, self.c, (px, py, self.c), self.ici, src=mine)

    def _passes(self, core, sems):
        for t in range(len(self.ins)):
            for k, (px, py) in enumerate(self.chips):
                yield self._copy(t, k, 2 * px + py, core, self.sibling, sems)

    def begin(self):
        for t in range(len(self.ins)):
            self._own(t).start()
        for cp in self._sends():
            cp.start()

    def relay(self):
        for arrived, onward in zip(self._passes(self.c, self.ici), self._passes(self.c, self.d2d)):
            arrived.wait_recv()
            onward.start()

    def finish(self):
        for cp in self._passes(1 - self.c, self.d2d):
            cp.wait_recv()
        for cp in list(self._sends()) + list(self._passes(self.c, self.d2d)):
            cp.wait_send()
        for t in range(len(self.ins)):
            self._own(t).wait()


def _gather_scratch(n):
    sems = pltpu.SemaphoreType.DMA((n, N_CHIPS - 1))
    return [sems, sems, sems, sems, pltpu.SemaphoreType.DMA((n,))]


def _gather_weights(shards):
    n = len(shards)

    def body(*refs):
        gather = _WeightGather(refs[:n], refs[n : 2 * n], refs[2 * n :])
        gather.begin()
        gather.relay()
        gather.finish()

    return pl.pallas_call(
        body,
        name="gather_weights",
        in_specs=[ANY] * n,
        out_specs=[ANY] * n,
        out_shape=[jax.ShapeDtypeStruct((N_CHIPS,) + w.shape, w.dtype) for w in shards],
        scratch_shapes=_gather_scratch(n),
    )(*shards)


def _swap_halves(grads, tag):
    n = len(grads)

    def body(*refs):
        ins, outs = refs[:n], refs[n : 2 * n]
        send_sems, recv_sems = refs[2 * n :]
        x, y, c = _mesh_position()
        copies = []
        for t in range(n):
            copies.append(pltpu.make_async_remote_copy(
                src_ref=ins[t].at[:, _half_rows(ins[t].shape[1], 1 - c), :], dst_ref=outs[t],
                send_sem=send_sems.at[t], recv_sem=recv_sems.at[t], device_id=(x, y, 1 - c), device_id_type=MESH_ID,
            ))
            copies[-1].start()
        for cp in copies:
            cp.wait()

    sems = pltpu.SemaphoreType.DMA((n,))
    return pl.pallas_call(
        body,
        name=f"swap_halves_{tag}",
        in_specs=[ANY] * n,
        out_specs=[ANY] * n,
        out_shape=[jax.ShapeDtypeStruct((g.shape[0], g.shape[1] // 2, g.shape[2]), g.dtype) for g in grads],
        scratch_shapes=[sems, sems],
    )(*grads)


class _ChipScatter:
    def __init__(self, ins, outs, sems):
        self.ins, self.outs = ins, outs
        self.send_sems, self.recv_sems = sems
        self.x, self.y, self.c = _mesh_position()

    def _copies(self):
        for t in range(len(self.ins)):
            for k, (px, py) in enumerate(_other_chips(self.x, self.y)):
                yield pltpu.make_async_remote_copy(
                    src_ref=self.ins[t].at[2 * px + py], dst_ref=self.outs[t].at[k],
                    send_sem=self.send_sems.at[t, k], recv_sem=self.recv_sems.at[t, k],
                    device_id=(px, py, self.c), device_id_type=MESH_ID,
                )

    def begin(self):
        for cp in self._copies():
            cp.start()

    def finish(self):
        for cp in self._copies():
            cp.wait()


def _scatter_scratch(n):
    sems = pltpu.SemaphoreType.DMA((n, N_CHIPS - 1))
    return [sems, sems]


def _scatter_shapes(parts):
    return [jax.ShapeDtypeStruct((N_CHIPS - 1,) + p.shape[1:], p.dtype) for p in parts]


def _scatter_to_chips(parts, tag):
    n = len(parts)

    def body(*refs):
        copies = _ChipScatter(refs[:n], refs[n : 2 * n], refs[2 * n :])
        copies.begin()
        copies.finish()

    return pl.pallas_call(
        body,
        name=f"scatter_to_chips_{tag}",
        in_specs=[ANY] * n,
        out_specs=[ANY] * n,
        out_shape=_scatter_shapes(parts),
        scratch_shapes=_scatter_scratch(n),
    )(*parts)


def _join_halves(shards):
    n = len(shards)

    def body(*refs):
        outs = refs[n : 2 * n]
        send_sems, recv_sems = refs[2 * n :]
        x, y, c = _mesh_position()
        copies = []
        for t in range(n):
            mine = outs[t].at[:, _half_rows(outs[t].shape[1], c), :]
            copies.append(pltpu.make_async_remote_copy(
                src_ref=mine, dst_ref=mine, send_sem=send_sems.at[t], recv_sem=recv_sems.at[t],
                device_id=(x, y, 1 - c), device_id_type=MESH_ID,
            ))
            copies[-1].start()
        for cp in copies:
            cp.wait()

    sems = pltpu.SemaphoreType.DMA((n,))
    return pl.pallas_call(
        body,
        name="join_halves",
        in_specs=[ANY] * n,
        out_specs=[ANY] * n,
        out_shape=[jax.ShapeDtypeStruct(g.shape, g.dtype) for g in shards],
        input_output_aliases={t: t for t in range(n)},
        scratch_shapes=[sems, sems],
    )(*shards)


def _gather_small(pack):
    def body(p_ref, o_ref, send_sems, recv_sems, local_sem):
        x, y, c = _mesh_position()
        own = pltpu.make_async_copy(p_ref, o_ref.at[4 * x + 2 * y + c], local_sem)
        own.start()
        copies = []
        for k in range(1, N_DEV):
            px, py, pc = x ^ (k >> 2), y ^ ((k >> 1) & 1), c ^ (k & 1)
            send = pltpu.make_async_remote_copy(
                src_ref=p_ref, dst_ref=o_ref.at[4 * x + 2 * y + c], send_sem=send_sems.at[k - 1], recv_sem=recv_sems.at[k - 1],
                device_id=(px, py, pc), device_id_type=MESH_ID,
            )
            send.start()
            copies.append((send, 4 * px + 2 * py + pc))
        for send, peer_slot in copies:
            send.wait_send()
        for k in range(1, N_DEV):
            px, py, pc = x ^ (k >> 2), y ^ ((k >> 1) & 1), c ^ (k & 1)
            pltpu.make_async_remote_copy(
                src_ref=p_ref, dst_ref=o_ref.at[4 * px + 2 * py + pc], send_sem=send_sems.at[k - 1], recv_sem=recv_sems.at[k - 1],
                device_id=(px, py, pc), device_id_type=MESH_ID,
            ).wait_recv()
        own.wait()

    sems = pltpu.SemaphoreType.DMA((N_DEV - 1,))
    return pl.pallas_call(
        body,
        name="gather_small",
        in_specs=[VMEM_SPEC],
        out_specs=VMEM_SPEC,
        out_shape=jax.ShapeDtypeStruct((N_DEV,) + pack.shape, pack.dtype),
        scratch_shapes=[sems, sems, pltpu.SemaphoreType.DMA],
    )(pack)


def _row_tile(rows):
    for tile in range(min(rows, 512) // 8 * 8, 0, -8):
        if rows % tile == 0:
            return tile
    return rows


def _add_half(grad, received, half_index, name):
    slots, h, cdim = received.shape
    tile = _row_tile(h)
    per_half = h // tile

    def body(c_ref, g_ref, r_ref, o_ref, ob_ref):
        total = g_ref[...] + r_ref[...]
        o_ref[...] = total
        ob_ref[...] = total.astype(BF16)

    block = pl.BlockSpec((1, tile, cdim), lambda j, i, c: (j, i, 0))
    grid_spec = pltpu.PrefetchScalarGridSpec(
        num_scalar_prefetch=1,
        grid=(slots, per_half),
        in_specs=[pl.BlockSpec((1, tile, cdim), lambda j, i, c: (j, c[0] * per_half + i, 0)), block],
        out_specs=[block, block],
    )
    return pl.pallas_call(
        body, name=name, grid_spec=grid_spec,
        out_shape=[jax.ShapeDtypeStruct(received.shape, F32), jax.ShapeDtypeStruct(received.shape, BF16)],
        compiler_params=_params(2),
    )(half_index, grad, received)


def _add_chips(part, received, chip_index, core_index, layer, n_layers, shard, name):
    _, h, cdim = part.shape
    tile = _row_tile(h)
    per_half = h // tile

    def body(chip_ref, core_ref, p_ref, r_ref, *rest):
        o_ref = rest[-1]
        o_ref[0] = ((p_ref[0] + r_ref[0].astype(F32)) + r_ref[1].astype(F32)) + r_ref[2].astype(F32)

    in_specs = [
        pl.BlockSpec((1, tile, cdim), lambda i, chip, core: (chip[0], i, 0)),
        pl.BlockSpec((N_CHIPS - 1, tile, cdim), lambda i, chip, core: (0, i, 0)),
    ]
    operands = [chip_index, core_index, part, received]
    aliases = {}
    if shard is not None:
        in_specs.append(ANY)
        operands.append(shard)
        aliases = {4: 0}
    grid_spec = pltpu.PrefetchScalarGridSpec(
        num_scalar_prefetch=2,
        grid=(per_half,),
        in_specs=in_specs,
        out_specs=pl.BlockSpec((1, tile, cdim), lambda i, chip, core: (layer, core[0] * per_half + i, 0)),
    )
    return pl.pallas_call(
        body, name=name, grid_spec=grid_spec, out_shape=jax.ShapeDtypeStruct((n_layers, 2 * h, cdim), F32),
        input_output_aliases=aliases, compiler_params=_params(1),
    )(*operands)


def _adamw(w, g, m, v, name):
    rows, cdim = w.shape
    tile = _row_tile(rows)

    def body(w_ref, g_ref, m_ref, v_ref, d_ref, nm_ref, nv_ref):
        gv = g_ref[...]
        nm = ADAM_B1 * m_ref[...] + (1.0 - ADAM_B1) * gv
        nv = ADAM_B2 * v_ref[...] + (1.0 - ADAM_B2) * (gv * gv)
        m_hat = nm / (1.0 - ADAM_B1 ** ADAM_STEP)
        v_hat = nv / (1.0 - ADAM_B2 ** ADAM_STEP)
        d_ref[...] = -ADAM_LR * (m_hat / (jnp.sqrt(v_hat) + ADAM_EPS) + ADAM_WD * w_ref[...])
        nm_ref[...] = nm
        nv_ref[...] = nv

    spec = pl.BlockSpec((tile, cdim), lambda i: (i, 0))
    shape = jax.ShapeDtypeStruct((rows, cdim), F32)
    return pl.pallas_call(
        body, name=name, grid=(rows // tile,), in_specs=[spec] * 4, out_specs=[spec] * 3, out_shape=[shape] * 3,
        compiler_params=_params(1),
    )(w, g, m, v)


SMALL_ROWS, SMALL_COLS = 24, 1024
ROW_NORM_MIX, ROW_NORM_FFN, ROW_LOSS, ROW_Q_NORM, ROW_K_NORM, ROW_CONV = 0, 2, 4, 8, 10, 16


def _sum_small(gathered):
    def body(g_ref, o_ref, heads_ref, lanes_ref):
        total = g_ref[0]
        for dev in range(1, N_DEV):
            total = total + g_ref[dev]
        o_ref[...] = total
        heads = o_ref[8:16, 0:LANES]
        for grp in range(1, ATTN_DIM // LANES):
            heads = heads + o_ref[8:16, grp * LANES : (grp + 1) * LANES]
        heads_ref[...] = heads + pltpu.roll(heads, HEAD_DIM, 1)
        lanes_ref[...] = jnp.broadcast_to(jnp.sum(o_ref[0:8, :], axis=-1, keepdims=True), (8, LANES))

    return pl.pallas_call(
        body,
        name="sum_small",
        in_specs=[VMEM_SPEC],
        out_specs=[VMEM_SPEC] * 3,
        out_shape=[jax.ShapeDtypeStruct((SMALL_ROWS, SMALL_COLS), F32), jax.ShapeDtypeStruct((8, LANES), F32), jax.ShapeDtypeStruct((8, LANES), F32)],
    )(gathered)


def _pad_rows(a, rows):
    return jnp.pad(a, ((0, rows - a.shape[0]), (0, 0)))


def _pad_to(a, rows, cols):
    return jnp.pad(a, ((0, rows - a.shape[0]), (0, cols - a.shape[1])))


def _conv_taps(conv_s):
    return jnp.transpose(conv_s[:, 0, 0:8], (1, 0, 2)).reshape(8, -1)


class _GradExchange:
    def __init__(self, chip_index, core_index, n_layers):
        self.chip_index, self.core_index, self.n_layers = chip_index, core_index, n_layers
        self.shards = {}
        self.pending = None

    def offer(self, layer, grads):
        assert self.pending is None
        names = list(grads)
        received = _swap_halves([grads[k] for k in names], f"{'_'.join(names)}_{layer}")
        parts = [_add_half(grads[k], r, self.core_index, f"add_half_{k}_{layer}") for k, r in zip(names, received)]
        self.pending = (layer, names, [p32 for p32, _ in parts], [p16 for _, p16 in parts])

    def payload(self):
        return () if self.pending is None else tuple(self.pending[3])

    def take(self, received):
        layer, names, parts, _ = self.pending
        self.pending = None
        for k, p, r in zip(names, parts, received):
            self.shards[k] = _add_chips(
                p, r, self.chip_index, self.core_index, layer, self.n_layers, self.shards.get(k), f"add_chips_{k}_{layer}")

    def finish(self):
        if self.pending is not None:
            layer, names = self.pending[0], self.pending[1]
            self.take(_scatter_to_chips(list(self.pending[3]), f"{'_'.join(names)}_{layer}"))
        return dict(zip(BIG, _join_halves([self.shards[k] for k in BIG])))


def _local_step(x, target, norm_mix, q_norm, k_norm, norm_ffn, layer_weights, exchange=None):
    layer_weights = list(layer_weights)

    def carrying(kernel_fn, n_out, *args):
        if exchange is None or exchange.pending is None:
            return kernel_fn(*args)
        out = kernel_fn(*args, scatter=exchange.payload())
        exchange.take(out[n_out:])
        return out[:n_out]

    n_layers = norm_mix.shape[0]
    s, d = x.shape
    tw = min(WGRAD_TILE, s)
    n_in = layer_weights[0][0].shape[-1]
    f = layer_weights[0][2].shape[-1]
    saved = []
    for l in range(n_layers):
        weights = list(layer_weights[l])
        q_gain = jnp.tile(q_norm[l][None, :], (1, 2))
        k_gain = jnp.tile(k_norm[l][None, :], (1, 2))
        h1, proj, qn, kn, vb = _in_proj(x, norm_mix[l][None, :], weights[0], 0, q_gain, k_gain, f"in_proj_{l}")
        missing = [n for n, w in enumerate(weights) if w.ndim == 3]
        if missing:
            attn, *arrived = _attn_fwd(qn, kn, vb, f"attn_fwd_{l}", gather=tuple(weights[n] for n in missing))
            for n, w in zip(missing, arrived):
                weights[n] = w
            layer_weights[l] = tuple(weights)
        else:
            attn = _attn_fwd(qn, kn, vb, f"attn_fwd_{l}")
        _, wout_s, wg_s, wu_s, wd_s, conv_s = weights
        taps = _conv_taps(conv_s)
        conv = _conv_fwd(proj, taps, f"conv_fwd_{l}")
        x_mid = _out_proj(x, attn, conv, wout_s, 0, f"out_proj_{l}")
        pending = ()
        if l + 1 < n_layers and all(w.ndim == 3 for w in layer_weights[l + 1]):
            pending = tuple(layer_weights[l + 1])
        x_out, gate, up, *arrived = _ffn_fwd(x_mid, norm_ffn[l][None, :], wg_s, wu_s, wd_s, 0, f"ffn_fwd_{l}", gather=pending)
        if pending:
            layer_weights[l + 1] = tuple(arrived)
        saved.append(dict(x=x, h1=h1, proj=proj, qn=qn, kn=kn, vb=vb, attn=attn, conv=conv, x_mid=x_mid, q_gain=q_gain, k_gain=k_gain,
                          gate=gate, up=up, taps=taps))
        x = x_out

    dy, loss_lanes = _loss_grad(x, target, "loss_grad")
    grads = [None] * n_layers
    for l in reversed(range(n_layers)):
        sv = saved[l]
        win_s, wout_s, wg_s, wu_s, wd_s, _ = layer_weights[l]
        dx_mid, d_norm_ffn, h2, dyb, dgate, dup, act = carrying(
            _ffn_bwd, 7, sv["x_mid"], dy, norm_ffn[l][None, :], sv["gate"], sv["up"], wg_s, wu_s, wd_s, 0, f"ffn_bwd_{l}")
        tok2 = pl.BlockSpec((tw, d), lambda j, i: (i, 0))
        hid = pl.BlockSpec((1, tw, f), lambda j, i: (j, i, 0))
        d_wg = _wgrad(h2, dgate, tok2, hid, N_CHIPS, d, f, f"wgrad_gate_{l}")
        d_wu = _wgrad(h2, dup, tok2, hid, N_CHIPS, d, f, f"wgrad_up_{l}")
        d_wd = _wgrad(act, dyb, hid, tok2, N_CHIPS, f, d, f"wgrad_down_{l}")
        if exchange is not None:
            exchange.offer(l, dict(w_gate=d_wg, w_up=d_wu, w_down=d_wd))
        d_attn, d_conv, dxb = _out_proj_bwd(dx_mid, wout_s, 0, f"out_proj_bwd_{l}")
        rows_out = wout_s.shape[2]
        mix_spec_a = pl.BlockSpec((tw, rows_out), lambda j, i: (i, j))
        d_wout_a = _wgrad(sv["attn"], dxb, mix_spec_a, tok2, ATTN_DIM // rows_out, rows_out, d, f"wgrad_out_attn_{l}")
        d_wout_c = _wgrad(sv["conv"], dxb, mix_spec_a, tok2, CONV_DIM // rows_out, rows_out, d, f"wgrad_out_conv_{l}")
        d_wout = jnp.concatenate([d_wout_a, d_wout_c], axis=0)
        dq, dk, dv = carrying(_attn_bwd, 3, sv["qn"], sv["kn"], sv["vb"], d_attn, f"attn_bwd_{l}")
        dproj, d_conv_w = _conv_bwd(sv["proj"], sv["taps"], d_conv, f"conv_bwd_{l}")
        dproj, d_qg, d_kg = _qkv_prep_bwd(sv["proj"], sv["q_gain"], sv["k_gain"], dq, dk, dv, dproj, f"qkv_prep_bwd_{l}")
        d_win = _wgrad(sv["h1"], dproj, tok2, pl.BlockSpec((tw, n_in), lambda j, i: (i, j)), N_CHIPS, d, n_in, f"wgrad_in_{l}")
        if exchange is not None:
            exchange.offer(l, dict(w_in=d_win, w_out=d_wout))
        dy, d_norm_mix = _in_proj_bwd(sv["x"], dx_mid, norm_mix[l][None, :], dproj, win_s, 0, f"in_proj_bwd_{l}")
        grads[l] = dict(norm_mix=d_norm_mix, norm_ffn=d_norm_ffn, q_norm=d_qg, k_norm=d_kg, conv_w=d_conv_w,
                        w_in=d_win, w_out=d_wout, w_gate=d_wg, w_up=d_wu, w_down=d_wd)
    return loss_lanes, dy, grads


BIG = ("w_in", "w_out", "w_gate", "w_up", "w_down")


def kernel(x, norm_mix, w_in, q_norm, k_norm, conv_w, w_out, norm_ffn, w_gate, w_up, w_down, loss_target, m_norm_mix, m_w_in, m_q_norm, m_k_norm, m_conv_w, m_w_out, m_norm_ffn, m_w_gate, m_w_up, m_w_down, v_norm_mix, v_w_in, v_q_norm, v_k_norm, v_conv_w, v_w_out, v_norm_ffn, v_w_gate, v_w_up, v_w_down):
    n_layers = norm_mix.shape[0]
    weights = dict(w_in=w_in, w_out=w_out, w_gate=w_gate, w_up=w_up, w_down=w_down)
    moments_m = dict(w_in=m_w_in, w_out=m_w_out, w_gate=m_w_gate, w_up=m_w_up, w_down=m_w_down)
    moments_v = dict(w_in=v_w_in, w_out=v_w_out, w_gate=v_w_gate, w_up=v_w_up, w_down=v_w_down)
    cx, cy, cc = _mesh_position()
    chip_index = (2 * cx + cy).astype(jnp.int32).reshape(1)
    core_index = cc.astype(jnp.int32).reshape(1)

    conv_pad = jnp.pad(conv_w, ((0, 0), (0, 16 - conv_w.shape[1]), (0, 0)))

    def shards_of(layer):
        return [weights[k][layer : layer + 1].astype(BF16) for k in BIG] + [conv_pad[layer : layer + 1]]

    first = shards_of(0)
    layer_weights = [tuple(_gather_weights(first[:1])) + tuple(first[1:])] + [tuple(shards_of(layer)) for layer in range(1, n_layers)]

    exchange = _GradExchange(chip_index, core_index, n_layers)
    loss_lanes, grad_x, grads = _local_step(
        x[0], loss_target[0], norm_mix, q_norm, k_norm, norm_ffn, layer_weights, exchange)

    big_grads = exchange.finish()

    def lanes(a):
        return _pad_to(a, a.shape[0], SMALL_COLS)

    def tile_of(*groups):
        return _pad_rows(jnp.concatenate([lanes(jnp.concatenate(g, axis=0)) for g in groups], axis=0), 8)

    layers = range(n_layers)
    pack = jnp.concatenate([
        tile_of([grads[l]["norm_mix"] for l in layers], [grads[l]["norm_ffn"] for l in layers], [loss_lanes]),
        tile_of([grads[l]["q_norm"] for l in layers], [grads[l]["k_norm"] for l in layers]),
        tile_of([grads[l]["conv_w"][0:3] for l in layers]),
    ], axis=0)
    small, small_heads, small_lanes = _sum_small(_gather_small(pack))
    loss = small_lanes[ROW_LOSS, 0]
    d_model = norm_mix.shape[1]
    conv_cols = conv_w.shape[2]
    conv_all = small[ROW_CONV : ROW_CONV + 3 * n_layers, 0:CONV_DIM].reshape(n_layers, 3, CONV_DIM)
    small_grads = dict(
        norm_mix=small[ROW_NORM_MIX : ROW_NORM_MIX + n_layers, 0:d_model],
        norm_ffn=small[ROW_NORM_FFN : ROW_NORM_FFN + n_layers, 0:d_model],
        q_norm=small_heads[ROW_Q_NORM - 8 : ROW_Q_NORM - 8 + n_layers, 0:HEAD_DIM],
        k_norm=small_heads[ROW_K_NORM - 8 : ROW_K_NORM - 8 + n_layers, 0:HEAD_DIM],
        conv_w=lax.dynamic_slice_in_dim(conv_all, (2 * cx + cy) * conv_cols, conv_cols, axis=2),
    )

    out_grad, out_delta, out_m, out_v = {}, {}, {}, {}
    for k in BIG:
        shape = weights[k].shape
        view = (shape[0] * shape[1], shape[2])
        g = big_grads[k]
        delta, new_m, new_v = _adamw(weights[k].reshape(view), g.reshape(view), moments_m[k].reshape(view), moments_v[k].reshape(view), f"adamw_{k}")
        out_grad[k], out_delta[k], out_m[k], out_v[k] = g, delta.reshape(shape), new_m.reshape(shape), new_v.reshape(shape)

    small_w = dict(norm_mix=norm_mix, norm_ffn=norm_ffn, q_norm=q_norm, k_norm=k_norm, conv_w=conv_w)
    small_m = dict(norm_mix=m_norm_mix, norm_ffn=m_norm_ffn, q_norm=m_q_norm, k_norm=m_k_norm, conv_w=m_conv_w)
    small_v = dict(norm_mix=v_norm_mix, norm_ffn=v_norm_ffn, q_norm=v_q_norm, k_norm=v_k_norm, conv_w=v_conv_w)
    order = ("norm_mix", "norm_ffn", "q_norm", "k_norm", "conv_w")

    def packed(tree):
        parts2 = [_pad_to(tree[k].reshape(-1, tree[k].shape[-1]), tree[k].reshape(-1, tree[k].shape[-1]).shape[0], SMALL_COLS) for k in order]
        return _pad_rows(jnp.concatenate(parts2, axis=0), SMALL_ROWS)

    delta_p, m_p, v_p = _adamw(packed(small_w), packed(small_grads), packed(small_m), packed(small_v), "adamw_small")
    row = 0
    for k in order:
        shape = small_w[k].shape
        n_rows = 1
        for dim in shape[:-1]:
            n_rows *= dim
        cut = (slice(row, row + n_rows), slice(0, shape[-1]))
        out_grad[k] = small_grads[k]
        out_delta[k], out_m[k], out_v[k] = delta_p[cut].reshape(shape), m_p[cut].reshape(shape), v_p[cut].reshape(shape)
        row += n_rows

    names_out = ("norm_mix", "w_in", "q_norm", "k_norm", "conv_w", "w_out", "norm_ffn", "w_gate", "w_up", "w_down")
    return (loss, grad_x[None], *[out_grad[k] for k in names_out], *[out_delta[k] for k in names_out],
            *[out_m[k] for k in names_out], *[out_v[k] for k in names_out])
```

```python
import functools

import jax
import jax.numpy as jnp
from jax import lax
from jax.experimental import pallas as pl
from jax.experimental.pallas import tpu as pltpu

F32 = jnp.float32
BF16 = jnp.bfloat16

EPS = 1e-6
HEAD_DIM = 64
LANES = 128
ATTN_DIM = 512
CONV_DIM = 512
N_CHIPS = 4
N_DEV = 8
Q_SCALE = HEAD_DIM ** -0.5
ATTN_Q_TILE = 256
ATTN_TILE = 256
TOKEN_TILE = 512
WGRAD_TILE = 4096
FFN_FWD_TILE = 1024
FFN_CHUNK = 256
VMEM_LIMIT = 56 * 1024 * 1024

ADAM_LR = 0.001
ADAM_B1 = 0.9
ADAM_B2 = 0.999
ADAM_EPS = 1e-08
ADAM_WD = 0.01
ADAM_STEP = 10

MESH_ID = pl.DeviceIdType.MESH
ANY = pl.BlockSpec(memory_space=pl.ANY)
VMEM_SPEC = pl.BlockSpec(memory_space=pltpu.VMEM)


def _params(n_axes):
    return pltpu.CompilerParams(dimension_semantics=("arbitrary",) * n_axes, vmem_limit_bytes=VMEM_LIMIT)


def _dot(a, b):
    return jnp.dot(a, b, preferred_element_type=F32)


def _dot_nt(a, b):
    return lax.dot_general(a, b, (((1,), (1,)), ((), ())), preferred_element_type=F32)


def _dot_tn(a, b):
    return lax.dot_general(a, b, (((0,), (0,)), ((), ())), preferred_element_type=F32)


SCORE_MAX = 80.0
UNDERFLOW_EXIT = 90.0


def _scores(q, k):
    return jnp.minimum(_dot_nt(q, k), SCORE_MAX)


def _softplus(z):
    return jnp.log(1.0 + jnp.exp(z))


def _head_norm(xv, gain, low):
    sq = xv * xv
    s_low = jnp.sum(jnp.where(low, sq, 0.0), axis=-1, keepdims=True)
    s_high = jnp.sum(jnp.where(low, 0.0, sq), axis=-1, keepdims=True)
    r = jnp.where(low, lax.rsqrt(s_low / HEAD_DIM + EPS), lax.rsqrt(s_high / HEAD_DIM + EPS))
    return xv * r * gain, r


def _in_proj(x, gain, w_s, layer, q_gain, k_gain, name):
    s, d = x.shape
    n_blocks, _, _, n = w_s.shape
    tm = TOKEN_TILE

    def body(x_ref, g_ref, w_ref, qg_ref, kg_ref, h_ref, o_ref, q_ref, k_ref, v_ref):
        xv = x_ref[...]
        r = lax.rsqrt(jnp.mean(xv * xv, axis=-1, keepdims=True) + EPS)
        h = (xv * r * g_ref[...]).astype(BF16)
        h_ref[...] = h
        for j in range(n_blocks):
            o_ref[:, j * n : (j + 1) * n] = _dot(h, w_ref[j, 0])
        low = lax.broadcasted_iota(jnp.int32, (tm, LANES), 1) < HEAD_DIM
        for g in range(ATTN_DIM // LANES):
            cq = slice(LANES * g, LANES * (g + 1))
            ck = slice(ATTN_DIM + LANES * g, ATTN_DIM + LANES * (g + 1))
            cv = slice(2 * ATTN_DIM + LANES * g, 2 * ATTN_DIM + LANES * (g + 1))
            qn, _ = _head_norm(o_ref[:, cq], qg_ref[...], low)
            kn, _ = _head_norm(o_ref[:, ck], kg_ref[...], low)
            q_ref[:, cq] = (qn * Q_SCALE).astype(BF16)
            k_ref[:, cq] = kn.astype(BF16)
            v_ref[:, cq] = o_ref[:, cv].astype(BF16)

    head_spec = pl.BlockSpec((tm, ATTN_DIM), lambda i: (i, 0))
    head_shape = jax.ShapeDtypeStruct((s, ATTN_DIM), BF16)
    gain_spec = pl.BlockSpec((1, LANES), lambda i: (0, 0))
    return pl.pallas_call(
        body,
        name=name,
        grid=(s // tm,),
        in_specs=[
            pl.BlockSpec((tm, d), lambda i: (i, 0)),
            pl.BlockSpec((1, d), lambda i: (0, 0)),
            pl.BlockSpec((n_blocks, 1, d, n), lambda i: (0, layer, 0, 0)),
            gain_spec, gain_spec,
        ],
        out_specs=[pl.BlockSpec((tm, d), lambda i: (i, 0)), pl.BlockSpec((tm, n_blocks * n), lambda i: (i, 0)), head_spec, head_spec, head_spec],
        out_shape=[jax.ShapeDtypeStruct((s, d), BF16), jax.ShapeDtypeStruct((s, n_blocks * n), F32), head_shape, head_shape, head_shape],
        compiler_params=_params(1),
    )(x, gain, w_s, q_gain, k_gain)


def _qkv_prep_bwd(proj, q_gain, k_gain, dq, dk, dv, dproj, name):
    s = proj.shape[0]
    tm = TOKEN_TILE

    def norm_bwd(xv, gain, dy, low):
        _, r = _head_norm(xv, gain, low)
        xhat = xv * r
        dxhat = dy * gain
        prod = dxhat * xhat
        m_low = jnp.sum(jnp.where(low, prod, 0.0), axis=-1, keepdims=True)
        m_high = jnp.sum(jnp.where(low, 0.0, prod), axis=-1, keepdims=True)
        mean = jnp.where(low, m_low, m_high) / HEAD_DIM
        return r * (dxhat - xhat * mean), jnp.sum(dy * xhat, axis=0, keepdims=True)

    def body(p_ref, qg_ref, kg_ref, dq_ref, dk_ref, dv_ref, dproj_ref, dp_ref, dqg_ref, dkg_ref):
        @pl.when(pl.program_id(0) == 0)
        def _():
            dqg_ref[...] = jnp.zeros_like(dqg_ref)
            dkg_ref[...] = jnp.zeros_like(dkg_ref)

        low = lax.broadcasted_iota(jnp.int32, (tm, LANES), 1) < HEAD_DIM
        for g in range(ATTN_DIM // LANES):
            cq = slice(LANES * g, LANES * (g + 1))
            ck = slice(ATTN_DIM + LANES * g, ATTN_DIM + LANES * (g + 1))
            cv = slice(2 * ATTN_DIM + LANES * g, 2 * ATTN_DIM + LANES * (g + 1))
            dxq, dgq = norm_bwd(p_ref[:, cq], qg_ref[...], dq_ref[:, cq] * Q_SCALE, low)
            dxk, dgk = norm_bwd(p_ref[:, ck], kg_ref[...], dk_ref[:, cq], low)
            dp_ref[:, cq] = dxq.astype(BF16)
            dp_ref[:, ck] = dxk.astype(BF16)
            dp_ref[:, cv] = dv_ref[:, cq].astype(BF16)
            dqg_ref[:, cq] += dgq
            dkg_ref[:, cq] += dgk

    grad_spec = pl.BlockSpec((tm, ATTN_DIM), lambda i: (i, 0))
    gain_spec = pl.BlockSpec((1, LANES), lambda i: (0, 0))
    sum_spec = pl.BlockSpec((1, ATTN_DIM), lambda i: (0, 0))
    return pl.pallas_call(
        body,
        name=name,
        grid=(s // tm,),
        in_specs=[pl.BlockSpec((tm, 3 * ATTN_DIM), lambda i: (i, 0)), gain_spec, gain_spec, grad_spec, grad_spec, grad_spec, ANY],
        out_specs=[pl.BlockSpec((tm, 3 * ATTN_DIM), lambda i: (i, 0)), sum_spec, sum_spec],
        out_shape=[
            jax.ShapeDtypeStruct(dproj.shape, BF16),
            jax.ShapeDtypeStruct((1, ATTN_DIM), F32),
            jax.ShapeDtypeStruct((1, ATTN_DIM), F32),
        ],
        input_output_aliases={6: 0},
        compiler_params=_params(1),
    )(proj, q_gain, k_gain, dq, dk, dv, dproj)


def _attn_tile_consts(t):
    row = lax.broadcasted_iota(jnp.int32, (t, t), 0)
    col = lax.broadcasted_iota(jnp.int32, (t, t), 1)
    return row, col


def _triangle_sum(v, triangle):
    return _dot(v.astype(BF16), triangle)


def _attn_fwd(qn, kn, vb, name, gather=()):
    s = qn.shape[0]
    t = min(ATTN_TILE, s)
    tq = min(ATTN_Q_TILE, t)
    per_key_tile = t // tq
    n_gather = len(gather)
    n_pairs, n_blocks = ATTN_DIM // LANES, s // tq

    def body(*refs):
        q_ref, k_ref, v_ref = refs[:3]
        o_ref = refs[3 + n_gather]
        if n_gather:
            copies = _WeightGather(refs[3 : 3 + n_gather], refs[4 + n_gather : 4 + 2 * n_gather], refs[4 + 2 * n_gather :])
            first = (pl.program_id(0) == 0) & (pl.program_id(1) == 0)
            pl.when(first)(copies.begin)
            pl.when((pl.program_id(0) == n_pairs - 1) & (pl.program_id(1) == 0))(copies.relay)
        i = pl.program_id(1) // per_key_tile
        low = lax.broadcasted_iota(jnp.int32, (tq, LANES), 1) < HEAD_DIM
        row, col = _attn_tile_consts(t)
        suffix = (row > col).astype(BF16)
        first_row = (pl.program_id(1) % per_key_tile) * tq
        causal = lax.broadcasted_iota(jnp.int32, (tq, t), 1) < lax.broadcasted_iota(jnp.int32, (tq, t), 0) + first_row
        q = q_ref[...]
        zero_q = jnp.zeros_like(q)
        qh = (jnp.where(low, q, zero_q), jnp.where(low, zero_q, q))

        def step(kbs, carry, diagonal_first=False):
            chains = [(head, m) for head in range(2) for m in range(len(kbs))]
            masked = [diagonal_first and m == 0 for _, m in chains]
            ks = [k_ref[pl.ds(pl.multiple_of(kb * t, t), t), :] for kb in kbs]
            vs = [v_ref[pl.ds(pl.multiple_of(kb * t, t), t), :] for kb in kbs]
            z = [_scores(qh[head], ks[kb]) for head, kb in chains]
            sp = [_softplus(zc) for zc in z]
            sp = [jnp.where(causal, s_, 0.0) if mk else s_ for s_, mk in zip(sp, masked)]
            inside = [_triangle_sum(s_, suffix) for s_ in sp]
            after = [carry[head][1] for head in range(2)]
            log_a = []
            for n, (head, kb) in enumerate(chains):
                log_a.append(z[n] - sp[n] - inside[n] - after[head])
                after[head] = after[head] + jnp.sum(sp[n], axis=-1, keepdims=True)
            a = [jnp.exp(l_) for l_ in log_a]
            a = [jnp.where(causal, a_, 0.0) if mk else a_ for a_, mk in zip(a, masked)]
            acc = [carry[head][0] for head in range(2)]
            for n, (head, kb) in enumerate(chains):
                acc[head] = acc[head] + _dot(a[n].astype(BF16), vs[kb])
            return tuple((acc[head], after[head]) for head in range(2))

        def live(c):
            return jnp.minimum(jnp.min(c[0][1]), jnp.min(c[1][1])) < UNDERFLOW_EXIT

        zero = (jnp.zeros((tq, LANES), F32), jnp.zeros((tq, 1), F32))
        start = lax.cond(i >= 1, lambda c: step((i, i - 1), c, True), lambda c: step((i,), c, True), (zero, zero))
        o_ref[...] = jnp.where(low, start[0][0], start[1][0]).astype(BF16)
        rest = jnp.maximum(i - 1, 0)

        @pl.when((rest > 0) & live(start))
        def _():
            carry = lax.cond(rest % 2 == 1, lambda c: step((i - 2,), c), lambda c: c, start)
            pairs = rest // 2
            _, carry = lax.while_loop(
                lambda st: (st[0] < pairs) & live(st[1]),
                lambda st: (st[0] + 1, step((2 * (pairs - st[0]) - 1, 2 * (pairs - st[0]) - 2), st[1])),
                (jnp.int32(0), carry))
            o_ref[...] = jnp.where(low, carry[0][0], carry[1][0]).astype(BF16)

        if n_gather:
            pl.when((pl.program_id(0) == n_pairs - 1) & (pl.program_id(1) == n_blocks - 1))(copies.finish)

    out = pl.pallas_call(
        body,
        name=name,
        grid=(n_pairs, n_blocks),
        in_specs=[
            pl.BlockSpec((tq, LANES), lambda p, i: (i, p)),
            pl.BlockSpec((s, LANES), lambda p, i: (0, p)),
            pl.BlockSpec((s, LANES), lambda p, i: (0, p)),
        ] + [ANY] * n_gather,
        out_specs=[pl.BlockSpec((tq, LANES), lambda p, i: (i, p))] + [ANY] * n_gather,
        out_shape=[jax.ShapeDtypeStruct((s, ATTN_DIM), BF16)] + [jax.ShapeDtypeStruct((N_CHIPS,) + w.shape, w.dtype) for w in gather],
        scratch_shapes=_gather_scratch(n_gather) if n_gather else [],
        compiler_params=_params(2),
    )(qn, kn, vb, *gather)
    return out if n_gather else out[0]


def _attn_bwd(qn, kn, vb, do, name, scatter=()):
    s = qn.shape[0]
    t = min(ATTN_TILE, s)
    nq = s // t
    n_scatter = len(scatter)
    n_pairs = ATTN_DIM // LANES

    def body(*refs):
        q_ref, k_ref, v_ref, do_ref = refs[:4]
        dq_ref, dk_ref, dv_ref = refs[4 + n_scatter : 7 + n_scatter]
        a_s, sg_s, a_f, sg_f = refs[7 + 2 * n_scatter : 11 + 2 * n_scatter]
        i = pl.program_id(1)
        if n_scatter:
            copies = _ChipScatter(refs[4 : 4 + n_scatter], refs[7 + n_scatter : 7 + 2 * n_scatter], refs[11 + 2 * n_scatter :])
            pl.when((pl.program_id(0) == 0) & (i == 0))(copies.begin)

        @pl.when(i == 0)
        def _():
            dk_ref[...] = jnp.zeros_like(dk_ref)
            dv_ref[...] = jnp.zeros_like(dv_ref)

        low = lax.broadcasted_iota(jnp.int32, (t, LANES), 1) < HEAD_DIM
        row, col = _attn_tile_consts(t)
        suffix = (row > col).astype(BF16)
        prefix = (row < col).astype(BF16)
        causal = col < row
        q = q_ref[...]
        dob = do_ref[...]
        zero_q = jnp.zeros_like(q)
        qhs = (jnp.where(low, q, zero_q), jnp.where(low, zero_q, q))
        dohs = (jnp.where(low, dob, zero_q), jnp.where(low, zero_q, dob))

        def rows_of(kb):
            return pl.ds(pl.multiple_of(kb * t, t), t)

        pair = [(head, m) for head in range(2) for m in range(2)]

        def short_pass1():
            z = [_scores(qhs[head], k_ref[rows_of(i - m), :]) for head, m in pair]
            sp = [_softplus(z_) for z_ in z]
            sp = [jnp.where(causal, s_, 0.0) if m == 0 else s_ for s_, (_, m) in zip(sp, pair)]
            inside = [_triangle_sum(s_, suffix) for s_ in sp]
            after = [jnp.zeros((t, 1), F32), jnp.zeros((t, 1), F32)]
            for n, (head, m) in enumerate(pair):
                log_sg = z[n] - sp[n]
                a = jnp.exp(log_sg - inside[n] - after[head])
                sg = jnp.exp(log_sg)
                if m == 0:
                    a = jnp.where(causal, a, 0.0)
                    sg = jnp.where(causal, sg, 0.0)
                a_f[n] = a
                sg_f[n] = sg
                after[head] = after[head] + jnp.sum(sp[n], axis=-1, keepdims=True)
            return jnp.minimum(jnp.min(after[0]), jnp.min(after[1])) >= UNDERFLOW_EXIT

        def short_pass2():
            order = [(head, m) for head in range(2) for m in (1, 0)]
            a = {c: a_f[pair.index(c)] for c in order}
            g = {c: a[c] * _dot_nt(dohs[c[0]], v_ref[rows_of(i - c[1]), :]) for c in order}
            for m in (1, 0):
                dv_ref[rows_of(i - m), :] += _dot_tn(a[(0, m)].astype(BF16), dohs[0]) + _dot_tn(a[(1, m)].astype(BF16), dohs[1])
            inside = {c: _triangle_sum(g[c], prefix) for c in order}
            before = [jnp.zeros((t, 1), F32), jnp.zeros((t, 1), F32)]
            dz = {}
            for c in order:
                sg = sg_f[pair.index(c)]
                dz[c] = (g[c] - sg * (g[c] + inside[c] + before[c[0]])).astype(BF16)
                before[c[0]] = before[c[0]] + jnp.sum(g[c], axis=-1, keepdims=True)
            for m in (1, 0):
                dk_ref[rows_of(i - m), :] += _dot_tn(dz[(0, m)], qhs[0]) + _dot_tn(dz[(1, m)], qhs[1])
            dq = [_dot(dz[(head, 1)], k_ref[rows_of(i - 1), :]) + _dot(dz[(head, 0)], k_ref[rows_of(i), :]) for head in range(2)]
            dq_ref[...] = jnp.where(low, dq[0], dq[1])

        def general_walk():
            heads = []
            for head in range(2):
                qh, doh = qhs[head], dohs[head]

                def pass1(kbs, after, diagonal_first=False):
                    z = [_scores(qh, k_ref[rows_of(kb), :]) for kb in kbs]
                    sp = [_softplus(z_) for z_ in z]
                    if diagonal_first:
                        sp[0] = jnp.where(causal, sp[0], 0.0)
                    inside = [_triangle_sum(s_, suffix) for s_ in sp]
                    for n, kb in enumerate(kbs):
                        log_sg = z[n] - sp[n]
                        a = jnp.exp(log_sg - inside[n] - after)
                        sg = jnp.exp(log_sg)
                        if diagonal_first and n == 0:
                            a = jnp.where(causal, a, 0.0)
                            sg = jnp.where(causal, sg, 0.0)
                        a_s[kb] = a
                        sg_s[kb] = sg
                        after = after + jnp.sum(sp[n], axis=-1, keepdims=True)
                    return after

                def live(after):
                    return jnp.min(after) < UNDERFLOW_EXIT

                after = jnp.zeros((t, 1), F32)
                after = lax.cond(i >= 1, lambda c: pass1((i, i - 1), c, True), lambda c: pass1((i,), c, True), after)
                rest = jnp.maximum(i - 1, 0)
                take_single = (rest % 2 == 1) & live(after)
                after = lax.cond(take_single, lambda c: pass1((i - 2,), c), lambda c: c, after)
                pairs = rest // 2
                pairs_done, _ = lax.while_loop(
                    lambda st: (st[0] < pairs) & live(st[1]),
                    lambda st: (st[0] + 1, pass1((2 * (pairs - st[0]) - 1, 2 * (pairs - st[0]) - 2), st[1])),
                    (jnp.int32(0), after))
                walked = jnp.minimum(i, 1) + 1 + take_single.astype(jnp.int32) + 2 * pairs_done
                first = i - walked + 1

                def pass2(kbs, carry):
                    dq, before = carry
                    ks = [k_ref[rows_of(kb), :] for kb in kbs]
                    a = [a_s[kb] for kb in kbs]
                    g = [a_ * _dot_nt(doh, v_ref[rows_of(kb), :]) for a_, kb in zip(a, kbs)]
                    for n, kb in enumerate(kbs):
                        dv_ref[rows_of(kb), :] += _dot_tn(a[n].astype(BF16), doh)
                    inside = [_triangle_sum(g_, prefix) for g_ in g]
                    dz = []
                    for n, kb in enumerate(kbs):
                        sg = sg_s[kb]
                        dz.append((g[n] - sg * (g[n] + inside[n] + before)).astype(BF16))
                        before = before + jnp.sum(g[n], axis=-1, keepdims=True)
                    for n, kb in enumerate(kbs):
                        dk_ref[rows_of(kb), :] += _dot_tn(dz[n], qh)
                    for n in range(len(kbs)):
                        dq = dq + _dot(dz[n], ks[n])
                    return dq, before

                carry = (jnp.zeros((t, LANES), F32), jnp.zeros((t, 1), F32))
                carry = lax.fori_loop(0, walked // 2, lambda n, c: pass2((first + 2 * n, first + 2 * n + 1), c), carry)
                carry = lax.cond(walked % 2 == 1, lambda c: pass2((i,), c), lambda c: c, carry)
                heads.append(carry[0])
            dq_ref[...] = jnp.where(low, heads[0], heads[1])

        short = lax.cond(i >= 1, short_pass1, lambda: jnp.bool_(False))
        pl.when(short)(short_pass2)
        pl.when(jnp.logical_not(short))(general_walk)
        if n_scatter:
            pl.when((pl.program_id(0) == n_pairs - 1) & (i == nq - 1))(copies.finish)

    q_spec = pl.BlockSpec((t, LANES), lambda p, i: (i, p))
    kv_spec = pl.BlockSpec((s, LANES), lambda p, i: (0, p))
    return pl.pallas_call(
        body,
        name=name,
        grid=(n_pairs, nq),
        in_specs=[q_spec, kv_spec, kv_spec, q_spec] + [ANY] * n_scatter,
        out_specs=[q_spec, kv_spec, kv_spec] + [ANY] * n_scatter,
        out_shape=[jax.ShapeDtypeStruct((s, ATTN_DIM), F32)] * 3 + _scatter_shapes(scatter),
        scratch_shapes=[pltpu.VMEM((nq, t, t), F32), pltpu.VMEM((nq, t, t), F32), pltpu.VMEM((4, t, t), F32), pltpu.VMEM((4, t, t), F32)]
        + (_scatter_scratch(n_scatter) if n_scatter else []),
        compiler_params=_params(2),
    )(qn, kn, vb, do, *scatter)


CB_BLOCK, CC_BLOCK, CU_BLOCK = 3, 4, 5


def _shift_down(h, prev_rows, n):
    row = lax.broadcasted_iota(jnp.int32, h.shape, 0)
    out = pltpu.roll(h, n, 0)
    for r in range(n):
        out = jnp.where(row == r, prev_rows[len(prev_rows) - n + r], out)
    return out


def _shift_up(h, next_rows, n):
    tm = h.shape[0]
    row = lax.broadcasted_iota(jnp.int32, h.shape, 0)
    out = pltpu.roll(h, tm - n, 0)
    for r in range(n):
        out = jnp.where(row == tm - n + r, next_rows[r], out)
    return out


def _conv_bwd(proj, conv_w, dconv, name):
    s = proj.shape[0]
    tm = TOKEN_TILE
    nb = tm // 8
    n_tiles = s // tm

    def body(cb_ref, cc_ref, cu_ref, dy_ref, pc_ref, pu_ref, nb_ref, ndy_ref, w_ref, dp_ref, dw_ref):
        i = pl.program_id(0)

        @pl.when(i == 0)
        def _():
            dw_ref[...] = jnp.zeros_like(dw_ref)

        first = i == 0
        last = i == n_tiles - 1
        cc, cu, cb, dy = cc_ref[...], cu_ref[...], cb_ref[...], dy_ref[...]
        h = cc * cu
        prev = [jnp.where(first, 0.0, pc_ref[r : r + 1, :] * pu_ref[r : r + 1, :]) for r in (6, 7)]
        h1 = _shift_down(h, prev, 1)
        h2 = _shift_down(h, prev, 2)
        y = w_ref[0:1, :] * h2 + w_ref[1:2, :] * h1 + w_ref[2:3, :] * h
        dyb = dy * cb
        nxt = [jnp.where(last, 0.0, ndy_ref[r : r + 1, :] * nb_ref[r : r + 1, :]) for r in (0, 1)]
        dh = w_ref[2:3, :] * dyb + w_ref[1:2, :] * _shift_up(dyb, nxt, 1) + w_ref[0:1, :] * _shift_up(dyb, nxt, 2)
        dp_ref[:, 0:CONV_DIM] = (dy * y).astype(BF16)
        dp_ref[:, CONV_DIM : 2 * CONV_DIM] = (dh * cu).astype(BF16)
        dp_ref[:, 2 * CONV_DIM : 3 * CONV_DIM] = (dh * cc).astype(BF16)
        dw_ref[0:1, :] += jnp.sum(dyb * h2, axis=0, keepdims=True)
        dw_ref[1:2, :] += jnp.sum(dyb * h1, axis=0, keepdims=True)
        dw_ref[2:3, :] += jnp.sum(dyb * h, axis=0, keepdims=True)

    def col(block):
        return pl.BlockSpec((tm, CONV_DIM), lambda i: (i, block))

    def halo_prev(block):
        return pl.BlockSpec((8, CONV_DIM), lambda i: (jnp.maximum(i * nb - 1, 0), block))

    def halo_next(block):
        return pl.BlockSpec((8, CONV_DIM), lambda i: (jnp.minimum((i + 1) * nb, s // 8 - 1), block))

    return pl.pallas_call(
        body,
        name=name,
        grid=(n_tiles,),
        in_specs=[
            col(CB_BLOCK), col(CC_BLOCK), col(CU_BLOCK), col(0),
            halo_prev(CC_BLOCK), halo_prev(CU_BLOCK), halo_next(CB_BLOCK), halo_next(0),
            pl.BlockSpec((8, CONV_DIM), lambda i: (0, 0)),
        ],
        out_specs=[pl.BlockSpec((tm, 3 * CONV_DIM), lambda i: (i, 1)), pl.BlockSpec((8, CONV_DIM), lambda i: (0, 0))],
        out_shape=[jax.ShapeDtypeStruct((s, 3 * ATTN_DIM + 3 * CONV_DIM), BF16), jax.ShapeDtypeStruct((8, CONV_DIM), F32)],
        compiler_params=_params(1),
    )(proj, proj, proj, dconv, proj, proj, proj, dconv, conv_w)


def _out_proj(x, attn, proj, conv_w, w_s, layer, name):
    s, d = x.shape
    tm = TOKEN_TILE
    nb = tm // 8
    rows = w_s.shape[2]

    def body(x_ref, a_ref, cb_ref, cc_ref, cu_ref, pc_ref, pu_ref, cw_ref, w_ref, o_ref, c_ref):
        first = pl.program_id(0) == 0
        h = cc_ref[...] * cu_ref[...]
        prev = [jnp.where(first, 0.0, pc_ref[r : r + 1, :] * pu_ref[r : r + 1, :]) for r in (6, 7)]
        y = cw_ref[0:1, :] * _shift_down(h, prev, 2) + cw_ref[1:2, :] * _shift_down(h, prev, 1) + cw_ref[2:3, :] * h
        c_ref[...] = (cb_ref[...] * y).astype(BF16)
        acc = x_ref[...]
        for j in range(N_CHIPS):
            src = a_ref if j < 2 else c_ref
            cols = slice((j % 2) * rows, (j % 2 + 1) * rows)
            acc = acc + _dot(src[:, cols], w_ref[j, 0])
        o_ref[...] = acc

    def col(block):
        return pl.BlockSpec((tm, CONV_DIM), lambda i: (i, block))

    def halo(block):
        return pl.BlockSpec((8, CONV_DIM), lambda i: (jnp.maximum(i * nb - 1, 0), block))

    return pl.pallas_call(
        body,
        name=name,
        grid=(s // tm,),
        in_specs=[
            pl.BlockSpec((tm, d), lambda i: (i, 0)),
            pl.BlockSpec((tm, ATTN_DIM), lambda i: (i, 0)),
            col(CB_BLOCK), col(CC_BLOCK), col(CU_BLOCK), halo(CC_BLOCK), halo(CU_BLOCK),
            pl.BlockSpec((8, CONV_DIM), lambda i: (0, 0)),
            pl.BlockSpec((N_CHIPS, 1, rows, d), lambda i: (0, layer, 0, 0)),
        ],
        out_specs=[pl.BlockSpec((tm, d), lambda i: (i, 0)), pl.BlockSpec((tm, CONV_DIM), lambda i: (i, 0))],
        out_shape=[jax.ShapeDtypeStruct((s, d), F32), jax.ShapeDtypeStruct((s, CONV_DIM), BF16)],
        compiler_params=_params(1),
    )(x, attn, proj, proj, proj, proj, proj, conv_w, w_s)


def _ffn_fwd(x, gain, wg_s, wu_s, wd_s, layer, name, gather=()):
    s, d = x.shape
    tm = min(FFN_FWD_TILE, s)
    f = wg_s.shape[3]
    n_gather = len(gather)
    n_tiles = s // tm

    def body(*refs):
        x_ref, g_ref, wg_ref, wu_ref, wd_ref = refs[:5]
        o_ref, gate_ref, up_ref = refs[5 + n_gather : 8 + n_gather]
        h_s = refs[8 + 2 * n_gather]
        i, j = pl.program_id(0), pl.program_id(1)
        if n_gather:
            copies = _WeightGather(refs[5 : 5 + n_gather], refs[8 + n_gather : 8 + 2 * n_gather], refs[9 + 2 * n_gather :])
            pl.when((i == 0) & (j == 0))(copies.begin)
            pl.when((i == (3 * n_tiles) // 4) & (j == 0))(copies.relay)

        @pl.when(j == 0)
        def _():
            xv = x_ref[...]
            r = lax.rsqrt(jnp.mean(xv * xv, axis=-1, keepdims=True) + EPS)
            h_s[...] = (xv * r * g_ref[...]).astype(BF16)
            o_ref[...] = xv

        halves = [slice(r, r + FFN_CHUNK) for r in range(0, tm, FFN_CHUNK)]
        pre = [(_dot(h_s[r, :], wg_ref[j, 0]), _dot(h_s[r, :], wu_ref[j, 0])) for r in halves]
        act = [((gate / (1.0 + jnp.exp(-gate))) * up).astype(BF16) for gate, up in pre]
        for r, (gate, up) in zip(halves, pre):
            gate_ref[0, r, :] = gate.astype(BF16)
            up_ref[0, r, :] = up.astype(BF16)
        for r, a in zip(halves, act):
            o_ref[r, :] += _dot(a, wd_ref[j, 0])

        if n_gather:
            pl.when((i == n_tiles - 1) & (j == N_CHIPS - 1))(copies.finish)

    hid = pl.BlockSpec((1, tm, f), lambda i, j: (j, i, 0))
    hid_shape = jax.ShapeDtypeStruct((N_CHIPS, s, f), BF16)
    return pl.pallas_call(
        body,
        name=name,
        grid=(n_tiles, N_CHIPS),
        in_specs=[
            pl.BlockSpec((tm, d), lambda i, j: (i, 0)),
            pl.BlockSpec((1, d), lambda i, j: (0, 0)),
            _resident((N_CHIPS, 1, d, f), layer),
            _resident((N_CHIPS, 1, d, f), layer),
            _resident((N_CHIPS, 1, f, d), layer),
        ] + [ANY] * n_gather,
        out_specs=[pl.BlockSpec((tm, d), lambda i, j: (i, 0)), hid, hid] + [ANY] * n_gather,
        out_shape=[jax.ShapeDtypeStruct((s, d), F32), hid_shape, hid_shape]
        + [jax.ShapeDtypeStruct((N_CHIPS,) + w.shape, w.dtype) for w in gather],
        scratch_shapes=[pltpu.VMEM((tm, d), BF16)] + (_gather_scratch(n_gather) if n_gather else []),
        compiler_params=_params(2),
    )(x, gain, wg_s, wu_s, wd_s, *gather)


def _resident(block, layer):
    return pl.BlockSpec(block, lambda i, j: (0, layer, 0, 0), pipeline_mode=pl.Buffered(1))


def _rms_bwd(xv, gain, dh):
    r = lax.rsqrt(jnp.mean(xv * xv, axis=-1, keepdims=True) + EPS)
    xhat = xv * r
    dxhat = dh * gain
    dx = r * (dxhat - xhat * jnp.mean(dxhat * xhat, axis=-1, keepdims=True))
    return dx, jnp.sum(dh * xhat, axis=0, keepdims=True)


def _ffn_bwd(x, dy, gain, gate_s, up_s, wg_s, wu_s, wd_s, wout_s, layer, name, scatter=()):
    s, d = x.shape
    tm = TOKEN_TILE
    f = wg_s.shape[3]

    n_scatter = len(scatter)
    n_tiles = s // tm
    out_rows = wout_s.shape[2]

    def body(*refs):
        x_ref, dy_ref, g_ref, gate_ref, up_ref, wg_ref, wu_ref, wd_ref, wo_ref = refs[:9]
        dx_ref, dgain_ref, h_ref, dyb_ref, dg_ref, du_ref, act_ref, da_ref, dc_ref, dxb_ref = refs[9 + n_scatter : 19 + n_scatter]
        acc_s = refs[19 + 2 * n_scatter]
        i, j = pl.program_id(0), pl.program_id(1)
        if n_scatter:
            copies = _ChipScatter(refs[9 : 9 + n_scatter], refs[19 + n_scatter : 19 + 2 * n_scatter], refs[20 + 2 * n_scatter :])
            pl.when((i == 0) & (j == 0))(copies.begin)

        @pl.when((i == 0) & (j == 0))
        def _():
            dgain_ref[...] = jnp.zeros_like(dgain_ref)

        @pl.when(j == 0)
        def _():
            xv = x_ref[...]
            r = lax.rsqrt(jnp.mean(xv * xv, axis=-1, keepdims=True) + EPS)
            h_ref[...] = (xv * r * g_ref[...]).astype(BF16)
            dyb_ref[...] = dy_ref[...].astype(BF16)
            acc_s[...] = jnp.zeros_like(acc_s)

        halves = [slice(0, tm // 2), slice(tm // 2, tm)]
        pre = [(gate_ref[0, r, :].astype(F32), up_ref[0, r, :].astype(F32), _dot_nt(dyb_ref[r, :], wd_ref[0, 0])) for r in halves]
        grads = []
        for r, (gate, up, dact) in zip(halves, pre):
            sig = 1.0 / (1.0 + jnp.exp(-gate))
            silu = gate * sig
            dgate = (dact * up * (sig * (1.0 + gate * (1.0 - sig)))).astype(BF16)
            dup = (dact * silu).astype(BF16)
            act_ref[0, r, :] = (silu * up).astype(BF16)
            dg_ref[0, r, :] = dgate
            du_ref[0, r, :] = dup
            grads.append((dgate, dup))
        for r, (dgate, dup) in zip(halves, grads):
            acc_s[r, :] += _dot_nt(dgate, wg_ref[0, 0]) + _dot_nt(dup, wu_ref[0, 0])

        @pl.when(j == N_CHIPS - 1)
        def _():
            dxn, dgain = _rms_bwd(x_ref[...], g_ref[...], acc_s[...])
            dx = dy_ref[...] + dxn
            dx_ref[...] = dx
            dgain_ref[...] += dgain
            dxb = dx.astype(BF16)
            dxb_ref[...] = dxb
            for blk in range(N_CHIPS):
                cols = slice((blk % 2) * out_rows, (blk % 2 + 1) * out_rows)
                part = _dot_nt(dxb, wo_ref[blk, 0])
                if blk < 2:
                    da_ref[:, cols] = part.astype(BF16)
                else:
                    dc_ref[:, cols] = part

        if n_scatter:
            pl.when((i == n_tiles - 1) & (j == N_CHIPS - 1))(copies.finish)

    tok = pl.BlockSpec((tm, d), lambda i, j: (i, 0))
    vec = pl.BlockSpec((1, d), lambda i, j: (0, 0))
    hid = pl.BlockSpec((1, tm, f), lambda i, j: (j, i, 0))
    hid_shape = jax.ShapeDtypeStruct((N_CHIPS, s, f), BF16)
    mix = pl.BlockSpec((tm, ATTN_DIM), lambda i, j: (i, 0))
    return pl.pallas_call(
        body,
        name=name,
        grid=(n_tiles, N_CHIPS),
        in_specs=[
            tok, tok, vec, hid, hid,
            pl.BlockSpec((1, 1, d, f), lambda i, j: (j, layer, 0, 0)),
            pl.BlockSpec((1, 1, d, f), lambda i, j: (j, layer, 0, 0)),
            pl.BlockSpec((1, 1, f, d), lambda i, j: (j, layer, 0, 0)),
            _resident((N_CHIPS, 1, out_rows, d), layer),
        ] + [ANY] * n_scatter,
        out_specs=[tok, vec, tok, tok, hid, hid, hid, mix, mix, tok] + [ANY] * n_scatter,
        out_shape=[
            jax.ShapeDtypeStruct((s, d), F32),
            jax.ShapeDtypeStruct((1, d), F32),
            jax.ShapeDtypeStruct((s, d), BF16),
            jax.ShapeDtypeStruct((s, d), BF16),
            hid_shape, hid_shape, hid_shape,
            jax.ShapeDtypeStruct((s, ATTN_DIM), BF16),
            jax.ShapeDtypeStruct((s, CONV_DIM), F32),
            jax.ShapeDtypeStruct((s, d), BF16),
        ] + _scatter_shapes(scatter),
        scratch_shapes=[pltpu.VMEM((tm, d), F32)] + (_scatter_scratch(n_scatter) if n_scatter else []),
        compiler_params=_params(2),
    )(x, dy, gain, gate_s, up_s, wg_s, wu_s, wd_s, wout_s, *scatter)


def _in_proj_bwd(x, dx_res, gain, dproj, w_s, layer, name):
    s, d = x.shape
    tm = TOKEN_TILE
    n = w_s.shape[3]

    def body(x_ref, r_ref, g_ref, dp_ref, w_ref, dx_ref, dgain_ref):
        @pl.when(pl.program_id(0) == 0)
        def _():
            dgain_ref[...] = jnp.zeros_like(dgain_ref)

        dh = _dot_nt(dp_ref[:, 0:n], w_ref[0, 0])
        for j in range(1, N_CHIPS):
            dh = dh + _dot_nt(dp_ref[:, j * n : (j + 1) * n], w_ref[j, 0])
        dxn, dgain = _rms_bwd(x_ref[...], g_ref[...], dh)
        dx_ref[...] = r_ref[...] + dxn
        dgain_ref[...] += dgain

    tok = pl.BlockSpec((tm, d), lambda i: (i, 0))
    vec = pl.BlockSpec((1, d), lambda i: (0, 0))
    return pl.pallas_call(
        body,
        name=name,
        grid=(s // tm,),
        in_specs=[tok, tok, vec, pl.BlockSpec((tm, N_CHIPS * n), lambda i: (i, 0)), pl.BlockSpec((N_CHIPS, 1, d, n), lambda i: (0, layer, 0, 0))],
        out_specs=[tok, vec],
        out_shape=[jax.ShapeDtypeStruct((s, d), F32), jax.ShapeDtypeStruct((1, d), F32)],
        compiler_params=_params(1),
    )(x, dx_res, gain, dproj, w_s)


def _loss_grad(y, target, name):
    s, d = y.shape
    tm = TOKEN_TILE

    def body(y_ref, t_ref, dy_ref, l_ref):
        @pl.when(pl.program_id(0) == 0)
        def _():
            l_ref[...] = jnp.zeros_like(l_ref)

        err = y_ref[...] - t_ref[...]
        dy_ref[...] = err / d
        l_ref[...] += jnp.sum(err * err, axis=0, keepdims=True) * (0.5 / d)

    tok = pl.BlockSpec((tm, d), lambda i: (i, 0))
    return pl.pallas_call(
        body,
        name=name,
        grid=(s // tm,),
        in_specs=[tok, tok],
        out_specs=[tok, pl.BlockSpec((1, d), lambda i: (0, 0))],
        out_shape=[jax.ShapeDtypeStruct((s, d), F32), jax.ShapeDtypeStruct((1, d), F32)],
        compiler_params=_params(1),
    )(y, target)


def _wgrad(a, b, a_spec, b_spec, n_blocks, k, n, name):
    n_tiles = a.shape[-2] // min(WGRAD_TILE, a.shape[-2])

    def body(a_ref, b_ref, o_ref):
        @pl.when(pl.program_id(1) == 0)
        def _():
            o_ref[...] = jnp.zeros_like(o_ref)

        av = a_ref[0] if len(a_ref.shape) == 3 else a_ref[...]
        bv = b_ref[0] if len(b_ref.shape) == 3 else b_ref[...]
        o_ref[0] += _dot_tn(av, bv)

    return pl.pallas_call(
        body,
        name=name,
        grid=(n_blocks, n_tiles),
        in_specs=[a_spec, b_spec],
        out_specs=pl.BlockSpec((1, k, n), lambda j, i: (j, 0, 0)),
        out_shape=jax.ShapeDtypeStruct((n_blocks, k, n), F32),
        compiler_params=_params(2),
    )(a, b)


def _mesh_position():
    return lax.axis_index("x"), lax.axis_index("y"), lax.axis_index("c")


def _other_chips(x, y):
    return [(1 - x, y), (x, 1 - y), (1 - x, 1 - y)]


def _half_rows(ref_rows, c):
    half = ref_rows // 2
    return pl.ds(c * half, half)


class _WeightGather:
    def __init__(self, ins, outs, sems):
        self.ins, self.outs = ins, outs
        send_sems, recv_sems, pass_send_sems, pass_recv_sems, self.local_sems = sems
        self.ici, self.d2d = (send_sems, recv_sems), (pass_send_sems, pass_recv_sems)
        self.x, self.y, self.c = _mesh_position()
        self.me = 2 * self.x + self.y
        self.sibling = (self.x, self.y, 1 - self.c)
        self.chips = _other_chips(self.x, self.y)

    def _copy(self, t, k, chip_index, core, to, sems, src=None):
        dst = self.outs[t].at[chip_index, :, _half_rows(self.ins[t].shape[1], core), :]
        return pltpu.make_async_remote_copy(
            src_ref=dst if src is None else src, dst_ref=dst, send_sem=sems[0].at[t, k], recv_sem=sems[1].at[t, k],
            device_id=to, device_id_type=MESH_ID,
        )

    def _own(self, t):
        return pltpu.make_async_copy(self.ins[t], self.outs[t].at[self.me], self.local_sems.at[t])

    def _sends(self):
        for t in range(len(self.ins)):
            mine = self.ins[t].at[:, _half_rows(self.ins[t].shape[1], self.c), :]
            for k, (px, py) in enumerate(self.chips):
                yield self._copy(t, k, self.me, self.c, (px, py, self.c), self.ici, src=mine)

    def _passes(self, core, sems):
        for t in range(len(self.ins)):
            for k, (px, py) in enumerate(self.chips):
                yield self._copy(t, k, 2 * px + py, core, self.sibling, sems)

    def begin(self):
        for t in range(len(self.ins)):
            self._own(t).start()
        for cp in self._sends():
            cp.start()

    def relay(self):
        for arrived, onward in zip(self._passes(self.c, self.ici), self._passes(self.c, self.d2d)):
            arrived.wait_recv()
            onward.start()

    def finish(self):
        for cp in self._passes(1 - self.c, self.d2d):
            cp.wait_recv()
        for cp in list(self._sends()) + list(self._passes(self.c, self.d2d)):
            cp.wait_send()
        for t in range(len(self.ins)):
            self._own(t).wait()


def _gather_scratch(n):
    sems = pltpu.SemaphoreType.DMA((n, N_CHIPS - 1))
    return [sems, sems, sems, sems, pltpu.SemaphoreType.DMA((n,))]


def _gather_weights(shards):
    n = len(shards)

    def body(*refs):
        gather = _WeightGather(refs[:n], refs[n : 2 * n], refs[2 * n :])
        gather.begin()
        gather.relay()
        gather.finish()

    return pl.pallas_call(
        body,
        name="gather_weights",
        in_specs=[ANY] * n,
        out_specs=[ANY] * n,
        out_shape=[jax.ShapeDtypeStruct((N_CHIPS,) + w.shape, w.dtype) for w in shards],
        scratch_shapes=_gather_scratch(n),
    )(*shards)


def _swap_halves(grads, tag):
    n = len(grads)

    def body(*refs):
        ins, outs = refs[:n], refs[n : 2 * n]
        send_sems, recv_sems = refs[2 * n :]
        x, y, c = _mesh_position()
        copies = []
        for t in range(n):
            copies.append(pltpu.make_async_remote_copy(
                src_ref=ins[t].at[:, _half_rows(ins[t].shape[1], 1 - c), :], dst_ref=outs[t],
                send_sem=send_sems.at[t], recv_sem=recv_sems.at[t], device_id=(x, y, 1 - c), device_id_type=MESH_ID,
            ))
            copies[-1].start()
        for cp in copies:
            cp.wait()

    sems = pltpu.SemaphoreType.DMA((n,))
    return pl.pallas_call(
        body,
        name=f"swap_halves_{tag}",
        in_specs=[ANY] * n,
        out_specs=[ANY] * n,
        out_shape=[jax.ShapeDtypeStruct((g.shape[0], g.shape[1] // 2, g.shape[2]), g.dtype) for g in grads],
        scratch_shapes=[sems, sems],
    )(*grads)


class _ChipScatter:
    def __init__(self, ins, outs, sems):
        self.ins, self.outs = ins, outs
        self.send_sems, self.recv_sems = sems
        self.x, self.y, self.c = _mesh_position()

    def _copies(self):
        for t in range(len(self.ins)):
            for k, (px, py) in enumerate(_other_chips(self.x, self.y)):
                yield pltpu.make_async_remote_copy(
                    src_ref=self.ins[t].at[2 * px + py], dst_ref=self.outs[t].at[k],
                    send_sem=self.send_sems.at[t, k], recv_sem=self.recv_sems.at[t, k],
                    device_id=(px, py, self.c), device_id_type=MESH_ID,
                )

    def begin(self):
        for cp in self._copies():
            cp.start()

    def finish(self):
        for cp in self._copies():
            cp.wait()


def _scatter_scratch(n):
    sems = pltpu.SemaphoreType.DMA((n, N_CHIPS - 1))
    return [sems, sems]


def _scatter_shapes(parts):
    return [jax.ShapeDtypeStruct((N_CHIPS - 1,) + p.shape[1:], p.dtype) for p in parts]


def _scatter_to_chips(parts, tag):
    n = len(parts)

    def body(*refs):
        copies = _ChipScatter(refs[:n], refs[n : 2 * n], refs[2 * n :])
        copies.begin()
        copies.finish()

    return pl.pallas_call(
        body,
        name=f"scatter_to_chips_{tag}",
        in_specs=[ANY] * n,
        out_specs=[ANY] * n,
        out_shape=_scatter_shapes(parts),
        scratch_shapes=_scatter_scratch(n),
    )(*parts)


def _join_halves(shards):
    n = len(shards)

    def body(*refs):
        outs = refs[n : 2 * n]
        send_sems, recv_sems = refs[2 * n :]
        x, y, c = _mesh_position()
        copies = []
        for t in range(n):
            mine = outs[t].at[:, _half_rows(outs[t].shape[1], c), :]
            copies.append(pltpu.make_async_remote_copy(
                src_ref=mine, dst_ref=mine, send_sem=send_sems.at[t], recv_sem=recv_sems.at[t],
                device_id=(x, y, 1 - c), device_id_type=MESH_ID,
            ))
            copies[-1].start()
        for cp in copies:
            cp.wait()

    sems = pltpu.SemaphoreType.DMA((n,))
    return pl.pallas_call(
        body,
        name="join_halves",
        in_specs=[ANY] * n,
        out_specs=[ANY] * n,
        out_shape=[jax.ShapeDtypeStruct(g.shape, g.dtype) for g in shards],
        input_output_aliases={t: t for t in range(n)},
        scratch_shapes=[sems, sems],
    )(*shards)


def _gather_small(pack):
    def body(p_ref, o_ref, send_sems, recv_sems, local_sem):
        x, y, c = _mesh_position()
        own = pltpu.make_async_copy(p_ref, o_ref.at[4 * x + 2 * y + c], local_sem)
        own.start()
        copies = []
        for k in range(1, N_DEV):
            px, py, pc = x ^ (k >> 2), y ^ ((k >> 1) & 1), c ^ (k & 1)
            send = pltpu.make_async_remote_copy(
                src_ref=p_ref, dst_ref=o_ref.at[4 * x + 2 * y + c], send_sem=send_sems.at[k - 1], recv_sem=recv_sems.at[k - 1],
                device_id=(px, py, pc), device_id_type=MESH_ID,
            )
            send.start()
            copies.append((send, 4 * px + 2 * py + pc))
        for send, peer_slot in copies:
            send.wait_send()
        for k in range(1, N_DEV):
            px, py, pc = x ^ (k >> 2), y ^ ((k >> 1) & 1), c ^ (k & 1)
            pltpu.make_async_remote_copy(
                src_ref=p_ref, dst_ref=o_ref.at[4 * px + 2 * py + pc], send_sem=send_sems.at[k - 1], recv_sem=recv_sems.at[k - 1],
                device_id=(px, py, pc), device_id_type=MESH_ID,
            ).wait_recv()
        own.wait()

    sems = pltpu.SemaphoreType.DMA((N_DEV - 1,))
    return pl.pallas_call(
        body,
        name="gather_small",
        in_specs=[VMEM_SPEC],
        out_specs=VMEM_SPEC,
        out_shape=jax.ShapeDtypeStruct((N_DEV,) + pack.shape, pack.dtype),
        scratch_shapes=[sems, sems, pltpu.SemaphoreType.DMA],
    )(pack)


def _row_tile(rows):
    for tile in range(min(rows, 512) // 8 * 8, 0, -8):
        if rows % tile == 0:
            return tile
    return rows


def _add_half(grad, received, half_index, name):
    slots, h, cdim = received.shape
    tile = _row_tile(h)
    per_half = h // tile

    def body(c_ref, g_ref, r_ref, o_ref, ob_ref):
        total = g_ref[...] + r_ref[...]
        o_ref[...] = total
        ob_ref[...] = total.astype(BF16)

    block = pl.BlockSpec((1, tile, cdim), lambda j, i, c: (j, i, 0))
    grid_spec = pltpu.PrefetchScalarGridSpec(
        num_scalar_prefetch=1,
        grid=(slots, per_half),
        in_specs=[pl.BlockSpec((1, tile, cdim), lambda j, i, c: (j, c[0] * per_half + i, 0)), block],
        out_specs=[block, block],
    )
    return pl.pallas_call(
        body, name=name, grid_spec=grid_spec,
        out_shape=[jax.ShapeDtypeStruct(received.shape, F32), jax.ShapeDtypeStruct(received.shape, BF16)],
        compiler_params=_params(2),
    )(half_index, grad, received)


def _add_chips(part, received, chip_index, core_index, layer, n_layers, shard, name):
    _, h, cdim = part.shape
    tile = _row_tile(h)
    per_half = h // tile

    def body(chip_ref, core_ref, p_ref, r_ref, *rest):
        o_ref = rest[-1]
        o_ref[0] = ((p_ref[0] + r_ref[0].astype(F32)) + r_ref[1].astype(F32)) + r_ref[2].astype(F32)

    in_specs = [
        pl.BlockSpec((1, tile, cdim), lambda i, chip, core: (chip[0], i, 0)),
        pl.BlockSpec((N_CHIPS - 1, tile, cdim), lambda i, chip, core: (0, i, 0)),
    ]
    operands = [chip_index, core_index, part, received]
    aliases = {}
    if shard is not None:
        in_specs.append(ANY)
        operands.append(shard)
        aliases = {4: 0}
    grid_spec = pltpu.PrefetchScalarGridSpec(
        num_scalar_prefetch=2,
        grid=(per_half,),
        in_specs=in_specs,
        out_specs=pl.BlockSpec((1, tile, cdim), lambda i, chip, core: (layer, core[0] * per_half + i, 0)),
    )
    return pl.pallas_call(
        body, name=name, grid_spec=grid_spec, out_shape=jax.ShapeDtypeStruct((n_layers, 2 * h, cdim), F32),
        input_output_aliases=aliases, compiler_params=_params(1),
    )(*operands)


def _adamw(w, g, m, v, name):
    rows, cdim = w.shape
    tile = _row_tile(rows)

    def body(w_ref, g_ref, m_ref, v_ref, d_ref, nm_ref, nv_ref):
        gv = g_ref[...]
        nm = ADAM_B1 * m_ref[...] + (1.0 - ADAM_B1) * gv
        nv = ADAM_B2 * v_ref[...] + (1.0 - ADAM_B2) * (gv * gv)
        m_hat = nm / (1.0 - ADAM_B1 ** ADAM_STEP)
        v_hat = nv / (1.0 - ADAM_B2 ** ADAM_STEP)
        d_ref[...] = -ADAM_LR * (m_hat / (jnp.sqrt(v_hat) + ADAM_EPS) + ADAM_WD * w_ref[...])
        nm_ref[...] = nm
        nv_ref[...] = nv

    spec = pl.BlockSpec((tile, cdim), lambda i: (i, 0))
    shape = jax.ShapeDtypeStruct((rows, cdim), F32)
    return pl.pallas_call(
        body, name=name, grid=(rows // tile,), in_specs=[spec] * 4, out_specs=[spec] * 3, out_shape=[shape] * 3,
        compiler_params=_params(1),
    )(w, g, m, v)


SMALL_ROWS, SMALL_COLS = 24, 1024
ROW_NORM_MIX, ROW_NORM_FFN, ROW_LOSS, ROW_Q_NORM, ROW_K_NORM, ROW_CONV = 0, 2, 4, 8, 10, 16


def _sum_small(gathered):
    def body(g_ref, o_ref, heads_ref, lanes_ref):
        total = g_ref[0]
        for dev in range(1, N_DEV):
            total = total + g_ref[dev]
        o_ref[...] = total
        heads = o_ref[8:16, 0:LANES]
        for grp in range(1, ATTN_DIM // LANES):
            heads = heads + o_ref[8:16, grp * LANES : (grp + 1) * LANES]
        heads_ref[...] = heads + pltpu.roll(heads, HEAD_DIM, 1)
        lanes_ref[...] = jnp.broadcast_to(jnp.sum(o_ref[0:8, :], axis=-1, keepdims=True), (8, LANES))

    return pl.pallas_call(
        body,
        name="sum_small",
        in_specs=[VMEM_SPEC],
        out_specs=[VMEM_SPEC] * 3,
        out_shape=[jax.ShapeDtypeStruct((SMALL_ROWS, SMALL_COLS), F32), jax.ShapeDtypeStruct((8, LANES), F32), jax.ShapeDtypeStruct((8, LANES), F32)],
    )(gathered)


def _pad_rows(a, rows):
    return jnp.pad(a, ((0, rows - a.shape[0]), (0, 0)))


def _pad_to(a, rows, cols):
    return jnp.pad(a, ((0, rows - a.shape[0]), (0, cols - a.shape[1])))


def _conv_taps(conv_s):
    return jnp.transpose(conv_s[:, 0, 0:8], (1, 0, 2)).reshape(8, -1)


class _GradExchange:
    def __init__(self, chip_index, core_index, n_layers):
        self.chip_index, self.core_index, self.n_layers = chip_index, core_index, n_layers
        self.shards = {}
        self.pending = None

    def offer(self, layer, grads):
        assert self.pending is None
        names = list(grads)
        received = _swap_halves([grads[k] for k in names], f"{'_'.join(names)}_{layer}")
        parts = [_add_half(grads[k], r, self.core_index, f"add_half_{k}_{layer}") for k, r in zip(names, received)]
        self.pending = (layer, names, [p32 for p32, _ in parts], [p16 for _, p16 in parts])

    def payload(self):
        return () if self.pending is None else tuple(self.pending[3])

    def take(self, received):
        layer, names, parts, _ = self.pending
        self.pending = None
        for k, p, r in zip(names, parts, received):
            self.shards[k] = _add_chips(
                p, r, self.chip_index, self.core_index, layer, self.n_layers, self.shards.get(k), f"add_chips_{k}_{layer}")

    def finish(self):
        if self.pending is not None:
            layer, names = self.pending[0], self.pending[1]
            self.take(_scatter_to_chips(list(self.pending[3]), f"{'_'.join(names)}_{layer}"))
        return dict(zip(BIG, _join_halves([self.shards[k] for k in BIG])))


def _local_step(x, target, norm_mix, q_norm, k_norm, norm_ffn, layer_weights, exchange=None):
    layer_weights = list(layer_weights)

    def carrying(kernel_fn, n_out, *args):
        if exchange is None or exchange.pending is None:
            return kernel_fn(*args)
        out = kernel_fn(*args, scatter=exchange.payload())
        exchange.take(out[n_out:])
        return out[:n_out]

    n_layers = norm_mix.shape[0]
    s, d = x.shape
    tw = min(WGRAD_TILE, s)
    n_in = layer_weights[0][0].shape[-1]
    f = layer_weights[0][2].shape[-1]
    saved = []
    for l in range(n_layers):
        weights = list(layer_weights[l])
        q_gain = jnp.tile(q_norm[l][None, :], (1, 2))
        k_gain = jnp.tile(k_norm[l][None, :], (1, 2))
        h1, proj, qn, kn, vb = _in_proj(x, norm_mix[l][None, :], weights[0], 0, q_gain, k_gain, f"in_proj_{l}")
        missing = [n for n, w in enumerate(weights) if w.ndim == 3]
        if missing:
            attn, *arrived = _attn_fwd(qn, kn, vb, f"attn_fwd_{l}", gather=tuple(weights[n] for n in missing))
            for n, w in zip(missing, arrived):
                weights[n] = w
            layer_weights[l] = tuple(weights)
        else:
            attn = _attn_fwd(qn, kn, vb, f"attn_fwd_{l}")
        _, wout_s, wg_s, wu_s, wd_s, conv_s = weights
        taps = _conv_taps(conv_s)
        x_mid, conv = _out_proj(x, attn, proj, taps, wout_s, 0, f"out_proj_{l}")
        pending = ()
        if l + 1 < n_layers and all(w.ndim == 3 for w in layer_weights[l + 1]):
            pending = tuple(layer_weights[l + 1])
        x_out, gate, up, *arrived = _ffn_fwd(x_mid, norm_ffn[l][None, :], wg_s, wu_s, wd_s, 0, f"ffn_fwd_{l}", gather=pending)
        if pending:
            layer_weights[l + 1] = tuple(arrived)
        saved.append(dict(x=x, h1=h1, proj=proj, qn=qn, kn=kn, vb=vb, attn=attn, conv=conv, x_mid=x_mid, q_gain=q_gain, k_gain=k_gain,
                          gate=gate, up=up, taps=taps))
        x = x_out

    dy, loss_lanes = _loss_grad(x, target, "loss_grad")
    grads = [None] * n_layers
    for l in reversed(range(n_layers)):
        sv = saved[l]
        win_s, wout_s, wg_s, wu_s, wd_s, _ = layer_weights[l]
        dx_mid, d_norm_ffn, h2, dyb, dgate, dup, act, d_attn, d_conv, dxb = carrying(
            _ffn_bwd, 10, sv["x_mid"], dy, norm_ffn[l][None, :], sv["gate"], sv["up"], wg_s, wu_s, wd_s, wout_s, 0, f"ffn_bwd_{l}")
        tok2 = pl.BlockSpec((tw, d), lambda j, i: (i, 0))
        hid = pl.BlockSpec((1, tw, f), lambda j, i: (j, i, 0))
        d_wg = _wgrad(h2, dgate, tok2, hid, N_CHIPS, d, f, f"wgrad_gate_{l}")
        d_wu = _wgrad(h2, dup, tok2, hid, N_CHIPS, d, f, f"wgrad_up_{l}")
        d_wd = _wgrad(act, dyb, hid, tok2, N_CHIPS, f, d, f"wgrad_down_{l}")
        if exchange is not None:
            exchange.offer(l, dict(w_gate=d_wg, w_up=d_wu, w_down=d_wd))
        rows_out = wout_s.shape[2]
        mix_spec_a = pl.BlockSpec((tw, rows_out), lambda j, i: (i, j))
        d_wout_a = _wgrad(sv["attn"], dxb, mix_spec_a, tok2, ATTN_DIM // rows_out, rows_out, d, f"wgrad_out_attn_{l}")
        d_wout_c = _wgrad(sv["conv"], dxb, mix_spec_a, tok2, CONV_DIM // rows_out, rows_out, d, f"wgrad_out_conv_{l}")
        d_wout = jnp.concatenate([d_wout_a, d_wout_c], axis=0)
        dq, dk, dv = carrying(_attn_bwd, 3, sv["qn"], sv["kn"], sv["vb"], d_attn, f"attn_bwd_{l}")
        dproj, d_conv_w = _conv_bwd(sv["proj"], sv["taps"], d_conv, f"conv_bwd_{l}")
        dproj, d_qg, d_kg = _qkv_prep_bwd(sv["proj"], sv["q_gain"], sv["k_gain"], dq, dk, dv, dproj, f"qkv_prep_bwd_{l}")
        d_win = _wgrad(sv["h1"], dproj, tok2, pl.BlockSpec((tw, n_in), lambda j, i: (i, j)), N_CHIPS, d, n_in, f"wgrad_in_{l}")
        if exchange is not None:
            exchange.offer(l, dict(w_in=d_win, w_out=d_wout))
        dy, d_norm_mix = _in_proj_bwd(sv["x"], dx_mid, norm_mix[l][None, :], dproj, win_s, 0, f"in_proj_bwd_{l}")
        grads[l] = dict(norm_mix=d_norm_mix, norm_ffn=d_norm_ffn, q_norm=d_qg, k_norm=d_kg, conv_w=d_conv_w,
                        w_in=d_win, w_out=d_wout, w_gate=d_wg, w_up=d_wu, w_down=d_wd)
    return loss_lanes, dy, grads


BIG = ("w_in", "w_out", "w_gate", "w_up", "w_down")


def kernel(x, norm_mix, w_in, q_norm, k_norm, conv_w, w_out, norm_ffn, w_gate, w_up, w_down, loss_target, m_norm_mix, m_w_in, m_q_norm, m_k_norm, m_conv_w, m_w_out, m_norm_ffn, m_w_gate, m_w_up, m_w_down, v_norm_mix, v_w_in, v_q_norm, v_k_norm, v_conv_w, v_w_out, v_norm_ffn, v_w_gate, v_w_up, v_w_down):
    n_layers = norm_mix.shape[0]
    weights = dict(w_in=w_in, w_out=w_out, w_gate=w_gate, w_up=w_up, w_down=w_down)
    moments_m = dict(w_in=m_w_in, w_out=m_w_out, w_gate=m_w_gate, w_up=m_w_up, w_down=m_w_down)
    moments_v = dict(w_in=v_w_in, w_out=v_w_out, w_gate=v_w_gate, w_up=v_w_up, w_down=v_w_down)
    cx, cy, cc = _mesh_position()
    chip_index = (2 * cx + cy).astype(jnp.int32).reshape(1)
    core_index = cc.astype(jnp.int32).reshape(1)

    conv_pad = jnp.pad(conv_w, ((0, 0), (0, 16 - conv_w.shape[1]), (0, 0)))

    def shards_of(layer):
        return [weights[k][layer : layer + 1].astype(BF16) for k in BIG] + [conv_pad[layer : layer + 1]]

    first = shards_of(0)
    layer_weights = [tuple(_gather_weights(first[:1])) + tuple(first[1:])] + [tuple(shards_of(layer)) for layer in range(1, n_layers)]

    exchange = _GradExchange(chip_index, core_index, n_layers)
    loss_lanes, grad_x, grads = _local_step(
        x[0], loss_target[0], norm_mix, q_norm, k_norm, norm_ffn, layer_weights, exchange)

    big_grads = exchange.finish()

    def lanes(a):
        return _pad_to(a, a.shape[0], SMALL_COLS)

    def tile_of(*groups):
        return _pad_rows(jnp.concatenate([lanes(jnp.concatenate(g, axis=0)) for g in groups], axis=0), 8)

    layers = range(n_layers)
    pack = jnp.concatenate([
        tile_of([grads[l]["norm_mix"] for l in layers], [grads[l]["norm_ffn"] for l in layers], [loss_lanes]),
        tile_of([grads[l]["q_norm"] for l in layers], [grads[l]["k_norm"] for l in layers]),
        tile_of([grads[l]["conv_w"][0:3] for l in layers]),
    ], axis=0)
    small, small_heads, small_lanes = _sum_small(_gather_small(pack))
    loss = small_lanes[ROW_LOSS, 0]
    d_model = norm_mix.shape[1]
    conv_cols = conv_w.shape[2]
    conv_all = small[ROW_CONV : ROW_CONV + 3 * n_layers, 0:CONV_DIM].reshape(n_layers, 3, CONV_DIM)
    small_grads = dict(
        norm_mix=small[ROW_NORM_MIX : ROW_NORM_MIX + n_layers, 0:d_model],
        norm_ffn=small[ROW_NORM_FFN : ROW_NORM_FFN + n_layers, 0:d_model],
        q_norm=small_heads[ROW_Q_NORM - 8 : ROW_Q_NORM - 8 + n_layers, 0:HEAD_DIM],
        k_norm=small_heads[ROW_K_NORM - 8 : ROW_K_NORM - 8 + n_layers, 0:HEAD_DIM],
        conv_w=lax.dynamic_slice_in_dim(conv_all, (2 * cx + cy) * conv_cols, conv_cols, axis=2),
    )

    out_grad, out_delta, out_m, out_v = {}, {}, {}, {}
    for k in BIG:
        shape = weights[k].shape
        view = (shape[0] * shape[1], shape[2])
        g = big_grads[k]
        delta, new_m, new_v = _adamw(weights[k].reshape(view), g.reshape(view), moments_m[k].reshape(view), moments_v[k].reshape(view), f"adamw_{k}")
        out_grad[k], out_delta[k], out_m[k], out_v[k] = g, delta.reshape(shape), new_m.reshape(shape), new_v.reshape(shape)

    small_w = dict(norm_mix=norm_mix, norm_ffn=norm_ffn, q_norm=q_norm, k_norm=k_norm, conv_w=conv_w)
    small_m = dict(norm_mix=m_norm_mix, norm_ffn=m_norm_ffn, q_norm=m_q_norm, k_norm=m_k_norm, conv_w=m_conv_w)
    small_v = dict(norm_mix=v_norm_mix, norm_ffn=v_norm_ffn, q_norm=v_q_norm, k_norm=v_k_norm, conv_w=v_conv_w)
    order = ("norm_mix", "norm_ffn", "q_norm", "k_norm", "conv_w")

    def packed(tree):
        parts2 = [_pad_to(tree[k].reshape(-1, tree[k].shape[-1]), tree[k].reshape(-1, tree[k].shape[-1]).shape[0], SMALL_COLS) for k in order]
        return _pad_rows(jnp.concatenate(parts2, axis=0), SMALL_ROWS)

    delta_p, m_p, v_p = _adamw(packed(small_w), packed(small_grads), packed(small_m), packed(small_v), "adamw_small")
    row = 0
    for k in order:
        shape = small_w[k].shape
        n_rows = 1
        for dim in shape[:-1]:
            n_rows *= dim
        cut = (slice(row, row + n_rows), slice(0, shape[-1]))
        out_grad[k] = small_grads[k]
        out_delta[k], out_m[k], out_v[k] = delta_p[cut].reshape(shape), m_p[cut].reshape(shape), v_p[cut].reshape(shape)
        row += n_rows

    names_out = ("norm_mix", "w_in", "q_norm", "k_norm", "conv_w", "w_out", "norm_ffn", "w_gate", "w_up", "w_down")
    return (loss, grad_x[None], *[out_grad[k] for k in names_out], *[out_delta[k] for k in names_out],
            *[out_m[k] for k in names_out], *[out_v[k] for k in names_out])
```

```python
import functools

import jax
import jax.numpy as jnp
from jax import lax
from jax.experimental import pallas as pl
from jax.experimental.pallas import tpu as pltpu

F32 = jnp.float32
BF16 = jnp.bfloat16

EPS = 1e-6
HEAD_DIM = 64
LANES = 128
ATTN_DIM = 512
CONV_DIM = 512
N_CHIPS = 4
N_DEV = 8
Q_SCALE = HEAD_DIM ** -0.5
ATTN_Q_TILE = 256
ATTN_TILE = 256
TOKEN_TILE = 512
WGRAD_TILE = 4096
FFN_FWD_TILE = 1024
FFN_CHUNK = 256
VMEM_LIMIT = 56 * 1024 * 1024

ADAM_LR = 0.001
ADAM_B1 = 0.9
ADAM_B2 = 0.999
ADAM_EPS = 1e-08
ADAM_WD = 0.01
ADAM_STEP = 10

MESH_ID = pl.DeviceIdType.MESH
ANY = pl.BlockSpec(memory_space=pl.ANY)
VMEM_SPEC = pl.BlockSpec(memory_space=pltpu.VMEM)


def _params(n_axes):
    return pltpu.CompilerParams(dimension_semantics=("arbitrary",) * n_axes, vmem_limit_bytes=VMEM_LIMIT)


def _dot(a, b):
    return jnp.dot(a, b, preferred_element_type=F32)


def _dot_nt(a, b):
    return lax.dot_general(a, b, (((1,), (1,)), ((), ())), preferred_element_type=F32)


def _dot_tn(a, b):
    return lax.dot_general(a, b, (((0,), (0,)), ((), ())), preferred_element_type=F32)


SCORE_MAX = 80.0
UNDERFLOW_EXIT = 90.0


def _scores(q, k):
    return jnp.minimum(_dot_nt(q, k), SCORE_MAX)


def _softplus(z):
    return jnp.log(1.0 + jnp.exp(z))


def _head_norm(xv, gain, low):
    sq = xv * xv
    s_low = jnp.sum(jnp.where(low, sq, 0.0), axis=-1, keepdims=True)
    s_high = jnp.sum(jnp.where(low, 0.0, sq), axis=-1, keepdims=True)
    r = jnp.where(low, lax.rsqrt(s_low / HEAD_DIM + EPS), lax.rsqrt(s_high / HEAD_DIM + EPS))
    return xv * r * gain, r


def _in_proj(x, gain, w_s, layer, q_gain, k_gain, name):
    s, d = x.shape
    n_blocks, _, _, n = w_s.shape
    tm = TOKEN_TILE

    def body(x_ref, g_ref, w_ref, qg_ref, kg_ref, h_ref, o_ref, q_ref, k_ref, v_ref):
        xv = x_ref[...]
        r = lax.rsqrt(jnp.mean(xv * xv, axis=-1, keepdims=True) + EPS)
        h = (xv * r * g_ref[...]).astype(BF16)
        h_ref[...] = h
        for j in range(n_blocks):
            o_ref[:, j * n : (j + 1) * n] = _dot(h, w_ref[j, 0])
        low = lax.broadcasted_iota(jnp.int32, (tm, LANES), 1) < HEAD_DIM
        for g in range(ATTN_DIM // LANES):
            cq = slice(LANES * g, LANES * (g + 1))
            ck = slice(ATTN_DIM + LANES * g, ATTN_DIM + LANES * (g + 1))
            cv = slice(2 * ATTN_DIM + LANES * g, 2 * ATTN_DIM + LANES * (g + 1))
            qn, _ = _head_norm(o_ref[:, cq], qg_ref[...], low)
            kn, _ = _head_norm(o_ref[:, ck], kg_ref[...], low)
            q_ref[:, cq] = (qn * Q_SCALE).astype(BF16)
            k_ref[:, cq] = kn.astype(BF16)
            v_ref[:, cq] = o_ref[:, cv].astype(BF16)

    head_spec = pl.BlockSpec((tm, ATTN_DIM), lambda i: (i, 0))
    head_shape = jax.ShapeDtypeStruct((s, ATTN_DIM), BF16)
    gain_spec = pl.BlockSpec((1, LANES), lambda i: (0, 0))
    return pl.pallas_call(
        body,
        name=name,
        grid=(s // tm,),
        in_specs=[
            pl.BlockSpec((tm, d), lambda i: (i, 0)),
            pl.BlockSpec((1, d), lambda i: (0, 0)),
            pl.BlockSpec((n_blocks, 1, d, n), lambda i: (0, layer, 0, 0)),
            gain_spec, gain_spec,
        ],
        out_specs=[pl.BlockSpec((tm, d), lambda i: (i, 0)), pl.BlockSpec((tm, n_blocks * n), lambda i: (i, 0)), head_spec, head_spec, head_spec],
        out_shape=[jax.ShapeDtypeStruct((s, d), BF16), jax.ShapeDtypeStruct((s, n_blocks * n), F32), head_shape, head_shape, head_shape],
        compiler_params=_params(1),
    )(x, gain, w_s, q_gain, k_gain)


def _qkv_prep_bwd(proj, q_gain, k_gain, dq, dk, dv, dproj, name):
    s = proj.shape[0]
    tm = TOKEN_TILE

    def norm_bwd(xv, gain, dy, low):
        _, r = _head_norm(xv, gain, low)
        xhat = xv * r
        dxhat = dy * gain
        prod = dxhat * xhat
        m_low = jnp.sum(jnp.where(low, prod, 0.0), axis=-1, keepdims=True)
        m_high = jnp.sum(jnp.where(low, 0.0, prod), axis=-1, keepdims=True)
        mean = jnp.where(low, m_low, m_high) / HEAD_DIM
        return r * (dxhat - xhat * mean), jnp.sum(dy * xhat, axis=0, keepdims=True)

    def body(p_ref, qg_ref, kg_ref, dq_ref, dk_ref, dv_ref, dproj_ref, dp_ref, dqg_ref, dkg_ref):
        @pl.when(pl.program_id(0) == 0)
        def _():
            dqg_ref[...] = jnp.zeros_like(dqg_ref)
            dkg_ref[...] = jnp.zeros_like(dkg_ref)

        low = lax.broadcasted_iota(jnp.int32, (tm, LANES), 1) < HEAD_DIM
        for g in range(ATTN_DIM // LANES):
            cq = slice(LANES * g, LANES * (g + 1))
            ck = slice(ATTN_DIM + LANES * g, ATTN_DIM + LANES * (g + 1))
            cv = slice(2 * ATTN_DIM + LANES * g, 2 * ATTN_DIM + LANES * (g + 1))
            dxq, dgq = norm_bwd(p_ref[:, cq], qg_ref[...], dq_ref[:, cq] * Q_SCALE, low)
            dxk, dgk = norm_bwd(p_ref[:, ck], kg_ref[...], dk_ref[:, cq], low)
            dp_ref[:, cq] = dxq.astype(BF16)
            dp_ref[:, ck] = dxk.astype(BF16)
            dp_ref[:, cv] = dv_ref[:, cq].astype(BF16)
            dqg_ref[:, cq] += dgq
            dkg_ref[:, cq] += dgk

    grad_spec = pl.BlockSpec((tm, ATTN_DIM), lambda i: (i, 0))
    gain_spec = pl.BlockSpec((1, LANES), lambda i: (0, 0))
    sum_spec = pl.BlockSpec((1, ATTN_DIM), lambda i: (0, 0))
    return pl.pallas_call(
        body,
        name=name,
        grid=(s // tm,),
        in_specs=[pl.BlockSpec((tm, 3 * ATTN_DIM), lambda i: (i, 0)), gain_spec, gain_spec, grad_spec, grad_spec, grad_spec, ANY],
        out_specs=[pl.BlockSpec((tm, 3 * ATTN_DIM), lambda i: (i, 0)), sum_spec, sum_spec],
        out_shape=[
            jax.ShapeDtypeStruct(dproj.shape, BF16),
            jax.ShapeDtypeStruct((1, ATTN_DIM), F32),
            jax.ShapeDtypeStruct((1, ATTN_DIM), F32),
        ],
        input_output_aliases={6: 0},
        compiler_params=_params(1),
    )(proj, q_gain, k_gain, dq, dk, dv, dproj)


def _attn_tile_consts(t):
    row = lax.broadcasted_iota(jnp.int32, (t, t), 0)
    col = lax.broadcasted_iota(jnp.int32, (t, t), 1)
    return row, col


def _triangle_sum(v, triangle):
    return _dot(v.astype(BF16), triangle)


def _attn_fwd(qn, kn, vb, name, gather=()):
    s = qn.shape[0]
    t = min(ATTN_TILE, s)
    tq = min(ATTN_Q_TILE, t)
    per_key_tile = t // tq
    n_gather = len(gather)
    n_pairs, n_blocks = ATTN_DIM // LANES, s // tq

    def body(*refs):
        q_ref, k_ref, v_ref = refs[:3]
        o_ref = refs[3 + n_gather]
        if n_gather:
            copies = _WeightGather(refs[3 : 3 + n_gather], refs[4 + n_gather : 4 + 2 * n_gather], refs[4 + 2 * n_gather :])
            first = (pl.program_id(0) == 0) & (pl.program_id(1) == 0)
            pl.when(first)(copies.begin)
            pl.when((pl.program_id(0) == n_pairs - 1) & (pl.program_id(1) == 0))(copies.relay)
        i = pl.program_id(1) // per_key_tile
        low = lax.broadcasted_iota(jnp.int32, (tq, LANES), 1) < HEAD_DIM
        row, col = _attn_tile_consts(t)
        suffix = (row > col).astype(BF16)
        first_row = (pl.program_id(1) % per_key_tile) * tq
        causal = lax.broadcasted_iota(jnp.int32, (tq, t), 1) < lax.broadcasted_iota(jnp.int32, (tq, t), 0) + first_row
        q = q_ref[...]
        zero_q = jnp.zeros_like(q)
        qh = (jnp.where(low, q, zero_q), jnp.where(low, zero_q, q))

        def step(kbs, carry, diagonal_first=False):
            chains = [(head, m) for head in range(2) for m in range(len(kbs))]
            masked = [diagonal_first and m == 0 for _, m in chains]
            ks = [k_ref[pl.ds(pl.multiple_of(kb * t, t), t), :] for kb in kbs]
            vs = [v_ref[pl.ds(pl.multiple_of(kb * t, t), t), :] for kb in kbs]
            z = [_scores(qh[head], ks[kb]) for head, kb in chains]
            sp = [_softplus(zc) for zc in z]
            sp = [jnp.where(causal, s_, 0.0) if mk else s_ for s_, mk in zip(sp, masked)]
            inside = [_triangle_sum(s_, suffix) for s_ in sp]
            after = [carry[head][1] for head in range(2)]
            log_a = []
            for n, (head, kb) in enumerate(chains):
                log_a.append(z[n] - sp[n] - inside[n] - after[head])
                after[head] = after[head] + jnp.sum(sp[n], axis=-1, keepdims=True)
            a = [jnp.exp(l_) for l_ in log_a]
            a = [jnp.where(causal, a_, 0.0) if mk else a_ for a_, mk in zip(a, masked)]
            acc = [carry[head][0] for head in range(2)]
            for n, (head, kb) in enumerate(chains):
                acc[head] = acc[head] + _dot(a[n].astype(BF16), vs[kb])
            return tuple((acc[head], after[head]) for head in range(2))

        def live(c):
            return jnp.minimum(jnp.min(c[0][1]), jnp.min(c[1][1])) < UNDERFLOW_EXIT

        zero = (jnp.zeros((tq, LANES), F32), jnp.zeros((tq, 1), F32))
        start = lax.cond(i >= 1, lambda c: step((i, i - 1), c, True), lambda c: step((i,), c, True), (zero, zero))
        o_ref[...] = jnp.where(low, start[0][0], start[1][0]).astype(BF16)
        rest = jnp.maximum(i - 1, 0)

        @pl.when((rest > 0) & live(start))
        def _():
            carry = lax.cond(rest % 2 == 1, lambda c: step((i - 2,), c), lambda c: c, start)
            pairs = rest // 2
            _, carry = lax.while_loop(
                lambda st: (st[0] < pairs) & live(st[1]),
                lambda st: (st[0] + 1, step((2 * (pairs - st[0]) - 1, 2 * (pairs - st[0]) - 2), st[1])),
                (jnp.int32(0), carry))
            o_ref[...] = jnp.where(low, carry[0][0], carry[1][0]).astype(BF16)

        if n_gather:
            pl.when((pl.program_id(0) == n_pairs - 1) & (pl.program_id(1) == n_blocks - 1))(copies.finish)

    out = pl.pallas_call(
        body,
        name=name,
        grid=(n_pairs, n_blocks),
        in_specs=[
            pl.BlockSpec((tq, LANES), lambda p, i: (i, p)),
            pl.BlockSpec((s, LANES), lambda p, i: (0, p)),
            pl.BlockSpec((s, LANES), lambda p, i: (0, p)),
        ] + [ANY] * n_gather,
        out_specs=[pl.BlockSpec((tq, LANES), lambda p, i: (i, p))] + [ANY] * n_gather,
        out_shape=[jax.ShapeDtypeStruct((s, ATTN_DIM), BF16)] + [jax.ShapeDtypeStruct((N_CHIPS,) + w.shape, w.dtype) for w in gather],
        scratch_shapes=_gather_scratch(n_gather) if n_gather else [],
        compiler_params=_params(2),
    )(qn, kn, vb, *gather)
    return out if n_gather else out[0]


def _attn_bwd(qn, kn, vb, do, name, scatter=()):
    s = qn.shape[0]
    t = min(ATTN_TILE, s)
    nq = s // t
    n_scatter = len(scatter)
    n_pairs = ATTN_DIM // LANES

    def body(*refs):
        q_ref, k_ref, v_ref, do_ref = refs[:4]
        dq_ref, dk_ref, dv_ref = refs[4 + n_scatter : 7 + n_scatter]
        a_s, sg_s, a_f, sg_f = refs[7 + 2 * n_scatter : 11 + 2 * n_scatter]
        i = pl.program_id(1)
        if n_scatter:
            copies = _ChipScatter(refs[4 : 4 + n_scatter], refs[7 + n_scatter : 7 + 2 * n_scatter], refs[11 + 2 * n_scatter :])
            pl.when((pl.program_id(0) == 0) & (i == 0))(copies.begin)

        @pl.when(i == 0)
        def _():
            dk_ref[...] = jnp.zeros_like(dk_ref)
            dv_ref[...] = jnp.zeros_like(dv_ref)

        low = lax.broadcasted_iota(jnp.int32, (t, LANES), 1) < HEAD_DIM
        row, col = _attn_tile_consts(t)
        suffix = (row > col).astype(BF16)
        prefix = (row < col).astype(BF16)
        causal = col < row
        q = q_ref[...]
        dob = do_ref[...]
        zero_q = jnp.zeros_like(q)
        qhs = (jnp.where(low, q, zero_q), jnp.where(low, zero_q, q))
        dohs = (jnp.where(low, dob, zero_q), jnp.where(low, zero_q, dob))

        def rows_of(kb):
            return pl.ds(pl.multiple_of(kb * t, t), t)

        pair = [(head, m) for head in range(2) for m in range(2)]

        def short_pass1():
            z = [_scores(qhs[head], k_ref[rows_of(i - m), :]) for head, m in pair]
            sp = [_softplus(z_) for z_ in z]
            sp = [jnp.where(causal, s_, 0.0) if m == 0 else s_ for s_, (_, m) in zip(sp, pair)]
            inside = [_triangle_sum(s_, suffix) for s_ in sp]
            after = [jnp.zeros((t, 1), F32), jnp.zeros((t, 1), F32)]
            for n, (head, m) in enumerate(pair):
                log_sg = z[n] - sp[n]
                a = jnp.exp(log_sg - inside[n] - after[head])
                sg = jnp.exp(log_sg)
                if m == 0:
                    a = jnp.where(causal, a, 0.0)
                    sg = jnp.where(causal, sg, 0.0)
                a_f[n] = a
                sg_f[n] = sg
                after[head] = after[head] + jnp.sum(sp[n], axis=-1, keepdims=True)
            return jnp.minimum(jnp.min(after[0]), jnp.min(after[1])) >= UNDERFLOW_EXIT

        def short_pass2():
            order = [(head, m) for head in range(2) for m in (1, 0)]
            a = {c: a_f[pair.index(c)] for c in order}
            g = {c: a[c] * _dot_nt(dohs[c[0]], v_ref[rows_of(i - c[1]), :]) for c in order}
            for m in (1, 0):
                dv_ref[rows_of(i - m), :] += _dot_tn(a[(0, m)].astype(BF16), dohs[0]) + _dot_tn(a[(1, m)].astype(BF16), dohs[1])
            inside = {c: _triangle_sum(g[c], prefix) for c in order}
            before = [jnp.zeros((t, 1), F32), jnp.zeros((t, 1), F32)]
            dz = {}
            for c in order:
                sg = sg_f[pair.index(c)]
                dz[c] = (g[c] - sg * (g[c] + inside[c] + before[c[0]])).astype(BF16)
                before[c[0]] = before[c[0]] + jnp.sum(g[c], axis=-1, keepdims=True)
            for m in (1, 0):
                dk_ref[rows_of(i - m), :] += _dot_tn(dz[(0, m)], qhs[0]) + _dot_tn(dz[(1, m)], qhs[1])
            dq = [_dot(dz[(head, 1)], k_ref[rows_of(i - 1), :]) + _dot(dz[(head, 0)], k_ref[rows_of(i), :]) for head in range(2)]
            dq_ref[...] = jnp.where(low, dq[0], dq[1])

        def general_walk():
            heads = []
            for head in range(2):
                qh, doh = qhs[head], dohs[head]

                def pass1(kbs, after, diagonal_first=False):
                    z = [_scores(qh, k_ref[rows_of(kb), :]) for kb in kbs]
                    sp = [_softplus(z_) for z_ in z]
                    if diagonal_first:
                        sp[0] = jnp.where(causal, sp[0], 0.0)
                    inside = [_triangle_sum(s_, suffix) for s_ in sp]
                    for n, kb in enumerate(kbs):
                        log_sg = z[n] - sp[n]
                        a = jnp.exp(log_sg - inside[n] - after)
                        sg = jnp.exp(log_sg)
                        if diagonal_first and n == 0:
                            a = jnp.where(causal, a, 0.0)
                            sg = jnp.where(causal, sg, 0.0)
                        a_s[kb] = a
                        sg_s[kb] = sg
                        after = after + jnp.sum(sp[n], axis=-1, keepdims=True)
                    return after

                def live(after):
                    return jnp.min(after) < UNDERFLOW_EXIT

                after = jnp.zeros((t, 1), F32)
                after = lax.cond(i >= 1, lambda c: pass1((i, i - 1), c, True), lambda c: pass1((i,), c, True), after)
                rest = jnp.maximum(i - 1, 0)
                take_single = (rest % 2 == 1) & live(after)
                after = lax.cond(take_single, lambda c: pass1((i - 2,), c), lambda c: c, after)
                pairs = rest // 2
                pairs_done, _ = lax.while_loop(
                    lambda st: (st[0] < pairs) & live(st[1]),
                    lambda st: (st[0] + 1, pass1((2 * (pairs - st[0]) - 1, 2 * (pairs - st[0]) - 2), st[1])),
                    (jnp.int32(0), after))
                walked = jnp.minimum(i, 1) + 1 + take_single.astype(jnp.int32) + 2 * pairs_done
                first = i - walked + 1

                def pass2(kbs, carry):
                    dq, before = carry
                    ks = [k_ref[rows_of(kb), :] for kb in kbs]
                    a = [a_s[kb] for kb in kbs]
                    g = [a_ * _dot_nt(doh, v_ref[rows_of(kb), :]) for a_, kb in zip(a, kbs)]
                    for n, kb in enumerate(kbs):
                        dv_ref[rows_of(kb), :] += _dot_tn(a[n].astype(BF16), doh)
                    inside = [_triangle_sum(g_, prefix) for g_ in g]
                    dz = []
                    for n, kb in enumerate(kbs):
                        sg = sg_s[kb]
                        dz.append((g[n] - sg * (g[n] + inside[n] + before)).astype(BF16))
                        before = before + jnp.sum(g[n], axis=-1, keepdims=True)
                    for n, kb in enumerate(kbs):
                        dk_ref[rows_of(kb), :] += _dot_tn(dz[n], qh)
                    for n in range(len(kbs)):
                        dq = dq + _dot(dz[n], ks[n])
                    return dq, before

                carry = (jnp.zeros((t, LANES), F32), jnp.zeros((t, 1), F32))
                carry = lax.fori_loop(0, walked // 2, lambda n, c: pass2((first + 2 * n, first + 2 * n + 1), c), carry)
                carry = lax.cond(walked % 2 == 1, lambda c: pass2((i,), c), lambda c: c, carry)
                heads.append(carry[0])
            dq_ref[...] = jnp.where(low, heads[0], heads[1])

        short = lax.cond(i >= 1, short_pass1, lambda: jnp.bool_(False))
        pl.when(short)(short_pass2)
        pl.when(jnp.logical_not(short))(general_walk)
        if n_scatter:
            pl.when((pl.program_id(0) == n_pairs - 1) & (i == nq - 1))(copies.finish)

    q_spec = pl.BlockSpec((t, LANES), lambda p, i: (i, p))
    kv_spec = pl.BlockSpec((s, LANES), lambda p, i: (0, p))
    return pl.pallas_call(
        body,
        name=name,
        grid=(n_pairs, nq),
        in_specs=[q_spec, kv_spec, kv_spec, q_spec] + [ANY] * n_scatter,
        out_specs=[q_spec, kv_spec, kv_spec] + [ANY] * n_scatter,
        out_shape=[jax.ShapeDtypeStruct((s, ATTN_DIM), F32)] * 3 + _scatter_shapes(scatter),
        scratch_shapes=[pltpu.VMEM((nq, t, t), F32), pltpu.VMEM((nq, t, t), F32), pltpu.VMEM((4, t, t), F32), pltpu.VMEM((4, t, t), F32)]
        + (_scatter_scratch(n_scatter) if n_scatter else []),
        compiler_params=_params(2),
    )(qn, kn, vb, do, *scatter)


CB_BLOCK, CC_BLOCK, CU_BLOCK = 3, 4, 5


def _shift_down(h, prev_rows, n):
    row = lax.broadcasted_iota(jnp.int32, h.shape, 0)
    out = pltpu.roll(h, n, 0)
    for r in range(n):
        out = jnp.where(row == r, prev_rows[len(prev_rows) - n + r], out)
    return out


def _shift_up(h, next_rows, n):
    tm = h.shape[0]
    row = lax.broadcasted_iota(jnp.int32, h.shape, 0)
    out = pltpu.roll(h, tm - n, 0)
    for r in range(n):
        out = jnp.where(row == tm - n + r, next_rows[r], out)
    return out


def _conv_bwd(proj, conv_w, dconv, name):
    s = proj.shape[0]
    tm = TOKEN_TILE
    nb = tm // 8
    n_tiles = s // tm

    def body(cb_ref, cc_ref, cu_ref, dy_ref, pc_ref, pu_ref, nb_ref, ndy_ref, w_ref, dp_ref, dw_ref):
        i = pl.program_id(0)

        @pl.when(i == 0)
        def _():
            dw_ref[...] = jnp.zeros_like(dw_ref)

        first = i == 0
        last = i == n_tiles - 1
        cc, cu, cb, dy = cc_ref[...], cu_ref[...], cb_ref[...], dy_ref[...]
        h = cc * cu
        prev = [jnp.where(first, 0.0, pc_ref[r : r + 1, :] * pu_ref[r : r + 1, :]) for r in (6, 7)]
        h1 = _shift_down(h, prev, 1)
        h2 = _shift_down(h, prev, 2)
        y = w_ref[0:1, :] * h2 + w_ref[1:2, :] * h1 + w_ref[2:3, :] * h
        dyb = dy * cb
        nxt = [jnp.where(last, 0.0, ndy_ref[r : r + 1, :] * nb_ref[r : r + 1, :]) for r in (0, 1)]
        dh = w_ref[2:3, :] * dyb + w_ref[1:2, :] * _shift_up(dyb, nxt, 1) + w_ref[0:1, :] * _shift_up(dyb, nxt, 2)
        dp_ref[:, 0:CONV_DIM] = (dy * y).astype(BF16)
        dp_ref[:, CONV_DIM : 2 * CONV_DIM] = (dh * cu).astype(BF16)
        dp_ref[:, 2 * CONV_DIM : 3 * CONV_DIM] = (dh * cc).astype(BF16)
        dw_ref[0:1, :] += jnp.sum(dyb * h2, axis=0, keepdims=True)
        dw_ref[1:2, :] += jnp.sum(dyb * h1, axis=0, keepdims=True)
        dw_ref[2:3, :] += jnp.sum(dyb * h, axis=0, keepdims=True)

    def col(block):
        return pl.BlockSpec((tm, CONV_DIM), lambda i: (i, block))

    def halo_prev(block):
        return pl.BlockSpec((8, CONV_DIM), lambda i: (jnp.maximum(i * nb - 1, 0), block))

    def halo_next(block):
        return pl.BlockSpec((8, CONV_DIM), lambda i: (jnp.minimum((i + 1) * nb, s // 8 - 1), block))

    return pl.pallas_call(
        body,
        name=name,
        grid=(n_tiles,),
        in_specs=[
            col(CB_BLOCK), col(CC_BLOCK), col(CU_BLOCK), col(0),
            halo_prev(CC_BLOCK), halo_prev(CU_BLOCK), halo_next(CB_BLOCK), halo_next(0),
            pl.BlockSpec((8, CONV_DIM), lambda i: (0, 0)),
        ],
        out_specs=[pl.BlockSpec((tm, 3 * CONV_DIM), lambda i: (i, 1)), pl.BlockSpec((8, CONV_DIM), lambda i: (0, 0))],
        out_shape=[jax.ShapeDtypeStruct((s, 3 * ATTN_DIM + 3 * CONV_DIM), BF16), jax.ShapeDtypeStruct((8, CONV_DIM), F32)],
        compiler_params=_params(1),
    )(proj, proj, proj, dconv, proj, proj, proj, dconv, conv_w)


def _out_proj(x, attn, proj, conv_w, w_s, layer, name):
    s, d = x.shape
    tm = TOKEN_TILE
    nb = tm // 8
    rows = w_s.shape[2]

    def body(x_ref, a_ref, cb_ref, cc_ref, cu_ref, pc_ref, pu_ref, cw_ref, w_ref, o_ref, c_ref):
        first = pl.program_id(0) == 0
        h = cc_ref[...] * cu_ref[...]
        prev = [jnp.where(first, 0.0, pc_ref[r : r + 1, :] * pu_ref[r : r + 1, :]) for r in (6, 7)]
        y = cw_ref[0:1, :] * _shift_down(h, prev, 2) + cw_ref[1:2, :] * _shift_down(h, prev, 1) + cw_ref[2:3, :] * h
        c_ref[...] = (cb_ref[...] * y).astype(BF16)
        acc = x_ref[...]
        for j in range(N_CHIPS):
            src = a_ref if j < 2 else c_ref
            cols = slice((j % 2) * rows, (j % 2 + 1) * rows)
            acc = acc + _dot(src[:, cols], w_ref[j, 0])
        o_ref[...] = acc

    def col(block):
        return pl.BlockSpec((tm, CONV_DIM), lambda i: (i, block))

    def halo(block):
        return pl.BlockSpec((8, CONV_DIM), lambda i: (jnp.maximum(i * nb - 1, 0), block))

    return pl.pallas_call(
        body,
        name=name,
        grid=(s // tm,),
        in_specs=[
            pl.BlockSpec((tm, d), lambda i: (i, 0)),
            pl.BlockSpec((tm, ATTN_DIM), lambda i: (i, 0)),
            col(CB_BLOCK), col(CC_BLOCK), col(CU_BLOCK), halo(CC_BLOCK), halo(CU_BLOCK),
            pl.BlockSpec((8, CONV_DIM), lambda i: (0, 0)),
            pl.BlockSpec((N_CHIPS, 1, rows, d), lambda i: (0, layer, 0, 0)),
        ],
        out_specs=[pl.BlockSpec((tm, d), lambda i: (i, 0)), pl.BlockSpec((tm, CONV_DIM), lambda i: (i, 0))],
        out_shape=[jax.ShapeDtypeStruct((s, d), F32), jax.ShapeDtypeStruct((s, CONV_DIM), BF16)],
        compiler_params=_params(1),
    )(x, attn, proj, proj, proj, proj, proj, conv_w, w_s)


def _out_proj_bwd(dx, w_s, layer, name):
    s, d = dx.shape
    tm = TOKEN_TILE
    rows = w_s.shape[2]

    def body(dx_ref, w_ref, da_ref, dc_ref, dxb_ref):
        dxb = dx_ref[...].astype(BF16)
        dxb_ref[...] = dxb
        for j in range(N_CHIPS):
            cols = slice((j % 2) * rows, (j % 2 + 1) * rows)
            part = _dot_nt(dxb, w_ref[j, 0])
            if j < 2:
                da_ref[:, cols] = part.astype(BF16)
            else:
                dc_ref[:, cols] = part

    return pl.pallas_call(
        body,
        name=name,
        grid=(s // tm,),
        in_specs=[pl.BlockSpec((tm, d), lambda i: (i, 0)), pl.BlockSpec((N_CHIPS, 1, rows, d), lambda i: (0, layer, 0, 0))],
        out_specs=[
            pl.BlockSpec((tm, ATTN_DIM), lambda i: (i, 0)),
            pl.BlockSpec((tm, CONV_DIM), lambda i: (i, 0)),
            pl.BlockSpec((tm, d), lambda i: (i, 0)),
        ],
        out_shape=[
            jax.ShapeDtypeStruct((s, ATTN_DIM), BF16),
            jax.ShapeDtypeStruct((s, CONV_DIM), F32),
            jax.ShapeDtypeStruct((s, d), BF16),
        ],
        compiler_params=_params(1),
    )(dx, w_s)


def _ffn_fwd(x, gain, wg_s, wu_s, wd_s, layer, name, gather=()):
    s, d = x.shape
    tm = min(FFN_FWD_TILE, s)
    f = wg_s.shape[3]
    n_gather = len(gather)
    n_tiles = s // tm

    def body(*refs):
        x_ref, g_ref, wg_ref, wu_ref, wd_ref = refs[:5]
        o_ref, gate_ref, up_ref = refs[5 + n_gather : 8 + n_gather]
        h_s = refs[8 + 2 * n_gather]
        i, j = pl.program_id(0), pl.program_id(1)
        if n_gather:
            copies = _WeightGather(refs[5 : 5 + n_gather], refs[8 + n_gather : 8 + 2 * n_gather], refs[9 + 2 * n_gather :])
            pl.when((i == 0) & (j == 0))(copies.begin)
            pl.when((i == (3 * n_tiles) // 4) & (j == 0))(copies.relay)

        @pl.when(j == 0)
        def _():
            xv = x_ref[...]
            r = lax.rsqrt(jnp.mean(xv * xv, axis=-1, keepdims=True) + EPS)
            h_s[...] = (xv * r * g_ref[...]).astype(BF16)
            o_ref[...] = xv

        halves = [slice(r, r + FFN_CHUNK) for r in range(0, tm, FFN_CHUNK)]
        pre = [(_dot(h_s[r, :], wg_ref[0, 0]), _dot(h_s[r, :], wu_ref[0, 0])) for r in halves]
        act = [((gate / (1.0 + jnp.exp(-gate))) * up).astype(BF16) for gate, up in pre]
        for r, (gate, up) in zip(halves, pre):
            gate_ref[0, r, :] = gate.astype(BF16)
            up_ref[0, r, :] = up.astype(BF16)
        for r, a in zip(halves, act):
            o_ref[r, :] += _dot(a, wd_ref[0, 0])

        if n_gather:
            pl.when((i == n_tiles - 1) & (j == N_CHIPS - 1))(copies.finish)

    hid = pl.BlockSpec((1, tm, f), lambda i, j: (j, i, 0))
    hid_shape = jax.ShapeDtypeStruct((N_CHIPS, s, f), BF16)
    return pl.pallas_call(
        body,
        name=name,
        grid=(n_tiles, N_CHIPS),
        in_specs=[
            pl.BlockSpec((tm, d), lambda i, j: (i, 0)),
            pl.BlockSpec((1, d), lambda i, j: (0, 0)),
            pl.BlockSpec((1, 1, d, f), lambda i, j: (j, layer, 0, 0)),
            pl.BlockSpec((1, 1, d, f), lambda i, j: (j, layer, 0, 0)),
            pl.BlockSpec((1, 1, f, d), lambda i, j: (j, layer, 0, 0)),
        ] + [ANY] * n_gather,
        out_specs=[pl.BlockSpec((tm, d), lambda i, j: (i, 0)), hid, hid] + [ANY] * n_gather,
        out_shape=[jax.ShapeDtypeStruct((s, d), F32), hid_shape, hid_shape]
        + [jax.ShapeDtypeStruct((N_CHIPS,) + w.shape, w.dtype) for w in gather],
        scratch_shapes=[pltpu.VMEM((tm, d), BF16)] + (_gather_scratch(n_gather) if n_gather else []),
        compiler_params=_params(2),
    )(x, gain, wg_s, wu_s, wd_s, *gather)


def _resident(block, layer):
    return pl.BlockSpec(block, lambda i, j: (0, layer, 0, 0), pipeline_mode=pl.Buffered(1))


def _rms_bwd(xv, gain, dh):
    r = lax.rsqrt(jnp.mean(xv * xv, axis=-1, keepdims=True) + EPS)
    xhat = xv * r
    dxhat = dh * gain
    dx = r * (dxhat - xhat * jnp.mean(dxhat * xhat, axis=-1, keepdims=True))
    return dx, jnp.sum(dh * xhat, axis=0, keepdims=True)


def _ffn_bwd(x, dy, gain, gate_s, up_s, wg_s, wu_s, wd_s, layer, name, scatter=()):
    s, d = x.shape
    tm = TOKEN_TILE
    f = wg_s.shape[3]

    n_scatter = len(scatter)
    n_tiles = s // tm

    def body(*refs):
        x_ref, dy_ref, g_ref, gate_ref, up_ref, wg_ref, wu_ref, wd_ref = refs[:8]
        dx_ref, dgain_ref, h_ref, dyb_ref, dg_ref, du_ref, act_ref = refs[8 + n_scatter : 15 + n_scatter]
        acc_s = refs[15 + 2 * n_scatter]
        i, j = pl.program_id(0), pl.program_id(1)
        if n_scatter:
            copies = _ChipScatter(refs[8 : 8 + n_scatter], refs[15 + n_scatter : 15 + 2 * n_scatter], refs[16 + 2 * n_scatter :])
            pl.when((i == 0) & (j == 0))(copies.begin)

        @pl.when((i == 0) & (j == 0))
        def _():
            dgain_ref[...] = jnp.zeros_like(dgain_ref)

        @pl.when(j == 0)
        def _():
            xv = x_ref[...]
            r = lax.rsqrt(jnp.mean(xv * xv, axis=-1, keepdims=True) + EPS)
            h_ref[...] = (xv * r * g_ref[...]).astype(BF16)
            dyb_ref[...] = dy_ref[...].astype(BF16)
            acc_s[...] = jnp.zeros_like(acc_s)

        halves = [slice(0, tm // 2), slice(tm // 2, tm)]
        pre = [(gate_ref[0, r, :].astype(F32), up_ref[0, r, :].astype(F32), _dot_nt(dyb_ref[r, :], wd_ref[j, 0])) for r in halves]
        grads = []
        for r, (gate, up, dact) in zip(halves, pre):
            sig = 1.0 / (1.0 + jnp.exp(-gate))
            silu = gate * sig
            dgate = (dact * up * (sig * (1.0 + gate * (1.0 - sig)))).astype(BF16)
            dup = (dact * silu).astype(BF16)
            act_ref[0, r, :] = (silu * up).astype(BF16)
            dg_ref[0, r, :] = dgate
            du_ref[0, r, :] = dup
            grads.append((dgate, dup))
        for r, (dgate, dup) in zip(halves, grads):
            acc_s[r, :] += _dot_nt(dgate, wg_ref[j, 0]) + _dot_nt(dup, wu_ref[j, 0])

        @pl.when(j == N_CHIPS - 1)
        def _():
            dxn, dgain = _rms_bwd(x_ref[...], g_ref[...], acc_s[...])
            dx_ref[...] = dy_ref[...] + dxn
            dgain_ref[...] += dgain

        if n_scatter:
            pl.when((i == n_tiles - 1) & (j == N_CHIPS - 1))(copies.finish)

    tok = pl.BlockSpec((tm, d), lambda i, j: (i, 0))
    vec = pl.BlockSpec((1, d), lambda i, j: (0, 0))
    hid = pl.BlockSpec((1, tm, f), lambda i, j: (j, i, 0))
    hid_shape = jax.ShapeDtypeStruct((N_CHIPS, s, f), BF16)
    return pl.pallas_call(
        body,
        name=name,
        grid=(n_tiles, N_CHIPS),
        in_specs=[
            tok, tok, vec, hid, hid,
            _resident((N_CHIPS, 1, d, f), layer),
            _resident((N_CHIPS, 1, d, f), layer),
            _resident((N_CHIPS, 1, f, d), layer),
        ] + [ANY] * n_scatter,
        out_specs=[tok, vec, tok, tok, hid, hid, hid] + [ANY] * n_scatter,
        out_shape=[
            jax.ShapeDtypeStruct((s, d), F32),
            jax.ShapeDtypeStruct((1, d), F32),
            jax.ShapeDtypeStruct((s, d), BF16),
            jax.ShapeDtypeStruct((s, d), BF16),
            hid_shape, hid_shape, hid_shape,
        ] + _scatter_shapes(scatter),
        scratch_shapes=[pltpu.VMEM((tm, d), F32)] + (_scatter_scratch(n_scatter) if n_scatter else []),
        compiler_params=_params(2),
    )(x, dy, gain, gate_s, up_s, wg_s, wu_s, wd_s, *scatter)


def _in_proj_bwd(x, dx_res, gain, dproj, w_s, layer, name):
    s, d = x.shape
    tm = TOKEN_TILE
    n = w_s.shape[3]

    def body(x_ref, r_ref, g_ref, dp_ref, w_ref, dx_ref, dgain_ref):
        @pl.when(pl.program_id(0) == 0)
        def _():
            dgain_ref[...] = jnp.zeros_like(dgain_ref)

        dh = _dot_nt(dp_ref[:, 0:n], w_ref[0, 0])
        for j in range(1, N_CHIPS):
            dh = dh + _dot_nt(dp_ref[:, j * n : (j + 1) * n], w_ref[j, 0])
        dxn, dgain = _rms_bwd(x_ref[...], g_ref[...], dh)
        dx_ref[...] = r_ref[...] + dxn
        dgain_ref[...] += dgain

    tok = pl.BlockSpec((tm, d), lambda i: (i, 0))
    vec = pl.BlockSpec((1, d), lambda i: (0, 0))
    return pl.pallas_call(
        body,
        name=name,
        grid=(s // tm,),
        in_specs=[tok, tok, vec, pl.BlockSpec((tm, N_CHIPS * n), lambda i: (i, 0)), pl.BlockSpec((N_CHIPS, 1, d, n), lambda i: (0, layer, 0, 0))],
        out_specs=[tok, vec],
        out_shape=[jax.ShapeDtypeStruct((s, d), F32), jax.ShapeDtypeStruct((1, d), F32)],
        compiler_params=_params(1),
    )(x, dx_res, gain, dproj, w_s)


def _loss_grad(y, target, name):
    s, d = y.shape
    tm = TOKEN_TILE

    def body(y_ref, t_ref, dy_ref, l_ref):
        @pl.when(pl.program_id(0) == 0)
        def _():
            l_ref[...] = jnp.zeros_like(l_ref)

        err = y_ref[...] - t_ref[...]
        dy_ref[...] = err / d
        l_ref[...] += jnp.sum(err * err, axis=0, keepdims=True) * (0.5 / d)

    tok = pl.BlockSpec((tm, d), lambda i: (i, 0))
    return pl.pallas_call(
        body,
        name=name,
        grid=(s // tm,),
        in_specs=[tok, tok],
        out_specs=[tok, pl.BlockSpec((1, d), lambda i: (0, 0))],
        out_shape=[jax.ShapeDtypeStruct((s, d), F32), jax.ShapeDtypeStruct((1, d), F32)],
        compiler_params=_params(1),
    )(y, target)


def _wgrad(a, b, a_spec, b_spec, n_blocks, k, n, name):
    n_tiles = a.shape[-2] // min(WGRAD_TILE, a.shape[-2])

    def body(a_ref, b_ref, o_ref):
        @pl.when(pl.program_id(1) == 0)
        def _():
            o_ref[...] = jnp.zeros_like(o_ref)

        av = a_ref[0] if len(a_ref.shape) == 3 else a_ref[...]
        bv = b_ref[0] if len(b_ref.shape) == 3 else b_ref[...]
        o_ref[0] += _dot_tn(av, bv)

    return pl.pallas_call(
        body,
        name=name,
        grid=(n_blocks, n_tiles),
        in_specs=[a_spec, b_spec],
        out_specs=pl.BlockSpec((1, k, n), lambda j, i: (j, 0, 0)),
        out_shape=jax.ShapeDtypeStruct((n_blocks, k, n), F32),
        compiler_params=_params(2),
    )(a, b)


def _mesh_position():
    return lax.axis_index("x"), lax.axis_index("y"), lax.axis_index("c")


def _other_chips(x, y):
    return [(1 - x, y), (x, 1 - y), (1 - x, 1 - y)]


def _half_rows(ref_rows, c):
    half = ref_rows // 2
    return pl.ds(c * half, half)


class _WeightGather:
    def __init__(self, ins, outs, sems):
        self.ins, self.outs = ins, outs
        send_sems, recv_sems, pass_send_sems, pass_recv_sems, self.local_sems = sems
        self.ici, self.d2d = (send_sems, recv_sems), (pass_send_sems, pass_recv_sems)
        self.x, self.y, self.c = _mesh_position()
        self.me = 2 * self.x + self.y
        self.sibling = (self.x, self.y, 1 - self.c)
        self.chips = _other_chips(self.x, self.y)

    def _copy(self, t, k, chip_index, core, to, sems, src=None):
        dst = self.outs[t].at[chip_index, :, _half_rows(self.ins[t].shape[1], core), :]
        return pltpu.make_async_remote_copy(
            src_ref=dst if src is None else src, dst_ref=dst, send_sem=sems[0].at[t, k], recv_sem=sems[1].at[t, k],
            device_id=to, device_id_type=MESH_ID,
        )

    def _own(self, t):
        return pltpu.make_async_copy(self.ins[t], self.outs[t].at[self.me], self.local_sems.at[t])

    def _sends(self):
        for t in range(len(self.ins)):
            mine = self.ins[t].at[:, _half_rows(self.ins[t].shape[1], self.c), :]
            for k, (px, py) in enumerate(self.chips):
                yield self._copy(t, k, self.me, self.c, (px, py, self.c), self.ici, src=mine)

    def _passes(self, core, sems):
        for t in range(len(self.ins)):
            for k, (px, py) in enumerate(self.chips):
                yield self._copy(t, k, 2 * px + py, core, self.sibling, sems)

    def begin(self):
        for t in range(len(self.ins)):
            self._own(t).start()
        for cp in self._sends():
            cp.start()

    def relay(self):
        for arrived, onward in zip(self._passes(self.c, self.ici), self._passes(self.c, self.d2d)):
            arrived.wait_recv()
            onward.start()

    def finish(self):
        for cp in self._passes(1 - self.c, self.d2d):
            cp.wait_recv()
        for cp in list(self._sends()) + list(self._passes(self.c, self.d2d)):
            cp.wait_send()
        for t in range(len(self.ins)):
            self._own(t).wait()


def _gather_scratch(n):
    sems = pltpu.SemaphoreType.DMA((n, N_CHIPS - 1))
    return [sems, sems, sems, sems, pltpu.SemaphoreType.DMA((n,))]


def _gather_weights(shards):
    n = len(shards)

    def body(*refs):
        gather = _WeightGather(refs[:n], refs[n : 2 * n], refs[2 * n :])
        gather.begin()
        gather.relay()
        gather.finish()

    return pl.pallas_call(
        body,
        name="gather_weights",
        in_specs=[ANY] * n,
        out_specs=[ANY] * n,
        out_shape=[jax.ShapeDtypeStruct((N_CHIPS,) + w.shape, w.dtype) for w in shards],
        scratch_shapes=_gather_scratch(n),
    )(*shards)


def _swap_halves(grads, tag):
    n = len(grads)

    def body(*refs):
        ins, outs = refs[:n], refs[n : 2 * n]
        send_sems, recv_sems = refs[2 * n :]
        x, y, c = _mesh_position()
        copies = []
        for t in range(n):
            copies.append(pltpu.make_async_remote_copy(
                src_ref=ins[t].at[:, _half_rows(ins[t].shape[1], 1 - c), :], dst_ref=outs[t],
                send_sem=send_sems.at[t], recv_sem=recv_sems.at[t], device_id=(x, y, 1 - c), device_id_type=MESH_ID,
            ))
            copies[-1].start()
        for cp in copies:
            cp.wait()

    sems = pltpu.SemaphoreType.DMA((n,))
    return pl.pallas_call(
        body,
        name=f"swap_halves_{tag}",
        in_specs=[ANY] * n,
        out_specs=[ANY] * n,
        out_shape=[jax.ShapeDtypeStruct((g.shape[0], g.shape[1] // 2, g.shape[2]), g.dtype) for g in grads],
        scratch_shapes=[sems, sems],
    )(*grads)


class _ChipScatter:
    def __init__(self, ins, outs, sems):
        self.ins, self.outs = ins, outs
        self.send_sems, self.recv_sems = sems
        self.x, self.y, self.c = _mesh_position()

    def _copies(self):
        for t in range(len(self.ins)):
            for k, (px, py) in enumerate(_other_chips(self.x, self.y)):
                yield pltpu.make_async_remote_copy(
                    src_ref=self.ins[t].at[2 * px + py], dst_ref=self.outs[t].at[k],
                    send_sem=self.send_sems.at[t, k], recv_sem=self.recv_sems.at[t, k],
                    device_id=(px, py, self.c), device_id_type=MESH_ID,
                )

    def begin(self):
        for cp in self._copies():
            cp.start()

    def finish(self):
        for cp in self._copies():
            cp.wait()


def _scatter_scratch(n):
    sems = pltpu.SemaphoreType.DMA((n, N_CHIPS - 1))
    return [sems, sems]


def _scatter_shapes(parts):
    return [jax.ShapeDtypeStruct((N_CHIPS - 1,) + p.shape[1:], p.dtype) for p in parts]


def _scatter_to_chips(parts, tag):
    n = len(parts)

    def body(*refs):
        copies = _ChipScatter(refs[:n], refs[n : 2 * n], refs[2 * n :])
        copies.begin()
        copies.finish()

    return pl.pallas_call(
        body,
        name=f"scatter_to_chips_{tag}",
        in_specs=[ANY] * n,
        out_specs=[ANY] * n,
        out_shape=_scatter_shapes(parts),
        scratch_shapes=_scatter_scratch(n),
    )(*parts)


def _join_halves(shards):
    n = len(shards)

    def body(*refs):
        outs = refs[n : 2 * n]
        send_sems, recv_sems = refs[2 * n :]
        x, y, c = _mesh_position()
        copies = []
        for t in range(n):
            mine = outs[t].at[:, _half_rows(outs[t].shape[1], c), :]
            copies.append(pltpu.make_async_remote_copy(
                src_ref=mine, dst_ref=mine, send_sem=send_sems.at[t], recv_sem=recv_sems.at[t],
                device_id=(x, y, 1 - c), device_id_type=MESH_ID,
            ))
            copies[-1].start()
        for cp in copies:
            cp.wait()

    sems = pltpu.SemaphoreType.DMA((n,))
    return pl.pallas_call(
        body,
        name="join_halves",
        in_specs=[ANY] * n,
        out_specs=[ANY] * n,
        out_shape=[jax.ShapeDtypeStruct(g.shape, g.dtype) for g in shards],
        input_output_aliases={t: t for t in range(n)},
        scratch_shapes=[sems, sems],
    )(*shards)


def _gather_small(pack):
    def body(p_ref, o_ref, send_sems, recv_sems, local_sem):
        x, y, c = _mesh_position()
        own = pltpu.make_async_copy(p_ref, o_ref.at[4 * x + 2 * y + c], local_sem)
        own.start()
        copies = []
        for k in range(1, N_DEV):
            px, py, pc = x ^ (k >> 2), y ^ ((k >> 1) & 1), c ^ (k & 1)
            send = pltpu.make_async_remote_copy(
                src_ref=p_ref, dst_ref=o_ref.at[4 * x + 2 * y + c], send_sem=send_sems.at[k - 1], recv_sem=recv_sems.at[k - 1],
                device_id=(px, py, pc), device_id_type=MESH_ID,
            )
            send.start()
            copies.append((send, 4 * px + 2 * py + pc))
        for send, peer_slot in copies:
            send.wait_send()
        for k in range(1, N_DEV):
            px, py, pc = x ^ (k >> 2), y ^ ((k >> 1) & 1), c ^ (k & 1)
            pltpu.make_async_remote_copy(
                src_ref=p_ref, dst_ref=o_ref.at[4 * px + 2 * py + pc], send_sem=send_sems.at[k - 1], recv_sem=recv_sems.at[k - 1],
                device_id=(px, py, pc), device_id_type=MESH_ID,
            ).wait_recv()
        own.wait()

    sems = pltpu.SemaphoreType.DMA((N_DEV - 1,))
    return pl.pallas_call(
        body,
        name="gather_small",
        in_specs=[VMEM_SPEC],
        out_specs=VMEM_SPEC,
        out_shape=jax.ShapeDtypeStruct((N_DEV,) + pack.shape, pack.dtype),
        scratch_shapes=[sems, sems, pltpu.SemaphoreType.DMA],
    )(pack)


def _row_tile(rows):
    for tile in range(min(rows, 512) // 8 * 8, 0, -8):
        if rows % tile == 0:
            return tile
    return rows


def _add_half(grad, received, half_index, name):
    slots, h, cdim = received.shape
    tile = _row_tile(h)
    per_half = h // tile

    def body(c_ref, g_ref, r_ref, o_ref, ob_ref):
        total = g_ref[...] + r_ref[...]
        o_ref[...] = total
        ob_ref[...] = total.astype(BF16)

    block = pl.BlockSpec((1, tile, cdim), lambda j, i, c: (j, i, 0))
    grid_spec = pltpu.PrefetchScalarGridSpec(
        num_scalar_prefetch=1,
        grid=(slots, per_half),
        in_specs=[pl.BlockSpec((1, tile, cdim), lambda j, i, c: (j, c[0] * per_half + i, 0)), block],
        out_specs=[block, block],
    )
    return pl.pallas_call(
        body, name=name, grid_spec=grid_spec,
        out_shape=[jax.ShapeDtypeStruct(received.shape, F32), jax.ShapeDtypeStruct(received.shape, BF16)],
        compiler_params=_params(2),
    )(half_index, grad, received)


def _add_chips(part, received, chip_index, core_index, layer, n_layers, shard, name):
    _, h, cdim = part.shape
    tile = _row_tile(h)
    per_half = h // tile

    def body(chip_ref, core_ref, p_ref, r_ref, *rest):
        o_ref = rest[-1]
        o_ref[0] = ((p_ref[0] + r_ref[0].astype(F32)) + r_ref[1].astype(F32)) + r_ref[2].astype(F32)

    in_specs = [
        pl.BlockSpec((1, tile, cdim), lambda i, chip, core: (chip[0], i, 0)),
        pl.BlockSpec((N_CHIPS - 1, tile, cdim), lambda i, chip, core: (0, i, 0)),
    ]
    operands = [chip_index, core_index, part, received]
    aliases = {}
    if shard is not None:
        in_specs.append(ANY)
        operands.append(shard)
        aliases = {4: 0}
    grid_spec = pltpu.PrefetchScalarGridSpec(
        num_scalar_prefetch=2,
        grid=(per_half,),
        in_specs=in_specs,
        out_specs=pl.BlockSpec((1, tile, cdim), lambda i, chip, core: (layer, core[0] * per_half + i, 0)),
    )
    return pl.pallas_call(
        body, name=name, grid_spec=grid_spec, out_shape=jax.ShapeDtypeStruct((n_layers, 2 * h, cdim), F32),
        input_output_aliases=aliases, compiler_params=_params(1),
    )(*operands)


def _adamw(w, g, m, v, name):
    rows, cdim = w.shape
    tile = _row_tile(rows)

    def body(w_ref, g_ref, m_ref, v_ref, d_ref, nm_ref, nv_ref):
        gv = g_ref[...]
        nm = ADAM_B1 * m_ref[...] + (1.0 - ADAM_B1) * gv
        nv = ADAM_B2 * v_ref[...] + (1.0 - ADAM_B2) * (gv * gv)
        m_hat = nm / (1.0 - ADAM_B1 ** ADAM_STEP)
        v_hat = nv / (1.0 - ADAM_B2 ** ADAM_STEP)
        d_ref[...] = -ADAM_LR * (m_hat / (jnp.sqrt(v_hat) + ADAM_EPS) + ADAM_WD * w_ref[...])
        nm_ref[...] = nm
        nv_ref[...] = nv

    spec = pl.BlockSpec((tile, cdim), lambda i: (i, 0))
    shape = jax.ShapeDtypeStruct((rows, cdim), F32)
    return pl.pallas_call(
        body, name=name, grid=(rows // tile,), in_specs=[spec] * 4, out_specs=[spec] * 3, out_shape=[shape] * 3,
        compiler_params=_params(1),
    )(w, g, m, v)


SMALL_ROWS, SMALL_COLS = 24, 1024
ROW_NORM_MIX, ROW_NORM_FFN, ROW_LOSS, ROW_Q_NORM, ROW_K_NORM, ROW_CONV = 0, 2, 4, 8, 10, 16


def _sum_small(gathered):
    def body(g_ref, o_ref, heads_ref, lanes_ref):
        total = g_ref[0]
        for dev in range(1, N_DEV):
            total = total + g_ref[dev]
        o_ref[...] = total
        heads = o_ref[8:16, 0:LANES]
        for grp in range(1, ATTN_DIM // LANES):
            heads = heads + o_ref[8:16, grp * LANES : (grp + 1) * LANES]
        heads_ref[...] = heads + pltpu.roll(heads, HEAD_DIM, 1)
        lanes_ref[...] = jnp.broadcast_to(jnp.sum(o_ref[0:8, :], axis=-1, keepdims=True), (8, LANES))

    return pl.pallas_call(
        body,
        name="sum_small",
        in_specs=[VMEM_SPEC],
        out_specs=[VMEM_SPEC] * 3,
        out_shape=[jax.ShapeDtypeStruct((SMALL_ROWS, SMALL_COLS), F32), jax.ShapeDtypeStruct((8, LANES), F32), jax.ShapeDtypeStruct((8, LANES), F32)],
    )(gathered)


def _pad_rows(a, rows):
    return jnp.pad(a, ((0, rows - a.shape[0]), (0, 0)))


def _pad_to(a, rows, cols):
    return jnp.pad(a, ((0, rows - a.shape[0]), (0, cols - a.shape[1])))


def _conv_taps(conv_s):
    return jnp.transpose(conv_s[:, 0, 0:8], (1, 0, 2)).reshape(8, -1)


class _GradExchange:
    def __init__(self, chip_index, core_index, n_layers):
        self.chip_index, self.core_index, self.n_layers = chip_index, core_index, n_layers
        self.shards = {}
        self.pending = None

    def offer(self, layer, grads):
        assert self.pending is None
        names = list(grads)
        received = _swap_halves([grads[k] for k in names], f"{'_'.join(names)}_{layer}")
        parts = [_add_half(grads[k], r, self.core_index, f"add_half_{k}_{layer}") for k, r in zip(names, received)]
        self.pending = (layer, names, [p32 for p32, _ in parts], [p16 for _, p16 in parts])

    def payload(self):
        return () if self.pending is None else tuple(self.pending[3])

    def take(self, received):
        layer, names, parts, _ = self.pending
        self.pending = None
        for k, p, r in zip(names, parts, received):
            self.shards[k] = _add_chips(
                p, r, self.chip_index, self.core_index, layer, self.n_layers, self.shards.get(k), f"add_chips_{k}_{layer}")

    def finish(self):
        if self.pending is not None:
            layer, names = self.pending[0], self.pending[1]
            self.take(_scatter_to_chips(list(self.pending[3]), f"{'_'.join(names)}_{layer}"))
        return dict(zip(BIG, _join_halves([self.shards[k] for k in BIG])))


def _local_step(x, target, norm_mix, q_norm, k_norm, norm_ffn, layer_weights, exchange=None):
    layer_weights = list(layer_weights)

    def carrying(kernel_fn, n_out, *args):
        if exchange is None or exchange.pending is None:
            return kernel_fn(*args)
        out = kernel_fn(*args, scatter=exchange.payload())
        exchange.take(out[n_out:])
        return out[:n_out]

    n_layers = norm_mix.shape[0]
    s, d = x.shape
    tw = min(WGRAD_TILE, s)
    n_in = layer_weights[0][0].shape[-1]
    f = layer_weights[0][2].shape[-1]
    saved = []
    for l in range(n_layers):
        weights = list(layer_weights[l])
        q_gain = jnp.tile(q_norm[l][None, :], (1, 2))
        k_gain = jnp.tile(k_norm[l][None, :], (1, 2))
        h1, proj, qn, kn, vb = _in_proj(x, norm_mix[l][None, :], weights[0], 0, q_gain, k_gain, f"in_proj_{l}")
        missing = [n for n, w in enumerate(weights) if w.ndim == 3]
        if missing:
            attn, *arrived = _attn_fwd(qn, kn, vb, f"attn_fwd_{l}", gather=tuple(weights[n] for n in missing))
            for n, w in zip(missing, arrived):
                weights[n] = w
            layer_weights[l] = tuple(weights)
        else:
            attn = _attn_fwd(qn, kn, vb, f"attn_fwd_{l}")
        _, wout_s, wg_s, wu_s, wd_s, conv_s = weights
        taps = _conv_taps(conv_s)
        x_mid, conv = _out_proj(x, attn, proj, taps, wout_s, 0, f"out_proj_{l}")
        pending = ()
        if l + 1 < n_layers and all(w.ndim == 3 for w in layer_weights[l + 1]):
            pending = tuple(layer_weights[l + 1])
        x_out, gate, up, *arrived = _ffn_fwd(x_mid, norm_ffn[l][None, :], wg_s, wu_s, wd_s, 0, f"ffn_fwd_{l}", gather=pending)
        if pending:
            layer_weights[l + 1] = tuple(arrived)
        saved.append(dict(x=x, h1=h1, proj=proj, qn=qn, kn=kn, vb=vb, attn=attn, conv=conv, x_mid=x_mid, q_gain=q_gain, k_gain=k_gain,
                          gate=gate, up=up, taps=taps))
        x = x_out

    dy, loss_lanes = _loss_grad(x, target, "loss_grad")
    grads = [None] * n_layers
    for l in reversed(range(n_layers)):
        sv = saved[l]
        win_s, wout_s, wg_s, wu_s, wd_s, _ = layer_weights[l]
        dx_mid, d_norm_ffn, h2, dyb, dgate, dup, act = carrying(
            _ffn_bwd, 7, sv["x_mid"], dy, norm_ffn[l][None, :], sv["gate"], sv["up"], wg_s, wu_s, wd_s, 0, f"ffn_bwd_{l}")
        tok2 = pl.BlockSpec((tw, d), lambda j, i: (i, 0))
        hid = pl.BlockSpec((1, tw, f), lambda j, i: (j, i, 0))
        d_wg = _wgrad(h2, dgate, tok2, hid, N_CHIPS, d, f, f"wgrad_gate_{l}")
        d_wu = _wgrad(h2, dup, tok2, hid, N_CHIPS, d, f, f"wgrad_up_{l}")
        d_wd = _wgrad(act, dyb, hid, tok2, N_CHIPS, f, d, f"wgrad_down_{l}")
        if exchange is not None:
            exchange.offer(l, dict(w_gate=d_wg, w_up=d_wu, w_down=d_wd))
        d_attn, d_conv, dxb = _out_proj_bwd(dx_mid, wout_s, 0, f"out_proj_bwd_{l}")
        rows_out = wout_s.shape[2]
        mix_spec_a = pl.BlockSpec((tw, rows_out), lambda j, i: (i, j))
        d_wout_a = _wgrad(sv["attn"], dxb, mix_spec_a, tok2, ATTN_DIM // rows_out, rows_out, d, f"wgrad_out_attn_{l}")
        d_wout_c = _wgrad(sv["conv"], dxb, mix_spec_a, tok2, CONV_DIM // rows_out, rows_out, d, f"wgrad_out_conv_{l}")
        d_wout = jnp.concatenate([d_wout_a, d_wout_c], axis=0)
        dq, dk, dv = carrying(_attn_bwd, 3, sv["qn"], sv["kn"], sv["vb"], d_attn, f"attn_bwd_{l}")
        dproj, d_conv_w = _conv_bwd(sv["proj"], sv["taps"], d_conv, f"conv_bwd_{l}")
        dproj, d_qg, d_kg = _qkv_prep_bwd(sv["proj"], sv["q_gain"], sv["k_gain"], dq, dk, dv, dproj, f"qkv_prep_bwd_{l}")
        d_win = _wgrad(sv["h1"], dproj, tok2, pl.BlockSpec((tw, n_in), lambda j, i: (i, j)), N_CHIPS, d, n_in, f"wgrad_in_{l}")
        if exchange is not None:
            exchange.offer(l, dict(w_in=d_win, w_out=d_wout))
        dy, d_norm_mix = _in_proj_bwd(sv["x"], dx_mid, norm_mix[l][None, :], dproj, win_s, 0, f"in_proj_bwd_{l}")
        grads[l] = dict(norm_mix=d_norm_mix, norm_ffn=d_norm_ffn, q_norm=d_qg, k_norm=d_kg, conv_w=d_conv_w,
                        w_in=d_win, w_out=d_wout, w_gate=d_wg, w_up=d_wu, w_down=d_wd)
    return loss_lanes, dy, grads


BIG = ("w_in", "w_out", "w_gate", "w_up", "w_down")


def kernel(x, norm_mix, w_in, q_norm, k_norm, conv_w, w_out, norm_ffn, w_gate, w_up, w_down, loss_target, m_norm_mix, m_w_in, m_q_norm, m_k_norm, m_conv_w, m_w_out, m_norm_ffn, m_w_gate, m_w_up, m_w_down, v_norm_mix, v_w_in, v_q_norm, v_k_norm, v_conv_w, v_w_out, v_norm_ffn, v_w_gate, v_w_up, v_w_down):
    n_layers = norm_mix.shape[0]
    weights = dict(w_in=w_in, w_out=w_out, w_gate=w_gate, w_up=w_up, w_down=w_down)
    moments_m = dict(w_in=m_w_in, w_out=m_w_out, w_gate=m_w_gate, w_up=m_w_up, w_down=m_w_down)
    moments_v = dict(w_in=v_w_in, w_out=v_w_out, w_gate=v_w_gate, w_up=v_w_up, w_down=v_w_down)
    cx, cy, cc = _mesh_position()
    chip_index = (2 * cx + cy).astype(jnp.int32).reshape(1)
    core_index = cc.astype(jnp.int32).reshape(1)

    conv_pad = jnp.pad(conv_w, ((0, 0), (0, 16 - conv_w.shape[1]), (0, 0)))

    def shards_of(layer):
        return [weights[k][layer : layer + 1].astype(BF16) for k in BIG] + [conv_pad[layer : layer + 1]]

    first = shards_of(0)
    layer_weights = [tuple(_gather_weights(first[:1])) + tuple(first[1:])] + [tuple(shards_of(layer)) for layer in range(1, n_layers)]

    exchange = _GradExchange(chip_index, core_index, n_layers)
    loss_lanes, grad_x, grads = _local_step(
        x[0], loss_target[0], norm_mix, q_norm, k_norm, norm_ffn, layer_weights, exchange)

    big_grads = exchange.finish()

    def lanes(a):
        return _pad_to(a, a.shape[0], SMALL_COLS)

    def tile_of(*groups):
        return _pad_rows(jnp.concatenate([lanes(jnp.concatenate(g, axis=0)) for g in groups], axis=0), 8)

    layers = range(n_layers)
    pack = jnp.concatenate([
        tile_of([grads[l]["norm_mix"] for l in layers], [grads[l]["norm_ffn"] for l in layers], [loss_lanes]),
        tile_of([grads[l]["q_norm"] for l in layers], [grads[l]["k_norm"] for l in layers]),
        tile_of([grads[l]["conv_w"][0:3] for l in layers]),
    ], axis=0)
    small, small_heads, small_lanes = _sum_small(_gather_small(pack))
    loss = small_lanes[ROW_LOSS, 0]
    d_model = norm_mix.shape[1]
    conv_cols = conv_w.shape[2]
    conv_all = small[ROW_CONV : ROW_CONV + 3 * n_layers, 0:CONV_DIM].reshape(n_layers, 3, CONV_DIM)
    small_grads = dict(
        norm_mix=small[ROW_NORM_MIX : ROW_NORM_MIX + n_layers, 0:d_model],
        norm_ffn=small[ROW_NORM_FFN : ROW_NORM_FFN + n_layers, 0:d_model],
        q_norm=small_heads[ROW_Q_NORM - 8 : ROW_Q_NORM - 8 + n_layers, 0:HEAD_DIM],
        k_norm=small_heads[ROW_K_NORM - 8 : ROW_K_NORM - 8 + n_layers, 0:HEAD_DIM],
        conv_w=lax.dynamic_slice_in_dim(conv_all, (2 * cx + cy) * conv_cols, conv_cols, axis=2),
    )

    out_grad, out_delta, out_m, out_v = {}, {}, {}, {}
    for k in BIG:
        shape = weights[k].shape
        view = (shape[0] * shape[1], shape[2])
        g = big_grads[k]
        delta, new_m, new_v = _adamw(weights[k].reshape(view), g.reshape(view), moments_m[k].reshape(view), moments_v[k].reshape(view), f"adamw_{k}")
        out_grad[k], out_delta[k], out_m[k], out_v[k] = g, delta.reshape(shape), new_m.reshape(shape), new_v.reshape(shape)

    small_w = dict(norm_mix=norm_mix, norm_ffn=norm_ffn, q_norm=q_norm, k_norm=k_norm, conv_w=conv_w)
    small_m = dict(norm_mix=m_norm_mix, norm_ffn=m_norm_ffn, q_norm=m_q_norm, k_norm=m_k_norm, conv_w=m_conv_w)
    small_v = dict(norm_mix=v_norm_mix, norm_ffn=v_norm_ffn, q_norm=v_q_norm, k_norm=v_k_norm, conv_w=v_conv_w)
    order = ("norm_mix", "norm_ffn", "q_norm", "k_norm", "conv_w")

    def packed(tree):
        parts2 = [_pad_to(tree[k].reshape(-1, tree[k].shape[-1]), tree[k].reshape(-1, tree[k].shape[-1]).shape[0], SMALL_COLS) for k in order]
        return _pad_rows(jnp.concatenate(parts2, axis=0), SMALL_ROWS)

    delta_p, m_p, v_p = _adamw(packed(small_w), packed(small_grads), packed(small_m), packed(small_v), "adamw_small")
    row = 0
    for k in order:
        shape = small_w[k].shape
        n_rows = 1
        for dim in shape[:-1]:
            n_rows *= dim
        cut = (slice(row, row + n_rows), slice(0, shape[-1]))
        out_grad[k] = small_grads[k]
        out_delta[k], out_m[k], out_v[k] = delta_p[cut].reshape(shape), m_p[cut].reshape(shape), v_p[cut].reshape(shape)
        row += n_rows

    names_out = ("norm_mix", "w_in", "q_norm", "k_norm", "conv_w", "w_out", "norm_ffn", "w_gate", "w_up", "w_down")
    return (loss, grad_x[None], *[out_grad[k] for k in names_out], *[out_delta[k] for k in names_out],
            *[out_m[k] for k in names_out], *[out_v[k] for k in names_out])
```

```python
import functools

import jax
import jax.numpy as jnp
from jax import lax
from jax.experimental import pallas as pl
from jax.experimental.pallas import tpu as pltpu

F32 = jnp.float32
BF16 = jnp.bfloat16

EPS = 1e-6
HEAD_DIM = 64
LANES = 128
ATTN_DIM = 512
CONV_DIM = 512
N_CHIPS = 4
N_DEV = 8
Q_SCALE = HEAD_DIM ** -0.5
ATTN_Q_TILE = 256
ATTN_TILE = 256
TOKEN_TILE = 512
WGRAD_TILE = 4096
FFN_FWD_TILE = 1024
FFN_CHUNK = 256
VMEM_LIMIT = 56 * 1024 * 1024

ADAM_LR = 0.001
ADAM_B1 = 0.9
ADAM_B2 = 0.999
ADAM_EPS = 1e-08
ADAM_WD = 0.01
ADAM_STEP = 10

MESH_ID = pl.DeviceIdType.MESH
ANY = pl.BlockSpec(memory_space=pl.ANY)
VMEM_SPEC = pl.BlockSpec(memory_space=pltpu.VMEM)


def _params(n_axes):
    return pltpu.CompilerParams(dimension_semantics=("arbitrary",) * n_axes, vmem_limit_bytes=VMEM_LIMIT)


def _dot(a, b):
    return jnp.dot(a, b, preferred_element_type=F32)


def _dot_nt(a, b):
    return lax.dot_general(a, b, (((1,), (1,)), ((), ())), preferred_element_type=F32)


def _dot_tn(a, b):
    return lax.dot_general(a, b, (((0,), (0,)), ((), ())), preferred_element_type=F32)


SCORE_MAX = 80.0
UNDERFLOW_EXIT = 90.0


def _scores(q, k):
    return jnp.minimum(_dot_nt(q, k), SCORE_MAX)


def _softplus(z):
    return jnp.log(1.0 + jnp.exp(z))


def _head_norm(xv, gain, low):
    sq = xv * xv
    s_low = jnp.sum(jnp.where(low, sq, 0.0), axis=-1, keepdims=True)
    s_high = jnp.sum(jnp.where(low, 0.0, sq), axis=-1, keepdims=True)
    r = jnp.where(low, lax.rsqrt(s_low / HEAD_DIM + EPS), lax.rsqrt(s_high / HEAD_DIM + EPS))
    return xv * r * gain, r


def _in_proj(x, gain, w_s, layer, q_gain, k_gain, name):
    s, d = x.shape
    n_blocks, _, _, n = w_s.shape
    tm = TOKEN_TILE

    def body(x_ref, g_ref, w_ref, qg_ref, kg_ref, h_ref, o_ref, q_ref, k_ref, v_ref):
        xv = x_ref[...]
        r = lax.rsqrt(jnp.mean(xv * xv, axis=-1, keepdims=True) + EPS)
        h = (xv * r * g_ref[...]).astype(BF16)
        h_ref[...] = h
        for j in range(n_blocks):
            o_ref[:, j * n : (j + 1) * n] = _dot(h, w_ref[j, 0])
        low = lax.broadcasted_iota(jnp.int32, (tm, LANES), 1) < HEAD_DIM
        for g in range(ATTN_DIM // LANES):
            cq = slice(LANES * g, LANES * (g + 1))
            ck = slice(ATTN_DIM + LANES * g, ATTN_DIM + LANES * (g + 1))
            cv = slice(2 * ATTN_DIM + LANES * g, 2 * ATTN_DIM + LANES * (g + 1))
            qn, _ = _head_norm(o_ref[:, cq], qg_ref[...], low)
            kn, _ = _head_norm(o_ref[:, ck], kg_ref[...], low)
            q_ref[:, cq] = (qn * Q_SCALE).astype(BF16)
            k_ref[:, cq] = kn.astype(BF16)
            v_ref[:, cq] = o_ref[:, cv].astype(BF16)

    head_spec = pl.BlockSpec((tm, ATTN_DIM), lambda i: (i, 0))
    head_shape = jax.ShapeDtypeStruct((s, ATTN_DIM), BF16)
    gain_spec = pl.BlockSpec((1, LANES), lambda i: (0, 0))
    return pl.pallas_call(
        body,
        name=name,
        grid=(s // tm,),
        in_specs=[
            pl.BlockSpec((tm, d), lambda i: (i, 0)),
            pl.BlockSpec((1, d), lambda i: (0, 0)),
            pl.BlockSpec((n_blocks, 1, d, n), lambda i: (0, layer, 0, 0)),
            gain_spec, gain_spec,
        ],
        out_specs=[pl.BlockSpec((tm, d), lambda i: (i, 0)), pl.BlockSpec((tm, n_blocks * n), lambda i: (i, 0)), head_spec, head_spec, head_spec],
        out_shape=[jax.ShapeDtypeStruct((s, d), BF16), jax.ShapeDtypeStruct((s, n_blocks * n), F32), head_shape, head_shape, head_shape],
        compiler_params=_params(1),
    )(x, gain, w_s, q_gain, k_gain)


def _attn_tile_consts(t):
    row = lax.broadcasted_iota(jnp.int32, (t, t), 0)
    col = lax.broadcasted_iota(jnp.int32, (t, t), 1)
    return row, col


def _triangle_sum(v, triangle):
    return _dot(v.astype(BF16), triangle)


def _attn_fwd(qn, kn, vb, name, gather=()):
    s = qn.shape[0]
    t = min(ATTN_TILE, s)
    tq = min(ATTN_Q_TILE, t)
    per_key_tile = t // tq
    n_gather = len(gather)
    n_pairs, n_blocks = ATTN_DIM // LANES, s // tq

    def body(*refs):
        q_ref, k_ref, v_ref = refs[:3]
        o_ref = refs[3 + n_gather]
        if n_gather:
            copies = _WeightGather(refs[3 : 3 + n_gather], refs[4 + n_gather : 4 + 2 * n_gather], refs[4 + 2 * n_gather :])
            first = (pl.program_id(0) == 0) & (pl.program_id(1) == 0)
            pl.when(first)(copies.begin)
            pl.when((pl.program_id(0) == n_pairs - 1) & (pl.program_id(1) == 0))(copies.relay)
        i = pl.program_id(1) // per_key_tile
        low = lax.broadcasted_iota(jnp.int32, (tq, LANES), 1) < HEAD_DIM
        row, col = _attn_tile_consts(t)
        suffix = (row > col).astype(BF16)
        first_row = (pl.program_id(1) % per_key_tile) * tq
        causal = lax.broadcasted_iota(jnp.int32, (tq, t), 1) < lax.broadcasted_iota(jnp.int32, (tq, t), 0) + first_row
        q = q_ref[...]
        zero_q = jnp.zeros_like(q)
        qh = (jnp.where(low, q, zero_q), jnp.where(low, zero_q, q))

        def step(kbs, carry, diagonal_first=False):
            chains = [(head, m) for head in range(2) for m in range(len(kbs))]
            masked = [diagonal_first and m == 0 for _, m in chains]
            ks = [k_ref[pl.ds(pl.multiple_of(kb * t, t), t), :] for kb in kbs]
            vs = [v_ref[pl.ds(pl.multiple_of(kb * t, t), t), :] for kb in kbs]
            z = [_scores(qh[head], ks[kb]) for head, kb in chains]
            sp = [_softplus(zc) for zc in z]
            sp = [jnp.where(causal, s_, 0.0) if mk else s_ for s_, mk in zip(sp, masked)]
            inside = [_triangle_sum(s_, suffix) for s_ in sp]
            after = [carry[head][1] for head in range(2)]
            log_a = []
            for n, (head, kb) in enumerate(chains):
                log_a.append(z[n] - sp[n] - inside[n] - after[head])
                after[head] = after[head] + jnp.sum(sp[n], axis=-1, keepdims=True)
            a = [jnp.exp(l_) for l_ in log_a]
            a = [jnp.where(causal, a_, 0.0) if mk else a_ for a_, mk in zip(a, masked)]
            acc = [carry[head][0] for head in range(2)]
            for n, (head, kb) in enumerate(chains):
                acc[head] = acc[head] + _dot(a[n].astype(BF16), vs[kb])
            return tuple((acc[head], after[head]) for head in range(2))

        def live(c):
            return jnp.minimum(jnp.min(c[0][1]), jnp.min(c[1][1])) < UNDERFLOW_EXIT

        zero = (jnp.zeros((tq, LANES), F32), jnp.zeros((tq, 1), F32))
        start = lax.cond(i >= 1, lambda c: step((i, i - 1), c, True), lambda c: step((i,), c, True), (zero, zero))
        o_ref[...] = jnp.where(low, start[0][0], start[1][0]).astype(BF16)
        rest = jnp.maximum(i - 1, 0)

        @pl.when((rest > 0) & live(start))
        def _():
            carry = lax.cond(rest % 2 == 1, lambda c: step((i - 2,), c), lambda c: c, start)
            pairs = rest // 2
            _, carry = lax.while_loop(
                lambda st: (st[0] < pairs) & live(st[1]),
                lambda st: (st[0] + 1, step((2 * (pairs - st[0]) - 1, 2 * (pairs - st[0]) - 2), st[1])),
                (jnp.int32(0), carry))
            o_ref[...] = jnp.where(low, carry[0][0], carry[1][0]).astype(BF16)

        if n_gather:
            pl.when((pl.program_id(0) == n_pairs - 1) & (pl.program_id(1) == n_blocks - 1))(copies.finish)

    out = pl.pallas_call(
        body,
        name=name,
        grid=(n_pairs, n_blocks),
        in_specs=[
            pl.BlockSpec((tq, LANES), lambda p, i: (i, p)),
            pl.BlockSpec((s, LANES), lambda p, i: (0, p)),
            pl.BlockSpec((s, LANES), lambda p, i: (0, p)),
        ] + [ANY] * n_gather,
        out_specs=[pl.BlockSpec((tq, LANES), lambda p, i: (i, p))] + [ANY] * n_gather,
        out_shape=[jax.ShapeDtypeStruct((s, ATTN_DIM), BF16)] + [jax.ShapeDtypeStruct((N_CHIPS,) + w.shape, w.dtype) for w in gather],
        scratch_shapes=_gather_scratch(n_gather) if n_gather else [],
        compiler_params=_params(2),
    )(qn, kn, vb, *gather)
    return out if n_gather else out[0]


def _attn_bwd(qn, kn, vb, do, name, scatter=()):
    s = qn.shape[0]
    t = min(ATTN_TILE, s)
    nq = s // t
    n_scatter = len(scatter)
    n_pairs = ATTN_DIM // LANES

    def body(*refs):
        q_ref, k_ref, v_ref, do_ref = refs[:4]
        dq_ref, dk_ref, dv_ref = refs[4 + n_scatter : 7 + n_scatter]
        a_s, sg_s, a_f, sg_f = refs[7 + 2 * n_scatter : 11 + 2 * n_scatter]
        i = pl.program_id(1)
        if n_scatter:
            copies = _ChipScatter(refs[4 : 4 + n_scatter], refs[7 + n_scatter : 7 + 2 * n_scatter], refs[11 + 2 * n_scatter :])
            pl.when((pl.program_id(0) == 0) & (i == 0))(copies.begin)

        @pl.when(i == 0)
        def _():
            dk_ref[...] = jnp.zeros_like(dk_ref)
            dv_ref[...] = jnp.zeros_like(dv_ref)

        low = lax.broadcasted_iota(jnp.int32, (t, LANES), 1) < HEAD_DIM
        row, col = _attn_tile_consts(t)
        suffix = (row > col).astype(BF16)
        prefix = (row < col).astype(BF16)
        causal = col < row
        q = q_ref[...]
        dob = do_ref[...]
        zero_q = jnp.zeros_like(q)
        qhs = (jnp.where(low, q, zero_q), jnp.where(low, zero_q, q))
        dohs = (jnp.where(low, dob, zero_q), jnp.where(low, zero_q, dob))

        def rows_of(kb):
            return pl.ds(pl.multiple_of(kb * t, t), t)

        pair = [(head, m) for head in range(2) for m in range(2)]

        def short_pass1():
            z = [_scores(qhs[head], k_ref[rows_of(i - m), :]) for head, m in pair]
            sp = [_softplus(z_) for z_ in z]
            sp = [jnp.where(causal, s_, 0.0) if m == 0 else s_ for s_, (_, m) in zip(sp, pair)]
            inside = [_triangle_sum(s_, suffix) for s_ in sp]
            after = [jnp.zeros((t, 1), F32), jnp.zeros((t, 1), F32)]
            for n, (head, m) in enumerate(pair):
                log_sg = z[n] - sp[n]
                a = jnp.exp(log_sg - inside[n] - after[head])
                sg = jnp.exp(log_sg)
                if m == 0:
                    a = jnp.where(causal, a, 0.0)
                    sg = jnp.where(causal, sg, 0.0)
                a_f[n] = a
                sg_f[n] = sg
                after[head] = after[head] + jnp.sum(sp[n], axis=-1, keepdims=True)
            return jnp.minimum(jnp.min(after[0]), jnp.min(after[1])) >= UNDERFLOW_EXIT

        def short_pass2():
            order = [(head, m) for head in range(2) for m in (1, 0)]
            a = {c: a_f[pair.index(c)] for c in order}
            g = {c: a[c] * _dot_nt(dohs[c[0]], v_ref[rows_of(i - c[1]), :]) for c in order}
            for m in (1, 0):
                dv_ref[rows_of(i - m), :] += _dot_tn(a[(0, m)].astype(BF16), dohs[0]) + _dot_tn(a[(1, m)].astype(BF16), dohs[1])
            inside = {c: _triangle_sum(g[c], prefix) for c in order}
            before = [jnp.zeros((t, 1), F32), jnp.zeros((t, 1), F32)]
            dz = {}
            for c in order:
                sg = sg_f[pair.index(c)]
                dz[c] = (g[c] - sg * (g[c] + inside[c] + before[c[0]])).astype(BF16)
                before[c[0]] = before[c[0]] + jnp.sum(g[c], axis=-1, keepdims=True)
            for m in (1, 0):
                dk_ref[rows_of(i - m), :] += _dot_tn(dz[(0, m)], qhs[0]) + _dot_tn(dz[(1, m)], qhs[1])
            dq = [_dot(dz[(head, 1)], k_ref[rows_of(i - 1), :]) + _dot(dz[(head, 0)], k_ref[rows_of(i), :]) for head in range(2)]
            dq_ref[...] = jnp.where(low, dq[0], dq[1])

        def general_walk():
            heads = []
            for head in range(2):
                qh, doh = qhs[head], dohs[head]

                def pass1(kbs, after, diagonal_first=False):
                    z = [_scores(qh, k_ref[rows_of(kb), :]) for kb in kbs]
                    sp = [_softplus(z_) for z_ in z]
                    if diagonal_first:
                        sp[0] = jnp.where(causal, sp[0], 0.0)
                    inside = [_triangle_sum(s_, suffix) for s_ in sp]
                    for n, kb in enumerate(kbs):
                        log_sg = z[n] - sp[n]
                        a = jnp.exp(log_sg - inside[n] - after)
                        sg = jnp.exp(log_sg)
                        if diagonal_first and n == 0:
                            a = jnp.where(causal, a, 0.0)
                            sg = jnp.where(causal, sg, 0.0)
                        a_s[kb] = a
                        sg_s[kb] = sg
                        after = after + jnp.sum(sp[n], axis=-1, keepdims=True)
                    return after

                def live(after):
                    return jnp.min(after) < UNDERFLOW_EXIT

                after = jnp.zeros((t, 1), F32)
                after = lax.cond(i >= 1, lambda c: pass1((i, i - 1), c, True), lambda c: pass1((i,), c, True), after)
                rest = jnp.maximum(i - 1, 0)
                take_single = (rest % 2 == 1) & live(after)
                after = lax.cond(take_single, lambda c: pass1((i - 2,), c), lambda c: c, after)
                pairs = rest // 2
                pairs_done, _ = lax.while_loop(
                    lambda st: (st[0] < pairs) & live(st[1]),
                    lambda st: (st[0] + 1, pass1((2 * (pairs - st[0]) - 1, 2 * (pairs - st[0]) - 2), st[1])),
                    (jnp.int32(0), after))
                walked = jnp.minimum(i, 1) + 1 + take_single.astype(jnp.int32) + 2 * pairs_done
                first = i - walked + 1

                def pass2(kbs, carry):
                    dq, before = carry
                    ks = [k_ref[rows_of(kb), :] for kb in kbs]
                    a = [a_s[kb] for kb in kbs]
                    g = [a_ * _dot_nt(doh, v_ref[rows_of(kb), :]) for a_, kb in zip(a, kbs)]
                    for n, kb in enumerate(kbs):
                        dv_ref[rows_of(kb), :] += _dot_tn(a[n].astype(BF16), doh)
                    inside = [_triangle_sum(g_, prefix) for g_ in g]
                    dz = []
                    for n, kb in enumerate(kbs):
                        sg = sg_s[kb]
                        dz.append((g[n] - sg * (g[n] + inside[n] + before)).astype(BF16))
                        before = before + jnp.sum(g[n], axis=-1, keepdims=True)
                    for n, kb in enumerate(kbs):
                        dk_ref[rows_of(kb), :] += _dot_tn(dz[n], qh)
                    for n in range(len(kbs)):
                        dq = dq + _dot(dz[n], ks[n])
                    return dq, before

                carry = (jnp.zeros((t, LANES), F32), jnp.zeros((t, 1), F32))
                carry = lax.fori_loop(0, walked // 2, lambda n, c: pass2((first + 2 * n, first + 2 * n + 1), c), carry)
                carry = lax.cond(walked % 2 == 1, lambda c: pass2((i,), c), lambda c: c, carry)
                heads.append(carry[0])
            dq_ref[...] = jnp.where(low, heads[0], heads[1])

        short = lax.cond(i >= 1, short_pass1, lambda: jnp.bool_(False))
        pl.when(short)(short_pass2)
        pl.when(jnp.logical_not(short))(general_walk)
        if n_scatter:
            pl.when((pl.program_id(0) == n_pairs - 1) & (i == nq - 1))(copies.finish)

    q_spec = pl.BlockSpec((t, LANES), lambda p, i: (i, p))
    kv_spec = pl.BlockSpec((s, LANES), lambda p, i: (0, p))
    return pl.pallas_call(
        body,
        name=name,
        grid=(n_pairs, nq),
        in_specs=[q_spec, kv_spec, kv_spec, q_spec] + [ANY] * n_scatter,
        out_specs=[q_spec, kv_spec, kv_spec] + [ANY] * n_scatter,
        out_shape=[jax.ShapeDtypeStruct((s, ATTN_DIM), F32)] * 3 + _scatter_shapes(scatter),
        scratch_shapes=[pltpu.VMEM((nq, t, t), F32), pltpu.VMEM((nq, t, t), F32), pltpu.VMEM((4, t, t), F32), pltpu.VMEM((4, t, t), F32)]
        + (_scatter_scratch(n_scatter) if n_scatter else []),
        compiler_params=_params(2),
    )(qn, kn, vb, do, *scatter)


CB_BLOCK, CC_BLOCK, CU_BLOCK = 3, 4, 5


def _shift_down(h, prev_rows, n):
    row = lax.broadcasted_iota(jnp.int32, h.shape, 0)
    out = pltpu.roll(h, n, 0)
    for r in range(n):
        out = jnp.where(row == r, prev_rows[len(prev_rows) - n + r], out)
    return out


def _shift_up(h, next_rows, n):
    tm = h.shape[0]
    row = lax.broadcasted_iota(jnp.int32, h.shape, 0)
    out = pltpu.roll(h, tm - n, 0)
    for r in range(n):
        out = jnp.where(row == tm - n + r, next_rows[r], out)
    return out


def _conv_bwd(proj, conv_w, dconv, name):
    s = proj.shape[0]
    tm = TOKEN_TILE
    nb = tm // 8
    n_tiles = s // tm

    def body(cb_ref, cc_ref, cu_ref, dy_ref, pc_ref, pu_ref, nb_ref, ndy_ref, w_ref, dp_ref, dw_ref):
        i = pl.program_id(0)

        @pl.when(i == 0)
        def _():
            dw_ref[...] = jnp.zeros_like(dw_ref)

        first = i == 0
        last = i == n_tiles - 1
        cc, cu, cb, dy = cc_ref[...], cu_ref[...], cb_ref[...], dy_ref[...]
        h = cc * cu
        prev = [jnp.where(first, 0.0, pc_ref[r : r + 1, :] * pu_ref[r : r + 1, :]) for r in (6, 7)]
        h1 = _shift_down(h, prev, 1)
        h2 = _shift_down(h, prev, 2)
        y = w_ref[0:1, :] * h2 + w_ref[1:2, :] * h1 + w_ref[2:3, :] * h
        dyb = dy * cb
        nxt = [jnp.where(last, 0.0, ndy_ref[r : r + 1, :] * nb_ref[r : r + 1, :]) for r in (0, 1)]
        dh = w_ref[2:3, :] * dyb + w_ref[1:2, :] * _shift_up(dyb, nxt, 1) + w_ref[0:1, :] * _shift_up(dyb, nxt, 2)
        dp_ref[:, 0:CONV_DIM] = (dy * y).astype(BF16)
        dp_ref[:, CONV_DIM : 2 * CONV_DIM] = (dh * cu).astype(BF16)
        dp_ref[:, 2 * CONV_DIM : 3 * CONV_DIM] = (dh * cc).astype(BF16)
        dw_ref[0:1, :] += jnp.sum(dyb * h2, axis=0, keepdims=True)
        dw_ref[1:2, :] += jnp.sum(dyb * h1, axis=0, keepdims=True)
        dw_ref[2:3, :] += jnp.sum(dyb * h, axis=0, keepdims=True)

    def col(block):
        return pl.BlockSpec((tm, CONV_DIM), lambda i: (i, block))

    def halo_prev(block):
        return pl.BlockSpec((8, CONV_DIM), lambda i: (jnp.maximum(i * nb - 1, 0), block))

    def halo_next(block):
        return pl.BlockSpec((8, CONV_DIM), lambda i: (jnp.minimum((i + 1) * nb, s // 8 - 1), block))

    return pl.pallas_call(
        body,
        name=name,
        grid=(n_tiles,),
        in_specs=[
            col(CB_BLOCK), col(CC_BLOCK), col(CU_BLOCK), col(0),
            halo_prev(CC_BLOCK), halo_prev(CU_BLOCK), halo_next(CB_BLOCK), halo_next(0),
            pl.BlockSpec((8, CONV_DIM), lambda i: (0, 0)),
        ],
        out_specs=[pl.BlockSpec((tm, 3 * CONV_DIM), lambda i: (i, 1)), pl.BlockSpec((8, CONV_DIM), lambda i: (0, 0))],
        out_shape=[jax.ShapeDtypeStruct((s, 3 * ATTN_DIM + 3 * CONV_DIM), BF16), jax.ShapeDtypeStruct((8, CONV_DIM), F32)],
        compiler_params=_params(1),
    )(proj, proj, proj, dconv, proj, proj, proj, dconv, conv_w)


def _out_proj(x, attn, proj, conv_w, w_s, layer, name):
    s, d = x.shape
    tm = TOKEN_TILE
    nb = tm // 8
    rows = w_s.shape[2]

    def body(x_ref, a_ref, cb_ref, cc_ref, cu_ref, pc_ref, pu_ref, cw_ref, w_ref, o_ref, c_ref):
        first = pl.program_id(0) == 0
        h = cc_ref[...] * cu_ref[...]
        prev = [jnp.where(first, 0.0, pc_ref[r : r + 1, :] * pu_ref[r : r + 1, :]) for r in (6, 7)]
        y = cw_ref[0:1, :] * _shift_down(h, prev, 2) + cw_ref[1:2, :] * _shift_down(h, prev, 1) + cw_ref[2:3, :] * h
        c_ref[...] = (cb_ref[...] * y).astype(BF16)
        acc = x_ref[...]
        for j in range(N_CHIPS):
            src = a_ref if j < 2 else c_ref
            cols = slice((j % 2) * rows, (j % 2 + 1) * rows)
            acc = acc + _dot(src[:, cols], w_ref[j, 0])
        o_ref[...] = acc

    def col(block):
        return pl.BlockSpec((tm, CONV_DIM), lambda i: (i, block))

    def halo(block):
        return pl.BlockSpec((8, CONV_DIM), lambda i: (jnp.maximum(i * nb - 1, 0), block))

    return pl.pallas_call(
        body,
        name=name,
        grid=(s // tm,),
        in_specs=[
            pl.BlockSpec((tm, d), lambda i: (i, 0)),
            pl.BlockSpec((tm, ATTN_DIM), lambda i: (i, 0)),
            col(CB_BLOCK), col(CC_BLOCK), col(CU_BLOCK), halo(CC_BLOCK), halo(CU_BLOCK),
            pl.BlockSpec((8, CONV_DIM), lambda i: (0, 0)),
            pl.BlockSpec((N_CHIPS, 1, rows, d), lambda i: (0, layer, 0, 0)),
        ],
        out_specs=[pl.BlockSpec((tm, d), lambda i: (i, 0)), pl.BlockSpec((tm, CONV_DIM), lambda i: (i, 0))],
        out_shape=[jax.ShapeDtypeStruct((s, d), F32), jax.ShapeDtypeStruct((s, CONV_DIM), BF16)],
        compiler_params=_params(1),
    )(x, attn, proj, proj, proj, proj, proj, conv_w, w_s)


def _out_proj_bwd(dx, w_s, layer, name):
    s, d = dx.shape
    tm = TOKEN_TILE
    rows = w_s.shape[2]

    def body(dx_ref, w_ref, da_ref, dc_ref, dxb_ref):
        dxb = dx_ref[...].astype(BF16)
        dxb_ref[...] = dxb
        for j in range(N_CHIPS):
            cols = slice((j % 2) * rows, (j % 2 + 1) * rows)
            part = _dot_nt(dxb, w_ref[j, 0])
            if j < 2:
                da_ref[:, cols] = part.astype(BF16)
            else:
                dc_ref[:, cols] = part

    return pl.pallas_call(
        body,
        name=name,
        grid=(s // tm,),
        in_specs=[pl.BlockSpec((tm, d), lambda i: (i, 0)), pl.BlockSpec((N_CHIPS, 1, rows, d), lambda i: (0, layer, 0, 0))],
        out_specs=[
            pl.BlockSpec((tm, ATTN_DIM), lambda i: (i, 0)),
            pl.BlockSpec((tm, CONV_DIM), lambda i: (i, 0)),
            pl.BlockSpec((tm, d), lambda i: (i, 0)),
        ],
        out_shape=[
            jax.ShapeDtypeStruct((s, ATTN_DIM), BF16),
            jax.ShapeDtypeStruct((s, CONV_DIM), F32),
            jax.ShapeDtypeStruct((s, d), BF16),
        ],
        compiler_params=_params(1),
    )(dx, w_s)


def _ffn_fwd(x, gain, wg_s, wu_s, wd_s, layer, name, gather=()):
    s, d = x.shape
    tm = min(FFN_FWD_TILE, s)
    f = wg_s.shape[3]
    n_gather = len(gather)
    n_tiles = s // tm

    def body(*refs):
        x_ref, g_ref, wg_ref, wu_ref, wd_ref = refs[:5]
        o_ref, gate_ref, up_ref = refs[5 + n_gather : 8 + n_gather]
        h_s = refs[8 + 2 * n_gather]
        i, j = pl.program_id(0), pl.program_id(1)
        if n_gather:
            copies = _WeightGather(refs[5 : 5 + n_gather], refs[8 + n_gather : 8 + 2 * n_gather], refs[9 + 2 * n_gather :])
            pl.when((i == 0) & (j == 0))(copies.begin)
            pl.when((i == (3 * n_tiles) // 4) & (j == 0))(copies.relay)

        @pl.when(j == 0)
        def _():
            xv = x_ref[...]
            r = lax.rsqrt(jnp.mean(xv * xv, axis=-1, keepdims=True) + EPS)
            h_s[...] = (xv * r * g_ref[...]).astype(BF16)
            o_ref[...] = xv

        halves = [slice(r, r + FFN_CHUNK) for r in range(0, tm, FFN_CHUNK)]
        pre = [(_dot(h_s[r, :], wg_ref[0, 0]), _dot(h_s[r, :], wu_ref[0, 0])) for r in halves]
        act = [((gate / (1.0 + jnp.exp(-gate))) * up).astype(BF16) for gate, up in pre]
        for r, (gate, up) in zip(halves, pre):
            gate_ref[0, r, :] = gate.astype(BF16)
            up_ref[0, r, :] = up.astype(BF16)
        for r, a in zip(halves, act):
            o_ref[r, :] += _dot(a, wd_ref[0, 0])

        if n_gather:
            pl.when((i == n_tiles - 1) & (j == N_CHIPS - 1))(copies.finish)

    hid = pl.BlockSpec((1, tm, f), lambda i, j: (j, i, 0))
    hid_shape = jax.ShapeDtypeStruct((N_CHIPS, s, f), BF16)
    return pl.pallas_call(
        body,
        name=name,
        grid=(n_tiles, N_CHIPS),
        in_specs=[
            pl.BlockSpec((tm, d), lambda i, j: (i, 0)),
            pl.BlockSpec((1, d), lambda i, j: (0, 0)),
            pl.BlockSpec((1, 1, d, f), lambda i, j: (j, layer, 0, 0)),
            pl.BlockSpec((1, 1, d, f), lambda i, j: (j, layer, 0, 0)),
            pl.BlockSpec((1, 1, f, d), lambda i, j: (j, layer, 0, 0)),
        ] + [ANY] * n_gather,
        out_specs=[pl.BlockSpec((tm, d), lambda i, j: (i, 0)), hid, hid] + [ANY] * n_gather,
        out_shape=[jax.ShapeDtypeStruct((s, d), F32), hid_shape, hid_shape]
        + [jax.ShapeDtypeStruct((N_CHIPS,) + w.shape, w.dtype) for w in gather],
        scratch_shapes=[pltpu.VMEM((tm, d), BF16)] + (_gather_scratch(n_gather) if n_gather else []),
        compiler_params=_params(2),
    )(x, gain, wg_s, wu_s, wd_s, *gather)


def _resident(block, layer):
    return pl.BlockSpec(block, lambda i, j: (0, layer, 0, 0), pipeline_mode=pl.Buffered(1))


def _rms_bwd(xv, gain, dh):
    r = lax.rsqrt(jnp.mean(xv * xv, axis=-1, keepdims=True) + EPS)
    xhat = xv * r
    dxhat = dh * gain
    dx = r * (dxhat - xhat * jnp.mean(dxhat * xhat, axis=-1, keepdims=True))
    return dx, jnp.sum(dh * xhat, axis=0, keepdims=True)


def _ffn_bwd(x, dy, gain, gate_s, up_s, wg_s, wu_s, wd_s, layer, name, scatter=()):
    s, d = x.shape
    tm = TOKEN_TILE
    f = wg_s.shape[3]

    n_scatter = len(scatter)
    n_tiles = s // tm

    def body(*refs):
        x_ref, dy_ref, g_ref, gate_ref, up_ref, wg_ref, wu_ref, wd_ref = refs[:8]
        dx_ref, dgain_ref, h_ref, dyb_ref, dg_ref, du_ref, act_ref = refs[8 + n_scatter : 15 + n_scatter]
        acc_s = refs[15 + 2 * n_scatter]
        i, j = pl.program_id(0), pl.program_id(1)
        if n_scatter:
            copies = _ChipScatter(refs[8 : 8 + n_scatter], refs[15 + n_scatter : 15 + 2 * n_scatter], refs[16 + 2 * n_scatter :])
            pl.when((i == 0) & (j == 0))(copies.begin)

        @pl.when((i == 0) & (j == 0))
        def _():
            dgain_ref[...] = jnp.zeros_like(dgain_ref)

        @pl.when(j == 0)
        def _():
            xv = x_ref[...]
            r = lax.rsqrt(jnp.mean(xv * xv, axis=-1, keepdims=True) + EPS)
            h_ref[...] = (xv * r * g_ref[...]).astype(BF16)
            dyb_ref[...] = dy_ref[...].astype(BF16)
            acc_s[...] = jnp.zeros_like(acc_s)

        halves = [slice(0, tm // 2), slice(tm // 2, tm)]
        pre = [(gate_ref[0, r, :].astype(F32), up_ref[0, r, :].astype(F32), _dot_nt(dyb_ref[r, :], wd_ref[j, 0])) for r in halves]
        grads = []
        for r, (gate, up, dact) in zip(halves, pre):
            sig = 1.0 / (1.0 + jnp.exp(-gate))
            silu = gate * sig
            dgate = (dact * up * (sig * (1.0 + gate * (1.0 - sig)))).astype(BF16)
            dup = (dact * silu).astype(BF16)
            act_ref[0, r, :] = (silu * up).astype(BF16)
            dg_ref[0, r, :] = dgate
            du_ref[0, r, :] = dup
            grads.append((dgate, dup))
        for r, (dgate, dup) in zip(halves, grads):
            acc_s[r, :] += _dot_nt(dgate, wg_ref[j, 0]) + _dot_nt(dup, wu_ref[j, 0])

        @pl.when(j == N_CHIPS - 1)
        def _():
            dxn, dgain = _rms_bwd(x_ref[...], g_ref[...], acc_s[...])
            dx_ref[...] = dy_ref[...] + dxn
            dgain_ref[...] += dgain

        if n_scatter:
            pl.when((i == n_tiles - 1) & (j == N_CHIPS - 1))(copies.finish)

    tok = pl.BlockSpec((tm, d), lambda i, j: (i, 0))
    vec = pl.BlockSpec((1, d), lambda i, j: (0, 0))
    hid = pl.BlockSpec((1, tm, f), lambda i, j: (j, i, 0))
    hid_shape = jax.ShapeDtypeStruct((N_CHIPS, s, f), BF16)
    return pl.pallas_call(
        body,
        name=name,
        grid=(n_tiles, N_CHIPS),
        in_specs=[
            tok, tok, vec, hid, hid,
            _resident((N_CHIPS, 1, d, f), layer),
            _resident((N_CHIPS, 1, d, f), layer),
            _resident((N_CHIPS, 1, f, d), layer),
        ] + [ANY] * n_scatter,
        out_specs=[tok, vec, tok, tok, hid, hid, hid] + [ANY] * n_scatter,
        out_shape=[
            jax.ShapeDtypeStruct((s, d), F32),
            jax.ShapeDtypeStruct((1, d), F32),
            jax.ShapeDtypeStruct((s, d), BF16),
            jax.ShapeDtypeStruct((s, d), BF16),
            hid_shape, hid_shape, hid_shape,
        ] + _scatter_shapes(scatter),
        scratch_shapes=[pltpu.VMEM((tm, d), F32)] + (_scatter_scratch(n_scatter) if n_scatter else []),
        compiler_params=_params(2),
    )(x, dy, gain, gate_s, up_s, wg_s, wu_s, wd_s, *scatter)


def _in_proj_bwd(x, dx_res, gain, w_s, layer, proj, q_gain, k_gain, dq, dk, dv, dproj_conv, name):
    s, d = x.shape
    tm = TOKEN_TILE
    n = w_s.shape[3]
    qkv = 3 * ATTN_DIM

    def norm_bwd(xv, head_gain, dy, low):
        _, r = _head_norm(xv, head_gain, low)
        xhat = xv * r
        dxhat = dy * head_gain
        prod = dxhat * xhat
        m_low = jnp.sum(jnp.where(low, prod, 0.0), axis=-1, keepdims=True)
        m_high = jnp.sum(jnp.where(low, 0.0, prod), axis=-1, keepdims=True)
        mean = jnp.where(low, m_low, m_high) / HEAD_DIM
        return r * (dxhat - xhat * mean), jnp.sum(dy * xhat, axis=0, keepdims=True)

    def body(x_ref, r_ref, g_ref, w_ref, p_ref, qg_ref, kg_ref, dq_ref, dk_ref, dv_ref, dpc_ref, dx_ref, dgain_ref, dp_ref, dqg_ref, dkg_ref):
        @pl.when(pl.program_id(0) == 0)
        def _():
            dgain_ref[...] = jnp.zeros_like(dgain_ref)
            dqg_ref[...] = jnp.zeros_like(dqg_ref)
            dkg_ref[...] = jnp.zeros_like(dkg_ref)

        low = lax.broadcasted_iota(jnp.int32, (tm, LANES), 1) < HEAD_DIM
        for g in range(ATTN_DIM // LANES):
            cq = slice(LANES * g, LANES * (g + 1))
            ck = slice(ATTN_DIM + LANES * g, ATTN_DIM + LANES * (g + 1))
            cv = slice(2 * ATTN_DIM + LANES * g, 2 * ATTN_DIM + LANES * (g + 1))
            dxq, dgq = norm_bwd(p_ref[:, cq], qg_ref[...], dq_ref[:, cq] * Q_SCALE, low)
            dxk, dgk = norm_bwd(p_ref[:, ck], kg_ref[...], dk_ref[:, cq], low)
            dp_ref[:, cq] = dxq.astype(BF16)
            dp_ref[:, ck] = dxk.astype(BF16)
            dp_ref[:, cv] = dv_ref[:, cq].astype(BF16)
            dqg_ref[:, cq] += dgq
            dkg_ref[:, cq] += dgk
        dp_ref[:, qkv:] = dpc_ref[...]

        dh = _dot_nt(dp_ref[:, 0:n], w_ref[0, 0])
        for j in range(1, N_CHIPS):
            dh = dh + _dot_nt(dp_ref[:, j * n : (j + 1) * n], w_ref[j, 0])
        dxn, dgain = _rms_bwd(x_ref[...], g_ref[...], dh)
        dx_ref[...] = r_ref[...] + dxn
        dgain_ref[...] += dgain

    tok = pl.BlockSpec((tm, d), lambda i: (i, 0))
    vec = pl.BlockSpec((1, d), lambda i: (0, 0))
    grad_spec = pl.BlockSpec((tm, ATTN_DIM), lambda i: (i, 0))
    gain_spec = pl.BlockSpec((1, LANES), lambda i: (0, 0))
    sum_spec = pl.BlockSpec((1, ATTN_DIM), lambda i: (0, 0))
    return pl.pallas_call(
        body,
        name=name,
        grid=(s // tm,),
        in_specs=[
            tok, tok, vec, pl.BlockSpec((N_CHIPS, 1, d, n), lambda i: (0, layer, 0, 0)),
            pl.BlockSpec((tm, qkv), lambda i: (i, 0)), gain_spec, gain_spec, grad_spec, grad_spec, grad_spec,
            pl.BlockSpec((tm, N_CHIPS * n - qkv), lambda i: (i, 1)),
        ],
        out_specs=[tok, vec, pl.BlockSpec((tm, N_CHIPS * n), lambda i: (i, 0)), sum_spec, sum_spec],
        out_shape=[
            jax.ShapeDtypeStruct((s, d), F32),
            jax.ShapeDtypeStruct((1, d), F32),
            jax.ShapeDtypeStruct((s, N_CHIPS * n), BF16),
            jax.ShapeDtypeStruct((1, ATTN_DIM), F32),
            jax.ShapeDtypeStruct((1, ATTN_DIM), F32),
        ],
        compiler_params=_params(1),
    )(x, dx_res, gain, w_s, proj, q_gain, k_gain, dq, dk, dv, dproj_conv)


def _loss_grad(y, target, name):
    s, d = y.shape
    tm = TOKEN_TILE

    def body(y_ref, t_ref, dy_ref, l_ref):
        @pl.when(pl.program_id(0) == 0)
        def _():
            l_ref[...] = jnp.zeros_like(l_ref)

        err = y_ref[...] - t_ref[...]
        dy_ref[...] = err / d
        l_ref[...] += jnp.sum(err * err, axis=0, keepdims=True) * (0.5 / d)

    tok = pl.BlockSpec((tm, d), lambda i: (i, 0))
    return pl.pallas_call(
        body,
        name=name,
        grid=(s // tm,),
        in_specs=[tok, tok],
        out_specs=[tok, pl.BlockSpec((1, d), lambda i: (0, 0))],
        out_shape=[jax.ShapeDtypeStruct((s, d), F32), jax.ShapeDtypeStruct((1, d), F32)],
        compiler_params=_params(1),
    )(y, target)


def _wgrad(a, b, a_spec, b_spec, n_blocks, k, n, name):
    n_tiles = a.shape[-2] // min(WGRAD_TILE, a.shape[-2])

    def body(a_ref, b_ref, o_ref):
        @pl.when(pl.program_id(1) == 0)
        def _():
            o_ref[...] = jnp.zeros_like(o_ref)

        av = a_ref[0] if len(a_ref.shape) == 3 else a_ref[...]
        bv = b_ref[0] if len(b_ref.shape) == 3 else b_ref[...]
        o_ref[0] += _dot_tn(av, bv)

    return pl.pallas_call(
        body,
        name=name,
        grid=(n_blocks, n_tiles),
        in_specs=[a_spec, b_spec],
        out_specs=pl.BlockSpec((1, k, n), lambda j, i: (j, 0, 0)),
        out_shape=jax.ShapeDtypeStruct((n_blocks, k, n), F32),
        compiler_params=_params(2),
    )(a, b)


def _mesh_position():
    return lax.axis_index("x"), lax.axis_index("y"), lax.axis_index("c")


def _other_chips(x, y):
    return [(1 - x, y), (x, 1 - y), (1 - x, 1 - y)]


def _half_rows(ref_rows, c):
    half = ref_rows // 2
    return pl.ds(c * half, half)


class _WeightGather:
    def __init__(self, ins, outs, sems):
        self.ins, self.outs = ins, outs
        send_sems, recv_sems, pass_send_sems, pass_recv_sems, self.local_sems = sems
        self.ici, self.d2d = (send_sems, recv_sems), (pass_send_sems, pass_recv_sems)
        self.x, self.y, self.c = _mesh_position()
        self.me = 2 * self.x + self.y
        self.sibling = (self.x, self.y, 1 - self.c)
        self.chips = _other_chips(self.x, self.y)

    def _copy(self, t, k, chip_index, core, to, sems, src=None):
        dst = self.outs[t].at[chip_index, :, _half_rows(self.ins[t].shape[1], core), :]
        return pltpu.make_async_remote_copy(
            src_ref=dst if src is None else src, dst_ref=dst, send_sem=sems[0].at[t, k], recv_sem=sems[1].at[t, k],
            device_id=to, device_id_type=MESH_ID,
        )

    def _own(self, t):
        return pltpu.make_async_copy(self.ins[t], self.outs[t].at[self.me], self.local_sems.at[t])

    def _sends(self):
        for t in range(len(self.ins)):
            mine = self.ins[t].at[:, _half_rows(self.ins[t].shape[1], self.c), :]
            for k, (px, py) in enumerate(self.chips):
                yield self._copy(t, k, self.me, self.c, (px, py, self.c), self.ici, src=mine)

    def _passes(self, core, sems):
        for t in range(len(self.ins)):
            for k, (px, py) in enumerate(self.chips):
                yield self._copy(t, k, 2 * px + py, core, self.sibling, sems)

    def begin(self):
        for t in range(len(self.ins)):
            self._own(t).start()
        for cp in self._sends():
            cp.start()

    def relay(self):
        for arrived, onward in zip(self._passes(self.c, self.ici), self._passes(self.c, self.d2d)):
            arrived.wait_recv()
            onward.start()

    def finish(self):
        for cp in self._passes(1 - self.c, self.d2d):
            cp.wait_recv()
        for cp in list(self._sends()) + list(self._passes(self.c, self.d2d)):
            cp.wait_send()
        for t in range(len(self.ins)):
            self._own(t).wait()


def _gather_scratch(n):
    sems = pltpu.SemaphoreType.DMA((n, N_CHIPS - 1))
    return [sems, sems, sems, sems, pltpu.SemaphoreType.DMA((n,))]


def _gather_weights(shards):
    n = len(shards)

    def body(*refs):
        gather = _WeightGather(refs[:n], refs[n : 2 * n], refs[2 * n :])
        gather.begin()
        gather.relay()
        gather.finish()

    return pl.pallas_call(
        body,
        name="gather_weights",
        in_specs=[ANY] * n,
        out_specs=[ANY] * n,
        out_shape=[jax.ShapeDtypeStruct((N_CHIPS,) + w.shape, w.dtype) for w in shards],
        scratch_shapes=_gather_scratch(n),
    )(*shards)


def _swap_halves(grads, tag):
    n = len(grads)

    def body(*refs):
        ins, outs = refs[:n], refs[n : 2 * n]
        send_sems, recv_sems = refs[2 * n :]
        x, y, c = _mesh_position()
        copies = []
        for t in range(n):
            copies.append(pltpu.make_async_remote_copy(
                src_ref=ins[t].at[:, _half_rows(ins[t].shape[1], 1 - c), :], dst_ref=outs[t],
                send_sem=send_sems.at[t], recv_sem=recv_sems.at[t], device_id=(x, y, 1 - c), device_id_type=MESH_ID,
            ))
            copies[-1].start()
        for cp in copies:
            cp.wait()

    sems = pltpu.SemaphoreType.DMA((n,))
    return pl.pallas_call(
        body,
        name=f"swap_halves_{tag}",
        in_specs=[ANY] * n,
        out_specs=[ANY] * n,
        out_shape=[jax.ShapeDtypeStruct((g.shape[0], g.shape[1] // 2, g.shape[2]), g.dtype) for g in grads],
        scratch_shapes=[sems, sems],
    )(*grads)


class _ChipScatter:
    def __init__(self, ins, outs, sems):
        self.ins, self.outs = ins, outs
        self.send_sems, self.recv_sems = sems
        self.x, self.y, self.c = _mesh_position()

    def _copies(self):
        for t in range(len(self.ins)):
            for k, (px, py) in enumerate(_other_chips(self.x, self.y)):
                yield pltpu.make_async_remote_copy(
                    src_ref=self.ins[t].at[2 * px + py], dst_ref=self.outs[t].at[k],
                    send_sem=self.send_sems.at[t, k], recv_sem=self.recv_sems.at[t, k],
                    device_id=(px, py, self.c), device_id_type=MESH_ID,
                )

    def begin(self):
        for cp in self._copies():
            cp.start()

    def finish(self):
        for cp in self._copies():
            cp.wait()


def _scatter_scratch(n):
    sems = pltpu.SemaphoreType.DMA((n, N_CHIPS - 1))
    return [sems, sems]


def _scatter_shapes(parts):
    return [jax.ShapeDtypeStruct((N_CHIPS - 1,) + p.shape[1:], p.dtype) for p in parts]


def _scatter_to_chips(parts, tag):
    n = len(parts)

    def body(*refs):
        copies = _ChipScatter(refs[:n], refs[n : 2 * n], refs[2 * n :])
        copies.begin()
        copies.finish()

    return pl.pallas_call(
        body,
        name=f"scatter_to_chips_{tag}",
        in_specs=[ANY] * n,
        out_specs=[ANY] * n,
        out_shape=_scatter_shapes(parts),
        scratch_shapes=_scatter_scratch(n),
    )(*parts)


def _join_halves(shards):
    n = len(shards)

    def body(*refs):
        outs = refs[n : 2 * n]
        send_sems, recv_sems = refs[2 * n :]
        x, y, c = _mesh_position()
        copies = []
        for t in range(n):
            mine = outs[t].at[:, _half_rows(outs[t].shape[1], c), :]
            copies.append(pltpu.make_async_remote_copy(
                src_ref=mine, dst_ref=mine, send_sem=send_sems.at[t], recv_sem=recv_sems.at[t],
                device_id=(x, y, 1 - c), device_id_type=MESH_ID,
            ))
            copies[-1].start()
        for cp in copies:
            cp.wait()

    sems = pltpu.SemaphoreType.DMA((n,))
    return pl.pallas_call(
        body,
        name="join_halves",
        in_specs=[ANY] * n,
        out_specs=[ANY] * n,
        out_shape=[jax.ShapeDtypeStruct(g.shape, g.dtype) for g in shards],
        input_output_aliases={t: t for t in range(n)},
        scratch_shapes=[sems, sems],
    )(*shards)


def _gather_small(pack):
    def body(p_ref, o_ref, send_sems, recv_sems, local_sem):
        x, y, c = _mesh_position()
        own = pltpu.make_async_copy(p_ref, o_ref.at[4 * x + 2 * y + c], local_sem)
        own.start()
        copies = []
        for k in range(1, N_DEV):
            px, py, pc = x ^ (k >> 2), y ^ ((k >> 1) & 1), c ^ (k & 1)
            send = pltpu.make_async_remote_copy(
                src_ref=p_ref, dst_ref=o_ref.at[4 * x + 2 * y + c], send_sem=send_sems.at[k - 1], recv_sem=recv_sems.at[k - 1],
                device_id=(px, py, pc), device_id_type=MESH_ID,
            )
            send.start()
            copies.append((send, 4 * px + 2 * py + pc))
        for send, peer_slot in copies:
            send.wait_send()
        for k in range(1, N_DEV):
            px, py, pc = x ^ (k >> 2), y ^ ((k >> 1) & 1), c ^ (k & 1)
            pltpu.make_async_remote_copy(
                src_ref=p_ref, dst_ref=o_ref.at[4 * px + 2 * py + pc], send_sem=send_sems.at[k - 1], recv_sem=recv_sems.at[k - 1],
                device_id=(px, py, pc), device_id_type=MESH_ID,
            ).wait_recv()
        own.wait()

    sems = pltpu.SemaphoreType.DMA((N_DEV - 1,))
    return pl.pallas_call(
        body,
        name="gather_small",
        in_specs=[VMEM_SPEC],
        out_specs=VMEM_SPEC,
        out_shape=jax.ShapeDtypeStruct((N_DEV,) + pack.shape, pack.dtype),
        scratch_shapes=[sems, sems, pltpu.SemaphoreType.DMA],
    )(pack)


def _row_tile(rows):
    for tile in range(min(rows, 512) // 8 * 8, 0, -8):
        if rows % tile == 0:
            return tile
    return rows


def _add_half(grad, received, half_index, name):
    slots, h, cdim = received.shape
    tile = _row_tile(h)
    per_half = h // tile

    def body(c_ref, g_ref, r_ref, o_ref, ob_ref):
        total = g_ref[...] + r_ref[...]
        o_ref[...] = total
        ob_ref[...] = total.astype(BF16)

    block = pl.BlockSpec((1, tile, cdim), lambda j, i, c: (j, i, 0))
    grid_spec = pltpu.PrefetchScalarGridSpec(
        num_scalar_prefetch=1,
        grid=(slots, per_half),
        in_specs=[pl.BlockSpec((1, tile, cdim), lambda j, i, c: (j, c[0] * per_half + i, 0)), block],
        out_specs=[block, block],
    )
    return pl.pallas_call(
        body, name=name, grid_spec=grid_spec,
        out_shape=[jax.ShapeDtypeStruct(received.shape, F32), jax.ShapeDtypeStruct(received.shape, BF16)],
        compiler_params=_params(2),
    )(half_index, grad, received)


def _add_chips(part, received, chip_index, core_index, layer, n_layers, shard, name):
    _, h, cdim = part.shape
    tile = _row_tile(h)
    per_half = h // tile

    def body(chip_ref, core_ref, p_ref, r_ref, *rest):
        o_ref = rest[-1]
        o_ref[0] = ((p_ref[0] + r_ref[0].astype(F32)) + r_ref[1].astype(F32)) + r_ref[2].astype(F32)

    in_specs = [
        pl.BlockSpec((1, tile, cdim), lambda i, chip, core: (chip[0], i, 0)),
        pl.BlockSpec((N_CHIPS - 1, tile, cdim), lambda i, chip, core: (0, i, 0)),
    ]
    operands = [chip_index, core_index, part, received]
    aliases = {}
    if shard is not None:
        in_specs.append(ANY)
        operands.append(shard)
        aliases = {4: 0}
    grid_spec = pltpu.PrefetchScalarGridSpec(
        num_scalar_prefetch=2,
        grid=(per_half,),
        in_specs=in_specs,
        out_specs=pl.BlockSpec((1, tile, cdim), lambda i, chip, core: (layer, core[0] * per_half + i, 0)),
    )
    return pl.pallas_call(
        body, name=name, grid_spec=grid_spec, out_shape=jax.ShapeDtypeStruct((n_layers, 2 * h, cdim), F32),
        input_output_aliases=aliases, compiler_params=_params(1),
    )(*operands)


def _adamw(w, g, m, v, name):
    rows, cdim = w.shape
    tile = _row_tile(rows)

    def body(w_ref, g_ref, m_ref, v_ref, d_ref, nm_ref, nv_ref):
        gv = g_ref[...]
        nm = ADAM_B1 * m_ref[...] + (1.0 - ADAM_B1) * gv
        nv = ADAM_B2 * v_ref[...] + (1.0 - ADAM_B2) * (gv * gv)
        m_hat = nm / (1.0 - ADAM_B1 ** ADAM_STEP)
        v_hat = nv / (1.0 - ADAM_B2 ** ADAM_STEP)
        d_ref[...] = -ADAM_LR * (m_hat / (jnp.sqrt(v_hat) + ADAM_EPS) + ADAM_WD * w_ref[...])
        nm_ref[...] = nm
        nv_ref[...] = nv

    spec = pl.BlockSpec((tile, cdim), lambda i: (i, 0))
    shape = jax.ShapeDtypeStruct((rows, cdim), F32)
    return pl.pallas_call(
        body, name=name, grid=(rows // tile,), in_specs=[spec] * 4, out_specs=[spec] * 3, out_shape=[shape] * 3,
        compiler_params=_params(1),
    )(w, g, m, v)


SMALL_ROWS, SMALL_COLS = 24, 1024
ROW_NORM_MIX, ROW_NORM_FFN, ROW_LOSS, ROW_Q_NORM, ROW_K_NORM, ROW_CONV = 0, 2, 4, 8, 10, 16


def _sum_small(gathered):
    def body(g_ref, o_ref, heads_ref, lanes_ref):
        total = g_ref[0]
        for dev in range(1, N_DEV):
            total = total + g_ref[dev]
        o_ref[...] = total
        heads = o_ref[8:16, 0:LANES]
        for grp in range(1, ATTN_DIM // LANES):
            heads = heads + o_ref[8:16, grp * LANES : (grp + 1) * LANES]
        heads_ref[...] = heads + pltpu.roll(heads, HEAD_DIM, 1)
        lanes_ref[...] = jnp.broadcast_to(jnp.sum(o_ref[0:8, :], axis=-1, keepdims=True), (8, LANES))

    return pl.pallas_call(
        body,
        name="sum_small",
        in_specs=[VMEM_SPEC],
        out_specs=[VMEM_SPEC] * 3,
        out_shape=[jax.ShapeDtypeStruct((SMALL_ROWS, SMALL_COLS), F32), jax.ShapeDtypeStruct((8, LANES), F32), jax.ShapeDtypeStruct((8, LANES), F32)],
    )(gathered)


def _pad_rows(a, rows):
    return jnp.pad(a, ((0, rows - a.shape[0]), (0, 0)))


def _pad_to(a, rows, cols):
    return jnp.pad(a, ((0, rows - a.shape[0]), (0, cols - a.shape[1])))


def _conv_taps(conv_s):
    return jnp.transpose(conv_s[:, 0, 0:8], (1, 0, 2)).reshape(8, -1)


class _GradExchange:
    def __init__(self, chip_index, core_index, n_layers):
        self.chip_index, self.core_index, self.n_layers = chip_index, core_index, n_layers
        self.shards = {}
        self.pending = None

    def offer(self, layer, grads):
        assert self.pending is None
        names = list(grads)
        received = _swap_halves([grads[k] for k in names], f"{'_'.join(names)}_{layer}")
        parts = [_add_half(grads[k], r, self.core_index, f"add_half_{k}_{layer}") for k, r in zip(names, received)]
        self.pending = (layer, names, [p32 for p32, _ in parts], [p16 for _, p16 in parts])

    def payload(self):
        return () if self.pending is None else tuple(self.pending[3])

    def take(self, received):
        layer, names, parts, _ = self.pending
        self.pending = None
        for k, p, r in zip(names, parts, received):
            self.shards[k] = _add_chips(
                p, r, self.chip_index, self.core_index, layer, self.n_layers, self.shards.get(k), f"add_chips_{k}_{layer}")

    def finish(self):
        if self.pending is not None:
            layer, names = self.pending[0], self.pending[1]
            self.take(_scatter_to_chips(list(self.pending[3]), f"{'_'.join(names)}_{layer}"))
        return dict(zip(BIG, _join_halves([self.shards[k] for k in BIG])))


def _local_step(x, target, norm_mix, q_norm, k_norm, norm_ffn, layer_weights, exchange=None):
    layer_weights = list(layer_weights)

    def carrying(kernel_fn, n_out, *args):
        if exchange is None or exchange.pending is None:
            return kernel_fn(*args)
        out = kernel_fn(*args, scatter=exchange.payload())
        exchange.take(out[n_out:])
        return out[:n_out]

    n_layers = norm_mix.shape[0]
    s, d = x.shape
    tw = min(WGRAD_TILE, s)
    n_in = layer_weights[0][0].shape[-1]
    f = layer_weights[0][2].shape[-1]
    saved = []
    for l in range(n_layers):
        weights = list(layer_weights[l])
        q_gain = jnp.tile(q_norm[l][None, :], (1, 2))
        k_gain = jnp.tile(k_norm[l][None, :], (1, 2))
        h1, proj, qn, kn, vb = _in_proj(x, norm_mix[l][None, :], weights[0], 0, q_gain, k_gain, f"in_proj_{l}")
        missing = [n for n, w in enumerate(weights) if w.ndim == 3]
        if missing:
            attn, *arrived = _attn_fwd(qn, kn, vb, f"attn_fwd_{l}", gather=tuple(weights[n] for n in missing))
            for n, w in zip(missing, arrived):
                weights[n] = w
            layer_weights[l] = tuple(weights)
        else:
            attn = _attn_fwd(qn, kn, vb, f"attn_fwd_{l}")
        _, wout_s, wg_s, wu_s, wd_s, conv_s = weights
        taps = _conv_taps(conv_s)
        x_mid, conv = _out_proj(x, attn, proj, taps, wout_s, 0, f"out_proj_{l}")
        pending = ()
        if l + 1 < n_layers and all(w.ndim == 3 for w in layer_weights[l + 1]):
            pending = tuple(layer_weights[l + 1])
        x_out, gate, up, *arrived = _ffn_fwd(x_mid, norm_ffn[l][None, :], wg_s, wu_s, wd_s, 0, f"ffn_fwd_{l}", gather=pending)
        if pending:
            layer_weights[l + 1] = tuple(arrived)
        saved.append(dict(x=x, h1=h1, proj=proj, qn=qn, kn=kn, vb=vb, attn=attn, conv=conv, x_mid=x_mid, q_gain=q_gain, k_gain=k_gain,
                          gate=gate, up=up, taps=taps))
        x = x_out

    dy, loss_lanes = _loss_grad(x, target, "loss_grad")
    grads = [None] * n_layers
    for l in reversed(range(n_layers)):
        sv = saved[l]
        win_s, wout_s, wg_s, wu_s, wd_s, _ = layer_weights[l]
        dx_mid, d_norm_ffn, h2, dyb, dgate, dup, act = carrying(
            _ffn_bwd, 7, sv["x_mid"], dy, norm_ffn[l][None, :], sv["gate"], sv["up"], wg_s, wu_s, wd_s, 0, f"ffn_bwd_{l}")
        tok2 = pl.BlockSpec((tw, d), lambda j, i: (i, 0))
        hid = pl.BlockSpec((1, tw, f), lambda j, i: (j, i, 0))
        d_wg = _wgrad(h2, dgate, tok2, hid, N_CHIPS, d, f, f"wgrad_gate_{l}")
        d_wu = _wgrad(h2, dup, tok2, hid, N_CHIPS, d, f, f"wgrad_up_{l}")
        d_wd = _wgrad(act, dyb, hid, tok2, N_CHIPS, f, d, f"wgrad_down_{l}")
        if exchange is not None:
            exchange.offer(l, dict(w_gate=d_wg, w_up=d_wu, w_down=d_wd))
        d_attn, d_conv, dxb = _out_proj_bwd(dx_mid, wout_s, 0, f"out_proj_bwd_{l}")
        rows_out = wout_s.shape[2]
        mix_spec_a = pl.BlockSpec((tw, rows_out), lambda j, i: (i, j))
        d_wout_a = _wgrad(sv["attn"], dxb, mix_spec_a, tok2, ATTN_DIM // rows_out, rows_out, d, f"wgrad_out_attn_{l}")
        d_wout_c = _wgrad(sv["conv"], dxb, mix_spec_a, tok2, CONV_DIM // rows_out, rows_out, d, f"wgrad_out_conv_{l}")
        d_wout = jnp.concatenate([d_wout_a, d_wout_c], axis=0)
        dq, dk, dv = carrying(_attn_bwd, 3, sv["qn"], sv["kn"], sv["vb"], d_attn, f"attn_bwd_{l}")
        dproj_conv, d_conv_w = _conv_bwd(sv["proj"], sv["taps"], d_conv, f"conv_bwd_{l}")
        dy, d_norm_mix, dproj, d_qg, d_kg = _in_proj_bwd(
            sv["x"], dx_mid, norm_mix[l][None, :], win_s, 0, sv["proj"], sv["q_gain"], sv["k_gain"], dq, dk, dv, dproj_conv, f"in_proj_bwd_{l}")
        d_win = _wgrad(sv["h1"], dproj, tok2, pl.BlockSpec((tw, n_in), lambda j, i: (i, j)), N_CHIPS, d, n_in, f"wgrad_in_{l}")
        if exchange is not None:
            exchange.offer(l, dict(w_in=d_win, w_out=d_wout))
        grads[l] = dict(norm_mix=d_norm_mix, norm_ffn=d_norm_ffn, q_norm=d_qg, k_norm=d_kg, conv_w=d_conv_w,
                        w_in=d_win, w_out=d_wout, w_gate=d_wg, w_up=d_wu, w_down=d_wd)
    return loss_lanes, dy, grads


BIG = ("w_in", "w_out", "w_gate", "w_up", "w_down")


def kernel(x, norm_mix, w_in, q_norm, k_norm, conv_w, w_out, norm_ffn, w_gate, w_up, w_down, loss_target, m_norm_mix, m_w_in, m_q_norm, m_k_norm, m_conv_w, m_w_out, m_norm_ffn, m_w_gate, m_w_up, m_w_down, v_norm_mix, v_w_in, v_q_norm, v_k_norm, v_conv_w, v_w_out, v_norm_ffn, v_w_gate, v_w_up, v_w_down):
    n_layers = norm_mix.shape[0]
    weights = dict(w_in=w_in, w_out=w_out, w_gate=w_gate, w_up=w_up, w_down=w_down)
    moments_m = dict(w_in=m_w_in, w_out=m_w_out, w_gate=m_w_gate, w_up=m_w_up, w_down=m_w_down)
    moments_v = dict(w_in=v_w_in, w_out=v_w_out, w_gate=v_w_gate, w_up=v_w_up, w_down=v_w_down)
    cx, cy, cc = _mesh_position()
    chip_index = (2 * cx + cy).astype(jnp.int32).reshape(1)
    core_index = cc.astype(jnp.int32).reshape(1)

    conv_pad = jnp.pad(conv_w, ((0, 0), (0, 16 - conv_w.shape[1]), (0, 0)))

    def shards_of(layer):
        return [weights[k][layer : layer + 1].astype(BF16) for k in BIG] + [conv_pad[layer : layer + 1]]

    first = shards_of(0)
    layer_weights = [tuple(_gather_weights(first[:1])) + tuple(first[1:])] + [tuple(shards_of(layer)) for layer in range(1, n_layers)]

    exchange = _GradExchange(chip_index, core_index, n_layers)
    loss_lanes, grad_x, grads = _local_step(
        x[0], loss_target[0], norm_mix, q_norm, k_norm, norm_ffn, layer_weights, exchange)

    big_grads = exchange.finish()

    def lanes(a):
        return _pad_to(a, a.shape[0], SMALL_COLS)

    def tile_of(*groups):
        return _pad_rows(jnp.concatenate([lanes(jnp.concatenate(g, axis=0)) for g in groups], axis=0), 8)

    layers = range(n_layers)
    pack = jnp.concatenate([
        tile_of([grads[l]["norm_mix"] for l in layers], [grads[l]["norm_ffn"] for l in layers], [loss_lanes]),
        tile_of([grads[l]["q_norm"] for l in layers], [grads[l]["k_norm"] for l in layers]),
        tile_of([grads[l]["conv_w"][0:3] for l in layers]),
    ], axis=0)
    small, small_heads, small_lanes = _sum_small(_gather_small(pack))
    loss = small_lanes[ROW_LOSS, 0]
    d_model = norm_mix.shape[1]
    conv_cols = conv_w.shape[2]
    conv_all = small[ROW_CONV : ROW_CONV + 3 * n_layers, 0:CONV_DIM].reshape(n_layers, 3, CONV_DIM)
    small_grads = dict(
        norm_mix=small[ROW_NORM_MIX : ROW_NORM_MIX + n_layers, 0:d_model],
        norm_ffn=small[ROW_NORM_FFN : ROW_NORM_FFN + n_layers, 0:d_model],
        q_norm=small_heads[ROW_Q_NORM - 8 : ROW_Q_NORM - 8 + n_layers, 0:HEAD_DIM],
        k_norm=small_heads[ROW_K_NORM - 8 : ROW_K_NORM - 8 + n_layers, 0:HEAD_DIM],
        conv_w=lax.dynamic_slice_in_dim(conv_all, (2 * cx + cy) * conv_cols, conv_cols, axis=2),
    )

    out_grad, out_delta, out_m, out_v = {}, {}, {}, {}
    for k in BIG:
        shape = weights[k].shape
        view = (shape[0] * shape[1], shape[2])
        g = big_grads[k]
        delta, new_m, new_v = _adamw(weights[k].reshape(view), g.reshape(view), moments_m[k].reshape(view), moments_v[k].reshape(view), f"adamw_{k}")
        out_grad[k], out_delta[k], out_m[k], out_v[k] = g, delta.reshape(shape), new_m.reshape(shape), new_v.reshape(shape)

    small_w = dict(norm_mix=norm_mix, norm_ffn=norm_ffn, q_norm=q_norm, k_norm=k_norm, conv_w=conv_w)
    small_m = dict(norm_mix=m_norm_mix, norm_ffn=m_norm_ffn, q_norm=m_q_norm, k_norm=m_k_norm, conv_w=m_conv_w)
    small_v = dict(norm_mix=v_norm_mix, norm_ffn=v_norm_ffn, q_norm=v_q_norm, k_norm=v_k_norm, conv_w=v_conv_w)
    order = ("norm_mix", "norm_ffn", "q_norm", "k_norm", "conv_w")

    def packed(tree):
        parts2 = [_pad_to(tree[k].reshape(-1, tree[k].shape[-1]), tree[k].reshape(-1, tree[k].shape[-1]).shape[0], SMALL_COLS) for k in order]
        return _pad_rows(jnp.concatenate(parts2, axis=0), SMALL_ROWS)

    delta_p, m_p, v_p = _adamw(packed(small_w), packed(small_grads), packed(small_m), packed(small_v), "adamw_small")
    row = 0
    for k in order:
        shape = small_w[k].shape
        n_rows = 1
        for dim in shape[:-1]:
            n_rows *= dim
        cut = (slice(row, row + n_rows), slice(0, shape[-1]))
        out_grad[k] = small_grads[k]
        out_delta[k], out_m[k], out_v[k] = delta_p[cut].reshape(shape), m_p[cut].reshape(shape), v_p[cut].reshape(shape)
        row += n_rows

    names_out = ("norm_mix", "w_in", "q_norm", "k_norm", "conv_w", "w_out", "norm_ffn", "w_gate", "w_up", "w_down")
    return (loss, grad_x[None], *[out_grad[k] for k in names_out], *[out_delta[k] for k in names_out],
            *[out_m[k] for k in names_out], *[out_v[k] for k in names_out])
```

```python
import functools

import jax
import jax.numpy as jnp
from jax import lax
from jax.experimental import pallas as pl
from jax.experimental.pallas import tpu as pltpu

F32 = jnp.float32
BF16 = jnp.bfloat16

EPS = 1e-6
HEAD_DIM = 64
LANES = 128
ATTN_DIM = 512
CONV_DIM = 512
N_CHIPS = 4
N_DEV = 8
Q_SCALE = HEAD_DIM ** -0.5
ATTN_Q_TILE = 256
ATTN_TILE = 256
TOKEN_TILE = 512
WGRAD_TILE = 4096
FFN_FWD_TILE = 1024
FFN_CHUNK = 256
VMEM_LIMIT = 56 * 1024 * 1024

ADAM_LR = 0.001
ADAM_B1 = 0.9
ADAM_B2 = 0.999
ADAM_EPS = 1e-08
ADAM_WD = 0.01
ADAM_STEP = 10

MESH_ID = pl.DeviceIdType.MESH
ANY = pl.BlockSpec(memory_space=pl.ANY)
VMEM_SPEC = pl.BlockSpec(memory_space=pltpu.VMEM)


def _params(n_axes):
    return pltpu.CompilerParams(dimension_semantics=("arbitrary",) * n_axes, vmem_limit_bytes=VMEM_LIMIT)


def _dot(a, b):
    return jnp.dot(a, b, preferred_element_type=F32)


def _dot_nt(a, b):
    return lax.dot_general(a, b, (((1,), (1,)), ((), ())), preferred_element_type=F32)


def _dot_tn(a, b):
    return lax.dot_general(a, b, (((0,), (0,)), ((), ())), preferred_element_type=F32)


SCORE_MAX = 80.0
UNDERFLOW_EXIT = 90.0


def _scores(q, k):
    return jnp.minimum(_dot_nt(q, k), SCORE_MAX)


def _softplus(z):
    return jnp.log(1.0 + jnp.exp(z))


def _head_norm(xv, gain, low):
    sq = xv * xv
    s_low = jnp.sum(jnp.where(low, sq, 0.0), axis=-1, keepdims=True)
    s_high = jnp.sum(jnp.where(low, 0.0, sq), axis=-1, keepdims=True)
    r = jnp.where(low, lax.rsqrt(s_low / HEAD_DIM + EPS), lax.rsqrt(s_high / HEAD_DIM + EPS))
    return xv * r * gain, r


def _in_proj(x, gain, w_s, layer, q_gain, k_gain, name):
    s, d = x.shape
    n_blocks, _, _, n = w_s.shape
    tm = TOKEN_TILE

    def body(x_ref, g_ref, w_ref, qg_ref, kg_ref, h_ref, o_ref, q_ref, k_ref, v_ref):
        xv = x_ref[...]
        r = lax.rsqrt(jnp.mean(xv * xv, axis=-1, keepdims=True) + EPS)
        h = (xv * r * g_ref[...]).astype(BF16)
        h_ref[...] = h
        for j in range(n_blocks):
            o_ref[:, j * n : (j + 1) * n] = _dot(h, w_ref[j, 0])
        low = lax.broadcasted_iota(jnp.int32, (tm, LANES), 1) < HEAD_DIM
        for g in range(ATTN_DIM // LANES):
            cq = slice(LANES * g, LANES * (g + 1))
            ck = slice(ATTN_DIM + LANES * g, ATTN_DIM + LANES * (g + 1))
            cv = slice(2 * ATTN_DIM + LANES * g, 2 * ATTN_DIM + LANES * (g + 1))
            qn, _ = _head_norm(o_ref[:, cq], qg_ref[...], low)
            kn, _ = _head_norm(o_ref[:, ck], kg_ref[...], low)
            q_ref[:, cq] = (qn * Q_SCALE).astype(BF16)
            k_ref[:, cq] = kn.astype(BF16)
            v_ref[:, cq] = o_ref[:, cv].astype(BF16)

    head_spec = pl.BlockSpec((tm, ATTN_DIM), lambda i: (i, 0))
    head_shape = jax.ShapeDtypeStruct((s, ATTN_DIM), BF16)
    gain_spec = pl.BlockSpec((1, LANES), lambda i: (0, 0))
    return pl.pallas_call(
        body,
        name=name,
        grid=(s // tm,),
        in_specs=[
            pl.BlockSpec((tm, d), lambda i: (i, 0)),
            pl.BlockSpec((1, d), lambda i: (0, 0)),
            pl.BlockSpec((n_blocks, 1, d, n), lambda i: (0, layer, 0, 0)),
            gain_spec, gain_spec,
        ],
        out_specs=[pl.BlockSpec((tm, d), lambda i: (i, 0)), pl.BlockSpec((tm, n_blocks * n), lambda i: (i, 0)), head_spec, head_spec, head_spec],
        out_shape=[jax.ShapeDtypeStruct((s, d), BF16), jax.ShapeDtypeStruct((s, n_blocks * n), F32), head_shape, head_shape, head_shape],
        compiler_params=_params(1),
    )(x, gain, w_s, q_gain, k_gain)


def _attn_tile_consts(t):
    row = lax.broadcasted_iota(jnp.int32, (t, t), 0)
    col = lax.broadcasted_iota(jnp.int32, (t, t), 1)
    return row, col


def _triangle_sum(v, triangle):
    return _dot(v.astype(BF16), triangle)


def _attn_fwd(qn, kn, vb, name, gather=()):
    s = qn.shape[0]
    t = min(ATTN_TILE, s)
    tq = min(ATTN_Q_TILE, t)
    per_key_tile = t // tq
    n_gather = len(gather)
    n_pairs, n_blocks = ATTN_DIM // LANES, s // tq

    def body(*refs):
        q_ref, k_ref, v_ref = refs[:3]
        o_ref = refs[3 + n_gather]
        if n_gather:
            copies = _WeightGather(refs[3 : 3 + n_gather], refs[4 + n_gather : 4 + 2 * n_gather], refs[4 + 2 * n_gather :])
            first = (pl.program_id(0) == 0) & (pl.program_id(1) == 0)
            pl.when(first)(copies.begin)
            pl.when((pl.program_id(0) == n_pairs - 1) & (pl.program_id(1) == 0))(copies.relay)
        i = pl.program_id(1) // per_key_tile
        low = lax.broadcasted_iota(jnp.int32, (tq, LANES), 1) < HEAD_DIM
        row, col = _attn_tile_consts(t)
        suffix = (row > col).astype(BF16)
        first_row = (pl.program_id(1) % per_key_tile) * tq
        causal = lax.broadcasted_iota(jnp.int32, (tq, t), 1) < lax.broadcasted_iota(jnp.int32, (tq, t), 0) + first_row
        q = q_ref[...]
        zero_q = jnp.zeros_like(q)
        qh = (jnp.where(low, q, zero_q), jnp.where(low, zero_q, q))

        def step(kbs, carry, diagonal_first=False):
            chains = [(head, m) for head in range(2) for m in range(len(kbs))]
            masked = [diagonal_first and m == 0 for _, m in chains]
            ks = [k_ref[pl.ds(pl.multiple_of(kb * t, t), t), :] for kb in kbs]
            vs = [v_ref[pl.ds(pl.multiple_of(kb * t, t), t), :] for kb in kbs]
            z = [_scores(qh[head], ks[kb]) for head, kb in chains]
            sp = [_softplus(zc) for zc in z]
            sp = [jnp.where(causal, s_, 0.0) if mk else s_ for s_, mk in zip(sp, masked)]
            inside = [_triangle_sum(s_, suffix) for s_ in sp]
            after = [carry[head][1] for head in range(2)]
            log_a = []
            for n, (head, kb) in enumerate(chains):
                log_a.append(z[n] - sp[n] - inside[n] - after[head])
                after[head] = after[head] + jnp.sum(sp[n], axis=-1, keepdims=True)
            a = [jnp.exp(l_) for l_ in log_a]
            a = [jnp.where(causal, a_, 0.0) if mk else a_ for a_, mk in zip(a, masked)]
            acc = [carry[head][0] for head in range(2)]
            for n, (head, kb) in enumerate(chains):
                acc[head] = acc[head] + _dot(a[n].astype(BF16), vs[kb])
            return tuple((acc[head], after[head]) for head in range(2))

        def live(c):
            return jnp.minimum(jnp.min(c[0][1]), jnp.min(c[1][1])) < UNDERFLOW_EXIT

        zero = (jnp.zeros((tq, LANES), F32), jnp.zeros((tq, 1), F32))
        start = lax.cond(i >= 1, lambda c: step((i, i - 1), c, True), lambda c: step((i,), c, True), (zero, zero))
        o_ref[...] = jnp.where(low, start[0][0], start[1][0]).astype(BF16)
        rest = jnp.maximum(i - 1, 0)

        @pl.when((rest > 0) & live(start))
        def _():
            carry = lax.cond(rest % 2 == 1, lambda c: step((i - 2,), c), lambda c: c, start)
            pairs = rest // 2
            _, carry = lax.while_loop(
                lambda st: (st[0] < pairs) & live(st[1]),
                lambda st: (st[0] + 1, step((2 * (pairs - st[0]) - 1, 2 * (pairs - st[0]) - 2), st[1])),
                (jnp.int32(0), carry))
            o_ref[...] = jnp.where(low, carry[0][0], carry[1][0]).astype(BF16)

        if n_gather:
            pl.when((pl.program_id(0) == n_pairs - 1) & (pl.program_id(1) == n_blocks - 1))(copies.finish)

    out = pl.pallas_call(
        body,
        name=name,
        grid=(n_pairs, n_blocks),
        in_specs=[
            pl.BlockSpec((tq, LANES), lambda p, i: (i, p)),
            pl.BlockSpec((s, LANES), lambda p, i: (0, p)),
            pl.BlockSpec((s, LANES), lambda p, i: (0, p)),
        ] + [ANY] * n_gather,
        out_specs=[pl.BlockSpec((tq, LANES), lambda p, i: (i, p))] + [ANY] * n_gather,
        out_shape=[jax.ShapeDtypeStruct((s, ATTN_DIM), BF16)] + [jax.ShapeDtypeStruct((N_CHIPS,) + w.shape, w.dtype) for w in gather],
        scratch_shapes=_gather_scratch(n_gather) if n_gather else [],
        compiler_params=_params(2),
    )(qn, kn, vb, *gather)
    return out if n_gather else out[0]


def _attn_bwd(qn, kn, vb, do, name, scatter=()):
    s = qn.shape[0]
    t = min(ATTN_TILE, s)
    nq = s // t
    n_scatter = len(scatter)
    n_pairs = ATTN_DIM // LANES

    def body(*refs):
        q_ref, k_ref, v_ref, do_ref = refs[:4]
        dq_ref, dk_ref, dv_ref = refs[4 + n_scatter : 7 + n_scatter]
        a_s, sg_s, a_f, sg_f = refs[7 + 2 * n_scatter : 11 + 2 * n_scatter]
        i = pl.program_id(1)
        if n_scatter:
            copies = _ChipScatter(refs[4 : 4 + n_scatter], refs[7 + n_scatter : 7 + 2 * n_scatter], refs[11 + 2 * n_scatter :])
            pl.when((pl.program_id(0) == 0) & (i == 0))(copies.begin)

        @pl.when(i == 0)
        def _():
            dk_ref[...] = jnp.zeros_like(dk_ref)
            dv_ref[...] = jnp.zeros_like(dv_ref)

        low = lax.broadcasted_iota(jnp.int32, (t, LANES), 1) < HEAD_DIM
        row, col = _attn_tile_consts(t)
        suffix = (row > col).astype(BF16)
        prefix = (row < col).astype(BF16)
        causal = col < row
        q = q_ref[...]
        dob = do_ref[...]
        zero_q = jnp.zeros_like(q)
        qhs = (jnp.where(low, q, zero_q), jnp.where(low, zero_q, q))
        dohs = (jnp.where(low, dob, zero_q), jnp.where(low, zero_q, dob))

        def rows_of(kb):
            return pl.ds(pl.multiple_of(kb * t, t), t)

        pair = [(head, m) for head in range(2) for m in range(2)]

        def short_pass1():
            z = [_scores(qhs[head], k_ref[rows_of(i - m), :]) for head, m in pair]
            sp = [_softplus(z_) for z_ in z]
            sp = [jnp.where(causal, s_, 0.0) if m == 0 else s_ for s_, (_, m) in zip(sp, pair)]
            inside = [_triangle_sum(s_, suffix) for s_ in sp]
            after = [jnp.zeros((t, 1), F32), jnp.zeros((t, 1), F32)]
            for n, (head, m) in enumerate(pair):
                log_sg = z[n] - sp[n]
                a = jnp.exp(log_sg - inside[n] - after[head])
                sg = jnp.exp(log_sg)
                if m == 0:
                    a = jnp.where(causal, a, 0.0)
                    sg = jnp.where(causal, sg, 0.0)
                a_f[n] = a
                sg_f[n] = sg
                after[head] = after[head] + jnp.sum(sp[n], axis=-1, keepdims=True)
            return jnp.minimum(jnp.min(after[0]), jnp.min(after[1])) >= UNDERFLOW_EXIT

        def short_pass2():
            order = [(head, m) for head in range(2) for m in (1, 0)]
            a = {c: a_f[pair.index(c)] for c in order}
            g = {c: a[c] * _dot_nt(dohs[c[0]], v_ref[rows_of(i - c[1]), :]) for c in order}
            for m in (1, 0):
                dv_ref[rows_of(i - m), :] += _dot_tn(a[(0, m)].astype(BF16), dohs[0]) + _dot_tn(a[(1, m)].astype(BF16), dohs[1])
            inside = {c: _triangle_sum(g[c], prefix) for c in order}
            before = [jnp.zeros((t, 1), F32), jnp.zeros((t, 1), F32)]
            dz = {}
            for c in order:
                sg = sg_f[pair.index(c)]
                dz[c] = (g[c] - sg * (g[c] + inside[c] + before[c[0]])).astype(BF16)
                before[c[0]] = before[c[0]] + jnp.sum(g[c], axis=-1, keepdims=True)
            for m in (1, 0):
                dk_ref[rows_of(i - m), :] += _dot_tn(dz[(0, m)], qhs[0]) + _dot_tn(dz[(1, m)], qhs[1])
            dq = [_dot(dz[(head, 1)], k_ref[rows_of(i - 1), :]) + _dot(dz[(head, 0)], k_ref[rows_of(i), :]) for head in range(2)]
            dq_ref[...] = jnp.where(low, dq[0], dq[1])

        def general_walk():
            heads = []
            for head in range(2):
                qh, doh = qhs[head], dohs[head]

                def pass1(kbs, after, diagonal_first=False):
                    z = [_scores(qh, k_ref[rows_of(kb), :]) for kb in kbs]
                    sp = [_softplus(z_) for z_ in z]
                    if diagonal_first:
                        sp[0] = jnp.where(causal, sp[0], 0.0)
                    inside = [_triangle_sum(s_, suffix) for s_ in sp]
                    for n, kb in enumerate(kbs):
                        log_sg = z[n] - sp[n]
                        a = jnp.exp(log_sg - inside[n] - after)
                        sg = jnp.exp(log_sg)
                        if diagonal_first and n == 0:
                            a = jnp.where(causal, a, 0.0)
                            sg = jnp.where(causal, sg, 0.0)
                        a_s[kb] = a
                        sg_s[kb] = sg
                        after = after + jnp.sum(sp[n], axis=-1, keepdims=True)
                    return after

                def live(after):
                    return jnp.min(after) < UNDERFLOW_EXIT

                after = jnp.zeros((t, 1), F32)
                after = lax.cond(i >= 1, lambda c: pass1((i, i - 1), c, True), lambda c: pass1((i,), c, True), after)
                rest = jnp.maximum(i - 1, 0)
                take_single = (rest % 2 == 1) & live(after)
                after = lax.cond(take_single, lambda c: pass1((i - 2,), c), lambda c: c, after)
                pairs = rest // 2
                pairs_done, _ = lax.while_loop(
                    lambda st: (st[0] < pairs) & live(st[1]),
                    lambda st: (st[0] + 1, pass1((2 * (pairs - st[0]) - 1, 2 * (pairs - st[0]) - 2), st[1])),
                    (jnp.int32(0), after))
                walked = jnp.minimum(i, 1) + 1 + take_single.astype(jnp.int32) + 2 * pairs_done
                first = i - walked + 1

                def pass2(kbs, carry):
                    dq, before = carry
                    ks = [k_ref[rows_of(kb), :] for kb in kbs]
                    a = [a_s[kb] for kb in kbs]
                    g = [a_ * _dot_nt(doh, v_ref[rows_of(kb), :]) for a_, kb in zip(a, kbs)]
                    for n, kb in enumerate(kbs):
                        dv_ref[rows_of(kb), :] += _dot_tn(a[n].astype(BF16), doh)
                    inside = [_triangle_sum(g_, prefix) for g_ in g]
                    dz = []
                    for n, kb in enumerate(kbs):
                        sg = sg_s[kb]
                        dz.append((g[n] - sg * (g[n] + inside[n] + before)).astype(BF16))
                        before = before + jnp.sum(g[n], axis=-1, keepdims=True)
                    for n, kb in enumerate(kbs):
                        dk_ref[rows_of(kb), :] += _dot_tn(dz[n], qh)
                    for n in range(len(kbs)):
                        dq = dq + _dot(dz[n], ks[n])
                    return dq, before

                carry = (jnp.zeros((t, LANES), F32), jnp.zeros((t, 1), F32))
                carry = lax.fori_loop(0, walked // 2, lambda n, c: pass2((first + 2 * n, first + 2 * n + 1), c), carry)
                carry = lax.cond(walked % 2 == 1, lambda c: pass2((i,), c), lambda c: c, carry)
                heads.append(carry[0])
            dq_ref[...] = jnp.where(low, heads[0], heads[1])

        short = lax.cond(i >= 1, short_pass1, lambda: jnp.bool_(False))
        pl.when(short)(short_pass2)
        pl.when(jnp.logical_not(short))(general_walk)
        if n_scatter:
            pl.when((pl.program_id(0) == n_pairs - 1) & (i == nq - 1))(copies.finish)

    q_spec = pl.BlockSpec((t, LANES), lambda p, i: (i, p))
    kv_spec = pl.BlockSpec((s, LANES), lambda p, i: (0, p))
    return pl.pallas_call(
        body,
        name=name,
        grid=(n_pairs, nq),
        in_specs=[q_spec, kv_spec, kv_spec, q_spec] + [ANY] * n_scatter,
        out_specs=[q_spec, kv_spec, kv_spec] + [ANY] * n_scatter,
        out_shape=[jax.ShapeDtypeStruct((s, ATTN_DIM), F32)] * 3 + _scatter_shapes(scatter),
        scratch_shapes=[pltpu.VMEM((nq, t, t), F32), pltpu.VMEM((nq, t, t), F32), pltpu.VMEM((4, t, t), F32), pltpu.VMEM((4, t, t), F32)]
        + (_scatter_scratch(n_scatter) if n_scatter else []),
        compiler_params=_params(2),
    )(qn, kn, vb, do, *scatter)


CB_BLOCK, CC_BLOCK, CU_BLOCK = 3, 4, 5


def _shift_down(h, prev_rows, n):
    row = lax.broadcasted_iota(jnp.int32, h.shape, 0)
    out = pltpu.roll(h, n, 0)
    for r in range(n):
        out = jnp.where(row == r, prev_rows[len(prev_rows) - n + r], out)
    return out


def _shift_up(h, next_rows, n):
    tm = h.shape[0]
    row = lax.broadcasted_iota(jnp.int32, h.shape, 0)
    out = pltpu.roll(h, tm - n, 0)
    for r in range(n):
        out = jnp.where(row == tm - n + r, next_rows[r], out)
    return out


def _conv_bwd(proj, conv_w, dconv, name):
    s = proj.shape[0]
    tm = TOKEN_TILE
    nb = tm // 8
    n_tiles = s // tm

    def body(cb_ref, cc_ref, cu_ref, dy_ref, pc_ref, pu_ref, nb_ref, ndy_ref, w_ref, dp_ref, dw_ref):
        i = pl.program_id(0)

        @pl.when(i == 0)
        def _():
            dw_ref[...] = jnp.zeros_like(dw_ref)

        first = i == 0
        last = i == n_tiles - 1
        cc, cu, cb, dy = cc_ref[...], cu_ref[...], cb_ref[...], dy_ref[...]
        h = cc * cu
        prev = [jnp.where(first, 0.0, pc_ref[r : r + 1, :] * pu_ref[r : r + 1, :]) for r in (6, 7)]
        h1 = _shift_down(h, prev, 1)
        h2 = _shift_down(h, prev, 2)
        y = w_ref[0:1, :] * h2 + w_ref[1:2, :] * h1 + w_ref[2:3, :] * h
        dyb = dy * cb
        nxt = [jnp.where(last, 0.0, ndy_ref[r : r + 1, :] * nb_ref[r : r + 1, :]) for r in (0, 1)]
        dh = w_ref[2:3, :] * dyb + w_ref[1:2, :] * _shift_up(dyb, nxt, 1) + w_ref[0:1, :] * _shift_up(dyb, nxt, 2)
        dp_ref[:, 0:CONV_DIM] = (dy * y).astype(BF16)
        dp_ref[:, CONV_DIM : 2 * CONV_DIM] = (dh * cu).astype(BF16)
        dp_ref[:, 2 * CONV_DIM : 3 * CONV_DIM] = (dh * cc).astype(BF16)
        dw_ref[0:1, :] += jnp.sum(dyb * h2, axis=0, keepdims=True)
        dw_ref[1:2, :] += jnp.sum(dyb * h1, axis=0, keepdims=True)
        dw_ref[2:3, :] += jnp.sum(dyb * h, axis=0, keepdims=True)

    def col(block):
        return pl.BlockSpec((tm, CONV_DIM), lambda i: (i, block))

    def halo_prev(block):
        return pl.BlockSpec((8, CONV_DIM), lambda i: (jnp.maximum(i * nb - 1, 0), block))

    def halo_next(block):
        return pl.BlockSpec((8, CONV_DIM), lambda i: (jnp.minimum((i + 1) * nb, s // 8 - 1), block))

    return pl.pallas_call(
        body,
        name=name,
        grid=(n_tiles,),
        in_specs=[
            col(CB_BLOCK), col(CC_BLOCK), col(CU_BLOCK), col(0),
            halo_prev(CC_BLOCK), halo_prev(CU_BLOCK), halo_next(CB_BLOCK), halo_next(0),
            pl.BlockSpec((8, CONV_DIM), lambda i: (0, 0)),
        ],
        out_specs=[pl.BlockSpec((tm, 3 * CONV_DIM), lambda i: (i, 1)), pl.BlockSpec((8, CONV_DIM), lambda i: (0, 0))],
        out_shape=[jax.ShapeDtypeStruct((s, 3 * ATTN_DIM + 3 * CONV_DIM), BF16), jax.ShapeDtypeStruct((8, CONV_DIM), F32)],
        compiler_params=_params(1),
    )(proj, proj, proj, dconv, proj, proj, proj, dconv, conv_w)


def _out_proj(x, attn, proj, conv_w, w_s, layer, name):
    s, d = x.shape
    tm = TOKEN_TILE
    nb = tm // 8
    rows = w_s.shape[2]

    def body(x_ref, a_ref, cb_ref, cc_ref, cu_ref, pc_ref, pu_ref, cw_ref, w_ref, o_ref, c_ref):
        first = pl.program_id(0) == 0
        h = cc_ref[...] * cu_ref[...]
        prev = [jnp.where(first, 0.0, pc_ref[r : r + 1, :] * pu_ref[r : r + 1, :]) for r in (6, 7)]
        y = cw_ref[0:1, :] * _shift_down(h, prev, 2) + cw_ref[1:2, :] * _shift_down(h, prev, 1) + cw_ref[2:3, :] * h
        c_ref[...] = (cb_ref[...] * y).astype(BF16)
        acc = x_ref[...]
        for j in range(N_CHIPS):
            src = a_ref if j < 2 else c_ref
            cols = slice((j % 2) * rows, (j % 2 + 1) * rows)
            acc = acc + _dot(src[:, cols], w_ref[j, 0])
        o_ref[...] = acc

    def col(block):
        return pl.BlockSpec((tm, CONV_DIM), lambda i: (i, block))

    def halo(block):
        return pl.BlockSpec((8, CONV_DIM), lambda i: (jnp.maximum(i * nb - 1, 0), block))

    return pl.pallas_call(
        body,
        name=name,
        grid=(s // tm,),
        in_specs=[
            pl.BlockSpec((tm, d), lambda i: (i, 0)),
            pl.BlockSpec((tm, ATTN_DIM), lambda i: (i, 0)),
            col(CB_BLOCK), col(CC_BLOCK), col(CU_BLOCK), halo(CC_BLOCK), halo(CU_BLOCK),
            pl.BlockSpec((8, CONV_DIM), lambda i: (0, 0)),
            pl.BlockSpec((N_CHIPS, 1, rows, d), lambda i: (0, layer, 0, 0)),
        ],
        out_specs=[pl.BlockSpec((tm, d), lambda i: (i, 0)), pl.BlockSpec((tm, CONV_DIM), lambda i: (i, 0))],
        out_shape=[jax.ShapeDtypeStruct((s, d), F32), jax.ShapeDtypeStruct((s, CONV_DIM), BF16)],
        compiler_params=_params(1),
    )(x, attn, proj, proj, proj, proj, proj, conv_w, w_s)


def _out_proj_bwd(dx, w_s, layer, name):
    s, d = dx.shape
    tm = TOKEN_TILE
    rows = w_s.shape[2]

    def body(dx_ref, w_ref, da_ref, dc_ref, dxb_ref):
        dxb = dx_ref[...].astype(BF16)
        dxb_ref[...] = dxb
        for j in range(N_CHIPS):
            cols = slice((j % 2) * rows, (j % 2 + 1) * rows)
            part = _dot_nt(dxb, w_ref[j, 0])
            if j < 2:
                da_ref[:, cols] = part.astype(BF16)
            else:
                dc_ref[:, cols] = part

    return pl.pallas_call(
        body,
        name=name,
        grid=(s // tm,),
        in_specs=[pl.BlockSpec((tm, d), lambda i: (i, 0)), pl.BlockSpec((N_CHIPS, 1, rows, d), lambda i: (0, layer, 0, 0))],
        out_specs=[
            pl.BlockSpec((tm, ATTN_DIM), lambda i: (i, 0)),
            pl.BlockSpec((tm, CONV_DIM), lambda i: (i, 0)),
            pl.BlockSpec((tm, d), lambda i: (i, 0)),
        ],
        out_shape=[
            jax.ShapeDtypeStruct((s, ATTN_DIM), BF16),
            jax.ShapeDtypeStruct((s, CONV_DIM), F32),
            jax.ShapeDtypeStruct((s, d), BF16),
        ],
        compiler_params=_params(1),
    )(dx, w_s)


def _ffn_fwd(x, gain, wg_s, wu_s, wd_s, layer, name, gather=(), target=None):
    s, d = x.shape
    tm = min(FFN_FWD_TILE, s)
    n_loss = 0 if target is None else 1
    f = wg_s.shape[3]
    n_gather = len(gather)
    n_tiles = s // tm

    def body(*refs):
        x_ref, g_ref, wg_ref, wu_ref, wd_ref = refs[:5]
        n_in = 5 + n_gather + n_loss
        o_ref, gate_ref, up_ref = refs[n_in : n_in + 3]
        h_s = refs[n_in + 3 + n_gather + 2 * n_loss]
        i, j = pl.program_id(0), pl.program_id(1)
        if n_gather:
            copies = _WeightGather(refs[5 : 5 + n_gather], refs[n_in + 3 : n_in + 3 + n_gather], refs[n_in + 4 + n_gather + 2 * n_loss :])
            pl.when((i == 0) & (j == 0))(copies.begin)
            pl.when((i == (3 * n_tiles) // 4) & (j == 0))(copies.relay)

        @pl.when(j == 0)
        def _():
            xv = x_ref[...]
            r = lax.rsqrt(jnp.mean(xv * xv, axis=-1, keepdims=True) + EPS)
            h_s[...] = (xv * r * g_ref[...]).astype(BF16)
            o_ref[...] = xv

        halves = [slice(r, r + FFN_CHUNK) for r in range(0, tm, FFN_CHUNK)]
        pre = [(_dot(h_s[r, :], wg_ref[0, 0]), _dot(h_s[r, :], wu_ref[0, 0])) for r in halves]
        act = [((gate / (1.0 + jnp.exp(-gate))) * up).astype(BF16) for gate, up in pre]
        for r, (gate, up) in zip(halves, pre):
            gate_ref[0, r, :] = gate.astype(BF16)
            up_ref[0, r, :] = up.astype(BF16)
        for r, a in zip(halves, act):
            o_ref[r, :] += _dot(a, wd_ref[0, 0])

        if n_loss:
            t_ref = refs[5 + n_gather]
            dy_ref, l_ref = refs[n_in + 3 + n_gather : n_in + 5 + n_gather]

            @pl.when((i == 0) & (j == 0))
            def _():
                l_ref[...] = jnp.zeros_like(l_ref)

            @pl.when(j == N_CHIPS - 1)
            def _():
                err = o_ref[...] - t_ref[...]
                dy_ref[...] = err / d
                l_ref[...] += jnp.sum(err * err, axis=0, keepdims=True) * (0.5 / d)

        if n_gather:
            pl.when((i == n_tiles - 1) & (j == N_CHIPS - 1))(copies.finish)

    tok = pl.BlockSpec((tm, d), lambda i, j: (i, 0))
    hid = pl.BlockSpec((1, tm, f), lambda i, j: (j, i, 0))
    hid_shape = jax.ShapeDtypeStruct((N_CHIPS, s, f), BF16)
    loss_specs = [tok, pl.BlockSpec((1, d), lambda i, j: (0, 0))] if n_loss else []
    loss_shapes = [jax.ShapeDtypeStruct((s, d), F32), jax.ShapeDtypeStruct((1, d), F32)] if n_loss else []
    return pl.pallas_call(
        body,
        name=name,
        grid=(n_tiles, N_CHIPS),
        in_specs=[
            pl.BlockSpec((tm, d), lambda i, j: (i, 0)),
            pl.BlockSpec((1, d), lambda i, j: (0, 0)),
            pl.BlockSpec((1, 1, d, f), lambda i, j: (j, layer, 0, 0)),
            pl.BlockSpec((1, 1, d, f), lambda i, j: (j, layer, 0, 0)),
            pl.BlockSpec((1, 1, f, d), lambda i, j: (j, layer, 0, 0)),
        ] + [ANY] * n_gather + ([tok] if n_loss else []),
        out_specs=[tok, hid, hid] + [ANY] * n_gather + loss_specs,
        out_shape=[jax.ShapeDtypeStruct((s, d), F32), hid_shape, hid_shape]
        + [jax.ShapeDtypeStruct((N_CHIPS,) + w.shape, w.dtype) for w in gather] + loss_shapes,
        scratch_shapes=[pltpu.VMEM((tm, d), BF16)] + (_gather_scratch(n_gather) if n_gather else []),
        compiler_params=_params(2),
    )(x, gain, wg_s, wu_s, wd_s, *gather, *([target] if n_loss else []))


def _resident(block, layer):
    return pl.BlockSpec(block, lambda i, j: (0, layer, 0, 0), pipeline_mode=pl.Buffered(1))


def _rms_bwd(xv, gain, dh):
    r = lax.rsqrt(jnp.mean(xv * xv, axis=-1, keepdims=True) + EPS)
    xhat = xv * r
    dxhat = dh * gain
    dx = r * (dxhat - xhat * jnp.mean(dxhat * xhat, axis=-1, keepdims=True))
    return dx, jnp.sum(dh * xhat, axis=0, keepdims=True)


def _ffn_bwd(x, dy, gain, gate_s, up_s, wg_s, wu_s, wd_s, layer, name, scatter=()):
    s, d = x.shape
    tm = TOKEN_TILE
    f = wg_s.shape[3]

    n_scatter = len(scatter)
    n_tiles = s // tm

    def body(*refs):
        x_ref, dy_ref, g_ref, gate_ref, up_ref, wg_ref, wu_ref, wd_ref = refs[:8]
        dx_ref, dgain_ref, h_ref, dyb_ref, dg_ref, du_ref, act_ref = refs[8 + n_scatter : 15 + n_scatter]
        acc_s = refs[15 + 2 * n_scatter]
        i, j = pl.program_id(0), pl.program_id(1)
        if n_scatter:
            copies = _ChipScatter(refs[8 : 8 + n_scatter], refs[15 + n_scatter : 15 + 2 * n_scatter], refs[16 + 2 * n_scatter :])
            pl.when((i == 0) & (j == 0))(copies.begin)

        @pl.when((i == 0) & (j == 0))
        def _():
            dgain_ref[...] = jnp.zeros_like(dgain_ref)

        @pl.when(j == 0)
        def _():
            xv = x_ref[...]
            r = lax.rsqrt(jnp.mean(xv * xv, axis=-1, keepdims=True) + EPS)
            h_ref[...] = (xv * r * g_ref[...]).astype(BF16)
            dyb_ref[...] = dy_ref[...].astype(BF16)
            acc_s[...] = jnp.zeros_like(acc_s)

        halves = [slice(0, tm // 2), slice(tm // 2, tm)]
        pre = [(gate_ref[0, r, :].astype(F32), up_ref[0, r, :].astype(F32), _dot_nt(dyb_ref[r, :], wd_ref[j, 0])) for r in halves]
        grads = []
        for r, (gate, up, dact) in zip(halves, pre):
            sig = 1.0 / (1.0 + jnp.exp(-gate))
            silu = gate * sig
            dgate = (dact * up * (sig * (1.0 + gate * (1.0 - sig)))).astype(BF16)
            dup = (dact * silu).astype(BF16)
            act_ref[0, r, :] = (silu * up).astype(BF16)
            dg_ref[0, r, :] = dgate
            du_ref[0, r, :] = dup
            grads.append((dgate, dup))
        for r, (dgate, dup) in zip(halves, grads):
            acc_s[r, :] += _dot_nt(dgate, wg_ref[j, 0]) + _dot_nt(dup, wu_ref[j, 0])

        @pl.when(j == N_CHIPS - 1)
        def _():
            dxn, dgain = _rms_bwd(x_ref[...], g_ref[...], acc_s[...])
            dx_ref[...] = dy_ref[...] + dxn
            dgain_ref[...] += dgain

        if n_scatter:
            pl.when((i == n_tiles - 1) & (j == N_CHIPS - 1))(copies.finish)

    tok = pl.BlockSpec((tm, d), lambda i, j: (i, 0))
    vec = pl.BlockSpec((1, d), lambda i, j: (0, 0))
    hid = pl.BlockSpec((1, tm, f), lambda i, j: (j, i, 0))
    hid_shape = jax.ShapeDtypeStruct((N_CHIPS, s, f), BF16)
    return pl.pallas_call(
        body,
        name=name,
        grid=(n_tiles, N_CHIPS),
        in_specs=[
            tok, tok, vec, hid, hid,
            _resident((N_CHIPS, 1, d, f), layer),
            _resident((N_CHIPS, 1, d, f), layer),
            _resident((N_CHIPS, 1, f, d), layer),
        ] + [ANY] * n_scatter,
        out_specs=[tok, vec, tok, tok, hid, hid, hid] + [ANY] * n_scatter,
        out_shape=[
            jax.ShapeDtypeStruct((s, d), F32),
            jax.ShapeDtypeStruct((1, d), F32),
            jax.ShapeDtypeStruct((s, d), BF16),
            jax.ShapeDtypeStruct((s, d), BF16),
            hid_shape, hid_shape, hid_shape,
        ] + _scatter_shapes(scatter),
        scratch_shapes=[pltpu.VMEM((tm, d), F32)] + (_scatter_scratch(n_scatter) if n_scatter else []),
        compiler_params=_params(2),
    )(x, dy, gain, gate_s, up_s, wg_s, wu_s, wd_s, *scatter)


def _in_proj_bwd(x, dx_res, gain, w_s, layer, proj, q_gain, k_gain, dq, dk, dv, dproj_conv, name):
    s, d = x.shape
    tm = TOKEN_TILE
    n = w_s.shape[3]
    qkv = 3 * ATTN_DIM

    def norm_bwd(xv, head_gain, dy, low):
        _, r = _head_norm(xv, head_gain, low)
        xhat = xv * r
        dxhat = dy * head_gain
        prod = dxhat * xhat
        m_low = jnp.sum(jnp.where(low, prod, 0.0), axis=-1, keepdims=True)
        m_high = jnp.sum(jnp.where(low, 0.0, prod), axis=-1, keepdims=True)
        mean = jnp.where(low, m_low, m_high) / HEAD_DIM
        return r * (dxhat - xhat * mean), jnp.sum(dy * xhat, axis=0, keepdims=True)

    def body(x_ref, r_ref, g_ref, w_ref, p_ref, qg_ref, kg_ref, dq_ref, dk_ref, dv_ref, dpc_ref, dx_ref, dgain_ref, dp_ref, dqg_ref, dkg_ref):
        @pl.when(pl.program_id(0) == 0)
        def _():
            dgain_ref[...] = jnp.zeros_like(dgain_ref)
            dqg_ref[...] = jnp.zeros_like(dqg_ref)
            dkg_ref[...] = jnp.zeros_like(dkg_ref)

        low = lax.broadcasted_iota(jnp.int32, (tm, LANES), 1) < HEAD_DIM
        for g in range(ATTN_DIM // LANES):
            cq = slice(LANES * g, LANES * (g + 1))
            ck = slice(ATTN_DIM + LANES * g, ATTN_DIM + LANES * (g + 1))
            cv = slice(2 * ATTN_DIM + LANES * g, 2 * ATTN_DIM + LANES * (g + 1))
            dxq, dgq = norm_bwd(p_ref[:, cq], qg_ref[...], dq_ref[:, cq] * Q_SCALE, low)
            dxk, dgk = norm_bwd(p_ref[:, ck], kg_ref[...], dk_ref[:, cq], low)
            dp_ref[:, cq] = dxq.astype(BF16)
            dp_ref[:, ck] = dxk.astype(BF16)
            dp_ref[:, cv] = dv_ref[:, cq].astype(BF16)
            dqg_ref[:, cq] += dgq
            dkg_ref[:, cq] += dgk
        dp_ref[:, qkv:] = dpc_ref[...]

        dh = _dot_nt(dp_ref[:, 0:n], w_ref[0, 0])
        for j in range(1, N_CHIPS):
            dh = dh + _dot_nt(dp_ref[:, j * n : (j + 1) * n], w_ref[j, 0])
        dxn, dgain = _rms_bwd(x_ref[...], g_ref[...], dh)
        dx_ref[...] = r_ref[...] + dxn
        dgain_ref[...] += dgain

    tok = pl.BlockSpec((tm, d), lambda i: (i, 0))
    vec = pl.BlockSpec((1, d), lambda i: (0, 0))
    grad_spec = pl.BlockSpec((tm, ATTN_DIM), lambda i: (i, 0))
    gain_spec = pl.BlockSpec((1, LANES), lambda i: (0, 0))
    sum_spec = pl.BlockSpec((1, ATTN_DIM), lambda i: (0, 0))
    return pl.pallas_call(
        body,
        name=name,
        grid=(s // tm,),
        in_specs=[
            tok, tok, vec, pl.BlockSpec((N_CHIPS, 1, d, n), lambda i: (0, layer, 0, 0)),
            pl.BlockSpec((tm, qkv), lambda i: (i, 0)), gain_spec, gain_spec, grad_spec, grad_spec, grad_spec,
            pl.BlockSpec((tm, N_CHIPS * n - qkv), lambda i: (i, 1)),
        ],
        out_specs=[tok, vec, pl.BlockSpec((tm, N_CHIPS * n), lambda i: (i, 0)), sum_spec, sum_spec],
        out_shape=[
            jax.ShapeDtypeStruct((s, d), F32),
            jax.ShapeDtypeStruct((1, d), F32),
            jax.ShapeDtypeStruct((s, N_CHIPS * n), BF16),
            jax.ShapeDtypeStruct((1, ATTN_DIM), F32),
            jax.ShapeDtypeStruct((1, ATTN_DIM), F32),
        ],
        compiler_params=_params(1),
    )(x, dx_res, gain, w_s, proj, q_gain, k_gain, dq, dk, dv, dproj_conv)


def _wgrad(a, b, a_spec, b_spec, n_blocks, k, n, name):
    n_tiles = a.shape[-2] // min(WGRAD_TILE, a.shape[-2])

    def body(a_ref, b_ref, o_ref):
        @pl.when(pl.program_id(1) == 0)
        def _():
            o_ref[...] = jnp.zeros_like(o_ref)

        av = a_ref[0] if len(a_ref.shape) == 3 else a_ref[...]
        bv = b_ref[0] if len(b_ref.shape) == 3 else b_ref[...]
        o_ref[0] += _dot_tn(av, bv)

    return pl.pallas_call(
        body,
        name=name,
        grid=(n_blocks, n_tiles),
        in_specs=[a_spec, b_spec],
        out_specs=pl.BlockSpec((1, k, n), lambda j, i: (j, 0, 0)),
        out_shape=jax.ShapeDtypeStruct((n_blocks, k, n), F32),
        compiler_params=_params(2),
    )(a, b)


def _wgrad_out(attn, conv, dxb, rows, name):
    s, d = dxb.shape
    tw = min(WGRAD_TILE // 2, s)

    def body(a_ref, c_ref, b_ref, o_ref):
        @pl.when(pl.program_id(0) == 0)
        def _():
            o_ref[...] = jnp.zeros_like(o_ref)

        bv = b_ref[...]
        for j in range(N_CHIPS):
            src = a_ref if j < 2 else c_ref
            o_ref[j] += _dot_tn(src[:, (j % 2) * rows : (j % 2 + 1) * rows], bv)

    return pl.pallas_call(
        body,
        name=name,
        grid=(s // tw,),
        in_specs=[
            pl.BlockSpec((tw, ATTN_DIM), lambda i: (i, 0)),
            pl.BlockSpec((tw, CONV_DIM), lambda i: (i, 0)),
            pl.BlockSpec((tw, d), lambda i: (i, 0)),
        ],
        out_specs=pl.BlockSpec((N_CHIPS, rows, d), lambda i: (0, 0, 0)),
        out_shape=jax.ShapeDtypeStruct((N_CHIPS, rows, d), F32),
        compiler_params=_params(1),
    )(attn, conv, dxb)


def _mesh_position():
    return lax.axis_index("x"), lax.axis_index("y"), lax.axis_index("c")


def _other_chips(x, y):
    return [(1 - x, y), (x, 1 - y), (1 - x, 1 - y)]


def _half_rows(ref_rows, c):
    half = ref_rows // 2
    return pl.ds(c * half, half)


class _WeightGather:
    def __init__(self, ins, outs, sems):
        self.ins, self.outs = ins, outs
        send_sems, recv_sems, pass_send_sems, pass_recv_sems, self.local_sems = sems
        self.ici, self.d2d = (send_sems, recv_sems), (pass_send_sems, pass_recv_sems)
        self.x, self.y, self.c = _mesh_position()
        self.me = 2 * self.x + self.y
        self.sibling = (self.x, self.y, 1 - self.c)
        self.chips = _other_chips(self.x, self.y)

    def _copy(self, t, k, chip_index, core, to, sems, src=None):
        dst = self.outs[t].at[chip_index, :, _half_rows(self.ins[t].shape[1], core), :]
        return pltpu.make_async_remote_copy(
            src_ref=dst if src is None else src, dst_ref=dst, send_sem=sems[0].at[t, k], recv_sem=sems[1].at[t, k],
            device_id=to, device_id_type=MESH_ID,
        )

    def _own(self, t):
        return pltpu.make_async_copy(self.ins[t], self.outs[t].at[self.me], self.local_sems.at[t])

    def _sends(self):
        for t in range(len(self.ins)):
            mine = self.ins[t].at[:, _half_rows(self.ins[t].shape[1], self.c), :]
            for k, (px, py) in enumerate(self.chips):
                yield self._copy(t, k, self.me, self.c, (px, py, self.c), self.ici, src=mine)

    def _passes(self, core, sems):
        for t in range(len(self.ins)):
            for k, (px, py) in enumerate(self.chips):
                yield self._copy(t, k, 2 * px + py, core, self.sibling, sems)

    def begin(self):
        for t in range(len(self.ins)):
            self._own(t).start()
        for cp in self._sends():
            cp.start()

    def relay(self):
        for arrived, onward in zip(self._passes(self.c, self.ici), self._passes(self.c, self.d2d)):
            arrived.wait_recv()
            onward.start()

    def finish(self):
        for cp in self._passes(1 - self.c, self.d2d):
            cp.wait_recv()
        for cp in list(self._sends()) + list(self._passes(self.c, self.d2d)):
            cp.wait_send()
        for t in range(len(self.ins)):
            self._own(t).wait()


def _gather_scratch(n):
    sems = pltpu.SemaphoreType.DMA((n, N_CHIPS - 1))
    return [sems, sems, sems, sems, pltpu.SemaphoreType.DMA((n,))]


def _gather_weights(shards):
    n = len(shards)

    def body(*refs):
        gather = _WeightGather(refs[:n], refs[n : 2 * n], refs[2 * n :])
        gather.begin()
        gather.relay()
        gather.finish()

    return pl.pallas_call(
        body,
        name="gather_weights",
        in_specs=[ANY] * n,
        out_specs=[ANY] * n,
        out_shape=[jax.ShapeDtypeStruct((N_CHIPS,) + w.shape, w.dtype) for w in shards],
        scratch_shapes=_gather_scratch(n),
    )(*shards)


def _swap_halves(grads, tag):
    n = len(grads)

    def body(*refs):
        ins, outs = refs[:n], refs[n : 2 * n]
        send_sems, recv_sems = refs[2 * n :]
        x, y, c = _mesh_position()
        copies = []
        for t in range(n):
            copies.append(pltpu.make_async_remote_copy(
                src_ref=ins[t].at[:, _half_rows(ins[t].shape[1], 1 - c), :], dst_ref=outs[t],
                send_sem=send_sems.at[t], recv_sem=recv_sems.at[t], device_id=(x, y, 1 - c), device_id_type=MESH_ID,
            ))
            copies[-1].start()
        for cp in copies:
            cp.wait()

    sems = pltpu.SemaphoreType.DMA((n,))
    return pl.pallas_call(
        body,
        name=f"swap_halves_{tag}",
        in_specs=[ANY] * n,
        out_specs=[ANY] * n,
        out_shape=[jax.ShapeDtypeStruct((g.shape[0], g.shape[1] // 2, g.shape[2]), g.dtype) for g in grads],
        scratch_shapes=[sems, sems],
    )(*grads)


class _ChipScatter:
    def __init__(self, ins, outs, sems):
        self.ins, self.outs = ins, outs
        self.send_sems, self.recv_sems = sems
        self.x, self.y, self.c = _mesh_position()

    def _copies(self):
        for t in range(len(self.ins)):
            for k, (px, py) in enumerate(_other_chips(self.x, self.y)):
                yield pltpu.make_async_remote_copy(
                    src_ref=self.ins[t].at[2 * px + py], dst_ref=self.outs[t].at[k],
                    send_sem=self.send_sems.at[t, k], recv_sem=self.recv_sems.at[t, k],
                    device_id=(px, py, self.c), device_id_type=MESH_ID,
                )

    def begin(self):
        for cp in self._copies():
            cp.start()

    def finish(self):
        for cp in self._copies():
            cp.wait()


def _scatter_scratch(n):
    sems = pltpu.SemaphoreType.DMA((n, N_CHIPS - 1))
    return [sems, sems]


def _scatter_shapes(parts):
    return [jax.ShapeDtypeStruct((N_CHIPS - 1,) + p.shape[1:], p.dtype) for p in parts]


def _scatter_to_chips(parts, tag):
    n = len(parts)

    def body(*refs):
        copies = _ChipScatter(refs[:n], refs[n : 2 * n], refs[2 * n :])
        copies.begin()
        copies.finish()

    return pl.pallas_call(
        body,
        name=f"scatter_to_chips_{tag}",
        in_specs=[ANY] * n,
        out_specs=[ANY] * n,
        out_shape=_scatter_shapes(parts),
        scratch_shapes=_scatter_scratch(n),
    )(*parts)


def _join_halves(shards):
    n = len(shards)

    def body(*refs):
        outs = refs[n : 2 * n]
        send_sems, recv_sems = refs[2 * n :]
        x, y, c = _mesh_position()
        copies = []
        for t in range(n):
            mine = outs[t].at[:, _half_rows(outs[t].shape[1], c), :]
            copies.append(pltpu.make_async_remote_copy(
                src_ref=mine, dst_ref=mine, send_sem=send_sems.at[t], recv_sem=recv_sems.at[t],
                device_id=(x, y, 1 - c), device_id_type=MESH_ID,
            ))
            copies[-1].start()
        for cp in copies:
            cp.wait()

    sems = pltpu.SemaphoreType.DMA((n,))
    return pl.pallas_call(
        body,
        name="join_halves",
        in_specs=[ANY] * n,
        out_specs=[ANY] * n,
        out_shape=[jax.ShapeDtypeStruct(g.shape, g.dtype) for g in shards],
        input_output_aliases={t: t for t in range(n)},
        scratch_shapes=[sems, sems],
    )(*shards)


def _gather_small(pack):
    def body(p_ref, o_ref, send_sems, recv_sems, local_sem):
        x, y, c = _mesh_position()
        own = pltpu.make_async_copy(p_ref, o_ref.at[4 * x + 2 * y + c], local_sem)
        own.start()
        copies = []
        for k in range(1, N_DEV):
            px, py, pc = x ^ (k >> 2), y ^ ((k >> 1) & 1), c ^ (k & 1)
            send = pltpu.make_async_remote_copy(
                src_ref=p_ref, dst_ref=o_ref.at[4 * x + 2 * y + c], send_sem=send_sems.at[k - 1], recv_sem=recv_sems.at[k - 1],
                device_id=(px, py, pc), device_id_type=MESH_ID,
            )
            send.start()
            copies.append((send, 4 * px + 2 * py + pc))
        for send, peer_slot in copies:
            send.wait_send()
        for k in range(1, N_DEV):
            px, py, pc = x ^ (k >> 2), y ^ ((k >> 1) & 1), c ^ (k & 1)
            pltpu.make_async_remote_copy(
                src_ref=p_ref, dst_ref=o_ref.at[4 * px + 2 * py + pc], send_sem=send_sems.at[k - 1], recv_sem=recv_sems.at[k - 1],
                device_id=(px, py, pc), device_id_type=MESH_ID,
            ).wait_recv()
        own.wait()

    sems = pltpu.SemaphoreType.DMA((N_DEV - 1,))
    return pl.pallas_call(
        body,
        name="gather_small",
        in_specs=[VMEM_SPEC],
        out_specs=VMEM_SPEC,
        out_shape=jax.ShapeDtypeStruct((N_DEV,) + pack.shape, pack.dtype),
        scratch_shapes=[sems, sems, pltpu.SemaphoreType.DMA],
    )(pack)


def _row_tile(rows):
    for tile in range(min(rows, 512) // 8 * 8, 0, -8):
        if rows % tile == 0:
            return tile
    return rows


def _add_half(grad, received, half_index, name):
    slots, h, cdim = received.shape
    tile = _row_tile(h)
    per_half = h // tile

    def body(c_ref, g_ref, r_ref, o_ref, ob_ref):
        total = g_ref[...] + r_ref[...]
        o_ref[...] = total
        ob_ref[...] = total.astype(BF16)

    block = pl.BlockSpec((1, tile, cdim), lambda j, i, c: (j, i, 0))
    grid_spec = pltpu.PrefetchScalarGridSpec(
        num_scalar_prefetch=1,
        grid=(slots, per_half),
        in_specs=[pl.BlockSpec((1, tile, cdim), lambda j, i, c: (j, c[0] * per_half + i, 0)), block],
        out_specs=[block, block],
    )
    return pl.pallas_call(
        body, name=name, grid_spec=grid_spec,
        out_shape=[jax.ShapeDtypeStruct(received.shape, F32), jax.ShapeDtypeStruct(received.shape, BF16)],
        compiler_params=_params(2),
    )(half_index, grad, received)


def _add_chips(part, received, chip_index, core_index, layer, n_layers, shard, name):
    _, h, cdim = part.shape
    tile = _row_tile(h)
    per_half = h // tile

    def body(chip_ref, core_ref, p_ref, r_ref, *rest):
        o_ref = rest[-1]
        o_ref[0] = ((p_ref[0] + r_ref[0].astype(F32)) + r_ref[1].astype(F32)) + r_ref[2].astype(F32)

    in_specs = [
        pl.BlockSpec((1, tile, cdim), lambda i, chip, core: (chip[0], i, 0)),
        pl.BlockSpec((N_CHIPS - 1, tile, cdim), lambda i, chip, core: (0, i, 0)),
    ]
    operands = [chip_index, core_index, part, received]
    aliases = {}
    if shard is not None:
        in_specs.append(ANY)
        operands.append(shard)
        aliases = {4: 0}
    grid_spec = pltpu.PrefetchScalarGridSpec(
        num_scalar_prefetch=2,
        grid=(per_half,),
        in_specs=in_specs,
        out_specs=pl.BlockSpec((1, tile, cdim), lambda i, chip, core: (layer, core[0] * per_half + i, 0)),
    )
    return pl.pallas_call(
        body, name=name, grid_spec=grid_spec, out_shape=jax.ShapeDtypeStruct((n_layers, 2 * h, cdim), F32),
        input_output_aliases=aliases, compiler_params=_params(1),
    )(*operands)


def _adamw(w, g, m, v, name):
    rows, cdim = w.shape
    tile = _row_tile(rows)

    def body(w_ref, g_ref, m_ref, v_ref, d_ref, nm_ref, nv_ref):
        gv = g_ref[...]
        nm = ADAM_B1 * m_ref[...] + (1.0 - ADAM_B1) * gv
        nv = ADAM_B2 * v_ref[...] + (1.0 - ADAM_B2) * (gv * gv)
        m_hat = nm / (1.0 - ADAM_B1 ** ADAM_STEP)
        v_hat = nv / (1.0 - ADAM_B2 ** ADAM_STEP)
        d_ref[...] = -ADAM_LR * (m_hat / (jnp.sqrt(v_hat) + ADAM_EPS) + ADAM_WD * w_ref[...])
        nm_ref[...] = nm
        nv_ref[...] = nv

    spec = pl.BlockSpec((tile, cdim), lambda i: (i, 0))
    shape = jax.ShapeDtypeStruct((rows, cdim), F32)
    return pl.pallas_call(
        body, name=name, grid=(rows // tile,), in_specs=[spec] * 4, out_specs=[spec] * 3, out_shape=[shape] * 3,
        compiler_params=_params(1),
    )(w, g, m, v)


SMALL_ROWS, SMALL_COLS = 24, 1024
ROW_NORM_MIX, ROW_NORM_FFN, ROW_LOSS, ROW_Q_NORM, ROW_K_NORM, ROW_CONV = 0, 2, 4, 8, 10, 16


def _sum_small(gathered):
    def body(g_ref, o_ref, heads_ref, lanes_ref):
        total = g_ref[0]
        for dev in range(1, N_DEV):
            total = total + g_ref[dev]
        o_ref[...] = total
        heads = o_ref[8:16, 0:LANES]
        for grp in range(1, ATTN_DIM // LANES):
            heads = heads + o_ref[8:16, grp * LANES : (grp + 1) * LANES]
        heads_ref[...] = heads + pltpu.roll(heads, HEAD_DIM, 1)
        lanes_ref[...] = jnp.broadcast_to(jnp.sum(o_ref[0:8, :], axis=-1, keepdims=True), (8, LANES))

    return pl.pallas_call(
        body,
        name="sum_small",
        in_specs=[VMEM_SPEC],
        out_specs=[VMEM_SPEC] * 3,
        out_shape=[jax.ShapeDtypeStruct((SMALL_ROWS, SMALL_COLS), F32), jax.ShapeDtypeStruct((8, LANES), F32), jax.ShapeDtypeStruct((8, LANES), F32)],
    )(gathered)


def _pad_rows(a, rows):
    return jnp.pad(a, ((0, rows - a.shape[0]), (0, 0)))


def _pad_to(a, rows, cols):
    return jnp.pad(a, ((0, rows - a.shape[0]), (0, cols - a.shape[1])))


def _conv_taps(conv_s):
    return jnp.transpose(conv_s[:, 0, 0:8], (1, 0, 2)).reshape(8, -1)


class _GradExchange:
    def __init__(self, chip_index, core_index, n_layers):
        self.chip_index, self.core_index, self.n_layers = chip_index, core_index, n_layers
        self.shards = {}
        self.pending = None

    def offer(self, layer, grads):
        assert self.pending is None
        names = list(grads)
        received = _swap_halves([grads[k] for k in names], f"{'_'.join(names)}_{layer}")
        parts = [_add_half(grads[k], r, self.core_index, f"add_half_{k}_{layer}") for k, r in zip(names, received)]
        self.pending = (layer, names, [p32 for p32, _ in parts], [p16 for _, p16 in parts])

    def payload(self):
        return () if self.pending is None else tuple(self.pending[3])

    def take(self, received):
        layer, names, parts, _ = self.pending
        self.pending = None
        for k, p, r in zip(names, parts, received):
            self.shards[k] = _add_chips(
                p, r, self.chip_index, self.core_index, layer, self.n_layers, self.shards.get(k), f"add_chips_{k}_{layer}")

    def finish(self):
        if self.pending is not None:
            layer, names = self.pending[0], self.pending[1]
            self.take(_scatter_to_chips(list(self.pending[3]), f"{'_'.join(names)}_{layer}"))
        return dict(zip(BIG, _join_halves([self.shards[k] for k in BIG])))


def _local_step(x, target, norm_mix, q_norm, k_norm, norm_ffn, layer_weights, exchange=None):
    layer_weights = list(layer_weights)

    def carrying(kernel_fn, n_out, *args):
        if exchange is None or exchange.pending is None:
            return kernel_fn(*args)
        out = kernel_fn(*args, scatter=exchange.payload())
        exchange.take(out[n_out:])
        return out[:n_out]

    n_layers = norm_mix.shape[0]
    s, d = x.shape
    tw = min(WGRAD_TILE, s)
    n_in = layer_weights[0][0].shape[-1]
    f = layer_weights[0][2].shape[-1]
    saved = []
    for l in range(n_layers):
        weights = list(layer_weights[l])
        q_gain = jnp.tile(q_norm[l][None, :], (1, 2))
        k_gain = jnp.tile(k_norm[l][None, :], (1, 2))
        h1, proj, qn, kn, vb = _in_proj(x, norm_mix[l][None, :], weights[0], 0, q_gain, k_gain, f"in_proj_{l}")
        missing = [n for n, w in enumerate(weights) if w.ndim == 3]
        if missing:
            attn, *arrived = _attn_fwd(qn, kn, vb, f"attn_fwd_{l}", gather=tuple(weights[n] for n in missing))
            for n, w in zip(missing, arrived):
                weights[n] = w
            layer_weights[l] = tuple(weights)
        else:
            attn = _attn_fwd(qn, kn, vb, f"attn_fwd_{l}")
        _, wout_s, wg_s, wu_s, wd_s, conv_s = weights
        taps = _conv_taps(conv_s)
        x_mid, conv = _out_proj(x, attn, proj, taps, wout_s, 0, f"out_proj_{l}")
        pending = ()
        if l + 1 < n_layers and all(w.ndim == 3 for w in layer_weights[l + 1]):
            pending = tuple(layer_weights[l + 1])
        x_out, gate, up, *more = _ffn_fwd(
            x_mid, norm_ffn[l][None, :], wg_s, wu_s, wd_s, 0, f"ffn_fwd_{l}", gather=pending, target=target if l == n_layers - 1 else None)
        if pending:
            layer_weights[l + 1] = tuple(more[: len(pending)])
        if l == n_layers - 1:
            dy, loss_lanes = more[len(pending) :]
        saved.append(dict(x=x, h1=h1, proj=proj, qn=qn, kn=kn, vb=vb, attn=attn, conv=conv, x_mid=x_mid, q_gain=q_gain, k_gain=k_gain,
                          gate=gate, up=up, taps=taps))
        x = x_out

    grads = [None] * n_layers
    for l in reversed(range(n_layers)):
        sv = saved[l]
        win_s, wout_s, wg_s, wu_s, wd_s, _ = layer_weights[l]
        dx_mid, d_norm_ffn, h2, dyb, dgate, dup, act = carrying(
            _ffn_bwd, 7, sv["x_mid"], dy, norm_ffn[l][None, :], sv["gate"], sv["up"], wg_s, wu_s, wd_s, 0, f"ffn_bwd_{l}")
        tok2 = pl.BlockSpec((tw, d), lambda j, i: (i, 0))
        hid = pl.BlockSpec((1, tw, f), lambda j, i: (j, i, 0))
        d_wg = _wgrad(h2, dgate, tok2, hid, N_CHIPS, d, f, f"wgrad_gate_{l}")
        d_wu = _wgrad(h2, dup, tok2, hid, N_CHIPS, d, f, f"wgrad_up_{l}")
        d_wd = _wgrad(act, dyb, hid, tok2, N_CHIPS, f, d, f"wgrad_down_{l}")
        if exchange is not None:
            exchange.offer(l, dict(w_gate=d_wg, w_up=d_wu, w_down=d_wd))
        d_attn, d_conv, dxb = _out_proj_bwd(dx_mid, wout_s, 0, f"out_proj_bwd_{l}")
        d_wout = _wgrad_out(sv["attn"], sv["conv"], dxb, wout_s.shape[2], f"wgrad_out_{l}")
        dq, dk, dv = carrying(_attn_bwd, 3, sv["qn"], sv["kn"], sv["vb"], d_attn, f"attn_bwd_{l}")
        dproj_conv, d_conv_w = _conv_bwd(sv["proj"], sv["taps"], d_conv, f"conv_bwd_{l}")
        dy, d_norm_mix, dproj, d_qg, d_kg = _in_proj_bwd(
            sv["x"], dx_mid, norm_mix[l][None, :], win_s, 0, sv["proj"], sv["q_gain"], sv["k_gain"], dq, dk, dv, dproj_conv, f"in_proj_bwd_{l}")
        d_win = _wgrad(sv["h1"], dproj, tok2, pl.BlockSpec((tw, n_in), lambda j, i: (i, j)), N_CHIPS, d, n_in, f"wgrad_in_{l}")
        if exchange is not None:
            exchange.offer(l, dict(w_in=d_win, w_out=d_wout))
        grads[l] = dict(norm_mix=d_norm_mix, norm_ffn=d_norm_ffn, q_norm=d_qg, k_norm=d_kg, conv_w=d_conv_w,
                        w_in=d_win, w_out=d_wout, w_gate=d_wg, w_up=d_wu, w_down=d_wd)
    return loss_lanes, dy, grads


BIG = ("w_in", "w_out", "w_gate", "w_up", "w_down")


def kernel(x, norm_mix, w_in, q_norm, k_norm, conv_w, w_out, norm_ffn, w_gate, w_up, w_down, loss_target, m_norm_mix, m_w_in, m_q_norm, m_k_norm, m_conv_w, m_w_out, m_norm_ffn, m_w_gate, m_w_up, m_w_down, v_norm_mix, v_w_in, v_q_norm, v_k_norm, v_conv_w, v_w_out, v_norm_ffn, v_w_gate, v_w_up, v_w_down):
    n_layers = norm_mix.shape[0]
    weights = dict(w_in=w_in, w_out=w_out, w_gate=w_gate, w_up=w_up, w_down=w_down)
    moments_m = dict(w_in=m_w_in, w_out=m_w_out, w_gate=m_w_gate, w_up=m_w_up, w_down=m_w_down)
    moments_v = dict(w_in=v_w_in, w_out=v_w_out, w_gate=v_w_gate, w_up=v_w_up, w_down=v_w_down)
    cx, cy, cc = _mesh_position()
    chip_index = (2 * cx + cy).astype(jnp.int32).reshape(1)
    core_index = cc.astype(jnp.int32).reshape(1)

    conv_pad = jnp.pad(conv_w, ((0, 0), (0, 16 - conv_w.shape[1]), (0, 0)))

    def shards_of(layer):
        return [weights[k][layer : layer + 1].astype(BF16) for k in BIG] + [conv_pad[layer : layer + 1]]

    first = shards_of(0)
    layer_weights = [tuple(_gather_weights(first[:1])) + tuple(first[1:])] + [tuple(shards_of(layer)) for layer in range(1, n_layers)]

    exchange = _GradExchange(chip_index, core_index, n_layers)
    loss_lanes, grad_x, grads = _local_step(
        x[0], loss_target[0], norm_mix, q_norm, k_norm, norm_ffn, layer_weights, exchange)

    big_grads = exchange.finish()

    def lanes(a):
        return _pad_to(a, a.shape[0], SMALL_COLS)

    def tile_of(*groups):
        return _pad_rows(jnp.concatenate([lanes(jnp.concatenate(g, axis=0)) for g in groups], axis=0), 8)

    layers = range(n_layers)
    pack = jnp.concatenate([
        tile_of([grads[l]["norm_mix"] for l in layers], [grads[l]["norm_ffn"] for l in layers], [loss_lanes]),
        tile_of([grads[l]["q_norm"] for l in layers], [grads[l]["k_norm"] for l in layers]),
        tile_of([grads[l]["conv_w"][0:3] for l in layers]),
    ], axis=0)
    small, small_heads, small_lanes = _sum_small(_gather_small(pack))
    loss = small_lanes[ROW_LOSS, 0]
    d_model = norm_mix.shape[1]
    conv_cols = conv_w.shape[2]
    conv_all = small[ROW_CONV : ROW_CONV + 3 * n_layers, 0:CONV_DIM].reshape(n_layers, 3, CONV_DIM)
    small_grads = dict(
        norm_mix=small[ROW_NORM_MIX : ROW_NORM_MIX + n_layers, 0:d_model],
        norm_ffn=small[ROW_NORM_FFN : ROW_NORM_FFN + n_layers, 0:d_model],
        q_norm=small_heads[ROW_Q_NORM - 8 : ROW_Q_NORM - 8 + n_layers, 0:HEAD_DIM],
        k_norm=small_heads[ROW_K_NORM - 8 : ROW_K_NORM - 8 + n_layers, 0:HEAD_DIM],
        conv_w=lax.dynamic_slice_in_dim(conv_all, (2 * cx + cy) * conv_cols, conv_cols, axis=2),
    )

    out_grad, out_delta, out_m, out_v = {}, {}, {}, {}
    for k in BIG:
        shape = weights[k].shape
        view = (shape[0] * shape[1], shape[2])
        g = big_grads[k]
        delta, new_m, new_v = _adamw(weights[k].reshape(view), g.reshape(view), moments_m[k].reshape(view), moments_v[k].reshape(view), f"adamw_{k}")
        out_grad[k], out_delta[k], out_m[k], out_v[k] = g, delta.reshape(shape), new_m.reshape(shape), new_v.reshape(shape)

    small_w = dict(norm_mix=norm_mix, norm_ffn=norm_ffn, q_norm=q_norm, k_norm=k_norm, conv_w=conv_w)
    small_m = dict(norm_mix=m_norm_mix, norm_ffn=m_norm_ffn, q_norm=m_q_norm, k_norm=m_k_norm, conv_w=m_conv_w)
    small_v = dict(norm_mix=v_norm_mix, norm_ffn=v_norm_ffn, q_norm=v_q_norm, k_norm=v_k_norm, conv_w=v_conv_w)
    order = ("norm_mix", "norm_ffn", "q_norm", "k_norm", "conv_w")

    def packed(tree):
        parts2 = [_pad_to(tree[k].reshape(-1, tree[k].shape[-1]), tree[k].reshape(-1, tree[k].shape[-1]).shape[0], SMALL_COLS) for k in order]
        return _pad_rows(jnp.concatenate(parts2, axis=0), SMALL_ROWS)

    delta_p, m_p, v_p = _adamw(packed(small_w), packed(small_grads), packed(small_m), packed(small_v), "adamw_small")
    row = 0
    for k in order:
        shape = small_w[k].shape
        n_rows = 1
        for dim in shape[:-1]:
            n_rows *= dim
        cut = (slice(row, row + n_rows), slice(0, shape[-1]))
        out_grad[k] = small_grads[k]
        out_delta[k], out_m[k], out_v[k] = delta_p[cut].reshape(shape), m_p[cut].reshape(shape), v_p[cut].reshape(shape)
        row += n_rows

    names_out = ("norm_mix", "w_in", "q_norm", "k_norm", "conv_w", "w_out", "norm_ffn", "w_gate", "w_up", "w_down")
    return (loss, grad_x[None], *[out_grad[k] for k in names_out], *[out_delta[k] for k in names_out],
            *[out_m[k] for k in names_out], *[out_v[k] for k in names_out])
```

```python
import jax
import jax.numpy as jnp
from jax import lax
from jax.experimental import pallas as pl
from jax.experimental.pallas import tpu as pltpu

F32 = jnp.float32
BF16 = jnp.bfloat16

EPS = 1e-6
HEAD_DIM = 64
LANES = 128
ATTN_DIM = 512
CONV_DIM = 512
N_CHIPS = 4
N_DEV = 8
Q_SCALE = HEAD_DIM ** -0.5
ATTN_Q_TILE = 256
ATTN_TILE = 256
TOKEN_TILE = 512
WGRAD_TILE = 4096
FFN_FWD_TILE = 1024
FFN_CHUNK = 256
VMEM_LIMIT = 56 * 1024 * 1024

ADAM_LR = 0.001
ADAM_B1 = 0.9
ADAM_B2 = 0.999
ADAM_EPS = 1e-08
ADAM_WD = 0.01
ADAM_STEP = 10

MESH_ID = pl.DeviceIdType.MESH
ANY = pl.BlockSpec(memory_space=pl.ANY)
VMEM_SPEC = pl.BlockSpec(memory_space=pltpu.VMEM)


def _params(n_axes):
    return pltpu.CompilerParams(dimension_semantics=("arbitrary",) * n_axes, vmem_limit_bytes=VMEM_LIMIT)


def _dot(a, b):
    return jnp.dot(a, b, preferred_element_type=F32)


def _dot_nt(a, b):
    return lax.dot_general(a, b, (((1,), (1,)), ((), ())), preferred_element_type=F32)


def _dot_tn(a, b):
    return lax.dot_general(a, b, (((0,), (0,)), ((), ())), preferred_element_type=F32)


SCORE_MAX = 80.0
UNDERFLOW_EXIT = 90.0


def _scores(q, k):
    return jnp.minimum(_dot_nt(q, k), SCORE_MAX)


def _softplus(z):
    return jnp.log(1.0 + jnp.exp(z))


def _head_norm(xv, gain, low):
    sq = xv * xv
    s_low = jnp.sum(jnp.where(low, sq, 0.0), axis=-1, keepdims=True)
    s_high = jnp.sum(jnp.where(low, 0.0, sq), axis=-1, keepdims=True)
    r = jnp.where(low, lax.rsqrt(s_low / HEAD_DIM + EPS), lax.rsqrt(s_high / HEAD_DIM + EPS))
    return xv * r * gain, r


def _in_proj(x, gain, w_s, layer, q_gain, k_gain, name):
    s, d = x.shape
    n_blocks, _, _, n = w_s.shape
    tm = TOKEN_TILE

    def body(x_ref, g_ref, w_ref, qg_ref, kg_ref, h_ref, o_ref, q_ref, k_ref, v_ref):
        xv = x_ref[...]
        r = lax.rsqrt(jnp.mean(xv * xv, axis=-1, keepdims=True) + EPS)
        h = (xv * r * g_ref[...]).astype(BF16)
        h_ref[...] = h
        for j in range(n_blocks):
            o_ref[:, j * n : (j + 1) * n] = _dot(h, w_ref[j, 0])
        low = lax.broadcasted_iota(jnp.int32, (tm, LANES), 1) < HEAD_DIM
        for g in range(ATTN_DIM // LANES):
            cq = slice(LANES * g, LANES * (g + 1))
            ck = slice(ATTN_DIM + LANES * g, ATTN_DIM + LANES * (g + 1))
            cv = slice(2 * ATTN_DIM + LANES * g, 2 * ATTN_DIM + LANES * (g + 1))
            qn, _ = _head_norm(o_ref[:, cq], qg_ref[...], low)
            kn, _ = _head_norm(o_ref[:, ck], kg_ref[...], low)
            q_ref[:, cq] = (qn * Q_SCALE).astype(BF16)
            k_ref[:, cq] = kn.astype(BF16)
            v_ref[:, cq] = o_ref[:, cv].astype(BF16)

    head_spec = pl.BlockSpec((tm, ATTN_DIM), lambda i: (i, 0))
    head_shape = jax.ShapeDtypeStruct((s, ATTN_DIM), BF16)
    gain_spec = pl.BlockSpec((1, LANES), lambda i: (0, 0))
    return pl.pallas_call(
        body,
        name=name,
        grid=(s // tm,),
        in_specs=[
            pl.BlockSpec((tm, d), lambda i: (i, 0)),
            pl.BlockSpec((1, d), lambda i: (0, 0)),
            pl.BlockSpec((n_blocks, 1, d, n), lambda i: (0, layer, 0, 0)),
            gain_spec, gain_spec,
        ],
        out_specs=[pl.BlockSpec((tm, d), lambda i: (i, 0)), pl.BlockSpec((tm, n_blocks * n), lambda i: (i, 0)), head_spec, head_spec, head_spec],
        out_shape=[jax.ShapeDtypeStruct((s, d), BF16), jax.ShapeDtypeStruct((s, n_blocks * n), F32), head_shape, head_shape, head_shape],
        compiler_params=_params(1),
    )(x, gain, w_s, q_gain, k_gain)


def _attn_tile_consts(t):
    row = lax.broadcasted_iota(jnp.int32, (t, t), 0)
    col = lax.broadcasted_iota(jnp.int32, (t, t), 1)
    return row, col


def _triangle_sum(v, triangle):
    return _dot(v.astype(BF16), triangle)


def _attn_fwd(qn, kn, vb, name, gather=()):
    s = qn.shape[0]
    t = min(ATTN_TILE, s)
    tq = min(ATTN_Q_TILE, t)
    per_key_tile = t // tq
    n_gather = len(gather)
    n_pairs, n_blocks = ATTN_DIM // LANES, s // tq

    def body(*refs):
        q_ref, k_ref, v_ref = refs[:3]
        o_ref = refs[3 + n_gather]
        if n_gather:
            copies = _WeightGather(refs[3 : 3 + n_gather], refs[4 + n_gather : 4 + 2 * n_gather], refs[4 + 2 * n_gather :])
            first = (pl.program_id(0) == 0) & (pl.program_id(1) == 0)
            pl.when(first)(copies.begin)
            pl.when((pl.program_id(0) == n_pairs - 1) & (pl.program_id(1) == 0))(copies.relay)
        i = pl.program_id(1) // per_key_tile
        low = lax.broadcasted_iota(jnp.int32, (tq, LANES), 1) < HEAD_DIM
        row, col = _attn_tile_consts(t)
        suffix = (row > col).astype(BF16)
        first_row = (pl.program_id(1) % per_key_tile) * tq
        causal = lax.broadcasted_iota(jnp.int32, (tq, t), 1) < lax.broadcasted_iota(jnp.int32, (tq, t), 0) + first_row
        q = q_ref[...]
        zero_q = jnp.zeros_like(q)
        qh = (jnp.where(low, q, zero_q), jnp.where(low, zero_q, q))

        def step(kbs, carry, diagonal_first=False):
            chains = [(head, m) for head in range(2) for m in range(len(kbs))]
            masked = [diagonal_first and m == 0 for _, m in chains]
            ks = [k_ref[pl.ds(pl.multiple_of(kb * t, t), t), :] for kb in kbs]
            vs = [v_ref[pl.ds(pl.multiple_of(kb * t, t), t), :] for kb in kbs]
            z = [_scores(qh[head], ks[kb]) for head, kb in chains]
            sp = [_softplus(zc) for zc in z]
            sp = [jnp.where(causal, s_, 0.0) if mk else s_ for s_, mk in zip(sp, masked)]
            inside = [_triangle_sum(s_, suffix) for s_ in sp]
            after = [carry[head][1] for head in range(2)]
            log_a = []
            for n, (head, kb) in enumerate(chains):
                log_a.append(z[n] - sp[n] - inside[n] - after[head])
                after[head] = after[head] + jnp.sum(sp[n], axis=-1, keepdims=True)
            a = [jnp.exp(l_) for l_ in log_a]
            a = [jnp.where(causal, a_, 0.0) if mk else a_ for a_, mk in zip(a, masked)]
            acc = [carry[head][0] for head in range(2)]
            for n, (head, kb) in enumerate(chains):
                acc[head] = acc[head] + _dot(a[n].astype(BF16), vs[kb])
            return tuple((acc[head], after[head]) for head in range(2))

        def live(c):
            return jnp.minimum(jnp.min(c[0][1]), jnp.min(c[1][1])) < UNDERFLOW_EXIT

        zero = (jnp.zeros((tq, LANES), F32), jnp.zeros((tq, 1), F32))
        start = lax.cond(i >= 1, lambda c: step((i, i - 1), c, True), lambda c: step((i,), c, True), (zero, zero))
        o_ref[...] = jnp.where(low, start[0][0], start[1][0]).astype(BF16)
        rest = jnp.maximum(i - 1, 0)

        @pl.when((rest > 0) & live(start))
        def _():
            carry = lax.cond(rest % 2 == 1, lambda c: step((i - 2,), c), lambda c: c, start)
            pairs = rest // 2
            _, carry = lax.while_loop(
                lambda st: (st[0] < pairs) & live(st[1]),
                lambda st: (st[0] + 1, step((2 * (pairs - st[0]) - 1, 2 * (pairs - st[0]) - 2), st[1])),
                (jnp.int32(0), carry))
            o_ref[...] = jnp.where(low, carry[0][0], carry[1][0]).astype(BF16)

        if n_gather:
            pl.when((pl.program_id(0) == n_pairs - 1) & (pl.program_id(1) == n_blocks - 1))(copies.finish)

    out = pl.pallas_call(
        body,
        name=name,
        grid=(n_pairs, n_blocks),
        in_specs=[
            pl.BlockSpec((tq, LANES), lambda p, i: (i, p)),
            pl.BlockSpec((s, LANES), lambda p, i: (0, p)),
            pl.BlockSpec((s, LANES), lambda p, i: (0, p)),
        ] + [ANY] * n_gather,
        out_specs=[pl.BlockSpec((tq, LANES), lambda p, i: (i, p))] + [ANY] * n_gather,
        out_shape=[jax.ShapeDtypeStruct((s, ATTN_DIM), BF16)] + [jax.ShapeDtypeStruct((N_CHIPS,) + w.shape, w.dtype) for w in gather],
        scratch_shapes=_gather_scratch(n_gather) if n_gather else [],
        compiler_params=_params(2),
    )(qn, kn, vb, *gather)
    return out if n_gather else out[0]


def _attn_bwd(qn, kn, vb, do, name, scatter=()):
    s = qn.shape[0]
    t = min(ATTN_TILE, s)
    nq = s // t
    n_scatter = len(scatter)
    n_pairs = ATTN_DIM // LANES

    def body(*refs):
        q_ref, k_ref, v_ref, do_ref = refs[:4]
        dq_ref, dk_ref, dv_ref = refs[4 + n_scatter : 7 + n_scatter]
        a_s, sg_s, a_f, sg_f = refs[7 + 2 * n_scatter : 11 + 2 * n_scatter]
        i = pl.program_id(1)
        if n_scatter:
            copies = _ChipScatter(refs[4 : 4 + n_scatter], refs[7 + n_scatter : 7 + 2 * n_scatter], refs[11 + 2 * n_scatter :])
            pl.when((pl.program_id(0) == 0) & (i == 0))(copies.begin)

        @pl.when(i == 0)
        def _():
            dk_ref[...] = jnp.zeros_like(dk_ref)
            dv_ref[...] = jnp.zeros_like(dv_ref)

        low = lax.broadcasted_iota(jnp.int32, (t, LANES), 1) < HEAD_DIM
        row, col = _attn_tile_consts(t)
        suffix = (row > col).astype(BF16)
        prefix = (row < col).astype(BF16)
        causal = col < row
        q = q_ref[...]
        dob = do_ref[...]
        zero_q = jnp.zeros_like(q)
        qhs = (jnp.where(low, q, zero_q), jnp.where(low, zero_q, q))
        dohs = (jnp.where(low, dob, zero_q), jnp.where(low, zero_q, dob))

        def rows_of(kb):
            return pl.ds(pl.multiple_of(kb * t, t), t)

        pair = [(head, m) for head in range(2) for m in range(2)]

        def short_pass1():
            z = [_scores(qhs[head], k_ref[rows_of(i - m), :]) for head, m in pair]
            sp = [_softplus(z_) for z_ in z]
            sp = [jnp.where(causal, s_, 0.0) if m == 0 else s_ for s_, (_, m) in zip(sp, pair)]
            inside = [_triangle_sum(s_, suffix) for s_ in sp]
            after = [jnp.zeros((t, 1), F32), jnp.zeros((t, 1), F32)]
            for n, (head, m) in enumerate(pair):
                log_sg = z[n] - sp[n]
                a = jnp.exp(log_sg - inside[n] - after[head])
                sg = jnp.exp(log_sg)
                if m == 0:
                    a = jnp.where(causal, a, 0.0)
                    sg = jnp.where(causal, sg, 0.0)
                a_f[n] = a
                sg_f[n] = sg
                after[head] = after[head] + jnp.sum(sp[n], axis=-1, keepdims=True)
            return jnp.minimum(jnp.min(after[0]), jnp.min(after[1])) >= UNDERFLOW_EXIT

        def short_pass2():
            order = [(head, m) for head in range(2) for m in (1, 0)]
            a = {c: a_f[pair.index(c)] for c in order}
            g = {c: a[c] * _dot_nt(dohs[c[0]], v_ref[rows_of(i - c[1]), :]) for c in order}
            for m in (1, 0):
                dv_ref[rows_of(i - m), :] += _dot_tn(a[(0, m)].astype(BF16), dohs[0]) + _dot_tn(a[(1, m)].astype(BF16), dohs[1])
            inside = {c: _triangle_sum(g[c], prefix) for c in order}
            before = [jnp.zeros((t, 1), F32), jnp.zeros((t, 1), F32)]
            dz = {}
            for c in order:
                sg = sg_f[pair.index(c)]
                dz[c] = (g[c] - sg * (g[c] + inside[c] + before[c[0]])).astype(BF16)
                before[c[0]] = before[c[0]] + jnp.sum(g[c], axis=-1, keepdims=True)
            for m in (1, 0):
                dk_ref[rows_of(i - m), :] += _dot_tn(dz[(0, m)], qhs[0]) + _dot_tn(dz[(1, m)], qhs[1])
            dq = [_dot(dz[(head, 1)], k_ref[rows_of(i - 1), :]) + _dot(dz[(head, 0)], k_ref[rows_of(i), :]) for head in range(2)]
            dq_ref[...] = jnp.where(low, dq[0], dq[1])

        def general_walk():
            heads = []
            for head in range(2):
                qh, doh = qhs[head], dohs[head]

                def pass1(kbs, after, diagonal_first=False):
                    z = [_scores(qh, k_ref[rows_of(kb), :]) for kb in kbs]
                    sp = [_softplus(z_) for z_ in z]
                    if diagonal_first:
                        sp[0] = jnp.where(causal, sp[0], 0.0)
                    inside = [_triangle_sum(s_, suffix) for s_ in sp]
                    for n, kb in enumerate(kbs):
                        log_sg = z[n] - sp[n]
                        a = jnp.exp(log_sg - inside[n] - after)
                        sg = jnp.exp(log_sg)
                        if diagonal_first and n == 0:
                            a = jnp.where(causal, a, 0.0)
                            sg = jnp.where(causal, sg, 0.0)
                        a_s[kb] = a
                        sg_s[kb] = sg
                        after = after + jnp.sum(sp[n], axis=-1, keepdims=True)
                    return after

                def live(after):
                    return jnp.min(after) < UNDERFLOW_EXIT

                after = jnp.zeros((t, 1), F32)
                after = lax.cond(i >= 1, lambda c: pass1((i, i - 1), c, True), lambda c: pass1((i,), c, True), after)
                rest = jnp.maximum(i - 1, 0)
                take_single = (rest % 2 == 1) & live(after)
                after = lax.cond(take_single, lambda c: pass1((i - 2,), c), lambda c: c, after)
                pairs = rest // 2
                pairs_done, _ = lax.while_loop(
                    lambda st: (st[0] < pairs) & live(st[1]),
                    lambda st: (st[0] + 1, pass1((2 * (pairs - st[0]) - 1, 2 * (pairs - st[0]) - 2), st[1])),
                    (jnp.int32(0), after))
                walked = jnp.minimum(i, 1) + 1 + take_single.astype(jnp.int32) + 2 * pairs_done
                first = i - walked + 1

                def pass2(kbs, carry):
                    dq, before = carry
                    ks = [k_ref[rows_of(kb), :] for kb in kbs]
                    a = [a_s[kb] for kb in kbs]
                    g = [a_ * _dot_nt(doh, v_ref[rows_of(kb), :]) for a_, kb in zip(a, kbs)]
                    for n, kb in enumerate(kbs):
                        dv_ref[rows_of(kb), :] += _dot_tn(a[n].astype(BF16), doh)
                    inside = [_triangle_sum(g_, prefix) for g_ in g]
                    dz = []
                    for n, kb in enumerate(kbs):
                        sg = sg_s[kb]
                        dz.append((g[n] - sg * (g[n] + inside[n] + before)).astype(BF16))
                        before = before + jnp.sum(g[n], axis=-1, keepdims=True)
                    for n, kb in enumerate(kbs):
                        dk_ref[rows_of(kb), :] += _dot_tn(dz[n], qh)
                    for n in range(len(kbs)):
                        dq = dq + _dot(dz[n], ks[n])
                    return dq, before

                carry = (jnp.zeros((t, LANES), F32), jnp.zeros((t, 1), F32))
                carry = lax.fori_loop(0, walked // 2, lambda n, c: pass2((first + 2 * n, first + 2 * n + 1), c), carry)
                carry = lax.cond(walked % 2 == 1, lambda c: pass2((i,), c), lambda c: c, carry)
                heads.append(carry[0])
            dq_ref[...] = jnp.where(low, heads[0], heads[1])

        short = lax.cond(i >= 1, short_pass1, lambda: jnp.bool_(False))
        pl.when(short)(short_pass2)
        pl.when(jnp.logical_not(short))(general_walk)
        if n_scatter:
            pl.when((pl.program_id(0) == n_pairs - 1) & (i == nq - 1))(copies.finish)

    q_spec = pl.BlockSpec((t, LANES), lambda p, i: (i, p))
    kv_spec = pl.BlockSpec((s, LANES), lambda p, i: (0, p))
    return pl.pallas_call(
        body,
        name=name,
        grid=(n_pairs, nq),
        in_specs=[q_spec, kv_spec, kv_spec, q_spec] + [ANY] * n_scatter,
        out_specs=[q_spec, kv_spec, kv_spec] + [ANY] * n_scatter,
        out_shape=[jax.ShapeDtypeStruct((s, ATTN_DIM), F32)] * 3 + _scatter_shapes(scatter),
        scratch_shapes=[pltpu.VMEM((nq, t, t), F32), pltpu.VMEM((nq, t, t), F32), pltpu.VMEM((4, t, t), F32), pltpu.VMEM((4, t, t), F32)]
        + (_scatter_scratch(n_scatter) if n_scatter else []),
        compiler_params=_params(2),
    )(qn, kn, vb, do, *scatter)


CB_BLOCK, CC_BLOCK, CU_BLOCK = 3, 4, 5


def _shift_down(h, prev_rows, n):
    row = lax.broadcasted_iota(jnp.int32, h.shape, 0)
    out = pltpu.roll(h, n, 0)
    for r in range(n):
        out = jnp.where(row == r, prev_rows[len(prev_rows) - n + r], out)
    return out


def _shift_up(h, next_rows, n):
    tm = h.shape[0]
    row = lax.broadcasted_iota(jnp.int32, h.shape, 0)
    out = pltpu.roll(h, tm - n, 0)
    for r in range(n):
        out = jnp.where(row == tm - n + r, next_rows[r], out)
    return out


def _conv_bwd(proj, conv_w, dconv, name):
    s = proj.shape[0]
    tm = TOKEN_TILE
    nb = tm // 8
    n_tiles = s // tm

    def body(cb_ref, cc_ref, cu_ref, dy_ref, pc_ref, pu_ref, nb_ref, ndy_ref, w_ref, dp_ref, dw_ref):
        i = pl.program_id(0)

        @pl.when(i == 0)
        def _():
            dw_ref[...] = jnp.zeros_like(dw_ref)

        first = i == 0
        last = i == n_tiles - 1
        cc, cu, cb, dy = cc_ref[...], cu_ref[...], cb_ref[...], dy_ref[...]
        h = cc * cu
        prev = [jnp.where(first, 0.0, pc_ref[r : r + 1, :] * pu_ref[r : r + 1, :]) for r in (6, 7)]
        h1 = _shift_down(h, prev, 1)
        h2 = _shift_down(h, prev, 2)
        y = w_ref[0:1, :] * h2 + w_ref[1:2, :] * h1 + w_ref[2:3, :] * h
        dyb = dy * cb
        nxt = [jnp.where(last, 0.0, ndy_ref[r : r + 1, :] * nb_ref[r : r + 1, :]) for r in (0, 1)]
        dh = w_ref[2:3, :] * dyb + w_ref[1:2, :] * _shift_up(dyb, nxt, 1) + w_ref[0:1, :] * _shift_up(dyb, nxt, 2)
        dp_ref[:, 0:CONV_DIM] = (dy * y).astype(BF16)
        dp_ref[:, CONV_DIM : 2 * CONV_DIM] = (dh * cu).astype(BF16)
        dp_ref[:, 2 * CONV_DIM : 3 * CONV_DIM] = (dh * cc).astype(BF16)
        dw_ref[0:1, :] += jnp.sum(dyb * h2, axis=0, keepdims=True)
        dw_ref[1:2, :] += jnp.sum(dyb * h1, axis=0, keepdims=True)
        dw_ref[2:3, :] += jnp.sum(dyb * h, axis=0, keepdims=True)

    def col(block):
        return pl.BlockSpec((tm, CONV_DIM), lambda i: (i, block))

    def halo_prev(block):
        return pl.BlockSpec((8, CONV_DIM), lambda i: (jnp.maximum(i * nb - 1, 0), block))

    def halo_next(block):
        return pl.BlockSpec((8, CONV_DIM), lambda i: (jnp.minimum((i + 1) * nb, s // 8 - 1), block))

    return pl.pallas_call(
        body,
        name=name,
        grid=(n_tiles,),
        in_specs=[
            col(CB_BLOCK), col(CC_BLOCK), col(CU_BLOCK), col(0),
            halo_prev(CC_BLOCK), halo_prev(CU_BLOCK), halo_next(CB_BLOCK), halo_next(0),
            pl.BlockSpec((8, CONV_DIM), lambda i: (0, 0)),
        ],
        out_specs=[pl.BlockSpec((tm, 3 * CONV_DIM), lambda i: (i, 1)), pl.BlockSpec((8, CONV_DIM), lambda i: (0, 0))],
        out_shape=[jax.ShapeDtypeStruct((s, 3 * ATTN_DIM + 3 * CONV_DIM), BF16), jax.ShapeDtypeStruct((8, CONV_DIM), F32)],
        compiler_params=_params(1),
    )(proj, proj, proj, dconv, proj, proj, proj, dconv, conv_w)


def _out_proj(x, attn, proj, conv_w, w_s, layer, name):
    s, d = x.shape
    tm = TOKEN_TILE
    nb = tm // 8
    rows = w_s.shape[2]

    def body(x_ref, a_ref, cb_ref, cc_ref, cu_ref, pc_ref, pu_ref, cw_ref, w_ref, o_ref, c_ref):
        first = pl.program_id(0) == 0
        h = cc_ref[...] * cu_ref[...]
        prev = [jnp.where(first, 0.0, pc_ref[r : r + 1, :] * pu_ref[r : r + 1, :]) for r in (6, 7)]
        y = cw_ref[0:1, :] * _shift_down(h, prev, 2) + cw_ref[1:2, :] * _shift_down(h, prev, 1) + cw_ref[2:3, :] * h
        c_ref[...] = (cb_ref[...] * y).astype(BF16)
        acc = x_ref[...]
        for j in range(N_CHIPS):
            src = a_ref if j < 2 else c_ref
            cols = slice((j % 2) * rows, (j % 2 + 1) * rows)
            acc = acc + _dot(src[:, cols], w_ref[j, 0])
        o_ref[...] = acc

    def col(block):
        return pl.BlockSpec((tm, CONV_DIM), lambda i: (i, block))

    def halo(block):
        return pl.BlockSpec((8, CONV_DIM), lambda i: (jnp.maximum(i * nb - 1, 0), block))

    return pl.pallas_call(
        body,
        name=name,
        grid=(s // tm,),
        in_specs=[
            pl.BlockSpec((tm, d), lambda i: (i, 0)),
            pl.BlockSpec((tm, ATTN_DIM), lambda i: (i, 0)),
            col(CB_BLOCK), col(CC_BLOCK), col(CU_BLOCK), halo(CC_BLOCK), halo(CU_BLOCK),
            pl.BlockSpec((8, CONV_DIM), lambda i: (0, 0)),
            pl.BlockSpec((N_CHIPS, 1, rows, d), lambda i: (0, layer, 0, 0)),
        ],
        out_specs=[pl.BlockSpec((tm, d), lambda i: (i, 0)), pl.BlockSpec((tm, CONV_DIM), lambda i: (i, 0))],
        out_shape=[jax.ShapeDtypeStruct((s, d), F32), jax.ShapeDtypeStruct((s, CONV_DIM), BF16)],
        compiler_params=_params(1),
    )(x, attn, proj, proj, proj, proj, proj, conv_w, w_s)


def _out_proj_bwd(dx, w_s, layer, name):
    s, d = dx.shape
    tm = TOKEN_TILE
    rows = w_s.shape[2]

    def body(dx_ref, w_ref, da_ref, dc_ref, dxb_ref):
        dxb = dx_ref[...].astype(BF16)
        dxb_ref[...] = dxb
        for j in range(N_CHIPS):
            cols = slice((j % 2) * rows, (j % 2 + 1) * rows)
            part = _dot_nt(dxb, w_ref[j, 0])
            if j < 2:
                da_ref[:, cols] = part.astype(BF16)
            else:
                dc_ref[:, cols] = part

    return pl.pallas_call(
        body,
        name=name,
        grid=(s // tm,),
        in_specs=[pl.BlockSpec((tm, d), lambda i: (i, 0)), pl.BlockSpec((N_CHIPS, 1, rows, d), lambda i: (0, layer, 0, 0))],
        out_specs=[
            pl.BlockSpec((tm, ATTN_DIM), lambda i: (i, 0)),
            pl.BlockSpec((tm, CONV_DIM), lambda i: (i, 0)),
            pl.BlockSpec((tm, d), lambda i: (i, 0)),
        ],
        out_shape=[
            jax.ShapeDtypeStruct((s, ATTN_DIM), BF16),
            jax.ShapeDtypeStruct((s, CONV_DIM), F32),
            jax.ShapeDtypeStruct((s, d), BF16),
        ],
        compiler_params=_params(1),
    )(dx, w_s)


def _ffn_fwd(x, gain, wg_s, wu_s, wd_s, layer, name, gather=(), target=None):
    s, d = x.shape
    tm = min(FFN_FWD_TILE, s)
    n_loss = 0 if target is None else 1
    f = wg_s.shape[3]
    n_gather = len(gather)
    n_tiles = s // tm

    def body(*refs):
        x_ref, g_ref, wg_ref, wu_ref, wd_ref = refs[:5]
        n_in = 5 + n_gather + n_loss
        o_ref, gate_ref, up_ref = refs[n_in : n_in + 3]
        h_s = refs[n_in + 3 + n_gather + 2 * n_loss]
        i, j = pl.program_id(0), pl.program_id(1)
        if n_gather:
            copies = _WeightGather(refs[5 : 5 + n_gather], refs[n_in + 3 : n_in + 3 + n_gather], refs[n_in + 4 + n_gather + 2 * n_loss :])
            pl.when((i == 0) & (j == 0))(copies.begin)
            pl.when((i == (3 * n_tiles) // 4) & (j == 0))(copies.relay)

        @pl.when(j == 0)
        def _():
            xv = x_ref[...]
            r = lax.rsqrt(jnp.mean(xv * xv, axis=-1, keepdims=True) + EPS)
            h_s[...] = (xv * r * g_ref[...]).astype(BF16)
            o_ref[...] = xv

        halves = [slice(r, r + FFN_CHUNK) for r in range(0, tm, FFN_CHUNK)]
        pre = [(_dot(h_s[r, :], wg_ref[0, 0]), _dot(h_s[r, :], wu_ref[0, 0])) for r in halves]
        act = [((gate / (1.0 + jnp.exp(-gate))) * up).astype(BF16) for gate, up in pre]
        for r, (gate, up) in zip(halves, pre):
            gate_ref[0, r, :] = gate.astype(BF16)
            up_ref[0, r, :] = up.astype(BF16)
        for r, a in zip(halves, act):
            o_ref[r, :] += _dot(a, wd_ref[0, 0])

        if n_loss:
            t_ref = refs[5 + n_gather]
            dy_ref, l_ref = refs[n_in + 3 + n_gather : n_in + 5 + n_gather]

            @pl.when((i == 0) & (j == 0))
            def _():
                l_ref[...] = jnp.zeros_like(l_ref)

            @pl.when(j == N_CHIPS - 1)
            def _():
                err = o_ref[...] - t_ref[...]
                dy_ref[...] = err / d
                l_ref[...] += jnp.sum(err * err, axis=0, keepdims=True) * (0.5 / d)

        if n_gather:
            pl.when((i == n_tiles - 1) & (j == N_CHIPS - 1))(copies.finish)

    tok = pl.BlockSpec((tm, d), lambda i, j: (i, 0))
    hid = pl.BlockSpec((1, tm, f), lambda i, j: (j, i, 0))
    hid_shape = jax.ShapeDtypeStruct((N_CHIPS, s, f), BF16)
    loss_specs = [tok, pl.BlockSpec((1, d), lambda i, j: (0, 0))] if n_loss else []
    loss_shapes = [jax.ShapeDtypeStruct((s, d), F32), jax.ShapeDtypeStruct((1, d), F32)] if n_loss else []
    return pl.pallas_call(
        body,
        name=name,
        grid=(n_tiles, N_CHIPS),
        in_specs=[
            pl.BlockSpec((tm, d), lambda i, j: (i, 0)),
            pl.BlockSpec((1, d), lambda i, j: (0, 0)),
            pl.BlockSpec((1, 1, d, f), lambda i, j: (j, layer, 0, 0)),
            pl.BlockSpec((1, 1, d, f), lambda i, j: (j, layer, 0, 0)),
            pl.BlockSpec((1, 1, f, d), lambda i, j: (j, layer, 0, 0)),
        ] + [ANY] * n_gather + ([tok] if n_loss else []),
        out_specs=[tok, hid, hid] + [ANY] * n_gather + loss_specs,
        out_shape=[jax.ShapeDtypeStruct((s, d), F32), hid_shape, hid_shape]
        + [jax.ShapeDtypeStruct((N_CHIPS,) + w.shape, w.dtype) for w in gather] + loss_shapes,
        scratch_shapes=[pltpu.VMEM((tm, d), BF16)] + (_gather_scratch(n_gather) if n_gather else []),
        compiler_params=_params(2),
    )(x, gain, wg_s, wu_s, wd_s, *gather, *([target] if n_loss else []))


def _resident(block, layer):
    return pl.BlockSpec(block, lambda i, j: (0, layer, 0, 0), pipeline_mode=pl.Buffered(1))


def _rms_bwd(xv, gain, dh):
    r = lax.rsqrt(jnp.mean(xv * xv, axis=-1, keepdims=True) + EPS)
    xhat = xv * r
    dxhat = dh * gain
    dx = r * (dxhat - xhat * jnp.mean(dxhat * xhat, axis=-1, keepdims=True))
    return dx, jnp.sum(dh * xhat, axis=0, keepdims=True)


def _ffn_bwd(x, dy, gain, gate_s, up_s, wg_s, wu_s, wd_s, layer, name, scatter=()):
    s, d = x.shape
    tm = TOKEN_TILE
    f = wg_s.shape[3]

    n_scatter = len(scatter)
    n_tiles = s // tm

    def body(*refs):
        x_ref, dy_ref, g_ref, gate_ref, up_ref, wg_ref, wu_ref, wd_ref = refs[:8]
        dx_ref, dgain_ref, h_ref, dyb_ref, dg_ref, du_ref, act_ref = refs[8 + n_scatter : 15 + n_scatter]
        acc_s = refs[15 + 2 * n_scatter]
        i, j = pl.program_id(0), pl.program_id(1)
        if n_scatter:
            copies = _ChipScatter(refs[8 : 8 + n_scatter], refs[15 + n_scatter : 15 + 2 * n_scatter], refs[16 + 2 * n_scatter :])
            pl.when((i == 0) & (j == 0))(copies.begin)

        @pl.when((i == 0) & (j == 0))
        def _():
            dgain_ref[...] = jnp.zeros_like(dgain_ref)

        @pl.when(j == 0)
        def _():
            xv = x_ref[...]
            r = lax.rsqrt(jnp.mean(xv * xv, axis=-1, keepdims=True) + EPS)
            h_ref[...] = (xv * r * g_ref[...]).astype(BF16)
            dyb_ref[...] = dy_ref[...].astype(BF16)
            acc_s[...] = jnp.zeros_like(acc_s)

        halves = [slice(0, tm // 2), slice(tm // 2, tm)]
        pre = [(gate_ref[0, r, :].astype(F32), up_ref[0, r, :].astype(F32), _dot_nt(dyb_ref[r, :], wd_ref[j, 0])) for r in halves]
        grads = []
        for r, (gate, up, dact) in zip(halves, pre):
            sig = 1.0 / (1.0 + jnp.exp(-gate))
            silu = gate * sig
            dgate = (dact * up * (sig * (1.0 + gate * (1.0 - sig)))).astype(BF16)
            dup = (dact * silu).astype(BF16)
            act_ref[0, r, :] = (silu * up).astype(BF16)
            dg_ref[0, r, :] = dgate
            du_ref[0, r, :] = dup
            grads.append((dgate, dup))
        for r, (dgate, dup) in zip(halves, grads):
            acc_s[r, :] += _dot_nt(dgate, wg_ref[j, 0]) + _dot_nt(dup, wu_ref[j, 0])

        @pl.when(j == N_CHIPS - 1)
        def _():
            dxn, dgain = _rms_bwd(x_ref[...], g_ref[...], acc_s[...])
            dx_ref[...] = dy_ref[...] + dxn
            dgain_ref[...] += dgain

        if n_scatter:
            pl.when((i == n_tiles - 1) & (j == N_CHIPS - 1))(copies.finish)

    tok = pl.BlockSpec((tm, d), lambda i, j: (i, 0))
    vec = pl.BlockSpec((1, d), lambda i, j: (0, 0))
    hid = pl.BlockSpec((1, tm, f), lambda i, j: (j, i, 0))
    hid_shape = jax.ShapeDtypeStruct((N_CHIPS, s, f), BF16)
    return pl.pallas_call(
        body,
        name=name,
        grid=(n_tiles, N_CHIPS),
        in_specs=[
            tok, tok, vec, hid, hid,
            _resident((N_CHIPS, 1, d, f), layer),
            _resident((N_CHIPS, 1, d, f), layer),
            _resident((N_CHIPS, 1, f, d), layer),
        ] + [ANY] * n_scatter,
        out_specs=[tok, vec, tok, tok, hid, hid, hid] + [ANY] * n_scatter,
        out_shape=[
            jax.ShapeDtypeStruct((s, d), F32),
            jax.ShapeDtypeStruct((1, d), F32),
            jax.ShapeDtypeStruct((s, d), BF16),
            jax.ShapeDtypeStruct((s, d), BF16),
            hid_shape, hid_shape, hid_shape,
        ] + _scatter_shapes(scatter),
        scratch_shapes=[pltpu.VMEM((tm, d), F32)] + (_scatter_scratch(n_scatter) if n_scatter else []),
        compiler_params=_params(2),
    )(x, dy, gain, gate_s, up_s, wg_s, wu_s, wd_s, *scatter)


def _in_proj_bwd(x, dx_res, gain, w_s, layer, proj, q_gain, k_gain, dq, dk, dv, dproj_conv, name):
    s, d = x.shape
    tm = TOKEN_TILE
    n = w_s.shape[3]
    qkv = 3 * ATTN_DIM

    def norm_bwd(xv, head_gain, dy, low):
        _, r = _head_norm(xv, head_gain, low)
        xhat = xv * r
        dxhat = dy * head_gain
        prod = dxhat * xhat
        m_low = jnp.sum(jnp.where(low, prod, 0.0), axis=-1, keepdims=True)
        m_high = jnp.sum(jnp.where(low, 0.0, prod), axis=-1, keepdims=True)
        mean = jnp.where(low, m_low, m_high) / HEAD_DIM
        return r * (dxhat - xhat * mean), jnp.sum(dy * xhat, axis=0, keepdims=True)

    def body(x_ref, r_ref, g_ref, w_ref, p_ref, qg_ref, kg_ref, dq_ref, dk_ref, dv_ref, dpc_ref, dx_ref, dgain_ref, dp_ref, dqg_ref, dkg_ref):
        @pl.when(pl.program_id(0) == 0)
        def _():
            dgain_ref[...] = jnp.zeros_like(dgain_ref)
            dqg_ref[...] = jnp.zeros_like(dqg_ref)
            dkg_ref[...] = jnp.zeros_like(dkg_ref)

        low = lax.broadcasted_iota(jnp.int32, (tm, LANES), 1) < HEAD_DIM
        for g in range(ATTN_DIM // LANES):
            cq = slice(LANES * g, LANES * (g + 1))
            ck = slice(ATTN_DIM + LANES * g, ATTN_DIM + LANES * (g + 1))
            cv = slice(2 * ATTN_DIM + LANES * g, 2 * ATTN_DIM + LANES * (g + 1))
            dxq, dgq = norm_bwd(p_ref[:, cq], qg_ref[...], dq_ref[:, cq] * Q_SCALE, low)
            dxk, dgk = norm_bwd(p_ref[:, ck], kg_ref[...], dk_ref[:, cq], low)
            dp_ref[:, cq] = dxq.astype(BF16)
            dp_ref[:, ck] = dxk.astype(BF16)
            dp_ref[:, cv] = dv_ref[:, cq].astype(BF16)
            dqg_ref[:, cq] += dgq
            dkg_ref[:, cq] += dgk
        dp_ref[:, qkv:] = dpc_ref[...]

        dh = _dot_nt(dp_ref[:, 0:n], w_ref[0, 0])
        for j in range(1, N_CHIPS):
            dh = dh + _dot_nt(dp_ref[:, j * n : (j + 1) * n], w_ref[j, 0])
        dxn, dgain = _rms_bwd(x_ref[...], g_ref[...], dh)
        dx_ref[...] = r_ref[...] + dxn
        dgain_ref[...] += dgain

    tok = pl.BlockSpec((tm, d), lambda i: (i, 0))
    vec = pl.BlockSpec((1, d), lambda i: (0, 0))
    grad_spec = pl.BlockSpec((tm, ATTN_DIM), lambda i: (i, 0))
    gain_spec = pl.BlockSpec((1, LANES), lambda i: (0, 0))
    sum_spec = pl.BlockSpec((1, ATTN_DIM), lambda i: (0, 0))
    return pl.pallas_call(
        body,
        name=name,
        grid=(s // tm,),
        in_specs=[
            tok, tok, vec, pl.BlockSpec((N_CHIPS, 1, d, n), lambda i: (0, layer, 0, 0)),
            pl.BlockSpec((tm, qkv), lambda i: (i, 0)), gain_spec, gain_spec, grad_spec, grad_spec, grad_spec,
            pl.BlockSpec((tm, N_CHIPS * n - qkv), lambda i: (i, 1)),
        ],
        out_specs=[tok, vec, pl.BlockSpec((tm, N_CHIPS * n), lambda i: (i, 0)), sum_spec, sum_spec],
        out_shape=[
            jax.ShapeDtypeStruct((s, d), F32),
            jax.ShapeDtypeStruct((1, d), F32),
            jax.ShapeDtypeStruct((s, N_CHIPS * n), BF16),
            jax.ShapeDtypeStruct((1, ATTN_DIM), F32),
            jax.ShapeDtypeStruct((1, ATTN_DIM), F32),
        ],
        compiler_params=_params(1),
    )(x, dx_res, gain, w_s, proj, q_gain, k_gain, dq, dk, dv, dproj_conv)


def _wgrad(a, b, a_spec, b_spec, n_blocks, k, n, name):
    n_tiles = a.shape[-2] // min(WGRAD_TILE, a.shape[-2])

    def body(a_ref, b_ref, o_ref):
        @pl.when(pl.program_id(1) == 0)
        def _():
            o_ref[...] = jnp.zeros_like(o_ref)

        av = a_ref[0] if len(a_ref.shape) == 3 else a_ref[...]
        bv = b_ref[0] if len(b_ref.shape) == 3 else b_ref[...]
        o_ref[0] += _dot_tn(av, bv)

    return pl.pallas_call(
        body,
        name=name,
        grid=(n_blocks, n_tiles),
        in_specs=[a_spec, b_spec],
        out_specs=pl.BlockSpec((1, k, n), lambda j, i: (j, 0, 0)),
        out_shape=jax.ShapeDtypeStruct((n_blocks, k, n), F32),
        compiler_params=_params(2),
    )(a, b)


def _wgrad_out(attn, conv, dxb, rows, name):
    s, d = dxb.shape
    tw = min(WGRAD_TILE // 2, s)

    def body(a_ref, c_ref, b_ref, o_ref):
        @pl.when(pl.program_id(0) == 0)
        def _():
            o_ref[...] = jnp.zeros_like(o_ref)

        bv = b_ref[...]
        for j in range(N_CHIPS):
            src = a_ref if j < 2 else c_ref
            o_ref[j] += _dot_tn(src[:, (j % 2) * rows : (j % 2 + 1) * rows], bv)

    return pl.pallas_call(
        body,
        name=name,
        grid=(s // tw,),
        in_specs=[
            pl.BlockSpec((tw, ATTN_DIM), lambda i: (i, 0)),
            pl.BlockSpec((tw, CONV_DIM), lambda i: (i, 0)),
            pl.BlockSpec((tw, d), lambda i: (i, 0)),
        ],
        out_specs=pl.BlockSpec((N_CHIPS, rows, d), lambda i: (0, 0, 0)),
        out_shape=jax.ShapeDtypeStruct((N_CHIPS, rows, d), F32),
        compiler_params=_params(1),
    )(attn, conv, dxb)


def _mesh_position():
    return lax.axis_index("x"), lax.axis_index("y"), lax.axis_index("c")


def _other_chips(x, y):
    return [(1 - x, y), (x, 1 - y), (1 - x, 1 - y)]


def _half_rows(ref_rows, c):
    half = ref_rows // 2
    return pl.ds(c * half, half)


class _WeightGather:
    def __init__(self, ins, outs, sems):
        self.ins, self.outs = ins, outs
        send_sems, recv_sems, pass_send_sems, pass_recv_sems, self.local_sems = sems
        self.ici, self.d2d = (send_sems, recv_sems), (pass_send_sems, pass_recv_sems)
        self.x, self.y, self.c = _mesh_position()
        self.me = 2 * self.x + self.y
        self.sibling = (self.x, self.y, 1 - self.c)
        self.chips = _other_chips(self.x, self.y)

    def _copy(self, t, k, chip_index, core, to, sems, src=None):
        dst = self.outs[t].at[chip_index, :, _half_rows(self.ins[t].shape[1], core), :]
        return pltpu.make_async_remote_copy(
            src_ref=dst if src is None else src, dst_ref=dst, send_sem=sems[0].at[t, k], recv_sem=sems[1].at[t, k],
            device_id=to, device_id_type=MESH_ID,
        )

    def _own(self, t):
        return pltpu.make_async_copy(self.ins[t], self.outs[t].at[self.me], self.local_sems.at[t])

    def _sends(self):
        for t in range(len(self.ins)):
            mine = self.ins[t].at[:, _half_rows(self.ins[t].shape[1], self.c), :]
            for k, (px, py) in enumerate(self.chips):
                yield self._copy(t, k, self.me, self.c, (px, py, self.c), self.ici, src=mine)

    def _passes(self, core, sems):
        for t in range(len(self.ins)):
            for k, (px, py) in enumerate(self.chips):
                yield self._copy(t, k, 2 * px + py, core, self.sibling, sems)

    def begin(self):
        for t in range(len(self.ins)):
            self._own(t).start()
        for cp in self._sends():
            cp.start()

    def relay(self):
        for arrived, onward in zip(self._passes(self.c, self.ici), self._passes(self.c, self.d2d)):
            arrived.wait_recv()
            onward.start()

    def finish(self):
        for cp in self._passes(1 - self.c, self.d2d):
            cp.wait_recv()
        for cp in list(self._sends()) + list(self._passes(self.c, self.d2d)):
            cp.wait_send()
        for t in range(len(self.ins)):
            self._own(t).wait()


def _gather_scratch(n):
    sems = pltpu.SemaphoreType.DMA((n, N_CHIPS - 1))
    return [sems, sems, sems, sems, pltpu.SemaphoreType.DMA((n,))]


def _gather_weights(shards):
    n = len(shards)

    def body(*refs):
        gather = _WeightGather(refs[:n], refs[n : 2 * n], refs[2 * n :])
        gather.begin()
        gather.relay()
        gather.finish()

    return pl.pallas_call(
        body,
        name="gather_weights",
        in_specs=[ANY] * n,
        out_specs=[ANY] * n,
        out_shape=[jax.ShapeDtypeStruct((N_CHIPS,) + w.shape, w.dtype) for w in shards],
        scratch_shapes=_gather_scratch(n),
    )(*shards)


def _swap_halves(grads, tag):
    n = len(grads)

    def body(*refs):
        ins, outs = refs[:n], refs[n : 2 * n]
        send_sems, recv_sems = refs[2 * n :]
        x, y, c = _mesh_position()
        copies = []
        for t in range(n):
            copies.append(pltpu.make_async_remote_copy(
                src_ref=ins[t].at[:, _half_rows(ins[t].shape[1], 1 - c), :], dst_ref=outs[t],
                send_sem=send_sems.at[t], recv_sem=recv_sems.at[t], device_id=(x, y, 1 - c), device_id_type=MESH_ID,
            ))
            copies[-1].start()
        for cp in copies:
            cp.wait()

    sems = pltpu.SemaphoreType.DMA((n,))
    return pl.pallas_call(
        body,
        name=f"swap_halves_{tag}",
        in_specs=[ANY] * n,
        out_specs=[ANY] * n,
        out_shape=[jax.ShapeDtypeStruct((g.shape[0], g.shape[1] // 2, g.shape[2]), g.dtype) for g in grads],
        scratch_shapes=[sems, sems],
    )(*grads)


class _ChipScatter:
    def __init__(self, ins, outs, sems):
        self.ins, self.outs = ins, outs
        self.send_sems, self.recv_sems = sems
        self.x, self.y, self.c = _mesh_position()

    def _copies(self):
        for t in range(len(self.ins)):
            for k, (px, py) in enumerate(_other_chips(self.x, self.y)):
                yield pltpu.make_async_remote_copy(
                    src_ref=self.ins[t].at[2 * px + py], dst_ref=self.outs[t].at[k],
                    send_sem=self.send_sems.at[t, k], recv_sem=self.recv_sems.at[t, k],
                    device_id=(px, py, self.c), device_id_type=MESH_ID,
                )

    def begin(self):
        for cp in self._copies():
            cp.start()

    def finish(self):
        for cp in self._copies():
            cp.wait()


def _scatter_scratch(n):
    sems = pltpu.SemaphoreType.DMA((n, N_CHIPS - 1))
    return [sems, sems]


def _scatter_shapes(parts):
    return [jax.ShapeDtypeStruct((N_CHIPS - 1,) + p.shape[1:], p.dtype) for p in parts]


def _scatter_to_chips(parts, tag):
    n = len(parts)

    def body(*refs):
        copies = _ChipScatter(refs[:n], refs[n : 2 * n], refs[2 * n :])
        copies.begin()
        copies.finish()

    return pl.pallas_call(
        body,
        name=f"scatter_to_chips_{tag}",
        in_specs=[ANY] * n,
        out_specs=[ANY] * n,
        out_shape=_scatter_shapes(parts),
        scratch_shapes=_scatter_scratch(n),
    )(*parts)


def _join_halves(shards):
    n = len(shards)

    def body(*refs):
        outs = refs[n : 2 * n]
        send_sems, recv_sems = refs[2 * n :]
        x, y, c = _mesh_position()
        copies = []
        for t in range(n):
            mine = outs[t].at[:, _half_rows(outs[t].shape[1], c), :]
            copies.append(pltpu.make_async_remote_copy(
                src_ref=mine, dst_ref=mine, send_sem=send_sems.at[t], recv_sem=recv_sems.at[t],
                device_id=(x, y, 1 - c), device_id_type=MESH_ID,
            ))
            copies[-1].start()
        for cp in copies:
            cp.wait()

    sems = pltpu.SemaphoreType.DMA((n,))
    return pl.pallas_call(
        body,
        name="join_halves",
        in_specs=[ANY] * n,
        out_specs=[ANY] * n,
        out_shape=[jax.ShapeDtypeStruct(g.shape, g.dtype) for g in shards],
        input_output_aliases={t: t for t in range(n)},
        scratch_shapes=[sems, sems],
    )(*shards)


def _gather_small(pack):
    def body(p_ref, o_ref, send_sems, recv_sems, local_sem):
        x, y, c = _mesh_position()
        own = pltpu.make_async_copy(p_ref, o_ref.at[4 * x + 2 * y + c], local_sem)
        own.start()
        copies = []
        for k in range(1, N_DEV):
            px, py, pc = x ^ (k >> 2), y ^ ((k >> 1) & 1), c ^ (k & 1)
            send = pltpu.make_async_remote_copy(
                src_ref=p_ref, dst_ref=o_ref.at[4 * x + 2 * y + c], send_sem=send_sems.at[k - 1], recv_sem=recv_sems.at[k - 1],
                device_id=(px, py, pc), device_id_type=MESH_ID,
            )
            send.start()
            copies.append((send, 4 * px + 2 * py + pc))
        for send, peer_slot in copies:
            send.wait_send()
        for k in range(1, N_DEV):
            px, py, pc = x ^ (k >> 2), y ^ ((k >> 1) & 1), c ^ (k & 1)
            pltpu.make_async_remote_copy(
                src_ref=p_ref, dst_ref=o_ref.at[4 * px + 2 * py + pc], send_sem=send_sems.at[k - 1], recv_sem=recv_sems.at[k - 1],
                device_id=(px, py, pc), device_id_type=MESH_ID,
            ).wait_recv()
        own.wait()

    sems = pltpu.SemaphoreType.DMA((N_DEV - 1,))
    return pl.pallas_call(
        body,
        name="gather_small",
        in_specs=[VMEM_SPEC],
        out_specs=VMEM_SPEC,
        out_shape=jax.ShapeDtypeStruct((N_DEV,) + pack.shape, pack.dtype),
        scratch_shapes=[sems, sems, pltpu.SemaphoreType.DMA],
    )(pack)


def _row_tile(rows):
    for tile in range(min(rows, 512) // 8 * 8, 0, -8):
        if rows % tile == 0:
            return tile
    return rows


def _add_half(grad, received, half_index, name):
    slots, h, cdim = received.shape
    tile = _row_tile(h)
    per_half = h // tile

    def body(c_ref, g_ref, r_ref, o_ref, ob_ref):
        total = g_ref[...] + r_ref[...]
        o_ref[...] = total
        ob_ref[...] = total.astype(BF16)

    block = pl.BlockSpec((1, tile, cdim), lambda j, i, c: (j, i, 0))
    grid_spec = pltpu.PrefetchScalarGridSpec(
        num_scalar_prefetch=1,
        grid=(slots, per_half),
        in_specs=[pl.BlockSpec((1, tile, cdim), lambda j, i, c: (j, c[0] * per_half + i, 0)), block],
        out_specs=[block, block],
    )
    return pl.pallas_call(
        body, name=name, grid_spec=grid_spec,
        out_shape=[jax.ShapeDtypeStruct(received.shape, F32), jax.ShapeDtypeStruct(received.shape, BF16)],
        compiler_params=_params(2),
    )(half_index, grad, received)


def _add_chips(part, received, chip_index, core_index, layer, n_layers, shard, name):
    _, h, cdim = part.shape
    tile = _row_tile(h)
    per_half = h // tile

    def body(chip_ref, core_ref, p_ref, r_ref, *rest):
        o_ref = rest[-1]
        o_ref[0] = ((p_ref[0] + r_ref[0].astype(F32)) + r_ref[1].astype(F32)) + r_ref[2].astype(F32)

    in_specs = [
        pl.BlockSpec((1, tile, cdim), lambda i, chip, core: (chip[0], i, 0)),
        pl.BlockSpec((N_CHIPS - 1, tile, cdim), lambda i, chip, core: (0, i, 0)),
    ]
    operands = [chip_index, core_index, part, received]
    aliases = {}
    if shard is not None:
        in_specs.append(ANY)
        operands.append(shard)
        aliases = {4: 0}
    grid_spec = pltpu.PrefetchScalarGridSpec(
        num_scalar_prefetch=2,
        grid=(per_half,),
        in_specs=in_specs,
        out_specs=pl.BlockSpec((1, tile, cdim), lambda i, chip, core: (layer, core[0] * per_half + i, 0)),
    )
    return pl.pallas_call(
        body, name=name, grid_spec=grid_spec, out_shape=jax.ShapeDtypeStruct((n_layers, 2 * h, cdim), F32),
        input_output_aliases=aliases, compiler_params=_params(1),
    )(*operands)


def _adamw(w, g, m, v, name):
    rows, cdim = w.shape
    tile = _row_tile(rows)

    def body(w_ref, g_ref, m_ref, v_ref, d_ref, nm_ref, nv_ref, go_ref):
        gv = g_ref[...]
        go_ref[...] = gv
        nm = ADAM_B1 * m_ref[...] + (1.0 - ADAM_B1) * gv
        nv = ADAM_B2 * v_ref[...] + (1.0 - ADAM_B2) * (gv * gv)
        m_hat = nm / (1.0 - ADAM_B1 ** ADAM_STEP)
        v_hat = nv / (1.0 - ADAM_B2 ** ADAM_STEP)
        d_ref[...] = -ADAM_LR * (m_hat / (jnp.sqrt(v_hat) + ADAM_EPS) + ADAM_WD * w_ref[...])
        nm_ref[...] = nm
        nv_ref[...] = nv

    spec = pl.BlockSpec((tile, cdim), lambda i: (i, 0))
    shape = jax.ShapeDtypeStruct((rows, cdim), F32)
    return pl.pallas_call(
        body, name=name, grid=(rows // tile,), in_specs=[spec] * 4, out_specs=[spec] * 4, out_shape=[shape] * 4,
        compiler_params=_params(1),
    )(w, g, m, v)


SMALL_ROWS, SMALL_COLS = 24, 1024
ROW_NORM_MIX, ROW_NORM_FFN, ROW_LOSS, ROW_Q_NORM, ROW_K_NORM, ROW_CONV = 0, 2, 4, 8, 10, 16


def _sum_small(gathered):
    def body(g_ref, o_ref, heads_ref, lanes_ref):
        total = g_ref[0]
        for dev in range(1, N_DEV):
            total = total + g_ref[dev]
        o_ref[...] = total
        heads = o_ref[8:16, 0:LANES]
        for grp in range(1, ATTN_DIM // LANES):
            heads = heads + o_ref[8:16, grp * LANES : (grp + 1) * LANES]
        heads_ref[...] = heads + pltpu.roll(heads, HEAD_DIM, 1)
        lanes_ref[...] = jnp.broadcast_to(jnp.sum(o_ref[0:8, :], axis=-1, keepdims=True), (8, LANES))

    return pl.pallas_call(
        body,
        name="sum_small",
        in_specs=[VMEM_SPEC],
        out_specs=[VMEM_SPEC] * 3,
        out_shape=[jax.ShapeDtypeStruct((SMALL_ROWS, SMALL_COLS), F32), jax.ShapeDtypeStruct((8, LANES), F32), jax.ShapeDtypeStruct((8, LANES), F32)],
    )(gathered)


def _pad_rows(a, rows):
    return jnp.pad(a, ((0, rows - a.shape[0]), (0, 0)))


def _pad_to(a, rows, cols):
    return jnp.pad(a, ((0, rows - a.shape[0]), (0, cols - a.shape[1])))


def _conv_taps(conv_s):
    return jnp.transpose(conv_s[:, 0, 0:8], (1, 0, 2)).reshape(8, -1)


class _GradExchange:
    def __init__(self, chip_index, core_index, n_layers):
        self.chip_index, self.core_index, self.n_layers = chip_index, core_index, n_layers
        self.shards = {}
        self.pending = None

    def offer(self, layer, grads):
        assert self.pending is None
        names = list(grads)
        received = _swap_halves([grads[k] for k in names], f"{'_'.join(names)}_{layer}")
        parts = [_add_half(grads[k], r, self.core_index, f"add_half_{k}_{layer}") for k, r in zip(names, received)]
        self.pending = (layer, names, [p32 for p32, _ in parts], [p16 for _, p16 in parts])

    def payload(self):
        return () if self.pending is None else tuple(self.pending[3])

    def take(self, received):
        layer, names, parts, _ = self.pending
        self.pending = None
        for k, p, r in zip(names, parts, received):
            self.shards[k] = _add_chips(
                p, r, self.chip_index, self.core_index, layer, self.n_layers, self.shards.get(k), f"add_chips_{k}_{layer}")

    def finish(self):
        if self.pending is not None:
            layer, names = self.pending[0], self.pending[1]
            self.take(_scatter_to_chips(list(self.pending[3]), f"{'_'.join(names)}_{layer}"))
        return dict(zip(BIG, _join_halves([self.shards[k] for k in BIG])))


def _local_step(x, target, norm_mix, q_norm, k_norm, norm_ffn, layer_weights, exchange=None):
    layer_weights = list(layer_weights)

    def carrying(kernel_fn, n_out, *args):
        if exchange is None or exchange.pending is None:
            return kernel_fn(*args)
        out = kernel_fn(*args, scatter=exchange.payload())
        exchange.take(out[n_out:])
        return out[:n_out]

    n_layers = norm_mix.shape[0]
    s, d = x.shape
    tw = min(WGRAD_TILE, s)
    n_in = layer_weights[0][0].shape[-1]
    f = layer_weights[0][2].shape[-1]
    saved = []
    for l in range(n_layers):
        weights = list(layer_weights[l])
        q_gain = jnp.tile(q_norm[l][None, :], (1, 2))
        k_gain = jnp.tile(k_norm[l][None, :], (1, 2))
        h1, proj, qn, kn, vb = _in_proj(x, norm_mix[l][None, :], weights[0], 0, q_gain, k_gain, f"in_proj_{l}")
        missing = [n for n, w in enumerate(weights) if w.ndim == 3]
        if missing:
            attn, *arrived = _attn_fwd(qn, kn, vb, f"attn_fwd_{l}", gather=tuple(weights[n] for n in missing))
            for n, w in zip(missing, arrived):
                weights[n] = w
            layer_weights[l] = tuple(weights)
        else:
            attn = _attn_fwd(qn, kn, vb, f"attn_fwd_{l}")
        _, wout_s, wg_s, wu_s, wd_s, conv_s = weights
        taps = _conv_taps(conv_s)
        x_mid, conv = _out_proj(x, attn, proj, taps, wout_s, 0, f"out_proj_{l}")
        pending = ()
        if l + 1 < n_layers and all(w.ndim == 3 for w in layer_weights[l + 1]):
            pending = tuple(layer_weights[l + 1])
        x_out, gate, up, *more = _ffn_fwd(
            x_mid, norm_ffn[l][None, :], wg_s, wu_s, wd_s, 0, f"ffn_fwd_{l}", gather=pending, target=target if l == n_layers - 1 else None)
        if pending:
            layer_weights[l + 1] = tuple(more[: len(pending)])
        if l == n_layers - 1:
            dy, loss_lanes = more[len(pending) :]
        saved.append(dict(x=x, h1=h1, proj=proj, qn=qn, kn=kn, vb=vb, attn=attn, conv=conv, x_mid=x_mid, q_gain=q_gain, k_gain=k_gain,
                          gate=gate, up=up, taps=taps))
        x = x_out

    grads = [None] * n_layers
    for l in reversed(range(n_layers)):
        sv = saved[l]
        win_s, wout_s, wg_s, wu_s, wd_s, _ = layer_weights[l]
        dx_mid, d_norm_ffn, h2, dyb, dgate, dup, act = carrying(
            _ffn_bwd, 7, sv["x_mid"], dy, norm_ffn[l][None, :], sv["gate"], sv["up"], wg_s, wu_s, wd_s, 0, f"ffn_bwd_{l}")
        tok2 = pl.BlockSpec((tw, d), lambda j, i: (i, 0))
        hid = pl.BlockSpec((1, tw, f), lambda j, i: (j, i, 0))
        d_wg = _wgrad(h2, dgate, tok2, hid, N_CHIPS, d, f, f"wgrad_gate_{l}")
        d_wu = _wgrad(h2, dup, tok2, hid, N_CHIPS, d, f, f"wgrad_up_{l}")
        d_wd = _wgrad(act, dyb, hid, tok2, N_CHIPS, f, d, f"wgrad_down_{l}")
        if exchange is not None:
            exchange.offer(l, dict(w_gate=d_wg, w_up=d_wu, w_down=d_wd))
        d_attn, d_conv, dxb = _out_proj_bwd(dx_mid, wout_s, 0, f"out_proj_bwd_{l}")
        d_wout = _wgrad_out(sv["attn"], sv["conv"], dxb, wout_s.shape[2], f"wgrad_out_{l}")
        dq, dk, dv = carrying(_attn_bwd, 3, sv["qn"], sv["kn"], sv["vb"], d_attn, f"attn_bwd_{l}")
        dproj_conv, d_conv_w = _conv_bwd(sv["proj"], sv["taps"], d_conv, f"conv_bwd_{l}")
        dy, d_norm_mix, dproj, d_qg, d_kg = _in_proj_bwd(
            sv["x"], dx_mid, norm_mix[l][None, :], win_s, 0, sv["proj"], sv["q_gain"], sv["k_gain"], dq, dk, dv, dproj_conv, f"in_proj_bwd_{l}")
        d_win = _wgrad(sv["h1"], dproj, tok2, pl.BlockSpec((tw, n_in), lambda j, i: (i, j)), N_CHIPS, d, n_in, f"wgrad_in_{l}")
        if exchange is not None:
            exchange.offer(l, dict(w_in=d_win, w_out=d_wout))
        grads[l] = dict(norm_mix=d_norm_mix, norm_ffn=d_norm_ffn, q_norm=d_qg, k_norm=d_kg, conv_w=d_conv_w,
                        w_in=d_win, w_out=d_wout, w_gate=d_wg, w_up=d_wu, w_down=d_wd)
    return loss_lanes, dy, grads


BIG = ("w_in", "w_out", "w_gate", "w_up", "w_down")


def kernel(x, norm_mix, w_in, q_norm, k_norm, conv_w, w_out, norm_ffn, w_gate, w_up, w_down, loss_target, m_norm_mix, m_w_in, m_q_norm, m_k_norm, m_conv_w, m_w_out, m_norm_ffn, m_w_gate, m_w_up, m_w_down, v_norm_mix, v_w_in, v_q_norm, v_k_norm, v_conv_w, v_w_out, v_norm_ffn, v_w_gate, v_w_up, v_w_down):
    n_layers = norm_mix.shape[0]
    weights = dict(w_in=w_in, w_out=w_out, w_gate=w_gate, w_up=w_up, w_down=w_down)
    moments_m = dict(w_in=m_w_in, w_out=m_w_out, w_gate=m_w_gate, w_up=m_w_up, w_down=m_w_down)
    moments_v = dict(w_in=v_w_in, w_out=v_w_out, w_gate=v_w_gate, w_up=v_w_up, w_down=v_w_down)
    cx, cy, cc = _mesh_position()
    chip_index = (2 * cx + cy).astype(jnp.int32).reshape(1)
    core_index = cc.astype(jnp.int32).reshape(1)

    conv_pad = jnp.pad(conv_w, ((0, 0), (0, 16 - conv_w.shape[1]), (0, 0)))

    def shards_of(layer):
        return [weights[k][layer : layer + 1].astype(BF16) for k in BIG] + [conv_pad[layer : layer + 1]]

    first = shards_of(0)
    layer_weights = [tuple(_gather_weights(first[:1])) + tuple(first[1:])] + [tuple(shards_of(layer)) for layer in range(1, n_layers)]

    exchange = _GradExchange(chip_index, core_index, n_layers)
    loss_lanes, grad_x, grads = _local_step(
        x[0], loss_target[0], norm_mix, q_norm, k_norm, norm_ffn, layer_weights, exchange)

    big_grads = exchange.finish()

    def lanes(a):
        return _pad_to(a, a.shape[0], SMALL_COLS)

    def tile_of(*groups):
        return _pad_rows(jnp.concatenate([lanes(jnp.concatenate(g, axis=0)) for g in groups], axis=0), 8)

    layers = range(n_layers)
    pack = jnp.concatenate([
        tile_of([grads[l]["norm_mix"] for l in layers], [grads[l]["norm_ffn"] for l in layers], [loss_lanes]),
        tile_of([grads[l]["q_norm"] for l in layers], [grads[l]["k_norm"] for l in layers]),
        tile_of([grads[l]["conv_w"][0:3] for l in layers]),
    ], axis=0)
    small, small_heads, small_lanes = _sum_small(_gather_small(pack))
    loss = small_lanes[ROW_LOSS, 0]
    d_model = norm_mix.shape[1]
    conv_cols = conv_w.shape[2]
    conv_all = small[ROW_CONV : ROW_CONV + 3 * n_layers, 0:CONV_DIM].reshape(n_layers, 3, CONV_DIM)
    small_grads = dict(
        norm_mix=small[ROW_NORM_MIX : ROW_NORM_MIX + n_layers, 0:d_model],
        norm_ffn=small[ROW_NORM_FFN : ROW_NORM_FFN + n_layers, 0:d_model],
        q_norm=small_heads[ROW_Q_NORM - 8 : ROW_Q_NORM - 8 + n_layers, 0:HEAD_DIM],
        k_norm=small_heads[ROW_K_NORM - 8 : ROW_K_NORM - 8 + n_layers, 0:HEAD_DIM],
        conv_w=lax.dynamic_slice_in_dim(conv_all, (2 * cx + cy) * conv_cols, conv_cols, axis=2),
    )

    out_grad, out_delta, out_m, out_v = {}, {}, {}, {}
    for k in BIG:
        shape = weights[k].shape
        view = (shape[0] * shape[1], shape[2])
        g = big_grads[k]
        delta, new_m, new_v, g = _adamw(weights[k].reshape(view), g.reshape(view), moments_m[k].reshape(view), moments_v[k].reshape(view), f"adamw_{k}")
        out_grad[k], out_delta[k], out_m[k], out_v[k] = g.reshape(shape), delta.reshape(shape), new_m.reshape(shape), new_v.reshape(shape)

    small_w = dict(norm_mix=norm_mix, norm_ffn=norm_ffn, q_norm=q_norm, k_norm=k_norm, conv_w=conv_w)
    small_m = dict(norm_mix=m_norm_mix, norm_ffn=m_norm_ffn, q_norm=m_q_norm, k_norm=m_k_norm, conv_w=m_conv_w)
    small_v = dict(norm_mix=v_norm_mix, norm_ffn=v_norm_ffn, q_norm=v_q_norm, k_norm=v_k_norm, conv_w=v_conv_w)
    order = ("norm_mix", "norm_ffn", "q_norm", "k_norm", "conv_w")

    def packed(tree):
        parts2 = [_pad_to(tree[k].reshape(-1, tree[k].shape[-1]), tree[k].reshape(-1, tree[k].shape[-1]).shape[0], SMALL_COLS) for k in order]
        return _pad_rows(jnp.concatenate(parts2, axis=0), SMALL_ROWS)

    delta_p, m_p, v_p, _ = _adamw(packed(small_w), packed(small_grads), packed(small_m), packed(small_v), "adamw_small")
    row = 0
    for k in order:
        shape = small_w[k].shape
        n_rows = 1
        for dim in shape[:-1]:
            n_rows *= dim
        cut = (slice(row, row + n_rows), slice(0, shape[-1]))
        out_grad[k] = small_grads[k]
        out_delta[k], out_m[k], out_v[k] = delta_p[cut].reshape(shape), m_p[cut].reshape(shape), v_p[cut].reshape(shape)
        row += n_rows

    names_out = ("norm_mix", "w_in", "q_norm", "k_norm", "conv_w", "w_out", "norm_ffn", "w_gate", "w_up", "w_down")
    return (loss, grad_x[None], *[out_grad[k] for k in names_out], *[out_delta[k] for k in names_out],
            *[out_m[k] for k in names_out], *[out_v[k] for k in names_out])
```

```python
import jax
import jax.numpy as jnp
from jax import lax
from jax.experimental import pallas as pl
from jax.experimental.pallas import tpu as pltpu

F32 = jnp.float32
BF16 = jnp.bfloat16

EPS = 1e-6
HEAD_DIM = 64
LANES = 128
ATTN_DIM = 512
CONV_DIM = 512
N_CHIPS = 4
N_DEV = 8
Q_SCALE = HEAD_DIM ** -0.5
ATTN_Q_TILE = 256
ATTN_TILE = 256
TOKEN_TILE = 512
WGRAD_TILE = 4096
FFN_FWD_TILE = 1024
FFN_CHUNK = 256
VMEM_LIMIT = 56 * 1024 * 1024

ADAM_LR = 0.001
ADAM_B1 = 0.9
ADAM_B2 = 0.999
ADAM_EPS = 1e-08
ADAM_WD = 0.01
ADAM_STEP = 10

MESH_ID = pl.DeviceIdType.MESH
ANY = pl.BlockSpec(memory_space=pl.ANY)
VMEM_SPEC = pl.BlockSpec(memory_space=pltpu.VMEM)


def _params(n_axes):
    return pltpu.CompilerParams(dimension_semantics=("arbitrary",) * n_axes, vmem_limit_bytes=VMEM_LIMIT)


def _dot(a, b):
    return jnp.dot(a, b, preferred_element_type=F32)


def _dot_nt(a, b):
    return lax.dot_general(a, b, (((1,), (1,)), ((), ())), preferred_element_type=F32)


def _dot_tn(a, b):
    return lax.dot_general(a, b, (((0,), (0,)), ((), ())), preferred_element_type=F32)


SCORE_MAX = 80.0
UNDERFLOW_EXIT = 90.0


def _scores(q, k):
    return jnp.minimum(_dot_nt(q, k), SCORE_MAX)


def _softplus(z):
    return jnp.log(1.0 + jnp.exp(z))


def _head_norm(xv, gain, low):
    sq = xv * xv
    s_low = jnp.sum(jnp.where(low, sq, 0.0), axis=-1, keepdims=True)
    s_high = jnp.sum(jnp.where(low, 0.0, sq), axis=-1, keepdims=True)
    r = jnp.where(low, lax.rsqrt(s_low / HEAD_DIM + EPS), lax.rsqrt(s_high / HEAD_DIM + EPS))
    return xv * r * gain, r


def _in_proj(x, gain, w_s, layer, q_gain, k_gain, name):
    s, d = x.shape
    n_blocks, _, _, n = w_s.shape
    tm = TOKEN_TILE

    def body(x_ref, g_ref, w_ref, qg_ref, kg_ref, h_ref, o_ref, q_ref, k_ref, v_ref):
        xv = x_ref[...]
        r = lax.rsqrt(jnp.mean(xv * xv, axis=-1, keepdims=True) + EPS)
        h = (xv * r * g_ref[...]).astype(BF16)
        h_ref[...] = h
        for j in range(n_blocks):
            o_ref[:, j * n : (j + 1) * n] = _dot(h, w_ref[j, 0])
        low = lax.broadcasted_iota(jnp.int32, (tm, LANES), 1) < HEAD_DIM
        for g in range(ATTN_DIM // LANES):
            cq = slice(LANES * g, LANES * (g + 1))
            ck = slice(ATTN_DIM + LANES * g, ATTN_DIM + LANES * (g + 1))
            cv = slice(2 * ATTN_DIM + LANES * g, 2 * ATTN_DIM + LANES * (g + 1))
            qn, _ = _head_norm(o_ref[:, cq], qg_ref[...], low)
            kn, _ = _head_norm(o_ref[:, ck], kg_ref[...], low)
            q_ref[:, cq] = (qn * Q_SCALE).astype(BF16)
            k_ref[:, cq] = kn.astype(BF16)
            v_ref[:, cq] = o_ref[:, cv].astype(BF16)

    head_spec = pl.BlockSpec((tm, ATTN_DIM), lambda i: (i, 0))
    head_shape = jax.ShapeDtypeStruct((s, ATTN_DIM), BF16)
    gain_spec = pl.BlockSpec((1, LANES), lambda i: (0, 0))
    return pl.pallas_call(
        body,
        name=name,
        grid=(s // tm,),
        in_specs=[
            pl.BlockSpec((tm, d), lambda i: (i, 0)),
            pl.BlockSpec((1, d), lambda i: (0, 0)),
            pl.BlockSpec((n_blocks, 1, d, n), lambda i: (0, layer, 0, 0)),
            gain_spec, gain_spec,
        ],
        out_specs=[pl.BlockSpec((tm, d), lambda i: (i, 0)), pl.BlockSpec((tm, n_blocks * n), lambda i: (i, 0)), head_spec, head_spec, head_spec],
        out_shape=[jax.ShapeDtypeStruct((s, d), BF16), jax.ShapeDtypeStruct((s, n_blocks * n), F32), head_shape, head_shape, head_shape],
        compiler_params=_params(1),
    )(x, gain, w_s, q_gain, k_gain)


def _attn_tile_consts(t):
    row = lax.broadcasted_iota(jnp.int32, (t, t), 0)
    col = lax.broadcasted_iota(jnp.int32, (t, t), 1)
    return row, col


def _triangle_sum(v, triangle):
    return _dot(v.astype(BF16), triangle)


def _attn_fwd(qn, kn, vb, name, gather=()):
    s = qn.shape[0]
    t = min(ATTN_TILE, s)
    tq = min(ATTN_Q_TILE, t)
    per_key_tile = t // tq
    n_gather = len(gather)
    n_pairs, n_blocks = ATTN_DIM // LANES, s // tq

    def body(*refs):
        q_ref, k_ref, v_ref = refs[:3]
        o_ref = refs[3 + n_gather]
        if n_gather:
            copies = _WeightGather(refs[3 : 3 + n_gather], refs[4 + n_gather : 4 + 2 * n_gather], refs[4 + 2 * n_gather :])
            first = (pl.program_id(0) == 0) & (pl.program_id(1) == 0)
            pl.when(first)(copies.begin)
            pl.when((pl.program_id(0) == n_pairs - 1) & (pl.program_id(1) == 0))(copies.relay)
        i = pl.program_id(1) // per_key_tile
        low = lax.broadcasted_iota(jnp.int32, (tq, LANES), 1) < HEAD_DIM
        row, col = _attn_tile_consts(t)
        suffix = (row > col).astype(BF16)
        first_row = (pl.program_id(1) % per_key_tile) * tq
        causal = lax.broadcasted_iota(jnp.int32, (tq, t), 1) < lax.broadcasted_iota(jnp.int32, (tq, t), 0) + first_row
        q = q_ref[...]
        zero_q = jnp.zeros_like(q)
        qh = (jnp.where(low, q, zero_q), jnp.where(low, zero_q, q))

        def step(kbs, carry, diagonal_first=False):
            chains = [(head, m) for head in range(2) for m in range(len(kbs))]
            masked = [diagonal_first and m == 0 for _, m in chains]
            ks = [k_ref[pl.ds(pl.multiple_of(kb * t, t), t), :] for kb in kbs]
            vs = [v_ref[pl.ds(pl.multiple_of(kb * t, t), t), :] for kb in kbs]
            z = [_scores(qh[head], ks[kb]) for head, kb in chains]
            sp = [_softplus(zc) for zc in z]
            sp = [jnp.where(causal, s_, 0.0) if mk else s_ for s_, mk in zip(sp, masked)]
            inside = [_triangle_sum(s_, suffix) for s_ in sp]
            after = [carry[head][1] for head in range(2)]
            log_a = []
            for n, (head, kb) in enumerate(chains):
                log_a.append(z[n] - sp[n] - inside[n] - after[head])
                after[head] = after[head] + jnp.sum(sp[n], axis=-1, keepdims=True)
            a = [jnp.exp(l_) for l_ in log_a]
            a = [jnp.where(causal, a_, 0.0) if mk else a_ for a_, mk in zip(a, masked)]
            acc = [carry[head][0] for head in range(2)]
            for n, (head, kb) in enumerate(chains):
                acc[head] = acc[head] + _dot(a[n].astype(BF16), vs[kb])
            return tuple((acc[head], after[head]) for head in range(2))

        def live(c):
            return jnp.minimum(jnp.min(c[0][1]), jnp.min(c[1][1])) < UNDERFLOW_EXIT

        zero = (jnp.zeros((tq, LANES), F32), jnp.zeros((tq, 1), F32))
        start = lax.cond(i >= 1, lambda c: step((i, i - 1), c, True), lambda c: step((i,), c, True), (zero, zero))
        o_ref[...] = jnp.where(low, start[0][0], start[1][0]).astype(BF16)
        rest = jnp.maximum(i - 1, 0)

        @pl.when((rest > 0) & live(start))
        def _():
            carry = lax.cond(rest % 2 == 1, lambda c: step((i - 2,), c), lambda c: c, start)
            pairs = rest // 2
            _, carry = lax.while_loop(
                lambda st: (st[0] < pairs) & live(st[1]),
                lambda st: (st[0] + 1, step((2 * (pairs - st[0]) - 1, 2 * (pairs - st[0]) - 2), st[1])),
                (jnp.int32(0), carry))
            o_ref[...] = jnp.where(low, carry[0][0], carry[1][0]).astype(BF16)

        if n_gather:
            pl.when((pl.program_id(0) == n_pairs - 1) & (pl.program_id(1) == n_blocks - 1))(copies.finish)

    out = pl.pallas_call(
        body,
        name=name,
        grid=(n_pairs, n_blocks),
        in_specs=[
            pl.BlockSpec((tq, LANES), lambda p, i: (i, p)),
            pl.BlockSpec((s, LANES), lambda p, i: (0, p)),
            pl.BlockSpec((s, LANES), lambda p, i: (0, p)),
        ] + [ANY] * n_gather,
        out_specs=[pl.BlockSpec((tq, LANES), lambda p, i: (i, p))] + [ANY] * n_gather,
        out_shape=[jax.ShapeDtypeStruct((s, ATTN_DIM), BF16)] + [jax.ShapeDtypeStruct((N_CHIPS,) + w.shape, w.dtype) for w in gather],
        scratch_shapes=_gather_scratch(n_gather) if n_gather else [],
        compiler_params=_params(2),
    )(qn, kn, vb, *gather)
    return out if n_gather else out[0]


def _attn_bwd(qn, kn, vb, do, name, scatter=()):
    s = qn.shape[0]
    t = min(ATTN_TILE, s)
    nq = s // t
    n_scatter = len(scatter)
    n_pairs = ATTN_DIM // LANES

    def body(*refs):
        q_ref, k_ref, v_ref, do_ref = refs[:4]
        dq_ref, dk_ref, dv_ref = refs[4 + n_scatter : 7 + n_scatter]
        a_s, sg_s, a_f, sg_f = refs[7 + 2 * n_scatter : 11 + 2 * n_scatter]
        i = pl.program_id(1)
        if n_scatter:
            copies = _ChipScatter(refs[4 : 4 + n_scatter], refs[7 + n_scatter : 7 + 2 * n_scatter], refs[11 + 2 * n_scatter :])
            pl.when((pl.program_id(0) == 0) & (i == 0))(copies.begin)

        @pl.when(i == 0)
        def _():
            dk_ref[...] = jnp.zeros_like(dk_ref)
            dv_ref[...] = jnp.zeros_like(dv_ref)

        low = lax.broadcasted_iota(jnp.int32, (t, LANES), 1) < HEAD_DIM
        row, col = _attn_tile_consts(t)
        suffix = (row > col).astype(BF16)
        prefix = (row < col).astype(BF16)
        causal = col < row
        q = q_ref[...]
        dob = do_ref[...]
        zero_q = jnp.zeros_like(q)
        qhs = (jnp.where(low, q, zero_q), jnp.where(low, zero_q, q))
        dohs = (jnp.where(low, dob, zero_q), jnp.where(low, zero_q, dob))

        def rows_of(kb):
            return pl.ds(pl.multiple_of(kb * t, t), t)

        pair = [(head, m) for head in range(2) for m in range(2)]

        def short_pass1():
            z = [_scores(qhs[head], k_ref[rows_of(i - m), :]) for head, m in pair]
            sp = [_softplus(z_) for z_ in z]
            sp = [jnp.where(causal, s_, 0.0) if m == 0 else s_ for s_, (_, m) in zip(sp, pair)]
            inside = [_triangle_sum(s_, suffix) for s_ in sp]
            after = [jnp.zeros((t, 1), F32), jnp.zeros((t, 1), F32)]
            for n, (head, m) in enumerate(pair):
                log_sg = z[n] - sp[n]
                a = jnp.exp(log_sg - inside[n] - after[head])
                sg = jnp.exp(log_sg)
                if m == 0:
                    a = jnp.where(causal, a, 0.0)
                    sg = jnp.where(causal, sg, 0.0)
                a_f[n] = a
                sg_f[n] = sg
                after[head] = after[head] + jnp.sum(sp[n], axis=-1, keepdims=True)
            return jnp.minimum(jnp.min(after[0]), jnp.min(after[1])) >= UNDERFLOW_EXIT

        def short_pass2():
            order = [(head, m) for head in range(2) for m in (1, 0)]
            a = {c: a_f[pair.index(c)] for c in order}
            g = {c: a[c] * _dot_nt(dohs[c[0]], v_ref[rows_of(i - c[1]), :]) for c in order}
            for m in (1, 0):
                dv_ref[rows_of(i - m), :] += _dot_tn(a[(0, m)].astype(BF16), dohs[0]) + _dot_tn(a[(1, m)].astype(BF16), dohs[1])
            inside = {c: _triangle_sum(g[c], prefix) for c in order}
            before = [jnp.zeros((t, 1), F32), jnp.zeros((t, 1), F32)]
            dz = {}
            for c in order:
                sg = sg_f[pair.index(c)]
                dz[c] = (g[c] - sg * (g[c] + inside[c] + before[c[0]])).astype(BF16)
                before[c[0]] = before[c[0]] + jnp.sum(g[c], axis=-1, keepdims=True)
            for m in (1, 0):
                dk_ref[rows_of(i - m), :] += _dot_tn(dz[(0, m)], qhs[0]) + _dot_tn(dz[(1, m)], qhs[1])
            dq = [_dot(dz[(head, 1)], k_ref[rows_of(i - 1), :]) + _dot(dz[(head, 0)], k_ref[rows_of(i), :]) for head in range(2)]
            dq_ref[...] = jnp.where(low, dq[0], dq[1])

        def general_walk():
            heads = []
            for head in range(2):
                qh, doh = qhs[head], dohs[head]

                def pass1(kbs, after, diagonal_first=False):
                    z = [_scores(qh, k_ref[rows_of(kb), :]) for kb in kbs]
                    sp = [_softplus(z_) for z_ in z]
                    if diagonal_first:
                        sp[0] = jnp.where(causal, sp[0], 0.0)
                    inside = [_triangle_sum(s_, suffix) for s_ in sp]
                    for n, kb in enumerate(kbs):
                        log_sg = z[n] - sp[n]
                        a = jnp.exp(log_sg - inside[n] - after)
                        sg = jnp.exp(log_sg)
                        if diagonal_first and n == 0:
                            a = jnp.where(causal, a, 0.0)
                            sg = jnp.where(causal, sg, 0.0)
                        a_s[kb] = a
                        sg_s[kb] = sg
                        after = after + jnp.sum(sp[n], axis=-1, keepdims=True)
                    return after

                def live(after):
                    return jnp.min(after) < UNDERFLOW_EXIT

                after = jnp.zeros((t, 1), F32)
                after = lax.cond(i >= 1, lambda c: pass1((i, i - 1), c, True), lambda c: pass1((i,), c, True), after)
                rest = jnp.maximum(i - 1, 0)
                take_single = (rest % 2 == 1) & live(after)
                after = lax.cond(take_single, lambda c: pass1((i - 2,), c), lambda c: c, after)
                pairs = rest // 2
                pairs_done, _ = lax.while_loop(
                    lambda st: (st[0] < pairs) & live(st[1]),
                    lambda st: (st[0] + 1, pass1((2 * (pairs - st[0]) - 1, 2 * (pairs - st[0]) - 2), st[1])),
                    (jnp.int32(0), after))
                walked = jnp.minimum(i, 1) + 1 + take_single.astype(jnp.int32) + 2 * pairs_done
                first = i - walked + 1

                def pass2(kbs, carry):
                    dq, before = carry
                    ks = [k_ref[rows_of(kb), :] for kb in kbs]
                    a = [a_s[kb] for kb in kbs]
                    g = [a_ * _dot_nt(doh, v_ref[rows_of(kb), :]) for a_, kb in zip(a, kbs)]
                    for n, kb in enumerate(kbs):
                        dv_ref[rows_of(kb), :] += _dot_tn(a[n].astype(BF16), doh)
                    inside = [_triangle_sum(g_, prefix) for g_ in g]
                    dz = []
                    for n, kb in enumerate(kbs):
                        sg = sg_s[kb]
                        dz.append((g[n] - sg * (g[n] + inside[n] + before)).astype(BF16))
                        before = before + jnp.sum(g[n], axis=-1, keepdims=True)
                    for n, kb in enumerate(kbs):
                        dk_ref[rows_of(kb), :] += _dot_tn(dz[n], qh)
                    for n in range(len(kbs)):
                        dq = dq + _dot(dz[n], ks[n])
                    return dq, before

                carry = (jnp.zeros((t, LANES), F32), jnp.zeros((t, 1), F32))
                carry = lax.fori_loop(0, walked // 2, lambda n, c: pass2((first + 2 * n, first + 2 * n + 1), c), carry)
                carry = lax.cond(walked % 2 == 1, lambda c: pass2((i,), c), lambda c: c, carry)
                heads.append(carry[0])
            dq_ref[...] = jnp.where(low, heads[0], heads[1])

        short = lax.cond(i >= 1, short_pass1, lambda: jnp.bool_(False))
        pl.when(short)(short_pass2)
        pl.when(jnp.logical_not(short))(general_walk)
        if n_scatter:
            pl.when((pl.program_id(0) == n_pairs - 1) & (i == nq - 1))(copies.finish)

    q_spec = pl.BlockSpec((t, LANES), lambda p, i: (i, p))
    kv_spec = pl.BlockSpec((s, LANES), lambda p, i: (0, p))
    return pl.pallas_call(
        body,
        name=name,
        grid=(n_pairs, nq),
        in_specs=[q_spec, kv_spec, kv_spec, q_spec] + [ANY] * n_scatter,
        out_specs=[q_spec, kv_spec, kv_spec] + [ANY] * n_scatter,
        out_shape=[jax.ShapeDtypeStruct((s, ATTN_DIM), F32)] * 3 + _scatter_shapes(scatter),
        scratch_shapes=[pltpu.VMEM((nq, t, t), F32), pltpu.VMEM((nq, t, t), F32), pltpu.VMEM((4, t, t), F32), pltpu.VMEM((4, t, t), F32)]
        + (_scatter_scratch(n_scatter) if n_scatter else []),
        compiler_params=_params(2),
    )(qn, kn, vb, do, *scatter)


CB_BLOCK, CC_BLOCK, CU_BLOCK = 3, 4, 5


def _shift_down(h, prev_rows, n):
    row = lax.broadcasted_iota(jnp.int32, h.shape, 0)
    out = pltpu.roll(h, n, 0)
    for r in range(n):
        out = jnp.where(row == r, prev_rows[len(prev_rows) - n + r], out)
    return out


def _shift_up(h, next_rows, n):
    tm = h.shape[0]
    row = lax.broadcasted_iota(jnp.int32, h.shape, 0)
    out = pltpu.roll(h, tm - n, 0)
    for r in range(n):
        out = jnp.where(row == tm - n + r, next_rows[r], out)
    return out


def _conv_bwd(proj, conv_w, dconv, name):
    s = proj.shape[0]
    tm = TOKEN_TILE
    nb = tm // 8
    n_tiles = s // tm

    def body(cb_ref, cc_ref, cu_ref, dy_ref, pc_ref, pu_ref, nb_ref, ndy_ref, w_ref, dp_ref, dw_ref):
        i = pl.program_id(0)

        @pl.when(i == 0)
        def _():
            dw_ref[...] = jnp.zeros_like(dw_ref)

        first = i == 0
        last = i == n_tiles - 1
        cc, cu, cb, dy = cc_ref[...], cu_ref[...], cb_ref[...], dy_ref[...]
        h = cc * cu
        prev = [jnp.where(first, 0.0, pc_ref[r : r + 1, :] * pu_ref[r : r + 1, :]) for r in (6, 7)]
        h1 = _shift_down(h, prev, 1)
        h2 = _shift_down(h, prev, 2)
        y = w_ref[0:1, :] * h2 + w_ref[1:2, :] * h1 + w_ref[2:3, :] * h
        dyb = dy * cb
        nxt = [jnp.where(last, 0.0, ndy_ref[r : r + 1, :] * nb_ref[r : r + 1, :]) for r in (0, 1)]
        dh = w_ref[2:3, :] * dyb + w_ref[1:2, :] * _shift_up(dyb, nxt, 1) + w_ref[0:1, :] * _shift_up(dyb, nxt, 2)
        dp_ref[:, 0:CONV_DIM] = (dy * y).astype(BF16)
        dp_ref[:, CONV_DIM : 2 * CONV_DIM] = (dh * cu).astype(BF16)
        dp_ref[:, 2 * CONV_DIM : 3 * CONV_DIM] = (dh * cc).astype(BF16)
        dw_ref[0:1, :] += jnp.sum(dyb * h2, axis=0, keepdims=True)
        dw_ref[1:2, :] += jnp.sum(dyb * h1, axis=0, keepdims=True)
        dw_ref[2:3, :] += jnp.sum(dyb * h, axis=0, keepdims=True)

    def col(block):
        return pl.BlockSpec((tm, CONV_DIM), lambda i: (i, block))

    def halo_prev(block):
        return pl.BlockSpec((8, CONV_DIM), lambda i: (jnp.maximum(i * nb - 1, 0), block))

    def halo_next(block):
        return pl.BlockSpec((8, CONV_DIM), lambda i: (jnp.minimum((i + 1) * nb, s // 8 - 1), block))

    return pl.pallas_call(
        body,
        name=name,
        grid=(n_tiles,),
        in_specs=[
            col(CB_BLOCK), col(CC_BLOCK), col(CU_BLOCK), col(0),
            halo_prev(CC_BLOCK), halo_prev(CU_BLOCK), halo_next(CB_BLOCK), halo_next(0),
            pl.BlockSpec((8, CONV_DIM), lambda i: (0, 0)),
        ],
        out_specs=[pl.BlockSpec((tm, 3 * CONV_DIM), lambda i: (i, 1)), pl.BlockSpec((8, CONV_DIM), lambda i: (0, 0))],
        out_shape=[jax.ShapeDtypeStruct((s, 3 * ATTN_DIM + 3 * CONV_DIM), BF16), jax.ShapeDtypeStruct((8, CONV_DIM), F32)],
        compiler_params=_params(1),
    )(proj, proj, proj, dconv, proj, proj, proj, dconv, conv_w)


def _out_proj(x, attn, proj, conv_w, w_s, layer, name):
    s, d = x.shape
    tm = TOKEN_TILE
    nb = tm // 8
    rows = w_s.shape[2]

    def body(x_ref, a_ref, cb_ref, cc_ref, cu_ref, pc_ref, pu_ref, cw_ref, w_ref, o_ref, c_ref):
        first = pl.program_id(0) == 0
        h = cc_ref[...] * cu_ref[...]
        prev = [jnp.where(first, 0.0, pc_ref[r : r + 1, :] * pu_ref[r : r + 1, :]) for r in (6, 7)]
        y = cw_ref[0:1, :] * _shift_down(h, prev, 2) + cw_ref[1:2, :] * _shift_down(h, prev, 1) + cw_ref[2:3, :] * h
        c_ref[...] = (cb_ref[...] * y).astype(BF16)
        acc = x_ref[...]
        for j in range(N_CHIPS):
            src = a_ref if j < 2 else c_ref
            cols = slice((j % 2) * rows, (j % 2 + 1) * rows)
            acc = acc + _dot(src[:, cols], w_ref[j, 0])
        o_ref[...] = acc

    def col(block):
        return pl.BlockSpec((tm, CONV_DIM), lambda i: (i, block))

    def halo(block):
        return pl.BlockSpec((8, CONV_DIM), lambda i: (jnp.maximum(i * nb - 1, 0), block))

    return pl.pallas_call(
        body,
        name=name,
        grid=(s // tm,),
        in_specs=[
            pl.BlockSpec((tm, d), lambda i: (i, 0)),
            pl.BlockSpec((tm, ATTN_DIM), lambda i: (i, 0)),
            col(CB_BLOCK), col(CC_BLOCK), col(CU_BLOCK), halo(CC_BLOCK), halo(CU_BLOCK),
            pl.BlockSpec((8, CONV_DIM), lambda i: (0, 0)),
            pl.BlockSpec((N_CHIPS, 1, rows, d), lambda i: (0, layer, 0, 0)),
        ],
        out_specs=[pl.BlockSpec((tm, d), lambda i: (i, 0)), pl.BlockSpec((tm, CONV_DIM), lambda i: (i, 0))],
        out_shape=[jax.ShapeDtypeStruct((s, d), F32), jax.ShapeDtypeStruct((s, CONV_DIM), BF16)],
        compiler_params=_params(1),
    )(x, attn, proj, proj, proj, proj, proj, conv_w, w_s)


def _out_proj_bwd(dx, w_s, layer, name):
    s, d = dx.shape
    tm = TOKEN_TILE
    rows = w_s.shape[2]

    def body(dx_ref, w_ref, da_ref, dc_ref, dxb_ref):
        dxb = dx_ref[...].astype(BF16)
        dxb_ref[...] = dxb
        for j in range(N_CHIPS):
            cols = slice((j % 2) * rows, (j % 2 + 1) * rows)
            part = _dot_nt(dxb, w_ref[j, 0])
            if j < 2:
                da_ref[:, cols] = part.astype(BF16)
            else:
                dc_ref[:, cols] = part

    return pl.pallas_call(
        body,
        name=name,
        grid=(s // tm,),
        in_specs=[pl.BlockSpec((tm, d), lambda i: (i, 0)), pl.BlockSpec((N_CHIPS, 1, rows, d), lambda i: (0, layer, 0, 0))],
        out_specs=[
            pl.BlockSpec((tm, ATTN_DIM), lambda i: (i, 0)),
            pl.BlockSpec((tm, CONV_DIM), lambda i: (i, 0)),
            pl.BlockSpec((tm, d), lambda i: (i, 0)),
        ],
        out_shape=[
            jax.ShapeDtypeStruct((s, ATTN_DIM), BF16),
            jax.ShapeDtypeStruct((s, CONV_DIM), F32),
            jax.ShapeDtypeStruct((s, d), BF16),
        ],
        compiler_params=_params(1),
    )(dx, w_s)


def _ffn_fwd(x, gain, wg_s, wu_s, wd_s, layer, name, gather=(), target=None):
    s, d = x.shape
    tm = min(FFN_FWD_TILE, s)
    n_loss = 0 if target is None else 1
    f = wg_s.shape[3]
    n_gather = len(gather)
    n_tiles = s // tm

    def body(*refs):
        x_ref, g_ref, wg_ref, wu_ref, wd_ref = refs[:5]
        n_in = 5 + n_gather + n_loss
        o_ref, gate_ref, up_ref = refs[n_in : n_in + 3]
        h_s = refs[n_in + 3 + n_gather + 2 * n_loss]
        i, j = pl.program_id(0), pl.program_id(1)
        if n_gather:
            copies = _WeightGather(refs[5 : 5 + n_gather], refs[n_in + 3 : n_in + 3 + n_gather], refs[n_in + 4 + n_gather + 2 * n_loss :])
            pl.when((i == 0) & (j == 0))(copies.begin)
            pl.when((i == (3 * n_tiles) // 4) & (j == 0))(copies.relay)

        @pl.when(j == 0)
        def _():
            xv = x_ref[...]
            r = lax.rsqrt(jnp.mean(xv * xv, axis=-1, keepdims=True) + EPS)
            h_s[...] = (xv * r * g_ref[...]).astype(BF16)
            o_ref[...] = xv

        halves = [slice(r, r + FFN_CHUNK) for r in range(0, tm, FFN_CHUNK)]
        pre = [(_dot(h_s[r, :], wg_ref[0, 0]), _dot(h_s[r, :], wu_ref[0, 0])) for r in halves]
        act = [((gate / (1.0 + jnp.exp(-gate))) * up).astype(BF16) for gate, up in pre]
        for r, (gate, up) in zip(halves, pre):
            gate_ref[0, r, :] = gate.astype(BF16)
            up_ref[0, r, :] = up.astype(BF16)
        for r, a in zip(halves, act):
            o_ref[r, :] += _dot(a, wd_ref[0, 0])

        if n_loss:
            t_ref = refs[5 + n_gather]
            dy_ref, l_ref = refs[n_in + 3 + n_gather : n_in + 5 + n_gather]

            @pl.when((i == 0) & (j == 0))
            def _():
                l_ref[...] = jnp.zeros_like(l_ref)

            @pl.when(j == N_CHIPS - 1)
            def _():
                err = o_ref[...] - t_ref[...]
                dy_ref[...] = err / d
                l_ref[...] += jnp.sum(err * err, axis=0, keepdims=True) * (0.5 / d)

        if n_gather:
            pl.when((i == n_tiles - 1) & (j == N_CHIPS - 1))(copies.finish)

    tok = pl.BlockSpec((tm, d), lambda i, j: (i, 0))
    hid = pl.BlockSpec((1, tm, f), lambda i, j: (j, i, 0))
    hid_shape = jax.ShapeDtypeStruct((N_CHIPS, s, f), BF16)
    loss_specs = [tok, pl.BlockSpec((1, d), lambda i, j: (0, 0))] if n_loss else []
    loss_shapes = [jax.ShapeDtypeStruct((s, d), F32), jax.ShapeDtypeStruct((1, d), F32)] if n_loss else []
    return pl.pallas_call(
        body,
        name=name,
        grid=(n_tiles, N_CHIPS),
        in_specs=[
            pl.BlockSpec((tm, d), lambda i, j: (i, 0)),
            pl.BlockSpec((1, d), lambda i, j: (0, 0)),
            pl.BlockSpec((1, 1, d, f), lambda i, j: (j, layer, 0, 0)),
            pl.BlockSpec((1, 1, d, f), lambda i, j: (j, layer, 0, 0)),
            pl.BlockSpec((1, 1, f, d), lambda i, j: (j, layer, 0, 0)),
        ] + [ANY] * n_gather + ([tok] if n_loss else []),
        out_specs=[tok, hid, hid] + [ANY] * n_gather + loss_specs,
        out_shape=[jax.ShapeDtypeStruct((s, d), F32), hid_shape, hid_shape]
        + [jax.ShapeDtypeStruct((N_CHIPS,) + w.shape, w.dtype) for w in gather] + loss_shapes,
        scratch_shapes=[pltpu.VMEM((tm, d), BF16)] + (_gather_scratch(n_gather) if n_gather else []),
        compiler_params=_params(2),
    )(x, gain, wg_s, wu_s, wd_s, *gather, *([target] if n_loss else []))


def _resident(block, layer):
    return pl.BlockSpec(block, lambda i, j: (0, layer, 0, 0), pipeline_mode=pl.Buffered(1))


def _rms_bwd(xv, gain, dh):
    r = lax.rsqrt(jnp.mean(xv * xv, axis=-1, keepdims=True) + EPS)
    xhat = xv * r
    dxhat = dh * gain
    dx = r * (dxhat - xhat * jnp.mean(dxhat * xhat, axis=-1, keepdims=True))
    return dx, jnp.sum(dh * xhat, axis=0, keepdims=True)


def _ffn_bwd(x, dy, gain, gate_s, up_s, wg_s, wu_s, wd_s, layer, name, scatter=()):
    s, d = x.shape
    tm = TOKEN_TILE
    f = wg_s.shape[3]

    n_scatter = len(scatter)
    n_tiles = s // tm

    def body(*refs):
        x_ref, dy_ref, g_ref, gate_ref, up_ref, wg_ref, wu_ref, wd_ref = refs[:8]
        dx_ref, dgain_ref, h_ref, dyb_ref, dg_ref, du_ref, act_ref = refs[8 + n_scatter : 15 + n_scatter]
        acc_s = refs[15 + 2 * n_scatter]
        i, j = pl.program_id(0), pl.program_id(1)
        if n_scatter:
            copies = _ChipScatter(refs[8 : 8 + n_scatter], refs[15 + n_scatter : 15 + 2 * n_scatter], refs[16 + 2 * n_scatter :])
            pl.when((i == 0) & (j == 0))(copies.begin)

        @pl.when((i == 0) & (j == 0))
        def _():
            dgain_ref[...] = jnp.zeros_like(dgain_ref)

        @pl.when(j == 0)
        def _():
            xv = x_ref[...]
            r = lax.rsqrt(jnp.mean(xv * xv, axis=-1, keepdims=True) + EPS)
            h_ref[...] = (xv * r * g_ref[...]).astype(BF16)
            dyb_ref[...] = dy_ref[...].astype(BF16)
            acc_s[...] = jnp.zeros_like(acc_s)

        halves = [slice(0, tm // 2), slice(tm // 2, tm)]
        pre = [(gate_ref[0, r, :].astype(F32), up_ref[0, r, :].astype(F32), _dot_nt(dyb_ref[r, :], wd_ref[j, 0])) for r in halves]
        grads = []
        for r, (gate, up, dact) in zip(halves, pre):
            sig = 1.0 / (1.0 + jnp.exp(-gate))
            silu = gate * sig
            dgate = (dact * up * (sig * (1.0 + gate * (1.0 - sig)))).astype(BF16)
            dup = (dact * silu).astype(BF16)
            act_ref[0, r, :] = (silu * up).astype(BF16)
            dg_ref[0, r, :] = dgate
            du_ref[0, r, :] = dup
            grads.append((dgate, dup))
        for r, (dgate, dup) in zip(halves, grads):
            acc_s[r, :] += _dot_nt(dgate, wg_ref[j, 0]) + _dot_nt(dup, wu_ref[j, 0])

        @pl.when(j == N_CHIPS - 1)
        def _():
            dxn, dgain = _rms_bwd(x_ref[...], g_ref[...], acc_s[...])
            dx_ref[...] = dy_ref[...] + dxn
            dgain_ref[...] += dgain

        if n_scatter:
            pl.when((i == n_tiles - 1) & (j == N_CHIPS - 1))(copies.finish)

    tok = pl.BlockSpec((tm, d), lambda i, j: (i, 0))
    vec = pl.BlockSpec((1, d), lambda i, j: (0, 0))
    hid = pl.BlockSpec((1, tm, f), lambda i, j: (j, i, 0))
    hid_shape = jax.ShapeDtypeStruct((N_CHIPS, s, f), BF16)
    return pl.pallas_call(
        body,
        name=name,
        grid=(n_tiles, N_CHIPS),
        in_specs=[
            tok, tok, vec, hid, hid,
            _resident((N_CHIPS, 1, d, f), layer),
            _resident((N_CHIPS, 1, d, f), layer),
            _resident((N_CHIPS, 1, f, d), layer),
        ] + [ANY] * n_scatter,
        out_specs=[tok, vec, tok, tok, hid, hid, hid] + [ANY] * n_scatter,
        out_shape=[
            jax.ShapeDtypeStruct((s, d), F32),
            jax.ShapeDtypeStruct((1, d), F32),
            jax.ShapeDtypeStruct((s, d), BF16),
            jax.ShapeDtypeStruct((s, d), BF16),
            hid_shape, hid_shape, hid_shape,
        ] + _scatter_shapes(scatter),
        scratch_shapes=[pltpu.VMEM((tm, d), F32)] + (_scatter_scratch(n_scatter) if n_scatter else []),
        compiler_params=_params(2),
    )(x, dy, gain, gate_s, up_s, wg_s, wu_s, wd_s, *scatter)


def _in_proj_bwd(x, dx_res, gain, w_s, layer, proj, q_gain, k_gain, dq, dk, dv, dproj_conv, name):
    s, d = x.shape
    tm = TOKEN_TILE
    n = w_s.shape[3]
    qkv = 3 * ATTN_DIM

    def norm_bwd(xv, head_gain, dy, low):
        _, r = _head_norm(xv, head_gain, low)
        xhat = xv * r
        dxhat = dy * head_gain
        prod = dxhat * xhat
        m_low = jnp.sum(jnp.where(low, prod, 0.0), axis=-1, keepdims=True)
        m_high = jnp.sum(jnp.where(low, 0.0, prod), axis=-1, keepdims=True)
        mean = jnp.where(low, m_low, m_high) / HEAD_DIM
        return r * (dxhat - xhat * mean), jnp.sum(dy * xhat, axis=0, keepdims=True)

    def body(x_ref, r_ref, g_ref, w_ref, p_ref, qg_ref, kg_ref, dq_ref, dk_ref, dv_ref, dpc_ref, dx_ref, dgain_ref, dp_ref, dqg_ref, dkg_ref):
        @pl.when(pl.program_id(0) == 0)
        def _():
            dgain_ref[...] = jnp.zeros_like(dgain_ref)
            dqg_ref[...] = jnp.zeros_like(dqg_ref)
            dkg_ref[...] = jnp.zeros_like(dkg_ref)

        low = lax.broadcasted_iota(jnp.int32, (tm, LANES), 1) < HEAD_DIM
        for g in range(ATTN_DIM // LANES):
            cq = slice(LANES * g, LANES * (g + 1))
            ck = slice(ATTN_DIM + LANES * g, ATTN_DIM + LANES * (g + 1))
            cv = slice(2 * ATTN_DIM + LANES * g, 2 * ATTN_DIM + LANES * (g + 1))
            dxq, dgq = norm_bwd(p_ref[:, cq], qg_ref[...], dq_ref[:, cq] * Q_SCALE, low)
            dxk, dgk = norm_bwd(p_ref[:, ck], kg_ref[...], dk_ref[:, cq], low)
            dp_ref[:, cq] = dxq.astype(BF16)
            dp_ref[:, ck] = dxk.astype(BF16)
            dp_ref[:, cv] = dv_ref[:, cq].astype(BF16)
            dqg_ref[:, cq] += dgq
            dkg_ref[:, cq] += dgk
        dp_ref[:, qkv:] = dpc_ref[...]

        dh = _dot_nt(dp_ref[:, 0:n], w_ref[0, 0])
        for j in range(1, N_CHIPS):
            dh = dh + _dot_nt(dp_ref[:, j * n : (j + 1) * n], w_ref[j, 0])
        dxn, dgain = _rms_bwd(x_ref[...], g_ref[...], dh)
        dx_ref[...] = r_ref[...] + dxn
        dgain_ref[...] += dgain

    tok = pl.BlockSpec((tm, d), lambda i: (i, 0))
    vec = pl.BlockSpec((1, d), lambda i: (0, 0))
    grad_spec = pl.BlockSpec((tm, ATTN_DIM), lambda i: (i, 0))
    gain_spec = pl.BlockSpec((1, LANES), lambda i: (0, 0))
    sum_spec = pl.BlockSpec((1, ATTN_DIM), lambda i: (0, 0))
    return pl.pallas_call(
        body,
        name=name,
        grid=(s // tm,),
        in_specs=[
            tok, tok, vec, pl.BlockSpec((N_CHIPS, 1, d, n), lambda i: (0, layer, 0, 0)),
            pl.BlockSpec((tm, qkv), lambda i: (i, 0)), gain_spec, gain_spec, grad_spec, grad_spec, grad_spec,
            pl.BlockSpec((tm, N_CHIPS * n - qkv), lambda i: (i, 1)),
        ],
        out_specs=[tok, vec, pl.BlockSpec((tm, N_CHIPS * n), lambda i: (i, 0)), sum_spec, sum_spec],
        out_shape=[
            jax.ShapeDtypeStruct((s, d), F32),
            jax.ShapeDtypeStruct((1, d), F32),
            jax.ShapeDtypeStruct((s, N_CHIPS * n), BF16),
            jax.ShapeDtypeStruct((1, ATTN_DIM), F32),
            jax.ShapeDtypeStruct((1, ATTN_DIM), F32),
        ],
        compiler_params=_params(1),
    )(x, dx_res, gain, w_s, proj, q_gain, k_gain, dq, dk, dv, dproj_conv)


def _wgrad(a, b, a_spec, b_spec, n_blocks, k, n, name):
    n_tiles = a.shape[-2] // min(WGRAD_TILE, a.shape[-2])

    def body(a_ref, b_ref, o_ref):
        @pl.when(pl.program_id(1) == 0)
        def _():
            o_ref[...] = jnp.zeros_like(o_ref)

        av = a_ref[0] if len(a_ref.shape) == 3 else a_ref[...]
        bv = b_ref[0] if len(b_ref.shape) == 3 else b_ref[...]
        o_ref[0] += _dot_tn(av, bv)

    return pl.pallas_call(
        body,
        name=name,
        grid=(n_blocks, n_tiles),
        in_specs=[a_spec, b_spec],
        out_specs=pl.BlockSpec((1, k, n), lambda j, i: (j, 0, 0)),
        out_shape=jax.ShapeDtypeStruct((n_blocks, k, n), F32),
        compiler_params=_params(2),
    )(a, b)


def _wgrad_out(attn, conv, dxb, rows, name):
    s, d = dxb.shape
    tw = min(WGRAD_TILE // 2, s)

    def body(a_ref, c_ref, b_ref, o_ref):
        @pl.when(pl.program_id(0) == 0)
        def _():
            o_ref[...] = jnp.zeros_like(o_ref)

        bv = b_ref[...]
        for j in range(N_CHIPS):
            src = a_ref if j < 2 else c_ref
            o_ref[j] += _dot_tn(src[:, (j % 2) * rows : (j % 2 + 1) * rows], bv)

    return pl.pallas_call(
        body,
        name=name,
        grid=(s // tw,),
        in_specs=[
            pl.BlockSpec((tw, ATTN_DIM), lambda i: (i, 0)),
            pl.BlockSpec((tw, CONV_DIM), lambda i: (i, 0)),
            pl.BlockSpec((tw, d), lambda i: (i, 0)),
        ],
        out_specs=pl.BlockSpec((N_CHIPS, rows, d), lambda i: (0, 0, 0)),
        out_shape=jax.ShapeDtypeStruct((N_CHIPS, rows, d), F32),
        compiler_params=_params(1),
    )(attn, conv, dxb)


def _mesh_position():
    return lax.axis_index("x"), lax.axis_index("y"), lax.axis_index("c")


def _other_chips(x, y):
    return [(1 - x, y), (x, 1 - y), (1 - x, 1 - y)]


def _half_rows(ref_rows, c):
    half = ref_rows // 2
    return pl.ds(c * half, half)


class _WeightGather:
    def __init__(self, ins, outs, sems):
        self.ins, self.outs = ins, outs
        send_sems, recv_sems, pass_send_sems, pass_recv_sems, self.local_sems = sems
        self.ici, self.d2d = (send_sems, recv_sems), (pass_send_sems, pass_recv_sems)
        self.x, self.y, self.c = _mesh_position()
        self.me = 2 * self.x + self.y
        self.sibling = (self.x, self.y, 1 - self.c)
        self.chips = _other_chips(self.x, self.y)

    def _copy(self, t, k, chip_index, core, to, sems, src=None):
        dst = self.outs[t].at[chip_index, :, _half_rows(self.ins[t].shape[1], core), :]
        return pltpu.make_async_remote_copy(
            src_ref=dst if src is None else src, dst_ref=dst, send_sem=sems[0].at[t, k], recv_sem=sems[1].at[t, k],
            device_id=to, device_id_type=MESH_ID,
        )

    def _own(self, t):
        return pltpu.make_async_copy(self.ins[t], self.outs[t].at[self.me], self.local_sems.at[t])

    def _sends(self):
        for t in range(len(self.ins)):
            mine = self.ins[t].at[:, _half_rows(self.ins[t].shape[1], self.c), :]
            for k, (px, py) in enumerate(self.chips):
                yield self._copy(t, k, self.me, self.c, (px, py, self.c), self.ici, src=mine)

    def _passes(self, core, sems):
        for t in range(len(self.ins)):
            for k, (px, py) in enumerate(self.chips):
                yield self._copy(t, k, 2 * px + py, core, self.sibling, sems)

    def begin(self):
        for t in range(len(self.ins)):
            self._own(t).start()
        for cp in self._sends():
            cp.start()

    def relay(self):
        for arrived, onward in zip(self._passes(self.c, self.ici), self._passes(self.c, self.d2d)):
            arrived.wait_recv()
            onward.start()

    def finish(self):
        for cp in self._passes(1 - self.c, self.d2d):
            cp.wait_recv()
        for cp in list(self._sends()) + list(self._passes(self.c, self.d2d)):
            cp.wait_send()
        for t in range(len(self.ins)):
            self._own(t).wait()


def _gather_scratch(n):
    sems = pltpu.SemaphoreType.DMA((n, N_CHIPS - 1))
    return [sems, sems, sems, sems, pltpu.SemaphoreType.DMA((n,))]


def _gather_weights(shards):
    n = len(shards)

    def body(*refs):
        gather = _WeightGather(refs[:n], refs[n : 2 * n], refs[2 * n :])
        gather.begin()
        gather.relay()
        gather.finish()

    return pl.pallas_call(
        body,
        name="gather_weights",
        in_specs=[ANY] * n,
        out_specs=[ANY] * n,
        out_shape=[jax.ShapeDtypeStruct((N_CHIPS,) + w.shape, w.dtype) for w in shards],
        scratch_shapes=_gather_scratch(n),
    )(*shards)


def _swap_halves(grads, tag):
    n = len(grads)

    def body(*refs):
        ins, outs = refs[:n], refs[n : 2 * n]
        send_sems, recv_sems = refs[2 * n :]
        x, y, c = _mesh_position()
        copies = []
        for t in range(n):
            copies.append(pltpu.make_async_remote_copy(
                src_ref=ins[t].at[:, _half_rows(ins[t].shape[1], 1 - c), :], dst_ref=outs[t],
                send_sem=send_sems.at[t], recv_sem=recv_sems.at[t], device_id=(x, y, 1 - c), device_id_type=MESH_ID,
            ))
            copies[-1].start()
        for cp in copies:
            cp.wait()

    sems = pltpu.SemaphoreType.DMA((n,))
    return pl.pallas_call(
        body,
        name=f"swap_halves_{tag}",
        in_specs=[ANY] * n,
        out_specs=[ANY] * n,
        out_shape=[jax.ShapeDtypeStruct((g.shape[0], g.shape[1] // 2, g.shape[2]), g.dtype) for g in grads],
        scratch_shapes=[sems, sems],
    )(*grads)


class _ChipScatter:
    def __init__(self, ins, outs, sems):
        self.ins, self.outs = ins, outs
        self.send_sems, self.recv_sems = sems
        self.x, self.y, self.c = _mesh_position()

    def _copies(self):
        for t in range(len(self.ins)):
            for k, (px, py) in enumerate(_other_chips(self.x, self.y)):
                yield pltpu.make_async_remote_copy(
                    src_ref=self.ins[t].at[2 * px + py], dst_ref=self.outs[t].at[k],
                    send_sem=self.send_sems.at[t, k], recv_sem=self.recv_sems.at[t, k],
                    device_id=(px, py, self.c), device_id_type=MESH_ID,
                )

    def begin(self):
        for cp in self._copies():
            cp.start()

    def finish(self):
        for cp in self._copies():
            cp.wait()


def _scatter_scratch(n):
    sems = pltpu.SemaphoreType.DMA((n, N_CHIPS - 1))
    return [sems, sems]


def _scatter_shapes(parts):
    return [jax.ShapeDtypeStruct((N_CHIPS - 1,) + p.shape[1:], p.dtype) for p in parts]


def _scatter_to_chips(parts, tag):
    n = len(parts)

    def body(*refs):
        copies = _ChipScatter(refs[:n], refs[n : 2 * n], refs[2 * n :])
        copies.begin()
        copies.finish()

    return pl.pallas_call(
        body,
        name=f"scatter_to_chips_{tag}",
        in_specs=[ANY] * n,
        out_specs=[ANY] * n,
        out_shape=_scatter_shapes(parts),
        scratch_shapes=_scatter_scratch(n),
    )(*parts)


def _join_halves(shards):
    n = len(shards)

    def body(*refs):
        outs = refs[n : 2 * n]
        send_sems, recv_sems = refs[2 * n :]
        x, y, c = _mesh_position()
        copies = []
        for t in range(n):
            mine = outs[t].at[:, _half_rows(outs[t].shape[1], c), :]
            copies.append(pltpu.make_async_remote_copy(
                src_ref=mine, dst_ref=mine, send_sem=send_sems.at[t], recv_sem=recv_sems.at[t],
                device_id=(x, y, 1 - c), device_id_type=MESH_ID,
            ))
            copies[-1].start()
        for cp in copies:
            cp.wait()

    sems = pltpu.SemaphoreType.DMA((n,))
    return pl.pallas_call(
        body,
        name="join_halves",
        in_specs=[ANY] * n,
        out_specs=[ANY] * n,
        out_shape=[jax.ShapeDtypeStruct(g.shape, g.dtype) for g in shards],
        input_output_aliases={t: t for t in range(n)},
        scratch_shapes=[sems, sems],
    )(*shards)


def _gather_small(pack):
    def body(p_ref, o_ref, send_sems, recv_sems, local_sem):
        x, y, c = _mesh_position()
        own = pltpu.make_async_copy(p_ref, o_ref.at[4 * x + 2 * y + c], local_sem)
        own.start()
        copies = []
        for k in range(1, N_DEV):
            px, py, pc = x ^ (k >> 2), y ^ ((k >> 1) & 1), c ^ (k & 1)
            send = pltpu.make_async_remote_copy(
                src_ref=p_ref, dst_ref=o_ref.at[4 * x + 2 * y + c], send_sem=send_sems.at[k - 1], recv_sem=recv_sems.at[k - 1],
                device_id=(px, py, pc), device_id_type=MESH_ID,
            )
            send.start()
            copies.append((send, 4 * px + 2 * py + pc))
        for send, peer_slot in copies:
            send.wait_send()
        for k in range(1, N_DEV):
            px, py, pc = x ^ (k >> 2), y ^ ((k >> 1) & 1), c ^ (k & 1)
            pltpu.make_async_remote_copy(
                src_ref=p_ref, dst_ref=o_ref.at[4 * px + 2 * py + pc], send_sem=send_sems.at[k - 1], recv_sem=recv_sems.at[k - 1],
                device_id=(px, py, pc), device_id_type=MESH_ID,
            ).wait_recv()
        own.wait()

    sems = pltpu.SemaphoreType.DMA((N_DEV - 1,))
    return pl.pallas_call(
        body,
        name="gather_small",
        in_specs=[VMEM_SPEC],
        out_specs=VMEM_SPEC,
        out_shape=jax.ShapeDtypeStruct((N_DEV,) + pack.shape, pack.dtype),
        scratch_shapes=[sems, sems, pltpu.SemaphoreType.DMA],
    )(pack)


def _row_tile(rows):
    for tile in range(min(rows, 512) // 8 * 8, 0, -8):
        if rows % tile == 0:
            return tile
    return rows


def _add_half(grad, received, half_index, name):
    slots, h, cdim = received.shape
    tile = _row_tile(h)
    per_half = h // tile

    def body(c_ref, g_ref, r_ref, o_ref, ob_ref):
        total = g_ref[...] + r_ref[...]
        o_ref[...] = total
        ob_ref[...] = total.astype(BF16)

    block = pl.BlockSpec((1, tile, cdim), lambda j, i, c: (j, i, 0))
    grid_spec = pltpu.PrefetchScalarGridSpec(
        num_scalar_prefetch=1,
        grid=(slots, per_half),
        in_specs=[pl.BlockSpec((1, tile, cdim), lambda j, i, c: (j, c[0] * per_half + i, 0)), block],
        out_specs=[block, block],
    )
    return pl.pallas_call(
        body, name=name, grid_spec=grid_spec,
        out_shape=[jax.ShapeDtypeStruct(received.shape, F32), jax.ShapeDtypeStruct(received.shape, BF16)],
        compiler_params=_params(2),
    )(half_index, grad, received)


def _add_chips(part, received, chip_index, core_index, layer, n_layers, shard, name):
    _, h, cdim = part.shape
    tile = _row_tile(h)
    per_half = h // tile

    def body(chip_ref, core_ref, p_ref, r_ref, *rest):
        o_ref = rest[-1]
        o_ref[0] = ((p_ref[0] + r_ref[0].astype(F32)) + r_ref[1].astype(F32)) + r_ref[2].astype(F32)

    in_specs = [
        pl.BlockSpec((1, tile, cdim), lambda i, chip, core: (chip[0], i, 0)),
        pl.BlockSpec((N_CHIPS - 1, tile, cdim), lambda i, chip, core: (0, i, 0)),
    ]
    operands = [chip_index, core_index, part, received]
    aliases = {}
    if shard is not None:
        in_specs.append(ANY)
        operands.append(shard)
        aliases = {4: 0}
    grid_spec = pltpu.PrefetchScalarGridSpec(
        num_scalar_prefetch=2,
        grid=(per_half,),
        in_specs=in_specs,
        out_specs=pl.BlockSpec((1, tile, cdim), lambda i, chip, core: (layer, core[0] * per_half + i, 0)),
    )
    return pl.pallas_call(
        body, name=name, grid_spec=grid_spec, out_shape=jax.ShapeDtypeStruct((n_layers, 2 * h, cdim), F32),
        input_output_aliases=aliases, compiler_params=_params(1),
    )(*operands)


def _adamw(w, g, m, v, name):
    rows, cdim = w.shape
    tile = _row_tile(rows)

    def body(w_ref, g_ref, m_ref, v_ref, d_ref, nm_ref, nv_ref, go_ref):
        gv = g_ref[...]
        go_ref[...] = gv
        nm = ADAM_B1 * m_ref[...] + (1.0 - ADAM_B1) * gv
        nv = ADAM_B2 * v_ref[...] + (1.0 - ADAM_B2) * (gv * gv)
        m_hat = nm / (1.0 - ADAM_B1 ** ADAM_STEP)
        v_hat = nv / (1.0 - ADAM_B2 ** ADAM_STEP)
        d_ref[...] = -ADAM_LR * (m_hat / (jnp.sqrt(v_hat) + ADAM_EPS) + ADAM_WD * w_ref[...])
        nm_ref[...] = nm
        nv_ref[...] = nv

    spec = pl.BlockSpec((tile, cdim), lambda i: (i, 0))
    shape = jax.ShapeDtypeStruct((rows, cdim), F32)
    return pl.pallas_call(
        body, name=name, grid=(rows // tile,), in_specs=[spec] * 4, out_specs=[spec] * 4, out_shape=[shape] * 4,
        compiler_params=_params(1),
    )(w, g, m, v)


SMALL_ROWS, SMALL_COLS = 24, 1024
ROW_NORM_MIX, ROW_NORM_FFN, ROW_LOSS, ROW_Q_NORM, ROW_K_NORM, ROW_CONV = 0, 2, 4, 8, 10, 16


def _sum_small(gathered):
    def body(g_ref, o_ref, heads_ref, lanes_ref):
        total = g_ref[0]
        for dev in range(1, N_DEV):
            total = total + g_ref[dev]
        o_ref[...] = total
        heads = o_ref[8:16, 0:LANES]
        for grp in range(1, ATTN_DIM // LANES):
            heads = heads + o_ref[8:16, grp * LANES : (grp + 1) * LANES]
        heads_ref[...] = heads + pltpu.roll(heads, HEAD_DIM, 1)
        lanes_ref[...] = jnp.broadcast_to(jnp.sum(o_ref[0:8, :], axis=-1, keepdims=True), (8, LANES))

    return pl.pallas_call(
        body,
        name="sum_small",
        in_specs=[VMEM_SPEC],
        out_specs=[VMEM_SPEC] * 3,
        out_shape=[jax.ShapeDtypeStruct((SMALL_ROWS, SMALL_COLS), F32), jax.ShapeDtypeStruct((8, LANES), F32), jax.ShapeDtypeStruct((8, LANES), F32)],
    )(gathered)


def _pad_rows(a, rows):
    return jnp.pad(a, ((0, rows - a.shape[0]), (0, 0)))


def _pad_to(a, rows, cols):
    return jnp.pad(a, ((0, rows - a.shape[0]), (0, cols - a.shape[1])))


def _conv_taps(conv_s):
    return jnp.transpose(conv_s[:, 0, 0:8], (1, 0, 2)).reshape(8, -1)


class _GradExchange:
    def __init__(self, chip_index, core_index, n_layers):
        self.chip_index, self.core_index, self.n_layers = chip_index, core_index, n_layers
        self.shards = {}
        self.pending = None

    def offer(self, layer, grads):
        assert self.pending is None
        names = list(grads)
        received = _swap_halves([grads[k] for k in names], f"{'_'.join(names)}_{layer}")
        parts = [_add_half(grads[k], r, self.core_index, f"add_half_{k}_{layer}") for k, r in zip(names, received)]
        self.pending = (layer, names, [p32 for p32, _ in parts], [p16 for _, p16 in parts])

    def payload(self):
        return () if self.pending is None else tuple(self.pending[3])

    def take(self, received):
        layer, names, parts, _ = self.pending
        self.pending = None
        for k, p, r in zip(names, parts, received):
            self.shards[k] = _add_chips(
                p, r, self.chip_index, self.core_index, layer, self.n_layers, self.shards.get(k), f"add_chips_{k}_{layer}")

    def finish(self):
        if self.pending is not None:
            layer, names = self.pending[0], self.pending[1]
            self.take(_scatter_to_chips(list(self.pending[3]), f"{'_'.join(names)}_{layer}"))
        return dict(zip(BIG, _join_halves([self.shards[k] for k in BIG])))


def _local_step(x, target, norm_mix, q_norm, k_norm, norm_ffn, layer_weights, exchange=None):
    layer_weights = list(layer_weights)

    def carrying(kernel_fn, n_out, *args):
        if exchange is None or exchange.pending is None:
            return kernel_fn(*args)
        out = kernel_fn(*args, scatter=exchange.payload())
        exchange.take(out[n_out:])
        return out[:n_out]

    n_layers = norm_mix.shape[0]
    s, d = x.shape
    tw = min(WGRAD_TILE, s)
    n_in = layer_weights[0][0].shape[-1]
    f = layer_weights[0][2].shape[-1]
    saved = []
    for l in range(n_layers):
        weights = list(layer_weights[l])
        q_gain = jnp.tile(q_norm[l][None, :], (1, 2))
        k_gain = jnp.tile(k_norm[l][None, :], (1, 2))
        h1, proj, qn, kn, vb = _in_proj(x, norm_mix[l][None, :], weights[0], 0, q_gain, k_gain, f"in_proj_{l}")
        missing = [n for n, w in enumerate(weights) if w.ndim == 3]
        if missing:
            attn, *arrived = _attn_fwd(qn, kn, vb, f"attn_fwd_{l}", gather=tuple(weights[n] for n in missing))
            for n, w in zip(missing, arrived):
                weights[n] = w
            layer_weights[l] = tuple(weights)
        else:
            attn = _attn_fwd(qn, kn, vb, f"attn_fwd_{l}")
        _, wout_s, wg_s, wu_s, wd_s, conv_s = weights
        taps = _conv_taps(conv_s)
        x_mid, conv = _out_proj(x, attn, proj, taps, wout_s, 0, f"out_proj_{l}")
        pending = ()
        if l + 1 < n_layers and all(w.ndim == 3 for w in layer_weights[l + 1]):
            pending = tuple(layer_weights[l + 1])
        x_out, gate, up, *more = _ffn_fwd(
            x_mid, norm_ffn[l][None, :], wg_s, wu_s, wd_s, 0, f"ffn_fwd_{l}", gather=pending, target=target if l == n_layers - 1 else None)
        if pending:
            layer_weights[l + 1] = tuple(more[: len(pending)])
        if l == n_layers - 1:
            dy, loss_lanes = more[len(pending) :]
        saved.append(dict(x=x, h1=h1, proj=proj, qn=qn, kn=kn, vb=vb, attn=attn, conv=conv, x_mid=x_mid, q_gain=q_gain, k_gain=k_gain,
                          gate=gate, up=up, taps=taps))
        x = x_out

    grads = [None] * n_layers
    for l in reversed(range(n_layers)):
        sv = saved[l]
        win_s, wout_s, wg_s, wu_s, wd_s, _ = layer_weights[l]
        dx_mid, d_norm_ffn, h2, dyb, dgate, dup, act = carrying(
            _ffn_bwd, 7, sv["x_mid"], dy, norm_ffn[l][None, :], sv["gate"], sv["up"], wg_s, wu_s, wd_s, 0, f"ffn_bwd_{l}")
        tok2 = pl.BlockSpec((tw, d), lambda j, i: (i, 0))
        hid = pl.BlockSpec((1, tw, f), lambda j, i: (j, i, 0))
        d_wg = _wgrad(h2, dgate, tok2, hid, N_CHIPS, d, f, f"wgrad_gate_{l}")
        d_wu = _wgrad(h2, dup, tok2, hid, N_CHIPS, d, f, f"wgrad_up_{l}")
        d_wd = _wgrad(act, dyb, hid, tok2, N_CHIPS, f, d, f"wgrad_down_{l}")
        d_attn, d_conv, dxb = _out_proj_bwd(dx_mid, wout_s, 0, f"out_proj_bwd_{l}")
        d_wout = _wgrad_out(sv["attn"], sv["conv"], dxb, wout_s.shape[2], f"wgrad_out_{l}")
        if exchange is not None:
            exchange.offer(l, dict(w_gate=d_wg, w_up=d_wu, w_down=d_wd, w_out=d_wout))
        dq, dk, dv = carrying(_attn_bwd, 3, sv["qn"], sv["kn"], sv["vb"], d_attn, f"attn_bwd_{l}")
        dproj_conv, d_conv_w = _conv_bwd(sv["proj"], sv["taps"], d_conv, f"conv_bwd_{l}")
        dy, d_norm_mix, dproj, d_qg, d_kg = _in_proj_bwd(
            sv["x"], dx_mid, norm_mix[l][None, :], win_s, 0, sv["proj"], sv["q_gain"], sv["k_gain"], dq, dk, dv, dproj_conv, f"in_proj_bwd_{l}")
        d_win = _wgrad(sv["h1"], dproj, tok2, pl.BlockSpec((tw, n_in), lambda j, i: (i, j)), N_CHIPS, d, n_in, f"wgrad_in_{l}")
        if exchange is not None:
            exchange.offer(l, dict(w_in=d_win))
        grads[l] = dict(norm_mix=d_norm_mix, norm_ffn=d_norm_ffn, q_norm=d_qg, k_norm=d_kg, conv_w=d_conv_w,
                        w_in=d_win, w_out=d_wout, w_gate=d_wg, w_up=d_wu, w_down=d_wd)
    return loss_lanes, dy, grads


BIG = ("w_in", "w_out", "w_gate", "w_up", "w_down")


def kernel(x, norm_mix, w_in, q_norm, k_norm, conv_w, w_out, norm_ffn, w_gate, w_up, w_down, loss_target, m_norm_mix, m_w_in, m_q_norm, m_k_norm, m_conv_w, m_w_out, m_norm_ffn, m_w_gate, m_w_up, m_w_down, v_norm_mix, v_w_in, v_q_norm, v_k_norm, v_conv_w, v_w_out, v_norm_ffn, v_w_gate, v_w_up, v_w_down):
    n_layers = norm_mix.shape[0]
    weights = dict(w_in=w_in, w_out=w_out, w_gate=w_gate, w_up=w_up, w_down=w_down)
    moments_m = dict(w_in=m_w_in, w_out=m_w_out, w_gate=m_w_gate, w_up=m_w_up, w_down=m_w_down)
    moments_v = dict(w_in=v_w_in, w_out=v_w_out, w_gate=v_w_gate, w_up=v_w_up, w_down=v_w_down)
    cx, cy, cc = _mesh_position()
    chip_index = (2 * cx + cy).astype(jnp.int32).reshape(1)
    core_index = cc.astype(jnp.int32).reshape(1)

    conv_pad = jnp.pad(conv_w, ((0, 0), (0, 16 - conv_w.shape[1]), (0, 0)))

    def shards_of(layer):
        return [weights[k][layer : layer + 1].astype(BF16) for k in BIG] + [conv_pad[layer : layer + 1]]

    first = shards_of(0)
    layer_weights = [tuple(_gather_weights(first[:1])) + tuple(first[1:])] + [tuple(shards_of(layer)) for layer in range(1, n_layers)]

    exchange = _GradExchange(chip_index, core_index, n_layers)
    loss_lanes, grad_x, grads = _local_step(
        x[0], loss_target[0], norm_mix, q_norm, k_norm, norm_ffn, layer_weights, exchange)

    big_grads = exchange.finish()

    def lanes(a):
        return _pad_to(a, a.shape[0], SMALL_COLS)

    def tile_of(*groups):
        return _pad_rows(jnp.concatenate([lanes(jnp.concatenate(g, axis=0)) for g in groups], axis=0), 8)

    layers = range(n_layers)
    pack = jnp.concatenate([
        tile_of([grads[l]["norm_mix"] for l in layers], [grads[l]["norm_ffn"] for l in layers], [loss_lanes]),
        tile_of([grads[l]["q_norm"] for l in layers], [grads[l]["k_norm"] for l in layers]),
        tile_of([grads[l]["conv_w"][0:3] for l in layers]),
    ], axis=0)
    small, small_heads, small_lanes = _sum_small(_gather_small(pack))
    loss = small_lanes[ROW_LOSS, 0]
    d_model = norm_mix.shape[1]
    conv_cols = conv_w.shape[2]
    conv_all = small[ROW_CONV : ROW_CONV + 3 * n_layers, 0:CONV_DIM].reshape(n_layers, 3, CONV_DIM)
    small_grads = dict(
        norm_mix=small[ROW_NORM_MIX : ROW_NORM_MIX + n_layers, 0:d_model],
        norm_ffn=small[ROW_NORM_FFN : ROW_NORM_FFN + n_layers, 0:d_model],
        q_norm=small_heads[ROW_Q_NORM - 8 : ROW_Q_NORM - 8 + n_layers, 0:HEAD_DIM],
        k_norm=small_heads[ROW_K_NORM - 8 : ROW_K_NORM - 8 + n_layers, 0:HEAD_DIM],
        conv_w=lax.dynamic_slice_in_dim(conv_all, (2 * cx + cy) * conv_cols, conv_cols, axis=2),
    )

    out_grad, out_delta, out_m, out_v = {}, {}, {}, {}
    for k in BIG:
        shape = weights[k].shape
        view = (shape[0] * shape[1], shape[2])
        g = big_grads[k]
        delta, new_m, new_v, g = _adamw(weights[k].reshape(view), g.reshape(view), moments_m[k].reshape(view), moments_v[k].reshape(view), f"adamw_{k}")
        out_grad[k], out_delta[k], out_m[k], out_v[k] = g.reshape(shape), delta.reshape(shape), new_m.reshape(shape), new_v.reshape(shape)

    small_w = dict(norm_mix=norm_mix, norm_ffn=norm_ffn, q_norm=q_norm, k_norm=k_norm, conv_w=conv_w)
    small_m = dict(norm_mix=m_norm_mix, norm_ffn=m_norm_ffn, q_norm=m_q_norm, k_norm=m_k_norm, conv_w=m_conv_w)
    small_v = dict(norm_mix=v_norm_mix, norm_ffn=v_norm_ffn, q_norm=v_q_norm, k_norm=v_k_norm, conv_w=v_conv_w)
    order = ("norm_mix", "norm_ffn", "q_norm", "k_norm", "conv_w")

    def packed(tree):
        parts2 = [_pad_to(tree[k].reshape(-1, tree[k].shape[-1]), tree[k].reshape(-1, tree[k].shape[-1]).shape[0], SMALL_COLS) for k in order]
        return _pad_rows(jnp.concatenate(parts2, axis=0), SMALL_ROWS)

    delta_p, m_p, v_p, _ = _adamw(packed(small_w), packed(small_grads), packed(small_m), packed(small_v), "adamw_small")
    row = 0
    for k in order:
        shape = small_w[k].shape
        n_rows = 1
        for dim in shape[:-1]:
            n_rows *= dim
        cut = (slice(row, row + n_rows), slice(0, shape[-1]))
        out_grad[k] = small_grads[k]
        out_delta[k], out_m[k], out_v[k] = delta_p[cut].reshape(shape), m_p[cut].reshape(shape), v_p[cut].reshape(shape)
        row += n_rows

    names_out = ("norm_mix", "w_in", "q_norm", "k_norm", "conv_w", "w_out", "norm_ffn", "w_gate", "w_up", "w_down")
    return (loss, grad_x[None], *[out_grad[k] for k in names_out], *[out_delta[k] for k in names_out],
            *[out_m[k] for k in names_out], *[out_v[k] for k in names_out])
```

```python
import jax
import jax.numpy as jnp
from jax import lax
from jax.experimental import pallas as pl
from jax.experimental.pallas import tpu as pltpu

F32 = jnp.float32
BF16 = jnp.bfloat16

EPS = 1e-6
HEAD_DIM = 64
LANES = 128
ATTN_DIM = 512
CONV_DIM = 512
N_CHIPS = 4
N_DEV = 8
Q_SCALE = HEAD_DIM ** -0.5
ATTN_Q_TILE = 256
ATTN_TILE = 256
TOKEN_TILE = 512
WGRAD_TILE = 4096
FFN_FWD_TILE = 1024
FFN_CHUNK = 256
VMEM_LIMIT = 56 * 1024 * 1024

ADAM_LR = 0.001
ADAM_B1 = 0.9
ADAM_B2 = 0.999
ADAM_EPS = 1e-08
ADAM_WD = 0.01
ADAM_STEP = 10

MESH_ID = pl.DeviceIdType.MESH
ANY = pl.BlockSpec(memory_space=pl.ANY)
VMEM_SPEC = pl.BlockSpec(memory_space=pltpu.VMEM)


def _params(n_axes):
    return pltpu.CompilerParams(dimension_semantics=("arbitrary",) * n_axes, vmem_limit_bytes=VMEM_LIMIT)


def _dot(a, b):
    return jnp.dot(a, b, preferred_element_type=F32)


def _dot_nt(a, b):
    return lax.dot_general(a, b, (((1,), (1,)), ((), ())), preferred_element_type=F32)


def _dot_tn(a, b):
    return lax.dot_general(a, b, (((0,), (0,)), ((), ())), preferred_element_type=F32)


SCORE_MAX = 80.0
UNDERFLOW_EXIT = 90.0


def _scores(q, k):
    return jnp.minimum(_dot_nt(q, k), SCORE_MAX)


def _softplus(z):
    return jnp.log(1.0 + jnp.exp(z))


def _head_norm(xv, gain, low):
    sq = xv * xv
    s_low = jnp.sum(jnp.where(low, sq, 0.0), axis=-1, keepdims=True)
    s_high = jnp.sum(jnp.where(low, 0.0, sq), axis=-1, keepdims=True)
    r = jnp.where(low, lax.rsqrt(s_low / HEAD_DIM + EPS), lax.rsqrt(s_high / HEAD_DIM + EPS))
    return xv * r * gain, r


def _in_proj(x, gain, w_s, layer, q_gain, k_gain, name):
    s, d = x.shape
    n_blocks, _, _, n = w_s.shape
    tm = TOKEN_TILE

    def body(x_ref, g_ref, w_ref, qg_ref, kg_ref, h_ref, o_ref, q_ref, k_ref, v_ref):
        xv = x_ref[...]
        r = lax.rsqrt(jnp.mean(xv * xv, axis=-1, keepdims=True) + EPS)
        h = (xv * r * g_ref[...]).astype(BF16)
        h_ref[...] = h
        for j in range(n_blocks):
            o_ref[:, j * n : (j + 1) * n] = _dot(h, w_ref[j, 0])
        low = lax.broadcasted_iota(jnp.int32, (tm, LANES), 1) < HEAD_DIM
        for g in range(ATTN_DIM // LANES):
            cq = slice(LANES * g, LANES * (g + 1))
            ck = slice(ATTN_DIM + LANES * g, ATTN_DIM + LANES * (g + 1))
            cv = slice(2 * ATTN_DIM + LANES * g, 2 * ATTN_DIM + LANES * (g + 1))
            qn, _ = _head_norm(o_ref[:, cq], qg_ref[...], low)
            kn, _ = _head_norm(o_ref[:, ck], kg_ref[...], low)
            q_ref[:, cq] = (qn * Q_SCALE).astype(BF16)
            k_ref[:, cq] = kn.astype(BF16)
            v_ref[:, cq] = o_ref[:, cv].astype(BF16)

    head_spec = pl.BlockSpec((tm, ATTN_DIM), lambda i: (i, 0))
    head_shape = jax.ShapeDtypeStruct((s, ATTN_DIM), BF16)
    gain_spec = pl.BlockSpec((1, LANES), lambda i: (0, 0))
    return pl.pallas_call(
        body,
        name=name,
        grid=(s // tm,),
        in_specs=[
            pl.BlockSpec((tm, d), lambda i: (i, 0)),
            pl.BlockSpec((1, d), lambda i: (0, 0)),
            pl.BlockSpec((n_blocks, 1, d, n), lambda i: (0, layer, 0, 0)),
            gain_spec, gain_spec,
        ],
        out_specs=[pl.BlockSpec((tm, d), lambda i: (i, 0)), pl.BlockSpec((tm, n_blocks * n), lambda i: (i, 0)), head_spec, head_spec, head_spec],
        out_shape=[jax.ShapeDtypeStruct((s, d), BF16), jax.ShapeDtypeStruct((s, n_blocks * n), F32), head_shape, head_shape, head_shape],
        compiler_params=_params(1),
    )(x, gain, w_s, q_gain, k_gain)


def _attn_tile_consts(t):
    row = lax.broadcasted_iota(jnp.int32, (t, t), 0)
    col = lax.broadcasted_iota(jnp.int32, (t, t), 1)
    return row, col


def _triangle_sum(v, triangle):
    return _dot(v.astype(BF16), triangle)


def _attn_fwd(qn, kn, vb, name, gather=()):
    s = qn.shape[0]
    t = min(ATTN_TILE, s)
    tq = min(ATTN_Q_TILE, t)
    per_key_tile = t // tq
    n_gather = len(gather)
    n_pairs, n_blocks = ATTN_DIM // LANES, s // tq

    def body(*refs):
        q_ref, k_ref, v_ref = refs[:3]
        o_ref = refs[3 + n_gather]
        if n_gather:
            copies = _WeightGather(refs[3 : 3 + n_gather], refs[4 + n_gather : 4 + 2 * n_gather], refs[4 + 2 * n_gather :])
            first = (pl.program_id(0) == 0) & (pl.program_id(1) == 0)
            pl.when(first)(copies.begin)
            pl.when((pl.program_id(0) == n_pairs - 1) & (pl.program_id(1) == 0))(copies.relay)
        i = pl.program_id(1) // per_key_tile
        low = lax.broadcasted_iota(jnp.int32, (tq, LANES), 1) < HEAD_DIM
        row, col = _attn_tile_consts(t)
        suffix = (row > col).astype(BF16)
        first_row = (pl.program_id(1) % per_key_tile) * tq
        causal = lax.broadcasted_iota(jnp.int32, (tq, t), 1) < lax.broadcasted_iota(jnp.int32, (tq, t), 0) + first_row
        q = q_ref[...]
        zero_q = jnp.zeros_like(q)
        qh = (jnp.where(low, q, zero_q), jnp.where(low, zero_q, q))

        def step(kbs, carry, diagonal_first=False):
            chains = [(head, m) for head in range(2) for m in range(len(kbs))]
            masked = [diagonal_first and m == 0 for _, m in chains]
            ks = [k_ref[pl.ds(pl.multiple_of(kb * t, t), t), :] for kb in kbs]
            vs = [v_ref[pl.ds(pl.multiple_of(kb * t, t), t), :] for kb in kbs]
            z = [_scores(qh[head], ks[kb]) for head, kb in chains]
            sp = [_softplus(zc) for zc in z]
            sp = [jnp.where(causal, s_, 0.0) if mk else s_ for s_, mk in zip(sp, masked)]
            inside = [_triangle_sum(s_, suffix) for s_ in sp]
            after = [carry[head][1] for head in range(2)]
            log_a = []
            for n, (head, kb) in enumerate(chains):
                log_a.append(z[n] - sp[n] - inside[n] - after[head])
                after[head] = after[head] + jnp.sum(sp[n], axis=-1, keepdims=True)
            a = [jnp.exp(l_) for l_ in log_a]
            a = [jnp.where(causal, a_, 0.0) if mk else a_ for a_, mk in zip(a, masked)]
            acc = [carry[head][0] for head in range(2)]
            for n, (head, kb) in enumerate(chains):
                acc[head] = acc[head] + _dot(a[n].astype(BF16), vs[kb])
            return tuple((acc[head], after[head]) for head in range(2))

        def live(c):
            return jnp.minimum(jnp.min(c[0][1]), jnp.min(c[1][1])) < UNDERFLOW_EXIT

        zero = (jnp.zeros((tq, LANES), F32), jnp.zeros((tq, 1), F32))
        start = lax.cond(i >= 1, lambda c: step((i, i - 1), c, True), lambda c: step((i,), c, True), (zero, zero))
        o_ref[...] = jnp.where(low, start[0][0], start[1][0]).astype(BF16)
        rest = jnp.maximum(i - 1, 0)

        @pl.when((rest > 0) & live(start))
        def _():
            carry = lax.cond(rest % 2 == 1, lambda c: step((i - 2,), c), lambda c: c, start)
            pairs = rest // 2
            _, carry = lax.while_loop(
                lambda st: (st[0] < pairs) & live(st[1]),
                lambda st: (st[0] + 1, step((2 * (pairs - st[0]) - 1, 2 * (pairs - st[0]) - 2), st[1])),
                (jnp.int32(0), carry))
            o_ref[...] = jnp.where(low, carry[0][0], carry[1][0]).astype(BF16)

        if n_gather:
            pl.when((pl.program_id(0) == n_pairs - 1) & (pl.program_id(1) == n_blocks - 1))(copies.finish)

    out = pl.pallas_call(
        body,
        name=name,
        grid=(n_pairs, n_blocks),
        in_specs=[
            pl.BlockSpec((tq, LANES), lambda p, i: (i, p)),
            pl.BlockSpec((s, LANES), lambda p, i: (0, p)),
            pl.BlockSpec((s, LANES), lambda p, i: (0, p)),
        ] + [ANY] * n_gather,
        out_specs=[pl.BlockSpec((tq, LANES), lambda p, i: (i, p))] + [ANY] * n_gather,
        out_shape=[jax.ShapeDtypeStruct((s, ATTN_DIM), BF16)] + [jax.ShapeDtypeStruct((N_CHIPS,) + w.shape, w.dtype) for w in gather],
        scratch_shapes=_gather_scratch(n_gather) if n_gather else [],
        compiler_params=_params(2),
    )(qn, kn, vb, *gather)
    return out if n_gather else out[0]


def _attn_bwd(qn, kn, vb, do, name, scatter=()):
    s = qn.shape[0]
    t = min(ATTN_TILE, s)
    nq = s // t
    n_scatter = len(scatter)
    n_pairs = ATTN_DIM // LANES

    def body(*refs):
        q_ref, k_ref, v_ref, do_ref = refs[:4]
        dq_ref, dk_ref, dv_ref = refs[4 + n_scatter : 7 + n_scatter]
        a_s, sg_s, a_f, sg_f = refs[7 + 2 * n_scatter : 11 + 2 * n_scatter]
        i = pl.program_id(1)
        if n_scatter:
            copies = _ChipScatter(refs[4 : 4 + n_scatter], refs[7 + n_scatter : 7 + 2 * n_scatter], refs[11 + 2 * n_scatter :])
            pl.when((pl.program_id(0) == 0) & (i == 0))(copies.begin)

        @pl.when(i == 0)
        def _():
            dk_ref[...] = jnp.zeros_like(dk_ref)
            dv_ref[...] = jnp.zeros_like(dv_ref)

        low = lax.broadcasted_iota(jnp.int32, (t, LANES), 1) < HEAD_DIM
        row, col = _attn_tile_consts(t)
        suffix = (row > col).astype(BF16)
        prefix = (row < col).astype(BF16)
        causal = col < row
        q = q_ref[...]
        dob = do_ref[...]
        zero_q = jnp.zeros_like(q)
        qhs = (jnp.where(low, q, zero_q), jnp.where(low, zero_q, q))
        dohs = (jnp.where(low, dob, zero_q), jnp.where(low, zero_q, dob))

        def rows_of(kb):
            return pl.ds(pl.multiple_of(kb * t, t), t)

        pair = [(head, m) for head in range(2) for m in range(2)]

        def short_pass1():
            z = [_scores(qhs[head], k_ref[rows_of(i - m), :]) for head, m in pair]
            sp = [_softplus(z_) for z_ in z]
            sp = [jnp.where(causal, s_, 0.0) if m == 0 else s_ for s_, (_, m) in zip(sp, pair)]
            inside = [_triangle_sum(s_, suffix) for s_ in sp]
            after = [jnp.zeros((t, 1), F32), jnp.zeros((t, 1), F32)]
            for n, (head, m) in enumerate(pair):
                log_sg = z[n] - sp[n]
                a = jnp.exp(log_sg - inside[n] - after[head])
                sg = jnp.exp(log_sg)
                if m == 0:
                    a = jnp.where(causal, a, 0.0)
                    sg = jnp.where(causal, sg, 0.0)
                a_f[n] = a
                sg_f[n] = sg
                after[head] = after[head] + jnp.sum(sp[n], axis=-1, keepdims=True)
            return jnp.minimum(jnp.min(after[0]), jnp.min(after[1])) >= UNDERFLOW_EXIT

        def short_pass2():
            order = [(head, m) for head in range(2) for m in (1, 0)]
            a = {c: a_f[pair.index(c)] for c in order}
            g = {c: a[c] * _dot_nt(dohs[c[0]], v_ref[rows_of(i - c[1]), :]) for c in order}
            for m in (1, 0):
                dv_ref[rows_of(i - m), :] += _dot_tn(a[(0, m)].astype(BF16), dohs[0]) + _dot_tn(a[(1, m)].astype(BF16), dohs[1])
            inside = {c: _triangle_sum(g[c], prefix) for c in order}
            before = [jnp.zeros((t, 1), F32), jnp.zeros((t, 1), F32)]
            dz = {}
            for c in order:
                sg = sg_f[pair.index(c)]
                dz[c] = (g[c] - sg * (g[c] + inside[c] + before[c[0]])).astype(BF16)
                before[c[0]] = before[c[0]] + jnp.sum(g[c], axis=-1, keepdims=True)
            for m in (1, 0):
                dk_ref[rows_of(i - m), :] += _dot_tn(dz[(0, m)], qhs[0]) + _dot_tn(dz[(1, m)], qhs[1])
            dq = [_dot(dz[(head, 1)], k_ref[rows_of(i - 1), :]) + _dot(dz[(head, 0)], k_ref[rows_of(i), :]) for head in range(2)]
            dq_ref[...] = jnp.where(low, dq[0], dq[1])

        def general_walk():
            heads = []
            for head in range(2):
                qh, doh = qhs[head], dohs[head]

                def pass1(kbs, after, diagonal_first=False):
                    z = [_scores(qh, k_ref[rows_of(kb), :]) for kb in kbs]
                    sp = [_softplus(z_) for z_ in z]
                    if diagonal_first:
                        sp[0] = jnp.where(causal, sp[0], 0.0)
                    inside = [_triangle_sum(s_, suffix) for s_ in sp]
                    for n, kb in enumerate(kbs):
                        log_sg = z[n] - sp[n]
                        a = jnp.exp(log_sg - inside[n] - after)
                        sg = jnp.exp(log_sg)
                        if diagonal_first and n == 0:
                            a = jnp.where(causal, a, 0.0)
                            sg = jnp.where(causal, sg, 0.0)
                        a_s[kb] = a
                        sg_s[kb] = sg
                        after = after + jnp.sum(sp[n], axis=-1, keepdims=True)
                    return after

                def live(after):
                    return jnp.min(after) < UNDERFLOW_EXIT

                after = jnp.zeros((t, 1), F32)
                after = lax.cond(i >= 1, lambda c: pass1((i, i - 1), c, True), lambda c: pass1((i,), c, True), after)
                rest = jnp.maximum(i - 1, 0)
                take_single = (rest % 2 == 1) & live(after)
                after = lax.cond(take_single, lambda c: pass1((i - 2,), c), lambda c: c, after)
                pairs = rest // 2
                pairs_done, _ = lax.while_loop(
                    lambda st: (st[0] < pairs) & live(st[1]),
                    lambda st: (st[0] + 1, pass1((2 * (pairs - st[0]) - 1, 2 * (pairs - st[0]) - 2), st[1])),
                    (jnp.int32(0), after))
                walked = jnp.minimum(i, 1) + 1 + take_single.astype(jnp.int32) + 2 * pairs_done
                first = i - walked + 1

                def pass2(kbs, carry):
                    dq, before = carry
                    ks = [k_ref[rows_of(kb), :] for kb in kbs]
                    a = [a_s[kb] for kb in kbs]
                    g = [a_ * _dot_nt(doh, v_ref[rows_of(kb), :]) for a_, kb in zip(a, kbs)]
                    for n, kb in enumerate(kbs):
                        dv_ref[rows_of(kb), :] += _dot_tn(a[n].astype(BF16), doh)
                    inside = [_triangle_sum(g_, prefix) for g_ in g]
                    dz = []
                    for n, kb in enumerate(kbs):
                        sg = sg_s[kb]
                        dz.append((g[n] - sg * (g[n] + inside[n] + before)).astype(BF16))
                        before = before + jnp.sum(g[n], axis=-1, keepdims=True)
                    for n, kb in enumerate(kbs):
                        dk_ref[rows_of(kb), :] += _dot_tn(dz[n], qh)
                    for n in range(len(kbs)):
                        dq = dq + _dot(dz[n], ks[n])
                    return dq, before

                carry = (jnp.zeros((t, LANES), F32), jnp.zeros((t, 1), F32))
                carry = lax.fori_loop(0, walked // 2, lambda n, c: pass2((first + 2 * n, first + 2 * n + 1), c), carry)
                carry = lax.cond(walked % 2 == 1, lambda c: pass2((i,), c), lambda c: c, carry)
                heads.append(carry[0])
            dq_ref[...] = jnp.where(low, heads[0], heads[1])

        short = lax.cond(i >= 1, short_pass1, lambda: jnp.bool_(False))
        pl.when(short)(short_pass2)
        pl.when(jnp.logical_not(short))(general_walk)
        if n_scatter:
            pl.when((pl.program_id(0) == n_pairs - 1) & (i == nq - 1))(copies.finish)

    q_spec = pl.BlockSpec((t, LANES), lambda p, i: (i, p))
    kv_spec = pl.BlockSpec((s, LANES), lambda p, i: (0, p))
    return pl.pallas_call(
        body,
        name=name,
        grid=(n_pairs, nq),
        in_specs=[q_spec, kv_spec, kv_spec, q_spec] + [ANY] * n_scatter,
        out_specs=[q_spec, kv_spec, kv_spec] + [ANY] * n_scatter,
        out_shape=[jax.ShapeDtypeStruct((s, ATTN_DIM), F32)] * 3 + _scatter_shapes(scatter),
        scratch_shapes=[pltpu.VMEM((nq, t, t), F32), pltpu.VMEM((nq, t, t), F32), pltpu.VMEM((4, t, t), F32), pltpu.VMEM((4, t, t), F32)]
        + (_scatter_scratch(n_scatter) if n_scatter else []),
        compiler_params=_params(2),
    )(qn, kn, vb, do, *scatter)


CB_BLOCK, CC_BLOCK, CU_BLOCK = 3, 4, 5


def _shift_down(h, prev_rows, n):
    row = lax.broadcasted_iota(jnp.int32, h.shape, 0)
    out = pltpu.roll(h, n, 0)
    for r in range(n):
        out = jnp.where(row == r, prev_rows[len(prev_rows) - n + r], out)
    return out


def _shift_up(h, next_rows, n):
    tm = h.shape[0]
    row = lax.broadcasted_iota(jnp.int32, h.shape, 0)
    out = pltpu.roll(h, tm - n, 0)
    for r in range(n):
        out = jnp.where(row == tm - n + r, next_rows[r], out)
    return out


def _conv_bwd(proj, conv_w, dconv, name):
    s = proj.shape[0]
    tm = TOKEN_TILE
    nb = tm // 8
    n_tiles = s // tm

    def body(cb_ref, cc_ref, cu_ref, dy_ref, pc_ref, pu_ref, nb_ref, ndy_ref, w_ref, dp_ref, dw_ref):
        i = pl.program_id(0)

        @pl.when(i == 0)
        def _():
            dw_ref[...] = jnp.zeros_like(dw_ref)

        first = i == 0
        last = i == n_tiles - 1
        cc, cu, cb, dy = cc_ref[...], cu_ref[...], cb_ref[...], dy_ref[...]
        h = cc * cu
        prev = [jnp.where(first, 0.0, pc_ref[r : r + 1, :] * pu_ref[r : r + 1, :]) for r in (6, 7)]
        h1 = _shift_down(h, prev, 1)
        h2 = _shift_down(h, prev, 2)
        y = w_ref[0:1, :] * h2 + w_ref[1:2, :] * h1 + w_ref[2:3, :] * h
        dyb = dy * cb
        nxt = [jnp.where(last, 0.0, ndy_ref[r : r + 1, :] * nb_ref[r : r + 1, :]) for r in (0, 1)]
        dh = w_ref[2:3, :] * dyb + w_ref[1:2, :] * _shift_up(dyb, nxt, 1) + w_ref[0:1, :] * _shift_up(dyb, nxt, 2)
        dp_ref[:, 0:CONV_DIM] = (dy * y).astype(BF16)
        dp_ref[:, CONV_DIM : 2 * CONV_DIM] = (dh * cu).astype(BF16)
        dp_ref[:, 2 * CONV_DIM : 3 * CONV_DIM] = (dh * cc).astype(BF16)
        dw_ref[0:1, :] += jnp.sum(dyb * h2, axis=0, keepdims=True)
        dw_ref[1:2, :] += jnp.sum(dyb * h1, axis=0, keepdims=True)
        dw_ref[2:3, :] += jnp.sum(dyb * h, axis=0, keepdims=True)

    def col(block):
        return pl.BlockSpec((tm, CONV_DIM), lambda i: (i, block))

    def halo_prev(block):
        return pl.BlockSpec((8, CONV_DIM), lambda i: (jnp.maximum(i * nb - 1, 0), block))

    def halo_next(block):
        return pl.BlockSpec((8, CONV_DIM), lambda i: (jnp.minimum((i + 1) * nb, s // 8 - 1), block))

    return pl.pallas_call(
        body,
        name=name,
        grid=(n_tiles,),
        in_specs=[
            col(CB_BLOCK), col(CC_BLOCK), col(CU_BLOCK), col(0),
            halo_prev(CC_BLOCK), halo_prev(CU_BLOCK), halo_next(CB_BLOCK), halo_next(0),
            pl.BlockSpec((8, CONV_DIM), lambda i: (0, 0)),
        ],
        out_specs=[pl.BlockSpec((tm, 3 * CONV_DIM), lambda i: (i, 1)), pl.BlockSpec((8, CONV_DIM), lambda i: (0, 0))],
        out_shape=[jax.ShapeDtypeStruct((s, 3 * ATTN_DIM + 3 * CONV_DIM), BF16), jax.ShapeDtypeStruct((8, CONV_DIM), F32)],
        compiler_params=_params(1),
    )(proj, proj, proj, dconv, proj, proj, proj, dconv, conv_w)


def _out_proj(x, attn, proj, conv_w, w_s, layer, name):
    s, d = x.shape
    tm = TOKEN_TILE
    nb = tm // 8
    rows = w_s.shape[2]

    def body(x_ref, a_ref, cb_ref, cc_ref, cu_ref, pc_ref, pu_ref, cw_ref, w_ref, o_ref, c_ref):
        first = pl.program_id(0) == 0
        h = cc_ref[...] * cu_ref[...]
        prev = [jnp.where(first, 0.0, pc_ref[r : r + 1, :] * pu_ref[r : r + 1, :]) for r in (6, 7)]
        y = cw_ref[0:1, :] * _shift_down(h, prev, 2) + cw_ref[1:2, :] * _shift_down(h, prev, 1) + cw_ref[2:3, :] * h
        c_ref[...] = (cb_ref[...] * y).astype(BF16)
        acc = x_ref[...]
        for j in range(N_CHIPS):
            src = a_ref if j < 2 else c_ref
            cols = slice((j % 2) * rows, (j % 2 + 1) * rows)
            acc = acc + _dot(src[:, cols], w_ref[j, 0])
        o_ref[...] = acc

    def col(block):
        return pl.BlockSpec((tm, CONV_DIM), lambda i: (i, block))

    def halo(block):
        return pl.BlockSpec((8, CONV_DIM), lambda i: (jnp.maximum(i * nb - 1, 0), block))

    return pl.pallas_call(
        body,
        name=name,
        grid=(s // tm,),
        in_specs=[
            pl.BlockSpec((tm, d), lambda i: (i, 0)),
            pl.BlockSpec((tm, ATTN_DIM), lambda i: (i, 0)),
            col(CB_BLOCK), col(CC_BLOCK), col(CU_BLOCK), halo(CC_BLOCK), halo(CU_BLOCK),
            pl.BlockSpec((8, CONV_DIM), lambda i: (0, 0)),
            pl.BlockSpec((N_CHIPS, 1, rows, d), lambda i: (0, layer, 0, 0)),
        ],
        out_specs=[pl.BlockSpec((tm, d), lambda i: (i, 0)), pl.BlockSpec((tm, CONV_DIM), lambda i: (i, 0))],
        out_shape=[jax.ShapeDtypeStruct((s, d), F32), jax.ShapeDtypeStruct((s, CONV_DIM), BF16)],
        compiler_params=_params(1),
    )(x, attn, proj, proj, proj, proj, proj, conv_w, w_s)


def _out_proj_bwd(dx, w_s, layer, name):
    s, d = dx.shape
    tm = TOKEN_TILE
    rows = w_s.shape[2]

    def body(dx_ref, w_ref, da_ref, dc_ref, dxb_ref):
        dxb = dx_ref[...].astype(BF16)
        dxb_ref[...] = dxb
        for j in range(N_CHIPS):
            cols = slice((j % 2) * rows, (j % 2 + 1) * rows)
            part = _dot_nt(dxb, w_ref[j, 0])
            if j < 2:
                da_ref[:, cols] = part.astype(BF16)
            else:
                dc_ref[:, cols] = part

    return pl.pallas_call(
        body,
        name=name,
        grid=(s // tm,),
        in_specs=[pl.BlockSpec((tm, d), lambda i: (i, 0)), pl.BlockSpec((N_CHIPS, 1, rows, d), lambda i: (0, layer, 0, 0))],
        out_specs=[
            pl.BlockSpec((tm, ATTN_DIM), lambda i: (i, 0)),
            pl.BlockSpec((tm, CONV_DIM), lambda i: (i, 0)),
            pl.BlockSpec((tm, d), lambda i: (i, 0)),
        ],
        out_shape=[
            jax.ShapeDtypeStruct((s, ATTN_DIM), BF16),
            jax.ShapeDtypeStruct((s, CONV_DIM), F32),
            jax.ShapeDtypeStruct((s, d), BF16),
        ],
        compiler_params=_params(1),
    )(dx, w_s)


def _ffn_fwd(x, gain, wg_s, wu_s, wd_s, layer, name, gather=(), target=None):
    s, d = x.shape
    tm = min(FFN_FWD_TILE, s)
    n_loss = 0 if target is None else 1
    f = wg_s.shape[3]
    n_gather = len(gather)
    n_tiles = s // tm

    def body(*refs):
        x_ref, g_ref, wg_ref, wu_ref, wd_ref = refs[:5]
        n_in = 5 + n_gather + n_loss
        o_ref, gate_ref, up_ref = refs[n_in : n_in + 3]
        h_s = refs[n_in + 3 + n_gather + 2 * n_loss]
        i, j = pl.program_id(0), pl.program_id(1)
        if n_gather:
            copies = _WeightGather(refs[5 : 5 + n_gather], refs[n_in + 3 : n_in + 3 + n_gather], refs[n_in + 4 + n_gather + 2 * n_loss :])
            pl.when((i == 0) & (j == 0))(copies.begin)
            pl.when((i == (3 * n_tiles) // 4) & (j == 0))(copies.relay)

        @pl.when(j == 0)
        def _():
            xv = x_ref[...]
            r = lax.rsqrt(jnp.mean(xv * xv, axis=-1, keepdims=True) + EPS)
            h_s[...] = (xv * r * g_ref[...]).astype(BF16)
            o_ref[...] = xv

        halves = [slice(r, r + FFN_CHUNK) for r in range(0, tm, FFN_CHUNK)]
        pre = [(_dot(h_s[r, :], wg_ref[0, 0]), _dot(h_s[r, :], wu_ref[0, 0])) for r in halves]
        act = [((gate / (1.0 + jnp.exp(-gate))) * up).astype(BF16) for gate, up in pre]
        for r, (gate, up) in zip(halves, pre):
            gate_ref[0, r, :] = gate.astype(BF16)
            up_ref[0, r, :] = up.astype(BF16)
        for r, a in zip(halves, act):
            o_ref[r, :] += _dot(a, wd_ref[0, 0])

        if n_loss:
            t_ref = refs[5 + n_gather]
            dy_ref, l_ref = refs[n_in + 3 + n_gather : n_in + 5 + n_gather]

            @pl.when((i == 0) & (j == 0))
            def _():
                l_ref[...] = jnp.zeros_like(l_ref)

            @pl.when(j == N_CHIPS - 1)
            def _():
                err = o_ref[...] - t_ref[...]
                dy_ref[...] = err / d
                l_ref[...] += jnp.sum(err * err, axis=0, keepdims=True) * (0.5 / d)

        if n_gather:
            pl.when((i == n_tiles - 1) & (j == N_CHIPS - 1))(copies.finish)

    tok = pl.BlockSpec((tm, d), lambda i, j: (i, 0))
    hid = pl.BlockSpec((1, tm, f), lambda i, j: (j, i, 0))
    hid_shape = jax.ShapeDtypeStruct((N_CHIPS, s, f), BF16)
    loss_specs = [tok, pl.BlockSpec((1, d), lambda i, j: (0, 0))] if n_loss else []
    loss_shapes = [jax.ShapeDtypeStruct((s, d), F32), jax.ShapeDtypeStruct((1, d), F32)] if n_loss else []
    return pl.pallas_call(
        body,
        name=name,
        grid=(n_tiles, N_CHIPS),
        in_specs=[
            pl.BlockSpec((tm, d), lambda i, j: (i, 0)),
            pl.BlockSpec((1, d), lambda i, j: (0, 0)),
            pl.BlockSpec((1, 1, d, f), lambda i, j: (j, layer, 0, 0)),
            pl.BlockSpec((1, 1, d, f), lambda i, j: (j, layer, 0, 0)),
            pl.BlockSpec((1, 1, f, d), lambda i, j: (j, layer, 0, 0)),
        ] + [ANY] * n_gather + ([tok] if n_loss else []),
        out_specs=[tok, hid, hid] + [ANY] * n_gather + loss_specs,
        out_shape=[jax.ShapeDtypeStruct((s, d), F32), hid_shape, hid_shape]
        + [jax.ShapeDtypeStruct((N_CHIPS,) + w.shape, w.dtype) for w in gather] + loss_shapes,
        scratch_shapes=[pltpu.VMEM((tm, d), BF16)] + (_gather_scratch(n_gather) if n_gather else []),
        compiler_params=_params(2),
    )(x, gain, wg_s, wu_s, wd_s, *gather, *([target] if n_loss else []))


def _resident(block, layer):
    return pl.BlockSpec(block, lambda i, j: (0, layer, 0, 0), pipeline_mode=pl.Buffered(1))


def _rms_bwd(xv, gain, dh):
    r = lax.rsqrt(jnp.mean(xv * xv, axis=-1, keepdims=True) + EPS)
    xhat = xv * r
    dxhat = dh * gain
    dx = r * (dxhat - xhat * jnp.mean(dxhat * xhat, axis=-1, keepdims=True))
    return dx, jnp.sum(dh * xhat, axis=0, keepdims=True)


def _ffn_bwd(x, dy, gain, gate_s, up_s, wg_s, wu_s, wd_s, layer, name, scatter=()):
    s, d = x.shape
    tm = TOKEN_TILE
    f = wg_s.shape[3]

    n_scatter = len(scatter)
    n_tiles = s // tm

    def body(*refs):
        x_ref, dy_ref, g_ref, gate_ref, up_ref, wg_ref, wu_ref, wd_ref = refs[:8]
        dx_ref, dgain_ref, h_ref, dyb_ref, dg_ref, du_ref, act_ref = refs[8 + n_scatter : 15 + n_scatter]
        acc_s = refs[15 + 2 * n_scatter]
        i, j = pl.program_id(0), pl.program_id(1)
        if n_scatter:
            copies = _ChipScatter(refs[8 : 8 + n_scatter], refs[15 + n_scatter : 15 + 2 * n_scatter], refs[16 + 2 * n_scatter :])
            pl.when((i == 0) & (j == 0))(copies.begin)

        @pl.when((i == 0) & (j == 0))
        def _():
            dgain_ref[...] = jnp.zeros_like(dgain_ref)

        @pl.when(j == 0)
        def _():
            xv = x_ref[...]
            r = lax.rsqrt(jnp.mean(xv * xv, axis=-1, keepdims=True) + EPS)
            h_ref[...] = (xv * r * g_ref[...]).astype(BF16)
            dyb_ref[...] = dy_ref[...].astype(BF16)
            acc_s[...] = jnp.zeros_like(acc_s)

        halves = [slice(0, tm // 2), slice(tm // 2, tm)]
        pre = [(gate_ref[0, r, :].astype(F32), up_ref[0, r, :].astype(F32), _dot_nt(dyb_ref[r, :], wd_ref[j, 0])) for r in halves]
        grads = []
        for r, (gate, up, dact) in zip(halves, pre):
            sig = 1.0 / (1.0 + jnp.exp(-gate))
            silu = gate * sig
            dgate = (dact * up * (sig * (1.0 + gate * (1.0 - sig)))).astype(BF16)
            dup = (dact * silu).astype(BF16)
            act_ref[0, r, :] = (silu * up).astype(BF16)
            dg_ref[0, r, :] = dgate
            du_ref[0, r, :] = dup
            grads.append((dgate, dup))
        for r, (dgate, dup) in zip(halves, grads):
            acc_s[r, :] += _dot_nt(dgate, wg_ref[j, 0]) + _dot_nt(dup, wu_ref[j, 0])

        @pl.when(j == N_CHIPS - 1)
        def _():
            dxn, dgain = _rms_bwd(x_ref[...], g_ref[...], acc_s[...])
            dx_ref[...] = dy_ref[...] + dxn
            dgain_ref[...] += dgain

        if n_scatter:
            pl.when((i == n_tiles - 1) & (j == N_CHIPS - 1))(copies.finish)

    tok = pl.BlockSpec((tm, d), lambda i, j: (i, 0))
    vec = pl.BlockSpec((1, d), lambda i, j: (0, 0))
    hid = pl.BlockSpec((1, tm, f), lambda i, j: (j, i, 0))
    hid_shape = jax.ShapeDtypeStruct((N_CHIPS, s, f), BF16)
    return pl.pallas_call(
        body,
        name=name,
        grid=(n_tiles, N_CHIPS),
        in_specs=[
            tok, tok, vec, hid, hid,
            _resident((N_CHIPS, 1, d, f), layer),
            _resident((N_CHIPS, 1, d, f), layer),
            _resident((N_CHIPS, 1, f, d), layer),
        ] + [ANY] * n_scatter,
        out_specs=[tok, vec, tok, tok, hid, hid, hid] + [ANY] * n_scatter,
        out_shape=[
            jax.ShapeDtypeStruct((s, d), F32),
            jax.ShapeDtypeStruct((1, d), F32),
            jax.ShapeDtypeStruct((s, d), BF16),
            jax.ShapeDtypeStruct((s, d), BF16),
            hid_shape, hid_shape, hid_shape,
        ] + _scatter_shapes(scatter),
        scratch_shapes=[pltpu.VMEM((tm, d), F32)] + (_scatter_scratch(n_scatter) if n_scatter else []),
        compiler_params=_params(2),
    )(x, dy, gain, gate_s, up_s, wg_s, wu_s, wd_s, *scatter)


def _in_proj_bwd(x, dx_res, gain, w_s, layer, proj, q_gain, k_gain, dq, dk, dv, dproj_conv, name):
    s, d = x.shape
    tm = TOKEN_TILE
    n = w_s.shape[3]
    qkv = 3 * ATTN_DIM

    def norm_bwd(xv, head_gain, dy, low):
        _, r = _head_norm(xv, head_gain, low)
        xhat = xv * r
        dxhat = dy * head_gain
        prod = dxhat * xhat
        m_low = jnp.sum(jnp.where(low, prod, 0.0), axis=-1, keepdims=True)
        m_high = jnp.sum(jnp.where(low, 0.0, prod), axis=-1, keepdims=True)
        mean = jnp.where(low, m_low, m_high) / HEAD_DIM
        return r * (dxhat - xhat * mean), jnp.sum(dy * xhat, axis=0, keepdims=True)

    def body(x_ref, r_ref, g_ref, w_ref, p_ref, qg_ref, kg_ref, dq_ref, dk_ref, dv_ref, dpc_ref, dx_ref, dgain_ref, dp_ref, dqg_ref, dkg_ref):
        @pl.when(pl.program_id(0) == 0)
        def _():
            dgain_ref[...] = jnp.zeros_like(dgain_ref)
            dqg_ref[...] = jnp.zeros_like(dqg_ref)
            dkg_ref[...] = jnp.zeros_like(dkg_ref)

        low = lax.broadcasted_iota(jnp.int32, (tm, LANES), 1) < HEAD_DIM
        for g in range(ATTN_DIM // LANES):
            cq = slice(LANES * g, LANES * (g + 1))
            ck = slice(ATTN_DIM + LANES * g, ATTN_DIM + LANES * (g + 1))
            cv = slice(2 * ATTN_DIM + LANES * g, 2 * ATTN_DIM + LANES * (g + 1))
            dxq, dgq = norm_bwd(p_ref[:, cq], qg_ref[...], dq_ref[:, cq] * Q_SCALE, low)
            dxk, dgk = norm_bwd(p_ref[:, ck], kg_ref[...], dk_ref[:, cq], low)
            dp_ref[:, cq] = dxq.astype(BF16)
            dp_ref[:, ck] = dxk.astype(BF16)
            dp_ref[:, cv] = dv_ref[:, cq].astype(BF16)
            dqg_ref[:, cq] += dgq
            dkg_ref[:, cq] += dgk
        dp_ref[:, qkv:] = dpc_ref[...]

        dh = _dot_nt(dp_ref[:, 0:n], w_ref[0, 0])
        for j in range(1, N_CHIPS):
            dh = dh + _dot_nt(dp_ref[:, j * n : (j + 1) * n], w_ref[j, 0])
        dxn, dgain = _rms_bwd(x_ref[...], g_ref[...], dh)
        dx_ref[...] = r_ref[...] + dxn
        dgain_ref[...] += dgain

    tok = pl.BlockSpec((tm, d), lambda i: (i, 0))
    vec = pl.BlockSpec((1, d), lambda i: (0, 0))
    grad_spec = pl.BlockSpec((tm, ATTN_DIM), lambda i: (i, 0))
    gain_spec = pl.BlockSpec((1, LANES), lambda i: (0, 0))
    sum_spec = pl.BlockSpec((1, ATTN_DIM), lambda i: (0, 0))
    return pl.pallas_call(
        body,
        name=name,
        grid=(s // tm,),
        in_specs=[
            tok, tok, vec, pl.BlockSpec((N_CHIPS, 1, d, n), lambda i: (0, layer, 0, 0)),
            pl.BlockSpec((tm, qkv), lambda i: (i, 0)), gain_spec, gain_spec, grad_spec, grad_spec, grad_spec,
            pl.BlockSpec((tm, N_CHIPS * n - qkv), lambda i: (i, 1)),
        ],
        out_specs=[tok, vec, pl.BlockSpec((tm, N_CHIPS * n), lambda i: (i, 0)), sum_spec, sum_spec],
        out_shape=[
            jax.ShapeDtypeStruct((s, d), F32),
            jax.ShapeDtypeStruct((1, d), F32),
            jax.ShapeDtypeStruct((s, N_CHIPS * n), BF16),
            jax.ShapeDtypeStruct((1, ATTN_DIM), F32),
            jax.ShapeDtypeStruct((1, ATTN_DIM), F32),
        ],
        compiler_params=_params(1),
    )(x, dx_res, gain, w_s, proj, q_gain, k_gain, dq, dk, dv, dproj_conv)


def _wgrad(a, b, a_spec, b_spec, n_blocks, k, n, name):
    n_tiles = a.shape[-2] // min(WGRAD_TILE, a.shape[-2])

    def body(a_ref, b_ref, o_ref):
        @pl.when(pl.program_id(1) == 0)
        def _():
            o_ref[...] = jnp.zeros_like(o_ref)

        av = a_ref[0] if len(a_ref.shape) == 3 else a_ref[...]
        bv = b_ref[0] if len(b_ref.shape) == 3 else b_ref[...]
        o_ref[0] += _dot_tn(av, bv)

    return pl.pallas_call(
        body,
        name=name,
        grid=(n_blocks, n_tiles),
        in_specs=[a_spec, b_spec],
        out_specs=pl.BlockSpec((1, k, n), lambda j, i: (j, 0, 0)),
        out_shape=jax.ShapeDtypeStruct((n_blocks, k, n), F32),
        compiler_params=_params(2),
    )(a, b)


def _wgrad_pair(a, b1, b2, k, n, name):
    s = a.shape[0]
    tw = min(WGRAD_TILE // 2, s)

    def body(a_ref, b1_ref, b2_ref, o1_ref, o2_ref):
        @pl.when(pl.program_id(1) == 0)
        def _():
            o1_ref[...] = jnp.zeros_like(o1_ref)
            o2_ref[...] = jnp.zeros_like(o2_ref)

        av = a_ref[...]
        o1_ref[0] += _dot_tn(av, b1_ref[0])
        o2_ref[0] += _dot_tn(av, b2_ref[0])

    hid = pl.BlockSpec((1, tw, n), lambda j, i: (j, i, 0))
    out = pl.BlockSpec((1, k, n), lambda j, i: (j, 0, 0))
    shape = jax.ShapeDtypeStruct((N_CHIPS, k, n), F32)
    return pl.pallas_call(
        body,
        name=name,
        grid=(N_CHIPS, s // tw),
        in_specs=[pl.BlockSpec((tw, k), lambda j, i: (i, 0)), hid, hid],
        out_specs=[out, out],
        out_shape=[shape, shape],
        compiler_params=_params(2),
    )(a, b1, b2)


def _wgrad_out(attn, conv, dxb, rows, name):
    s, d = dxb.shape
    tw = min(WGRAD_TILE // 2, s)

    def body(a_ref, c_ref, b_ref, o_ref):
        @pl.when(pl.program_id(0) == 0)
        def _():
            o_ref[...] = jnp.zeros_like(o_ref)

        bv = b_ref[...]
        for j in range(N_CHIPS):
            src = a_ref if j < 2 else c_ref
            o_ref[j] += _dot_tn(src[:, (j % 2) * rows : (j % 2 + 1) * rows], bv)

    return pl.pallas_call(
        body,
        name=name,
        grid=(s // tw,),
        in_specs=[
            pl.BlockSpec((tw, ATTN_DIM), lambda i: (i, 0)),
            pl.BlockSpec((tw, CONV_DIM), lambda i: (i, 0)),
            pl.BlockSpec((tw, d), lambda i: (i, 0)),
        ],
        out_specs=pl.BlockSpec((N_CHIPS, rows, d), lambda i: (0, 0, 0)),
        out_shape=jax.ShapeDtypeStruct((N_CHIPS, rows, d), F32),
        compiler_params=_params(1),
    )(attn, conv, dxb)


def _mesh_position():
    return lax.axis_index("x"), lax.axis_index("y"), lax.axis_index("c")


def _other_chips(x, y):
    return [(1 - x, y), (x, 1 - y), (1 - x, 1 - y)]


def _half_rows(ref_rows, c):
    half = ref_rows // 2
    return pl.ds(c * half, half)


class _WeightGather:
    def __init__(self, ins, outs, sems):
        self.ins, self.outs = ins, outs
        send_sems, recv_sems, pass_send_sems, pass_recv_sems, self.local_sems = sems
        self.ici, self.d2d = (send_sems, recv_sems), (pass_send_sems, pass_recv_sems)
        self.x, self.y, self.c = _mesh_position()
        self.me = 2 * self.x + self.y
        self.sibling = (self.x, self.y, 1 - self.c)
        self.chips = _other_chips(self.x, self.y)

    def _copy(self, t, k, chip_index, core, to, sems, src=None):
        dst = self.outs[t].at[chip_index, :, _half_rows(self.ins[t].shape[1], core), :]
        return pltpu.make_async_remote_copy(
            src_ref=dst if src is None else src, dst_ref=dst, send_sem=sems[0].at[t, k], recv_sem=sems[1].at[t, k],
            device_id=to, device_id_type=MESH_ID,
        )

    def _own(self, t):
        return pltpu.make_async_copy(self.ins[t], self.outs[t].at[self.me], self.local_sems.at[t])

    def _sends(self):
        for t in range(len(self.ins)):
            mine = self.ins[t].at[:, _half_rows(self.ins[t].shape[1], self.c), :]
            for k, (px, py) in enumerate(self.chips):
                yield self._copy(t, k, self.me, self.c, (px, py, self.c), self.ici, src=mine)

    def _passes(self, core, sems):
        for t in range(len(self.ins)):
            for k, (px, py) in enumerate(self.chips):
                yield self._copy(t, k, 2 * px + py, core, self.sibling, sems)

    def begin(self):
        for t in range(len(self.ins)):
            self._own(t).start()
        for cp in self._sends():
            cp.start()

    def relay(self):
        for arrived, onward in zip(self._passes(self.c, self.ici), self._passes(self.c, self.d2d)):
            arrived.wait_recv()
            onward.start()

    def finish(self):
        for cp in self._passes(1 - self.c, self.d2d):
            cp.wait_recv()
        for cp in list(self._sends()) + list(self._passes(self.c, self.d2d)):
            cp.wait_send()
        for t in range(len(self.ins)):
            self._own(t).wait()


def _gather_scratch(n):
    sems = pltpu.SemaphoreType.DMA((n, N_CHIPS - 1))
    return [sems, sems, sems, sems, pltpu.SemaphoreType.DMA((n,))]


def _gather_weights(shards):
    n = len(shards)

    def body(*refs):
        gather = _WeightGather(refs[:n], refs[n : 2 * n], refs[2 * n :])
        gather.begin()
        gather.relay()
        gather.finish()

    return pl.pallas_call(
        body,
        name="gather_weights",
        in_specs=[ANY] * n,
        out_specs=[ANY] * n,
        out_shape=[jax.ShapeDtypeStruct((N_CHIPS,) + w.shape, w.dtype) for w in shards],
        scratch_shapes=_gather_scratch(n),
    )(*shards)


def _swap_halves(grads, tag):
    n = len(grads)

    def body(*refs):
        ins, outs = refs[:n], refs[n : 2 * n]
        send_sems, recv_sems = refs[2 * n :]
        x, y, c = _mesh_position()
        copies = []
        for t in range(n):
            copies.append(pltpu.make_async_remote_copy(
                src_ref=ins[t].at[:, _half_rows(ins[t].shape[1], 1 - c), :], dst_ref=outs[t],
                send_sem=send_sems.at[t], recv_sem=recv_sems.at[t], device_id=(x, y, 1 - c), device_id_type=MESH_ID,
            ))
            copies[-1].start()
        for cp in copies:
            cp.wait()

    sems = pltpu.SemaphoreType.DMA((n,))
    return pl.pallas_call(
        body,
        name=f"swap_halves_{tag}",
        in_specs=[ANY] * n,
        out_specs=[ANY] * n,
        out_shape=[jax.ShapeDtypeStruct((g.shape[0], g.shape[1] // 2, g.shape[2]), g.dtype) for g in grads],
        scratch_shapes=[sems, sems],
    )(*grads)


class _ChipScatter:
    def __init__(self, ins, outs, sems):
        self.ins, self.outs = ins, outs
        self.send_sems, self.recv_sems = sems
        self.x, self.y, self.c = _mesh_position()

    def _copies(self):
        for t in range(len(self.ins)):
            for k, (px, py) in enumerate(_other_chips(self.x, self.y)):
                yield pltpu.make_async_remote_copy(
                    src_ref=self.ins[t].at[2 * px + py], dst_ref=self.outs[t].at[k],
                    send_sem=self.send_sems.at[t, k], recv_sem=self.recv_sems.at[t, k],
                    device_id=(px, py, self.c), device_id_type=MESH_ID,
                )

    def begin(self):
        for cp in self._copies():
            cp.start()

    def finish(self):
        for cp in self._copies():
            cp.wait()


def _scatter_scratch(n):
    sems = pltpu.SemaphoreType.DMA((n, N_CHIPS - 1))
    return [sems, sems]


def _scatter_shapes(parts):
    return [jax.ShapeDtypeStruct((N_CHIPS - 1,) + p.shape[1:], p.dtype) for p in parts]


def _scatter_to_chips(parts, tag):
    n = len(parts)

    def body(*refs):
        copies = _ChipScatter(refs[:n], refs[n : 2 * n], refs[2 * n :])
        copies.begin()
        copies.finish()

    return pl.pallas_call(
        body,
        name=f"scatter_to_chips_{tag}",
        in_specs=[ANY] * n,
        out_specs=[ANY] * n,
        out_shape=_scatter_shapes(parts),
        scratch_shapes=_scatter_scratch(n),
    )(*parts)


def _join_halves(shards):
    n = len(shards)

    def body(*refs):
        outs = refs[n : 2 * n]
        send_sems, recv_sems = refs[2 * n :]
        x, y, c = _mesh_position()
        copies = []
        for t in range(n):
            mine = outs[t].at[:, _half_rows(outs[t].shape[1], c), :]
            copies.append(pltpu.make_async_remote_copy(
                src_ref=mine, dst_ref=mine, send_sem=send_sems.at[t], recv_sem=recv_sems.at[t],
                device_id=(x, y, 1 - c), device_id_type=MESH_ID,
            ))
            copies[-1].start()
        for cp in copies:
            cp.wait()

    sems = pltpu.SemaphoreType.DMA((n,))
    return pl.pallas_call(
        body,
        name="join_halves",
        in_specs=[ANY] * n,
        out_specs=[ANY] * n,
        out_shape=[jax.ShapeDtypeStruct(g.shape, g.dtype) for g in shards],
        input_output_aliases={t: t for t in range(n)},
        scratch_shapes=[sems, sems],
    )(*shards)


def _gather_small(pack):
    def body(p_ref, o_ref, send_sems, recv_sems, local_sem):
        x, y, c = _mesh_position()
        own = pltpu.make_async_copy(p_ref, o_ref.at[4 * x + 2 * y + c], local_sem)
        own.start()
        copies = []
        for k in range(1, N_DEV):
            px, py, pc = x ^ (k >> 2), y ^ ((k >> 1) & 1), c ^ (k & 1)
            send = pltpu.make_async_remote_copy(
                src_ref=p_ref, dst_ref=o_ref.at[4 * x + 2 * y + c], send_sem=send_sems.at[k - 1], recv_sem=recv_sems.at[k - 1],
                device_id=(px, py, pc), device_id_type=MESH_ID,
            )
            send.start()
            copies.append((send, 4 * px + 2 * py + pc))
        for send, peer_slot in copies:
            send.wait_send()
        for k in range(1, N_DEV):
            px, py, pc = x ^ (k >> 2), y ^ ((k >> 1) & 1), c ^ (k & 1)
            pltpu.make_async_remote_copy(
                src_ref=p_ref, dst_ref=o_ref.at[4 * px + 2 * py + pc], send_sem=send_sems.at[k - 1], recv_sem=recv_sems.at[k - 1],
                device_id=(px, py, pc), device_id_type=MESH_ID,
            ).wait_recv()
        own.wait()

    sems = pltpu.SemaphoreType.DMA((N_DEV - 1,))
    return pl.pallas_call(
        body,
        name="gather_small",
        in_specs=[VMEM_SPEC],
        out_specs=VMEM_SPEC,
        out_shape=jax.ShapeDtypeStruct((N_DEV,) + pack.shape, pack.dtype),
        scratch_shapes=[sems, sems, pltpu.SemaphoreType.DMA],
    )(pack)


def _row_tile(rows):
    for tile in range(min(rows, 512) // 8 * 8, 0, -8):
        if rows % tile == 0:
            return tile
    return rows


def _add_half(grad, received, half_index, name):
    slots, h, cdim = received.shape
    tile = _row_tile(h)
    per_half = h // tile

    def body(c_ref, g_ref, r_ref, o_ref, ob_ref):
        total = g_ref[...] + r_ref[...]
        o_ref[...] = total
        ob_ref[...] = total.astype(BF16)

    block = pl.BlockSpec((1, tile, cdim), lambda j, i, c: (j, i, 0))
    grid_spec = pltpu.PrefetchScalarGridSpec(
        num_scalar_prefetch=1,
        grid=(slots, per_half),
        in_specs=[pl.BlockSpec((1, tile, cdim), lambda j, i, c: (j, c[0] * per_half + i, 0)), block],
        out_specs=[block, block],
    )
    return pl.pallas_call(
        body, name=name, grid_spec=grid_spec,
        out_shape=[jax.ShapeDtypeStruct(received.shape, F32), jax.ShapeDtypeStruct(received.shape, BF16)],
        compiler_params=_params(2),
    )(half_index, grad, received)


def _add_chips(part, received, chip_index, core_index, layer, n_layers, shard, name):
    _, h, cdim = part.shape
    tile = _row_tile(h)
    per_half = h // tile

    def body(chip_ref, core_ref, p_ref, r_ref, *rest):
        o_ref = rest[-1]
        o_ref[0] = ((p_ref[0] + r_ref[0].astype(F32)) + r_ref[1].astype(F32)) + r_ref[2].astype(F32)

    in_specs = [
        pl.BlockSpec((1, tile, cdim), lambda i, chip, core: (chip[0], i, 0)),
        pl.BlockSpec((N_CHIPS - 1, tile, cdim), lambda i, chip, core: (0, i, 0)),
    ]
    operands = [chip_index, core_index, part, received]
    aliases = {}
    if shard is not None:
        in_specs.append(ANY)
        operands.append(shard)
        aliases = {4: 0}
    grid_spec = pltpu.PrefetchScalarGridSpec(
        num_scalar_prefetch=2,
        grid=(per_half,),
        in_specs=in_specs,
        out_specs=pl.BlockSpec((1, tile, cdim), lambda i, chip, core: (layer, core[0] * per_half + i, 0)),
    )
    return pl.pallas_call(
        body, name=name, grid_spec=grid_spec, out_shape=jax.ShapeDtypeStruct((n_layers, 2 * h, cdim), F32),
        input_output_aliases=aliases, compiler_params=_params(1),
    )(*operands)


def _adamw(w, g, m, v, name):
    rows, cdim = w.shape
    tile = _row_tile(rows)

    def body(w_ref, g_ref, m_ref, v_ref, d_ref, nm_ref, nv_ref, go_ref):
        gv = g_ref[...]
        go_ref[...] = gv
        nm = ADAM_B1 * m_ref[...] + (1.0 - ADAM_B1) * gv
        nv = ADAM_B2 * v_ref[...] + (1.0 - ADAM_B2) * (gv * gv)
        m_hat = nm / (1.0 - ADAM_B1 ** ADAM_STEP)
        v_hat = nv / (1.0 - ADAM_B2 ** ADAM_STEP)
        d_ref[...] = -ADAM_LR * (m_hat / (jnp.sqrt(v_hat) + ADAM_EPS) + ADAM_WD * w_ref[...])
        nm_ref[...] = nm
        nv_ref[...] = nv

    spec = pl.BlockSpec((tile, cdim), lambda i: (i, 0))
    shape = jax.ShapeDtypeStruct((rows, cdim), F32)
    return pl.pallas_call(
        body, name=name, grid=(rows // tile,), in_specs=[spec] * 4, out_specs=[spec] * 4, out_shape=[shape] * 4,
        compiler_params=_params(1),
    )(w, g, m, v)


SMALL_ROWS, SMALL_COLS = 24, 1024
ROW_NORM_MIX, ROW_NORM_FFN, ROW_LOSS, ROW_Q_NORM, ROW_K_NORM, ROW_CONV = 0, 2, 4, 8, 10, 16


def _sum_small(gathered):
    def body(g_ref, o_ref, heads_ref, lanes_ref):
        total = g_ref[0]
        for dev in range(1, N_DEV):
            total = total + g_ref[dev]
        o_ref[...] = total
        heads = o_ref[8:16, 0:LANES]
        for grp in range(1, ATTN_DIM // LANES):
            heads = heads + o_ref[8:16, grp * LANES : (grp + 1) * LANES]
        heads_ref[...] = heads + pltpu.roll(heads, HEAD_DIM, 1)
        lanes_ref[...] = jnp.broadcast_to(jnp.sum(o_ref[0:8, :], axis=-1, keepdims=True), (8, LANES))

    return pl.pallas_call(
        body,
        name="sum_small",
        in_specs=[VMEM_SPEC],
        out_specs=[VMEM_SPEC] * 3,
        out_shape=[jax.ShapeDtypeStruct((SMALL_ROWS, SMALL_COLS), F32), jax.ShapeDtypeStruct((8, LANES), F32), jax.ShapeDtypeStruct((8, LANES), F32)],
    )(gathered)


def _pad_rows(a, rows):
    return jnp.pad(a, ((0, rows - a.shape[0]), (0, 0)))


def _pad_to(a, rows, cols):
    return jnp.pad(a, ((0, rows - a.shape[0]), (0, cols - a.shape[1])))


def _conv_taps(conv_s):
    return jnp.transpose(conv_s[:, 0, 0:8], (1, 0, 2)).reshape(8, -1)


class _GradExchange:
    def __init__(self, chip_index, core_index, n_layers):
        self.chip_index, self.core_index, self.n_layers = chip_index, core_index, n_layers
        self.shards = {}
        self.pending = None

    def offer(self, layer, grads):
        assert self.pending is None
        names = list(grads)
        received = _swap_halves([grads[k] for k in names], f"{'_'.join(names)}_{layer}")
        parts = [_add_half(grads[k], r, self.core_index, f"add_half_{k}_{layer}") for k, r in zip(names, received)]
        self.pending = (layer, names, [p32 for p32, _ in parts], [p16 for _, p16 in parts])

    def payload(self):
        return () if self.pending is None else tuple(self.pending[3])

    def take(self, received):
        layer, names, parts, _ = self.pending
        self.pending = None
        for k, p, r in zip(names, parts, received):
            self.shards[k] = _add_chips(
                p, r, self.chip_index, self.core_index, layer, self.n_layers, self.shards.get(k), f"add_chips_{k}_{layer}")

    def finish(self):
        if self.pending is not None:
            layer, names = self.pending[0], self.pending[1]
            self.take(_scatter_to_chips(list(self.pending[3]), f"{'_'.join(names)}_{layer}"))
        return dict(zip(BIG, _join_halves([self.shards[k] for k in BIG])))


def _local_step(x, target, norm_mix, q_norm, k_norm, norm_ffn, layer_weights, exchange=None):
    layer_weights = list(layer_weights)

    def carrying(kernel_fn, n_out, *args):
        if exchange is None or exchange.pending is None:
            return kernel_fn(*args)
        out = kernel_fn(*args, scatter=exchange.payload())
        exchange.take(out[n_out:])
        return out[:n_out]

    n_layers = norm_mix.shape[0]
    s, d = x.shape
    tw = min(WGRAD_TILE, s)
    n_in = layer_weights[0][0].shape[-1]
    f = layer_weights[0][2].shape[-1]
    saved = []
    for l in range(n_layers):
        weights = list(layer_weights[l])
        q_gain = jnp.tile(q_norm[l][None, :], (1, 2))
        k_gain = jnp.tile(k_norm[l][None, :], (1, 2))
        h1, proj, qn, kn, vb = _in_proj(x, norm_mix[l][None, :], weights[0], 0, q_gain, k_gain, f"in_proj_{l}")
        missing = [n for n, w in enumerate(weights) if w.ndim == 3]
        if missing:
            attn, *arrived = _attn_fwd(qn, kn, vb, f"attn_fwd_{l}", gather=tuple(weights[n] for n in missing))
            for n, w in zip(missing, arrived):
                weights[n] = w
            layer_weights[l] = tuple(weights)
        else:
            attn = _attn_fwd(qn, kn, vb, f"attn_fwd_{l}")
        _, wout_s, wg_s, wu_s, wd_s, conv_s = weights
        taps = _conv_taps(conv_s)
        x_mid, conv = _out_proj(x, attn, proj, taps, wout_s, 0, f"out_proj_{l}")
        pending = ()
        if l + 1 < n_layers and all(w.ndim == 3 for w in layer_weights[l + 1]):
            pending = tuple(layer_weights[l + 1])
        x_out, gate, up, *more = _ffn_fwd(
            x_mid, norm_ffn[l][None, :], wg_s, wu_s, wd_s, 0, f"ffn_fwd_{l}", gather=pending, target=target if l == n_layers - 1 else None)
        if pending:
            layer_weights[l + 1] = tuple(more[: len(pending)])
        if l == n_layers - 1:
            dy, loss_lanes = more[len(pending) :]
        saved.append(dict(x=x, h1=h1, proj=proj, qn=qn, kn=kn, vb=vb, attn=attn, conv=conv, x_mid=x_mid, q_gain=q_gain, k_gain=k_gain,
                          gate=gate, up=up, taps=taps))
        x = x_out

    grads = [None] * n_layers
    for l in reversed(range(n_layers)):
        sv = saved[l]
        win_s, wout_s, wg_s, wu_s, wd_s, _ = layer_weights[l]
        dx_mid, d_norm_ffn, h2, dyb, dgate, dup, act = carrying(
            _ffn_bwd, 7, sv["x_mid"], dy, norm_ffn[l][None, :], sv["gate"], sv["up"], wg_s, wu_s, wd_s, 0, f"ffn_bwd_{l}")
        tok2 = pl.BlockSpec((tw, d), lambda j, i: (i, 0))
        hid = pl.BlockSpec((1, tw, f), lambda j, i: (j, i, 0))
        d_wg, d_wu = _wgrad_pair(h2, dgate, dup, d, f, f"wgrad_gate_up_{l}")
        d_wd = _wgrad(act, dyb, hid, tok2, N_CHIPS, f, d, f"wgrad_down_{l}")
        d_attn, d_conv, dxb = _out_proj_bwd(dx_mid, wout_s, 0, f"out_proj_bwd_{l}")
        d_wout = _wgrad_out(sv["attn"], sv["conv"], dxb, wout_s.shape[2], f"wgrad_out_{l}")
        if exchange is not None:
            exchange.offer(l, dict(w_gate=d_wg, w_up=d_wu, w_down=d_wd, w_out=d_wout))
        dq, dk, dv = carrying(_attn_bwd, 3, sv["qn"], sv["kn"], sv["vb"], d_attn, f"attn_bwd_{l}")
        dproj_conv, d_conv_w = _conv_bwd(sv["proj"], sv["taps"], d_conv, f"conv_bwd_{l}")
        dy, d_norm_mix, dproj, d_qg, d_kg = _in_proj_bwd(
            sv["x"], dx_mid, norm_mix[l][None, :], win_s, 0, sv["proj"], sv["q_gain"], sv["k_gain"], dq, dk, dv, dproj_conv, f"in_proj_bwd_{l}")
        d_win = _wgrad(sv["h1"], dproj, tok2, pl.BlockSpec((tw, n_in), lambda j, i: (i, j)), N_CHIPS, d, n_in, f"wgrad_in_{l}")
        if exchange is not None:
            exchange.offer(l, dict(w_in=d_win))
        grads[l] = dict(norm_mix=d_norm_mix, norm_ffn=d_norm_ffn, q_norm=d_qg, k_norm=d_kg, conv_w=d_conv_w,
                        w_in=d_win, w_out=d_wout, w_gate=d_wg, w_up=d_wu, w_down=d_wd)
    return loss_lanes, dy, grads


BIG = ("w_in", "w_out", "w_gate", "w_up", "w_down")


def kernel(x, norm_mix, w_in, q_norm, k_norm, conv_w, w_out, norm_ffn, w_gate, w_up, w_down, loss_target, m_norm_mix, m_w_in, m_q_norm, m_k_norm, m_conv_w, m_w_out, m_norm_ffn, m_w_gate, m_w_up, m_w_down, v_norm_mix, v_w_in, v_q_norm, v_k_norm, v_conv_w, v_w_out, v_norm_ffn, v_w_gate, v_w_up, v_w_down):
    n_layers = norm_mix.shape[0]
    weights = dict(w_in=w_in, w_out=w_out, w_gate=w_gate, w_up=w_up, w_down=w_down)
    moments_m = dict(w_in=m_w_in, w_out=m_w_out, w_gate=m_w_gate, w_up=m_w_up, w_down=m_w_down)
    moments_v = dict(w_in=v_w_in, w_out=v_w_out, w_gate=v_w_gate, w_up=v_w_up, w_down=v_w_down)
    cx, cy, cc = _mesh_position()
    chip_index = (2 * cx + cy).astype(jnp.int32).reshape(1)
    core_index = cc.astype(jnp.int32).reshape(1)

    conv_pad = jnp.pad(conv_w, ((0, 0), (0, 16 - conv_w.shape[1]), (0, 0)))

    def shards_of(layer):
        return [weights[k][layer : layer + 1].astype(BF16) for k in BIG] + [conv_pad[layer : layer + 1]]

    first = shards_of(0)
    layer_weights = [tuple(_gather_weights(first[:1])) + tuple(first[1:])] + [tuple(shards_of(layer)) for layer in range(1, n_layers)]

    exchange = _GradExchange(chip_index, core_index, n_layers)
    loss_lanes, grad_x, grads = _local_step(
        x[0], loss_target[0], norm_mix, q_norm, k_norm, norm_ffn, layer_weights, exchange)

    big_grads = exchange.finish()

    def lanes(a):
        return _pad_to(a, a.shape[0], SMALL_COLS)

    def tile_of(*groups):
        return _pad_rows(jnp.concatenate([lanes(jnp.concatenate(g, axis=0)) for g in groups], axis=0), 8)

    layers = range(n_layers)
    pack = jnp.concatenate([
        tile_of([grads[l]["norm_mix"] for l in layers], [grads[l]["norm_ffn"] for l in layers], [loss_lanes]),
        tile_of([grads[l]["q_norm"] for l in layers], [grads[l]["k_norm"] for l in layers]),
        tile_of([grads[l]["conv_w"][0:3] for l in layers]),
    ], axis=0)
    small, small_heads, small_lanes = _sum_small(_gather_small(pack))
    loss = small_lanes[ROW_LOSS, 0]
    d_model = norm_mix.shape[1]
    conv_cols = conv_w.shape[2]
    conv_all = small[ROW_CONV : ROW_CONV + 3 * n_layers, 0:CONV_DIM].reshape(n_layers, 3, CONV_DIM)
    small_grads = dict(
        norm_mix=small[ROW_NORM_MIX : ROW_NORM_MIX + n_layers, 0:d_model],
        norm_ffn=small[ROW_NORM_FFN : ROW_NORM_FFN + n_layers, 0:d_model],
        q_norm=small_heads[ROW_Q_NORM - 8 : ROW_Q_NORM - 8 + n_layers, 0:HEAD_DIM],
        k_norm=small_heads[ROW_K_NORM - 8 : ROW_K_NORM - 8 + n_layers, 0:HEAD_DIM],
        conv_w=lax.dynamic_slice_in_dim(conv_all, (2 * cx + cy) * conv_cols, conv_cols, axis=2),
    )

    out_grad, out_delta, out_m, out_v = {}, {}, {}, {}
    for k in BIG:
        shape = weights[k].shape
        view = (shape[0] * shape[1], shape[2])
        g = big_grads[k]
        delta, new_m, new_v, g = _adamw(weights[k].reshape(view), g.reshape(view), moments_m[k].reshape(view), moments_v[k].reshape(view), f"adamw_{k}")
        out_grad[k], out_delta[k], out_m[k], out_v[k] = g.reshape(shape), delta.reshape(shape), new_m.reshape(shape), new_v.reshape(shape)

    small_w = dict(norm_mix=norm_mix, norm_ffn=norm_ffn, q_norm=q_norm, k_norm=k_norm, conv_w=conv_w)
    small_m = dict(norm_mix=m_norm_mix, norm_ffn=m_norm_ffn, q_norm=m_q_norm, k_norm=m_k_norm, conv_w=m_conv_w)
    small_v = dict(norm_mix=v_norm_mix, norm_ffn=v_norm_ffn, q_norm=v_q_norm, k_norm=v_k_norm, conv_w=v_conv_w)
    order = ("norm_mix", "norm_ffn", "q_norm", "k_norm", "conv_w")

    def packed(tree):
        parts2 = [_pad_to(tree[k].reshape(-1, tree[k].shape[-1]), tree[k].reshape(-1, tree[k].shape[-1]).shape[0], SMALL_COLS) for k in order]
        return _pad_rows(jnp.concatenate(parts2, axis=0), SMALL_ROWS)

    delta_p, m_p, v_p, _ = _adamw(packed(small_w), packed(small_grads), packed(small_m), packed(small_v), "adamw_small")
    row = 0
    for k in order:
        shape = small_w[k].shape
        n_rows = 1
        for dim in shape[:-1]:
            n_rows *= dim
        cut = (slice(row, row + n_rows), slice(0, shape[-1]))
        out_grad[k] = small_grads[k]
        out_delta[k], out_m[k], out_v[k] = delta_p[cut].reshape(shape), m_p[cut].reshape(shape), v_p[cut].reshape(shape)
        row += n_rows

    names_out = ("norm_mix", "w_in", "q_norm", "k_norm", "conv_w", "w_out", "norm_ffn", "w_gate", "w_up", "w_down")
    return (loss, grad_x[None], *[out_grad[k] for k in names_out], *[out_delta[k] for k in names_out],
            *[out_m[k] for k in names_out], *[out_v[k] for k in names_out])
```

```python
import jax
import jax.numpy as jnp
from jax import lax
from jax.experimental import pallas as pl
from jax.experimental.pallas import tpu as pltpu

F32 = jnp.float32
BF16 = jnp.bfloat16

EPS = 1e-6
HEAD_DIM = 64
LANES = 128
ATTN_DIM = 512
CONV_DIM = 512
N_CHIPS = 4
N_DEV = 8
Q_SCALE = HEAD_DIM ** -0.5
ATTN_Q_TILE = 256
ATTN_TILE = 256
TOKEN_TILE = 512
WGRAD_TILE = 4096
WGRAD_CHUNK = 1024
FFN_FWD_TILE = 1024
FFN_CHUNK = 256
VMEM_LIMIT = 56 * 1024 * 1024

ADAM_LR = 0.001
ADAM_B1 = 0.9
ADAM_B2 = 0.999
ADAM_EPS = 1e-08
ADAM_WD = 0.01
ADAM_STEP = 10

MESH_ID = pl.DeviceIdType.MESH
ANY = pl.BlockSpec(memory_space=pl.ANY)
VMEM_SPEC = pl.BlockSpec(memory_space=pltpu.VMEM)


def _params(n_axes):
    return pltpu.CompilerParams(dimension_semantics=("arbitrary",) * n_axes, vmem_limit_bytes=VMEM_LIMIT)


def _dot(a, b):
    return jnp.dot(a, b, preferred_element_type=F32)


def _dot_nt(a, b):
    return lax.dot_general(a, b, (((1,), (1,)), ((), ())), preferred_element_type=F32)


def _dot_tn(a, b):
    return lax.dot_general(a, b, (((0,), (0,)), ((), ())), preferred_element_type=F32)


SCORE_MAX = 80.0
UNDERFLOW_EXIT = 90.0


def _scores(q, k):
    return jnp.minimum(_dot_nt(q, k), SCORE_MAX)


def _softplus(z):
    return jnp.log(1.0 + jnp.exp(z))


def _head_norm(xv, gain, low):
    sq = xv * xv
    s_low = jnp.sum(jnp.where(low, sq, 0.0), axis=-1, keepdims=True)
    s_high = jnp.sum(jnp.where(low, 0.0, sq), axis=-1, keepdims=True)
    r = jnp.where(low, lax.rsqrt(s_low / HEAD_DIM + EPS), lax.rsqrt(s_high / HEAD_DIM + EPS))
    return xv * r * gain, r


def _in_proj(x, gain, w_s, layer, q_gain, k_gain, name):
    s, d = x.shape
    n_blocks, _, _, n = w_s.shape
    tm = TOKEN_TILE

    def body(x_ref, g_ref, w_ref, qg_ref, kg_ref, h_ref, o_ref, q_ref, k_ref, v_ref):
        xv = x_ref[...]
        r = lax.rsqrt(jnp.mean(xv * xv, axis=-1, keepdims=True) + EPS)
        h = (xv * r * g_ref[...]).astype(BF16)
        h_ref[...] = h
        for j in range(n_blocks):
            o_ref[:, j * n : (j + 1) * n] = _dot(h, w_ref[j, 0])
        low = lax.broadcasted_iota(jnp.int32, (tm, LANES), 1) < HEAD_DIM
        for g in range(ATTN_DIM // LANES):
            cq = slice(LANES * g, LANES * (g + 1))
            ck = slice(ATTN_DIM + LANES * g, ATTN_DIM + LANES * (g + 1))
            cv = slice(2 * ATTN_DIM + LANES * g, 2 * ATTN_DIM + LANES * (g + 1))
            qn, _ = _head_norm(o_ref[:, cq], qg_ref[...], low)
            kn, _ = _head_norm(o_ref[:, ck], kg_ref[...], low)
            q_ref[:, cq] = (qn * Q_SCALE).astype(BF16)
            k_ref[:, cq] = kn.astype(BF16)
            v_ref[:, cq] = o_ref[:, cv].astype(BF16)

    head_spec = pl.BlockSpec((tm, ATTN_DIM), lambda i: (i, 0))
    head_shape = jax.ShapeDtypeStruct((s, ATTN_DIM), BF16)
    gain_spec = pl.BlockSpec((1, LANES), lambda i: (0, 0))
    return pl.pallas_call(
        body,
        name=name,
        grid=(s // tm,),
        in_specs=[
            pl.BlockSpec((tm, d), lambda i: (i, 0)),
            pl.BlockSpec((1, d), lambda i: (0, 0)),
            pl.BlockSpec((n_blocks, 1, d, n), lambda i: (0, layer, 0, 0)),
            gain_spec, gain_spec,
        ],
        out_specs=[pl.BlockSpec((tm, d), lambda i: (i, 0)), pl.BlockSpec((tm, n_blocks * n), lambda i: (i, 0)), head_spec, head_spec, head_spec],
        out_shape=[jax.ShapeDtypeStruct((s, d), BF16), jax.ShapeDtypeStruct((s, n_blocks * n), F32), head_shape, head_shape, head_shape],
        compiler_params=_params(1),
    )(x, gain, w_s, q_gain, k_gain)


def _attn_tile_consts(t):
    row = lax.broadcasted_iota(jnp.int32, (t, t), 0)
    col = lax.broadcasted_iota(jnp.int32, (t, t), 1)
    return row, col


def _triangle_sum(v, triangle):
    return _dot(v.astype(BF16), triangle)


def _attn_fwd(qn, kn, vb, name, gather=()):
    s = qn.shape[0]
    t = min(ATTN_TILE, s)
    tq = min(ATTN_Q_TILE, t)
    per_key_tile = t // tq
    n_gather = len(gather)
    n_pairs, n_blocks = ATTN_DIM // LANES, s // tq

    def body(*refs):
        q_ref, k_ref, v_ref = refs[:3]
        o_ref = refs[3 + n_gather]
        if n_gather:
            copies = _WeightGather(refs[3 : 3 + n_gather], refs[4 + n_gather : 4 + 2 * n_gather], refs[4 + 2 * n_gather :])
            first = (pl.program_id(0) == 0) & (pl.program_id(1) == 0)
            pl.when(first)(copies.begin)
            pl.when((pl.program_id(0) == n_pairs - 1) & (pl.program_id(1) == 0))(copies.relay)
        i = pl.program_id(1) // per_key_tile
        low = lax.broadcasted_iota(jnp.int32, (tq, LANES), 1) < HEAD_DIM
        row, col = _attn_tile_consts(t)
        suffix = (row > col).astype(BF16)
        first_row = (pl.program_id(1) % per_key_tile) * tq
        causal = lax.broadcasted_iota(jnp.int32, (tq, t), 1) < lax.broadcasted_iota(jnp.int32, (tq, t), 0) + first_row
        q = q_ref[...]
        zero_q = jnp.zeros_like(q)
        qh = (jnp.where(low, q, zero_q), jnp.where(low, zero_q, q))

        def step(kbs, carry, diagonal_first=False):
            chains = [(head, m) for head in range(2) for m in range(len(kbs))]
            masked = [diagonal_first and m == 0 for _, m in chains]
            ks = [k_ref[pl.ds(pl.multiple_of(kb * t, t), t), :] for kb in kbs]
            vs = [v_ref[pl.ds(pl.multiple_of(kb * t, t), t), :] for kb in kbs]
            z = [_scores(qh[head], ks[kb]) for head, kb in chains]
            sp = [_softplus(zc) for zc in z]
            sp = [jnp.where(causal, s_, 0.0) if mk else s_ for s_, mk in zip(sp, masked)]
            inside = [_triangle_sum(s_, suffix) for s_ in sp]
            after = [carry[head][1] for head in range(2)]
            log_a = []
            for n, (head, kb) in enumerate(chains):
                log_a.append(z[n] - sp[n] - inside[n] - after[head])
                after[head] = after[head] + jnp.sum(sp[n], axis=-1, keepdims=True)
            a = [jnp.exp(l_) for l_ in log_a]
            a = [jnp.where(causal, a_, 0.0) if mk else a_ for a_, mk in zip(a, masked)]
            acc = [carry[head][0] for head in range(2)]
            for n, (head, kb) in enumerate(chains):
                acc[head] = acc[head] + _dot(a[n].astype(BF16), vs[kb])
            return tuple((acc[head], after[head]) for head in range(2))

        def live(c):
            return jnp.minimum(jnp.min(c[0][1]), jnp.min(c[1][1])) < UNDERFLOW_EXIT

        zero = (jnp.zeros((tq, LANES), F32), jnp.zeros((tq, 1), F32))
        start = lax.cond(i >= 1, lambda c: step((i, i - 1), c, True), lambda c: step((i,), c, True), (zero, zero))
        o_ref[...] = jnp.where(low, start[0][0], start[1][0]).astype(BF16)
        rest = jnp.maximum(i - 1, 0)

        @pl.when((rest > 0) & live(start))
        def _():
            carry = lax.cond(rest % 2 == 1, lambda c: step((i - 2,), c), lambda c: c, start)
            pairs = rest // 2
            _, carry = lax.while_loop(
                lambda st: (st[0] < pairs) & live(st[1]),
                lambda st: (st[0] + 1, step((2 * (pairs - st[0]) - 1, 2 * (pairs - st[0]) - 2), st[1])),
                (jnp.int32(0), carry))
            o_ref[...] = jnp.where(low, carry[0][0], carry[1][0]).astype(BF16)

        if n_gather:
            pl.when((pl.program_id(0) == n_pairs - 1) & (pl.program_id(1) == n_blocks - 1))(copies.finish)

    out = pl.pallas_call(
        body,
        name=name,
        grid=(n_pairs, n_blocks),
        in_specs=[
            pl.BlockSpec((tq, LANES), lambda p, i: (i, p)),
            pl.BlockSpec((s, LANES), lambda p, i: (0, p)),
            pl.BlockSpec((s, LANES), lambda p, i: (0, p)),
        ] + [ANY] * n_gather,
        out_specs=[pl.BlockSpec((tq, LANES), lambda p, i: (i, p))] + [ANY] * n_gather,
        out_shape=[jax.ShapeDtypeStruct((s, ATTN_DIM), BF16)] + [jax.ShapeDtypeStruct((N_CHIPS,) + w.shape, w.dtype) for w in gather],
        scratch_shapes=_gather_scratch(n_gather) if n_gather else [],
        compiler_params=_params(2),
    )(qn, kn, vb, *gather)
    return out if n_gather else out[0]


def _attn_bwd(qn, kn, vb, do, name, scatter=()):
    s = qn.shape[0]
    t = min(ATTN_TILE, s)
    nq = s // t
    n_scatter = len(scatter)
    n_pairs = ATTN_DIM // LANES

    def body(*refs):
        q_ref, k_ref, v_ref, do_ref = refs[:4]
        dq_ref, dk_ref, dv_ref = refs[4 + n_scatter : 7 + n_scatter]
        a_s, sg_s, a_f, sg_f = refs[7 + 2 * n_scatter : 11 + 2 * n_scatter]
        i = pl.program_id(1)
        if n_scatter:
            copies = _ChipScatter(refs[4 : 4 + n_scatter], refs[7 + n_scatter : 7 + 2 * n_scatter], refs[11 + 2 * n_scatter :])
            pl.when((pl.program_id(0) == 0) & (i == 0))(copies.begin)

        @pl.when(i == 0)
        def _():
            dk_ref[...] = jnp.zeros_like(dk_ref)
            dv_ref[...] = jnp.zeros_like(dv_ref)

        low = lax.broadcasted_iota(jnp.int32, (t, LANES), 1) < HEAD_DIM
        row, col = _attn_tile_consts(t)
        suffix = (row > col).astype(BF16)
        prefix = (row < col).astype(BF16)
        causal = col < row
        q = q_ref[...]
        dob = do_ref[...]
        zero_q = jnp.zeros_like(q)
        qhs = (jnp.where(low, q, zero_q), jnp.where(low, zero_q, q))
        dohs = (jnp.where(low, dob, zero_q), jnp.where(low, zero_q, dob))

        def rows_of(kb):
            return pl.ds(pl.multiple_of(kb * t, t), t)

        pair = [(head, m) for head in range(2) for m in range(2)]

        def short_pass1():
            z = [_scores(qhs[head], k_ref[rows_of(i - m), :]) for head, m in pair]
            sp = [_softplus(z_) for z_ in z]
            sp = [jnp.where(causal, s_, 0.0) if m == 0 else s_ for s_, (_, m) in zip(sp, pair)]
            inside = [_triangle_sum(s_, suffix) for s_ in sp]
            after = [jnp.zeros((t, 1), F32), jnp.zeros((t, 1), F32)]
            for n, (head, m) in enumerate(pair):
                log_sg = z[n] - sp[n]
                a = jnp.exp(log_sg - inside[n] - after[head])
                sg = jnp.exp(log_sg)
                if m == 0:
                    a = jnp.where(causal, a, 0.0)
                    sg = jnp.where(causal, sg, 0.0)
                a_f[n] = a
                sg_f[n] = sg
                after[head] = after[head] + jnp.sum(sp[n], axis=-1, keepdims=True)
            return jnp.minimum(jnp.min(after[0]), jnp.min(after[1])) >= UNDERFLOW_EXIT

        def short_pass2():
            order = [(head, m) for head in range(2) for m in (1, 0)]
            a = {c: a_f[pair.index(c)] for c in order}
            g = {c: a[c] * _dot_nt(dohs[c[0]], v_ref[rows_of(i - c[1]), :]) for c in order}
            for m in (1, 0):
                dv_ref[rows_of(i - m), :] += _dot_tn(a[(0, m)].astype(BF16), dohs[0]) + _dot_tn(a[(1, m)].astype(BF16), dohs[1])
            inside = {c: _triangle_sum(g[c], prefix) for c in order}
            before = [jnp.zeros((t, 1), F32), jnp.zeros((t, 1), F32)]
            dz = {}
            for c in order:
                sg = sg_f[pair.index(c)]
                dz[c] = (g[c] - sg * (g[c] + inside[c] + before[c[0]])).astype(BF16)
                before[c[0]] = before[c[0]] + jnp.sum(g[c], axis=-1, keepdims=True)
            for m in (1, 0):
                dk_ref[rows_of(i - m), :] += _dot_tn(dz[(0, m)], qhs[0]) + _dot_tn(dz[(1, m)], qhs[1])
            dq = [_dot(dz[(head, 1)], k_ref[rows_of(i - 1), :]) + _dot(dz[(head, 0)], k_ref[rows_of(i), :]) for head in range(2)]
            dq_ref[...] = jnp.where(low, dq[0], dq[1])

        def general_walk():
            heads = []
            for head in range(2):
                qh, doh = qhs[head], dohs[head]

                def pass1(kbs, after, diagonal_first=False):
                    z = [_scores(qh, k_ref[rows_of(kb), :]) for kb in kbs]
                    sp = [_softplus(z_) for z_ in z]
                    if diagonal_first:
                        sp[0] = jnp.where(causal, sp[0], 0.0)
                    inside = [_triangle_sum(s_, suffix) for s_ in sp]
                    for n, kb in enumerate(kbs):
                        log_sg = z[n] - sp[n]
                        a = jnp.exp(log_sg - inside[n] - after)
                        sg = jnp.exp(log_sg)
                        if diagonal_first and n == 0:
                            a = jnp.where(causal, a, 0.0)
                            sg = jnp.where(causal, sg, 0.0)
                        a_s[kb] = a
                        sg_s[kb] = sg
                        after = after + jnp.sum(sp[n], axis=-1, keepdims=True)
                    return after

                def live(after):
                    return jnp.min(after) < UNDERFLOW_EXIT

                after = jnp.zeros((t, 1), F32)
                after = lax.cond(i >= 1, lambda c: pass1((i, i - 1), c, True), lambda c: pass1((i,), c, True), after)
                rest = jnp.maximum(i - 1, 0)
                take_single = (rest % 2 == 1) & live(after)
                after = lax.cond(take_single, lambda c: pass1((i - 2,), c), lambda c: c, after)
                pairs = rest // 2
                pairs_done, _ = lax.while_loop(
                    lambda st: (st[0] < pairs) & live(st[1]),
                    lambda st: (st[0] + 1, pass1((2 * (pairs - st[0]) - 1, 2 * (pairs - st[0]) - 2), st[1])),
                    (jnp.int32(0), after))
                walked = jnp.minimum(i, 1) + 1 + take_single.astype(jnp.int32) + 2 * pairs_done
                first = i - walked + 1

                def pass2(kbs, carry):
                    dq, before = carry
                    ks = [k_ref[rows_of(kb), :] for kb in kbs]
                    a = [a_s[kb] for kb in kbs]
                    g = [a_ * _dot_nt(doh, v_ref[rows_of(kb), :]) for a_, kb in zip(a, kbs)]
                    for n, kb in enumerate(kbs):
                        dv_ref[rows_of(kb), :] += _dot_tn(a[n].astype(BF16), doh)
                    inside = [_triangle_sum(g_, prefix) for g_ in g]
                    dz = []
                    for n, kb in enumerate(kbs):
                        sg = sg_s[kb]
                        dz.append((g[n] - sg * (g[n] + inside[n] + before)).astype(BF16))
                        before = before + jnp.sum(g[n], axis=-1, keepdims=True)
                    for n, kb in enumerate(kbs):
                        dk_ref[rows_of(kb), :] += _dot_tn(dz[n], qh)
                    for n in range(len(kbs)):
                        dq = dq + _dot(dz[n], ks[n])
                    return dq, before

                carry = (jnp.zeros((t, LANES), F32), jnp.zeros((t, 1), F32))
                carry = lax.fori_loop(0, walked // 2, lambda n, c: pass2((first + 2 * n, first + 2 * n + 1), c), carry)
                carry = lax.cond(walked % 2 == 1, lambda c: pass2((i,), c), lambda c: c, carry)
                heads.append(carry[0])
            dq_ref[...] = jnp.where(low, heads[0], heads[1])

        short = lax.cond(i >= 1, short_pass1, lambda: jnp.bool_(False))
        pl.when(short)(short_pass2)
        pl.when(jnp.logical_not(short))(general_walk)
        if n_scatter:
            pl.when((pl.program_id(0) == n_pairs - 1) & (i == nq - 1))(copies.finish)

    q_spec = pl.BlockSpec((t, LANES), lambda p, i: (i, p))
    kv_spec = pl.BlockSpec((s, LANES), lambda p, i: (0, p))
    return pl.pallas_call(
        body,
        name=name,
        grid=(n_pairs, nq),
        in_specs=[q_spec, kv_spec, kv_spec, q_spec] + [ANY] * n_scatter,
        out_specs=[q_spec, kv_spec, kv_spec] + [ANY] * n_scatter,
        out_shape=[jax.ShapeDtypeStruct((s, ATTN_DIM), F32)] * 3 + _scatter_shapes(scatter),
        scratch_shapes=[pltpu.VMEM((nq, t, t), F32), pltpu.VMEM((nq, t, t), F32), pltpu.VMEM((4, t, t), F32), pltpu.VMEM((4, t, t), F32)]
        + (_scatter_scratch(n_scatter) if n_scatter else []),
        compiler_params=_params(2),
    )(qn, kn, vb, do, *scatter)


CB_BLOCK, CC_BLOCK, CU_BLOCK = 3, 4, 5


def _shift_down(h, prev_rows, n):
    row = lax.broadcasted_iota(jnp.int32, h.shape, 0)
    out = pltpu.roll(h, n, 0)
    for r in range(n):
        out = jnp.where(row == r, prev_rows[len(prev_rows) - n + r], out)
    return out


def _shift_up(h, next_rows, n):
    tm = h.shape[0]
    row = lax.broadcasted_iota(jnp.int32, h.shape, 0)
    out = pltpu.roll(h, tm - n, 0)
    for r in range(n):
        out = jnp.where(row == tm - n + r, next_rows[r], out)
    return out


def _conv_bwd(proj, conv_w, dconv, name):
    s = proj.shape[0]
    tm = TOKEN_TILE
    nb = tm // 8
    n_tiles = s // tm

    def body(cb_ref, cc_ref, cu_ref, dy_ref, pc_ref, pu_ref, nb_ref, ndy_ref, w_ref, dp_ref, dw_ref):
        i = pl.program_id(0)

        @pl.when(i == 0)
        def _():
            dw_ref[...] = jnp.zeros_like(dw_ref)

        first = i == 0
        last = i == n_tiles - 1
        cc, cu, cb, dy = cc_ref[...], cu_ref[...], cb_ref[...], dy_ref[...]
        h = cc * cu
        prev = [jnp.where(first, 0.0, pc_ref[r : r + 1, :] * pu_ref[r : r + 1, :]) for r in (6, 7)]
        h1 = _shift_down(h, prev, 1)
        h2 = _shift_down(h, prev, 2)
        y = w_ref[0:1, :] * h2 + w_ref[1:2, :] * h1 + w_ref[2:3, :] * h
        dyb = dy * cb
        nxt = [jnp.where(last, 0.0, ndy_ref[r : r + 1, :] * nb_ref[r : r + 1, :]) for r in (0, 1)]
        dh = w_ref[2:3, :] * dyb + w_ref[1:2, :] * _shift_up(dyb, nxt, 1) + w_ref[0:1, :] * _shift_up(dyb, nxt, 2)
        dp_ref[:, 0:CONV_DIM] = (dy * y).astype(BF16)
        dp_ref[:, CONV_DIM : 2 * CONV_DIM] = (dh * cu).astype(BF16)
        dp_ref[:, 2 * CONV_DIM : 3 * CONV_DIM] = (dh * cc).astype(BF16)
        dw_ref[0:1, :] += jnp.sum(dyb * h2, axis=0, keepdims=True)
        dw_ref[1:2, :] += jnp.sum(dyb * h1, axis=0, keepdims=True)
        dw_ref[2:3, :] += jnp.sum(dyb * h, axis=0, keepdims=True)

    def col(block):
        return pl.BlockSpec((tm, CONV_DIM), lambda i: (i, block))

    def halo_prev(block):
        return pl.BlockSpec((8, CONV_DIM), lambda i: (jnp.maximum(i * nb - 1, 0), block))

    def halo_next(block):
        return pl.BlockSpec((8, CONV_DIM), lambda i: (jnp.minimum((i + 1) * nb, s // 8 - 1), block))

    return pl.pallas_call(
        body,
        name=name,
        grid=(n_tiles,),
        in_specs=[
            col(CB_BLOCK), col(CC_BLOCK), col(CU_BLOCK), col(0),
            halo_prev(CC_BLOCK), halo_prev(CU_BLOCK), halo_next(CB_BLOCK), halo_next(0),
            pl.BlockSpec((8, CONV_DIM), lambda i: (0, 0)),
        ],
        out_specs=[pl.BlockSpec((tm, 3 * CONV_DIM), lambda i: (i, 1)), pl.BlockSpec((8, CONV_DIM), lambda i: (0, 0))],
        out_shape=[jax.ShapeDtypeStruct((s, 3 * ATTN_DIM + 3 * CONV_DIM), BF16), jax.ShapeDtypeStruct((8, CONV_DIM), F32)],
        compiler_params=_params(1),
    )(proj, proj, proj, dconv, proj, proj, proj, dconv, conv_w)


def _out_proj(x, attn, proj, conv_w, w_s, layer, name):
    s, d = x.shape
    tm = TOKEN_TILE
    nb = tm // 8
    rows = w_s.shape[2]

    def body(x_ref, a_ref, cb_ref, cc_ref, cu_ref, pc_ref, pu_ref, cw_ref, w_ref, o_ref, c_ref):
        first = pl.program_id(0) == 0
        h = cc_ref[...] * cu_ref[...]
        prev = [jnp.where(first, 0.0, pc_ref[r : r + 1, :] * pu_ref[r : r + 1, :]) for r in (6, 7)]
        y = cw_ref[0:1, :] * _shift_down(h, prev, 2) + cw_ref[1:2, :] * _shift_down(h, prev, 1) + cw_ref[2:3, :] * h
        c_ref[...] = (cb_ref[...] * y).astype(BF16)
        acc = x_ref[...]
        for j in range(N_CHIPS):
            src = a_ref if j < 2 else c_ref
            cols = slice((j % 2) * rows, (j % 2 + 1) * rows)
            acc = acc + _dot(src[:, cols], w_ref[j, 0])
        o_ref[...] = acc

    def col(block):
        return pl.BlockSpec((tm, CONV_DIM), lambda i: (i, block))

    def halo(block):
        return pl.BlockSpec((8, CONV_DIM), lambda i: (jnp.maximum(i * nb - 1, 0), block))

    return pl.pallas_call(
        body,
        name=name,
        grid=(s // tm,),
        in_specs=[
            pl.BlockSpec((tm, d), lambda i: (i, 0)),
            pl.BlockSpec((tm, ATTN_DIM), lambda i: (i, 0)),
            col(CB_BLOCK), col(CC_BLOCK), col(CU_BLOCK), halo(CC_BLOCK), halo(CU_BLOCK),
            pl.BlockSpec((8, CONV_DIM), lambda i: (0, 0)),
            pl.BlockSpec((N_CHIPS, 1, rows, d), lambda i: (0, layer, 0, 0)),
        ],
        out_specs=[pl.BlockSpec((tm, d), lambda i: (i, 0)), pl.BlockSpec((tm, CONV_DIM), lambda i: (i, 0))],
        out_shape=[jax.ShapeDtypeStruct((s, d), F32), jax.ShapeDtypeStruct((s, CONV_DIM), BF16)],
        compiler_params=_params(1),
    )(x, attn, proj, proj, proj, proj, proj, conv_w, w_s)


def _out_proj_bwd(dx, w_s, layer, name):
    s, d = dx.shape
    tm = TOKEN_TILE
    rows = w_s.shape[2]

    def body(dx_ref, w_ref, da_ref, dc_ref, dxb_ref):
        dxb = dx_ref[...].astype(BF16)
        dxb_ref[...] = dxb
        for j in range(N_CHIPS):
            cols = slice((j % 2) * rows, (j % 2 + 1) * rows)
            part = _dot_nt(dxb, w_ref[j, 0])
            if j < 2:
                da_ref[:, cols] = part.astype(BF16)
            else:
                dc_ref[:, cols] = part

    return pl.pallas_call(
        body,
        name=name,
        grid=(s // tm,),
        in_specs=[pl.BlockSpec((tm, d), lambda i: (i, 0)), pl.BlockSpec((N_CHIPS, 1, rows, d), lambda i: (0, layer, 0, 0))],
        out_specs=[
            pl.BlockSpec((tm, ATTN_DIM), lambda i: (i, 0)),
            pl.BlockSpec((tm, CONV_DIM), lambda i: (i, 0)),
            pl.BlockSpec((tm, d), lambda i: (i, 0)),
        ],
        out_shape=[
            jax.ShapeDtypeStruct((s, ATTN_DIM), BF16),
            jax.ShapeDtypeStruct((s, CONV_DIM), F32),
            jax.ShapeDtypeStruct((s, d), BF16),
        ],
        compiler_params=_params(1),
    )(dx, w_s)


def _ffn_fwd(x, gain, wg_s, wu_s, wd_s, layer, name, gather=(), target=None):
    s, d = x.shape
    tm = min(FFN_FWD_TILE, s)
    n_loss = 0 if target is None else 1
    f = wg_s.shape[3]
    n_gather = len(gather)
    n_tiles = s // tm

    def body(*refs):
        x_ref, g_ref, wg_ref, wu_ref, wd_ref = refs[:5]
        n_in = 5 + n_gather + n_loss
        o_ref, gate_ref, up_ref = refs[n_in : n_in + 3]
        h_s = refs[n_in + 3 + n_gather + 2 * n_loss]
        i, j = pl.program_id(0), pl.program_id(1)
        if n_gather:
            copies = _WeightGather(refs[5 : 5 + n_gather], refs[n_in + 3 : n_in + 3 + n_gather], refs[n_in + 4 + n_gather + 2 * n_loss :])
            pl.when((i == 0) & (j == 0))(copies.begin)
            pl.when((i == (3 * n_tiles) // 4) & (j == 0))(copies.relay)

        @pl.when(j == 0)
        def _():
            xv = x_ref[...]
            r = lax.rsqrt(jnp.mean(xv * xv, axis=-1, keepdims=True) + EPS)
            h_s[...] = (xv * r * g_ref[...]).astype(BF16)
            o_ref[...] = xv

        halves = [slice(r, r + FFN_CHUNK) for r in range(0, tm, FFN_CHUNK)]
        pre = [(_dot(h_s[r, :], wg_ref[0, 0]), _dot(h_s[r, :], wu_ref[0, 0])) for r in halves]
        act = [((gate / (1.0 + jnp.exp(-gate))) * up).astype(BF16) for gate, up in pre]
        for r, (gate, up) in zip(halves, pre):
            gate_ref[0, r, :] = gate.astype(BF16)
            up_ref[0, r, :] = up.astype(BF16)
        for r, a in zip(halves, act):
            o_ref[r, :] += _dot(a, wd_ref[0, 0])

        if n_loss:
            t_ref = refs[5 + n_gather]
            dy_ref, l_ref = refs[n_in + 3 + n_gather : n_in + 5 + n_gather]

            @pl.when((i == 0) & (j == 0))
            def _():
                l_ref[...] = jnp.zeros_like(l_ref)

            @pl.when(j == N_CHIPS - 1)
            def _():
                err = o_ref[...] - t_ref[...]
                dy_ref[...] = err / d
                l_ref[...] += jnp.sum(err * err, axis=0, keepdims=True) * (0.5 / d)

        if n_gather:
            pl.when((i == n_tiles - 1) & (j == N_CHIPS - 1))(copies.finish)

    tok = pl.BlockSpec((tm, d), lambda i, j: (i, 0))
    hid = pl.BlockSpec((1, tm, f), lambda i, j: (j, i, 0))
    hid_shape = jax.ShapeDtypeStruct((N_CHIPS, s, f), BF16)
    loss_specs = [tok, pl.BlockSpec((1, d), lambda i, j: (0, 0))] if n_loss else []
    loss_shapes = [jax.ShapeDtypeStruct((s, d), F32), jax.ShapeDtypeStruct((1, d), F32)] if n_loss else []
    return pl.pallas_call(
        body,
        name=name,
        grid=(n_tiles, N_CHIPS),
        in_specs=[
            pl.BlockSpec((tm, d), lambda i, j: (i, 0)),
            pl.BlockSpec((1, d), lambda i, j: (0, 0)),
            pl.BlockSpec((1, 1, d, f), lambda i, j: (j, layer, 0, 0)),
            pl.BlockSpec((1, 1, d, f), lambda i, j: (j, layer, 0, 0)),
            pl.BlockSpec((1, 1, f, d), lambda i, j: (j, layer, 0, 0)),
        ] + [ANY] * n_gather + ([tok] if n_loss else []),
        out_specs=[tok, hid, hid] + [ANY] * n_gather + loss_specs,
        out_shape=[jax.ShapeDtypeStruct((s, d), F32), hid_shape, hid_shape]
        + [jax.ShapeDtypeStruct((N_CHIPS,) + w.shape, w.dtype) for w in gather] + loss_shapes,
        scratch_shapes=[pltpu.VMEM((tm, d), BF16)] + (_gather_scratch(n_gather) if n_gather else []),
        compiler_params=_params(2),
    )(x, gain, wg_s, wu_s, wd_s, *gather, *([target] if n_loss else []))


def _resident(block, layer):
    return pl.BlockSpec(block, lambda i, j: (0, layer, 0, 0), pipeline_mode=pl.Buffered(1))


def _rms_bwd(xv, gain, dh):
    r = lax.rsqrt(jnp.mean(xv * xv, axis=-1, keepdims=True) + EPS)
    xhat = xv * r
    dxhat = dh * gain
    dx = r * (dxhat - xhat * jnp.mean(dxhat * xhat, axis=-1, keepdims=True))
    return dx, jnp.sum(dh * xhat, axis=0, keepdims=True)


def _ffn_bwd(x, dy, gain, gate_s, up_s, wg_s, wu_s, wd_s, layer, name, scatter=()):
    s, d = x.shape
    tm = TOKEN_TILE
    f = wg_s.shape[3]

    n_scatter = len(scatter)
    n_tiles = s // tm

    def body(*refs):
        x_ref, dy_ref, g_ref, gate_ref, up_ref, wg_ref, wu_ref, wd_ref = refs[:8]
        dx_ref, dgain_ref, h_ref, dyb_ref, dg_ref, du_ref, act_ref = refs[8 + n_scatter : 15 + n_scatter]
        acc_s = refs[15 + 2 * n_scatter]
        i, j = pl.program_id(0), pl.program_id(1)
        if n_scatter:
            copies = _ChipScatter(refs[8 : 8 + n_scatter], refs[15 + n_scatter : 15 + 2 * n_scatter], refs[16 + 2 * n_scatter :])
            pl.when((i == 0) & (j == 0))(copies.begin)

        @pl.when((i == 0) & (j == 0))
        def _():
            dgain_ref[...] = jnp.zeros_like(dgain_ref)

        @pl.when(j == 0)
        def _():
            xv = x_ref[...]
            r = lax.rsqrt(jnp.mean(xv * xv, axis=-1, keepdims=True) + EPS)
            h_ref[...] = (xv * r * g_ref[...]).astype(BF16)
            dyb_ref[...] = dy_ref[...].astype(BF16)
            acc_s[...] = jnp.zeros_like(acc_s)

        halves = [slice(0, tm // 2), slice(tm // 2, tm)]
        pre = [(gate_ref[0, r, :].astype(F32), up_ref[0, r, :].astype(F32), _dot_nt(dyb_ref[r, :], wd_ref[j, 0])) for r in halves]
        grads = []
        for r, (gate, up, dact) in zip(halves, pre):
            sig = 1.0 / (1.0 + jnp.exp(-gate))
            silu = gate * sig
            dgate = (dact * up * (sig * (1.0 + gate * (1.0 - sig)))).astype(BF16)
            dup = (dact * silu).astype(BF16)
            act_ref[0, r, :] = (silu * up).astype(BF16)
            dg_ref[0, r, :] = dgate
            du_ref[0, r, :] = dup
            grads.append((dgate, dup))
        for r, (dgate, dup) in zip(halves, grads):
            acc_s[r, :] += _dot_nt(dgate, wg_ref[j, 0]) + _dot_nt(dup, wu_ref[j, 0])

        @pl.when(j == N_CHIPS - 1)
        def _():
            dxn, dgain = _rms_bwd(x_ref[...], g_ref[...], acc_s[...])
            dx_ref[...] = dy_ref[...] + dxn
            dgain_ref[...] += dgain

        if n_scatter:
            pl.when((i == n_tiles - 1) & (j == N_CHIPS - 1))(copies.finish)

    tok = pl.BlockSpec((tm, d), lambda i, j: (i, 0))
    vec = pl.BlockSpec((1, d), lambda i, j: (0, 0))
    hid = pl.BlockSpec((1, tm, f), lambda i, j: (j, i, 0))
    hid_shape = jax.ShapeDtypeStruct((N_CHIPS, s, f), BF16)
    return pl.pallas_call(
        body,
        name=name,
        grid=(n_tiles, N_CHIPS),
        in_specs=[
            tok, tok, vec, hid, hid,
            _resident((N_CHIPS, 1, d, f), layer),
            _resident((N_CHIPS, 1, d, f), layer),
            _resident((N_CHIPS, 1, f, d), layer),
        ] + [ANY] * n_scatter,
        out_specs=[tok, vec, tok, tok, hid, hid, hid] + [ANY] * n_scatter,
        out_shape=[
            jax.ShapeDtypeStruct((s, d), F32),
            jax.ShapeDtypeStruct((1, d), F32),
            jax.ShapeDtypeStruct((s, d), BF16),
            jax.ShapeDtypeStruct((s, d), BF16),
            hid_shape, hid_shape, hid_shape,
        ] + _scatter_shapes(scatter),
        scratch_shapes=[pltpu.VMEM((tm, d), F32)] + (_scatter_scratch(n_scatter) if n_scatter else []),
        compiler_params=_params(2),
    )(x, dy, gain, gate_s, up_s, wg_s, wu_s, wd_s, *scatter)


def _in_proj_bwd(x, dx_res, gain, w_s, layer, proj, q_gain, k_gain, dq, dk, dv, dproj_conv, name):
    s, d = x.shape
    tm = TOKEN_TILE
    n = w_s.shape[3]
    qkv = 3 * ATTN_DIM

    def norm_bwd(xv, head_gain, dy, low):
        _, r = _head_norm(xv, head_gain, low)
        xhat = xv * r
        dxhat = dy * head_gain
        prod = dxhat * xhat
        m_low = jnp.sum(jnp.where(low, prod, 0.0), axis=-1, keepdims=True)
        m_high = jnp.sum(jnp.where(low, 0.0, prod), axis=-1, keepdims=True)
        mean = jnp.where(low, m_low, m_high) / HEAD_DIM
        return r * (dxhat - xhat * mean), jnp.sum(dy * xhat, axis=0, keepdims=True)

    def body(x_ref, r_ref, g_ref, w_ref, p_ref, qg_ref, kg_ref, dq_ref, dk_ref, dv_ref, dpc_ref, dx_ref, dgain_ref, dp_ref, dqg_ref, dkg_ref):
        @pl.when(pl.program_id(0) == 0)
        def _():
            dgain_ref[...] = jnp.zeros_like(dgain_ref)
            dqg_ref[...] = jnp.zeros_like(dqg_ref)
            dkg_ref[...] = jnp.zeros_like(dkg_ref)

        low = lax.broadcasted_iota(jnp.int32, (tm, LANES), 1) < HEAD_DIM
        for g in range(ATTN_DIM // LANES):
            cq = slice(LANES * g, LANES * (g + 1))
            ck = slice(ATTN_DIM + LANES * g, ATTN_DIM + LANES * (g + 1))
            cv = slice(2 * ATTN_DIM + LANES * g, 2 * ATTN_DIM + LANES * (g + 1))
            dxq, dgq = norm_bwd(p_ref[:, cq], qg_ref[...], dq_ref[:, cq] * Q_SCALE, low)
            dxk, dgk = norm_bwd(p_ref[:, ck], kg_ref[...], dk_ref[:, cq], low)
            dp_ref[:, cq] = dxq.astype(BF16)
            dp_ref[:, ck] = dxk.astype(BF16)
            dp_ref[:, cv] = dv_ref[:, cq].astype(BF16)
            dqg_ref[:, cq] += dgq
            dkg_ref[:, cq] += dgk
        dp_ref[:, qkv:] = dpc_ref[...]

        dh = _dot_nt(dp_ref[:, 0:n], w_ref[0, 0])
        for j in range(1, N_CHIPS):
            dh = dh + _dot_nt(dp_ref[:, j * n : (j + 1) * n], w_ref[j, 0])
        dxn, dgain = _rms_bwd(x_ref[...], g_ref[...], dh)
        dx_ref[...] = r_ref[...] + dxn
        dgain_ref[...] += dgain

    tok = pl.BlockSpec((tm, d), lambda i: (i, 0))
    vec = pl.BlockSpec((1, d), lambda i: (0, 0))
    grad_spec = pl.BlockSpec((tm, ATTN_DIM), lambda i: (i, 0))
    gain_spec = pl.BlockSpec((1, LANES), lambda i: (0, 0))
    sum_spec = pl.BlockSpec((1, ATTN_DIM), lambda i: (0, 0))
    return pl.pallas_call(
        body,
        name=name,
        grid=(s // tm,),
        in_specs=[
            tok, tok, vec, pl.BlockSpec((N_CHIPS, 1, d, n), lambda i: (0, layer, 0, 0)),
            pl.BlockSpec((tm, qkv), lambda i: (i, 0)), gain_spec, gain_spec, grad_spec, grad_spec, grad_spec,
            pl.BlockSpec((tm, N_CHIPS * n - qkv), lambda i: (i, 1)),
        ],
        out_specs=[tok, vec, pl.BlockSpec((tm, N_CHIPS * n), lambda i: (i, 0)), sum_spec, sum_spec],
        out_shape=[
            jax.ShapeDtypeStruct((s, d), F32),
            jax.ShapeDtypeStruct((1, d), F32),
            jax.ShapeDtypeStruct((s, N_CHIPS * n), BF16),
            jax.ShapeDtypeStruct((1, ATTN_DIM), F32),
            jax.ShapeDtypeStruct((1, ATTN_DIM), F32),
        ],
        compiler_params=_params(1),
    )(x, dx_res, gain, w_s, proj, q_gain, k_gain, dq, dk, dv, dproj_conv)


def _wgrad(a, b, a_spec, b_spec, n_blocks, k, n, name):
    n_tiles = a.shape[-2] // min(WGRAD_TILE, a.shape[-2])

    def body(a_ref, b_ref, o_ref):
        @pl.when(pl.program_id(1) == 0)
        def _():
            o_ref[...] = jnp.zeros_like(o_ref)

        rows = a_ref.shape[-2]
        chunk = min(WGRAD_CHUNK, rows)
        total = None
        for r in range(0, rows, chunk):
            av = a_ref[0, r : r + chunk, :] if len(a_ref.shape) == 3 else a_ref[r : r + chunk, :]
            bv = b_ref[0, r : r + chunk, :] if len(b_ref.shape) == 3 else b_ref[r : r + chunk, :]
            part = _dot_tn(av, bv)
            total = part if total is None else total + part
        o_ref[0] += total

    return pl.pallas_call(
        body,
        name=name,
        grid=(n_blocks, n_tiles),
        in_specs=[a_spec, b_spec],
        out_specs=pl.BlockSpec((1, k, n), lambda j, i: (j, 0, 0)),
        out_shape=jax.ShapeDtypeStruct((n_blocks, k, n), F32),
        compiler_params=_params(2),
    )(a, b)


def _wgrad_pair(a, b1, b2, k, n, name):
    s = a.shape[0]
    tw = min(WGRAD_TILE // 2, s)

    def body(a_ref, b1_ref, b2_ref, o1_ref, o2_ref):
        @pl.when(pl.program_id(1) == 0)
        def _():
            o1_ref[...] = jnp.zeros_like(o1_ref)
            o2_ref[...] = jnp.zeros_like(o2_ref)

        av = a_ref[...]
        o1_ref[0] += _dot_tn(av, b1_ref[0])
        o2_ref[0] += _dot_tn(av, b2_ref[0])

    hid = pl.BlockSpec((1, tw, n), lambda j, i: (j, i, 0))
    out = pl.BlockSpec((1, k, n), lambda j, i: (j, 0, 0))
    shape = jax.ShapeDtypeStruct((N_CHIPS, k, n), F32)
    return pl.pallas_call(
        body,
        name=name,
        grid=(N_CHIPS, s // tw),
        in_specs=[pl.BlockSpec((tw, k), lambda j, i: (i, 0)), hid, hid],
        out_specs=[out, out],
        out_shape=[shape, shape],
        compiler_params=_params(2),
    )(a, b1, b2)


def _wgrad_out(attn, conv, dxb, rows, name):
    s, d = dxb.shape
    tw = min(WGRAD_TILE // 2, s)

    def body(a_ref, c_ref, b_ref, o_ref):
        @pl.when(pl.program_id(0) == 0)
        def _():
            o_ref[...] = jnp.zeros_like(o_ref)

        bv = b_ref[...]
        for j in range(N_CHIPS):
            src = a_ref if j < 2 else c_ref
            o_ref[j] += _dot_tn(src[:, (j % 2) * rows : (j % 2 + 1) * rows], bv)

    return pl.pallas_call(
        body,
        name=name,
        grid=(s // tw,),
        in_specs=[
            pl.BlockSpec((tw, ATTN_DIM), lambda i: (i, 0)),
            pl.BlockSpec((tw, CONV_DIM), lambda i: (i, 0)),
            pl.BlockSpec((tw, d), lambda i: (i, 0)),
        ],
        out_specs=pl.BlockSpec((N_CHIPS, rows, d), lambda i: (0, 0, 0)),
        out_shape=jax.ShapeDtypeStruct((N_CHIPS, rows, d), F32),
        compiler_params=_params(1),
    )(attn, conv, dxb)


def _mesh_position():
    return lax.axis_index("x"), lax.axis_index("y"), lax.axis_index("c")


def _other_chips(x, y):
    return [(1 - x, y), (x, 1 - y), (1 - x, 1 - y)]


def _half_rows(ref_rows, c):
    half = ref_rows // 2
    return pl.ds(c * half, half)


class _WeightGather:
    def __init__(self, ins, outs, sems):
        self.ins, self.outs = ins, outs
        send_sems, recv_sems, pass_send_sems, pass_recv_sems, self.local_sems = sems
        self.ici, self.d2d = (send_sems, recv_sems), (pass_send_sems, pass_recv_sems)
        self.x, self.y, self.c = _mesh_position()
        self.me = 2 * self.x + self.y
        self.sibling = (self.x, self.y, 1 - self.c)
        self.chips = _other_chips(self.x, self.y)

    def _copy(self, t, k, chip_index, core, to, sems, src=None):
        dst = self.outs[t].at[chip_index, :, _half_rows(self.ins[t].shape[1], core), :]
        return pltpu.make_async_remote_copy(
            src_ref=dst if src is None else src, dst_ref=dst, send_sem=sems[0].at[t, k], recv_sem=sems[1].at[t, k],
            device_id=to, device_id_type=MESH_ID,
        )

    def _own(self, t):
        return pltpu.make_async_copy(self.ins[t], self.outs[t].at[self.me], self.local_sems.at[t])

    def _sends(self):
        for t in range(len(self.ins)):
            mine = self.ins[t].at[:, _half_rows(self.ins[t].shape[1], self.c), :]
            for k, (px, py) in enumerate(self.chips):
                yield self._copy(t, k, self.me, self.c, (px, py, self.c), self.ici, src=mine)

    def _passes(self, core, sems):
        for t in range(len(self.ins)):
            for k, (px, py) in enumerate(self.chips):
                yield self._copy(t, k, 2 * px + py, core, self.sibling, sems)

    def begin(self):
        for t in range(len(self.ins)):
            self._own(t).start()
        for cp in self._sends():
            cp.start()

    def relay(self):
        for arrived, onward in zip(self._passes(self.c, self.ici), self._passes(self.c, self.d2d)):
            arrived.wait_recv()
            onward.start()

    def finish(self):
        for cp in self._passes(1 - self.c, self.d2d):
            cp.wait_recv()
        for cp in list(self._sends()) + list(self._passes(self.c, self.d2d)):
            cp.wait_send()
        for t in range(len(self.ins)):
            self._own(t).wait()


def _gather_scratch(n):
    sems = pltpu.SemaphoreType.DMA((n, N_CHIPS - 1))
    return [sems, sems, sems, sems, pltpu.SemaphoreType.DMA((n,))]


def _gather_weights(shards):
    n = len(shards)

    def body(*refs):
        gather = _WeightGather(refs[:n], refs[n : 2 * n], refs[2 * n :])
        gather.begin()
        gather.relay()
        gather.finish()

    return pl.pallas_call(
        body,
        name="gather_weights",
        in_specs=[ANY] * n,
        out_specs=[ANY] * n,
        out_shape=[jax.ShapeDtypeStruct((N_CHIPS,) + w.shape, w.dtype) for w in shards],
        scratch_shapes=_gather_scratch(n),
    )(*shards)


def _swap_halves(grads, tag):
    n = len(grads)

    def body(*refs):
        ins, outs = refs[:n], refs[n : 2 * n]
        send_sems, recv_sems = refs[2 * n :]
        x, y, c = _mesh_position()
        copies = []
        for t in range(n):
            copies.append(pltpu.make_async_remote_copy(
                src_ref=ins[t].at[:, _half_rows(ins[t].shape[1], 1 - c), :], dst_ref=outs[t],
                send_sem=send_sems.at[t], recv_sem=recv_sems.at[t], device_id=(x, y, 1 - c), device_id_type=MESH_ID,
            ))
            copies[-1].start()
        for cp in copies:
            cp.wait()

    sems = pltpu.SemaphoreType.DMA((n,))
    return pl.pallas_call(
        body,
        name=f"swap_halves_{tag}",
        in_specs=[ANY] * n,
        out_specs=[ANY] * n,
        out_shape=[jax.ShapeDtypeStruct((g.shape[0], g.shape[1] // 2, g.shape[2]), g.dtype) for g in grads],
        scratch_shapes=[sems, sems],
    )(*grads)


class _ChipScatter:
    def __init__(self, ins, outs, sems):
        self.ins, self.outs = ins, outs
        self.send_sems, self.recv_sems = sems
        self.x, self.y, self.c = _mesh_position()

    def _copies(self):
        for t in range(len(self.ins)):
            for k, (px, py) in enumerate(_other_chips(self.x, self.y)):
                yield pltpu.make_async_remote_copy(
                    src_ref=self.ins[t].at[2 * px + py], dst_ref=self.outs[t].at[k],
                    send_sem=self.send_sems.at[t, k], recv_sem=self.recv_sems.at[t, k],
                    device_id=(px, py, self.c), device_id_type=MESH_ID,
                )

    def begin(self):
        for cp in self._copies():
            cp.start()

    def finish(self):
        for cp in self._copies():
            cp.wait()


def _scatter_scratch(n):
    sems = pltpu.SemaphoreType.DMA((n, N_CHIPS - 1))
    return [sems, sems]


def _scatter_shapes(parts):
    return [jax.ShapeDtypeStruct((N_CHIPS - 1,) + p.shape[1:], p.dtype) for p in parts]


def _scatter_to_chips(parts, tag):
    n = len(parts)

    def body(*refs):
        copies = _ChipScatter(refs[:n], refs[n : 2 * n], refs[2 * n :])
        copies.begin()
        copies.finish()

    return pl.pallas_call(
        body,
        name=f"scatter_to_chips_{tag}",
        in_specs=[ANY] * n,
        out_specs=[ANY] * n,
        out_shape=_scatter_shapes(parts),
        scratch_shapes=_scatter_scratch(n),
    )(*parts)


def _join_halves(shards):
    n = len(shards)

    def body(*refs):
        outs = refs[n : 2 * n]
        send_sems, recv_sems = refs[2 * n :]
        x, y, c = _mesh_position()
        copies = []
        for t in range(n):
            mine = outs[t].at[:, _half_rows(outs[t].shape[1], c), :]
            copies.append(pltpu.make_async_remote_copy(
                src_ref=mine, dst_ref=mine, send_sem=send_sems.at[t], recv_sem=recv_sems.at[t],
                device_id=(x, y, 1 - c), device_id_type=MESH_ID,
            ))
            copies[-1].start()
        for cp in copies:
            cp.wait()

    sems = pltpu.SemaphoreType.DMA((n,))
    return pl.pallas_call(
        body,
        name="join_halves",
        in_specs=[ANY] * n,
        out_specs=[ANY] * n,
        out_shape=[jax.ShapeDtypeStruct(g.shape, g.dtype) for g in shards],
        input_output_aliases={t: t for t in range(n)},
        scratch_shapes=[sems, sems],
    )(*shards)


def _gather_small(pack):
    def body(p_ref, o_ref, send_sems, recv_sems, local_sem):
        x, y, c = _mesh_position()
        own = pltpu.make_async_copy(p_ref, o_ref.at[4 * x + 2 * y + c], local_sem)
        own.start()
        copies = []
        for k in range(1, N_DEV):
            px, py, pc = x ^ (k >> 2), y ^ ((k >> 1) & 1), c ^ (k & 1)
            send = pltpu.make_async_remote_copy(
                src_ref=p_ref, dst_ref=o_ref.at[4 * x + 2 * y + c], send_sem=send_sems.at[k - 1], recv_sem=recv_sems.at[k - 1],
                device_id=(px, py, pc), device_id_type=MESH_ID,
            )
            send.start()
            copies.append((send, 4 * px + 2 * py + pc))
        for send, peer_slot in copies:
            send.wait_send()
        for k in range(1, N_DEV):
            px, py, pc = x ^ (k >> 2), y ^ ((k >> 1) & 1), c ^ (k & 1)
            pltpu.make_async_remote_copy(
                src_ref=p_ref, dst_ref=o_ref.at[4 * px + 2 * py + pc], send_sem=send_sems.at[k - 1], recv_sem=recv_sems.at[k - 1],
                device_id=(px, py, pc), device_id_type=MESH_ID,
            ).wait_recv()
        own.wait()

    sems = pltpu.SemaphoreType.DMA((N_DEV - 1,))
    return pl.pallas_call(
        body,
        name="gather_small",
        in_specs=[VMEM_SPEC],
        out_specs=VMEM_SPEC,
        out_shape=jax.ShapeDtypeStruct((N_DEV,) + pack.shape, pack.dtype),
        scratch_shapes=[sems, sems, pltpu.SemaphoreType.DMA],
    )(pack)


def _row_tile(rows):
    for tile in range(min(rows, 512) // 8 * 8, 0, -8):
        if rows % tile == 0:
            return tile
    return rows


def _add_half(grad, received, half_index, name):
    slots, h, cdim = received.shape
    tile = _row_tile(h)
    per_half = h // tile

    def body(c_ref, g_ref, r_ref, o_ref, ob_ref):
        total = g_ref[...] + r_ref[...]
        o_ref[...] = total
        ob_ref[...] = total.astype(BF16)

    block = pl.BlockSpec((1, tile, cdim), lambda j, i, c: (j, i, 0))
    grid_spec = pltpu.PrefetchScalarGridSpec(
        num_scalar_prefetch=1,
        grid=(slots, per_half),
        in_specs=[pl.BlockSpec((1, tile, cdim), lambda j, i, c: (j, c[0] * per_half + i, 0)), block],
        out_specs=[block, block],
    )
    return pl.pallas_call(
        body, name=name, grid_spec=grid_spec,
        out_shape=[jax.ShapeDtypeStruct(received.shape, F32), jax.ShapeDtypeStruct(received.shape, BF16)],
        compiler_params=_params(2),
    )(half_index, grad, received)


def _add_chips(part, received, chip_index, core_index, layer, n_layers, shard, name):
    _, h, cdim = part.shape
    tile = _row_tile(h)
    per_half = h // tile

    def body(chip_ref, core_ref, p_ref, r_ref, *rest):
        o_ref = rest[-1]
        o_ref[0] = ((p_ref[0] + r_ref[0].astype(F32)) + r_ref[1].astype(F32)) + r_ref[2].astype(F32)

    in_specs = [
        pl.BlockSpec((1, tile, cdim), lambda i, chip, core: (chip[0], i, 0)),
        pl.BlockSpec((N_CHIPS - 1, tile, cdim), lambda i, chip, core: (0, i, 0)),
    ]
    operands = [chip_index, core_index, part, received]
    aliases = {}
    if shard is not None:
        in_specs.append(ANY)
        operands.append(shard)
        aliases = {4: 0}
    grid_spec = pltpu.PrefetchScalarGridSpec(
        num_scalar_prefetch=2,
        grid=(per_half,),
        in_specs=in_specs,
        out_specs=pl.BlockSpec((1, tile, cdim), lambda i, chip, core: (layer, core[0] * per_half + i, 0)),
    )
    return pl.pallas_call(
        body, name=name, grid_spec=grid_spec, out_shape=jax.ShapeDtypeStruct((n_layers, 2 * h, cdim), F32),
        input_output_aliases=aliases, compiler_params=_params(1),
    )(*operands)


def _adamw(w, g, m, v, name):
    rows, cdim = w.shape
    tile = _row_tile(rows)

    def body(w_ref, g_ref, m_ref, v_ref, d_ref, nm_ref, nv_ref, go_ref):
        gv = g_ref[...]
        go_ref[...] = gv
        nm = ADAM_B1 * m_ref[...] + (1.0 - ADAM_B1) * gv
        nv = ADAM_B2 * v_ref[...] + (1.0 - ADAM_B2) * (gv * gv)
        m_hat = nm / (1.0 - ADAM_B1 ** ADAM_STEP)
        v_hat = nv / (1.0 - ADAM_B2 ** ADAM_STEP)
        d_ref[...] = -ADAM_LR * (m_hat / (jnp.sqrt(v_hat) + ADAM_EPS) + ADAM_WD * w_ref[...])
        nm_ref[...] = nm
        nv_ref[...] = nv

    spec = pl.BlockSpec((tile, cdim), lambda i: (i, 0))
    shape = jax.ShapeDtypeStruct((rows, cdim), F32)
    return pl.pallas_call(
        body, name=name, grid=(rows // tile,), in_specs=[spec] * 4, out_specs=[spec] * 4, out_shape=[shape] * 4,
        compiler_params=_params(1),
    )(w, g, m, v)


SMALL_ROWS, SMALL_COLS = 24, 1024
ROW_NORM_MIX, ROW_NORM_FFN, ROW_LOSS, ROW_Q_NORM, ROW_K_NORM, ROW_CONV = 0, 2, 4, 8, 10, 16


def _sum_small(gathered):
    def body(g_ref, o_ref, heads_ref, lanes_ref):
        total = g_ref[0]
        for dev in range(1, N_DEV):
            total = total + g_ref[dev]
        o_ref[...] = total
        heads = o_ref[8:16, 0:LANES]
        for grp in range(1, ATTN_DIM // LANES):
            heads = heads + o_ref[8:16, grp * LANES : (grp + 1) * LANES]
        heads_ref[...] = heads + pltpu.roll(heads, HEAD_DIM, 1)
        lanes_ref[...] = jnp.broadcast_to(jnp.sum(o_ref[0:8, :], axis=-1, keepdims=True), (8, LANES))

    return pl.pallas_call(
        body,
        name="sum_small",
        in_specs=[VMEM_SPEC],
        out_specs=[VMEM_SPEC] * 3,
        out_shape=[jax.ShapeDtypeStruct((SMALL_ROWS, SMALL_COLS), F32), jax.ShapeDtypeStruct((8, LANES), F32), jax.ShapeDtypeStruct((8, LANES), F32)],
    )(gathered)


def _pad_rows(a, rows):
    return jnp.pad(a, ((0, rows - a.shape[0]), (0, 0)))


def _pad_to(a, rows, cols):
    return jnp.pad(a, ((0, rows - a.shape[0]), (0, cols - a.shape[1])))


def _conv_taps(conv_s):
    return jnp.transpose(conv_s[:, 0, 0:8], (1, 0, 2)).reshape(8, -1)


class _GradExchange:
    def __init__(self, chip_index, core_index, n_layers):
        self.chip_index, self.core_index, self.n_layers = chip_index, core_index, n_layers
        self.shards = {}
        self.pending = None

    def offer(self, layer, grads):
        assert self.pending is None
        names = list(grads)
        received = _swap_halves([grads[k] for k in names], f"{'_'.join(names)}_{layer}")
        parts = [_add_half(grads[k], r, self.core_index, f"add_half_{k}_{layer}") for k, r in zip(names, received)]
        self.pending = (layer, names, [p32 for p32, _ in parts], [p16 for _, p16 in parts])

    def payload(self):
        return () if self.pending is None else tuple(self.pending[3])

    def take(self, received):
        layer, names, parts, _ = self.pending
        self.pending = None
        for k, p, r in zip(names, parts, received):
            self.shards[k] = _add_chips(
                p, r, self.chip_index, self.core_index, layer, self.n_layers, self.shards.get(k), f"add_chips_{k}_{layer}")

    def finish(self):
        if self.pending is not None:
            layer, names = self.pending[0], self.pending[1]
            self.take(_scatter_to_chips(list(self.pending[3]), f"{'_'.join(names)}_{layer}"))
        return dict(zip(BIG, _join_halves([self.shards[k] for k in BIG])))


def _local_step(x, target, norm_mix, q_norm, k_norm, norm_ffn, layer_weights, exchange=None):
    layer_weights = list(layer_weights)

    def carrying(kernel_fn, n_out, *args):
        if exchange is None or exchange.pending is None:
            return kernel_fn(*args)
        out = kernel_fn(*args, scatter=exchange.payload())
        exchange.take(out[n_out:])
        return out[:n_out]

    n_layers = norm_mix.shape[0]
    s, d = x.shape
    tw = min(WGRAD_TILE, s)
    n_in = layer_weights[0][0].shape[-1]
    f = layer_weights[0][2].shape[-1]
    saved = []
    for l in range(n_layers):
        weights = list(layer_weights[l])
        q_gain = jnp.tile(q_norm[l][None, :], (1, 2))
        k_gain = jnp.tile(k_norm[l][None, :], (1, 2))
        h1, proj, qn, kn, vb = _in_proj(x, norm_mix[l][None, :], weights[0], 0, q_gain, k_gain, f"in_proj_{l}")
        missing = [n for n, w in enumerate(weights) if w.ndim == 3]
        if missing:
            attn, *arrived = _attn_fwd(qn, kn, vb, f"attn_fwd_{l}", gather=tuple(weights[n] for n in missing))
            for n, w in zip(missing, arrived):
                weights[n] = w
            layer_weights[l] = tuple(weights)
        else:
            attn = _attn_fwd(qn, kn, vb, f"attn_fwd_{l}")
        _, wout_s, wg_s, wu_s, wd_s, conv_s = weights
        taps = _conv_taps(conv_s)
        x_mid, conv = _out_proj(x, attn, proj, taps, wout_s, 0, f"out_proj_{l}")
        pending = ()
        if l + 1 < n_layers and all(w.ndim == 3 for w in layer_weights[l + 1]):
            pending = tuple(layer_weights[l + 1])
        x_out, gate, up, *more = _ffn_fwd(
            x_mid, norm_ffn[l][None, :], wg_s, wu_s, wd_s, 0, f"ffn_fwd_{l}", gather=pending, target=target if l == n_layers - 1 else None)
        if pending:
            layer_weights[l + 1] = tuple(more[: len(pending)])
        if l == n_layers - 1:
            dy, loss_lanes = more[len(pending) :]
        saved.append(dict(x=x, h1=h1, proj=proj, qn=qn, kn=kn, vb=vb, attn=attn, conv=conv, x_mid=x_mid, q_gain=q_gain, k_gain=k_gain,
                          gate=gate, up=up, taps=taps))
        x = x_out

    grads = [None] * n_layers
    for l in reversed(range(n_layers)):
        sv = saved[l]
        win_s, wout_s, wg_s, wu_s, wd_s, _ = layer_weights[l]
        dx_mid, d_norm_ffn, h2, dyb, dgate, dup, act = carrying(
            _ffn_bwd, 7, sv["x_mid"], dy, norm_ffn[l][None, :], sv["gate"], sv["up"], wg_s, wu_s, wd_s, 0, f"ffn_bwd_{l}")
        tok2 = pl.BlockSpec((tw, d), lambda j, i: (i, 0))
        hid = pl.BlockSpec((1, tw, f), lambda j, i: (j, i, 0))
        d_wg, d_wu = _wgrad_pair(h2, dgate, dup, d, f, f"wgrad_gate_up_{l}")
        d_wd = _wgrad(act, dyb, hid, tok2, N_CHIPS, f, d, f"wgrad_down_{l}")
        d_attn, d_conv, dxb = _out_proj_bwd(dx_mid, wout_s, 0, f"out_proj_bwd_{l}")
        d_wout = _wgrad_out(sv["attn"], sv["conv"], dxb, wout_s.shape[2], f"wgrad_out_{l}")
        if exchange is not None:
            exchange.offer(l, dict(w_gate=d_wg, w_up=d_wu, w_down=d_wd, w_out=d_wout))
        dq, dk, dv = carrying(_attn_bwd, 3, sv["qn"], sv["kn"], sv["vb"], d_attn, f"attn_bwd_{l}")
        dproj_conv, d_conv_w = _conv_bwd(sv["proj"], sv["taps"], d_conv, f"conv_bwd_{l}")
        dy, d_norm_mix, dproj, d_qg, d_kg = _in_proj_bwd(
            sv["x"], dx_mid, norm_mix[l][None, :], win_s, 0, sv["proj"], sv["q_gain"], sv["k_gain"], dq, dk, dv, dproj_conv, f"in_proj_bwd_{l}")
        d_win = _wgrad(sv["h1"], dproj, tok2, pl.BlockSpec((tw, n_in), lambda j, i: (i, j)), N_CHIPS, d, n_in, f"wgrad_in_{l}")
        if exchange is not None:
            exchange.offer(l, dict(w_in=d_win))
        grads[l] = dict(norm_mix=d_norm_mix, norm_ffn=d_norm_ffn, q_norm=d_qg, k_norm=d_kg, conv_w=d_conv_w,
                        w_in=d_win, w_out=d_wout, w_gate=d_wg, w_up=d_wu, w_down=d_wd)
    return loss_lanes, dy, grads


BIG = ("w_in", "w_out", "w_gate", "w_up", "w_down")


def kernel(x, norm_mix, w_in, q_norm, k_norm, conv_w, w_out, norm_ffn, w_gate, w_up, w_down, loss_target, m_norm_mix, m_w_in, m_q_norm, m_k_norm, m_conv_w, m_w_out, m_norm_ffn, m_w_gate, m_w_up, m_w_down, v_norm_mix, v_w_in, v_q_norm, v_k_norm, v_conv_w, v_w_out, v_norm_ffn, v_w_gate, v_w_up, v_w_down):
    n_layers = norm_mix.shape[0]
    weights = dict(w_in=w_in, w_out=w_out, w_gate=w_gate, w_up=w_up, w_down=w_down)
    moments_m = dict(w_in=m_w_in, w_out=m_w_out, w_gate=m_w_gate, w_up=m_w_up, w_down=m_w_down)
    moments_v = dict(w_in=v_w_in, w_out=v_w_out, w_gate=v_w_gate, w_up=v_w_up, w_down=v_w_down)
    cx, cy, cc = _mesh_position()
    chip_index = (2 * cx + cy).astype(jnp.int32).reshape(1)
    core_index = cc.astype(jnp.int32).reshape(1)

    conv_pad = jnp.pad(conv_w, ((0, 0), (0, 16 - conv_w.shape[1]), (0, 0)))

    def shards_of(layer):
        return [weights[k][layer : layer + 1].astype(BF16) for k in BIG] + [conv_pad[layer : layer + 1]]

    first = shards_of(0)
    layer_weights = [tuple(_gather_weights(first[:1])) + tuple(first[1:])] + [tuple(shards_of(layer)) for layer in range(1, n_layers)]

    exchange = _GradExchange(chip_index, core_index, n_layers)
    loss_lanes, grad_x, grads = _local_step(
        x[0], loss_target[0], norm_mix, q_norm, k_norm, norm_ffn, layer_weights, exchange)

    big_grads = exchange.finish()

    def lanes(a):
        return _pad_to(a, a.shape[0], SMALL_COLS)

    def tile_of(*groups):
        return _pad_rows(jnp.concatenate([lanes(jnp.concatenate(g, axis=0)) for g in groups], axis=0), 8)

    layers = range(n_layers)
    pack = jnp.concatenate([
        tile_of([grads[l]["norm_mix"] for l in layers], [grads[l]["norm_ffn"] for l in layers], [loss_lanes]),
        tile_of([grads[l]["q_norm"] for l in layers], [grads[l]["k_norm"] for l in layers]),
        tile_of([grads[l]["conv_w"][0:3] for l in layers]),
    ], axis=0)
    small, small_heads, small_lanes = _sum_small(_gather_small(pack))
    loss = small_lanes[ROW_LOSS, 0]
    d_model = norm_mix.shape[1]
    conv_cols = conv_w.shape[2]
    conv_all = small[ROW_CONV : ROW_CONV + 3 * n_layers, 0:CONV_DIM].reshape(n_layers, 3, CONV_DIM)
    small_grads = dict(
        norm_mix=small[ROW_NORM_MIX : ROW_NORM_MIX + n_layers, 0:d_model],
        norm_ffn=small[ROW_NORM_FFN : ROW_NORM_FFN + n_layers, 0:d_model],
        q_norm=small_heads[ROW_Q_NORM - 8 : ROW_Q_NORM - 8 + n_layers, 0:HEAD_DIM],
        k_norm=small_heads[ROW_K_NORM - 8 : ROW_K_NORM - 8 + n_layers, 0:HEAD_DIM],
        conv_w=lax.dynamic_slice_in_dim(conv_all, (2 * cx + cy) * conv_cols, conv_cols, axis=2),
    )

    out_grad, out_delta, out_m, out_v = {}, {}, {}, {}
    for k in BIG:
        shape = weights[k].shape
        view = (shape[0] * shape[1], shape[2])
        g = big_grads[k]
        delta, new_m, new_v, g = _adamw(weights[k].reshape(view), g.reshape(view), moments_m[k].reshape(view), moments_v[k].reshape(view), f"adamw_{k}")
        out_grad[k], out_delta[k], out_m[k], out_v[k] = g.reshape(shape), delta.reshape(shape), new_m.reshape(shape), new_v.reshape(shape)

    small_w = dict(norm_mix=norm_mix, norm_ffn=norm_ffn, q_norm=q_norm, k_norm=k_norm, conv_w=conv_w)
    small_m = dict(norm_mix=m_norm_mix, norm_ffn=m_norm_ffn, q_norm=m_q_norm, k_norm=m_k_norm, conv_w=m_conv_w)
    small_v = dict(norm_mix=v_norm_mix, norm_ffn=v_norm_ffn, q_norm=v_q_norm, k_norm=v_k_norm, conv_w=v_conv_w)
    order = ("norm_mix", "norm_ffn", "q_norm", "k_norm", "conv_w")

    def packed(tree):
        parts2 = [_pad_to(tree[k].reshape(-1, tree[k].shape[-1]), tree[k].reshape(-1, tree[k].shape[-1]).shape[0], SMALL_COLS) for k in order]
        return _pad_rows(jnp.concatenate(parts2, axis=0), SMALL_ROWS)

    delta_p, m_p, v_p, _ = _adamw(packed(small_w), packed(small_grads), packed(small_m), packed(small_v), "adamw_small")
    row = 0
    for k in order:
        shape = small_w[k].shape
        n_rows = 1
        for dim in shape[:-1]:
            n_rows *= dim
        cut = (slice(row, row + n_rows), slice(0, shape[-1]))
        out_grad[k] = small_grads[k]
        out_delta[k], out_m[k], out_v[k] = delta_p[cut].reshape(shape), m_p[cut].reshape(shape), v_p[cut].reshape(shape)
        row += n_rows

    names_out = ("norm_mix", "w_in", "q_norm", "k_norm", "conv_w", "w_out", "norm_ffn", "w_gate", "w_up", "w_down")
    return (loss, grad_x[None], *[out_grad[k] for k in names_out], *[out_delta[k] for k in names_out],
            *[out_m[k] for k in names_out], *[out_v[k] for k in names_out])
```
